```python
import math
import jax, jax.numpy as jnp
from jax import lax
import numpy as np

D_MODEL = 1024
BATCH = 16
SEQ = 2048
DEPTH = 2

N_EVEN = (DEPTH + 1) // 2
N_ODD = DEPTH // 2

MLA_HEADS = 8
Q_LORA = 256
KV_LORA = 128
QK_NOPE = 64
QK_ROPE = 32
V_HEAD = 64
MLA_WIDTH = MLA_HEADS * V_HEAD
ROPE_BASE = 10000.0
Q_BLOCK = 128

LRU_HEADS = 8
LRU_WIDTH = 512
LRU_BLOCK = LRU_WIDTH // LRU_HEADS
LRU_CONV = 4
LRU_C = 8.0

AB_IN = Q_LORA + KV_LORA + QK_ROPE + 2 * LRU_WIDTH
AB_MIX = MLA_WIDTH + LRU_WIDTH

CHUNK = 128
SGU_GROUPS = 8
SGU_WIDTH = D_MODEL
SGU_GROUP_DIM = SGU_WIDTH // SGU_GROUPS

D_FF = 2816
FFN_CONV = 3

NORM_EPS = 1e-6

kernel_name = "hybrid_mla_rglru_chunksgu_convffn"


def rms_norm(x, g):
    xf = x.astype(jnp.float32)
    y = xf * lax.rsqrt(jnp.mean(xf * xf, axis=-1, keepdims=True) + NORM_EPS)
    return (y * g.astype(jnp.float32)).astype(x.dtype)


def layer_norm(x, g, b):
    xf = x.astype(jnp.float32)
    mu = jnp.mean(xf, axis=-1, keepdims=True)
    xc = xf - mu
    y = xc * lax.rsqrt(jnp.mean(xc * xc, axis=-1, keepdims=True) + NORM_EPS)
    return (y * g.astype(jnp.float32) + b.astype(jnp.float32)).astype(x.dtype)


def causal_dwconv(x, w, b):
    K = w.shape[0]
    S = x.shape[1]
    xp = jnp.pad(x, ((0, 0), (K - 1, 0), (0, 0)))
    y = xp[:, 0:S] * w[0]
    for k in range(1, K):
        y = y + xp[:, k:k + S] * w[k]
    return y + b


def rope(x, positions):
    half = x.shape[-1] // 2
    inv_freq = jnp.exp(-math.log(ROPE_BASE) * jnp.arange(half, dtype=jnp.float32) / half)
    ang = positions.astype(jnp.float32)[..., None] * inv_freq
    cos = jnp.cos(ang)[:, :, None, :]
    sin = jnp.sin(ang)[:, :, None, :]
    xf = x.astype(jnp.float32)
    x1, x2 = xf[..., :half], xf[..., half:]
    return jnp.concatenate([x1 * cos - x2 * sin, x2 * cos + x1 * sin], axis=-1).astype(x.dtype)


def causal_block_attention(q, k, v):
    B, S, H, Dk = q.shape
    Dv = v.shape[-1]
    nb = S // Q_BLOCK
    scale = Dk ** -0.5
    qb = q.reshape(B, nb, Q_BLOCK, H, Dk).transpose(1, 0, 2, 3, 4)
    kpos = jnp.arange(S)

    def one_block(args):
        q_blk, blk = args
        s = jnp.einsum('bqhd,bkhd->bhqk', q_blk, k,
                       preferred_element_type=jnp.float32) * scale
        qpos = blk * Q_BLOCK + jnp.arange(Q_BLOCK)
        s = jnp.where(kpos[None, :] <= qpos[:, None], s, -jnp.inf)
        p = jax.nn.softmax(s, axis=-1)
        return jnp.einsum('bhqk,bkhd->bqhd', p.astype(v.dtype), v)

    o = lax.map(one_block, (qb, jnp.arange(nb)))
    return o.transpose(1, 0, 2, 3, 4).reshape(B, S, H * Dv)


def rg_lru(x, w_a, b_a, w_x, b_x, lam):
    B, S, C = x.shape
    xg = x.reshape(B, S, LRU_HEADS, LRU_BLOCK)
    r = jax.nn.sigmoid(jnp.einsum('bsgi,gij->bsgj', xg, w_a).reshape(B, S, C) + b_a).astype(jnp.float32)
    i = jax.nn.sigmoid(jnp.einsum('bsgi,gij->bsgj', xg, w_x).reshape(B, S, C) + b_x).astype(jnp.float32)
    log_a = -LRU_C * r * jax.nn.softplus(-lam.astype(jnp.float32))
    a = jnp.exp(log_a)
    bx = jnp.sqrt(-jnp.expm1(2.0 * log_a)) * (i * x.astype(jnp.float32))

    def combine(left, right):
        a_l, b_l = left
        a_r, b_r = right
        return a_l * a_r, a_r * b_l + b_r

    _, h = lax.associative_scan(combine, (a, bx), axis=1)
    return h.astype(x.dtype)


def mla_lru_mixer(h, positions, w_in, q_norm, w_q_b, kv_norm, w_kv_b, conv_w, conv_b,
                  w_rg_a, b_rg_a, w_rg_x, b_rg_x, lam, w_out):
    B, S, _ = h.shape
    z = h @ w_in
    o1 = Q_LORA
    o2 = o1 + KV_LORA
    o3 = o2 + QK_ROPE
    o4 = o3 + LRU_WIDTH
    c_q, c_kv, k_pe, x_lru, gate_lru = jnp.split(z, [o1, o2, o3, o4], axis=-1)

    q = (rms_norm(c_q, q_norm) @ w_q_b).reshape(B, S, MLA_HEADS, QK_NOPE + QK_ROPE)
    q = jnp.concatenate([q[..., :QK_NOPE], rope(q[..., QK_NOPE:], positions)], axis=-1)
    kv = (rms_norm(c_kv, kv_norm) @ w_kv_b).reshape(B, S, MLA_HEADS, QK_NOPE + V_HEAD)
    k_pe = jnp.broadcast_to(rope(k_pe[:, :, None, :], positions), (B, S, MLA_HEADS, QK_ROPE))
    k = jnp.concatenate([kv[..., :QK_NOPE], k_pe], axis=-1)
    v = kv[..., QK_NOPE:]
    y_mla = causal_block_attention(q, k, v)

    xc = causal_dwconv(x_lru, conv_w, conv_b)
    y_lru = rg_lru(xc, w_rg_a, b_rg_a, w_rg_x, b_rg_x, lam) * jax.nn.gelu(gate_lru)

    return jnp.concatenate([y_mla, y_lru], axis=-1) @ w_out


def chunk_sgu_mixer(h, w_in, ln_g, ln_b, w_s, b_s, w_out):
    B, S, _ = h.shape
    z = jax.nn.gelu(h @ w_in)
    u, v = jnp.split(z, 2, axis=-1)
    v = layer_norm(v, ln_g, ln_b).reshape(B, S // CHUNK, CHUNK, SGU_GROUPS, SGU_GROUP_DIM)
    causal = jnp.tril(jnp.ones((CHUNK, CHUNK), dtype=w_s.dtype))
    s = jnp.einsum('gts,bnsgc->bntgc', w_s * causal, v) + b_s.T[:, :, None]
    return (u * s.reshape(B, S, SGU_WIDTH)) @ w_out


def conv_ffn(h, w_gate, w_up, conv_w, conv_b, w_down):
    g = causal_dwconv(h @ w_gate, conv_w, conv_b)
    return (jax.nn.gelu(g) * (h @ w_up)) @ w_down


def _fwd_setup_inputs(seed: int = 0) -> dict:
    key = jax.random.key(seed)
    ks = jax.random.split(key, 32)
    f32 = jnp.float32

    def nrm(k, shape, fan_in):
        return jax.random.normal(k, shape, f32) * (fan_in ** -0.5)

    def gain(k, shape, s=0.02):
        return 1.0 + s * jax.random.normal(k, shape, f32)

    def bias(k, shape):
        return 0.02 * jax.random.normal(k, shape, f32)

    x = jax.random.normal(ks[0], (BATCH, SEQ, D_MODEL), f32)
    positions = jnp.broadcast_to(jnp.arange(SEQ, dtype=jnp.int32), (BATCH, SEQ))

    a0 = jax.random.uniform(ks[13], (N_EVEN, LRU_WIDTH), f32, minval=0.9, maxval=0.999)
    lam = jnp.log(a0) - jnp.log1p(-a0)

    return {
        "x": x,
        "positions": positions,
        "ab_norm": gain(ks[1], (N_EVEN, D_MODEL)),
        "ab_w_in": nrm(ks[2], (N_EVEN, D_MODEL, AB_IN), D_MODEL),
        "ab_q_norm": gain(ks[3], (N_EVEN, Q_LORA)),
        "ab_w_q_b": nrm(ks[4], (N_EVEN, Q_LORA, MLA_HEADS * (QK_NOPE + QK_ROPE)), Q_LORA),
        "ab_kv_norm": gain(ks[5], (N_EVEN, KV_LORA)),
        "ab_w_kv_b": nrm(ks[6], (N_EVEN, KV_LORA, MLA_HEADS * (QK_NOPE + V_HEAD)), KV_LORA),
        "ab_conv_w": nrm(ks[7], (N_EVEN, LRU_CONV, LRU_WIDTH), LRU_CONV),
        "ab_conv_b": bias(ks[8], (N_EVEN, LRU_WIDTH)),
        "ab_w_rg_a": nrm(ks[9], (N_EVEN, LRU_HEADS, LRU_BLOCK, LRU_BLOCK), LRU_BLOCK),
        "ab_b_rg_a": bias(ks[10], (N_EVEN, LRU_WIDTH)),
        "ab_w_rg_x": nrm(ks[11], (N_EVEN, LRU_HEADS, LRU_BLOCK, LRU_BLOCK), LRU_BLOCK),
        "ab_b_rg_x": bias(ks[12], (N_EVEN, LRU_WIDTH)),
        "ab_lambda": lam,
        "ab_w_out": nrm(ks[14], (N_EVEN, AB_MIX, D_MODEL), AB_MIX),
        "c_norm": gain(ks[15], (N_ODD, D_MODEL)),
        "c_w_in": nrm(ks[16], (N_ODD, D_MODEL, 2 * SGU_WIDTH), D_MODEL),
        "c_ln_g": gain(ks[17], (N_ODD, SGU_WIDTH)),
        "c_ln_b": bias(ks[18], (N_ODD, SGU_WIDTH)),
        "c_w_s": nrm(ks[19], (N_ODD, SGU_GROUPS, CHUNK, CHUNK), CHUNK),
        "c_b_s": gain(ks[20], (N_ODD, SGU_GROUPS, CHUNK), 0.1),
        "c_w_out": nrm(ks[21], (N_ODD, SGU_WIDTH, D_MODEL), SGU_WIDTH),
        "ffn_norm": gain(ks[22], (DEPTH, D_MODEL)),
        "ffn_w_gate": nrm(ks[23], (DEPTH, D_MODEL, D_FF), D_MODEL),
        "ffn_w_up": nrm(ks[24], (DEPTH, D_MODEL, D_FF), D_MODEL),
        "ffn_conv_w": nrm(ks[25], (DEPTH, FFN_CONV, D_FF), FFN_CONV),
        "ffn_conv_b": bias(ks[26], (DEPTH, D_FF)),
        "ffn_w_down": nrm(ks[27], (DEPTH, D_FF, D_MODEL), D_FF),
        "final_norm": gain(ks[28], (D_MODEL,)),
    }


def _fwd_reference(x, positions, ab_norm, ab_w_in, ab_q_norm, ab_w_q_b, ab_kv_norm, ab_w_kv_b,
              ab_conv_w, ab_conv_b, ab_w_rg_a, ab_b_rg_a, ab_w_rg_x, ab_b_rg_x, ab_lambda,
              ab_w_out, c_norm, c_w_in, c_ln_g, c_ln_b, c_w_s, c_b_s, c_w_out,
              ffn_norm, ffn_w_gate, ffn_w_up, ffn_conv_w, ffn_conv_b, ffn_w_down, final_norm):
    h = x
    for layer in range(DEPTH):
        if layer % 2 == 0:
            i = layer // 2
            h = h + mla_lru_mixer(rms_norm(h, ab_norm[i]), positions, ab_w_in[i],
                                  ab_q_norm[i], ab_w_q_b[i], ab_kv_norm[i], ab_w_kv_b[i],
                                  ab_conv_w[i], ab_conv_b[i], ab_w_rg_a[i], ab_b_rg_a[i],
                                  ab_w_rg_x[i], ab_b_rg_x[i], ab_lambda[i], ab_w_out[i])
        else:
            i = layer // 2
            h = h + chunk_sgu_mixer(rms_norm(h, c_norm[i]), c_w_in[i], c_ln_g[i], c_ln_b[i],
                                    c_w_s[i], c_b_s[i], c_w_out[i])
        h = h + conv_ffn(rms_norm(h, ffn_norm[layer]), ffn_w_gate[layer], ffn_w_up[layer],
                         ffn_conv_w[layer], ffn_conv_b[layer], ffn_w_down[layer])
    return rms_norm(h, final_norm)


import jax as _jax
import jax.numpy as _jnp

TWIN_FORMAT = 'train_step'
FWD_PARAMS = ['x', 'positions', 'ab_norm', 'ab_w_in', 'ab_q_norm', 'ab_w_q_b', 'ab_kv_norm', 'ab_w_kv_b', 'ab_conv_w', 'ab_conv_b', 'ab_w_rg_a', 'ab_b_rg_a', 'ab_w_rg_x', 'ab_b_rg_x', 'ab_lambda', 'ab_w_out', 'c_norm', 'c_w_in', 'c_ln_g', 'c_ln_b', 'c_w_s', 'c_b_s', 'c_w_out', 'ffn_norm', 'ffn_w_gate', 'ffn_w_up', 'ffn_conv_w', 'ffn_conv_b', 'ffn_w_down', 'final_norm']
TWIN_WEIGHTS = ['ab_norm', 'ab_w_in', 'ab_q_norm', 'ab_w_q_b', 'ab_kv_norm', 'ab_w_kv_b', 'ab_conv_w', 'ab_conv_b', 'ab_w_rg_a', 'ab_b_rg_a', 'ab_w_rg_x', 'ab_b_rg_x', 'ab_lambda', 'ab_w_out', 'c_norm', 'c_w_in', 'c_ln_g', 'c_ln_b', 'c_w_s', 'c_b_s', 'c_w_out', 'ffn_norm', 'ffn_w_gate', 'ffn_w_up', 'ffn_conv_w', 'ffn_conv_b', 'ffn_w_down', 'final_norm']
TWIN_DIFF_INPUT = 'x'
TWIN_INPUTS = ['x', 'positions', 'ab_norm', 'ab_w_in', 'ab_q_norm', 'ab_w_q_b', 'ab_kv_norm', 'ab_w_kv_b', 'ab_conv_w', 'ab_conv_b', 'ab_w_rg_a', 'ab_b_rg_a', 'ab_w_rg_x', 'ab_b_rg_x', 'ab_lambda', 'ab_w_out', 'c_norm', 'c_w_in', 'c_ln_g', 'c_ln_b', 'c_w_s', 'c_b_s', 'c_w_out', 'ffn_norm', 'ffn_w_gate', 'ffn_w_up', 'ffn_conv_w', 'ffn_conv_b', 'ffn_w_down', 'final_norm', 'loss_target', 'm_ab_norm', 'm_ab_w_in', 'm_ab_q_norm', 'm_ab_w_q_b', 'm_ab_kv_norm', 'm_ab_w_kv_b', 'm_ab_conv_w', 'm_ab_conv_b', 'm_ab_w_rg_a', 'm_ab_b_rg_a', 'm_ab_w_rg_x', 'm_ab_b_rg_x', 'm_ab_lambda', 'm_ab_w_out', 'm_c_norm', 'm_c_w_in', 'm_c_ln_g', 'm_c_ln_b', 'm_c_w_s', 'm_c_b_s', 'm_c_w_out', 'm_ffn_norm', 'm_ffn_w_gate', 'm_ffn_w_up', 'm_ffn_conv_w', 'm_ffn_conv_b', 'm_ffn_w_down', 'm_final_norm', 'v_ab_norm', 'v_ab_w_in', 'v_ab_q_norm', 'v_ab_w_q_b', 'v_ab_kv_norm', 'v_ab_w_kv_b', 'v_ab_conv_w', 'v_ab_conv_b', 'v_ab_w_rg_a', 'v_ab_b_rg_a', 'v_ab_w_rg_x', 'v_ab_b_rg_x', 'v_ab_lambda', 'v_ab_w_out', 'v_c_norm', 'v_c_w_in', 'v_c_ln_g', 'v_c_ln_b', 'v_c_w_s', 'v_c_b_s', 'v_c_w_out', 'v_ffn_norm', 'v_ffn_w_gate', 'v_ffn_w_up', 'v_ffn_conv_w', 'v_ffn_conv_b', 'v_ffn_w_down', 'v_final_norm']
TWIN_OUTPUTS = ['loss', 'grad_x', 'grad_ab_norm', 'grad_ab_w_in', 'grad_ab_q_norm', 'grad_ab_w_q_b', 'grad_ab_kv_norm', 'grad_ab_w_kv_b', 'grad_ab_conv_w', 'grad_ab_conv_b', 'grad_ab_w_rg_a', 'grad_ab_b_rg_a', 'grad_ab_w_rg_x', 'grad_ab_b_rg_x', 'grad_ab_lambda', 'grad_ab_w_out', 'grad_c_norm', 'grad_c_w_in', 'grad_c_ln_g', 'grad_c_ln_b', 'grad_c_w_s', 'grad_c_b_s', 'grad_c_w_out', 'grad_ffn_norm', 'grad_ffn_w_gate', 'grad_ffn_w_up', 'grad_ffn_conv_w', 'grad_ffn_conv_b', 'grad_ffn_w_down', 'grad_final_norm', 'delta_ab_norm', 'delta_ab_w_in', 'delta_ab_q_norm', 'delta_ab_w_q_b', 'delta_ab_kv_norm', 'delta_ab_w_kv_b', 'delta_ab_conv_w', 'delta_ab_conv_b', 'delta_ab_w_rg_a', 'delta_ab_b_rg_a', 'delta_ab_w_rg_x', 'delta_ab_b_rg_x', 'delta_ab_lambda', 'delta_ab_w_out', 'delta_c_norm', 'delta_c_w_in', 'delta_c_ln_g', 'delta_c_ln_b', 'delta_c_w_s', 'delta_c_b_s', 'delta_c_w_out', 'delta_ffn_norm', 'delta_ffn_w_gate', 'delta_ffn_w_up', 'delta_ffn_conv_w', 'delta_ffn_conv_b', 'delta_ffn_w_down', 'delta_final_norm', 'new_m_ab_norm', 'new_m_ab_w_in', 'new_m_ab_q_norm', 'new_m_ab_w_q_b', 'new_m_ab_kv_norm', 'new_m_ab_w_kv_b', 'new_m_ab_conv_w', 'new_m_ab_conv_b', 'new_m_ab_w_rg_a', 'new_m_ab_b_rg_a', 'new_m_ab_w_rg_x', 'new_m_ab_b_rg_x', 'new_m_ab_lambda', 'new_m_ab_w_out', 'new_m_c_norm', 'new_m_c_w_in', 'new_m_c_ln_g', 'new_m_c_ln_b', 'new_m_c_w_s', 'new_m_c_b_s', 'new_m_c_w_out', 'new_m_ffn_norm', 'new_m_ffn_w_gate', 'new_m_ffn_w_up', 'new_m_ffn_conv_w', 'new_m_ffn_conv_b', 'new_m_ffn_w_down', 'new_m_final_norm', 'new_v_ab_norm', 'new_v_ab_w_in', 'new_v_ab_q_norm', 'new_v_ab_w_q_b', 'new_v_ab_kv_norm', 'new_v_ab_w_kv_b', 'new_v_ab_conv_w', 'new_v_ab_conv_b', 'new_v_ab_w_rg_a', 'new_v_ab_b_rg_a', 'new_v_ab_w_rg_x', 'new_v_ab_b_rg_x', 'new_v_ab_lambda', 'new_v_ab_w_out', 'new_v_c_norm', 'new_v_c_w_in', 'new_v_c_ln_g', 'new_v_c_ln_b', 'new_v_c_w_s', 'new_v_c_b_s', 'new_v_c_w_out', 'new_v_ffn_norm', 'new_v_ffn_w_gate', 'new_v_ffn_w_up', 'new_v_ffn_conv_w', 'new_v_ffn_conv_b', 'new_v_ffn_w_down', 'new_v_final_norm']
TWIN_LEAF_KINDS = {'loss': 'loss', 'grad_x': 'grad_x', 'grad_ab_norm': 'grad_w', 'grad_ab_w_in': 'grad_w', 'grad_ab_q_norm': 'grad_w', 'grad_ab_w_q_b': 'grad_w', 'grad_ab_kv_norm': 'grad_w', 'grad_ab_w_kv_b': 'grad_w', 'grad_ab_conv_w': 'grad_w', 'grad_ab_conv_b': 'grad_w', 'grad_ab_w_rg_a': 'grad_w', 'grad_ab_b_rg_a': 'grad_w', 'grad_ab_w_rg_x': 'grad_w', 'grad_ab_b_rg_x': 'grad_w', 'grad_ab_lambda': 'grad_w', 'grad_ab_w_out': 'grad_w', 'grad_c_norm': 'grad_w', 'grad_c_w_in': 'grad_w', 'grad_c_ln_g': 'grad_w', 'grad_c_ln_b': 'grad_w', 'grad_c_w_s': 'grad_w', 'grad_c_b_s': 'grad_w', 'grad_c_w_out': 'grad_w', 'grad_ffn_norm': 'grad_w', 'grad_ffn_w_gate': 'grad_w', 'grad_ffn_w_up': 'grad_w', 'grad_ffn_conv_w': 'grad_w', 'grad_ffn_conv_b': 'grad_w', 'grad_ffn_w_down': 'grad_w', 'grad_final_norm': 'grad_w', 'delta_ab_norm': 'delta_w', 'delta_ab_w_in': 'delta_w', 'delta_ab_q_norm': 'delta_w', 'delta_ab_w_q_b': 'delta_w', 'delta_ab_kv_norm': 'delta_w', 'delta_ab_w_kv_b': 'delta_w', 'delta_ab_conv_w': 'delta_w', 'delta_ab_conv_b': 'delta_w', 'delta_ab_w_rg_a': 'delta_w', 'delta_ab_b_rg_a': 'delta_w', 'delta_ab_w_rg_x': 'delta_w', 'delta_ab_b_rg_x': 'delta_w', 'delta_ab_lambda': 'delta_w', 'delta_ab_w_out': 'delta_w', 'delta_c_norm': 'delta_w', 'delta_c_w_in': 'delta_w', 'delta_c_ln_g': 'delta_w', 'delta_c_ln_b': 'delta_w', 'delta_c_w_s': 'delta_w', 'delta_c_b_s': 'delta_w', 'delta_c_w_out': 'delta_w', 'delta_ffn_norm': 'delta_w', 'delta_ffn_w_gate': 'delta_w', 'delta_ffn_w_up': 'delta_w', 'delta_ffn_conv_w': 'delta_w', 'delta_ffn_conv_b': 'delta_w', 'delta_ffn_w_down': 'delta_w', 'delta_final_norm': 'delta_w', 'new_m_ab_norm': 'new_m', 'new_m_ab_w_in': 'new_m', 'new_m_ab_q_norm': 'new_m', 'new_m_ab_w_q_b': 'new_m', 'new_m_ab_kv_norm': 'new_m', 'new_m_ab_w_kv_b': 'new_m', 'new_m_ab_conv_w': 'new_m', 'new_m_ab_conv_b': 'new_m', 'new_m_ab_w_rg_a': 'new_m', 'new_m_ab_b_rg_a': 'new_m', 'new_m_ab_w_rg_x': 'new_m', 'new_m_ab_b_rg_x': 'new_m', 'new_m_ab_lambda': 'new_m', 'new_m_ab_w_out': 'new_m', 'new_m_c_norm': 'new_m', 'new_m_c_w_in': 'new_m', 'new_m_c_ln_g': 'new_m', 'new_m_c_ln_b': 'new_m', 'new_m_c_w_s': 'new_m', 'new_m_c_b_s': 'new_m', 'new_m_c_w_out': 'new_m', 'new_m_ffn_norm': 'new_m', 'new_m_ffn_w_gate': 'new_m', 'new_m_ffn_w_up': 'new_m', 'new_m_ffn_conv_w': 'new_m', 'new_m_ffn_conv_b': 'new_m', 'new_m_ffn_w_down': 'new_m', 'new_m_final_norm': 'new_m', 'new_v_ab_norm': 'new_v', 'new_v_ab_w_in': 'new_v', 'new_v_ab_q_norm': 'new_v', 'new_v_ab_w_q_b': 'new_v', 'new_v_ab_kv_norm': 'new_v', 'new_v_ab_w_kv_b': 'new_v', 'new_v_ab_conv_w': 'new_v', 'new_v_ab_conv_b': 'new_v', 'new_v_ab_w_rg_a': 'new_v', 'new_v_ab_b_rg_a': 'new_v', 'new_v_ab_w_rg_x': 'new_v', 'new_v_ab_b_rg_x': 'new_v', 'new_v_ab_lambda': 'new_v', 'new_v_ab_w_out': 'new_v', 'new_v_c_norm': 'new_v', 'new_v_c_w_in': 'new_v', 'new_v_c_ln_g': 'new_v', 'new_v_c_ln_b': 'new_v', 'new_v_c_w_s': 'new_v', 'new_v_c_b_s': 'new_v', 'new_v_c_w_out': 'new_v', 'new_v_ffn_norm': 'new_v', 'new_v_ffn_w_gate': 'new_v', 'new_v_ffn_w_up': 'new_v', 'new_v_ffn_conv_w': 'new_v', 'new_v_ffn_conv_b': 'new_v', 'new_v_ffn_w_down': 'new_v', 'new_v_final_norm': 'new_v'}


def _forward(args):
    return _fwd_reference(*[args[k] for k in FWD_PARAMS])


def _output_shape():
    out = _jax.eval_shape(lambda: _forward(_fwd_setup_inputs(0)))
    return out.shape, out.dtype

N_MICROBATCH = 1
ADAM_LR = 0.001
ADAM_B1 = 0.9
ADAM_B2 = 0.999
ADAM_EPS = 1e-08
ADAM_WD = 0.01
ADAM_STEP = 10
PER_EXAMPLE_BATCH_AXIS = {'x': 0, 'positions': 0, 'loss_target': 0}
SHARED_INPUTS = []
_WEIGHT_DTYPES = {'ab_norm': _jnp.float32, 'ab_w_in': _jnp.float32, 'ab_q_norm': _jnp.float32, 'ab_w_q_b': _jnp.float32, 'ab_kv_norm': _jnp.float32, 'ab_w_kv_b': _jnp.float32, 'ab_conv_w': _jnp.float32, 'ab_conv_b': _jnp.float32, 'ab_w_rg_a': _jnp.float32, 'ab_b_rg_a': _jnp.float32, 'ab_w_rg_x': _jnp.float32, 'ab_b_rg_x': _jnp.float32, 'ab_lambda': _jnp.float32, 'ab_w_out': _jnp.float32, 'c_norm': _jnp.float32, 'c_w_in': _jnp.float32, 'c_ln_g': _jnp.float32, 'c_ln_b': _jnp.float32, 'c_w_s': _jnp.float32, 'c_b_s': _jnp.float32, 'c_w_out': _jnp.float32, 'ffn_norm': _jnp.float32, 'ffn_w_gate': _jnp.float32, 'ffn_w_up': _jnp.float32, 'ffn_conv_w': _jnp.float32, 'ffn_conv_b': _jnp.float32, 'ffn_w_down': _jnp.float32, 'final_norm': _jnp.float32}
MOMENT_SCALE = {'ab_norm': 9.797813e-02, 'ab_w_in': 8.428778e-02, 'ab_q_norm': 6.028690e-02, 'ab_w_q_b': 3.474157e-02, 'ab_kv_norm': 1.214350e-01, 'ab_w_kv_b': 4.371492e-02, 'ab_conv_w': 1.008263e-01, 'ab_conv_b': 4.013944e-01, 'ab_w_rg_a': 2.474211e-02, 'ab_b_rg_a': 2.421452e-02, 'ab_w_rg_x': 4.388937e-02, 'ab_b_rg_x': 3.368052e-02, 'ab_lambda': 4.488746e-02, 'ab_w_out': 6.819967e-02, 'c_norm': 1.256300e-01, 'c_w_in': 9.069280e-02, 'c_ln_g': 6.558131e-02, 'c_ln_b': 6.036042e-02, 'c_w_s': 5.963829e-02, 'c_b_s': 8.738651e-02, 'c_w_out': 1.065442e-01, 'ffn_norm': 1.393724e-01, 'ffn_w_gate': 5.864868e-02, 'ffn_w_up': 5.706016e-02, 'ffn_conv_w': 5.912499e-02, 'ffn_conv_b': 5.662007e-02, 'ffn_w_down': 9.438073e-02, 'final_norm': 3.206818e+01}


def _to_microbatches(a, axis):
    t = _jnp.moveaxis(a, axis, 0)
    t = t.reshape((N_MICROBATCH, t.shape[0] // N_MICROBATCH) + t.shape[1:])
    return _jnp.moveaxis(t, 1, axis + 1)


def setup_inputs(seed: int = 0) -> dict:
    inp = _fwd_setup_inputs(seed)
    key = _jax.random.fold_in(_jax.random.key(seed), 7919)
    shape, _ = _output_shape()
    out = dict(inp)
    out["loss_target"] = _jax.random.normal(_jax.random.fold_in(key, 0), shape, _jnp.float32)
    for i, name in enumerate(TWIN_WEIGHTS):
        w = inp[name].astype(_jnp.float32)
        if MOMENT_SCALE is None:
            s = _jnp.sqrt(_jnp.mean(_jnp.square(w)) + 1e-30)
        else:
            s = MOMENT_SCALE[name]
        km, kv = _jax.random.split(_jax.random.fold_in(key, i + 1))
        out[name] = w
        out["m_" + name] = s * _jax.random.normal(km, w.shape, _jnp.float32)
        out["v_" + name] = (s * s) * _jax.random.uniform(kv, w.shape, _jnp.float32, 0.5, 1.5)
    if N_MICROBATCH > 1:
        for name, axis in PER_EXAMPLE_BATCH_AXIS.items():
            out[name] = _to_microbatches(out[name], axis)
    return {'x': out['x'], 'positions': out['positions'], 'ab_norm': out['ab_norm'], 'ab_w_in': out['ab_w_in'], 'ab_q_norm': out['ab_q_norm'], 'ab_w_q_b': out['ab_w_q_b'], 'ab_kv_norm': out['ab_kv_norm'], 'ab_w_kv_b': out['ab_w_kv_b'], 'ab_conv_w': out['ab_conv_w'], 'ab_conv_b': out['ab_conv_b'], 'ab_w_rg_a': out['ab_w_rg_a'], 'ab_b_rg_a': out['ab_b_rg_a'], 'ab_w_rg_x': out['ab_w_rg_x'], 'ab_b_rg_x': out['ab_b_rg_x'], 'ab_lambda': out['ab_lambda'], 'ab_w_out': out['ab_w_out'], 'c_norm': out['c_norm'], 'c_w_in': out['c_w_in'], 'c_ln_g': out['c_ln_g'], 'c_ln_b': out['c_ln_b'], 'c_w_s': out['c_w_s'], 'c_b_s': out['c_b_s'], 'c_w_out': out['c_w_out'], 'ffn_norm': out['ffn_norm'], 'ffn_w_gate': out['ffn_w_gate'], 'ffn_w_up': out['ffn_w_up'], 'ffn_conv_w': out['ffn_conv_w'], 'ffn_conv_b': out['ffn_conv_b'], 'ffn_w_down': out['ffn_w_down'], 'final_norm': out['final_norm'], 'loss_target': out['loss_target'], 'm_ab_norm': out['m_ab_norm'], 'm_ab_w_in': out['m_ab_w_in'], 'm_ab_q_norm': out['m_ab_q_norm'], 'm_ab_w_q_b': out['m_ab_w_q_b'], 'm_ab_kv_norm': out['m_ab_kv_norm'], 'm_ab_w_kv_b': out['m_ab_w_kv_b'], 'm_ab_conv_w': out['m_ab_conv_w'], 'm_ab_conv_b': out['m_ab_conv_b'], 'm_ab_w_rg_a': out['m_ab_w_rg_a'], 'm_ab_b_rg_a': out['m_ab_b_rg_a'], 'm_ab_w_rg_x': out['m_ab_w_rg_x'], 'm_ab_b_rg_x': out['m_ab_b_rg_x'], 'm_ab_lambda': out['m_ab_lambda'], 'm_ab_w_out': out['m_ab_w_out'], 'm_c_norm': out['m_c_norm'], 'm_c_w_in': out['m_c_w_in'], 'm_c_ln_g': out['m_c_ln_g'], 'm_c_ln_b': out['m_c_ln_b'], 'm_c_w_s': out['m_c_w_s'], 'm_c_b_s': out['m_c_b_s'], 'm_c_w_out': out['m_c_w_out'], 'm_ffn_norm': out['m_ffn_norm'], 'm_ffn_w_gate': out['m_ffn_w_gate'], 'm_ffn_w_up': out['m_ffn_w_up'], 'm_ffn_conv_w': out['m_ffn_conv_w'], 'm_ffn_conv_b': out['m_ffn_conv_b'], 'm_ffn_w_down': out['m_ffn_w_down'], 'm_final_norm': out['m_final_norm'], 'v_ab_norm': out['v_ab_norm'], 'v_ab_w_in': out['v_ab_w_in'], 'v_ab_q_norm': out['v_ab_q_norm'], 'v_ab_w_q_b': out['v_ab_w_q_b'], 'v_ab_kv_norm': out['v_ab_kv_norm'], 'v_ab_w_kv_b': out['v_ab_w_kv_b'], 'v_ab_conv_w': out['v_ab_conv_w'], 'v_ab_conv_b': out['v_ab_conv_b'], 'v_ab_w_rg_a': out['v_ab_w_rg_a'], 'v_ab_b_rg_a': out['v_ab_b_rg_a'], 'v_ab_w_rg_x': out['v_ab_w_rg_x'], 'v_ab_b_rg_x': out['v_ab_b_rg_x'], 'v_ab_lambda': out['v_ab_lambda'], 'v_ab_w_out': out['v_ab_w_out'], 'v_c_norm': out['v_c_norm'], 'v_c_w_in': out['v_c_w_in'], 'v_c_ln_g': out['v_c_ln_g'], 'v_c_ln_b': out['v_c_ln_b'], 'v_c_w_s': out['v_c_w_s'], 'v_c_b_s': out['v_c_b_s'], 'v_c_w_out': out['v_c_w_out'], 'v_ffn_norm': out['v_ffn_norm'], 'v_ffn_w_gate': out['v_ffn_w_gate'], 'v_ffn_w_up': out['v_ffn_w_up'], 'v_ffn_conv_w': out['v_ffn_conv_w'], 'v_ffn_conv_b': out['v_ffn_conv_b'], 'v_ffn_w_down': out['v_ffn_w_down'], 'v_final_norm': out['v_final_norm']}


def _loss(weights, diff, rest, loss_target):
    with _jax.named_scope("forward"):
        args = {**rest, TWIN_DIFF_INPUT: diff, **{k: w.astype(_WEIGHT_DTYPES[k]) for k, w in weights.items()}}
        y = _forward(args)
    with _jax.named_scope("loss_head"):
        err = _jnp.square(y.astype(_jnp.float32) - loss_target)
        return 0.5 * _jnp.sum(_jnp.mean(err, axis=-1)) if err.ndim else 0.5 * err


def _adamw(w, g, m, v):
    m = ADAM_B1 * m + (1.0 - ADAM_B1) * g
    v = ADAM_B2 * v + (1.0 - ADAM_B2) * _jnp.square(g)
    m_hat = m / (1.0 - ADAM_B1 ** ADAM_STEP)
    v_hat = v / (1.0 - ADAM_B2 ** ADAM_STEP)
    delta = -ADAM_LR * (m_hat / (_jnp.sqrt(v_hat) + ADAM_EPS) + ADAM_WD * w)
    return delta, m, v


def reference(x, positions, ab_norm, ab_w_in, ab_q_norm, ab_w_q_b, ab_kv_norm, ab_w_kv_b, ab_conv_w, ab_conv_b, ab_w_rg_a, ab_b_rg_a, ab_w_rg_x, ab_b_rg_x, ab_lambda, ab_w_out, c_norm, c_w_in, c_ln_g, c_ln_b, c_w_s, c_b_s, c_w_out, ffn_norm, ffn_w_gate, ffn_w_up, ffn_conv_w, ffn_conv_b, ffn_w_down, final_norm, loss_target, m_ab_norm, m_ab_w_in, m_ab_q_norm, m_ab_w_q_b, m_ab_kv_norm, m_ab_w_kv_b, m_ab_conv_w, m_ab_conv_b, m_ab_w_rg_a, m_ab_b_rg_a, m_ab_w_rg_x, m_ab_b_rg_x, m_ab_lambda, m_ab_w_out, m_c_norm, m_c_w_in, m_c_ln_g, m_c_ln_b, m_c_w_s, m_c_b_s, m_c_w_out, m_ffn_norm, m_ffn_w_gate, m_ffn_w_up, m_ffn_conv_w, m_ffn_conv_b, m_ffn_w_down, m_final_norm, v_ab_norm, v_ab_w_in, v_ab_q_norm, v_ab_w_q_b, v_ab_kv_norm, v_ab_w_kv_b, v_ab_conv_w, v_ab_conv_b, v_ab_w_rg_a, v_ab_b_rg_a, v_ab_w_rg_x, v_ab_b_rg_x, v_ab_lambda, v_ab_w_out, v_c_norm, v_c_w_in, v_c_ln_g, v_c_ln_b, v_c_w_s, v_c_b_s, v_c_w_out, v_ffn_norm, v_ffn_w_gate, v_ffn_w_up, v_ffn_conv_w, v_ffn_conv_b, v_ffn_w_down, v_final_norm):
    given = dict(x=x, positions=positions, ab_norm=ab_norm, ab_w_in=ab_w_in, ab_q_norm=ab_q_norm, ab_w_q_b=ab_w_q_b, ab_kv_norm=ab_kv_norm, ab_w_kv_b=ab_w_kv_b, ab_conv_w=ab_conv_w, ab_conv_b=ab_conv_b, ab_w_rg_a=ab_w_rg_a, ab_b_rg_a=ab_b_rg_a, ab_w_rg_x=ab_w_rg_x, ab_b_rg_x=ab_b_rg_x, ab_lambda=ab_lambda, ab_w_out=ab_w_out, c_norm=c_norm, c_w_in=c_w_in, c_ln_g=c_ln_g, c_ln_b=c_ln_b, c_w_s=c_w_s, c_b_s=c_b_s, c_w_out=c_w_out, ffn_norm=ffn_norm, ffn_w_gate=ffn_w_gate, ffn_w_up=ffn_w_up, ffn_conv_w=ffn_conv_w, ffn_conv_b=ffn_conv_b, ffn_w_down=ffn_w_down, final_norm=final_norm, loss_target=loss_target, m_ab_norm=m_ab_norm, m_ab_w_in=m_ab_w_in, m_ab_q_norm=m_ab_q_norm, m_ab_w_q_b=m_ab_w_q_b, m_ab_kv_norm=m_ab_kv_norm, m_ab_w_kv_b=m_ab_w_kv_b, m_ab_conv_w=m_ab_conv_w, m_ab_conv_b=m_ab_conv_b, m_ab_w_rg_a=m_ab_w_rg_a, m_ab_b_rg_a=m_ab_b_rg_a, m_ab_w_rg_x=m_ab_w_rg_x, m_ab_b_rg_x=m_ab_b_rg_x, m_ab_lambda=m_ab_lambda, m_ab_w_out=m_ab_w_out, m_c_norm=m_c_norm, m_c_w_in=m_c_w_in, m_c_ln_g=m_c_ln_g, m_c_ln_b=m_c_ln_b, m_c_w_s=m_c_w_s, m_c_b_s=m_c_b_s, m_c_w_out=m_c_w_out, m_ffn_norm=m_ffn_norm, m_ffn_w_gate=m_ffn_w_gate, m_ffn_w_up=m_ffn_w_up, m_ffn_conv_w=m_ffn_conv_w, m_ffn_conv_b=m_ffn_conv_b, m_ffn_w_down=m_ffn_w_down, m_final_norm=m_final_norm, v_ab_norm=v_ab_norm, v_ab_w_in=v_ab_w_in, v_ab_q_norm=v_ab_q_norm, v_ab_w_q_b=v_ab_w_q_b, v_ab_kv_norm=v_ab_kv_norm, v_ab_w_kv_b=v_ab_w_kv_b, v_ab_conv_w=v_ab_conv_w, v_ab_conv_b=v_ab_conv_b, v_ab_w_rg_a=v_ab_w_rg_a, v_ab_b_rg_a=v_ab_b_rg_a, v_ab_w_rg_x=v_ab_w_rg_x, v_ab_b_rg_x=v_ab_b_rg_x, v_ab_lambda=v_ab_lambda, v_ab_w_out=v_ab_w_out, v_c_norm=v_c_norm, v_c_w_in=v_c_w_in, v_c_ln_g=v_c_ln_g, v_c_ln_b=v_c_ln_b, v_c_w_s=v_c_w_s, v_c_b_s=v_c_b_s, v_c_w_out=v_c_w_out, v_ffn_norm=v_ffn_norm, v_ffn_w_gate=v_ffn_w_gate, v_ffn_w_up=v_ffn_w_up, v_ffn_conv_w=v_ffn_conv_w, v_ffn_conv_b=v_ffn_conv_b, v_ffn_w_down=v_ffn_w_down, v_final_norm=v_final_norm)
    weights = {n: given[n] for n in TWIN_WEIGHTS}
    shared = {n: given[n] for n in SHARED_INPUTS}
    per_example = {n: given[n] for n in ['x', 'positions']}
    grad_fn = _jax.value_and_grad(_loss, argnums=(0, 1))

    def one_microbatch(ex, loss_target):
        ex = dict(ex)
        diff = ex.pop(TWIN_DIFF_INPUT)
        return grad_fn(weights, diff, {**shared, **ex}, loss_target)

    if N_MICROBATCH == 1:
        loss, (grad_w, grad_x) = one_microbatch(per_example, given["loss_target"])
    else:
        def body(carry, xs):
            loss_sum, grad_sum = carry
            l_k, (gw_k, gx_k) = one_microbatch(xs[0], xs[1])
            with _jax.named_scope("update"):
                return (loss_sum + l_k, _jax.tree.map(_jnp.add, grad_sum, gw_k)), gx_k

        init = (_jnp.zeros((), _jnp.float32), _jax.tree.map(_jnp.zeros_like, weights))
        (loss, grad_w), grad_x = _jax.lax.scan(body, init, (per_example, given["loss_target"]))
    with _jax.named_scope("update"):
        delta_w, new_m, new_v = {}, {}, {}
        for n in TWIN_WEIGHTS:
            delta_w[n], new_m[n], new_v[n] = _adamw(weights[n], grad_w[n], given["m_" + n], given["v_" + n])
    return (loss, grad_x, *[grad_w[n] for n in TWIN_WEIGHTS], *[delta_w[n] for n in TWIN_WEIGHTS],
            *[new_m[n] for n in TWIN_WEIGHTS], *[new_v[n] for n in TWIN_WEIGHTS])
```

```python
import math

import jax
import jax.numpy as jnp
from jax import lax
from jax.experimental import pallas as pl
from jax.experimental.pallas import tpu as pltpu

F32 = jnp.float32
BF16 = jnp.bfloat16
MESH = pl.DeviceIdType.MESH

N_DEV = 8
LANES = 128
HALO = 8
VMEM_LIMIT = 56 << 20

NORM_EPS = 1e-6
HEADS = 8
HEAD_PAD = 128
QK_NOPE = 64
QK_ROPE = 32
ROPE_HALF = 16
ROPE_BASE = 10000.0
ATTN_SCALE = (QK_NOPE + QK_ROPE) ** -0.5
LRU_C = 8.0
LRU_W = 512
CHUNK = 128
SGU_GROUPS = 8
D_FF = 2816
FF_BLOCKS = 2

ADAM_LR, ADAM_B1, ADAM_B2, ADAM_EPS, ADAM_WD, ADAM_STEP = 0.001, 0.9, 0.999, 1e-08, 0.01, 10

WEIGHTS = ['ab_norm', 'ab_w_in', 'ab_q_norm', 'ab_w_q_b', 'ab_kv_norm', 'ab_w_kv_b', 'ab_conv_w', 'ab_conv_b',
           'ab_w_rg_a', 'ab_b_rg_a', 'ab_w_rg_x', 'ab_b_rg_x', 'ab_lambda', 'ab_w_out', 'c_norm', 'c_w_in', 'c_ln_g',
           'c_ln_b', 'c_w_s', 'c_b_s', 'c_w_out', 'ffn_norm', 'ffn_w_gate', 'ffn_w_up', 'ffn_conv_w', 'ffn_conv_b',
           'ffn_w_down', 'final_norm']
SHARD_AXIS = {'ab_w_in': 2, 'ab_w_q_b': 2, 'ab_w_kv_b': 2, 'ab_conv_w': 2, 'ab_w_out': 1, 'c_norm': 1, 'c_w_in': 2,
              'c_ln_g': 1, 'c_ln_b': 1, 'c_w_out': 1, 'ffn_w_gate': 2, 'ffn_w_up': 2, 'ffn_conv_w': 2, 'ffn_w_down': 1}
MATRICES = ['ab_w_in', 'ab_w_q_b', 'ab_w_kv_b', 'ab_w_out', 'c_w_in', 'c_w_out', 'ffn_w_gate', 'ffn_w_up', 'ffn_w_down']
BIG = ['ab_w_in', 'c_w_in', 'ffn_w_gate', 'ffn_w_up', 'ab_w_out', 'c_w_out', 'ffn_w_down']
REPLICATED = [n for n in WEIGHTS if n not in SHARD_AXIS]
SMALL_SHARDED = [n for n in WEIGHTS if n in SHARD_AXIS and n not in BIG]
SMALL = SMALL_SHARDED + REPLICATED


def _bf(x):
    return x.astype(BF16)


def _nn(a, b):
    return lax.dot_general(_bf(a), _bf(b), (((1,), (0,)), ((), ())), preferred_element_type=F32)


def _nt(a, b):
    return lax.dot_general(_bf(a), _bf(b), (((1,), (1,)), ((), ())), preferred_element_type=F32)


def _tn(a, b):
    return lax.dot_general(_bf(a), _bf(b), (((0,), (0,)), ((), ())), preferred_element_type=F32)


def _rms(x, g):
    return x * lax.rsqrt(jnp.mean(x * x, axis=-1, keepdims=True) + NORM_EPS) * g


def _layer_norm(x, g, b):
    xc = x - jnp.mean(x, axis=-1, keepdims=True)
    return xc * lax.rsqrt(jnp.mean(xc * xc, axis=-1, keepdims=True) + NORM_EPS) * g + b


def _gelu(x):
    return jax.nn.gelu(x)


def _colsum(x):
    return jnp.sum(x, axis=0, keepdims=True)


def _softplus(x):
    return jnp.maximum(x, 0.0) + jnp.log1p(jnp.exp(-jnp.abs(x)))


@jax.custom_vjp
def _one_minus_exp(x):
    u = jnp.exp(x)
    lg = jnp.log(u)
    near = lg == 0.0
    em1 = jnp.where(near, x, (u - 1.0) * x / jnp.where(near, 1.0, lg))
    return -jnp.where(x < -20.0, u - 1.0, em1)


def _one_minus_exp_fwd(x):
    return _one_minus_exp(x), x


def _one_minus_exp_bwd(x, ct):
    return (-jnp.exp(x) * ct,)


_one_minus_exp.defvjp(_one_minus_exp_fwd, _one_minus_exp_bwd)


def _accumulate(ref, val, first):
    @pl.when(first)
    def _():
        ref[...] = val

    @pl.when(jnp.logical_not(first))
    def _():
        ref[...] += val


def _params(n_axes=1):
    return pltpu.CompilerParams(dimension_semantics=("arbitrary",) * n_axes, vmem_limit_bytes=VMEM_LIMIT)


def _row(tm, n):
    return pl.BlockSpec((tm, n), lambda i: (i, 0))


def _const(shape):
    nd = len(shape)
    return pl.BlockSpec(shape, lambda i: (0,) * nd, pipeline_mode=pl.Buffered(1))


def _prev_halo(tm, n):
    return pl.BlockSpec((HALO, n), lambda i: (jnp.maximum(i * (tm // HALO) - 1, 0), 0))


def _next_halo(tm, n, n_tiles):
    last = n_tiles * (tm // HALO) - 1
    return pl.BlockSpec((HALO, n), lambda i: (jnp.minimum((i + 1) * (tm // HALO), last), 0))


def _sds(shape, dtype=F32):
    return jax.ShapeDtypeStruct(shape, dtype)


def _rope_tables(posb):
    lane = lax.broadcasted_iota(jnp.int32, posb.shape, 1)
    in_rope = jnp.logical_and(lane >= QK_NOPE, lane < QK_NOPE + QK_ROPE)
    j = (lane & (ROPE_HALF - 1)).astype(F32)
    inv_freq = jnp.exp((-math.log(ROPE_BASE)) * j / ROPE_HALF)
    ang = posb * inv_freq
    return jnp.where(in_rope, jnp.cos(ang), 1.0), jnp.where(in_rope, jnp.sin(ang), 0.0)


def _rot(q):
    n = q.shape[1]
    lane = lax.broadcasted_iota(jnp.int32, q.shape, 1) & (HEAD_PAD - 1)
    first_half = jnp.where(lane >= QK_NOPE, -pltpu.roll(q, n - ROPE_HALF, 1), 0.0)
    second_half = jnp.where(lane < QK_NOPE + QK_ROPE, pltpu.roll(q, ROPE_HALF, 1), 0.0)
    return jnp.where(lane < QK_NOPE + ROPE_HALF, first_half, second_half)


def _rope(q, cos_t, sin_t):
    return q * cos_t + _rot(q) * sin_t


def _rope_transpose(dq, cos_t, sin_t):
    return dq * cos_t - _rot(dq * sin_t)


def _tile_heads(t):
    return jnp.concatenate([t] * HEADS, axis=1)


Q_LORA, KV_LORA = 256, 128
Z_KPE = Q_LORA + KV_LORA
Z_LRU = Z_KPE + HEAD_PAD
Z_GATE = Z_LRU + LRU_W
Z_WIDTH = Z_GATE + LRU_W


def _ab_in_fwd(x, posb, w, tm):
    t, d = x.shape

    def body(x_ref, pos_ref, gn_ref, win_ref, qn_ref, wq_ref, kvn_ref, wk_ref, wv_ref, q_out, k_out, v_out, xl_out, gate_out):
        hn = _rms(x_ref[...], gn_ref[...])
        z = _nn(hn, win_ref[...])
        cqn = _rms(z[:, :Q_LORA], qn_ref[...])
        kvn = _rms(z[:, Q_LORA:Z_KPE], kvn_ref[...])
        cos_t, sin_t = _rope_tables(pos_ref[...])
        q_out[...] = _rope(_nn(cqn, wq_ref[...]), _tile_heads(cos_t), _tile_heads(sin_t))
        kpe = _rope(z[:, Z_KPE:Z_LRU], cos_t, sin_t)
        k_out[...] = _nn(kvn, wk_ref[...]) + _tile_heads(kpe)
        v_out[...] = _nn(kvn, wv_ref[...])
        xl_out[...] = z[:, Z_LRU:Z_GATE]
        gate_out[...] = z[:, Z_GATE:]

    hp = HEADS * HEAD_PAD
    return pl.pallas_call(
        body, name="ab_in_fwd", grid=(t // tm,),
        in_specs=[_row(tm, d), _row(tm, LANES), _const((1, d)), _const((d, Z_WIDTH)), _const((1, Q_LORA)), _const((Q_LORA, hp)),
                  _const((1, KV_LORA)), _const((KV_LORA, hp)), _const((KV_LORA, hp))],
        out_specs=[_row(tm, hp), _row(tm, hp), _row(tm, hp), _row(tm, LRU_W), _row(tm, LRU_W)],
        out_shape=[_sds((t, hp)), _sds((t, hp)), _sds((t, hp)), _sds((t, LRU_W)), _sds((t, LRU_W))],
        compiler_params=_params(),
    )(x, posb, w['ab_norm'], w['W_in'], w['ab_q_norm'], w['Wq'], w['ab_kv_norm'], w['Wk'], w['Wv'])


def _ab_in_bwd(x, posb, w, dq, dk, dv, dxl, dgate, dres, tm):
    t, d = x.shape
    hp = HEADS * HEAD_PAD

    def body(x_ref, pos_ref, gn_ref, win_ref, qn_ref, wq_ref, kvn_ref, wk_ref, wv_ref, dq_ref, dk_ref, dv_ref, dxl_ref, dgate_ref,
             dres_ref, dx_out, dgn_out, dwin_out, dqn_out, dwq_out, dkvn_out, dwk_out, dwv_out):
        first = pl.program_id(0) == 0
        hn, vjp_in = jax.vjp(_rms, x_ref[...], gn_ref[...])
        z = _nn(hn, win_ref[...])
        cqn, vjp_q = jax.vjp(_rms, z[:, :Q_LORA], qn_ref[...])
        kvn, vjp_kv = jax.vjp(_rms, z[:, Q_LORA:Z_KPE], kvn_ref[...])
        cos_t, sin_t = _rope_tables(pos_ref[...])
        dq0 = _rope_transpose(dq_ref[...], _tile_heads(cos_t), _tile_heads(sin_t))
        dk0 = dk_ref[...]
        dv0 = dv_ref[...]
        dkpe = dk0[:, :HEAD_PAD]
        for h in range(1, HEADS):
            dkpe = dkpe + dk0[:, h * HEAD_PAD:(h + 1) * HEAD_PAD]
        dkpe = _rope_transpose(dkpe, cos_t, sin_t)
        _accumulate(dwq_out, _tn(cqn, dq0), first)
        _accumulate(dwk_out, _tn(kvn, dk0), first)
        _accumulate(dwv_out, _tn(kvn, dv0), first)
        dcq, dqn = vjp_q(_nt(dq0, wq_ref[...]))
        dckv, dkvn = vjp_kv(_nt(dk0, wk_ref[...]) + _nt(dv0, wv_ref[...]))
        _accumulate(dqn_out, dqn, first)
        _accumulate(dkvn_out, dkvn, first)
        dz = jnp.concatenate([dcq, dckv, dkpe, dxl_ref[...], dgate_ref[...]], axis=1)
        _accumulate(dwin_out, _tn(hn, dz), first)
        dx, dgn = vjp_in(_nt(dz, win_ref[...]))
        _accumulate(dgn_out, dgn, first)
        dx_out[...] = dx + dres_ref[...]

    return pl.pallas_call(
        body, name="ab_in_bwd", grid=(t // tm,),
        in_specs=[_row(tm, d), _row(tm, LANES), _const((1, d)), _const((d, Z_WIDTH)), _const((1, Q_LORA)), _const((Q_LORA, hp)),
                  _const((1, KV_LORA)), _const((KV_LORA, hp)), _const((KV_LORA, hp)),
                  _row(tm, hp), _row(tm, hp), _row(tm, hp), _row(tm, LRU_W), _row(tm, LRU_W), _row(tm, d)],
        out_specs=[_row(tm, d), _const((1, d)), _const((d, Z_WIDTH)), _const((1, Q_LORA)), _const((Q_LORA, hp)),
                   _const((1, KV_LORA)), _const((KV_LORA, hp)), _const((KV_LORA, hp))],
        out_shape=[_sds((t, d)), _sds((1, d)), _sds((d, Z_WIDTH)), _sds((1, Q_LORA)), _sds((Q_LORA, hp)),
                   _sds((1, KV_LORA)), _sds((KV_LORA, hp)), _sds((KV_LORA, hp))],
        compiler_params=_params(),
    )(x, posb, w['ab_norm'], w['W_in'], w['ab_q_norm'], w['Wq'], w['ab_kv_norm'], w['Wk'], w['Wv'], dq, dk, dv, dxl, dgate, dres)


def _attn_probs(q_blk, k_ext, i, tq):
    ext = k_ext.shape[0]
    s = lax.dot_general(q_blk, k_ext, (((1,), (1,)), ((), ())), preferred_element_type=F32) * ATTN_SCALE
    row = lax.broadcasted_iota(jnp.int32, (tq, ext), 0) + i * tq
    col = lax.broadcasted_iota(jnp.int32, (tq, ext), 1)
    s = jnp.where(col <= row, s, -1e30)
    p = jnp.exp(s - jnp.max(s, axis=1, keepdims=True))
    return p / jnp.sum(p, axis=1, keepdims=True)


def _attn_fwd(q, k, v, tq):
    b, s, hp = q.shape
    blk = pl.BlockSpec((1, s, HEAD_PAD), lambda bi, h: (bi, 0, h))

    def body(q_ref, k_ref, v_ref, o_ref):
        kb = _bf(k_ref[0])
        vb = _bf(v_ref[0])
        for i in range(s // tq):
            ext = (i + 1) * tq
            p = _attn_probs(_bf(q_ref[0, i * tq:ext, :]), kb[:ext], i, tq)
            o_ref[0, i * tq:ext, :] = lax.dot_general(_bf(p), vb[:ext], (((1,), (0,)), ((), ())), preferred_element_type=F32)

    return pl.pallas_call(body, name="attn_fwd", grid=(b, HEADS), in_specs=[blk, blk, blk], out_specs=blk,
                          out_shape=_sds((b, s, hp)), compiler_params=_params(2))(q, k, v)


def _attn_bwd(q, k, v, do, tq):
    b, s, hp = q.shape
    blk = pl.BlockSpec((1, s, HEAD_PAD), lambda bi, h: (bi, 0, h))

    def body(q_ref, k_ref, v_ref, do_ref, dq_ref, dk_ref, dv_ref):
        kb = _bf(k_ref[0])
        vb = _bf(v_ref[0])
        dk_ref[...] = jnp.zeros_like(dk_ref)
        dv_ref[...] = jnp.zeros_like(dv_ref)
        for i in range(s // tq):
            ext = (i + 1) * tq
            qb = _bf(q_ref[0, i * tq:ext, :])
            dob = _bf(do_ref[0, i * tq:ext, :])
            p = _attn_probs(qb, kb[:ext], i, tq)
            dv_ref[0, :ext, :] += lax.dot_general(_bf(p), dob, (((0,), (0,)), ((), ())), preferred_element_type=F32)
            dp = lax.dot_general(dob, vb[:ext], (((1,), (1,)), ((), ())), preferred_element_type=F32)
            ds = _bf(p * (dp - jnp.sum(p * dp, axis=1, keepdims=True)) * ATTN_SCALE)
            dq_ref[0, i * tq:ext, :] = lax.dot_general(ds, kb[:ext], (((1,), (0,)), ((), ())), preferred_element_type=F32)
            dk_ref[0, :ext, :] += lax.dot_general(ds, qb, (((0,), (0,)), ((), ())), preferred_element_type=F32)

    return pl.pallas_call(body, name="attn_bwd", grid=(b, HEADS), in_specs=[blk, blk, blk, blk], out_specs=[blk, blk, blk],
                          out_shape=[_sds((b, s, hp))] * 3, compiler_params=_params(2))(q, k, v, do)


LRU_CONV = 4


def _lru_point(pre_a, pre_x, xc, lam):
    r = jax.nn.sigmoid(pre_a)
    i = jax.nn.sigmoid(pre_x)
    log_a = -LRU_C * r * _softplus(-lam)
    return jnp.exp(log_a), jnp.sqrt(_one_minus_exp(2.0 * log_a)) * (i * xc)


def _causal_conv(pad_ref, x, halo, first_in_seq, w, taps):
    tm = x.shape[0]
    pad_ref[:HALO, :] = jnp.where(first_in_seq, 0.0, halo)
    pad_ref[HALO:, :] = x
    y = w[taps - 1:taps, :] * x
    for k in range(taps - 1):
        off = HALO - (taps - 1) + k
        y = y + w[k:k + 1, :] * pad_ref[off:off + tm, :]
    return y


def _causal_conv_wgrad(pad_ref, dy, taps):
    tm = dy.shape[0]
    return jnp.concatenate([_colsum(dy * pad_ref[HALO - (taps - 1) + k:HALO - (taps - 1) + k + tm, :]) for k in range(taps)], axis=0)


def _causal_conv_transpose(pad_ref, dy, halo_next, last_in_seq, w, taps):
    tm = dy.shape[0]
    pad_ref[:tm, :] = dy
    pad_ref[tm:, :] = jnp.where(last_in_seq, 0.0, halo_next)
    dx = w[taps - 1:taps, :] * dy
    for k in range(taps - 1):
        off = (taps - 1) - k
        dx = dx + w[k:k + 1, :] * pad_ref[off:off + tm, :]
    return dx


def _lru_fwd(xl, gate, w, ts, seq):
    t, n = xl.shape
    tiles_per_seq = seq // ts

    def body(xl_ref, halo_ref, gate_ref, cw_ref, cb_ref, wa_ref, ba_ref, wx_ref, bx_ref, lam_ref, y_out, h_out, pad_ref, a_ref, b_ref, carry_ref):
        first_in_seq = pl.program_id(0) % tiles_per_seq == 0
        xc = _causal_conv(pad_ref, xl_ref[...], halo_ref[...], first_in_seq, cw_ref[...], LRU_CONV) + cb_ref[...]
        a, bx = _lru_point(_nn(xc, wa_ref[...]) + ba_ref[...], _nn(xc, wx_ref[...]) + bx_ref[...], xc, lam_ref[...])
        a_ref[...] = a
        b_ref[...] = bx

        @pl.when(first_in_seq)
        def _():
            carry_ref[...] = jnp.zeros_like(carry_ref)

        def step(r, h):
            h = a_ref[pl.ds(r, 1), :] * h + b_ref[pl.ds(r, 1), :]
            h_out[pl.ds(r, 1), :] = h
            return h

        carry_ref[...] = lax.fori_loop(0, ts, step, carry_ref[...], unroll=8)
        y_out[...] = h_out[...] * _gelu(gate_ref[...])

    return pl.pallas_call(
        body, name="lru_fwd", grid=(t // ts,),
        in_specs=[_row(ts, n), _prev_halo(ts, n), _row(ts, n), _const((LRU_CONV, n)), _const((1, n)), _const((n, n)), _const((1, n)),
                  _const((n, n)), _const((1, n)), _const((1, n))],
        out_specs=[_row(ts, n), _row(ts, n)], out_shape=[_sds((t, n)), _sds((t, n))],
        scratch_shapes=[pltpu.VMEM((HALO + ts, n), F32), pltpu.VMEM((ts, n), F32), pltpu.VMEM((ts, n), F32), pltpu.VMEM((1, n), F32)],
        compiler_params=_params(),
    )(xl, xl, gate, w['ab_conv_w'], w['ab_conv_b'], w['Wa'], w['ab_b_rg_a'], w['Wx'], w['ab_b_rg_x'], w['ab_lambda'])


def _lru_bwd(xl, gate, hs, dy, w, ts, seq):
    t, n = xl.shape
    tiles_per_seq = seq // ts
    n_tiles = t // ts

    def rev(i):
        return n_tiles - 1 - i

    row = pl.BlockSpec((ts, n), lambda i: (rev(i), 0))
    prev = pl.BlockSpec((HALO, n), lambda i: (jnp.maximum(rev(i) * (ts // HALO) - 1, 0), 0))
    acc = lambda shape: pl.BlockSpec(shape, lambda i: (0,) * len(shape))

    def body(xl_ref, xhalo_ref, gate_ref, h_ref, hhalo_ref, dy_ref, cw_ref, cb_ref, wa_ref, ba_ref, wx_ref, bx_ref, lam_ref,
             dxl_out, dgate_out, dcw_out, dcb_out, dwa_out, dba_out, dwx_out, dbx_out, dlam_out,
             pad_ref, padh_ref, padd_ref, a_ref, g_ref, carry_ref, dhalo_ref):
        step_id = pl.program_id(0)
        first = step_id == 0
        tile = rev(step_id)
        first_in_seq = tile % tiles_per_seq == 0
        last_in_seq = tile % tiles_per_seq == tiles_per_seq - 1
        cw = cw_ref[...]
        xc = _causal_conv(pad_ref, xl_ref[...], xhalo_ref[...], first_in_seq, cw, LRU_CONV) + cb_ref[...]
        pre_a = _nn(xc, wa_ref[...]) + ba_ref[...]
        pre_x = _nn(xc, wx_ref[...]) + bx_ref[...]
        (a, _), vjp_point = jax.vjp(_lru_point, pre_a, pre_x, xc, lam_ref[...])
        h = h_ref[...]
        _, vjp_out = jax.vjp(lambda h_, g_: h_ * _gelu(g_), h, gate_ref[...])
        dh, dgate = vjp_out(dy_ref[...])
        dgate_out[...] = dgate
        a_ref[...] = a
        g_ref[...] = dh

        @pl.when(last_in_seq)
        def _():
            carry_ref[...] = jnp.zeros_like(carry_ref)

        def step(j, c):
            r = ts - 1 - j
            g = g_ref[pl.ds(r, 1), :] + c
            g_ref[pl.ds(r, 1), :] = g
            return a_ref[pl.ds(r, 1), :] * g

        carry_ref[...] = lax.fori_loop(0, ts, step, carry_ref[...], unroll=8)
        g = g_ref[...]
        padh_ref[:HALO, :] = jnp.where(first_in_seq, 0.0, hhalo_ref[...])
        padh_ref[HALO:, :] = h
        dpre_a, dpre_x, dxc, dlam = vjp_point((g * padh_ref[HALO - 1:HALO - 1 + ts, :], g))
        dxc = dxc + _nt(dpre_a, wa_ref[...]) + _nt(dpre_x, wx_ref[...])
        _accumulate(dwa_out, _tn(xc, dpre_a), first)
        _accumulate(dwx_out, _tn(xc, dpre_x), first)
        _accumulate(dba_out, _colsum(dpre_a), first)
        _accumulate(dbx_out, _colsum(dpre_x), first)
        _accumulate(dlam_out, dlam, first)
        _accumulate(dcb_out, _colsum(dxc), first)
        _accumulate(dcw_out, _causal_conv_wgrad(pad_ref, dxc, LRU_CONV), first)
        dxl_out[...] = _causal_conv_transpose(padd_ref, dxc, dhalo_ref[...], last_in_seq, cw, LRU_CONV)
        dhalo_ref[...] = dxc[:HALO, :]

    return pl.pallas_call(
        body, name="lru_bwd", grid=(n_tiles,),
        in_specs=[row, prev, row, row, prev, row, _const((LRU_CONV, n)), _const((1, n)), _const((n, n)), _const((1, n)),
                  _const((n, n)), _const((1, n)), _const((1, n))],
        out_specs=[row, row, acc((LRU_CONV, n)), acc((1, n)), acc((n, n)), acc((1, n)), acc((n, n)), acc((1, n)), acc((1, n))],
        out_shape=[_sds((t, n)), _sds((t, n)), _sds((LRU_CONV, n)), _sds((1, n)), _sds((n, n)), _sds((1, n)), _sds((n, n)),
                   _sds((1, n)), _sds((1, n))],
        scratch_shapes=[pltpu.VMEM((HALO + ts, n), F32), pltpu.VMEM((HALO + ts, n), F32), pltpu.VMEM((ts + HALO, n), F32),
                        pltpu.VMEM((ts, n), F32), pltpu.VMEM((ts, n), F32), pltpu.VMEM((1, n), F32), pltpu.VMEM((HALO, n), F32)],
        compiler_params=_params(),
    )(xl, xl, gate, hs, hs, dy, w['ab_conv_w'], w['ab_conv_b'], w['Wa'], w['ab_b_rg_a'], w['Wx'], w['ab_b_rg_x'], w['ab_lambda'])


def _ab_out_fwd(x, o, y, w, tm):
    t, d = x.shape
    hp = o.shape[1]

    def body(x_ref, o_ref, y_ref, wa_ref, wb_ref, h_out):
        h_out[...] = x_ref[...] + _nn(o_ref[...], wa_ref[...]) + _nn(y_ref[...], wb_ref[...])

    return pl.pallas_call(body, name="ab_out_fwd", grid=(t // tm,),
                          in_specs=[_row(tm, d), _row(tm, hp), _row(tm, LRU_W), _const((hp, d)), _const((LRU_W, d))],
                          out_specs=_row(tm, d), out_shape=_sds((t, d)), compiler_params=_params())(x, o, y, w['Wo_a'], w['Wo_b'])


def _ab_out_bwd(o, y, dh, w, tm):
    t, d = dh.shape
    hp = o.shape[1]

    def body(o_ref, y_ref, dh_ref, wa_ref, wb_ref, do_out, dy_out, dwa_out, dwb_out):
        first = pl.program_id(0) == 0
        dh_t = dh_ref[...]
        do_out[...] = _nt(dh_t, wa_ref[...])
        dy_out[...] = _nt(dh_t, wb_ref[...])
        _accumulate(dwa_out, _tn(o_ref[...], dh_t), first)
        _accumulate(dwb_out, _tn(y_ref[...], dh_t), first)

    return pl.pallas_call(body, name="ab_out_bwd", grid=(t // tm,),
                          in_specs=[_row(tm, hp), _row(tm, LRU_W), _row(tm, d), _const((hp, d)), _const((LRU_W, d))],
                          out_specs=[_row(tm, hp), _row(tm, LRU_W), _const((hp, d)), _const((LRU_W, d))],
                          out_shape=[_sds((t, hp)), _sds((t, LRU_W)), _sds((hp, d)), _sds((LRU_W, d))],
                          compiler_params=_params())(o, y, dh, w['Wo_a'], w['Wo_b'])


FFN_CONV = 3


def _ffn_a_fwd(h, norm, wg, wu, tm):
    t, d = h.shape
    fb = D_FF // FF_BLOCKS

    def body(h_ref, gn_ref, wg_ref, wu_ref, g_out, u_out):
        hn = _rms(h_ref[...], gn_ref[...])
        g_out[...] = _nn(hn, wg_ref[...])
        u_out[...] = _nn(hn, wu_ref[...])

    wspec = pl.BlockSpec((d, fb), lambda f, i: (0, f))
    ospec = pl.BlockSpec((tm, fb), lambda f, i: (i, f))
    return pl.pallas_call(body, name="ffn_a_fwd", grid=(FF_BLOCKS, t // tm),
                          in_specs=[pl.BlockSpec((tm, d), lambda f, i: (i, 0)), pl.BlockSpec((1, d), lambda f, i: (0, 0)), wspec, wspec],
                          out_specs=[ospec, ospec], out_shape=[_sds((t, D_FF)), _sds((t, D_FF))],
                          compiler_params=_params(2))(h, norm, wg, wu)


def _ffn_b_fwd(g, u, h, cw, cb, wd, tm, seq):
    t, d = h.shape
    tiles_per_seq = seq // tm

    def body(g_ref, halo_ref, u_ref, h_ref, cw_ref, cb_ref, wd_ref, h_out, pad_ref):
        first_in_seq = pl.program_id(0) % tiles_per_seq == 0
        gc = _causal_conv(pad_ref, g_ref[...], halo_ref[...], first_in_seq, cw_ref[...], FFN_CONV) + cb_ref[...]
        h_out[...] = h_ref[...] + _nn(_gelu(gc) * u_ref[...], wd_ref[...])

    return pl.pallas_call(body, name="ffn_b_fwd", grid=(t // tm,),
                          in_specs=[_row(tm, D_FF), _prev_halo(tm, D_FF), _row(tm, D_FF), _row(tm, d), _const((FFN_CONV, D_FF)),
                                    _const((1, D_FF)), _const((D_FF, d))],
                          out_specs=_row(tm, d), out_shape=_sds((t, d)),
                          scratch_shapes=[pltpu.VMEM((HALO + tm, D_FF), F32)], compiler_params=_params())(g, g, u, h, cw, cb, wd)


def _ffn_b_bwd(g, u, dout, cw, cb, wd, tm, seq):
    t, d = dout.shape
    fb = D_FF // FF_BLOCKS
    tiles_per_seq = seq // tm

    def body(g_ref, halo_ref, u_ref, dout_ref, cw_ref, cb_ref, wd_ref, dgc_out, du_out, dwd_out, dcw_out, dcb_out, pad_ref):
        i = pl.program_id(1)
        first = i == 0
        gc = _causal_conv(pad_ref, g_ref[...], halo_ref[...], i % tiles_per_seq == 0, cw_ref[...], FFN_CONV) + cb_ref[...]
        act, vjp_act = jax.vjp(lambda gc_, u_: _gelu(gc_) * u_, gc, u_ref[...])
        dout_t = dout_ref[...]
        dgc, du = vjp_act(_nt(dout_t, wd_ref[...]))
        dgc_out[...] = dgc
        du_out[...] = du
        _accumulate(dwd_out, _tn(act, dout_t), first)
        _accumulate(dcb_out, _colsum(dgc), first)
        _accumulate(dcw_out, _causal_conv_wgrad(pad_ref, dgc, FFN_CONV), first)

    blk = pl.BlockSpec((tm, fb), lambda f, i: (i, f))
    halo = pl.BlockSpec((HALO, fb), lambda f, i: (jnp.maximum(i * (tm // HALO) - 1, 0), f))
    return pl.pallas_call(
        body, name="ffn_b_bwd", grid=(FF_BLOCKS, t // tm),
        in_specs=[blk, halo, blk, pl.BlockSpec((tm, d), lambda f, i: (i, 0)), pl.BlockSpec((FFN_CONV, fb), lambda f, i: (0, f)),
                  pl.BlockSpec((1, fb), lambda f, i: (0, f)), pl.BlockSpec((fb, d), lambda f, i: (f, 0))],
        out_specs=[blk, blk, pl.BlockSpec((fb, d), lambda f, i: (f, 0)), pl.BlockSpec((FFN_CONV, fb), lambda f, i: (0, f)),
                   pl.BlockSpec((1, fb), lambda f, i: (0, f))],
        out_shape=[_sds((t, D_FF)), _sds((t, D_FF)), _sds((D_FF, d)), _sds((FFN_CONV, D_FF)), _sds((1, D_FF))],
        scratch_shapes=[pltpu.VMEM((HALO + tm, fb), F32)], compiler_params=_params(2))(g, g, u, dout, cw, cb, wd)


def _ffn_a_dgrad(h, norm, dgc, du, dres, cw, wg, wu, tm, seq):
    t, d = h.shape
    tiles_per_seq = seq // tm
    n_tiles = t // tm

    def body(h_ref, gn_ref, dgc_ref, halo_ref, du_ref, dres_ref, cw_ref, wg_ref, wu_ref, dh_out, dg_out, dgn_out, pad_ref):
        i = pl.program_id(0)
        last_in_seq = i % tiles_per_seq == tiles_per_seq - 1
        dg = _causal_conv_transpose(pad_ref, dgc_ref[...], halo_ref[...], last_in_seq, cw_ref[...], FFN_CONV)
        dg_out[...] = dg
        _, vjp_norm = jax.vjp(_rms, h_ref[...], gn_ref[...])
        dh, dgn = vjp_norm(_nt(dg, wg_ref[...]) + _nt(du_ref[...], wu_ref[...]))
        dh_out[...] = dh + dres_ref[...]
        _accumulate(dgn_out, dgn, i == 0)

    return pl.pallas_call(
        body, name="ffn_a_dgrad", grid=(n_tiles,),
        in_specs=[_row(tm, d), _const((1, d)), _row(tm, D_FF), _next_halo(tm, D_FF, n_tiles), _row(tm, D_FF), _row(tm, d),
                  _const((FFN_CONV, D_FF)), _const((d, D_FF)), _const((d, D_FF))],
        out_specs=[_row(tm, d), _row(tm, D_FF), _const((1, d))], out_shape=[_sds((t, d)), _sds((t, D_FF)), _sds((1, d))],
        scratch_shapes=[pltpu.VMEM((tm + HALO, D_FF), F32)], compiler_params=_params())(h, norm, dgc, dgc, du, dres, cw, wg, wu)


def _ffn_a_wgrad(h, norm, dg, du, tm):
    t, d = h.shape
    fb = D_FF // FF_BLOCKS

    def body(h_ref, gn_ref, dg_ref, du_ref, dwg_out, dwu_out):
        first = pl.program_id(1) == 0
        hn = _rms(h_ref[...], gn_ref[...])
        _accumulate(dwg_out, _tn(hn, dg_ref[...]), first)
        _accumulate(dwu_out, _tn(hn, du_ref[...]), first)

    blk = pl.BlockSpec((tm, fb), lambda f, i: (i, f))
    wspec = pl.BlockSpec((d, fb), lambda f, i: (0, f))
    return pl.pallas_call(body, name="ffn_a_wgrad", grid=(FF_BLOCKS, t // tm),
                          in_specs=[pl.BlockSpec((tm, d), lambda f, i: (i, 0)), pl.BlockSpec((1, d), lambda f, i: (0, 0)), blk, blk],
                          out_specs=[wspec, wspec], out_shape=[_sds((d, D_FF)), _sds((d, D_FF))],
                          compiler_params=_params(2))(h, norm, dg, du)


def _sgu_mix(vn, ws_ref, bst):
    tril = lax.broadcasted_iota(jnp.int32, (CHUNK, CHUNK), 0) >= lax.broadcasted_iota(jnp.int32, (CHUNK, CHUNK), 1)
    wms = [jnp.where(tril, ws_ref[g], 0.0) for g in range(SGU_GROUPS)]
    chunks = []
    for n in range(vn.shape[0] // CHUNK):
        vc = vn[n * CHUNK:(n + 1) * CHUNK, :]
        chunks.append(jnp.concatenate(
            [_nn(wms[g], vc[:, g * CHUNK:(g + 1) * CHUNK]) + bst[:, g:g + 1] for g in range(SGU_GROUPS)], axis=1))
    return jnp.concatenate(chunks, axis=0)


def _sgu_fwd(h, w, tm):
    t, d = h.shape

    def body(h_ref, cn_ref, win_ref, lg_ref, lb_ref, ws_ref, bst_ref, wout_ref, h_out):
        h_t = h_ref[...]
        z = _gelu(_nn(_rms(h_t, cn_ref[...]), win_ref[...]))
        vn = _layer_norm(z[:, d:], lg_ref[...], lb_ref[...])
        s = _sgu_mix(vn, ws_ref, bst_ref[...])
        h_out[...] = h_t + _nn(z[:, :d] * s, wout_ref[...])

    return pl.pallas_call(
        body, name="sgu_fwd", grid=(t // tm,),
        in_specs=[_row(tm, d), _const((1, d)), _const((d, 2 * d)), _const((1, d)), _const((1, d)), _const((SGU_GROUPS, CHUNK, CHUNK)),
                  _const((CHUNK, LANES)), _const((d, d))],
        out_specs=_row(tm, d), out_shape=_sds((t, d)), compiler_params=_params(),
    )(h, w['c_norm'], w['c_w_in'], w['c_ln_g'], w['c_ln_b'], w['c_w_s'], w['bsT'], w['c_w_out'])


def _sgu_bwd(h, dout, w, tm):
    t, d = h.shape

    def body(h_ref, dout_ref, cn_ref, win_ref, lg_ref, lb_ref, ws_ref, bst_ref, wout_ref,
             dh_out, dcn_out, dwin_out, dlg_out, dlb_out, dws_out, dbst_out, dwout_out):
        first = pl.program_id(0) == 0
        hn, vjp_norm = jax.vjp(_rms, h_ref[...], cn_ref[...])
        zpre = _nn(hn, win_ref[...])
        u, vjp_u = jax.vjp(_gelu, zpre[:, :d])
        vn, vjp_v = jax.vjp(lambda zp, lg, lb: _layer_norm(_gelu(zp), lg, lb), zpre[:, d:], lg_ref[...], lb_ref[...])
        s = _sgu_mix(vn, ws_ref, bst_ref[...])
        dout_t = dout_ref[...]
        dus = _nt(dout_t, wout_ref[...])
        _accumulate(dwout_out, _tn(u * s, dout_t), first)
        ds = dus * u
        tril = lax.broadcasted_iota(jnp.int32, (CHUNK, CHUNK), 0) >= lax.broadcasted_iota(jnp.int32, (CHUNK, CHUNK), 1)
        lane = lax.broadcasted_iota(jnp.int32, (CHUNK, LANES), 1)
        dws = [jnp.zeros((CHUNK, CHUNK), F32) for _ in range(SGU_GROUPS)]
        dbst = jnp.zeros((CHUNK, LANES), F32)
        dvn_chunks = []
        for n in range(tm // CHUNK):
            cols = []
            for g in range(SGU_GROUPS):
                ds_ng = ds[n * CHUNK:(n + 1) * CHUNK, g * CHUNK:(g + 1) * CHUNK]
                vc_ng = vn[n * CHUNK:(n + 1) * CHUNK, g * CHUNK:(g + 1) * CHUNK]
                cols.append(_tn(jnp.where(tril, ws_ref[g], 0.0), ds_ng))
                dws[g] = dws[g] + _nt(ds_ng, vc_ng)
                dbst = dbst + jnp.where(lane == g, jnp.sum(ds_ng, axis=1, keepdims=True), 0.0)
            dvn_chunks.append(jnp.concatenate(cols, axis=1))
        dvn = jnp.concatenate(dvn_chunks, axis=0)
        for g in range(SGU_GROUPS):
            val = jnp.where(tril, dws[g], 0.0)

            @pl.when(first)
            def _():
                dws_out[g] = val

            @pl.when(jnp.logical_not(first))
            def _():
                dws_out[g] += val
        _accumulate(dbst_out, dbst, first)
        (dzu,) = vjp_u(dus * s)
        dzv, dlg, dlb = vjp_v(dvn)
        _accumulate(dlg_out, dlg, first)
        _accumulate(dlb_out, dlb, first)
        dzpre = jnp.concatenate([dzu, dzv], axis=1)
        _accumulate(dwin_out, _tn(hn, dzpre), first)
        dh, dcn = vjp_norm(_nt(dzpre, win_ref[...]))
        _accumulate(dcn_out, dcn, first)
        dh_out[...] = dh + dout_t

    return pl.pallas_call(
        body, name="sgu_bwd", grid=(t // tm,),
        in_specs=[_row(tm, d), _row(tm, d), _const((1, d)), _const((d, 2 * d)), _const((1, d)), _const((1, d)),
                  _const((SGU_GROUPS, CHUNK, CHUNK)), _const((CHUNK, LANES)), _const((d, d))],
        out_specs=[_row(tm, d), _const((1, d)), _const((d, 2 * d)), _const((1, d)), _const((1, d)), _const((SGU_GROUPS, CHUNK, CHUNK)),
                   _const((CHUNK, LANES)), _const((d, d))],
        out_shape=[_sds((t, d)), _sds((1, d)), _sds((d, 2 * d)), _sds((1, d)), _sds((1, d)), _sds((SGU_GROUPS, CHUNK, CHUNK)),
                   _sds((CHUNK, LANES)), _sds((d, d))],
        compiler_params=_params(),
    )(h, dout, w['c_norm'], w['c_w_in'], w['c_ln_g'], w['c_ln_b'], w['c_w_s'], w['bsT'], w['c_w_out'])


def _final_loss(h, target, norm, tm):
    t, d = h.shape

    def body(h_ref, tgt_ref, gn_ref, dh_out, loss_out, dgn_out):
        first = pl.program_id(0) == 0
        tgt = tgt_ref[...]

        def loss_fn(h_, g_):
            err = _rms(h_, g_) - tgt
            return 0.5 * jnp.sum(jnp.mean(err * err, axis=-1, keepdims=True), axis=0, keepdims=True)

        loss, vjp_loss = jax.vjp(loss_fn, h_ref[...], gn_ref[...])
        dh, dgn = vjp_loss(jnp.ones((1, 1), F32))
        dh_out[...] = dh
        _accumulate(loss_out, loss, first)
        _accumulate(dgn_out, dgn, first)

    return pl.pallas_call(body, name="final_loss", grid=(t // tm,), in_specs=[_row(tm, d), _row(tm, d), _const((1, d))],
                          out_specs=[_row(tm, d), _const((1, 1)), _const((1, d))],
                          out_shape=[_sds((t, d)), _sds((1, 1)), _sds((1, d))], compiler_params=_params())(h, target, norm)


def _tile(t, seq, want):
    tm = min(want, seq)
    assert seq % tm == 0 and t % tm == 0 and tm % CHUNK == 0
    return tm


def _local_step(x, posb, target, w, seq):
    t, d = x.shape
    b = t // seq
    hp = HEADS * HEAD_PAD
    tm_big, tm_mid = _tile(t, seq, 512), _tile(t, seq, 256)
    tq = _tile(t, seq, 512)

    q, k, v, xl, gate = _ab_in_fwd(x, posb, w, tm_big)
    o = _attn_fwd(q.reshape(b, seq, hp), k.reshape(b, seq, hp), v.reshape(b, seq, hp), tq).reshape(t, hp)
    y, hs = _lru_fwd(xl, gate, w, tm_big, seq)
    h1 = _ab_out_fwd(x, o, y, w, tm_big)
    hcur = h1
    saved = []
    for l in range(2):
        if l == 1:
            saved_h2 = hcur
            hcur = _sgu_fwd(hcur, w, tm_mid)
        g, u = _ffn_a_fwd(hcur, w['ffn_norm'][l], w['Wg'][l], w['Wu'][l], tm_big)
        hnext = _ffn_b_fwd(g, u, hcur, w['ffn_conv_w'][l], w['ffn_conv_b'][l], w['Wd'][l], tm_mid, seq)
        saved.append((hcur, g, u))
        hcur = hnext
    dh, loss, d_final = _final_loss(hcur, target, w['final_norm'], tm_big)

    grads = {'final_norm': d_final}
    ffn = {}
    for l in (1, 0):
        hin, g, u = saved[l]
        dgc, du, d_wd, d_cw, d_cb = _ffn_b_bwd(g, u, dh, w['ffn_conv_w'][l], w['ffn_conv_b'][l], w['Wd'][l], tm_mid, seq)
        dh, dg, d_norm = _ffn_a_dgrad(hin, w['ffn_norm'][l], dgc, du, dh, w['ffn_conv_w'][l], w['Wg'][l], w['Wu'][l], tm_mid, seq)
        d_wg, d_wu = _ffn_a_wgrad(hin, w['ffn_norm'][l], dg, du, tm_big)
        ffn[l] = dict(ffn_norm=d_norm, Wg=d_wg, Wu=d_wu, ffn_conv_w=d_cw, ffn_conv_b=d_cb, Wd=d_wd)
        if l == 1:
            dh, d_cn, d_cwin, d_lg, d_lb, d_ws, d_bst, d_cwout = _sgu_bwd(saved_h2, dh, w, tm_mid)
            grads.update(c_norm=d_cn, c_w_in=d_cwin, c_ln_g=d_lg, c_ln_b=d_lb, c_w_s=d_ws, bsT=d_bst, c_w_out=d_cwout)
    for name in ffn[0]:
        grads[name] = [ffn[0][name], ffn[1][name]]
    do, dy, d_woa, d_wob = _ab_out_bwd(o, y, dh, w, tm_big)
    dxl, dgate, d_cw, d_cb, d_wa, d_ba, d_wx, d_bx, d_lam = _lru_bwd(xl, gate, hs, dy, w, tm_big, seq)
    dq, dk, dv = _attn_bwd(q.reshape(b, seq, hp), k.reshape(b, seq, hp), v.reshape(b, seq, hp), do.reshape(b, seq, hp), tq)
    dx, d_gn, d_win, d_qn, d_wq, d_kvn, d_wk, d_wv = _ab_in_bwd(
        x, posb, w, dq.reshape(t, hp), dk.reshape(t, hp), dv.reshape(t, hp), dxl, dgate, dh, tm_mid)
    grads.update(Wo_a=d_woa, Wo_b=d_wob, ab_conv_w=d_cw, ab_conv_b=d_cb, Wa=d_wa, ab_b_rg_a=d_ba, Wx=d_wx, ab_b_rg_x=d_bx,
                 ab_lambda=d_lam, ab_norm=d_gn, W_in=d_win, ab_q_norm=d_qn, Wq=d_wq, ab_kv_norm=d_kvn, Wk=d_wk, Wv=d_wv)
    return loss, dx, grads


def _block_diag(wg):
    g, n, _ = wg.shape
    return jnp.einsum('gij,gh->gihj', wg, jnp.eye(g, dtype=wg.dtype)).reshape(g * n, g * n)


def _prepare(full):
    d = full['ab_w_in'].shape[1]
    w_in = full['ab_w_in'][0]
    zeros = lambda n: jnp.zeros((d, n), w_in.dtype)
    wq = full['ab_w_q_b'][0].reshape(Q_LORA, HEADS, QK_NOPE + QK_ROPE)
    wkv = full['ab_w_kv_b'][0].reshape(KV_LORA, HEADS, 2 * QK_NOPE)
    pad_head = lambda a: jnp.pad(a, ((0, 0), (0, 0), (0, HEAD_PAD - a.shape[2]))).reshape(a.shape[0], HEADS * HEAD_PAD)
    w_out = full['ab_w_out'][0]
    mla = HEADS * QK_NOPE
    w = {
        'W_in': jnp.concatenate([w_in[:, :Z_KPE], zeros(QK_NOPE), w_in[:, Z_KPE:Z_KPE + QK_ROPE],
                                 zeros(HEAD_PAD - QK_NOPE - QK_ROPE), w_in[:, Z_KPE + QK_ROPE:]], axis=1),
        'Wq': pad_head(wq), 'Wk': pad_head(wkv[:, :, :QK_NOPE]), 'Wv': pad_head(wkv[:, :, QK_NOPE:]),
        'Wo_a': jnp.pad(w_out[:mla].reshape(HEADS, QK_NOPE, d), ((0, 0), (0, HEAD_PAD - QK_NOPE), (0, 0))).reshape(HEADS * HEAD_PAD, d),
        'Wo_b': w_out[mla:],
        'Wa': _bf(_block_diag(full['ab_w_rg_a'][0])), 'Wx': _bf(_block_diag(full['ab_w_rg_x'][0])),
        'c_w_in': full['c_w_in'][0], 'c_w_out': full['c_w_out'][0], 'c_w_s': full['c_w_s'][0],
        'bsT': jnp.pad(full['c_b_s'][0].T, ((0, 0), (0, LANES - SGU_GROUPS))),
        'Wg': [full['ffn_w_gate'][l] for l in range(2)], 'Wu': [full['ffn_w_up'][l] for l in range(2)],
        'Wd': [full['ffn_w_down'][l] for l in range(2)],
        'ffn_norm': [full['ffn_norm'][l:l + 1] for l in range(2)], 'ffn_conv_w': [full['ffn_conv_w'][l] for l in range(2)],
        'ffn_conv_b': [full['ffn_conv_b'][l:l + 1] for l in range(2)],
        'ab_conv_w': full['ab_conv_w'][0], 'final_norm': full['final_norm'][None, :],
    }
    for name in ('ab_norm', 'ab_q_norm', 'ab_kv_norm', 'ab_conv_b', 'ab_b_rg_a', 'ab_b_rg_x', 'ab_lambda', 'c_norm', 'c_ln_g', 'c_ln_b'):
        w[name] = full[name]
    return w


def _unprepare(g):
    d = g['W_in'].shape[0]
    unpad_head = lambda a, n: a.reshape(a.shape[0], HEADS, HEAD_PAD)[:, :, :n]
    d_win = g['W_in']
    diag = lambda a: jnp.einsum('gigj->gij', a.reshape(HEADS, LRU_W // HEADS, HEADS, LRU_W // HEADS))
    out = {
        'ab_w_in': jnp.concatenate([d_win[:, :Z_KPE], d_win[:, Z_KPE + QK_NOPE:Z_KPE + QK_NOPE + QK_ROPE], d_win[:, Z_LRU:]], axis=1)[None],
        'ab_w_q_b': unpad_head(g['Wq'], QK_NOPE + QK_ROPE).reshape(1, Q_LORA, -1),
        'ab_w_kv_b': jnp.concatenate([unpad_head(g['Wk'], QK_NOPE), unpad_head(g['Wv'], QK_NOPE)], axis=2).reshape(1, KV_LORA, -1),
        'ab_w_out': jnp.concatenate([g['Wo_a'].reshape(HEADS, HEAD_PAD, d)[:, :QK_NOPE].reshape(HEADS * QK_NOPE, d), g['Wo_b']], axis=0)[None],
        'ab_w_rg_a': diag(g['Wa'])[None], 'ab_w_rg_x': diag(g['Wx'])[None],
        'c_w_in': g['c_w_in'][None], 'c_w_out': g['c_w_out'][None], 'c_w_s': g['c_w_s'][None],
        'c_b_s': g['bsT'][:, :SGU_GROUPS].T[None],
        'ffn_w_gate': jnp.stack(g['Wg']), 'ffn_w_up': jnp.stack(g['Wu']), 'ffn_w_down': jnp.stack(g['Wd']),
        'ffn_norm': jnp.concatenate(g['ffn_norm'], axis=0), 'ffn_conv_w': jnp.stack(g['ffn_conv_w']),
        'ffn_conv_b': jnp.concatenate(g['ffn_conv_b'], axis=0),
        'ab_conv_w': g['ab_conv_w'][None], 'final_norm': g['final_norm'][0],
    }
    for name in ('ab_norm', 'ab_q_norm', 'ab_kv_norm', 'ab_conv_b', 'ab_b_rg_a', 'ab_b_rg_x', 'ab_lambda', 'c_norm', 'c_ln_g', 'c_ln_b'):
        out[name] = g[name]
    return out


SLAB_ROWS = 16


def _round_up(n, m):
    return -(-n // m) * m


def _to_chunks(full, axis):
    s = full.shape
    return jnp.moveaxis(full.reshape(s[:axis] + (N_DEV, s[axis] // N_DEV) + s[axis + 1:]), axis, 0)


def _from_chunks(chunks, axis):
    local = chunks.shape[1:]
    return jnp.moveaxis(chunks, 0, axis).reshape(local[:axis] + (N_DEV * local[axis],) + local[axis + 1:])


def _slab_rows(n):
    return _round_up(-(-n // LANES), SLAB_ROWS)


def _to_slab(a, lead):
    a = a.reshape(lead + (-1,))
    rows = _slab_rows(a.shape[-1])
    a = jnp.pad(a, [(0, 0)] * len(lead) + [(0, rows * LANES - a.shape[-1])])
    return a.reshape(lead + (rows, LANES))


def _pack_slabs(parts, lead):
    return jnp.concatenate([_to_slab(p, lead) for p in parts], axis=len(lead))


def _unpack_slabs(packed, shapes):
    lead = packed.shape[:-2]
    out, row = [], 0
    for shape in shapes:
        size = math.prod(shape)
        rows = _slab_rows(size)
        piece = lax.slice_in_dim(packed, row, row + rows, axis=len(lead))
        out.append(piece.reshape(lead + (rows * LANES,))[..., :size].reshape(lead + tuple(shape)))
        row += rows
    return out


HBM = pl.BlockSpec(memory_space=pl.ANY)


def _other_chips(x, y):
    return [(1 - x, y), (x, 1 - y), (1 - x, 1 - y)]


def _all_gather(blocks):
    n = len(blocks)

    def body(*refs):
        x_refs, out_refs = refs[:n], refs[n:2 * n]
        send_sems, recv_sems, local_sems = refs[2 * n:]
        x, y, c = lax.axis_index("x"), lax.axis_index("y"), lax.axis_index("c")
        me, sibling = (x, y, c), (x, y, 1 - c)
        chips = _other_chips(x, y)

        def slab(a, px, py, pc):
            return out_refs[a].at[4 * px + 2 * py + pc]

        def copy(a, k, blk, to, src=None):
            return pltpu.make_async_remote_copy(src_ref=slab(a, *blk) if src is None else src, dst_ref=slab(a, *blk),
                                                send_sem=send_sems.at[7 * a + k], recv_sem=recv_sems.at[7 * a + k],
                                                device_id=to, device_id_type=MESH)

        mine = [pltpu.make_async_copy(x_refs[a], slab(a, *me), local_sems.at[a]) for a in range(n)]
        started = []
        for a in range(n):
            mine[a].start()
            started.append(copy(a, 0, me, sibling, src=x_refs[a]))
            started += [copy(a, 1 + j, me, (*chip, c), src=x_refs[a]) for j, chip in enumerate(chips)]
        for cp in started:
            cp.start()
        for j, chip in enumerate(chips):
            for a in range(n):
                copy(a, 1 + j, (*chip, c), me).wait_recv()
                passed = copy(a, 4 + j, (*chip, c), sibling)
                passed.start()
                started.append(passed)
        for a in range(n):
            copy(a, 0, sibling, me).wait_recv()
        for j, chip in enumerate(chips):
            for a in range(n):
                copy(a, 4 + j, (*chip, 1 - c), me).wait_recv()
        for cp in started:
            cp.wait_send()
        for a in range(n):
            mine[a].wait()

    return pl.pallas_call(
        body, name="all_gather_weights", out_shape=[jax.ShapeDtypeStruct((N_DEV,) + b.shape, b.dtype) for b in blocks],
        in_specs=[HBM] * n, out_specs=[HBM] * n,
        scratch_shapes=[pltpu.SemaphoreType.DMA((7 * n,)), pltpu.SemaphoreType.DMA((7 * n,)), pltpu.SemaphoreType.DMA((n,))],
    )(*blocks)


def _pair_exchange(chunks):
    n = len(chunks)

    def body(*refs):
        g_refs, theirs_refs = refs[:n], refs[n:2 * n]
        send_sems, recv_sems = refs[2 * n:]
        x, y, c = lax.axis_index("x"), lax.axis_index("y"), lax.axis_index("c")
        sends = [pltpu.make_async_remote_copy(src_ref=g_refs[a].at[2 * k + 1 - c], dst_ref=theirs_refs[a].at[k], send_sem=send_sems.at[4 * a + k],
                                              recv_sem=recv_sems.at[4 * a + k], device_id=(x, y, 1 - c), device_id_type=MESH)
                 for a in range(n) for k in range(4)]
        for cp in sends:
            cp.start()
        for cp in sends:
            cp.wait_recv()
        for cp in sends:
            cp.wait_send()

    return pl.pallas_call(
        body, name="grad_pair_exchange", out_shape=[jax.ShapeDtypeStruct((4,) + g.shape[1:], g.dtype) for g in chunks],
        in_specs=[HBM] * n, out_specs=[HBM] * n,
        scratch_shapes=[pltpu.SemaphoreType.DMA((4 * n,)), pltpu.SemaphoreType.DMA((4 * n,))],
    )(*chunks)


def _row_tile(rows):
    return rows // 2 if (rows // 2) % SLAB_ROWS == 0 else rows


def _pair_sum(chunks, theirs, core, name):
    _, l, r, n = theirs.shape
    tr = _row_tile(r)
    mine = pl.BlockSpec((1, 1, 1, tr, n), lambda k, li, ri, c_ref: (k, c_ref[0], li, ri, 0))
    blk = pl.BlockSpec((1, 1, tr, n), lambda k, li, ri, c_ref: (k, li, ri, 0))

    def body(c_ref, a_ref, b_ref, o_ref):
        o_ref[0, 0] = (a_ref[0, 0, 0].astype(F32) + b_ref[0, 0].astype(F32)).astype(o_ref.dtype)

    return pl.pallas_call(
        body, name="pair_sum_" + name,
        grid_spec=pltpu.PrefetchScalarGridSpec(num_scalar_prefetch=1, grid=(4, l, r // tr), in_specs=[mine, blk], out_specs=blk),
        out_shape=jax.ShapeDtypeStruct(theirs.shape, theirs.dtype), compiler_params=_params(3),
    )(core, chunks.reshape((4, 2) + chunks.shape[1:]), theirs)


def _chip_exchange(sums):
    n = len(sums)

    def body(*refs):
        p_refs, land_refs = refs[:n], refs[n:2 * n]
        send_sems, recv_sems, local_sems = refs[2 * n:]
        x, y, c = lax.axis_index("x"), lax.axis_index("y"), lax.axis_index("c")
        my_chip = 2 * x + y
        keeps = [pltpu.make_async_copy(p_refs[a].at[my_chip], land_refs[a].at[my_chip], local_sems.at[a]) for a in range(n)]

        def copy(a, j, cx, cy, src_chip, dst_chip):
            return pltpu.make_async_remote_copy(src_ref=p_refs[a].at[src_chip], dst_ref=land_refs[a].at[dst_chip], send_sem=send_sems.at[3 * a + j],
                                                recv_sem=recv_sems.at[3 * a + j], device_id=(cx, cy, c), device_id_type=MESH)

        sends = [copy(a, j, cx, cy, 2 * cx + cy, my_chip) for a in range(n) for j, (cx, cy) in enumerate(_other_chips(x, y))]
        for cp in keeps + sends:
            cp.start()
        for a in range(n):
            for j, (cx, cy) in enumerate(_other_chips(x, y)):
                copy(a, j, cx, cy, my_chip, 2 * cx + cy).wait_recv()
        for cp in sends:
            cp.wait_send()
        for cp in keeps:
            cp.wait()

    return pl.pallas_call(
        body, name="grad_chip_exchange", out_shape=[jax.ShapeDtypeStruct(s.shape, s.dtype) for s in sums],
        in_specs=[HBM] * n, out_specs=[HBM] * n,
        scratch_shapes=[pltpu.SemaphoreType.DMA((3 * n,)), pltpu.SemaphoreType.DMA((3 * n,)), pltpu.SemaphoreType.DMA((n,))],
    )(*sums)


def _sum_and_adamw(landed, wts, m, v, name):
    l, r, n = wts.shape
    tr = _row_tile(r)
    blk = pl.BlockSpec((1, tr, n), lambda li, ri: (li, ri, 0))
    c1 = 1.0 / (1.0 - ADAM_B1 ** ADAM_STEP)
    c2 = 1.0 / (1.0 - ADAM_B2 ** ADAM_STEP)

    def body(l_ref, w_ref, m_ref, v_ref, g_out, d_out, m_out, v_out):
        g = l_ref[0].astype(F32)
        for k in range(1, 4):
            g = g + l_ref[k].astype(F32)
        m_new = ADAM_B1 * m_ref[...] + (1.0 - ADAM_B1) * g
        v_new = ADAM_B2 * v_ref[...] + (1.0 - ADAM_B2) * (g * g)
        g_out[...] = g
        m_out[...] = m_new
        v_out[...] = v_new
        d_out[...] = -ADAM_LR * ((m_new * c1) / (jnp.sqrt(v_new * c2) + ADAM_EPS) + ADAM_WD * w_ref[...])

    return pl.pallas_call(body, name="adamw_" + name, grid=(l, r // tr),
                          in_specs=[pl.BlockSpec((4, 1, tr, n), lambda li, ri: (0, li, ri, 0)), blk, blk, blk], out_specs=[blk] * 4,
                          out_shape=[_sds((l, r, n))] * 4, compiler_params=_params(2))(landed, wts, m, v)


def _gather_weights(local):
    small = [_bf(local[n]) if n in MATRICES else lax.bitcast_convert_type(local[n], BF16) for n in SMALL_SHARDED]
    gathered = _all_gather([_bf(local[n]) for n in BIG] + [_pack_slabs(small, ())])
    full = {n: local[n] for n in REPLICATED}
    for n, g in zip(BIG, gathered):
        full[n] = _from_chunks(g, SHARD_AXIS[n])
    for n, p in zip(SMALL_SHARDED, _unpack_slabs(gathered[-1], [s.shape for s in small])):
        full[n] = _from_chunks(p if n in MATRICES else lax.bitcast_convert_type(p, F32), SHARD_AXIS[n])
    return full


def kernel(x, positions, ab_norm, ab_w_in, ab_q_norm, ab_w_q_b, ab_kv_norm, ab_w_kv_b, ab_conv_w, ab_conv_b, ab_w_rg_a, ab_b_rg_a, ab_w_rg_x, ab_b_rg_x, ab_lambda, ab_w_out, c_norm, c_w_in, c_ln_g, c_ln_b, c_w_s, c_b_s, c_w_out, ffn_norm, ffn_w_gate, ffn_w_up, ffn_conv_w, ffn_conv_b, ffn_w_down, final_norm, loss_target, m_ab_norm, m_ab_w_in, m_ab_q_norm, m_ab_w_q_b, m_ab_kv_norm, m_ab_w_kv_b, m_ab_conv_w, m_ab_conv_b, m_ab_w_rg_a, m_ab_b_rg_a, m_ab_w_rg_x, m_ab_b_rg_x, m_ab_lambda, m_ab_w_out, m_c_norm, m_c_w_in, m_c_ln_g, m_c_ln_b, m_c_w_s, m_c_b_s, m_c_w_out, m_ffn_norm, m_ffn_w_gate, m_ffn_w_up, m_ffn_conv_w, m_ffn_conv_b, m_ffn_w_down, m_final_norm, v_ab_norm, v_ab_w_in, v_ab_q_norm, v_ab_w_q_b, v_ab_kv_norm, v_ab_w_kv_b, v_ab_conv_w, v_ab_conv_b, v_ab_w_rg_a, v_ab_b_rg_a, v_ab_w_rg_x, v_ab_b_rg_x, v_ab_lambda, v_ab_w_out, v_c_norm, v_c_w_in, v_c_ln_g, v_c_ln_b, v_c_w_s, v_c_b_s, v_c_w_out, v_ffn_norm, v_ffn_w_gate, v_ffn_w_up, v_ffn_conv_w, v_ffn_conv_b, v_ffn_w_down, v_final_norm):
    given = dict(locals())
    local = {n: given[n] for n in WEIGHTS}
    b, seq, d = x.shape
    t = b * seq

    w = _prepare(_gather_weights(local))
    posb = jnp.broadcast_to(positions.astype(F32).reshape(t, 1), (t, LANES))
    loss, dx, grads = _local_step(x.reshape(t, d), posb, loss_target.reshape(t, d), w, seq)
    full_grads = _unprepare(grads)

    chunks = [_bf(_to_chunks(full_grads[n], SHARD_AXIS[n])) for n in BIG]
    small = [_to_chunks(full_grads[n], SHARD_AXIS[n]) for n in SMALL_SHARDED]
    small += [jnp.broadcast_to(full_grads[n][None], (N_DEV,) + full_grads[n].shape) for n in REPLICATED]
    chunks.append(_bf(_pack_slabs(small, (N_DEV,)))[:, None])
    names = BIG + ['small']
    core = lax.axis_index("c").astype(jnp.int32).reshape(1)
    theirs = _pair_exchange(chunks)
    landed = _chip_exchange([_pair_sum(a, b, core, n) for a, b, n in zip(chunks, theirs, names)])

    updated = {n: _sum_and_adamw(l, given[n], given['m_' + n], given['v_' + n], n) for n, l in zip(BIG, landed)}
    pack_small = lambda prefix: _pack_slabs([given[prefix + n] for n in SMALL], ())[None]
    packed = _sum_and_adamw(landed[-1], pack_small(''), pack_small('m_'), pack_small('v_'), 'small')
    unpacked = [_unpack_slabs(p[0], [local[n].shape for n in SMALL]) for p in packed]
    for i, n in enumerate(SMALL):
        updated[n] = [u[i] for u in unpacked]
    total = lax.psum(loss[0, 0], ("x", "y", "c"))
    return (total, dx.reshape(b, seq, d), *[updated[n][kind] for kind in range(4) for n in WEIGHTS])
```

```python
import math

import jax
import jax.numpy as jnp
from jax import lax
from jax.experimental import pallas as pl
from jax.experimental.pallas import tpu as pltpu

F32 = jnp.float32
BF16 = jnp.bfloat16
MESH = pl.DeviceIdType.MESH

N_DEV = 8
LANES = 128
HALO = 8
VMEM_LIMIT = 56 << 20

NORM_EPS = 1e-6
HEADS = 8
HEAD_PAD = 128
QK_NOPE = 64
QK_ROPE = 32
ROPE_HALF = 16
ROPE_BASE = 10000.0
ATTN_SCALE = (QK_NOPE + QK_ROPE) ** -0.5
LRU_C = 8.0
LRU_W = 512
CHUNK = 128
SGU_GROUPS = 8
D_FF = 2816
FF_BLOCKS = 2

ADAM_LR, ADAM_B1, ADAM_B2, ADAM_EPS, ADAM_WD, ADAM_STEP = 0.001, 0.9, 0.999, 1e-08, 0.01, 10

WEIGHTS = ['ab_norm', 'ab_w_in', 'ab_q_norm', 'ab_w_q_b', 'ab_kv_norm', 'ab_w_kv_b', 'ab_conv_w', 'ab_conv_b',
           'ab_w_rg_a', 'ab_b_rg_a', 'ab_w_rg_x', 'ab_b_rg_x', 'ab_lambda', 'ab_w_out', 'c_norm', 'c_w_in', 'c_ln_g',
           'c_ln_b', 'c_w_s', 'c_b_s', 'c_w_out', 'ffn_norm', 'ffn_w_gate', 'ffn_w_up', 'ffn_conv_w', 'ffn_conv_b',
           'ffn_w_down', 'final_norm']
SHARD_AXIS = {'ab_w_in': 2, 'ab_w_q_b': 2, 'ab_w_kv_b': 2, 'ab_conv_w': 2, 'ab_w_out': 1, 'c_norm': 1, 'c_w_in': 2,
              'c_ln_g': 1, 'c_ln_b': 1, 'c_w_out': 1, 'ffn_w_gate': 2, 'ffn_w_up': 2, 'ffn_conv_w': 2, 'ffn_w_down': 1}
MATRICES = ['ab_w_in', 'ab_w_q_b', 'ab_w_kv_b', 'ab_w_out', 'c_w_in', 'c_w_out', 'ffn_w_gate', 'ffn_w_up', 'ffn_w_down']
BIG = ['ab_w_in', 'c_w_in', 'ffn_w_gate', 'ffn_w_up', 'ab_w_out', 'c_w_out', 'ffn_w_down']
REPLICATED = [n for n in WEIGHTS if n not in SHARD_AXIS]
SMALL_SHARDED = [n for n in WEIGHTS if n in SHARD_AXIS and n not in BIG]
SMALL = SMALL_SHARDED + REPLICATED


def _bf(x):
    return x.astype(BF16)


def _nn(a, b):
    return lax.dot_general(_bf(a), _bf(b), (((1,), (0,)), ((), ())), preferred_element_type=F32)


def _nt(a, b):
    return lax.dot_general(_bf(a), _bf(b), (((1,), (1,)), ((), ())), preferred_element_type=F32)


def _tn(a, b):
    return lax.dot_general(_bf(a), _bf(b), (((0,), (0,)), ((), ())), preferred_element_type=F32)


def _rms(x, g):
    return x * lax.rsqrt(jnp.mean(x * x, axis=-1, keepdims=True) + NORM_EPS) * g


def _layer_norm(x, g, b):
    xc = x - jnp.mean(x, axis=-1, keepdims=True)
    return xc * lax.rsqrt(jnp.mean(xc * xc, axis=-1, keepdims=True) + NORM_EPS) * g + b


def _gelu(x):
    return jax.nn.gelu(x)


def _colsum(x):
    return jnp.sum(x, axis=0, keepdims=True)


def _softplus(x):
    return jnp.maximum(x, 0.0) + jnp.log1p(jnp.exp(-jnp.abs(x)))


@jax.custom_vjp
def _one_minus_exp(x):
    u = jnp.exp(x)
    lg = jnp.log(u)
    near = lg == 0.0
    em1 = jnp.where(near, x, (u - 1.0) * x / jnp.where(near, 1.0, lg))
    return -jnp.where(x < -20.0, u - 1.0, em1)


def _one_minus_exp_fwd(x):
    return _one_minus_exp(x), x


def _one_minus_exp_bwd(x, ct):
    return (-jnp.exp(x) * ct,)


_one_minus_exp.defvjp(_one_minus_exp_fwd, _one_minus_exp_bwd)


def _accumulate(ref, val, first):
    @pl.when(first)
    def _():
        ref[...] = val

    @pl.when(jnp.logical_not(first))
    def _():
        ref[...] += val


def _params(n_axes=1):
    return pltpu.CompilerParams(dimension_semantics=("arbitrary",) * n_axes, vmem_limit_bytes=VMEM_LIMIT)


def _row(tm, n):
    return pl.BlockSpec((tm, n), lambda i: (i, 0))


def _const(shape):
    nd = len(shape)
    return pl.BlockSpec(shape, lambda i: (0,) * nd, pipeline_mode=pl.Buffered(1))


def _prev_halo(tm, n):
    return pl.BlockSpec((HALO, n), lambda i: (jnp.maximum(i * (tm // HALO) - 1, 0), 0))


def _next_halo(tm, n, n_tiles):
    last = n_tiles * (tm // HALO) - 1
    return pl.BlockSpec((HALO, n), lambda i: (jnp.minimum((i + 1) * (tm // HALO), last), 0))


def _sds(shape, dtype=F32):
    return jax.ShapeDtypeStruct(shape, dtype)


def _rope_tables(posb):
    lane = lax.broadcasted_iota(jnp.int32, posb.shape, 1)
    in_rope = jnp.logical_and(lane >= QK_NOPE, lane < QK_NOPE + QK_ROPE)
    j = (lane & (ROPE_HALF - 1)).astype(F32)
    inv_freq = jnp.exp((-math.log(ROPE_BASE)) * j / ROPE_HALF)
    ang = posb * inv_freq
    return jnp.where(in_rope, jnp.cos(ang), 1.0), jnp.where(in_rope, jnp.sin(ang), 0.0)


def _rot(q):
    n = q.shape[1]
    lane = lax.broadcasted_iota(jnp.int32, q.shape, 1) & (HEAD_PAD - 1)
    first_half = jnp.where(lane >= QK_NOPE, -pltpu.roll(q, n - ROPE_HALF, 1), 0.0)
    second_half = jnp.where(lane < QK_NOPE + QK_ROPE, pltpu.roll(q, ROPE_HALF, 1), 0.0)
    return jnp.where(lane < QK_NOPE + ROPE_HALF, first_half, second_half)


def _rope(q, cos_t, sin_t):
    return q * cos_t + _rot(q) * sin_t


def _rope_transpose(dq, cos_t, sin_t):
    return dq * cos_t - _rot(dq * sin_t)


def _tile_heads(t):
    return jnp.concatenate([t] * HEADS, axis=1)


Q_LORA, KV_LORA = 256, 128
Z_KPE = Q_LORA + KV_LORA
Z_LRU = Z_KPE + HEAD_PAD
Z_GATE = Z_LRU + LRU_W
Z_WIDTH = Z_GATE + LRU_W


def _ab_in_fwd(x, posb, w, tm):
    t, d = x.shape

    def body(x_ref, pos_ref, gn_ref, win_ref, qn_ref, wq_ref, kvn_ref, wk_ref, wv_ref, q_out, k_out, v_out, xl_out, gate_out):
        hn = _rms(x_ref[...], gn_ref[...])
        z = _nn(hn, win_ref[...])
        cqn = _rms(z[:, :Q_LORA], qn_ref[...])
        kvn = _rms(z[:, Q_LORA:Z_KPE], kvn_ref[...])
        cos_t, sin_t = _rope_tables(pos_ref[...])
        q_out[...] = _rope(_nn(cqn, wq_ref[...]), _tile_heads(cos_t), _tile_heads(sin_t))
        kpe = _rope(z[:, Z_KPE:Z_LRU], cos_t, sin_t)
        k_out[...] = _nn(kvn, wk_ref[...]) + _tile_heads(kpe)
        v_out[...] = _nn(kvn, wv_ref[...])
        xl_out[...] = z[:, Z_LRU:Z_GATE]
        gate_out[...] = z[:, Z_GATE:]

    hp = HEADS * HEAD_PAD
    return pl.pallas_call(
        body, name="ab_in_fwd", grid=(t // tm,),
        in_specs=[_row(tm, d), _row(tm, LANES), _const((1, d)), _const((d, Z_WIDTH)), _const((1, Q_LORA)), _const((Q_LORA, hp)),
                  _const((1, KV_LORA)), _const((KV_LORA, hp)), _const((KV_LORA, hp))],
        out_specs=[_row(tm, hp), _row(tm, hp), _row(tm, hp), _row(tm, LRU_W), _row(tm, LRU_W)],
        out_shape=[_sds((t, hp)), _sds((t, hp)), _sds((t, hp)), _sds((t, LRU_W)), _sds((t, LRU_W))],
        compiler_params=_params(),
    )(x, posb, w['ab_norm'], w['W_in'], w['ab_q_norm'], w['Wq'], w['ab_kv_norm'], w['Wk'], w['Wv'])


def _ab_in_bwd(x, posb, w, dq, dk, dv, dxl, dgate, dres, tm):
    t, d = x.shape
    hp = HEADS * HEAD_PAD

    def body(x_ref, pos_ref, gn_ref, win_ref, qn_ref, wq_ref, kvn_ref, wk_ref, wv_ref, dq_ref, dk_ref, dv_ref, dxl_ref, dgate_ref,
             dres_ref, dx_out, dgn_out, dwin_out, dqn_out, dwq_out, dkvn_out, dwk_out, dwv_out):
        first = pl.program_id(0) == 0
        hn, vjp_in = jax.vjp(_rms, x_ref[...], gn_ref[...])
        z = _nn(hn, win_ref[...])
        cqn, vjp_q = jax.vjp(_rms, z[:, :Q_LORA], qn_ref[...])
        kvn, vjp_kv = jax.vjp(_rms, z[:, Q_LORA:Z_KPE], kvn_ref[...])
        cos_t, sin_t = _rope_tables(pos_ref[...])
        dq0 = _rope_transpose(dq_ref[...], _tile_heads(cos_t), _tile_heads(sin_t))
        dk0 = dk_ref[...]
        dv0 = dv_ref[...]
        dkpe = dk0[:, :HEAD_PAD]
        for h in range(1, HEADS):
            dkpe = dkpe + dk0[:, h * HEAD_PAD:(h + 1) * HEAD_PAD]
        dkpe = _rope_transpose(dkpe, cos_t, sin_t)
        _accumulate(dwq_out, _tn(cqn, dq0), first)
        _accumulate(dwk_out, _tn(kvn, dk0), first)
        _accumulate(dwv_out, _tn(kvn, dv0), first)
        dcq, dqn = vjp_q(_nt(dq0, wq_ref[...]))
        dckv, dkvn = vjp_kv(_nt(dk0, wk_ref[...]) + _nt(dv0, wv_ref[...]))
        _accumulate(dqn_out, dqn, first)
        _accumulate(dkvn_out, dkvn, first)
        dz = jnp.concatenate([dcq, dckv, dkpe, dxl_ref[...], dgate_ref[...]], axis=1)
        _accumulate(dwin_out, _tn(hn, dz), first)
        dx, dgn = vjp_in(_nt(dz, win_ref[...]))
        _accumulate(dgn_out, dgn, first)
        dx_out[...] = dx + dres_ref[...]

    return pl.pallas_call(
        body, name="ab_in_bwd", grid=(t // tm,),
        in_specs=[_row(tm, d), _row(tm, LANES), _const((1, d)), _const((d, Z_WIDTH)), _const((1, Q_LORA)), _const((Q_LORA, hp)),
                  _const((1, KV_LORA)), _const((KV_LORA, hp)), _const((KV_LORA, hp)),
                  _row(tm, hp), _row(tm, hp), _row(tm, hp), _row(tm, LRU_W), _row(tm, LRU_W), _row(tm, d)],
        out_specs=[_row(tm, d), _const((1, d)), _const((d, Z_WIDTH)), _const((1, Q_LORA)), _const((Q_LORA, hp)),
                   _const((1, KV_LORA)), _const((KV_LORA, hp)), _const((KV_LORA, hp))],
        out_shape=[_sds((t, d)), _sds((1, d)), _sds((d, Z_WIDTH)), _sds((1, Q_LORA)), _sds((Q_LORA, hp)),
                   _sds((1, KV_LORA)), _sds((KV_LORA, hp)), _sds((KV_LORA, hp))],
        compiler_params=_params(),
    )(x, posb, w['ab_norm'], w['W_in'], w['ab_q_norm'], w['Wq'], w['ab_kv_norm'], w['Wk'], w['Wv'], dq, dk, dv, dxl, dgate, dres)


def _attn_probs(q_blk, k_ext, i, tq):
    ext = k_ext.shape[0]
    s = lax.dot_general(q_blk, k_ext, (((1,), (1,)), ((), ())), preferred_element_type=F32) * ATTN_SCALE
    row = lax.broadcasted_iota(jnp.int32, (tq, ext), 0) + i * tq
    col = lax.broadcasted_iota(jnp.int32, (tq, ext), 1)
    s = jnp.where(col <= row, s, -1e30)
    p = jnp.exp(s - jnp.max(s, axis=1, keepdims=True))
    return p / jnp.sum(p, axis=1, keepdims=True)


def _attn_fwd(q, k, v, tq):
    b, s, hp = q.shape
    blk = pl.BlockSpec((1, s, HEAD_PAD), lambda bi, h: (bi, 0, h))

    def body(q_ref, k_ref, v_ref, o_ref):
        kb = _bf(k_ref[0])
        vb = _bf(v_ref[0])
        for i in range(s // tq):
            ext = (i + 1) * tq
            p = _attn_probs(_bf(q_ref[0, i * tq:ext, :]), kb[:ext], i, tq)
            o_ref[0, i * tq:ext, :] = lax.dot_general(_bf(p), vb[:ext], (((1,), (0,)), ((), ())), preferred_element_type=F32)

    return pl.pallas_call(body, name="attn_fwd", grid=(b, HEADS), in_specs=[blk, blk, blk], out_specs=blk,
                          out_shape=_sds((b, s, hp)), compiler_params=_params(2))(q, k, v)


def _attn_bwd(q, k, v, do, tq):
    b, s, hp = q.shape
    blk = pl.BlockSpec((1, s, HEAD_PAD), lambda bi, h: (bi, 0, h))

    def body(q_ref, k_ref, v_ref, do_ref, dq_ref, dk_ref, dv_ref):
        kb = _bf(k_ref[0])
        vb = _bf(v_ref[0])
        dk_ref[...] = jnp.zeros_like(dk_ref)
        dv_ref[...] = jnp.zeros_like(dv_ref)
        for i in range(s // tq):
            ext = (i + 1) * tq
            qb = _bf(q_ref[0, i * tq:ext, :])
            dob = _bf(do_ref[0, i * tq:ext, :])
            p = _attn_probs(qb, kb[:ext], i, tq)
            dv_ref[0, :ext, :] += lax.dot_general(_bf(p), dob, (((0,), (0,)), ((), ())), preferred_element_type=F32)
            dp = lax.dot_general(dob, vb[:ext], (((1,), (1,)), ((), ())), preferred_element_type=F32)
            ds = _bf(p * (dp - jnp.sum(p * dp, axis=1, keepdims=True)) * ATTN_SCALE)
            dq_ref[0, i * tq:ext, :] = lax.dot_general(ds, kb[:ext], (((1,), (0,)), ((), ())), preferred_element_type=F32)
            dk_ref[0, :ext, :] += lax.dot_general(ds, qb, (((0,), (0,)), ((), ())), preferred_element_type=F32)

    return pl.pallas_call(body, name="attn_bwd", grid=(b, HEADS), in_specs=[blk, blk, blk, blk], out_specs=[blk, blk, blk],
                          out_shape=[_sds((b, s, hp))] * 3, compiler_params=_params(2))(q, k, v, do)


LRU_CONV = 4


def _lru_point(pre_a, pre_x, xc, lam):
    r = jax.nn.sigmoid(pre_a)
    i = jax.nn.sigmoid(pre_x)
    log_a = -LRU_C * r * _softplus(-lam)
    return jnp.exp(log_a), jnp.sqrt(_one_minus_exp(2.0 * log_a)) * (i * xc)


def _causal_conv(pad_ref, x, halo, first_in_seq, w, taps):
    tm = x.shape[0]
    pad_ref[:HALO, :] = jnp.where(first_in_seq, 0.0, halo)
    pad_ref[HALO:, :] = x
    y = w[taps - 1:taps, :] * x
    for k in range(taps - 1):
        off = HALO - (taps - 1) + k
        y = y + w[k:k + 1, :] * pad_ref[off:off + tm, :]
    return y


def _causal_conv_wgrad(pad_ref, dy, taps):
    tm = dy.shape[0]
    return jnp.concatenate([_colsum(dy * pad_ref[HALO - (taps - 1) + k:HALO - (taps - 1) + k + tm, :]) for k in range(taps)], axis=0)


def _causal_conv_transpose(pad_ref, dy, halo_next, last_in_seq, w, taps):
    tm = dy.shape[0]
    pad_ref[:tm, :] = dy
    pad_ref[tm:, :] = jnp.where(last_in_seq, 0.0, halo_next)
    dx = w[taps - 1:taps, :] * dy
    for k in range(taps - 1):
        off = (taps - 1) - k
        dx = dx + w[k:k + 1, :] * pad_ref[off:off + tm, :]
    return dx


def _lru_fwd(xl, gate, w, ts, seq):
    t, n = xl.shape
    tiles_per_seq = seq // ts

    def body(xl_ref, halo_ref, gate_ref, cw_ref, cb_ref, wa_ref, ba_ref, wx_ref, bx_ref, lam_ref, y_out, h_out, pad_ref, a_ref, b_ref, carry_ref):
        first_in_seq = pl.program_id(0) % tiles_per_seq == 0
        xc = _causal_conv(pad_ref, xl_ref[...], halo_ref[...], first_in_seq, cw_ref[...], LRU_CONV) + cb_ref[...]
        a, bx = _lru_point(_nn(xc, wa_ref[...]) + ba_ref[...], _nn(xc, wx_ref[...]) + bx_ref[...], xc, lam_ref[...])
        a_ref[...] = a
        b_ref[...] = bx

        @pl.when(first_in_seq)
        def _():
            carry_ref[...] = jnp.zeros_like(carry_ref)

        def step(r, h):
            h = a_ref[pl.ds(r, 1), :] * h + b_ref[pl.ds(r, 1), :]
            h_out[pl.ds(r, 1), :] = h
            return h

        carry_ref[...] = lax.fori_loop(0, ts, step, carry_ref[...], unroll=8)
        y_out[...] = h_out[...] * _gelu(gate_ref[...])

    return pl.pallas_call(
        body, name="lru_fwd", grid=(t // ts,),
        in_specs=[_row(ts, n), _prev_halo(ts, n), _row(ts, n), _const((LRU_CONV, n)), _const((1, n)), _const((n, n)), _const((1, n)),
                  _const((n, n)), _const((1, n)), _const((1, n))],
        out_specs=[_row(ts, n), _row(ts, n)], out_shape=[_sds((t, n)), _sds((t, n))],
        scratch_shapes=[pltpu.VMEM((HALO + ts, n), F32), pltpu.VMEM((ts, n), F32), pltpu.VMEM((ts, n), F32), pltpu.VMEM((1, n), F32)],
        compiler_params=_params(),
    )(xl, xl, gate, w['ab_conv_w'], w['ab_conv_b'], w['Wa'], w['ab_b_rg_a'], w['Wx'], w['ab_b_rg_x'], w['ab_lambda'])


def _lru_bwd(xl, gate, hs, dy, w, ts, seq):
    t, n = xl.shape
    tiles_per_seq = seq // ts
    n_tiles = t // ts

    def rev(i):
        return n_tiles - 1 - i

    row = pl.BlockSpec((ts, n), lambda i: (rev(i), 0))
    prev = pl.BlockSpec((HALO, n), lambda i: (jnp.maximum(rev(i) * (ts // HALO) - 1, 0), 0))
    acc = lambda shape: pl.BlockSpec(shape, lambda i: (0,) * len(shape))

    def body(xl_ref, xhalo_ref, gate_ref, h_ref, hhalo_ref, dy_ref, cw_ref, cb_ref, wa_ref, ba_ref, wx_ref, bx_ref, lam_ref,
             dxl_out, dgate_out, dcw_out, dcb_out, dwa_out, dba_out, dwx_out, dbx_out, dlam_out,
             pad_ref, padh_ref, padd_ref, a_ref, g_ref, carry_ref, dhalo_ref):
        step_id = pl.program_id(0)
        first = step_id == 0
        tile = rev(step_id)
        first_in_seq = tile % tiles_per_seq == 0
        last_in_seq = tile % tiles_per_seq == tiles_per_seq - 1
        cw = cw_ref[...]
        xc = _causal_conv(pad_ref, xl_ref[...], xhalo_ref[...], first_in_seq, cw, LRU_CONV) + cb_ref[...]
        pre_a = _nn(xc, wa_ref[...]) + ba_ref[...]
        pre_x = _nn(xc, wx_ref[...]) + bx_ref[...]
        (a, _), vjp_point = jax.vjp(_lru_point, pre_a, pre_x, xc, lam_ref[...])
        h = h_ref[...]
        _, vjp_out = jax.vjp(lambda h_, g_: h_ * _gelu(g_), h, gate_ref[...])
        dh, dgate = vjp_out(dy_ref[...])
        dgate_out[...] = dgate
        a_ref[...] = a
        g_ref[...] = dh

        @pl.when(last_in_seq)
        def _():
            carry_ref[...] = jnp.zeros_like(carry_ref)

        def step(j, c):
            r = ts - 1 - j
            g = g_ref[pl.ds(r, 1), :] + c
            g_ref[pl.ds(r, 1), :] = g
            return a_ref[pl.ds(r, 1), :] * g

        carry_ref[...] = lax.fori_loop(0, ts, step, carry_ref[...], unroll=8)
        g = g_ref[...]
        padh_ref[:HALO, :] = jnp.where(first_in_seq, 0.0, hhalo_ref[...])
        padh_ref[HALO:, :] = h
        dpre_a, dpre_x, dxc, dlam = vjp_point((g * padh_ref[HALO - 1:HALO - 1 + ts, :], g))
        dxc = dxc + _nt(dpre_a, wa_ref[...]) + _nt(dpre_x, wx_ref[...])
        _accumulate(dwa_out, _tn(xc, dpre_a), first)
        _accumulate(dwx_out, _tn(xc, dpre_x), first)
        _accumulate(dba_out, _colsum(dpre_a), first)
        _accumulate(dbx_out, _colsum(dpre_x), first)
        _accumulate(dlam_out, dlam, first)
        _accumulate(dcb_out, _colsum(dxc), first)
        _accumulate(dcw_out, _causal_conv_wgrad(pad_ref, dxc, LRU_CONV), first)
        dxl_out[...] = _causal_conv_transpose(padd_ref, dxc, dhalo_ref[...], last_in_seq, cw, LRU_CONV)
        dhalo_ref[...] = dxc[:HALO, :]

    return pl.pallas_call(
        body, name="lru_bwd", grid=(n_tiles,),
        in_specs=[row, prev, row, row, prev, row, _const((LRU_CONV, n)), _const((1, n)), _const((n, n)), _const((1, n)),
                  _const((n, n)), _const((1, n)), _const((1, n))],
        out_specs=[row, row, acc((LRU_CONV, n)), acc((1, n)), acc((n, n)), acc((1, n)), acc((n, n)), acc((1, n)), acc((1, n))],
        out_shape=[_sds((t, n)), _sds((t, n)), _sds((LRU_CONV, n)), _sds((1, n)), _sds((n, n)), _sds((1, n)), _sds((n, n)),
                   _sds((1, n)), _sds((1, n))],
        scratch_shapes=[pltpu.VMEM((HALO + ts, n), F32), pltpu.VMEM((HALO + ts, n), F32), pltpu.VMEM((ts + HALO, n), F32),
                        pltpu.VMEM((ts, n), F32), pltpu.VMEM((ts, n), F32), pltpu.VMEM((1, n), F32), pltpu.VMEM((HALO, n), F32)],
        compiler_params=_params(),
    )(xl, xl, gate, hs, hs, dy, w['ab_conv_w'], w['ab_conv_b'], w['Wa'], w['ab_b_rg_a'], w['Wx'], w['ab_b_rg_x'], w['ab_lambda'])


def _ab_out_fwd(x, o, y, w, tm):
    t, d = x.shape
    hp = o.shape[1]

    def body(x_ref, o_ref, y_ref, wa_ref, wb_ref, h_out):
        h_out[...] = x_ref[...] + _nn(o_ref[...], wa_ref[...]) + _nn(y_ref[...], wb_ref[...])

    return pl.pallas_call(body, name="ab_out_fwd", grid=(t // tm,),
                          in_specs=[_row(tm, d), _row(tm, hp), _row(tm, LRU_W), _const((hp, d)), _const((LRU_W, d))],
                          out_specs=_row(tm, d), out_shape=_sds((t, d)), compiler_params=_params())(x, o, y, w['Wo_a'], w['Wo_b'])


def _ab_out_bwd(o, y, dh, w, tm):
    t, d = dh.shape
    hp = o.shape[1]

    def body(o_ref, y_ref, dh_ref, wa_ref, wb_ref, do_out, dy_out, dwa_out, dwb_out):
        first = pl.program_id(0) == 0
        dh_t = dh_ref[...]
        do_out[...] = _nt(dh_t, wa_ref[...])
        dy_out[...] = _nt(dh_t, wb_ref[...])
        _accumulate(dwa_out, _tn(o_ref[...], dh_t), first)
        _accumulate(dwb_out, _tn(y_ref[...], dh_t), first)

    return pl.pallas_call(body, name="ab_out_bwd", grid=(t // tm,),
                          in_specs=[_row(tm, hp), _row(tm, LRU_W), _row(tm, d), _const((hp, d)), _const((LRU_W, d))],
                          out_specs=[_row(tm, hp), _row(tm, LRU_W), _const((hp, d)), _const((LRU_W, d))],
                          out_shape=[_sds((t, hp)), _sds((t, LRU_W)), _sds((hp, d)), _sds((LRU_W, d))],
                          compiler_params=_params())(o, y, dh, w['Wo_a'], w['Wo_b'])


FFN_CONV = 3


def _ffn_a_fwd(h, norm, wg, wu, tm):
    t, d = h.shape
    fb = D_FF // FF_BLOCKS

    def body(h_ref, gn_ref, wg_ref, wu_ref, g_out, u_out):
        hn = _rms(h_ref[...], gn_ref[...])
        g_out[...] = _nn(hn, wg_ref[...])
        u_out[...] = _nn(hn, wu_ref[...])

    wspec = pl.BlockSpec((d, fb), lambda f, i: (0, f))
    ospec = pl.BlockSpec((tm, fb), lambda f, i: (i, f))
    return pl.pallas_call(body, name="ffn_a_fwd", grid=(FF_BLOCKS, t // tm),
                          in_specs=[pl.BlockSpec((tm, d), lambda f, i: (i, 0)), pl.BlockSpec((1, d), lambda f, i: (0, 0)), wspec, wspec],
                          out_specs=[ospec, ospec], out_shape=[_sds((t, D_FF)), _sds((t, D_FF))],
                          compiler_params=_params(2))(h, norm, wg, wu)


def _ffn_b_fwd(g, u, h, cw, cb, wd, tm, seq):
    t, d = h.shape
    tiles_per_seq = seq // tm

    def body(g_ref, halo_ref, u_ref, h_ref, cw_ref, cb_ref, wd_ref, h_out, pad_ref):
        first_in_seq = pl.program_id(0) % tiles_per_seq == 0
        gc = _causal_conv(pad_ref, g_ref[...], halo_ref[...], first_in_seq, cw_ref[...], FFN_CONV) + cb_ref[...]
        h_out[...] = h_ref[...] + _nn(_gelu(gc) * u_ref[...], wd_ref[...])

    return pl.pallas_call(body, name="ffn_b_fwd", grid=(t // tm,),
                          in_specs=[_row(tm, D_FF), _prev_halo(tm, D_FF), _row(tm, D_FF), _row(tm, d), _const((FFN_CONV, D_FF)),
                                    _const((1, D_FF)), _const((D_FF, d))],
                          out_specs=_row(tm, d), out_shape=_sds((t, d)),
                          scratch_shapes=[pltpu.VMEM((HALO + tm, D_FF), F32)], compiler_params=_params())(g, g, u, h, cw, cb, wd)


def _ffn_b_bwd(g, u, dout, cw, cb, wd, tm, seq):
    t, d = dout.shape
    fb = D_FF // FF_BLOCKS
    tiles_per_seq = seq // tm

    def body(g_ref, halo_ref, u_ref, dout_ref, cw_ref, cb_ref, wd_ref, dgc_out, du_out, dwd_out, dcw_out, dcb_out, pad_ref):
        i = pl.program_id(1)
        first = i == 0
        gc = _causal_conv(pad_ref, g_ref[...], halo_ref[...], i % tiles_per_seq == 0, cw_ref[...], FFN_CONV) + cb_ref[...]
        act, vjp_act = jax.vjp(lambda gc_, u_: _gelu(gc_) * u_, gc, u_ref[...])
        dout_t = dout_ref[...]
        dgc, du = vjp_act(_nt(dout_t, wd_ref[...]))
        dgc_out[...] = dgc
        du_out[...] = du
        _accumulate(dwd_out, _tn(act, dout_t), first)
        _accumulate(dcb_out, _colsum(dgc), first)
        _accumulate(dcw_out, _causal_conv_wgrad(pad_ref, dgc, FFN_CONV), first)

    blk = pl.BlockSpec((tm, fb), lambda f, i: (i, f))
    halo = pl.BlockSpec((HALO, fb), lambda f, i: (jnp.maximum(i * (tm // HALO) - 1, 0), f))
    return pl.pallas_call(
        body, name="ffn_b_bwd", grid=(FF_BLOCKS, t // tm),
        in_specs=[blk, halo, blk, pl.BlockSpec((tm, d), lambda f, i: (i, 0)), pl.BlockSpec((FFN_CONV, fb), lambda f, i: (0, f)),
                  pl.BlockSpec((1, fb), lambda f, i: (0, f)), pl.BlockSpec((fb, d), lambda f, i: (f, 0))],
        out_specs=[blk, blk, pl.BlockSpec((fb, d), lambda f, i: (f, 0)), pl.BlockSpec((FFN_CONV, fb), lambda f, i: (0, f)),
                   pl.BlockSpec((1, fb), lambda f, i: (0, f))],
        out_shape=[_sds((t, D_FF)), _sds((t, D_FF)), _sds((D_FF, d)), _sds((FFN_CONV, D_FF)), _sds((1, D_FF))],
        scratch_shapes=[pltpu.VMEM((HALO + tm, fb), F32)], compiler_params=_params(2))(g, g, u, dout, cw, cb, wd)


def _ffn_a_dgrad(h, norm, dgc, du, dres, cw, wg, wu, tm, seq):
    t, d = h.shape
    tiles_per_seq = seq // tm
    n_tiles = t // tm

    def body(h_ref, gn_ref, dgc_ref, halo_ref, du_ref, dres_ref, cw_ref, wg_ref, wu_ref, dh_out, dg_out, dgn_out, pad_ref):
        i = pl.program_id(0)
        last_in_seq = i % tiles_per_seq == tiles_per_seq - 1
        dg = _causal_conv_transpose(pad_ref, dgc_ref[...], halo_ref[...], last_in_seq, cw_ref[...], FFN_CONV)
        dg_out[...] = dg
        _, vjp_norm = jax.vjp(_rms, h_ref[...], gn_ref[...])
        dh, dgn = vjp_norm(_nt(dg, wg_ref[...]) + _nt(du_ref[...], wu_ref[...]))
        dh_out[...] = dh + dres_ref[...]
        _accumulate(dgn_out, dgn, i == 0)

    return pl.pallas_call(
        body, name="ffn_a_dgrad", grid=(n_tiles,),
        in_specs=[_row(tm, d), _const((1, d)), _row(tm, D_FF), _next_halo(tm, D_FF, n_tiles), _row(tm, D_FF), _row(tm, d),
                  _const((FFN_CONV, D_FF)), _const((d, D_FF)), _const((d, D_FF))],
        out_specs=[_row(tm, d), _row(tm, D_FF), _const((1, d))], out_shape=[_sds((t, d)), _sds((t, D_FF)), _sds((1, d))],
        scratch_shapes=[pltpu.VMEM((tm + HALO, D_FF), F32)], compiler_params=_params())(h, norm, dgc, dgc, du, dres, cw, wg, wu)


def _ffn_a_wgrad(h, norm, dg, du, tm):
    t, d = h.shape
    fb = D_FF // FF_BLOCKS

    def body(h_ref, gn_ref, dg_ref, du_ref, dwg_out, dwu_out):
        first = pl.program_id(1) == 0
        hn = _rms(h_ref[...], gn_ref[...])
        _accumulate(dwg_out, _tn(hn, dg_ref[...]), first)
        _accumulate(dwu_out, _tn(hn, du_ref[...]), first)

    blk = pl.BlockSpec((tm, fb), lambda f, i: (i, f))
    wspec = pl.BlockSpec((d, fb), lambda f, i: (0, f))
    return pl.pallas_call(body, name="ffn_a_wgrad", grid=(FF_BLOCKS, t // tm),
                          in_specs=[pl.BlockSpec((tm, d), lambda f, i: (i, 0)), pl.BlockSpec((1, d), lambda f, i: (0, 0)), blk, blk],
                          out_specs=[wspec, wspec], out_shape=[_sds((d, D_FF)), _sds((d, D_FF))],
                          compiler_params=_params(2))(h, norm, dg, du)


def _sgu_mix(vn, ws_ref, bst):
    tril = lax.broadcasted_iota(jnp.int32, (CHUNK, CHUNK), 0) >= lax.broadcasted_iota(jnp.int32, (CHUNK, CHUNK), 1)
    wms = [jnp.where(tril, ws_ref[g], 0.0) for g in range(SGU_GROUPS)]
    chunks = []
    for n in range(vn.shape[0] // CHUNK):
        vc = vn[n * CHUNK:(n + 1) * CHUNK, :]
        chunks.append(jnp.concatenate(
            [_nn(wms[g], vc[:, g * CHUNK:(g + 1) * CHUNK]) + bst[:, g:g + 1] for g in range(SGU_GROUPS)], axis=1))
    return jnp.concatenate(chunks, axis=0)


def _sgu_fwd(h, w, tm):
    t, d = h.shape

    def body(h_ref, cn_ref, win_ref, lg_ref, lb_ref, ws_ref, bst_ref, wout_ref, h_out):
        h_t = h_ref[...]
        z = _gelu(_nn(_rms(h_t, cn_ref[...]), win_ref[...]))
        vn = _layer_norm(z[:, d:], lg_ref[...], lb_ref[...])
        s = _sgu_mix(vn, ws_ref, bst_ref[...])
        h_out[...] = h_t + _nn(z[:, :d] * s, wout_ref[...])

    return pl.pallas_call(
        body, name="sgu_fwd", grid=(t // tm,),
        in_specs=[_row(tm, d), _const((1, d)), _const((d, 2 * d)), _const((1, d)), _const((1, d)), _const((SGU_GROUPS, CHUNK, CHUNK)),
                  _const((CHUNK, LANES)), _const((d, d))],
        out_specs=_row(tm, d), out_shape=_sds((t, d)), compiler_params=_params(),
    )(h, w['c_norm'], w['c_w_in'], w['c_ln_g'], w['c_ln_b'], w['c_w_s'], w['bsT'], w['c_w_out'])


def _sgu_bwd(h, dout, w, tm):
    t, d = h.shape

    def body(h_ref, dout_ref, cn_ref, win_ref, lg_ref, lb_ref, ws_ref, bst_ref, wout_ref,
             dh_out, dcn_out, dwin_out, dlg_out, dlb_out, dws_out, dbst_out, dwout_out):
        first = pl.program_id(0) == 0
        hn, vjp_norm = jax.vjp(_rms, h_ref[...], cn_ref[...])
        zpre = _nn(hn, win_ref[...])
        u, vjp_u = jax.vjp(_gelu, zpre[:, :d])
        vn, vjp_v = jax.vjp(lambda zp, lg, lb: _layer_norm(_gelu(zp), lg, lb), zpre[:, d:], lg_ref[...], lb_ref[...])
        s = _sgu_mix(vn, ws_ref, bst_ref[...])
        dout_t = dout_ref[...]
        dus = _nt(dout_t, wout_ref[...])
        _accumulate(dwout_out, _tn(u * s, dout_t), first)
        ds = dus * u
        tril = lax.broadcasted_iota(jnp.int32, (CHUNK, CHUNK), 0) >= lax.broadcasted_iota(jnp.int32, (CHUNK, CHUNK), 1)
        lane = lax.broadcasted_iota(jnp.int32, (CHUNK, LANES), 1)
        dws = [jnp.zeros((CHUNK, CHUNK), F32) for _ in range(SGU_GROUPS)]
        dbst = jnp.zeros((CHUNK, LANES), F32)
        dvn_chunks = []
        for n in range(tm // CHUNK):
            cols = []
            for g in range(SGU_GROUPS):
                ds_ng = ds[n * CHUNK:(n + 1) * CHUNK, g * CHUNK:(g + 1) * CHUNK]
                vc_ng = vn[n * CHUNK:(n + 1) * CHUNK, g * CHUNK:(g + 1) * CHUNK]
                cols.append(_tn(jnp.where(tril, ws_ref[g], 0.0), ds_ng))
                dws[g] = dws[g] + _nt(ds_ng, vc_ng)
                dbst = dbst + jnp.where(lane == g, jnp.sum(ds_ng, axis=1, keepdims=True), 0.0)
            dvn_chunks.append(jnp.concatenate(cols, axis=1))
        dvn = jnp.concatenate(dvn_chunks, axis=0)
        for g in range(SGU_GROUPS):
            val = jnp.where(tril, dws[g], 0.0)

            @pl.when(first)
            def _():
                dws_out[g] = val

            @pl.when(jnp.logical_not(first))
            def _():
                dws_out[g] += val
        _accumulate(dbst_out, dbst, first)
        (dzu,) = vjp_u(dus * s)
        dzv, dlg, dlb = vjp_v(dvn)
        _accumulate(dlg_out, dlg, first)
        _accumulate(dlb_out, dlb, first)
        dzpre = jnp.concatenate([dzu, dzv], axis=1)
        _accumulate(dwin_out, _tn(hn, dzpre), first)
        dh, dcn = vjp_norm(_nt(dzpre, win_ref[...]))
        _accumulate(dcn_out, dcn, first)
        dh_out[...] = dh + dout_t

    return pl.pallas_call(
        body, name="sgu_bwd", grid=(t // tm,),
        in_specs=[_row(tm, d), _row(tm, d), _const((1, d)), _const((d, 2 * d)), _const((1, d)), _const((1, d)),
                  _const((SGU_GROUPS, CHUNK, CHUNK)), _const((CHUNK, LANES)), _const((d, d))],
        out_specs=[_row(tm, d), _const((1, d)), _const((d, 2 * d)), _const((1, d)), _const((1, d)), _const((SGU_GROUPS, CHUNK, CHUNK)),
                   _const((CHUNK, LANES)), _const((d, d))],
        out_shape=[_sds((t, d)), _sds((1, d)), _sds((d, 2 * d)), _sds((1, d)), _sds((1, d)), _sds((SGU_GROUPS, CHUNK, CHUNK)),
                   _sds((CHUNK, LANES)), _sds((d, d))],
        compiler_params=_params(),
    )(h, dout, w['c_norm'], w['c_w_in'], w['c_ln_g'], w['c_ln_b'], w['c_w_s'], w['bsT'], w['c_w_out'])


def _final_loss(h, target, norm, tm):
    t, d = h.shape

    def body(h_ref, tgt_ref, gn_ref, dh_out, loss_out, dgn_out):
        first = pl.program_id(0) == 0
        tgt = tgt_ref[...]

        def loss_fn(h_, g_):
            err = _rms(h_, g_) - tgt
            return 0.5 * jnp.sum(jnp.mean(err * err, axis=-1, keepdims=True), axis=0, keepdims=True)

        loss, vjp_loss = jax.vjp(loss_fn, h_ref[...], gn_ref[...])
        dh, dgn = vjp_loss(jnp.ones((1, 1), F32))
        dh_out[...] = dh
        _accumulate(loss_out, loss, first)
        _accumulate(dgn_out, dgn, first)

    return pl.pallas_call(body, name="final_loss", grid=(t // tm,), in_specs=[_row(tm, d), _row(tm, d), _const((1, d))],
                          out_specs=[_row(tm, d), _const((1, 1)), _const((1, d))],
                          out_shape=[_sds((t, d)), _sds((1, 1)), _sds((1, d))], compiler_params=_params())(h, target, norm)


def _tile(t, seq, want):
    tm = min(want, seq)
    assert seq % tm == 0 and t % tm == 0 and tm % CHUNK == 0
    return tm


def _local_step(x, posb, target, w, seq, late_weights, on_late_grads):
    t, d = x.shape
    b = t // seq
    hp = HEADS * HEAD_PAD
    tm_big, tm_mid = _tile(t, seq, 512), _tile(t, seq, 256)
    tq = _tile(t, seq, 512)

    q, k, v, xl, gate = _ab_in_fwd(x, posb, w, tm_big)
    o = _attn_fwd(q.reshape(b, seq, hp), k.reshape(b, seq, hp), v.reshape(b, seq, hp), tq).reshape(t, hp)
    y, hs = _lru_fwd(xl, gate, w, tm_big, seq)
    h1 = _ab_out_fwd(x, o, y, w, tm_big)
    hcur = h1
    saved = []
    for l in range(2):
        if l == 1:
            w = {**w, **late_weights('mix1', hcur)}
            saved_h2 = hcur
            hcur = _sgu_fwd(hcur, w, tm_mid)
        wl = late_weights('ffn%d' % l, hcur)
        g, u = _ffn_a_fwd(hcur, w['ffn_norm'][l], wl['Wg'], wl['Wu'], tm_big)
        hnext = _ffn_b_fwd(g, u, hcur, w['ffn_conv_w'][l], w['ffn_conv_b'][l], wl['Wd'], tm_mid, seq)
        saved.append((hcur, g, u, wl))
        hcur = hnext
    dh, loss, d_final = _final_loss(hcur, target, w['final_norm'], tm_big)

    grads = {'final_norm': d_final}
    late = {}
    ffn = {}
    for l in (1, 0):
        hin, g, u, wl = saved[l]
        dgc, du, d_wd, d_cw, d_cb = _ffn_b_bwd(g, u, dh, w['ffn_conv_w'][l], w['ffn_conv_b'][l], wl['Wd'], tm_mid, seq)
        dh, dg, d_norm = _ffn_a_dgrad(hin, w['ffn_norm'][l], dgc, du, dh, w['ffn_conv_w'][l], wl['Wg'], wl['Wu'], tm_mid, seq)
        d_wg, d_wu = _ffn_a_wgrad(hin, w['ffn_norm'][l], dg, du, tm_big)
        ffn[l] = dict(ffn_norm=d_norm, ffn_conv_w=d_cw, ffn_conv_b=d_cb, Wg=d_wg, Wu=d_wu, Wd=d_wd)
        if l == 1:
            dh, d_cn, d_cwin, d_lg, d_lb, d_ws, d_bst, d_cwout = _sgu_bwd(saved_h2, dh, w, tm_mid)
            grads.update(c_norm=d_cn, c_ln_g=d_lg, c_ln_b=d_lb, c_w_s=d_ws, bsT=d_bst)
            late.update(c_w_in=d_cwin, c_w_out=d_cwout)
    for name in ('ffn_norm', 'ffn_conv_w', 'ffn_conv_b'):
        grads[name] = [ffn[0][name], ffn[1][name]]
    for name in ('Wg', 'Wu', 'Wd'):
        late[name] = [ffn[0][name], ffn[1][name]]
    zero = on_late_grads(late)
    w = {**w, 'Wo_b': w['Wo_b'] + zero.astype(w['Wo_b'].dtype)}
    do, dy, d_woa, d_wob = _ab_out_bwd(o, y, dh, w, tm_big)
    dxl, dgate, d_cw, d_cb, d_wa, d_ba, d_wx, d_bx, d_lam = _lru_bwd(xl, gate, hs, dy, w, tm_big, seq)
    dq, dk, dv = _attn_bwd(q.reshape(b, seq, hp), k.reshape(b, seq, hp), v.reshape(b, seq, hp), do.reshape(b, seq, hp), tq)
    dx, d_gn, d_win, d_qn, d_wq, d_kvn, d_wk, d_wv = _ab_in_bwd(
        x, posb, w, dq.reshape(t, hp), dk.reshape(t, hp), dv.reshape(t, hp), dxl, dgate, dh, tm_mid)
    grads.update(Wo_a=d_woa, Wo_b=d_wob, ab_conv_w=d_cw, ab_conv_b=d_cb, Wa=d_wa, ab_b_rg_a=d_ba, Wx=d_wx, ab_b_rg_x=d_bx,
                 ab_lambda=d_lam, ab_norm=d_gn, W_in=d_win, ab_q_norm=d_qn, Wq=d_wq, ab_kv_norm=d_kvn, Wk=d_wk, Wv=d_wv)
    return loss, dx, grads


def _block_diag(wg):
    g, n, _ = wg.shape
    return jnp.einsum('gij,gh->gihj', wg, jnp.eye(g, dtype=wg.dtype)).reshape(g * n, g * n)


def _prepare(full):
    d = full['ab_w_in'].shape[1]
    w_in = full['ab_w_in'][0]
    zeros = lambda n: jnp.zeros((d, n), w_in.dtype)
    wq = full['ab_w_q_b'][0].reshape(Q_LORA, HEADS, QK_NOPE + QK_ROPE)
    wkv = full['ab_w_kv_b'][0].reshape(KV_LORA, HEADS, 2 * QK_NOPE)
    pad_head = lambda a: jnp.pad(a, ((0, 0), (0, 0), (0, HEAD_PAD - a.shape[2]))).reshape(a.shape[0], HEADS * HEAD_PAD)
    w_out = full['ab_w_out'][0]
    mla = HEADS * QK_NOPE
    w = {
        'W_in': jnp.concatenate([w_in[:, :Z_KPE], zeros(QK_NOPE), w_in[:, Z_KPE:Z_KPE + QK_ROPE],
                                 zeros(HEAD_PAD - QK_NOPE - QK_ROPE), w_in[:, Z_KPE + QK_ROPE:]], axis=1),
        'Wq': pad_head(wq), 'Wk': pad_head(wkv[:, :, :QK_NOPE]), 'Wv': pad_head(wkv[:, :, QK_NOPE:]),
        'Wo_a': jnp.pad(w_out[:mla].reshape(HEADS, QK_NOPE, d), ((0, 0), (0, HEAD_PAD - QK_NOPE), (0, 0))).reshape(HEADS * HEAD_PAD, d),
        'Wo_b': w_out[mla:],
        'Wa': _bf(_block_diag(full['ab_w_rg_a'][0])), 'Wx': _bf(_block_diag(full['ab_w_rg_x'][0])),
        'c_w_s': full['c_w_s'][0],
        'bsT': jnp.pad(full['c_b_s'][0].T, ((0, 0), (0, LANES - SGU_GROUPS))),
        'ffn_norm': [full['ffn_norm'][l:l + 1] for l in range(2)], 'ffn_conv_w': [full['ffn_conv_w'][l] for l in range(2)],
        'ffn_conv_b': [full['ffn_conv_b'][l:l + 1] for l in range(2)],
        'ab_conv_w': full['ab_conv_w'][0], 'final_norm': full['final_norm'][None, :],
    }
    for name in ('ab_norm', 'ab_q_norm', 'ab_kv_norm', 'ab_conv_b', 'ab_b_rg_a', 'ab_b_rg_x', 'ab_lambda', 'c_norm', 'c_ln_g', 'c_ln_b'):
        w[name] = full[name]
    return w


def _unprepare(g):
    d = g['W_in'].shape[0]
    unpad_head = lambda a, n: a.reshape(a.shape[0], HEADS, HEAD_PAD)[:, :, :n]
    d_win = g['W_in']
    diag = lambda a: jnp.einsum('gigj->gij', a.reshape(HEADS, LRU_W // HEADS, HEADS, LRU_W // HEADS))
    out = {
        'ab_w_in': jnp.concatenate([d_win[:, :Z_KPE], d_win[:, Z_KPE + QK_NOPE:Z_KPE + QK_NOPE + QK_ROPE], d_win[:, Z_LRU:]], axis=1)[None],
        'ab_w_q_b': unpad_head(g['Wq'], QK_NOPE + QK_ROPE).reshape(1, Q_LORA, -1),
        'ab_w_kv_b': jnp.concatenate([unpad_head(g['Wk'], QK_NOPE), unpad_head(g['Wv'], QK_NOPE)], axis=2).reshape(1, KV_LORA, -1),
        'ab_w_out': jnp.concatenate([g['Wo_a'].reshape(HEADS, HEAD_PAD, d)[:, :QK_NOPE].reshape(HEADS * QK_NOPE, d), g['Wo_b']], axis=0)[None],
        'ab_w_rg_a': diag(g['Wa'])[None], 'ab_w_rg_x': diag(g['Wx'])[None],
        'c_w_s': g['c_w_s'][None],
        'c_b_s': g['bsT'][:, :SGU_GROUPS].T[None],
        'ffn_norm': jnp.concatenate(g['ffn_norm'], axis=0), 'ffn_conv_w': jnp.stack(g['ffn_conv_w']),
        'ffn_conv_b': jnp.concatenate(g['ffn_conv_b'], axis=0),
        'ab_conv_w': g['ab_conv_w'][None], 'final_norm': g['final_norm'][0],
    }
    for name in ('ab_norm', 'ab_q_norm', 'ab_kv_norm', 'ab_conv_b', 'ab_b_rg_a', 'ab_b_rg_x', 'ab_lambda', 'c_norm', 'c_ln_g', 'c_ln_b'):
        out[name] = g[name]
    return out


SLAB_ROWS = 16


def _round_up(n, m):
    return -(-n // m) * m


def _to_chunks(full, axis):
    s = full.shape
    return jnp.moveaxis(full.reshape(s[:axis] + (N_DEV, s[axis] // N_DEV) + s[axis + 1:]), axis, 0)


def _from_chunks(chunks, axis):
    local = chunks.shape[1:]
    return jnp.moveaxis(chunks, 0, axis).reshape(local[:axis] + (N_DEV * local[axis],) + local[axis + 1:])


def _slab_rows(n):
    return _round_up(-(-n // LANES), SLAB_ROWS)


def _to_slab(a, lead):
    a = a.reshape(lead + (-1,))
    rows = _slab_rows(a.shape[-1])
    a = jnp.pad(a, [(0, 0)] * len(lead) + [(0, rows * LANES - a.shape[-1])])
    return a.reshape(lead + (rows, LANES))


def _pack_slabs(parts, lead):
    return jnp.concatenate([_to_slab(p, lead) for p in parts], axis=len(lead))


def _unpack_slabs(packed, shapes):
    lead = packed.shape[:-2]
    out, row = [], 0
    for shape in shapes:
        size = math.prod(shape)
        rows = _slab_rows(size)
        piece = lax.slice_in_dim(packed, row, row + rows, axis=len(lead))
        out.append(piece.reshape(lead + (rows * LANES,))[..., :size].reshape(lead + tuple(shape)))
        row += rows
    return out


HBM = pl.BlockSpec(memory_space=pl.ANY)


def _other_chips(x, y):
    return [(1 - x, y), (x, 1 - y), (1 - x, 1 - y)]


def _all_gather(blocks):
    n = len(blocks)

    def body(*refs):
        x_refs, out_refs = refs[:n], refs[n:2 * n]
        send_sems, recv_sems, local_sems = refs[2 * n:]
        x, y, c = lax.axis_index("x"), lax.axis_index("y"), lax.axis_index("c")
        me, sibling = (x, y, c), (x, y, 1 - c)
        chips = _other_chips(x, y)

        def slab(a, px, py, pc):
            return out_refs[a].at[4 * px + 2 * py + pc]

        def copy(a, k, blk, to, src=None):
            return pltpu.make_async_remote_copy(src_ref=slab(a, *blk) if src is None else src, dst_ref=slab(a, *blk),
                                                send_sem=send_sems.at[7 * a + k], recv_sem=recv_sems.at[7 * a + k],
                                                device_id=to, device_id_type=MESH)

        mine = [pltpu.make_async_copy(x_refs[a], slab(a, *me), local_sems.at[a]) for a in range(n)]
        started = []
        for a in range(n):
            mine[a].start()
            started.append(copy(a, 0, me, sibling, src=x_refs[a]))
            started += [copy(a, 1 + j, me, (*chip, c), src=x_refs[a]) for j, chip in enumerate(chips)]
        for cp in started:
            cp.start()
        for j, chip in enumerate(chips):
            for a in range(n):
                copy(a, 1 + j, (*chip, c), me).wait_recv()
                passed = copy(a, 4 + j, (*chip, c), sibling)
                passed.start()
                started.append(passed)
        for a in range(n):
            copy(a, 0, sibling, me).wait_recv()
        for j, chip in enumerate(chips):
            for a in range(n):
                copy(a, 4 + j, (*chip, 1 - c), me).wait_recv()
        for cp in started:
            cp.wait_send()
        for a in range(n):
            mine[a].wait()

    return pl.pallas_call(
        body, name="all_gather_weights", out_shape=[jax.ShapeDtypeStruct((N_DEV,) + b.shape, b.dtype) for b in blocks],
        in_specs=[HBM] * n, out_specs=[HBM] * n,
        scratch_shapes=[pltpu.SemaphoreType.DMA((7 * n,)), pltpu.SemaphoreType.DMA((7 * n,)), pltpu.SemaphoreType.DMA((n,))],
    )(*blocks)


FLIPS = [(0, 0, 1), (1, 0, 0), (1, 0, 1), (0, 1, 0), (0, 1, 1), (1, 1, 0), (1, 1, 1)]


def _peers(x, y, c):
    flip = lambda v, f: 1 - v if f else v
    return [(flip(x, fx), flip(y, fy), flip(c, fc)) for fx, fy, fc in FLIPS]


def _direct_copies(src_refs, land_refs, send_sems, recv_sems, scatter):
    x, y, c = lax.axis_index("x"), lax.axis_index("y"), lax.axis_index("c")
    me = 4 * x + 2 * y + c
    starts, waits = [], []
    for a in range(len(src_refs)):
        for k, (px, py, pc) in enumerate(_peers(x, y, c)):
            peer = 4 * px + 2 * py + pc
            sems = dict(send_sem=send_sems.at[7 * a + k], recv_sem=recv_sems.at[7 * a + k], device_id=(px, py, pc), device_id_type=MESH)
            src = src_refs[a].at[peer] if scatter else src_refs[a]
            starts.append(pltpu.make_async_remote_copy(src_ref=src, dst_ref=land_refs[a].at[me], **sems))
            waits.append(pltpu.make_async_remote_copy(src_ref=src, dst_ref=land_refs[a].at[peer], **sems))
    return starts, waits


def _landing(src, scatter):
    block = src.shape[1:] if scatter else src.shape
    return jax.ShapeDtypeStruct((N_DEV,) + block, src.dtype)


def _direct_exchange(name, srcs, scatter):
    n = len(srcs)

    def body(*refs):
        starts, waits = _direct_copies(refs[:n], refs[n:2 * n], refs[2 * n], refs[2 * n + 1], scatter)
        for cp in starts:
            cp.start()
        for cp in waits:
            cp.wait_recv()
        for cp in waits:
            cp.wait_send()

    return list(pl.pallas_call(body, name=name, out_shape=[_landing(s, scatter) for s in srcs], in_specs=[HBM] * n, out_specs=[HBM] * n,
                               scratch_shapes=[pltpu.SemaphoreType.DMA((7 * n,)), pltpu.SemaphoreType.DMA((7 * n,))])(*srcs))


HBM_SPACE = pl.BlockSpec(memory_space=pltpu.HBM)
SEMAPHORES = pl.BlockSpec(memory_space=pltpu.SEMAPHORE)
SPLIT_EFFECT = pltpu.SideEffectType.DATAFLOW_SIDE_EFFECTING


def _start_exchange(name, srcs, scatter):
    n = len(srcs)
    lands = [lax.empty(s.shape, s.dtype) for s in (_landing(s, scatter) for s in srcs)]

    def body(*refs):
        starts, _ = _direct_copies(refs[:n], refs[n:2 * n], refs[2 * n], refs[2 * n + 1], scatter)
        for cp in starts:
            cp.start()
        refs[-1][...] = jnp.zeros_like(refs[-1])

    held = [pltpu.with_memory_space_constraint(a, pltpu.HBM) for a in list(srcs) + lands]
    out = pl.pallas_call(
        body, name=name + "_start",
        out_shape=(pltpu.SemaphoreType.DMA((7 * n,)), pltpu.SemaphoreType.DMA((7 * n,)), *[pltpu.HBM(a.shape, a.dtype) for a in held],
                   jax.ShapeDtypeStruct((8, LANES), F32)),
        in_specs=[HBM_SPACE] * (2 * n), out_specs=(SEMAPHORES, SEMAPHORES, *[HBM_SPACE] * (2 * n), pl.BlockSpec(memory_space=pltpu.VMEM)),
        input_output_aliases={i: 2 + i for i in range(2 * n)},
        compiler_params=pltpu.CompilerParams(has_side_effects=SPLIT_EFFECT),
    )(*held)
    return out[0], out[1], list(out[2:2 + n]), list(out[2 + n:2 + 2 * n]), out[-1][0, 0]


def _wait_exchange(name, started, after, scatter):
    send_sems, recv_sems, srcs, lands, _ = started
    n = len(srcs)

    def body(*refs):
        _, waits = _direct_copies(refs[:n], refs[n:2 * n], refs[2 * n], refs[2 * n + 1], scatter)
        for cp in waits:
            cp.wait_send()
        for cp in waits:
            cp.wait_recv()

    out = pl.pallas_call(
        body, name=name + "_wait", out_shape=tuple(pltpu.HBM(a.shape, a.dtype) for a in srcs + lands),
        in_specs=[HBM_SPACE] * (2 * n) + [SEMAPHORES, SEMAPHORES, HBM], out_specs=tuple([HBM_SPACE] * (2 * n)),
        input_output_aliases={i: i for i in range(2 * n)},
        compiler_params=pltpu.CompilerParams(has_side_effects=SPLIT_EFFECT),
    )(*srcs, *lands, send_sems, recv_sems, after)
    return list(out[:n]), list(out[n:])


def _row_tile(rows):
    return rows // 2 if (rows // 2) % SLAB_ROWS == 0 else rows


def _sum_and_adamw(me, landed, own, wts, m, v, name):
    l, r, n = wts.shape
    tr = _row_tile(r)
    blk = pl.BlockSpec((1, tr, n), lambda li, ri, me_ref: (li, ri, 0))
    c1 = 1.0 / (1.0 - ADAM_B1 ** ADAM_STEP)
    c2 = 1.0 / (1.0 - ADAM_B2 ** ADAM_STEP)

    def body(me_ref, l_ref, own_ref, w_ref, m_ref, v_ref, g_out, d_out, m_out, v_out):
        mine = own_ref[0].astype(F32)
        g = jnp.where(me_ref[0] == 0, mine, l_ref[0].astype(F32))
        for dev in range(1, N_DEV):
            g = g + jnp.where(me_ref[0] == dev, mine, l_ref[dev].astype(F32))
        m_new = ADAM_B1 * m_ref[...] + (1.0 - ADAM_B1) * g
        v_new = ADAM_B2 * v_ref[...] + (1.0 - ADAM_B2) * (g * g)
        g_out[...] = g
        m_out[...] = m_new
        v_out[...] = v_new
        d_out[...] = -ADAM_LR * ((m_new * c1) / (jnp.sqrt(v_new * c2) + ADAM_EPS) + ADAM_WD * w_ref[...])

    return pl.pallas_call(
        body, name="adamw_" + name,
        grid_spec=pltpu.PrefetchScalarGridSpec(
            num_scalar_prefetch=1, grid=(l, r // tr),
            in_specs=[pl.BlockSpec((N_DEV, 1, tr, n), lambda li, ri, me_ref: (0, li, ri, 0)),
                      pl.BlockSpec((1, 1, tr, n), lambda li, ri, me_ref: (me_ref[0], li, ri, 0)), blk, blk, blk],
            out_specs=[blk] * 4),
        out_shape=[_sds((l, r, n))] * 4, compiler_params=_params(2))(me, landed, own, wts, m, v)


EARLY = ['ab_w_in', 'ab_w_out']
LATE = [n for n in BIG if n not in EARLY]
LATE_STAGES = {
    'ffn0': [('ffn_w_gate', 0, 'Wg'), ('ffn_w_up', 0, 'Wu'), ('ffn_w_down', 0, 'Wd')],
    'mix1': [('c_w_in', None, 'c_w_in'), ('c_w_out', None, 'c_w_out')],
    'ffn1': [('ffn_w_gate', 1, 'Wg'), ('ffn_w_up', 1, 'Wu'), ('ffn_w_down', 1, 'Wd')],
}


def _gather_early(local):
    small = [_bf(local[n]) if n in MATRICES else lax.bitcast_convert_type(local[n], BF16) for n in SMALL_SHARDED]
    gathered = _all_gather([_bf(local[n]) for n in EARLY] + [_pack_slabs(small, ())])
    full = {n: local[n] for n in REPLICATED}
    for n, g in zip(EARLY, gathered):
        full[n] = _from_chunks(g, SHARD_AXIS[n])
    for n, p in zip(SMALL_SHARDED, _unpack_slabs(gathered[-1], [s.shape for s in small])):
        full[n] = _from_chunks(p if n in MATRICES else lax.bitcast_convert_type(p, F32), SHARD_AXIS[n])
    return full


def kernel(x, positions, ab_norm, ab_w_in, ab_q_norm, ab_w_q_b, ab_kv_norm, ab_w_kv_b, ab_conv_w, ab_conv_b, ab_w_rg_a, ab_b_rg_a, ab_w_rg_x, ab_b_rg_x, ab_lambda, ab_w_out, c_norm, c_w_in, c_ln_g, c_ln_b, c_w_s, c_b_s, c_w_out, ffn_norm, ffn_w_gate, ffn_w_up, ffn_conv_w, ffn_conv_b, ffn_w_down, final_norm, loss_target, m_ab_norm, m_ab_w_in, m_ab_q_norm, m_ab_w_q_b, m_ab_kv_norm, m_ab_w_kv_b, m_ab_conv_w, m_ab_conv_b, m_ab_w_rg_a, m_ab_b_rg_a, m_ab_w_rg_x, m_ab_b_rg_x, m_ab_lambda, m_ab_w_out, m_c_norm, m_c_w_in, m_c_ln_g, m_c_ln_b, m_c_w_s, m_c_b_s, m_c_w_out, m_ffn_norm, m_ffn_w_gate, m_ffn_w_up, m_ffn_conv_w, m_ffn_conv_b, m_ffn_w_down, m_final_norm, v_ab_norm, v_ab_w_in, v_ab_q_norm, v_ab_w_q_b, v_ab_kv_norm, v_ab_w_kv_b, v_ab_conv_w, v_ab_conv_b, v_ab_w_rg_a, v_ab_b_rg_a, v_ab_w_rg_x, v_ab_b_rg_x, v_ab_lambda, v_ab_w_out, v_c_norm, v_c_w_in, v_c_ln_g, v_c_ln_b, v_c_w_s, v_c_b_s, v_c_w_out, v_ffn_norm, v_ffn_w_gate, v_ffn_w_up, v_ffn_conv_w, v_ffn_conv_b, v_ffn_w_down, v_final_norm):
    given = dict(locals())
    local = {n: given[n] for n in WEIGHTS}
    b, seq, d = x.shape
    t = b * seq

    me = (4 * lax.axis_index("x") + 2 * lax.axis_index("y") + lax.axis_index("c")).astype(jnp.int32)
    is_me = (jnp.arange(N_DEV, dtype=jnp.int32) == me).reshape(N_DEV, 1, 1, 1)

    full = _gather_early(local)
    gathers = {}
    zero = jnp.zeros((), F32)
    for stage, members in LATE_STAGES.items():
        srcs = [_bf(local[n] if layer is None else local[n][layer:layer + 1]) for n, layer, _ in members]
        gathers[stage] = _start_exchange('gather_' + stage, srcs, scatter=False)
        zero = zero + gathers[stage][4]
    w = _prepare(full)
    w['ab_norm'] = w['ab_norm'] + zero

    def late_weights(stage, after):
        srcs, lands = _wait_exchange('gather_' + stage, gathers[stage], after, scatter=False)
        whole = [_from_chunks(jnp.where(is_me, s[None], l), SHARD_AXIS[n]) for (n, _, _), s, l in zip(LATE_STAGES[stage], srcs, lands)]
        return {key: a[0] for (_, _, key), a in zip(LATE_STAGES[stage], whole)}

    scatters = {}

    def on_late_grads(g):
        whole = {'c_w_in': g['c_w_in'][None], 'c_w_out': g['c_w_out'][None], 'ffn_w_gate': jnp.stack(g['Wg']),
                 'ffn_w_up': jnp.stack(g['Wu']), 'ffn_w_down': jnp.stack(g['Wd'])}
        scatters['late'] = _start_exchange('scatter_late', [_bf(_to_chunks(whole[n], SHARD_AXIS[n])) for n in LATE], scatter=True)
        return scatters['late'][4]

    posb = jnp.broadcast_to(positions.astype(F32).reshape(t, 1), (t, LANES))
    loss, dx, grads = _local_step(x.reshape(t, d), posb, loss_target.reshape(t, d), w, seq, late_weights, on_late_grads)
    full_grads = _unprepare(grads)

    own_early = [_bf(_to_chunks(full_grads[n], SHARD_AXIS[n])) for n in EARLY]
    small = [_to_chunks(full_grads[n], SHARD_AXIS[n]) for n in SMALL_SHARDED]
    small += [jnp.broadcast_to(full_grads[n][None], (N_DEV,) + full_grads[n].shape) for n in REPLICATED]
    own_early.append(_bf(_pack_slabs(small, (N_DEV,)))[:, None])
    landed_early = _direct_exchange('scatter_early', own_early, scatter=True)
    own_late, landed_late = _wait_exchange('scatter_late', scatters['late'], landed_early[0], scatter=True)

    me1 = me.reshape(1)
    updated = {}
    for n, own, landed in zip(EARLY + LATE, own_early[:-1] + own_late, landed_early[:-1] + landed_late):
        updated[n] = _sum_and_adamw(me1, landed, own, given[n], given['m_' + n], given['v_' + n], n)
    pack_small = lambda prefix: _pack_slabs([given[prefix + n] for n in SMALL], ())[None]
    packed = _sum_and_adamw(me1, landed_early[-1], own_early[-1], pack_small(''), pack_small('m_'), pack_small('v_'), 'small')
    unpacked = [_unpack_slabs(p[0], [local[n].shape for n in SMALL]) for p in packed]
    for i, n in enumerate(SMALL):
        updated[n] = [u[i] for u in unpacked]
    total = lax.psum(loss[0, 0], ("x", "y", "c"))
    return (total, dx.reshape(b, seq, d), *[updated[n][kind] for kind in range(4) for n in WEIGHTS])
```

```python
import math

import jax
import jax.numpy as jnp
from jax import lax
from jax.experimental import pallas as pl
from jax.experimental.pallas import tpu as pltpu

F32 = jnp.float32
BF16 = jnp.bfloat16
MESH = pl.DeviceIdType.MESH

N_DEV = 8
LANES = 128
HALO = 8
VMEM_LIMIT = 56 << 20

NORM_EPS = 1e-6
HEADS = 8
HEAD_PAD = 128
QK_NOPE = 64
QK_ROPE = 32
ROPE_HALF = 16
ROPE_BASE = 10000.0
ATTN_SCALE = (QK_NOPE + QK_ROPE) ** -0.5
LRU_C = 8.0
LRU_W = 512
CHUNK = 128
SGU_GROUPS = 8
D_FF = 2816
FF_BLOCKS = 2

ADAM_LR, ADAM_B1, ADAM_B2, ADAM_EPS, ADAM_WD, ADAM_STEP = 0.001, 0.9, 0.999, 1e-08, 0.01, 10

WEIGHTS = ['ab_norm', 'ab_w_in', 'ab_q_norm', 'ab_w_q_b', 'ab_kv_norm', 'ab_w_kv_b', 'ab_conv_w', 'ab_conv_b',
           'ab_w_rg_a', 'ab_b_rg_a', 'ab_w_rg_x', 'ab_b_rg_x', 'ab_lambda', 'ab_w_out', 'c_norm', 'c_w_in', 'c_ln_g',
           'c_ln_b', 'c_w_s', 'c_b_s', 'c_w_out', 'ffn_norm', 'ffn_w_gate', 'ffn_w_up', 'ffn_conv_w', 'ffn_conv_b',
           'ffn_w_down', 'final_norm']
SHARD_AXIS = {'ab_w_in': 2, 'ab_w_q_b': 2, 'ab_w_kv_b': 2, 'ab_conv_w': 2, 'ab_w_out': 1, 'c_norm': 1, 'c_w_in': 2,
              'c_ln_g': 1, 'c_ln_b': 1, 'c_w_out': 1, 'ffn_w_gate': 2, 'ffn_w_up': 2, 'ffn_conv_w': 2, 'ffn_w_down': 1}
MATRICES = ['ab_w_in', 'ab_w_q_b', 'ab_w_kv_b', 'ab_w_out', 'c_w_in', 'c_w_out', 'ffn_w_gate', 'ffn_w_up', 'ffn_w_down']
BIG = ['ab_w_in', 'c_w_in', 'ffn_w_gate', 'ffn_w_up', 'ab_w_out', 'c_w_out', 'ffn_w_down']
REPLICATED = [n for n in WEIGHTS if n not in SHARD_AXIS]
SMALL_SHARDED = [n for n in WEIGHTS if n in SHARD_AXIS and n not in BIG]


def _bf(x):
    return x.astype(BF16)


def _nn(a, b):
    return lax.dot_general(_bf(a), _bf(b), (((1,), (0,)), ((), ())), preferred_element_type=F32)


def _nt(a, b):
    return lax.dot_general(_bf(a), _bf(b), (((1,), (1,)), ((), ())), preferred_element_type=F32)


def _tn(a, b):
    return lax.dot_general(_bf(a), _bf(b), (((0,), (0,)), ((), ())), preferred_element_type=F32)


def _rms(x, g):
    return x * lax.rsqrt(jnp.mean(x * x, axis=-1, keepdims=True) + NORM_EPS) * g


def _layer_norm(x, g, b):
    xc = x - jnp.mean(x, axis=-1, keepdims=True)
    return xc * lax.rsqrt(jnp.mean(xc * xc, axis=-1, keepdims=True) + NORM_EPS) * g + b


def _gelu(x):
    return jax.nn.gelu(x)


def _colsum(x):
    return jnp.sum(x, axis=0, keepdims=True)


def _softplus(x):
    return jnp.maximum(x, 0.0) + jnp.log1p(jnp.exp(-jnp.abs(x)))


@jax.custom_vjp
def _one_minus_exp(x):
    u = jnp.exp(x)
    lg = jnp.log(u)
    near = lg == 0.0
    em1 = jnp.where(near, x, (u - 1.0) * x / jnp.where(near, 1.0, lg))
    return -jnp.where(x < -20.0, u - 1.0, em1)


def _one_minus_exp_fwd(x):
    return _one_minus_exp(x), x


def _one_minus_exp_bwd(x, ct):
    return (-jnp.exp(x) * ct,)


_one_minus_exp.defvjp(_one_minus_exp_fwd, _one_minus_exp_bwd)


def _accumulate(ref, val, first):
    @pl.when(first)
    def _():
        ref[...] = val

    @pl.when(jnp.logical_not(first))
    def _():
        ref[...] += val


def _params(n_axes=1):
    return pltpu.CompilerParams(dimension_semantics=("arbitrary",) * n_axes, vmem_limit_bytes=VMEM_LIMIT)


def _row(tm, n):
    return pl.BlockSpec((tm, n), lambda i: (i, 0))


def _const(shape):
    nd = len(shape)
    return pl.BlockSpec(shape, lambda i: (0,) * nd, pipeline_mode=pl.Buffered(1))


def _prev_halo(tm, n):
    return pl.BlockSpec((HALO, n), lambda i: (jnp.maximum(i * (tm // HALO) - 1, 0), 0))


def _next_halo(tm, n, n_tiles):
    last = n_tiles * (tm // HALO) - 1
    return pl.BlockSpec((HALO, n), lambda i: (jnp.minimum((i + 1) * (tm // HALO), last), 0))


def _sds(shape, dtype=F32):
    return jax.ShapeDtypeStruct(shape, dtype)


def _rope_tables(posb):
    lane = lax.broadcasted_iota(jnp.int32, posb.shape, 1)
    in_rope = jnp.logical_and(lane >= QK_NOPE, lane < QK_NOPE + QK_ROPE)
    j = (lane & (ROPE_HALF - 1)).astype(F32)
    inv_freq = jnp.exp((-math.log(ROPE_BASE)) * j / ROPE_HALF)
    ang = posb * inv_freq
    return jnp.where(in_rope, jnp.cos(ang), 1.0), jnp.where(in_rope, jnp.sin(ang), 0.0)


def _rot(q):
    n = q.shape[1]
    lane = lax.broadcasted_iota(jnp.int32, q.shape, 1) & (HEAD_PAD - 1)
    first_half = jnp.where(lane >= QK_NOPE, -pltpu.roll(q, n - ROPE_HALF, 1), 0.0)
    second_half = jnp.where(lane < QK_NOPE + QK_ROPE, pltpu.roll(q, ROPE_HALF, 1), 0.0)
    return jnp.where(lane < QK_NOPE + ROPE_HALF, first_half, second_half)


def _rope(q, cos_t, sin_t):
    return q * cos_t + _rot(q) * sin_t


def _rope_transpose(dq, cos_t, sin_t):
    return dq * cos_t - _rot(dq * sin_t)


def _tile_heads(t):
    return jnp.concatenate([t] * HEADS, axis=1)


Q_LORA, KV_LORA = 256, 128
Z_KPE = Q_LORA + KV_LORA
Z_LRU = Z_KPE + HEAD_PAD
Z_GATE = Z_LRU + LRU_W
Z_WIDTH = Z_GATE + LRU_W


def _ab_in_fwd(x, posb, w, tm):
    t, d = x.shape

    def body(x_ref, pos_ref, gn_ref, win_ref, qn_ref, wq_ref, kvn_ref, wk_ref, wv_ref, q_out, k_out, v_out, xl_out, gate_out):
        hn = _rms(x_ref[...], gn_ref[...])
        z = _nn(hn, win_ref[...])
        cqn = _rms(z[:, :Q_LORA], qn_ref[...])
        kvn = _rms(z[:, Q_LORA:Z_KPE], kvn_ref[...])
        cos_t, sin_t = _rope_tables(pos_ref[...])
        q_out[...] = _rope(_nn(cqn, wq_ref[...]), _tile_heads(cos_t), _tile_heads(sin_t))
        kpe = _rope(z[:, Z_KPE:Z_LRU], cos_t, sin_t)
        k_out[...] = _nn(kvn, wk_ref[...]) + _tile_heads(kpe)
        v_out[...] = _nn(kvn, wv_ref[...])
        xl_out[...] = z[:, Z_LRU:Z_GATE]
        gate_out[...] = z[:, Z_GATE:]

    hp = HEADS * HEAD_PAD
    return pl.pallas_call(
        body, name="ab_in_fwd", grid=(t // tm,),
        in_specs=[_row(tm, d), _row(tm, LANES), _const((1, d)), _const((d, Z_WIDTH)), _const((1, Q_LORA)), _const((Q_LORA, hp)),
                  _const((1, KV_LORA)), _const((KV_LORA, hp)), _const((KV_LORA, hp))],
        out_specs=[_row(tm, hp), _row(tm, hp), _row(tm, hp), _row(tm, LRU_W), _row(tm, LRU_W)],
        out_shape=[_sds((t, hp)), _sds((t, hp)), _sds((t, hp)), _sds((t, LRU_W)), _sds((t, LRU_W))],
        compiler_params=_params(),
    )(x, posb, w['ab_norm'], w['W_in'], w['ab_q_norm'], w['Wq'], w['ab_kv_norm'], w['Wk'], w['Wv'])


def _ab_in_bwd(x, posb, w, dq, dk, dv, dxl, dgate, dres, tm):
    t, d = x.shape
    hp = HEADS * HEAD_PAD

    def body(x_ref, pos_ref, gn_ref, win_ref, qn_ref, wq_ref, kvn_ref, wk_ref, wv_ref, dq_ref, dk_ref, dv_ref, dxl_ref, dgate_ref,
             dres_ref, dx_out, dgn_out, dwin_out, dqn_out, dwq_out, dkvn_out, dwk_out, dwv_out):
        first = pl.program_id(0) == 0
        hn, vjp_in = jax.vjp(_rms, x_ref[...], gn_ref[...])
        z = _nn(hn, win_ref[...])
        cqn, vjp_q = jax.vjp(_rms, z[:, :Q_LORA], qn_ref[...])
        kvn, vjp_kv = jax.vjp(_rms, z[:, Q_LORA:Z_KPE], kvn_ref[...])
        cos_t, sin_t = _rope_tables(pos_ref[...])
        dq0 = _rope_transpose(dq_ref[...], _tile_heads(cos_t), _tile_heads(sin_t))
        dk0 = dk_ref[...]
        dv0 = dv_ref[...]
        dkpe = dk0[:, :HEAD_PAD]
        for h in range(1, HEADS):
            dkpe = dkpe + dk0[:, h * HEAD_PAD:(h + 1) * HEAD_PAD]
        dkpe = _rope_transpose(dkpe, cos_t, sin_t)
        _accumulate(dwq_out, _tn(cqn, dq0), first)
        _accumulate(dwk_out, _tn(kvn, dk0), first)
        _accumulate(dwv_out, _tn(kvn, dv0), first)
        dcq, dqn = vjp_q(_nt(dq0, wq_ref[...]))
        dckv, dkvn = vjp_kv(_nt(dk0, wk_ref[...]) + _nt(dv0, wv_ref[...]))
        _accumulate(dqn_out, dqn, first)
        _accumulate(dkvn_out, dkvn, first)
        dz = jnp.concatenate([dcq, dckv, dkpe, dxl_ref[...], dgate_ref[...]], axis=1)
        _accumulate(dwin_out, _tn(hn, dz), first)
        dx, dgn = vjp_in(_nt(dz, win_ref[...]))
        _accumulate(dgn_out, dgn, first)
        dx_out[...] = dx + dres_ref[...]

    return pl.pallas_call(
        body, name="ab_in_bwd", grid=(t // tm,),
        in_specs=[_row(tm, d), _row(tm, LANES), _const((1, d)), _const((d, Z_WIDTH)), _const((1, Q_LORA)), _const((Q_LORA, hp)),
                  _const((1, KV_LORA)), _const((KV_LORA, hp)), _const((KV_LORA, hp)),
                  _row(tm, hp), _row(tm, hp), _row(tm, hp), _row(tm, LRU_W), _row(tm, LRU_W), _row(tm, d)],
        out_specs=[_row(tm, d), _const((1, d)), _const((d, Z_WIDTH)), _const((1, Q_LORA)), _const((Q_LORA, hp)),
                   _const((1, KV_LORA)), _const((KV_LORA, hp)), _const((KV_LORA, hp))],
        out_shape=[_sds((t, d)), _sds((1, d)), _sds((d, Z_WIDTH)), _sds((1, Q_LORA)), _sds((Q_LORA, hp)),
                   _sds((1, KV_LORA)), _sds((KV_LORA, hp)), _sds((KV_LORA, hp))],
        compiler_params=_params(),
    )(x, posb, w['ab_norm'], w['W_in'], w['ab_q_norm'], w['Wq'], w['ab_kv_norm'], w['Wk'], w['Wv'], dq, dk, dv, dxl, dgate, dres)


def _attn_probs(q_blk, k_ext, i, tq):
    ext = k_ext.shape[0]
    s = lax.dot_general(q_blk, k_ext, (((1,), (1,)), ((), ())), preferred_element_type=F32) * ATTN_SCALE
    row = lax.broadcasted_iota(jnp.int32, (tq, ext), 0) + i * tq
    col = lax.broadcasted_iota(jnp.int32, (tq, ext), 1)
    s = jnp.where(col <= row, s, -1e30)
    p = jnp.exp(s - jnp.max(s, axis=1, keepdims=True))
    return p / jnp.sum(p, axis=1, keepdims=True)


def _attn_fwd(q, k, v, tq):
    b, s, hp = q.shape
    blk = pl.BlockSpec((1, s, HEAD_PAD), lambda bi, h: (bi, 0, h))

    def body(q_ref, k_ref, v_ref, o_ref):
        kb = _bf(k_ref[0])
        vb = _bf(v_ref[0])
        for i in range(s // tq):
            ext = (i + 1) * tq
            p = _attn_probs(_bf(q_ref[0, i * tq:ext, :]), kb[:ext], i, tq)
            o_ref[0, i * tq:ext, :] = lax.dot_general(_bf(p), vb[:ext], (((1,), (0,)), ((), ())), preferred_element_type=F32)

    return pl.pallas_call(body, name="attn_fwd", grid=(b, HEADS), in_specs=[blk, blk, blk], out_specs=blk,
                          out_shape=_sds((b, s, hp)), compiler_params=_params(2))(q, k, v)


def _attn_bwd(q, k, v, do, tq):
    b, s, hp = q.shape
    blk = pl.BlockSpec((1, s, HEAD_PAD), lambda bi, h: (bi, 0, h))

    def body(q_ref, k_ref, v_ref, do_ref, dq_ref, dk_ref, dv_ref):
        kb = _bf(k_ref[0])
        vb = _bf(v_ref[0])
        dk_ref[...] = jnp.zeros_like(dk_ref)
        dv_ref[...] = jnp.zeros_like(dv_ref)
        for i in range(s // tq):
            ext = (i + 1) * tq
            qb = _bf(q_ref[0, i * tq:ext, :])
            dob = _bf(do_ref[0, i * tq:ext, :])
            p = _attn_probs(qb, kb[:ext], i, tq)
            dv_ref[0, :ext, :] += lax.dot_general(_bf(p), dob, (((0,), (0,)), ((), ())), preferred_element_type=F32)
            dp = lax.dot_general(dob, vb[:ext], (((1,), (1,)), ((), ())), preferred_element_type=F32)
            ds = _bf(p * (dp - jnp.sum(p * dp, axis=1, keepdims=True)) * ATTN_SCALE)
            dq_ref[0, i * tq:ext, :] = lax.dot_general(ds, kb[:ext], (((1,), (0,)), ((), ())), preferred_element_type=F32)
            dk_ref[0, :ext, :] += lax.dot_general(ds, qb, (((0,), (0,)), ((), ())), preferred_element_type=F32)

    return pl.pallas_call(body, name="attn_bwd", grid=(b, HEADS), in_specs=[blk, blk, blk, blk], out_specs=[blk, blk, blk],
                          out_shape=[_sds((b, s, hp))] * 3, compiler_params=_params(2))(q, k, v, do)


LRU_CONV = 4


def _lru_point(pre_a, pre_x, xc, lam):
    r = jax.nn.sigmoid(pre_a)
    i = jax.nn.sigmoid(pre_x)
    log_a = -LRU_C * r * _softplus(-lam)
    return jnp.exp(log_a), jnp.sqrt(_one_minus_exp(2.0 * log_a)) * (i * xc)


def _causal_conv(pad_ref, x, halo, first_in_seq, w, taps):
    tm = x.shape[0]
    pad_ref[:HALO, :] = jnp.where(first_in_seq, 0.0, halo)
    pad_ref[HALO:, :] = x
    y = w[taps - 1:taps, :] * x
    for k in range(taps - 1):
        off = HALO - (taps - 1) + k
        y = y + w[k:k + 1, :] * pad_ref[off:off + tm, :]
    return y


def _causal_conv_wgrad(pad_ref, dy, taps):
    tm = dy.shape[0]
    return jnp.concatenate([_colsum(dy * pad_ref[HALO - (taps - 1) + k:HALO - (taps - 1) + k + tm, :]) for k in range(taps)], axis=0)


def _causal_conv_transpose(pad_ref, dy, halo_next, last_in_seq, w, taps):
    tm = dy.shape[0]
    pad_ref[:tm, :] = dy
    pad_ref[tm:, :] = jnp.where(last_in_seq, 0.0, halo_next)
    dx = w[taps - 1:taps, :] * dy
    for k in range(taps - 1):
        off = (taps - 1) - k
        dx = dx + w[k:k + 1, :] * pad_ref[off:off + tm, :]
    return dx


def _lru_fwd(xl, gate, w, ts, seq):
    t, n = xl.shape
    tiles_per_seq = seq // ts

    def body(xl_ref, halo_ref, gate_ref, cw_ref, cb_ref, wa_ref, ba_ref, wx_ref, bx_ref, lam_ref, y_out, h_out, pad_ref, a_ref, b_ref, carry_ref):
        first_in_seq = pl.program_id(0) % tiles_per_seq == 0
        xc = _causal_conv(pad_ref, xl_ref[...], halo_ref[...], first_in_seq, cw_ref[...], LRU_CONV) + cb_ref[...]
        a, bx = _lru_point(_nn(xc, wa_ref[...]) + ba_ref[...], _nn(xc, wx_ref[...]) + bx_ref[...], xc, lam_ref[...])
        a_ref[...] = a
        b_ref[...] = bx

        @pl.when(first_in_seq)
        def _():
            carry_ref[...] = jnp.zeros_like(carry_ref)

        def step(r, h):
            h = a_ref[pl.ds(r, 1), :] * h + b_ref[pl.ds(r, 1), :]
            h_out[pl.ds(r, 1), :] = h
            return h

        carry_ref[...] = lax.fori_loop(0, ts, step, carry_ref[...], unroll=8)
        y_out[...] = h_out[...] * _gelu(gate_ref[...])

    return pl.pallas_call(
        body, name="lru_fwd", grid=(t // ts,),
        in_specs=[_row(ts, n), _prev_halo(ts, n), _row(ts, n), _const((LRU_CONV, n)), _const((1, n)), _const((n, n)), _const((1, n)),
                  _const((n, n)), _const((1, n)), _const((1, n))],
        out_specs=[_row(ts, n), _row(ts, n)], out_shape=[_sds((t, n)), _sds((t, n))],
        scratch_shapes=[pltpu.VMEM((HALO + ts, n), F32), pltpu.VMEM((ts, n), F32), pltpu.VMEM((ts, n), F32), pltpu.VMEM((1, n), F32)],
        compiler_params=_params(),
    )(xl, xl, gate, w['ab_conv_w'], w['ab_conv_b'], w['Wa'], w['ab_b_rg_a'], w['Wx'], w['ab_b_rg_x'], w['ab_lambda'])


def _lru_bwd(xl, gate, hs, dy, w, ts, seq):
    t, n = xl.shape
    tiles_per_seq = seq // ts
    n_tiles = t // ts

    def rev(i):
        return n_tiles - 1 - i

    row = pl.BlockSpec((ts, n), lambda i: (rev(i), 0))
    prev = pl.BlockSpec((HALO, n), lambda i: (jnp.maximum(rev(i) * (ts // HALO) - 1, 0), 0))
    acc = lambda shape: pl.BlockSpec(shape, lambda i: (0,) * len(shape))

    def body(xl_ref, xhalo_ref, gate_ref, h_ref, hhalo_ref, dy_ref, cw_ref, cb_ref, wa_ref, ba_ref, wx_ref, bx_ref, lam_ref,
             dxl_out, dgate_out, dcw_out, dcb_out, dwa_out, dba_out, dwx_out, dbx_out, dlam_out,
             pad_ref, padh_ref, padd_ref, a_ref, g_ref, carry_ref, dhalo_ref):
        step_id = pl.program_id(0)
        first = step_id == 0
        tile = rev(step_id)
        first_in_seq = tile % tiles_per_seq == 0
        last_in_seq = tile % tiles_per_seq == tiles_per_seq - 1
        cw = cw_ref[...]
        xc = _causal_conv(pad_ref, xl_ref[...], xhalo_ref[...], first_in_seq, cw, LRU_CONV) + cb_ref[...]
        pre_a = _nn(xc, wa_ref[...]) + ba_ref[...]
        pre_x = _nn(xc, wx_ref[...]) + bx_ref[...]
        (a, _), vjp_point = jax.vjp(_lru_point, pre_a, pre_x, xc, lam_ref[...])
        h = h_ref[...]
        _, vjp_out = jax.vjp(lambda h_, g_: h_ * _gelu(g_), h, gate_ref[...])
        dh, dgate = vjp_out(dy_ref[...])
        dgate_out[...] = dgate
        a_ref[...] = a
        g_ref[...] = dh

        @pl.when(last_in_seq)
        def _():
            carry_ref[...] = jnp.zeros_like(carry_ref)

        def step(j, c):
            r = ts - 1 - j
            g = g_ref[pl.ds(r, 1), :] + c
            g_ref[pl.ds(r, 1), :] = g
            return a_ref[pl.ds(r, 1), :] * g

        carry_ref[...] = lax.fori_loop(0, ts, step, carry_ref[...], unroll=8)
        g = g_ref[...]
        padh_ref[:HALO, :] = jnp.where(first_in_seq, 0.0, hhalo_ref[...])
        padh_ref[HALO:, :] = h
        dpre_a, dpre_x, dxc, dlam = vjp_point((g * padh_ref[HALO - 1:HALO - 1 + ts, :], g))
        dxc = dxc + _nt(dpre_a, wa_ref[...]) + _nt(dpre_x, wx_ref[...])
        _accumulate(dwa_out, _tn(xc, dpre_a), first)
        _accumulate(dwx_out, _tn(xc, dpre_x), first)
        _accumulate(dba_out, _colsum(dpre_a), first)
        _accumulate(dbx_out, _colsum(dpre_x), first)
        _accumulate(dlam_out, dlam, first)
        _accumulate(dcb_out, _colsum(dxc), first)
        _accumulate(dcw_out, _causal_conv_wgrad(pad_ref, dxc, LRU_CONV), first)
        dxl_out[...] = _causal_conv_transpose(padd_ref, dxc, dhalo_ref[...], last_in_seq, cw, LRU_CONV)
        dhalo_ref[...] = dxc[:HALO, :]

    return pl.pallas_call(
        body, name="lru_bwd", grid=(n_tiles,),
        in_specs=[row, prev, row, row, prev, row, _const((LRU_CONV, n)), _const((1, n)), _const((n, n)), _const((1, n)),
                  _const((n, n)), _const((1, n)), _const((1, n))],
        out_specs=[row, row, acc((LRU_CONV, n)), acc((1, n)), acc((n, n)), acc((1, n)), acc((n, n)), acc((1, n)), acc((1, n))],
        out_shape=[_sds((t, n)), _sds((t, n)), _sds((LRU_CONV, n)), _sds((1, n)), _sds((n, n)), _sds((1, n)), _sds((n, n)),
                   _sds((1, n)), _sds((1, n))],
        scratch_shapes=[pltpu.VMEM((HALO + ts, n), F32), pltpu.VMEM((HALO + ts, n), F32), pltpu.VMEM((ts + HALO, n), F32),
                        pltpu.VMEM((ts, n), F32), pltpu.VMEM((ts, n), F32), pltpu.VMEM((1, n), F32), pltpu.VMEM((HALO, n), F32)],
        compiler_params=_params(),
    )(xl, xl, gate, hs, hs, dy, w['ab_conv_w'], w['ab_conv_b'], w['Wa'], w['ab_b_rg_a'], w['Wx'], w['ab_b_rg_x'], w['ab_lambda'])


def _ab_out_fwd(x, o, y, w, tm):
    t, d = x.shape
    hp = o.shape[1]

    def body(x_ref, o_ref, y_ref, wa_ref, wb_ref, h_out):
        h_out[...] = x_ref[...] + _nn(o_ref[...], wa_ref[...]) + _nn(y_ref[...], wb_ref[...])

    return pl.pallas_call(body, name="ab_out_fwd", grid=(t // tm,),
                          in_specs=[_row(tm, d), _row(tm, hp), _row(tm, LRU_W), _const((hp, d)), _const((LRU_W, d))],
                          out_specs=_row(tm, d), out_shape=_sds((t, d)), compiler_params=_params())(x, o, y, w['Wo_a'], w['Wo_b'])


def _ab_out_bwd(o, y, dh, w, tm):
    t, d = dh.shape
    hp = o.shape[1]

    def body(o_ref, y_ref, dh_ref, wa_ref, wb_ref, do_out, dy_out, dwa_out, dwb_out):
        first = pl.program_id(0) == 0
        dh_t = dh_ref[...]
        do_out[...] = _nt(dh_t, wa_ref[...])
        dy_out[...] = _nt(dh_t, wb_ref[...])
        _accumulate(dwa_out, _tn(o_ref[...], dh_t), first)
        _accumulate(dwb_out, _tn(y_ref[...], dh_t), first)

    return pl.pallas_call(body, name="ab_out_bwd", grid=(t // tm,),
                          in_specs=[_row(tm, hp), _row(tm, LRU_W), _row(tm, d), _const((hp, d)), _const((LRU_W, d))],
                          out_specs=[_row(tm, hp), _row(tm, LRU_W), _const((hp, d)), _const((LRU_W, d))],
                          out_shape=[_sds((t, hp)), _sds((t, LRU_W)), _sds((hp, d)), _sds((LRU_W, d))],
                          compiler_params=_params())(o, y, dh, w['Wo_a'], w['Wo_b'])


FFN_CONV = 3


def _ffn_a_fwd(h, norm, wg, wu, tm):
    t, d = h.shape
    fb = D_FF // FF_BLOCKS

    def body(h_ref, gn_ref, wg_ref, wu_ref, g_out, u_out):
        hn = _rms(h_ref[...], gn_ref[...])
        g_out[...] = _nn(hn, wg_ref[...])
        u_out[...] = _nn(hn, wu_ref[...])

    wspec = pl.BlockSpec((d, fb), lambda f, i: (0, f))
    ospec = pl.BlockSpec((tm, fb), lambda f, i: (i, f))
    return pl.pallas_call(body, name="ffn_a_fwd", grid=(FF_BLOCKS, t // tm),
                          in_specs=[pl.BlockSpec((tm, d), lambda f, i: (i, 0)), pl.BlockSpec((1, d), lambda f, i: (0, 0)), wspec, wspec],
                          out_specs=[ospec, ospec], out_shape=[_sds((t, D_FF)), _sds((t, D_FF))],
                          compiler_params=_params(2))(h, norm, wg, wu)


def _ffn_b_fwd(g, u, h, cw, cb, wd, tm, seq):
    t, d = h.shape
    tiles_per_seq = seq // tm

    def body(g_ref, halo_ref, u_ref, h_ref, cw_ref, cb_ref, wd_ref, h_out, pad_ref):
        first_in_seq = pl.program_id(0) % tiles_per_seq == 0
        gc = _causal_conv(pad_ref, g_ref[...], halo_ref[...], first_in_seq, cw_ref[...], FFN_CONV) + cb_ref[...]
        h_out[...] = h_ref[...] + _nn(_gelu(gc) * u_ref[...], wd_ref[...])

    return pl.pallas_call(body, name="ffn_b_fwd", grid=(t // tm,),
                          in_specs=[_row(tm, D_FF), _prev_halo(tm, D_FF), _row(tm, D_FF), _row(tm, d), _const((FFN_CONV, D_FF)),
                                    _const((1, D_FF)), _const((D_FF, d))],
                          out_specs=_row(tm, d), out_shape=_sds((t, d)),
                          scratch_shapes=[pltpu.VMEM((HALO + tm, D_FF), F32)], compiler_params=_params())(g, g, u, h, cw, cb, wd)


def _ffn_b_bwd(g, u, dout, cw, cb, wd, tm, seq):
    t, d = dout.shape
    fb = D_FF // FF_BLOCKS
    tiles_per_seq = seq // tm

    def body(g_ref, halo_ref, u_ref, dout_ref, cw_ref, cb_ref, wd_ref, dgc_out, du_out, dwd_out, dcw_out, dcb_out, pad_ref):
        i = pl.program_id(1)
        first = i == 0
        gc = _causal_conv(pad_ref, g_ref[...], halo_ref[...], i % tiles_per_seq == 0, cw_ref[...], FFN_CONV) + cb_ref[...]
        act, vjp_act = jax.vjp(lambda gc_, u_: _gelu(gc_) * u_, gc, u_ref[...])
        dout_t = dout_ref[...]
        dgc, du = vjp_act(_nt(dout_t, wd_ref[...]))
        dgc_out[...] = dgc
        du_out[...] = du
        _accumulate(dwd_out, _tn(act, dout_t), first)
        _accumulate(dcb_out, _colsum(dgc), first)
        _accumulate(dcw_out, _causal_conv_wgrad(pad_ref, dgc, FFN_CONV), first)

    blk = pl.BlockSpec((tm, fb), lambda f, i: (i, f))
    halo = pl.BlockSpec((HALO, fb), lambda f, i: (jnp.maximum(i * (tm // HALO) - 1, 0), f))
    return pl.pallas_call(
        body, name="ffn_b_bwd", grid=(FF_BLOCKS, t // tm),
        in_specs=[blk, halo, blk, pl.BlockSpec((tm, d), lambda f, i: (i, 0)), pl.BlockSpec((FFN_CONV, fb), lambda f, i: (0, f)),
                  pl.BlockSpec((1, fb), lambda f, i: (0, f)), pl.BlockSpec((fb, d), lambda f, i: (f, 0))],
        out_specs=[blk, blk, pl.BlockSpec((fb, d), lambda f, i: (f, 0)), pl.BlockSpec((FFN_CONV, fb), lambda f, i: (0, f)),
                   pl.BlockSpec((1, fb), lambda f, i: (0, f))],
        out_shape=[_sds((t, D_FF)), _sds((t, D_FF)), _sds((D_FF, d)), _sds((FFN_CONV, D_FF)), _sds((1, D_FF))],
        scratch_shapes=[pltpu.VMEM((HALO + tm, fb), F32)], compiler_params=_params(2))(g, g, u, dout, cw, cb, wd)


def _ffn_a_dgrad(h, norm, dgc, du, dres, cw, wg, wu, tm, seq):
    t, d = h.shape
    tiles_per_seq = seq // tm
    n_tiles = t // tm

    def body(h_ref, gn_ref, dgc_ref, halo_ref, du_ref, dres_ref, cw_ref, wg_ref, wu_ref, dh_out, dg_out, dgn_out, pad_ref):
        i = pl.program_id(0)
        last_in_seq = i % tiles_per_seq == tiles_per_seq - 1
        dg = _causal_conv_transpose(pad_ref, dgc_ref[...], halo_ref[...], last_in_seq, cw_ref[...], FFN_CONV)
        dg_out[...] = dg
        _, vjp_norm = jax.vjp(_rms, h_ref[...], gn_ref[...])
        dh, dgn = vjp_norm(_nt(dg, wg_ref[...]) + _nt(du_ref[...], wu_ref[...]))
        dh_out[...] = dh + dres_ref[...]
        _accumulate(dgn_out, dgn, i == 0)

    return pl.pallas_call(
        body, name="ffn_a_dgrad", grid=(n_tiles,),
        in_specs=[_row(tm, d), _const((1, d)), _row(tm, D_FF), _next_halo(tm, D_FF, n_tiles), _row(tm, D_FF), _row(tm, d),
                  _const((FFN_CONV, D_FF)), _const((d, D_FF)), _const((d, D_FF))],
        out_specs=[_row(tm, d), _row(tm, D_FF), _const((1, d))], out_shape=[_sds((t, d)), _sds((t, D_FF)), _sds((1, d))],
        scratch_shapes=[pltpu.VMEM((tm + HALO, D_FF), F32)], compiler_params=_params())(h, norm, dgc, dgc, du, dres, cw, wg, wu)


def _ffn_a_wgrad(h, norm, dg, du, tm):
    t, d = h.shape
    fb = D_FF // FF_BLOCKS

    def body(h_ref, gn_ref, dg_ref, du_ref, dwg_out, dwu_out):
        first = pl.program_id(1) == 0
        hn = _rms(h_ref[...], gn_ref[...])
        _accumulate(dwg_out, _tn(hn, dg_ref[...]), first)
        _accumulate(dwu_out, _tn(hn, du_ref[...]), first)

    blk = pl.BlockSpec((tm, fb), lambda f, i: (i, f))
    wspec = pl.BlockSpec((d, fb), lambda f, i: (0, f))
    return pl.pallas_call(body, name="ffn_a_wgrad", grid=(FF_BLOCKS, t // tm),
                          in_specs=[pl.BlockSpec((tm, d), lambda f, i: (i, 0)), pl.BlockSpec((1, d), lambda f, i: (0, 0)), blk, blk],
                          out_specs=[wspec, wspec], out_shape=[_sds((d, D_FF)), _sds((d, D_FF))],
                          compiler_params=_params(2))(h, norm, dg, du)


def _sgu_mix(vn, ws_ref, bst):
    tril = lax.broadcasted_iota(jnp.int32, (CHUNK, CHUNK), 0) >= lax.broadcasted_iota(jnp.int32, (CHUNK, CHUNK), 1)
    wms = [jnp.where(tril, ws_ref[g], 0.0) for g in range(SGU_GROUPS)]
    chunks = []
    for n in range(vn.shape[0] // CHUNK):
        vc = vn[n * CHUNK:(n + 1) * CHUNK, :]
        chunks.append(jnp.concatenate(
            [_nn(wms[g], vc[:, g * CHUNK:(g + 1) * CHUNK]) + bst[:, g:g + 1] for g in range(SGU_GROUPS)], axis=1))
    return jnp.concatenate(chunks, axis=0)


def _sgu_fwd(h, w, tm):
    t, d = h.shape

    def body(h_ref, cn_ref, win_ref, lg_ref, lb_ref, ws_ref, bst_ref, wout_ref, h_out):
        h_t = h_ref[...]
        z = _gelu(_nn(_rms(h_t, cn_ref[...]), win_ref[...]))
        vn = _layer_norm(z[:, d:], lg_ref[...], lb_ref[...])
        s = _sgu_mix(vn, ws_ref, bst_ref[...])
        h_out[...] = h_t + _nn(z[:, :d] * s, wout_ref[...])

    return pl.pallas_call(
        body, name="sgu_fwd", grid=(t // tm,),
        in_specs=[_row(tm, d), _const((1, d)), _const((d, 2 * d)), _const((1, d)), _const((1, d)), _const((SGU_GROUPS, CHUNK, CHUNK)),
                  _const((CHUNK, LANES)), _const((d, d))],
        out_specs=_row(tm, d), out_shape=_sds((t, d)), compiler_params=_params(),
    )(h, w['c_norm'], w['c_w_in'], w['c_ln_g'], w['c_ln_b'], w['c_w_s'], w['bsT'], w['c_w_out'])


def _sgu_bwd(h, dout, w, tm):
    t, d = h.shape

    def body(h_ref, dout_ref, cn_ref, win_ref, lg_ref, lb_ref, ws_ref, bst_ref, wout_ref,
             dh_out, dcn_out, dwin_out, dlg_out, dlb_out, dws_out, dbst_out, dwout_out):
        first = pl.program_id(0) == 0
        hn, vjp_norm = jax.vjp(_rms, h_ref[...], cn_ref[...])
        zpre = _nn(hn, win_ref[...])
        u, vjp_u = jax.vjp(_gelu, zpre[:, :d])
        vn, vjp_v = jax.vjp(lambda zp, lg, lb: _layer_norm(_gelu(zp), lg, lb), zpre[:, d:], lg_ref[...], lb_ref[...])
        s = _sgu_mix(vn, ws_ref, bst_ref[...])
        dout_t = dout_ref[...]
        dus = _nt(dout_t, wout_ref[...])
        _accumulate(dwout_out, _tn(u * s, dout_t), first)
        ds = dus * u
        tril = lax.broadcasted_iota(jnp.int32, (CHUNK, CHUNK), 0) >= lax.broadcasted_iota(jnp.int32, (CHUNK, CHUNK), 1)
        lane = lax.broadcasted_iota(jnp.int32, (CHUNK, LANES), 1)
        dws = [jnp.zeros((CHUNK, CHUNK), F32) for _ in range(SGU_GROUPS)]
        dbst = jnp.zeros((CHUNK, LANES), F32)
        dvn_chunks = []
        for n in range(tm // CHUNK):
            cols = []
            for g in range(SGU_GROUPS):
                ds_ng = ds[n * CHUNK:(n + 1) * CHUNK, g * CHUNK:(g + 1) * CHUNK]
                vc_ng = vn[n * CHUNK:(n + 1) * CHUNK, g * CHUNK:(g + 1) * CHUNK]
                cols.append(_tn(jnp.where(tril, ws_ref[g], 0.0), ds_ng))
                dws[g] = dws[g] + _nt(ds_ng, vc_ng)
                dbst = dbst + jnp.where(lane == g, jnp.sum(ds_ng, axis=1, keepdims=True), 0.0)
            dvn_chunks.append(jnp.concatenate(cols, axis=1))
        dvn = jnp.concatenate(dvn_chunks, axis=0)
        for g in range(SGU_GROUPS):
            val = jnp.where(tril, dws[g], 0.0)

            @pl.when(first)
            def _():
                dws_out[g] = val

            @pl.when(jnp.logical_not(first))
            def _():
                dws_out[g] += val
        _accumulate(dbst_out, dbst, first)
        (dzu,) = vjp_u(dus * s)
        dzv, dlg, dlb = vjp_v(dvn)
        _accumulate(dlg_out, dlg, first)
        _accumulate(dlb_out, dlb, first)
        dzpre = jnp.concatenate([dzu, dzv], axis=1)
        _accumulate(dwin_out, _tn(hn, dzpre), first)
        dh, dcn = vjp_norm(_nt(dzpre, win_ref[...]))
        _accumulate(dcn_out, dcn, first)
        dh_out[...] = dh + dout_t

    return pl.pallas_call(
        body, name="sgu_bwd", grid=(t // tm,),
        in_specs=[_row(tm, d), _row(tm, d), _const((1, d)), _const((d, 2 * d)), _const((1, d)), _const((1, d)),
                  _const((SGU_GROUPS, CHUNK, CHUNK)), _const((CHUNK, LANES)), _const((d, d))],
        out_specs=[_row(tm, d), _const((1, d)), _const((d, 2 * d)), _const((1, d)), _const((1, d)), _const((SGU_GROUPS, CHUNK, CHUNK)),
                   _const((CHUNK, LANES)), _const((d, d))],
        out_shape=[_sds((t, d)), _sds((1, d)), _sds((d, 2 * d)), _sds((1, d)), _sds((1, d)), _sds((SGU_GROUPS, CHUNK, CHUNK)),
                   _sds((CHUNK, LANES)), _sds((d, d))],
        compiler_params=_params(),
    )(h, dout, w['c_norm'], w['c_w_in'], w['c_ln_g'], w['c_ln_b'], w['c_w_s'], w['bsT'], w['c_w_out'])


def _final_loss(h, target, norm, tm):
    t, d = h.shape

    def body(h_ref, tgt_ref, gn_ref, dh_out, loss_out, dgn_out):
        first = pl.program_id(0) == 0
        tgt = tgt_ref[...]

        def loss_fn(h_, g_):
            err = _rms(h_, g_) - tgt
            return 0.5 * jnp.sum(jnp.mean(err * err, axis=-1, keepdims=True), axis=0, keepdims=True)

        loss, vjp_loss = jax.vjp(loss_fn, h_ref[...], gn_ref[...])
        dh, dgn = vjp_loss(jnp.ones((1, 1), F32))
        dh_out[...] = dh
        _accumulate(loss_out, loss, first)
        _accumulate(dgn_out, dgn, first)

    return pl.pallas_call(body, name="final_loss", grid=(t // tm,), in_specs=[_row(tm, d), _row(tm, d), _const((1, d))],
                          out_specs=[_row(tm, d), _const((1, 1)), _const((1, d))],
                          out_shape=[_sds((t, d)), _sds((1, 1)), _sds((1, d))], compiler_params=_params())(h, target, norm)


def _tile(t, seq, want):
    tm = min(want, seq)
    assert seq % tm == 0 and t % tm == 0 and tm % CHUNK == 0
    return tm


def _local_step(x, posb, target, w, seq, late_weights, on_grads):
    t, d = x.shape
    b = t // seq
    hp = HEADS * HEAD_PAD
    tm_big, tm_mid = _tile(t, seq, 512), _tile(t, seq, 256)
    tq = _tile(t, seq, 512)

    q, k, v, xl, gate = _ab_in_fwd(x, posb, w, tm_big)
    o = _attn_fwd(q.reshape(b, seq, hp), k.reshape(b, seq, hp), v.reshape(b, seq, hp), tq).reshape(t, hp)
    y, hs = _lru_fwd(xl, gate, w, tm_big, seq)
    w = {**w, **late_weights('out0', y)}
    h1 = _ab_out_fwd(x, o, y, w, tm_big)
    hcur = h1
    saved = []
    for l in range(2):
        if l == 1:
            w = {**w, **late_weights('mix1', hcur)}
            saved_h2 = hcur
            hcur = _sgu_fwd(hcur, w, tm_mid)
        wl = late_weights('ffn%d' % l, hcur)
        g, u = _ffn_a_fwd(hcur, w['ffn_norm'][l], wl['Wg'], wl['Wu'], tm_big)
        hnext = _ffn_b_fwd(g, u, hcur, w['ffn_conv_w'][l], w['ffn_conv_b'][l], wl['Wd'], tm_mid, seq)
        saved.append((hcur, g, u, wl))
        hcur = hnext
    dh, loss, d_final = _final_loss(hcur, target, w['final_norm'], tm_big)

    late = {'final_norm': d_final}
    ffn = {}
    for l in (1, 0):
        hin, g, u, wl = saved[l]
        dgc, du, d_wd, d_cw, d_cb = _ffn_b_bwd(g, u, dh, w['ffn_conv_w'][l], w['ffn_conv_b'][l], wl['Wd'], tm_mid, seq)
        dh, dg, d_norm = _ffn_a_dgrad(hin, w['ffn_norm'][l], dgc, du, dh, w['ffn_conv_w'][l], wl['Wg'], wl['Wu'], tm_mid, seq)
        d_wg, d_wu = _ffn_a_wgrad(hin, w['ffn_norm'][l], dg, du, tm_big)
        ffn[l] = dict(ffn_norm=d_norm, ffn_conv_w=d_cw, ffn_conv_b=d_cb, Wg=d_wg, Wu=d_wu, Wd=d_wd)
        if l == 1:
            dh, d_cn, d_cwin, d_lg, d_lb, d_ws, d_bst, d_cwout = _sgu_bwd(saved_h2, dh, w, tm_mid)
            late.update(c_norm=d_cn, c_ln_g=d_lg, c_ln_b=d_lb, c_w_s=d_ws, bsT=d_bst, c_w_in=d_cwin, c_w_out=d_cwout)
    for name in ffn[0]:
        late[name] = [ffn[0][name], ffn[1][name]]
    zero = on_grads('late', late)
    w = {**w, 'Wo_b': w['Wo_b'] + zero.astype(w['Wo_b'].dtype)}
    do, dy, d_woa, d_wob = _ab_out_bwd(o, y, dh, w, tm_big)
    dxl, dgate, d_cw, d_cb, d_wa, d_ba, d_wx, d_bx, d_lam = _lru_bwd(xl, gate, hs, dy, w, tm_big, seq)
    zero = on_grads('mid', dict(Wo_a=d_woa, Wo_b=d_wob, ab_conv_w=d_cw, ab_conv_b=d_cb, Wa=d_wa, ab_b_rg_a=d_ba, Wx=d_wx,
                                ab_b_rg_x=d_bx, ab_lambda=d_lam))
    w = {**w, 'ab_norm': w['ab_norm'] + zero}
    dq, dk, dv = _attn_bwd(q.reshape(b, seq, hp), k.reshape(b, seq, hp), v.reshape(b, seq, hp), do.reshape(b, seq, hp), tq)
    dx, d_gn, d_win, d_qn, d_wq, d_kvn, d_wk, d_wv = _ab_in_bwd(
        x, posb, w, dq.reshape(t, hp), dk.reshape(t, hp), dv.reshape(t, hp), dxl, dgate, dh, tm_mid)
    return loss, dx, dict(ab_norm=d_gn, W_in=d_win, ab_q_norm=d_qn, Wq=d_wq, ab_kv_norm=d_kvn, Wk=d_wk, Wv=d_wv)


def _block_diag(wg):
    g, n, _ = wg.shape
    return jnp.einsum('gij,gh->gihj', wg, jnp.eye(g, dtype=wg.dtype)).reshape(g * n, g * n)


def _prepare_out(w_out):
    d = w_out.shape[2]
    mla = HEADS * QK_NOPE
    return {'Wo_a': jnp.pad(w_out[0, :mla].reshape(HEADS, QK_NOPE, d), ((0, 0), (0, HEAD_PAD - QK_NOPE), (0, 0))).reshape(HEADS * HEAD_PAD, d),
            'Wo_b': w_out[0, mla:]}


def _prepare(full):
    d = full['ab_w_in'].shape[1]
    w_in = full['ab_w_in'][0]
    zeros = lambda n: jnp.zeros((d, n), w_in.dtype)
    wq = full['ab_w_q_b'][0].reshape(Q_LORA, HEADS, QK_NOPE + QK_ROPE)
    wkv = full['ab_w_kv_b'][0].reshape(KV_LORA, HEADS, 2 * QK_NOPE)
    pad_head = lambda a: jnp.pad(a, ((0, 0), (0, 0), (0, HEAD_PAD - a.shape[2]))).reshape(a.shape[0], HEADS * HEAD_PAD)
    w = {
        'W_in': jnp.concatenate([w_in[:, :Z_KPE], zeros(QK_NOPE), w_in[:, Z_KPE:Z_KPE + QK_ROPE],
                                 zeros(HEAD_PAD - QK_NOPE - QK_ROPE), w_in[:, Z_KPE + QK_ROPE:]], axis=1),
        'Wq': pad_head(wq), 'Wk': pad_head(wkv[:, :, :QK_NOPE]), 'Wv': pad_head(wkv[:, :, QK_NOPE:]),
        'Wa': _bf(_block_diag(full['ab_w_rg_a'][0])), 'Wx': _bf(_block_diag(full['ab_w_rg_x'][0])),
        'c_w_s': full['c_w_s'][0],
        'bsT': jnp.pad(full['c_b_s'][0].T, ((0, 0), (0, LANES - SGU_GROUPS))),
        'ffn_norm': [full['ffn_norm'][l:l + 1] for l in range(2)], 'ffn_conv_w': [full['ffn_conv_w'][l] for l in range(2)],
        'ffn_conv_b': [full['ffn_conv_b'][l:l + 1] for l in range(2)],
        'ab_conv_w': full['ab_conv_w'][0], 'final_norm': full['final_norm'][None, :],
    }
    for name in ('ab_norm', 'ab_q_norm', 'ab_kv_norm', 'ab_conv_b', 'ab_b_rg_a', 'ab_b_rg_x', 'ab_lambda', 'c_norm', 'c_ln_g', 'c_ln_b'):
        w[name] = full[name]
    return w


def _unprepare(g):
    unpad_head = lambda a, n: a.reshape(a.shape[0], HEADS, HEAD_PAD)[:, :, :n]
    diag = lambda a: jnp.einsum('gigj->gij', a.reshape(HEADS, LRU_W // HEADS, HEADS, LRU_W // HEADS))
    rules = {
        'ab_w_in': (('W_in',), lambda a: jnp.concatenate([a[:, :Z_KPE], a[:, Z_KPE + QK_NOPE:Z_KPE + QK_NOPE + QK_ROPE], a[:, Z_LRU:]], axis=1)[None]),
        'ab_w_q_b': (('Wq',), lambda a: unpad_head(a, QK_NOPE + QK_ROPE).reshape(1, Q_LORA, -1)),
        'ab_w_kv_b': (('Wk', 'Wv'), lambda a, b: jnp.concatenate([unpad_head(a, QK_NOPE), unpad_head(b, QK_NOPE)], axis=2).reshape(1, KV_LORA, -1)),
        'ab_w_out': (('Wo_a', 'Wo_b'), lambda a, b: jnp.concatenate(
            [a.reshape(HEADS, HEAD_PAD, -1)[:, :QK_NOPE].reshape(HEADS * QK_NOPE, -1), b], axis=0)[None]),
        'ab_w_rg_a': (('Wa',), lambda a: diag(a)[None]), 'ab_w_rg_x': (('Wx',), lambda a: diag(a)[None]),
        'c_w_in': (('c_w_in',), lambda a: a[None]), 'c_w_out': (('c_w_out',), lambda a: a[None]), 'c_w_s': (('c_w_s',), lambda a: a[None]),
        'c_b_s': (('bsT',), lambda a: a[:, :SGU_GROUPS].T[None]),
        'ffn_w_gate': (('Wg',), jnp.stack), 'ffn_w_up': (('Wu',), jnp.stack), 'ffn_w_down': (('Wd',), jnp.stack),
        'ffn_norm': (('ffn_norm',), lambda a: jnp.concatenate(a, axis=0)), 'ffn_conv_w': (('ffn_conv_w',), jnp.stack),
        'ffn_conv_b': (('ffn_conv_b',), lambda a: jnp.concatenate(a, axis=0)),
        'ab_conv_w': (('ab_conv_w',), lambda a: a[None]), 'final_norm': (('final_norm',), lambda a: a[0]),
    }
    for name in ('ab_norm', 'ab_q_norm', 'ab_kv_norm', 'ab_conv_b', 'ab_b_rg_a', 'ab_b_rg_x', 'ab_lambda', 'c_norm', 'c_ln_g', 'c_ln_b'):
        rules[name] = ((name,), lambda a: a)
    return {name: fn(*[g[k] for k in keys]) for name, (keys, fn) in rules.items() if all(k in g for k in keys)}


SLAB_ROWS = 16


def _round_up(n, m):
    return -(-n // m) * m


def _to_chunks(full, axis):
    s = full.shape
    return jnp.moveaxis(full.reshape(s[:axis] + (N_DEV, s[axis] // N_DEV) + s[axis + 1:]), axis, 0)


def _from_chunks(chunks, axis):
    local = chunks.shape[1:]
    return jnp.moveaxis(chunks, 0, axis).reshape(local[:axis] + (N_DEV * local[axis],) + local[axis + 1:])


def _slab_rows(n):
    return _round_up(-(-n // LANES), SLAB_ROWS)


def _to_slab(a, lead):
    a = a.reshape(lead + (-1,))
    rows = _slab_rows(a.shape[-1])
    a = jnp.pad(a, [(0, 0)] * len(lead) + [(0, rows * LANES - a.shape[-1])])
    return a.reshape(lead + (rows, LANES))


def _pack_slabs(parts, lead):
    return jnp.concatenate([_to_slab(p, lead) for p in parts], axis=len(lead))


def _unpack_slabs(packed, shapes):
    lead = packed.shape[:-2]
    out, row = [], 0
    for shape in shapes:
        size = math.prod(shape)
        rows = _slab_rows(size)
        piece = lax.slice_in_dim(packed, row, row + rows, axis=len(lead))
        out.append(piece.reshape(lead + (rows * LANES,))[..., :size].reshape(lead + tuple(shape)))
        row += rows
    return out


HBM = pl.BlockSpec(memory_space=pl.ANY)


def _other_chips(x, y):
    return [(1 - x, y), (x, 1 - y), (1 - x, 1 - y)]


def _all_gather(blocks):
    n = len(blocks)

    def body(*refs):
        x_refs, out_refs = refs[:n], refs[n:2 * n]
        send_sems, recv_sems, local_sems = refs[2 * n:]
        x, y, c = lax.axis_index("x"), lax.axis_index("y"), lax.axis_index("c")
        me, sibling = (x, y, c), (x, y, 1 - c)
        chips = _other_chips(x, y)

        def slab(a, px, py, pc):
            return out_refs[a].at[4 * px + 2 * py + pc]

        def copy(a, k, blk, to, src=None):
            return pltpu.make_async_remote_copy(src_ref=slab(a, *blk) if src is None else src, dst_ref=slab(a, *blk),
                                                send_sem=send_sems.at[7 * a + k], recv_sem=recv_sems.at[7 * a + k],
                                                device_id=to, device_id_type=MESH)

        mine = [pltpu.make_async_copy(x_refs[a], slab(a, *me), local_sems.at[a]) for a in range(n)]
        started = []
        for a in range(n):
            mine[a].start()
            started.append(copy(a, 0, me, sibling, src=x_refs[a]))
            started += [copy(a, 1 + j, me, (*chip, c), src=x_refs[a]) for j, chip in enumerate(chips)]
        for cp in started:
            cp.start()
        for j, chip in enumerate(chips):
            for a in range(n):
                copy(a, 1 + j, (*chip, c), me).wait_recv()
                passed = copy(a, 4 + j, (*chip, c), sibling)
                passed.start()
                started.append(passed)
        for a in range(n):
            copy(a, 0, sibling, me).wait_recv()
        for j, chip in enumerate(chips):
            for a in range(n):
                copy(a, 4 + j, (*chip, 1 - c), me).wait_recv()
        for cp in started:
            cp.wait_send()
        for a in range(n):
            mine[a].wait()

    return pl.pallas_call(
        body, name="all_gather_weights", out_shape=[jax.ShapeDtypeStruct((N_DEV,) + b.shape, b.dtype) for b in blocks],
        in_specs=[HBM] * n, out_specs=[HBM] * n,
        scratch_shapes=[pltpu.SemaphoreType.DMA((7 * n,)), pltpu.SemaphoreType.DMA((7 * n,)), pltpu.SemaphoreType.DMA((n,))],
    )(*blocks)


FLIPS = [(0, 0, 1), (1, 0, 0), (1, 0, 1), (0, 1, 0), (0, 1, 1), (1, 1, 0), (1, 1, 1)]


def _peers(x, y, c):
    flip = lambda v, f: 1 - v if f else v
    return [(flip(x, fx), flip(y, fy), flip(c, fc)) for fx, fy, fc in FLIPS]


def _direct_copies(src_refs, land_refs, send_sems, recv_sems, scatter):
    x, y, c = lax.axis_index("x"), lax.axis_index("y"), lax.axis_index("c")
    me = 4 * x + 2 * y + c
    starts, waits = [], []
    for a in range(len(src_refs)):
        for k, (px, py, pc) in enumerate(_peers(x, y, c)):
            peer = 4 * px + 2 * py + pc
            sems = dict(send_sem=send_sems.at[7 * a + k], recv_sem=recv_sems.at[7 * a + k], device_id=(px, py, pc), device_id_type=MESH)
            src = src_refs[a].at[peer] if scatter else src_refs[a]
            starts.append(pltpu.make_async_remote_copy(src_ref=src, dst_ref=land_refs[a].at[me], **sems))
            waits.append(pltpu.make_async_remote_copy(src_ref=src, dst_ref=land_refs[a].at[peer], **sems))
    return starts, waits


def _landing(src, scatter):
    block = src.shape[1:] if scatter else src.shape
    return jax.ShapeDtypeStruct((N_DEV,) + block, src.dtype)


HBM_SPACE = pl.BlockSpec(memory_space=pltpu.HBM)
SEMAPHORES = pl.BlockSpec(memory_space=pltpu.SEMAPHORE)
SPLIT_EFFECT = pltpu.SideEffectType.DATAFLOW_SIDE_EFFECTING


def _start_exchange(name, srcs, scatter):
    n = len(srcs)
    lands = [lax.empty(s.shape, s.dtype) for s in (_landing(s, scatter) for s in srcs)]

    def body(*refs):
        starts, _ = _direct_copies(refs[:n], refs[n:2 * n], refs[2 * n], refs[2 * n + 1], scatter)
        for cp in starts:
            cp.start()
        refs[-1][...] = jnp.zeros_like(refs[-1])

    held = [pltpu.with_memory_space_constraint(a, pltpu.HBM) for a in list(srcs) + lands]
    out = pl.pallas_call(
        body, name=name + "_start",
        out_shape=(pltpu.SemaphoreType.DMA((7 * n,)), pltpu.SemaphoreType.DMA((7 * n,)), *[pltpu.HBM(a.shape, a.dtype) for a in held],
                   jax.ShapeDtypeStruct((8, LANES), F32)),
        in_specs=[HBM_SPACE] * (2 * n), out_specs=(SEMAPHORES, SEMAPHORES, *[HBM_SPACE] * (2 * n), pl.BlockSpec(memory_space=pltpu.VMEM)),
        input_output_aliases={i: 2 + i for i in range(2 * n)},
        compiler_params=pltpu.CompilerParams(has_side_effects=SPLIT_EFFECT),
    )(*held)
    return out[0], out[1], list(out[2:2 + n]), list(out[2 + n:2 + 2 * n]), out[-1][0, 0], out[-1]


def _wait_exchange(name, started, after, scatter):
    send_sems, recv_sems, srcs, lands = started[:4]
    n = len(srcs)

    def body(*refs):
        _, waits = _direct_copies(refs[:n], refs[n:2 * n], refs[2 * n], refs[2 * n + 1], scatter)
        for cp in waits:
            cp.wait_send()
        for cp in waits:
            cp.wait_recv()

    out = pl.pallas_call(
        body, name=name + "_wait", out_shape=tuple(pltpu.HBM(a.shape, a.dtype) for a in srcs + lands),
        in_specs=[HBM_SPACE] * (2 * n) + [SEMAPHORES, SEMAPHORES, HBM], out_specs=tuple([HBM_SPACE] * (2 * n)),
        input_output_aliases={i: i for i in range(2 * n)},
        compiler_params=pltpu.CompilerParams(has_side_effects=SPLIT_EFFECT),
    )(*srcs, *lands, send_sems, recv_sems, after)
    return list(out[:n]), list(out[n:])


def _row_tile(rows):
    return rows // 2 if (rows // 2) % SLAB_ROWS == 0 else rows


def _sum_and_adamw(me, landed, own, wts, m, v, name):
    l, r, n = wts.shape
    tr = _row_tile(r)
    blk = pl.BlockSpec((1, tr, n), lambda li, ri, me_ref: (li, ri, 0))
    c1 = 1.0 / (1.0 - ADAM_B1 ** ADAM_STEP)
    c2 = 1.0 / (1.0 - ADAM_B2 ** ADAM_STEP)

    def body(me_ref, l_ref, own_ref, w_ref, m_ref, v_ref, g_out, d_out, m_out, v_out):
        mine = own_ref[0].astype(F32)
        g = jnp.where(me_ref[0] == 0, mine, l_ref[0].astype(F32))
        for dev in range(1, N_DEV):
            g = g + jnp.where(me_ref[0] == dev, mine, l_ref[dev].astype(F32))
        m_new = ADAM_B1 * m_ref[...] + (1.0 - ADAM_B1) * g
        v_new = ADAM_B2 * v_ref[...] + (1.0 - ADAM_B2) * (g * g)
        g_out[...] = g
        m_out[...] = m_new
        v_out[...] = v_new
        d_out[...] = -ADAM_LR * ((m_new * c1) / (jnp.sqrt(v_new * c2) + ADAM_EPS) + ADAM_WD * w_ref[...])

    return pl.pallas_call(
        body, name="adamw_" + name,
        grid_spec=pltpu.PrefetchScalarGridSpec(
            num_scalar_prefetch=1, grid=(l, r // tr),
            in_specs=[pl.BlockSpec((N_DEV, 1, tr, n), lambda li, ri, me_ref: (0, li, ri, 0)),
                      pl.BlockSpec((1, 1, tr, n), lambda li, ri, me_ref: (me_ref[0], li, ri, 0)), blk, blk, blk],
            out_specs=[blk] * 4),
        out_shape=[_sds((l, r, n))] * 4, compiler_params=_params(2))(me, landed, own, wts, m, v)


EARLY = ['ab_w_in']
LATE_STAGES = {
    'out0': [('ab_w_out', None, 'ab_w_out')],
    'ffn0': [('ffn_w_gate', 0, 'Wg'), ('ffn_w_up', 0, 'Wu'), ('ffn_w_down', 0, 'Wd')],
    'mix1': [('c_w_in', None, 'c_w_in'), ('c_w_out', None, 'c_w_out')],
    'ffn1': [('ffn_w_gate', 1, 'Wg'), ('ffn_w_up', 1, 'Wu'), ('ffn_w_down', 1, 'Wd')],
}
GRAD_STAGES = {
    'late': (['c_w_in', 'c_w_out', 'ffn_w_gate', 'ffn_w_up', 'ffn_w_down'],
             ['c_norm', 'c_ln_g', 'c_ln_b', 'c_w_s', 'c_b_s', 'ffn_norm', 'ffn_conv_w', 'ffn_conv_b', 'final_norm']),
    'mid': (['ab_w_out'], ['ab_conv_w', 'ab_conv_b', 'ab_w_rg_a', 'ab_b_rg_a', 'ab_w_rg_x', 'ab_b_rg_x', 'ab_lambda']),
    'last': (['ab_w_in'], ['ab_norm', 'ab_q_norm', 'ab_w_q_b', 'ab_kv_norm', 'ab_w_kv_b']),
}


def _gather_early(local):
    small = [_bf(local[n]) if n in MATRICES else lax.bitcast_convert_type(local[n], BF16) for n in SMALL_SHARDED]
    gathered = _all_gather([_bf(local[n]) for n in EARLY] + [_pack_slabs(small, ())])
    full = {n: local[n] for n in REPLICATED}
    for n, g in zip(EARLY, gathered):
        full[n] = _from_chunks(g, SHARD_AXIS[n])
    for n, p in zip(SMALL_SHARDED, _unpack_slabs(gathered[-1], [s.shape for s in small])):
        full[n] = _from_chunks(p if n in MATRICES else lax.bitcast_convert_type(p, F32), SHARD_AXIS[n])
    return full


def kernel(x, positions, ab_norm, ab_w_in, ab_q_norm, ab_w_q_b, ab_kv_norm, ab_w_kv_b, ab_conv_w, ab_conv_b, ab_w_rg_a, ab_b_rg_a, ab_w_rg_x, ab_b_rg_x, ab_lambda, ab_w_out, c_norm, c_w_in, c_ln_g, c_ln_b, c_w_s, c_b_s, c_w_out, ffn_norm, ffn_w_gate, ffn_w_up, ffn_conv_w, ffn_conv_b, ffn_w_down, final_norm, loss_target, m_ab_norm, m_ab_w_in, m_ab_q_norm, m_ab_w_q_b, m_ab_kv_norm, m_ab_w_kv_b, m_ab_conv_w, m_ab_conv_b, m_ab_w_rg_a, m_ab_b_rg_a, m_ab_w_rg_x, m_ab_b_rg_x, m_ab_lambda, m_ab_w_out, m_c_norm, m_c_w_in, m_c_ln_g, m_c_ln_b, m_c_w_s, m_c_b_s, m_c_w_out, m_ffn_norm, m_ffn_w_gate, m_ffn_w_up, m_ffn_conv_w, m_ffn_conv_b, m_ffn_w_down, m_final_norm, v_ab_norm, v_ab_w_in, v_ab_q_norm, v_ab_w_q_b, v_ab_kv_norm, v_ab_w_kv_b, v_ab_conv_w, v_ab_conv_b, v_ab_w_rg_a, v_ab_b_rg_a, v_ab_w_rg_x, v_ab_b_rg_x, v_ab_lambda, v_ab_w_out, v_c_norm, v_c_w_in, v_c_ln_g, v_c_ln_b, v_c_w_s, v_c_b_s, v_c_w_out, v_ffn_norm, v_ffn_w_gate, v_ffn_w_up, v_ffn_conv_w, v_ffn_conv_b, v_ffn_w_down, v_final_norm):
    given = dict(locals())
    local = {n: given[n] for n in WEIGHTS}
    b, seq, d = x.shape
    t = b * seq

    me = (4 * lax.axis_index("x") + 2 * lax.axis_index("y") + lax.axis_index("c")).astype(jnp.int32)
    is_me = (jnp.arange(N_DEV, dtype=jnp.int32) == me).reshape(N_DEV, 1, 1, 1)

    full = _gather_early(local)
    gathers = {}
    zero = jnp.zeros((), F32)
    for stage, members in LATE_STAGES.items():
        srcs = [_bf(local[n] if layer is None else local[n][layer:layer + 1]) for n, layer, _ in members]
        gathers[stage] = _start_exchange('gather_' + stage, srcs, scatter=False)
        zero = zero + gathers[stage][4]
    w = _prepare(full)
    w['ab_norm'] = w['ab_norm'] + zero

    def late_weights(stage, after):
        srcs, lands = _wait_exchange('gather_' + stage, gathers[stage], after, scatter=False)
        whole = [_from_chunks(jnp.where(is_me, s[None], l), SHARD_AXIS[n]) for (n, _, _), s, l in zip(LATE_STAGES[stage], srcs, lands)]
        if stage == 'out0':
            return _prepare_out(whole[0])
        return {key: a[0] for (_, _, key), a in zip(LATE_STAGES[stage], whole)}

    scatters = {}

    def start_scatter(stage, g):
        whole = _unprepare(g)
        big, small = GRAD_STAGES[stage]
        slab = [_to_chunks(whole[n], SHARD_AXIS[n]) if n in SHARD_AXIS else jnp.broadcast_to(whole[n][None], (N_DEV,) + whole[n].shape)
                for n in small]
        own = [_bf(_to_chunks(whole[n], SHARD_AXIS[n])) for n in big] + [_bf(_pack_slabs(slab, (N_DEV,)))[:, None]]
        scatters[stage] = _start_exchange('scatter_' + stage, own, scatter=True)
        return scatters[stage][4]

    posb = jnp.broadcast_to(positions.astype(F32).reshape(t, 1), (t, LANES))
    loss, dx, grads = _local_step(x.reshape(t, d), posb, loss_target.reshape(t, d), w, seq, late_weights, start_scatter)
    start_scatter('last', grads)
    after = scatters['last'][5]

    me1 = me.reshape(1)
    updated = {}
    for stage, (big, small) in GRAD_STAGES.items():
        owns, landed = _wait_exchange('scatter_' + stage, scatters[stage], after, scatter=True)
        for n, own, land in zip(big, owns, landed):
            updated[n] = _sum_and_adamw(me1, land, own, given[n], given['m_' + n], given['v_' + n], n)
        pack_small = lambda prefix: _pack_slabs([given[prefix + n] for n in small], ())[None]
        packed = _sum_and_adamw(me1, landed[-1], owns[-1], pack_small(''), pack_small('m_'), pack_small('v_'), 'small_' + stage)
        unpacked = [_unpack_slabs(p[0], [local[n].shape for n in small]) for p in packed]
        for i, n in enumerate(small):
            updated[n] = [u[i] for u in unpacked]
        after = packed[0]
    total = lax.psum(loss[0, 0], ("x", "y", "c"))
    return (total, dx.reshape(b, seq, d), *[updated[n][kind] for kind in range(4) for n in WEIGHTS])
```

```python
import math

import jax
import jax.numpy as jnp
from jax import lax
from jax.experimental import pallas as pl
from jax.experimental.pallas import tpu as pltpu

F32 = jnp.float32
BF16 = jnp.bfloat16
MESH = pl.DeviceIdType.MESH

N_DEV = 8
LANES = 128
HALO = 8
VMEM_LIMIT = 56 << 20

NORM_EPS = 1e-6
HEADS = 8
HEAD_PAD = 128
QK_NOPE = 64
QK_ROPE = 32
ROPE_HALF = 16
ROPE_BASE = 10000.0
ATTN_SCALE = (QK_NOPE + QK_ROPE) ** -0.5
LRU_C = 8.0
LRU_W = 512
CHUNK = 128
SGU_GROUPS = 8
D_FF = 2816
FF_BLOCKS = 2

ADAM_LR, ADAM_B1, ADAM_B2, ADAM_EPS, ADAM_WD, ADAM_STEP = 0.001, 0.9, 0.999, 1e-08, 0.01, 10

WEIGHTS = ['ab_norm', 'ab_w_in', 'ab_q_norm', 'ab_w_q_b', 'ab_kv_norm', 'ab_w_kv_b', 'ab_conv_w', 'ab_conv_b',
           'ab_w_rg_a', 'ab_b_rg_a', 'ab_w_rg_x', 'ab_b_rg_x', 'ab_lambda', 'ab_w_out', 'c_norm', 'c_w_in', 'c_ln_g',
           'c_ln_b', 'c_w_s', 'c_b_s', 'c_w_out', 'ffn_norm', 'ffn_w_gate', 'ffn_w_up', 'ffn_conv_w', 'ffn_conv_b',
           'ffn_w_down', 'final_norm']
SHARD_AXIS = {'ab_w_in': 2, 'ab_w_q_b': 2, 'ab_w_kv_b': 2, 'ab_conv_w': 2, 'ab_w_out': 1, 'c_norm': 1, 'c_w_in': 2,
              'c_ln_g': 1, 'c_ln_b': 1, 'c_w_out': 1, 'ffn_w_gate': 2, 'ffn_w_up': 2, 'ffn_conv_w': 2, 'ffn_w_down': 1}
MATRICES = ['ab_w_in', 'ab_w_q_b', 'ab_w_kv_b', 'ab_w_out', 'c_w_in', 'c_w_out', 'ffn_w_gate', 'ffn_w_up', 'ffn_w_down']
BIG = ['ab_w_in', 'c_w_in', 'ffn_w_gate', 'ffn_w_up', 'ab_w_out', 'c_w_out', 'ffn_w_down']
REPLICATED = [n for n in WEIGHTS if n not in SHARD_AXIS]
SMALL_SHARDED = [n for n in WEIGHTS if n in SHARD_AXIS and n not in BIG]


def _bf(x):
    return x.astype(BF16)


def _nn(a, b):
    return lax.dot_general(_bf(a), _bf(b), (((1,), (0,)), ((), ())), preferred_element_type=F32)


def _nt(a, b):
    return lax.dot_general(_bf(a), _bf(b), (((1,), (1,)), ((), ())), preferred_element_type=F32)


def _tn(a, b):
    return lax.dot_general(_bf(a), _bf(b), (((0,), (0,)), ((), ())), preferred_element_type=F32)


def _rms(x, g):
    return x * lax.rsqrt(jnp.mean(x * x, axis=-1, keepdims=True) + NORM_EPS) * g


def _layer_norm(x, g, b):
    xc = x - jnp.mean(x, axis=-1, keepdims=True)
    return xc * lax.rsqrt(jnp.mean(xc * xc, axis=-1, keepdims=True) + NORM_EPS) * g + b


def _gelu(x):
    return jax.nn.gelu(x)


def _colsum(x):
    return jnp.sum(x, axis=0, keepdims=True)


def _softplus(x):
    return jnp.maximum(x, 0.0) + jnp.log1p(jnp.exp(-jnp.abs(x)))


@jax.custom_vjp
def _one_minus_exp(x):
    u = jnp.exp(x)
    lg = jnp.log(u)
    near = lg == 0.0
    em1 = jnp.where(near, x, (u - 1.0) * x / jnp.where(near, 1.0, lg))
    return -jnp.where(x < -20.0, u - 1.0, em1)


def _one_minus_exp_fwd(x):
    return _one_minus_exp(x), x


def _one_minus_exp_bwd(x, ct):
    return (-jnp.exp(x) * ct,)


_one_minus_exp.defvjp(_one_minus_exp_fwd, _one_minus_exp_bwd)


def _accumulate(ref, val, first):
    @pl.when(first)
    def _():
        ref[...] = val

    @pl.when(jnp.logical_not(first))
    def _():
        ref[...] += val


def _params(n_axes=1):
    return pltpu.CompilerParams(dimension_semantics=("arbitrary",) * n_axes, vmem_limit_bytes=VMEM_LIMIT)


def _row(tm, n):
    return pl.BlockSpec((tm, n), lambda i: (i, 0))


def _const(shape):
    nd = len(shape)
    return pl.BlockSpec(shape, lambda i: (0,) * nd, pipeline_mode=pl.Buffered(1))


def _prev_halo(tm, n):
    return pl.BlockSpec((HALO, n), lambda i: (jnp.maximum(i * (tm // HALO) - 1, 0), 0))


def _next_halo(tm, n, n_tiles):
    last = n_tiles * (tm // HALO) - 1
    return pl.BlockSpec((HALO, n), lambda i: (jnp.minimum((i + 1) * (tm // HALO), last), 0))


def _sds(shape, dtype=F32):
    return jax.ShapeDtypeStruct(shape, dtype)


def _rope_tables(posb):
    lane = lax.broadcasted_iota(jnp.int32, posb.shape, 1)
    in_rope = jnp.logical_and(lane >= QK_NOPE, lane < QK_NOPE + QK_ROPE)
    j = (lane & (ROPE_HALF - 1)).astype(F32)
    inv_freq = jnp.exp((-math.log(ROPE_BASE)) * j / ROPE_HALF)
    ang = posb * inv_freq
    return jnp.where(in_rope, jnp.cos(ang), 1.0), jnp.where(in_rope, jnp.sin(ang), 0.0)


def _rot(q):
    n = q.shape[1]
    lane = lax.broadcasted_iota(jnp.int32, q.shape, 1) & (HEAD_PAD - 1)
    first_half = jnp.where(lane >= QK_NOPE, -pltpu.roll(q, n - ROPE_HALF, 1), 0.0)
    second_half = jnp.where(lane < QK_NOPE + QK_ROPE, pltpu.roll(q, ROPE_HALF, 1), 0.0)
    return jnp.where(lane < QK_NOPE + ROPE_HALF, first_half, second_half)


def _rope(q, cos_t, sin_t):
    return q * cos_t + _rot(q) * sin_t


def _rope_transpose(dq, cos_t, sin_t):
    return dq * cos_t - _rot(dq * sin_t)


def _tile_heads(t):
    return jnp.concatenate([t] * HEADS, axis=1)


Q_LORA, KV_LORA = 256, 128
Z_KPE = Q_LORA + KV_LORA
Z_LRU = Z_KPE + HEAD_PAD
Z_GATE = Z_LRU + LRU_W
Z_WIDTH = Z_GATE + LRU_W


def _ab_in_fwd(x, posb, w, tm):
    t, d = x.shape

    def body(x_ref, pos_ref, gn_ref, win_ref, qn_ref, wq_ref, kvn_ref, wk_ref, wv_ref, q_out, k_out, v_out, xl_out, gate_out):
        hn = _rms(x_ref[...], gn_ref[...])
        z = _nn(hn, win_ref[...])
        cqn = _rms(z[:, :Q_LORA], qn_ref[...])
        kvn = _rms(z[:, Q_LORA:Z_KPE], kvn_ref[...])
        cos_t, sin_t = _rope_tables(pos_ref[...])
        q_out[...] = _rope(_nn(cqn, wq_ref[...]), _tile_heads(cos_t), _tile_heads(sin_t))
        kpe = _rope(z[:, Z_KPE:Z_LRU], cos_t, sin_t)
        k_out[...] = _nn(kvn, wk_ref[...]) + _tile_heads(kpe)
        v_out[...] = _nn(kvn, wv_ref[...])
        xl_out[...] = z[:, Z_LRU:Z_GATE]
        gate_out[...] = z[:, Z_GATE:]

    hp = HEADS * HEAD_PAD
    return pl.pallas_call(
        body, name="ab_in_fwd", grid=(t // tm,),
        in_specs=[_row(tm, d), _row(tm, LANES), _const((1, d)), _const((d, Z_WIDTH)), _const((1, Q_LORA)), _const((Q_LORA, hp)),
                  _const((1, KV_LORA)), _const((KV_LORA, hp)), _const((KV_LORA, hp))],
        out_specs=[_row(tm, hp), _row(tm, hp), _row(tm, hp), _row(tm, LRU_W), _row(tm, LRU_W)],
        out_shape=[_sds((t, hp)), _sds((t, hp)), _sds((t, hp)), _sds((t, LRU_W)), _sds((t, LRU_W))],
        compiler_params=_params(),
    )(x, posb, w['ab_norm'], w['W_in'], w['ab_q_norm'], w['Wq'], w['ab_kv_norm'], w['Wk'], w['Wv'])


def _ab_in_bwd(x, posb, w, dq, dk, dv, dxl, dgate, dres, tm):
    t, d = x.shape
    hp = HEADS * HEAD_PAD

    def body(x_ref, pos_ref, gn_ref, win_ref, qn_ref, wq_ref, kvn_ref, wk_ref, wv_ref, dq_ref, dk_ref, dv_ref, dxl_ref, dgate_ref,
             dres_ref, dx_out, dgn_out, dwin_out, dqn_out, dwq_out, dkvn_out, dwk_out, dwv_out):
        first = pl.program_id(0) == 0
        hn, vjp_in = jax.vjp(_rms, x_ref[...], gn_ref[...])
        z = _nn(hn, win_ref[...])
        cqn, vjp_q = jax.vjp(_rms, z[:, :Q_LORA], qn_ref[...])
        kvn, vjp_kv = jax.vjp(_rms, z[:, Q_LORA:Z_KPE], kvn_ref[...])
        cos_t, sin_t = _rope_tables(pos_ref[...])
        dq0 = _rope_transpose(dq_ref[...], _tile_heads(cos_t), _tile_heads(sin_t))
        dk0 = dk_ref[...]
        dv0 = dv_ref[...]
        dkpe = dk0[:, :HEAD_PAD]
        for h in range(1, HEADS):
            dkpe = dkpe + dk0[:, h * HEAD_PAD:(h + 1) * HEAD_PAD]
        dkpe = _rope_transpose(dkpe, cos_t, sin_t)
        _accumulate(dwq_out, _tn(cqn, dq0), first)
        _accumulate(dwk_out, _tn(kvn, dk0), first)
        _accumulate(dwv_out, _tn(kvn, dv0), first)
        dcq, dqn = vjp_q(_nt(dq0, wq_ref[...]))
        dckv, dkvn = vjp_kv(_nt(dk0, wk_ref[...]) + _nt(dv0, wv_ref[...]))
        _accumulate(dqn_out, dqn, first)
        _accumulate(dkvn_out, dkvn, first)
        dz = jnp.concatenate([dcq, dckv, dkpe, dxl_ref[...], dgate_ref[...]], axis=1)
        _accumulate(dwin_out, _tn(hn, dz), first)
        dx, dgn = vjp_in(_nt(dz, win_ref[...]))
        _accumulate(dgn_out, dgn, first)
        dx_out[...] = dx + dres_ref[...]

    return pl.pallas_call(
        body, name="ab_in_bwd", grid=(t // tm,),
        in_specs=[_row(tm, d), _row(tm, LANES), _const((1, d)), _const((d, Z_WIDTH)), _const((1, Q_LORA)), _const((Q_LORA, hp)),
                  _const((1, KV_LORA)), _const((KV_LORA, hp)), _const((KV_LORA, hp)),
                  _row(tm, hp), _row(tm, hp), _row(tm, hp), _row(tm, LRU_W), _row(tm, LRU_W), _row(tm, d)],
        out_specs=[_row(tm, d), _const((1, d)), _const((d, Z_WIDTH)), _const((1, Q_LORA)), _const((Q_LORA, hp)),
                   _const((1, KV_LORA)), _const((KV_LORA, hp)), _const((KV_LORA, hp))],
        out_shape=[_sds((t, d)), _sds((1, d)), _sds((d, Z_WIDTH)), _sds((1, Q_LORA)), _sds((Q_LORA, hp)),
                   _sds((1, KV_LORA)), _sds((KV_LORA, hp)), _sds((KV_LORA, hp))],
        compiler_params=_params(),
    )(x, posb, w['ab_norm'], w['W_in'], w['ab_q_norm'], w['Wq'], w['ab_kv_norm'], w['Wk'], w['Wv'], dq, dk, dv, dxl, dgate, dres)


def _attn_probs(q_blk, k_ext, i, tq):
    ext = k_ext.shape[0]
    s = lax.dot_general(q_blk, k_ext, (((1,), (1,)), ((), ())), preferred_element_type=F32) * ATTN_SCALE
    row = lax.broadcasted_iota(jnp.int32, (tq, ext), 0) + i * tq
    col = lax.broadcasted_iota(jnp.int32, (tq, ext), 1)
    s = jnp.where(col <= row, s, -1e30)
    p = jnp.exp(s - jnp.max(s, axis=1, keepdims=True))
    return p / jnp.sum(p, axis=1, keepdims=True)


def _attn_fwd(q, k, v, tq):
    b, s, hp = q.shape
    blk = pl.BlockSpec((1, s, HEAD_PAD), lambda bi, h: (bi, 0, h))

    def body(q_ref, k_ref, v_ref, o_ref):
        kb = _bf(k_ref[0])
        vb = _bf(v_ref[0])
        for i in range(s // tq):
            ext = (i + 1) * tq
            p = _attn_probs(_bf(q_ref[0, i * tq:ext, :]), kb[:ext], i, tq)
            o_ref[0, i * tq:ext, :] = lax.dot_general(_bf(p), vb[:ext], (((1,), (0,)), ((), ())), preferred_element_type=F32)

    return pl.pallas_call(body, name="attn_fwd", grid=(b, HEADS), in_specs=[blk, blk, blk], out_specs=blk,
                          out_shape=_sds((b, s, hp)), compiler_params=_params(2))(q, k, v)


def _attn_bwd(q, k, v, do, tq):
    b, s, hp = q.shape
    blk = pl.BlockSpec((1, s, HEAD_PAD), lambda bi, h: (bi, 0, h))

    def body(q_ref, k_ref, v_ref, do_ref, dq_ref, dk_ref, dv_ref):
        kb = _bf(k_ref[0])
        vb = _bf(v_ref[0])
        dk_ref[...] = jnp.zeros_like(dk_ref)
        dv_ref[...] = jnp.zeros_like(dv_ref)
        for i in range(s // tq):
            ext = (i + 1) * tq
            qb = _bf(q_ref[0, i * tq:ext, :])
            dob = _bf(do_ref[0, i * tq:ext, :])
            p = _attn_probs(qb, kb[:ext], i, tq)
            dv_ref[0, :ext, :] += lax.dot_general(_bf(p), dob, (((0,), (0,)), ((), ())), preferred_element_type=F32)
            dp = lax.dot_general(dob, vb[:ext], (((1,), (1,)), ((), ())), preferred_element_type=F32)
            ds = _bf(p * (dp - jnp.sum(p * dp, axis=1, keepdims=True)) * ATTN_SCALE)
            dq_ref[0, i * tq:ext, :] = lax.dot_general(ds, kb[:ext], (((1,), (0,)), ((), ())), preferred_element_type=F32)
            dk_ref[0, :ext, :] += lax.dot_general(ds, qb, (((0,), (0,)), ((), ())), preferred_element_type=F32)

    return pl.pallas_call(body, name="attn_bwd", grid=(b, HEADS), in_specs=[blk, blk, blk, blk], out_specs=[blk, blk, blk],
                          out_shape=[_sds((b, s, hp))] * 3, compiler_params=_params(2))(q, k, v, do)


LRU_CONV = 4


def _lru_point(pre_a, pre_x, xc, lam):
    r = jax.nn.sigmoid(pre_a)
    i = jax.nn.sigmoid(pre_x)
    log_a = -LRU_C * r * _softplus(-lam)
    return jnp.exp(log_a), jnp.sqrt(_one_minus_exp(2.0 * log_a)) * (i * xc)


def _causal_conv(pad_ref, x, halo, first_in_seq, w, taps):
    tm = x.shape[0]
    pad_ref[:HALO, :] = jnp.where(first_in_seq, 0.0, halo)
    pad_ref[HALO:, :] = x
    y = w[taps - 1:taps, :] * x
    for k in range(taps - 1):
        off = HALO - (taps - 1) + k
        y = y + w[k:k + 1, :] * pad_ref[off:off + tm, :]
    return y


def _causal_conv_wgrad(pad_ref, dy, taps):
    tm = dy.shape[0]
    return jnp.concatenate([_colsum(dy * pad_ref[HALO - (taps - 1) + k:HALO - (taps - 1) + k + tm, :]) for k in range(taps)], axis=0)


def _causal_conv_transpose(pad_ref, dy, halo_next, last_in_seq, w, taps):
    tm = dy.shape[0]
    pad_ref[:tm, :] = dy
    pad_ref[tm:, :] = jnp.where(last_in_seq, 0.0, halo_next)
    dx = w[taps - 1:taps, :] * dy
    for k in range(taps - 1):
        off = (taps - 1) - k
        dx = dx + w[k:k + 1, :] * pad_ref[off:off + tm, :]
    return dx


def _lru_fwd(xl, gate, w, ts, seq):
    t, n = xl.shape
    tiles_per_seq = seq // ts

    def body(xl_ref, halo_ref, gate_ref, cw_ref, cb_ref, wa_ref, ba_ref, wx_ref, bx_ref, lam_ref, y_out, h_out, pad_ref, a_ref, b_ref, carry_ref):
        first_in_seq = pl.program_id(0) % tiles_per_seq == 0
        xc = _causal_conv(pad_ref, xl_ref[...], halo_ref[...], first_in_seq, cw_ref[...], LRU_CONV) + cb_ref[...]
        a, bx = _lru_point(_nn(xc, wa_ref[...]) + ba_ref[...], _nn(xc, wx_ref[...]) + bx_ref[...], xc, lam_ref[...])
        a_ref[...] = a
        b_ref[...] = bx

        @pl.when(first_in_seq)
        def _():
            carry_ref[...] = jnp.zeros_like(carry_ref)

        def step(r, h):
            h = a_ref[pl.ds(r, 1), :] * h + b_ref[pl.ds(r, 1), :]
            h_out[pl.ds(r, 1), :] = h
            return h

        carry_ref[...] = lax.fori_loop(0, ts, step, carry_ref[...], unroll=8)
        y_out[...] = h_out[...] * _gelu(gate_ref[...])

    return pl.pallas_call(
        body, name="lru_fwd", grid=(t // ts,),
        in_specs=[_row(ts, n), _prev_halo(ts, n), _row(ts, n), _const((LRU_CONV, n)), _const((1, n)), _const((n, n)), _const((1, n)),
                  _const((n, n)), _const((1, n)), _const((1, n))],
        out_specs=[_row(ts, n), _row(ts, n)], out_shape=[_sds((t, n)), _sds((t, n))],
        scratch_shapes=[pltpu.VMEM((HALO + ts, n), F32), pltpu.VMEM((ts, n), F32), pltpu.VMEM((ts, n), F32), pltpu.VMEM((1, n), F32)],
        compiler_params=_params(),
    )(xl, xl, gate, w['ab_conv_w'], w['ab_conv_b'], w['Wa'], w['ab_b_rg_a'], w['Wx'], w['ab_b_rg_x'], w['ab_lambda'])


def _lru_bwd(xl, gate, hs, dy, w, ts, seq):
    t, n = xl.shape
    tiles_per_seq = seq // ts
    n_tiles = t // ts

    def rev(i):
        return n_tiles - 1 - i

    row = pl.BlockSpec((ts, n), lambda i: (rev(i), 0))
    prev = pl.BlockSpec((HALO, n), lambda i: (jnp.maximum(rev(i) * (ts // HALO) - 1, 0), 0))
    acc = lambda shape: pl.BlockSpec(shape, lambda i: (0,) * len(shape))

    def body(xl_ref, xhalo_ref, gate_ref, h_ref, hhalo_ref, dy_ref, cw_ref, cb_ref, wa_ref, ba_ref, wx_ref, bx_ref, lam_ref,
             dxl_out, dgate_out, dcw_out, dcb_out, dwa_out, dba_out, dwx_out, dbx_out, dlam_out,
             pad_ref, padh_ref, padd_ref, a_ref, g_ref, carry_ref, dhalo_ref):
        step_id = pl.program_id(0)
        first = step_id == 0
        tile = rev(step_id)
        first_in_seq = tile % tiles_per_seq == 0
        last_in_seq = tile % tiles_per_seq == tiles_per_seq - 1
        cw = cw_ref[...]
        xc = _causal_conv(pad_ref, xl_ref[...], xhalo_ref[...], first_in_seq, cw, LRU_CONV) + cb_ref[...]
        pre_a = _nn(xc, wa_ref[...]) + ba_ref[...]
        pre_x = _nn(xc, wx_ref[...]) + bx_ref[...]
        (a, _), vjp_point = jax.vjp(_lru_point, pre_a, pre_x, xc, lam_ref[...])
        h = h_ref[...]
        _, vjp_out = jax.vjp(lambda h_, g_: h_ * _gelu(g_), h, gate_ref[...])
        dh, dgate = vjp_out(dy_ref[...])
        dgate_out[...] = dgate
        a_ref[...] = a
        g_ref[...] = dh

        @pl.when(last_in_seq)
        def _():
            carry_ref[...] = jnp.zeros_like(carry_ref)

        def step(j, c):
            r = ts - 1 - j
            g = g_ref[pl.ds(r, 1), :] + c
            g_ref[pl.ds(r, 1), :] = g
            return a_ref[pl.ds(r, 1), :] * g

        carry_ref[...] = lax.fori_loop(0, ts, step, carry_ref[...], unroll=8)
        g = g_ref[...]
        padh_ref[:HALO, :] = jnp.where(first_in_seq, 0.0, hhalo_ref[...])
        padh_ref[HALO:, :] = h
        dpre_a, dpre_x, dxc, dlam = vjp_point((g * padh_ref[HALO - 1:HALO - 1 + ts, :], g))
        dxc = dxc + _nt(dpre_a, wa_ref[...]) + _nt(dpre_x, wx_ref[...])
        _accumulate(dwa_out, _tn(xc, dpre_a), first)
        _accumulate(dwx_out, _tn(xc, dpre_x), first)
        _accumulate(dba_out, _colsum(dpre_a), first)
        _accumulate(dbx_out, _colsum(dpre_x), first)
        _accumulate(dlam_out, dlam, first)
        _accumulate(dcb_out, _colsum(dxc), first)
        _accumulate(dcw_out, _causal_conv_wgrad(pad_ref, dxc, LRU_CONV), first)
        dxl_out[...] = _causal_conv_transpose(padd_ref, dxc, dhalo_ref[...], last_in_seq, cw, LRU_CONV)
        dhalo_ref[...] = dxc[:HALO, :]

    return pl.pallas_call(
        body, name="lru_bwd", grid=(n_tiles,),
        in_specs=[row, prev, row, row, prev, row, _const((LRU_CONV, n)), _const((1, n)), _const((n, n)), _const((1, n)),
                  _const((n, n)), _const((1, n)), _const((1, n))],
        out_specs=[row, row, acc((LRU_CONV, n)), acc((1, n)), acc((n, n)), acc((1, n)), acc((n, n)), acc((1, n)), acc((1, n))],
        out_shape=[_sds((t, n)), _sds((t, n)), _sds((LRU_CONV, n)), _sds((1, n)), _sds((n, n)), _sds((1, n)), _sds((n, n)),
                   _sds((1, n)), _sds((1, n))],
        scratch_shapes=[pltpu.VMEM((HALO + ts, n), F32), pltpu.VMEM((HALO + ts, n), F32), pltpu.VMEM((ts + HALO, n), F32),
                        pltpu.VMEM((ts, n), F32), pltpu.VMEM((ts, n), F32), pltpu.VMEM((1, n), F32), pltpu.VMEM((HALO, n), F32)],
        compiler_params=_params(),
    )(xl, xl, gate, hs, hs, dy, w['ab_conv_w'], w['ab_conv_b'], w['Wa'], w['ab_b_rg_a'], w['Wx'], w['ab_b_rg_x'], w['ab_lambda'])


def _ab_out_fwd(x, o, y, w, tm):
    t, d = x.shape
    hp = o.shape[1]

    def body(x_ref, o_ref, y_ref, wa_ref, wb_ref, h_out):
        h_out[...] = x_ref[...] + _nn(o_ref[...], wa_ref[...]) + _nn(y_ref[...], wb_ref[...])

    return pl.pallas_call(body, name="ab_out_fwd", grid=(t // tm,),
                          in_specs=[_row(tm, d), _row(tm, hp), _row(tm, LRU_W), _const((hp, d)), _const((LRU_W, d))],
                          out_specs=_row(tm, d), out_shape=_sds((t, d)), compiler_params=_params())(x, o, y, w['Wo_a'], w['Wo_b'])


def _ab_out_bwd(o, y, dh, w, tm):
    t, d = dh.shape
    hp = o.shape[1]

    def body(o_ref, y_ref, dh_ref, wa_ref, wb_ref, do_out, dy_out, dwa_out, dwb_out):
        first = pl.program_id(0) == 0
        dh_t = dh_ref[...]
        do_out[...] = _nt(dh_t, wa_ref[...])
        dy_out[...] = _nt(dh_t, wb_ref[...])
        _accumulate(dwa_out, _tn(o_ref[...], dh_t), first)
        _accumulate(dwb_out, _tn(y_ref[...], dh_t), first)

    return pl.pallas_call(body, name="ab_out_bwd", grid=(t // tm,),
                          in_specs=[_row(tm, hp), _row(tm, LRU_W), _row(tm, d), _const((hp, d)), _const((LRU_W, d))],
                          out_specs=[_row(tm, hp), _row(tm, LRU_W), _const((hp, d)), _const((LRU_W, d))],
                          out_shape=[_sds((t, hp)), _sds((t, LRU_W)), _sds((hp, d)), _sds((LRU_W, d))],
                          compiler_params=_params())(o, y, dh, w['Wo_a'], w['Wo_b'])


FFN_CONV = 3


def _ffn_a_fwd(h, norm, wg, wu, tm):
    t, d = h.shape
    fb = D_FF // FF_BLOCKS

    def body(h_ref, gn_ref, wg_ref, wu_ref, g_out, u_out):
        hn = _rms(h_ref[...], gn_ref[...])
        g_out[...] = _nn(hn, wg_ref[...])
        u_out[...] = _nn(hn, wu_ref[...])

    wspec = pl.BlockSpec((d, fb), lambda f, i: (0, f))
    ospec = pl.BlockSpec((tm, fb), lambda f, i: (i, f))
    return pl.pallas_call(body, name="ffn_a_fwd", grid=(FF_BLOCKS, t // tm),
                          in_specs=[pl.BlockSpec((tm, d), lambda f, i: (i, 0)), pl.BlockSpec((1, d), lambda f, i: (0, 0)), wspec, wspec],
                          out_specs=[ospec, ospec], out_shape=[_sds((t, D_FF)), _sds((t, D_FF))],
                          compiler_params=_params(2))(h, norm, wg, wu)


def _ffn_b_fwd(g, u, h, cw, cb, wd, tm, seq):
    t, d = h.shape
    tiles_per_seq = seq // tm

    def body(g_ref, halo_ref, u_ref, h_ref, cw_ref, cb_ref, wd_ref, h_out, pad_ref):
        first_in_seq = pl.program_id(0) % tiles_per_seq == 0
        gc = _causal_conv(pad_ref, g_ref[...], halo_ref[...], first_in_seq, cw_ref[...], FFN_CONV) + cb_ref[...]
        h_out[...] = h_ref[...] + _nn(_gelu(gc) * u_ref[...], wd_ref[...])

    return pl.pallas_call(body, name="ffn_b_fwd", grid=(t // tm,),
                          in_specs=[_row(tm, D_FF), _prev_halo(tm, D_FF), _row(tm, D_FF), _row(tm, d), _const((FFN_CONV, D_FF)),
                                    _const((1, D_FF)), _const((D_FF, d))],
                          out_specs=_row(tm, d), out_shape=_sds((t, d)),
                          scratch_shapes=[pltpu.VMEM((HALO + tm, D_FF), F32)], compiler_params=_params())(g, g, u, h, cw, cb, wd)


def _ffn_b_bwd(g, u, dout, cw, cb, wd, tm, seq):
    t, d = dout.shape
    fb = D_FF // FF_BLOCKS
    tiles_per_seq = seq // tm

    def body(g_ref, halo_ref, u_ref, dout_ref, cw_ref, cb_ref, wd_ref, dgc_out, du_out, dwd_out, dcw_out, dcb_out, pad_ref):
        i = pl.program_id(1)
        first = i == 0
        gc = _causal_conv(pad_ref, g_ref[...], halo_ref[...], i % tiles_per_seq == 0, cw_ref[...], FFN_CONV) + cb_ref[...]
        act, vjp_act = jax.vjp(lambda gc_, u_: _gelu(gc_) * u_, gc, u_ref[...])
        dout_t = dout_ref[...]
        dgc, du = vjp_act(_nt(dout_t, wd_ref[...]))
        dgc_out[...] = dgc
        du_out[...] = du
        _accumulate(dwd_out, _tn(act, dout_t), first)
        _accumulate(dcb_out, _colsum(dgc), first)
        _accumulate(dcw_out, _causal_conv_wgrad(pad_ref, dgc, FFN_CONV), first)

    blk = pl.BlockSpec((tm, fb), lambda f, i: (i, f))
    halo = pl.BlockSpec((HALO, fb), lambda f, i: (jnp.maximum(i * (tm // HALO) - 1, 0), f))
    return pl.pallas_call(
        body, name="ffn_b_bwd", grid=(FF_BLOCKS, t // tm),
        in_specs=[blk, halo, blk, pl.BlockSpec((tm, d), lambda f, i: (i, 0)), pl.BlockSpec((FFN_CONV, fb), lambda f, i: (0, f)),
                  pl.BlockSpec((1, fb), lambda f, i: (0, f)), pl.BlockSpec((fb, d), lambda f, i: (f, 0))],
        out_specs=[blk, blk, pl.BlockSpec((fb, d), lambda f, i: (f, 0)), pl.BlockSpec((FFN_CONV, fb), lambda f, i: (0, f)),
                   pl.BlockSpec((1, fb), lambda f, i: (0, f))],
        out_shape=[_sds((t, D_FF)), _sds((t, D_FF)), _sds((D_FF, d)), _sds((FFN_CONV, D_FF)), _sds((1, D_FF))],
        scratch_shapes=[pltpu.VMEM((HALO + tm, fb), F32)], compiler_params=_params(2))(g, g, u, dout, cw, cb, wd)


def _ffn_a_dgrad(h, norm, dgc, du, dres, cw, wg, wu, tm, seq):
    t, d = h.shape
    tiles_per_seq = seq // tm
    n_tiles = t // tm

    def body(h_ref, gn_ref, dgc_ref, halo_ref, du_ref, dres_ref, cw_ref, wg_ref, wu_ref, dh_out, dg_out, dgn_out, pad_ref):
        i = pl.program_id(0)
        last_in_seq = i % tiles_per_seq == tiles_per_seq - 1
        dg = _causal_conv_transpose(pad_ref, dgc_ref[...], halo_ref[...], last_in_seq, cw_ref[...], FFN_CONV)
        dg_out[...] = dg
        _, vjp_norm = jax.vjp(_rms, h_ref[...], gn_ref[...])
        dh, dgn = vjp_norm(_nt(dg, wg_ref[...]) + _nt(du_ref[...], wu_ref[...]))
        dh_out[...] = dh + dres_ref[...]
        _accumulate(dgn_out, dgn, i == 0)

    return pl.pallas_call(
        body, name="ffn_a_dgrad", grid=(n_tiles,),
        in_specs=[_row(tm, d), _const((1, d)), _row(tm, D_FF), _next_halo(tm, D_FF, n_tiles), _row(tm, D_FF), _row(tm, d),
                  _const((FFN_CONV, D_FF)), _const((d, D_FF)), _const((d, D_FF))],
        out_specs=[_row(tm, d), _row(tm, D_FF), _const((1, d))], out_shape=[_sds((t, d)), _sds((t, D_FF)), _sds((1, d))],
        scratch_shapes=[pltpu.VMEM((tm + HALO, D_FF), F32)], compiler_params=_params())(h, norm, dgc, dgc, du, dres, cw, wg, wu)


def _ffn_a_wgrad(h, norm, dg, du, tm):
    t, d = h.shape
    fb = D_FF // FF_BLOCKS

    def body(h_ref, gn_ref, dg_ref, du_ref, dwg_out, dwu_out):
        first = pl.program_id(1) == 0
        hn = _rms(h_ref[...], gn_ref[...])
        _accumulate(dwg_out, _tn(hn, dg_ref[...]), first)
        _accumulate(dwu_out, _tn(hn, du_ref[...]), first)

    blk = pl.BlockSpec((tm, fb), lambda f, i: (i, f))
    wspec = pl.BlockSpec((d, fb), lambda f, i: (0, f))
    return pl.pallas_call(body, name="ffn_a_wgrad", grid=(FF_BLOCKS, t // tm),
                          in_specs=[pl.BlockSpec((tm, d), lambda f, i: (i, 0)), pl.BlockSpec((1, d), lambda f, i: (0, 0)), blk, blk],
                          out_specs=[wspec, wspec], out_shape=[_sds((d, D_FF)), _sds((d, D_FF))],
                          compiler_params=_params(2))(h, norm, dg, du)


def _sgu_mix(vn, ws_ref, bst):
    tril = lax.broadcasted_iota(jnp.int32, (CHUNK, CHUNK), 0) >= lax.broadcasted_iota(jnp.int32, (CHUNK, CHUNK), 1)
    wms = [jnp.where(tril, ws_ref[g], 0.0) for g in range(SGU_GROUPS)]
    chunks = []
    for n in range(vn.shape[0] // CHUNK):
        vc = vn[n * CHUNK:(n + 1) * CHUNK, :]
        chunks.append(jnp.concatenate(
            [_nn(wms[g], vc[:, g * CHUNK:(g + 1) * CHUNK]) + bst[:, g:g + 1] for g in range(SGU_GROUPS)], axis=1))
    return jnp.concatenate(chunks, axis=0)


def _sgu_fwd(h, w, tm):
    t, d = h.shape

    def body(h_ref, cn_ref, win_ref, lg_ref, lb_ref, ws_ref, bst_ref, wout_ref, h_out):
        h_t = h_ref[...]
        z = _gelu(_nn(_rms(h_t, cn_ref[...]), win_ref[...]))
        vn = _layer_norm(z[:, d:], lg_ref[...], lb_ref[...])
        s = _sgu_mix(vn, ws_ref, bst_ref[...])
        h_out[...] = h_t + _nn(z[:, :d] * s, wout_ref[...])

    return pl.pallas_call(
        body, name="sgu_fwd", grid=(t // tm,),
        in_specs=[_row(tm, d), _const((1, d)), _const((d, 2 * d)), _const((1, d)), _const((1, d)), _const((SGU_GROUPS, CHUNK, CHUNK)),
                  _const((CHUNK, LANES)), _const((d, d))],
        out_specs=_row(tm, d), out_shape=_sds((t, d)), compiler_params=_params(),
    )(h, w['c_norm'], w['c_w_in'], w['c_ln_g'], w['c_ln_b'], w['c_w_s'], w['bsT'], w['c_w_out'])


def _sgu_bwd(h, dout, w, tm):
    t, d = h.shape

    def body(h_ref, dout_ref, cn_ref, win_ref, lg_ref, lb_ref, ws_ref, bst_ref, wout_ref,
             dh_out, dcn_out, dwin_out, dlg_out, dlb_out, dws_out, dbst_out, dwout_out):
        first = pl.program_id(0) == 0
        hn, vjp_norm = jax.vjp(_rms, h_ref[...], cn_ref[...])
        zpre = _nn(hn, win_ref[...])
        u, vjp_u = jax.vjp(_gelu, zpre[:, :d])
        vn, vjp_v = jax.vjp(lambda zp, lg, lb: _layer_norm(_gelu(zp), lg, lb), zpre[:, d:], lg_ref[...], lb_ref[...])
        s = _sgu_mix(vn, ws_ref, bst_ref[...])
        dout_t = dout_ref[...]
        dus = _nt(dout_t, wout_ref[...])
        _accumulate(dwout_out, _tn(u * s, dout_t), first)
        ds = dus * u
        tril = lax.broadcasted_iota(jnp.int32, (CHUNK, CHUNK), 0) >= lax.broadcasted_iota(jnp.int32, (CHUNK, CHUNK), 1)
        lane = lax.broadcasted_iota(jnp.int32, (CHUNK, LANES), 1)
        dws = [jnp.zeros((CHUNK, CHUNK), F32) for _ in range(SGU_GROUPS)]
        dbst = jnp.zeros((CHUNK, LANES), F32)
        dvn_chunks = []
        for n in range(tm // CHUNK):
            cols = []
            for g in range(SGU_GROUPS):
                ds_ng = ds[n * CHUNK:(n + 1) * CHUNK, g * CHUNK:(g + 1) * CHUNK]
                vc_ng = vn[n * CHUNK:(n + 1) * CHUNK, g * CHUNK:(g + 1) * CHUNK]
                cols.append(_tn(jnp.where(tril, ws_ref[g], 0.0), ds_ng))
                dws[g] = dws[g] + _nt(ds_ng, vc_ng)
                dbst = dbst + jnp.where(lane == g, jnp.sum(ds_ng, axis=1, keepdims=True), 0.0)
            dvn_chunks.append(jnp.concatenate(cols, axis=1))
        dvn = jnp.concatenate(dvn_chunks, axis=0)
        for g in range(SGU_GROUPS):
            val = jnp.where(tril, dws[g], 0.0)

            @pl.when(first)
            def _():
                dws_out[g] = val

            @pl.when(jnp.logical_not(first))
            def _():
                dws_out[g] += val
        _accumulate(dbst_out, dbst, first)
        (dzu,) = vjp_u(dus * s)
        dzv, dlg, dlb = vjp_v(dvn)
        _accumulate(dlg_out, dlg, first)
        _accumulate(dlb_out, dlb, first)
        dzpre = jnp.concatenate([dzu, dzv], axis=1)
        _accumulate(dwin_out, _tn(hn, dzpre), first)
        dh, dcn = vjp_norm(_nt(dzpre, win_ref[...]))
        _accumulate(dcn_out, dcn, first)
        dh_out[...] = dh + dout_t

    return pl.pallas_call(
        body, name="sgu_bwd", grid=(t // tm,),
        in_specs=[_row(tm, d), _row(tm, d), _const((1, d)), _const((d, 2 * d)), _const((1, d)), _const((1, d)),
                  _const((SGU_GROUPS, CHUNK, CHUNK)), _const((CHUNK, LANES)), _const((d, d))],
        out_specs=[_row(tm, d), _const((1, d)), _const((d, 2 * d)), _const((1, d)), _const((1, d)), _const((SGU_GROUPS, CHUNK, CHUNK)),
                   _const((CHUNK, LANES)), _const((d, d))],
        out_shape=[_sds((t, d)), _sds((1, d)), _sds((d, 2 * d)), _sds((1, d)), _sds((1, d)), _sds((SGU_GROUPS, CHUNK, CHUNK)),
                   _sds((CHUNK, LANES)), _sds((d, d))],
        compiler_params=_params(),
    )(h, dout, w['c_norm'], w['c_w_in'], w['c_ln_g'], w['c_ln_b'], w['c_w_s'], w['bsT'], w['c_w_out'])


def _final_loss(h, target, norm, tm):
    t, d = h.shape

    def body(h_ref, tgt_ref, gn_ref, dh_out, loss_out, dgn_out):
        first = pl.program_id(0) == 0
        tgt = tgt_ref[...]

        def loss_fn(h_, g_):
            err = _rms(h_, g_) - tgt
            return 0.5 * jnp.sum(jnp.mean(err * err, axis=-1, keepdims=True), axis=0, keepdims=True)

        loss, vjp_loss = jax.vjp(loss_fn, h_ref[...], gn_ref[...])
        dh, dgn = vjp_loss(jnp.ones((1, 1), F32))
        dh_out[...] = dh
        _accumulate(loss_out, loss, first)
        _accumulate(dgn_out, dgn, first)

    return pl.pallas_call(body, name="final_loss", grid=(t // tm,), in_specs=[_row(tm, d), _row(tm, d), _const((1, d))],
                          out_specs=[_row(tm, d), _const((1, 1)), _const((1, d))],
                          out_shape=[_sds((t, d)), _sds((1, 1)), _sds((1, d))], compiler_params=_params())(h, target, norm)


def _tile(t, seq, want):
    tm = min(want, seq)
    assert seq % tm == 0 and t % tm == 0 and tm % CHUNK == 0
    return tm


def _local_step(x, posb, target, w, seq, late_weights, on_grads):
    t, d = x.shape
    b = t // seq
    hp = HEADS * HEAD_PAD
    tm_big, tm_mid = _tile(t, seq, 512), _tile(t, seq, 256)
    tq = _tile(t, seq, 512)

    q, k, v, xl, gate = _ab_in_fwd(x, posb, w, tm_big)
    o = _attn_fwd(q.reshape(b, seq, hp), k.reshape(b, seq, hp), v.reshape(b, seq, hp), tq).reshape(t, hp)
    y, hs = _lru_fwd(xl, gate, w, tm_big, seq)
    w = {**w, **late_weights('out0', y)}
    h1 = _ab_out_fwd(x, o, y, w, tm_big)
    hcur = h1
    saved = []
    for l in range(2):
        if l == 1:
            w = {**w, **late_weights('mix1', hcur)}
            saved_h2 = hcur
            hcur = _sgu_fwd(hcur, w, tm_mid)
        wl = late_weights('ffn%d' % l, hcur)
        g, u = _ffn_a_fwd(hcur, w['ffn_norm'][l], wl['Wg'], wl['Wu'], tm_big)
        hnext = _ffn_b_fwd(g, u, hcur, w['ffn_conv_w'][l], w['ffn_conv_b'][l], wl['Wd'], tm_mid, seq)
        saved.append((hcur, g, u, wl))
        hcur = hnext
    dh, loss, d_final = _final_loss(hcur, target, w['final_norm'], tm_big)

    ffn = {}
    conv_b = list(w['ffn_conv_b'])
    for l in (1, 0):
        hin, g, u, wl = saved[l]
        dgc, du, d_wd, d_cw, d_cb = _ffn_b_bwd(g, u, dh, w['ffn_conv_w'][l], conv_b[l], wl['Wd'], tm_mid, seq)
        dh, dg, d_norm = _ffn_a_dgrad(hin, w['ffn_norm'][l], dgc, du, dh, w['ffn_conv_w'][l], wl['Wg'], wl['Wu'], tm_mid, seq)
        d_wg, d_wu = _ffn_a_wgrad(hin, w['ffn_norm'][l], dg, du, tm_big)
        ffn[l] = dict(ffn_norm=d_norm, ffn_conv_w=d_cw, ffn_conv_b=d_cb, Wg=d_wg, Wu=d_wu, Wd=d_wd)
        if l == 1:
            dh, d_cn, d_cwin, d_lg, d_lb, d_ws, d_bst, d_cwout = _sgu_bwd(saved_h2, dh, w, tm_mid)
            zero = on_grads('late1', dict(final_norm=d_final, c_norm=d_cn, c_ln_g=d_lg, c_ln_b=d_lb, c_w_s=d_ws, bsT=d_bst, c_w_in=d_cwin,
                                          c_w_out=d_cwout, Wg=[d_wg], Wu=[d_wu], Wd=[d_wd]))
            conv_b[0] = conv_b[0] + zero
    late0 = {name: [ffn[0][name], ffn[1][name]] for name in ('ffn_norm', 'ffn_conv_w', 'ffn_conv_b')}
    zero = on_grads('late0', dict(late0, Wg=[ffn[0]['Wg']], Wu=[ffn[0]['Wu']], Wd=[ffn[0]['Wd']]))
    w = {**w, 'Wo_b': w['Wo_b'] + zero.astype(w['Wo_b'].dtype)}
    do, dy, d_woa, d_wob = _ab_out_bwd(o, y, dh, w, tm_big)
    dxl, dgate, d_cw, d_cb, d_wa, d_ba, d_wx, d_bx, d_lam = _lru_bwd(xl, gate, hs, dy, w, tm_big, seq)
    zero = on_grads('mid', dict(Wo_a=d_woa, Wo_b=d_wob, ab_conv_w=d_cw, ab_conv_b=d_cb, Wa=d_wa, ab_b_rg_a=d_ba, Wx=d_wx,
                                ab_b_rg_x=d_bx, ab_lambda=d_lam))
    w = {**w, 'ab_norm': w['ab_norm'] + zero}
    dq, dk, dv = _attn_bwd(q.reshape(b, seq, hp), k.reshape(b, seq, hp), v.reshape(b, seq, hp), do.reshape(b, seq, hp), tq)
    dx, d_gn, d_win, d_qn, d_wq, d_kvn, d_wk, d_wv = _ab_in_bwd(
        x, posb, w, dq.reshape(t, hp), dk.reshape(t, hp), dv.reshape(t, hp), dxl, dgate, dh, tm_mid)
    return loss, dx, dict(ab_norm=d_gn, W_in=d_win, ab_q_norm=d_qn, Wq=d_wq, ab_kv_norm=d_kvn, Wk=d_wk, Wv=d_wv)


def _block_diag(wg):
    g, n, _ = wg.shape
    return jnp.einsum('gij,gh->gihj', wg, jnp.eye(g, dtype=wg.dtype)).reshape(g * n, g * n)


def _prepare_out(w_out):
    d = w_out.shape[2]
    mla = HEADS * QK_NOPE
    return {'Wo_a': jnp.pad(w_out[0, :mla].reshape(HEADS, QK_NOPE, d), ((0, 0), (0, HEAD_PAD - QK_NOPE), (0, 0))).reshape(HEADS * HEAD_PAD, d),
            'Wo_b': w_out[0, mla:]}


def _prepare(full):
    d = full['ab_w_in'].shape[1]
    w_in = full['ab_w_in'][0]
    zeros = lambda n: jnp.zeros((d, n), w_in.dtype)
    wq = full['ab_w_q_b'][0].reshape(Q_LORA, HEADS, QK_NOPE + QK_ROPE)
    wkv = full['ab_w_kv_b'][0].reshape(KV_LORA, HEADS, 2 * QK_NOPE)
    pad_head = lambda a: jnp.pad(a, ((0, 0), (0, 0), (0, HEAD_PAD - a.shape[2]))).reshape(a.shape[0], HEADS * HEAD_PAD)
    w = {
        'W_in': jnp.concatenate([w_in[:, :Z_KPE], zeros(QK_NOPE), w_in[:, Z_KPE:Z_KPE + QK_ROPE],
                                 zeros(HEAD_PAD - QK_NOPE - QK_ROPE), w_in[:, Z_KPE + QK_ROPE:]], axis=1),
        'Wq': pad_head(wq), 'Wk': pad_head(wkv[:, :, :QK_NOPE]), 'Wv': pad_head(wkv[:, :, QK_NOPE:]),
        'Wa': _bf(_block_diag(full['ab_w_rg_a'][0])), 'Wx': _bf(_block_diag(full['ab_w_rg_x'][0])),
        'c_w_s': full['c_w_s'][0],
        'bsT': jnp.pad(full['c_b_s'][0].T, ((0, 0), (0, LANES - SGU_GROUPS))),
        'ffn_norm': [full['ffn_norm'][l:l + 1] for l in range(2)], 'ffn_conv_w': [full['ffn_conv_w'][l] for l in range(2)],
        'ffn_conv_b': [full['ffn_conv_b'][l:l + 1] for l in range(2)],
        'ab_conv_w': full['ab_conv_w'][0], 'final_norm': full['final_norm'][None, :],
    }
    for name in ('ab_norm', 'ab_q_norm', 'ab_kv_norm', 'ab_conv_b', 'ab_b_rg_a', 'ab_b_rg_x', 'ab_lambda', 'c_norm', 'c_ln_g', 'c_ln_b'):
        w[name] = full[name]
    return w


def _unprepare(g):
    unpad_head = lambda a, n: a.reshape(a.shape[0], HEADS, HEAD_PAD)[:, :, :n]
    diag = lambda a: jnp.einsum('gigj->gij', a.reshape(HEADS, LRU_W // HEADS, HEADS, LRU_W // HEADS))
    rules = {
        'ab_w_in': (('W_in',), lambda a: jnp.concatenate([a[:, :Z_KPE], a[:, Z_KPE + QK_NOPE:Z_KPE + QK_NOPE + QK_ROPE], a[:, Z_LRU:]], axis=1)[None]),
        'ab_w_q_b': (('Wq',), lambda a: unpad_head(a, QK_NOPE + QK_ROPE).reshape(1, Q_LORA, -1)),
        'ab_w_kv_b': (('Wk', 'Wv'), lambda a, b: jnp.concatenate([unpad_head(a, QK_NOPE), unpad_head(b, QK_NOPE)], axis=2).reshape(1, KV_LORA, -1)),
        'ab_w_out': (('Wo_a', 'Wo_b'), lambda a, b: jnp.concatenate(
            [a.reshape(HEADS, HEAD_PAD, -1)[:, :QK_NOPE].reshape(HEADS * QK_NOPE, -1), b], axis=0)[None]),
        'ab_w_rg_a': (('Wa',), lambda a: diag(a)[None]), 'ab_w_rg_x': (('Wx',), lambda a: diag(a)[None]),
        'c_w_in': (('c_w_in',), lambda a: a[None]), 'c_w_out': (('c_w_out',), lambda a: a[None]), 'c_w_s': (('c_w_s',), lambda a: a[None]),
        'c_b_s': (('bsT',), lambda a: a[:, :SGU_GROUPS].T[None]),
        'ffn_w_gate': (('Wg',), jnp.stack), 'ffn_w_up': (('Wu',), jnp.stack), 'ffn_w_down': (('Wd',), jnp.stack),
        'ffn_norm': (('ffn_norm',), lambda a: jnp.concatenate(a, axis=0)), 'ffn_conv_w': (('ffn_conv_w',), jnp.stack),
        'ffn_conv_b': (('ffn_conv_b',), lambda a: jnp.concatenate(a, axis=0)),
        'ab_conv_w': (('ab_conv_w',), lambda a: a[None]), 'final_norm': (('final_norm',), lambda a: a[0]),
    }
    for name in ('ab_norm', 'ab_q_norm', 'ab_kv_norm', 'ab_conv_b', 'ab_b_rg_a', 'ab_b_rg_x', 'ab_lambda', 'c_norm', 'c_ln_g', 'c_ln_b'):
        rules[name] = ((name,), lambda a: a)
    return {name: fn(*[g[k] for k in keys]) for name, (keys, fn) in rules.items() if all(k in g for k in keys)}


SLAB_ROWS = 16


def _round_up(n, m):
    return -(-n // m) * m


def _to_chunks(full, axis):
    s = full.shape
    return jnp.moveaxis(full.reshape(s[:axis] + (N_DEV, s[axis] // N_DEV) + s[axis + 1:]), axis, 0)


def _from_chunks(chunks, axis):
    local = chunks.shape[1:]
    return jnp.moveaxis(chunks, 0, axis).reshape(local[:axis] + (N_DEV * local[axis],) + local[axis + 1:])


def _slab_rows(n):
    return _round_up(-(-n // LANES), SLAB_ROWS)


def _to_slab(a, lead):
    a = a.reshape(lead + (-1,))
    rows = _slab_rows(a.shape[-1])
    a = jnp.pad(a, [(0, 0)] * len(lead) + [(0, rows * LANES - a.shape[-1])])
    return a.reshape(lead + (rows, LANES))


def _pack_slabs(parts, lead):
    return jnp.concatenate([_to_slab(p, lead) for p in parts], axis=len(lead))


def _unpack_slabs(packed, shapes):
    lead = packed.shape[:-2]
    out, row = [], 0
    for shape in shapes:
        size = math.prod(shape)
        rows = _slab_rows(size)
        piece = lax.slice_in_dim(packed, row, row + rows, axis=len(lead))
        out.append(piece.reshape(lead + (rows * LANES,))[..., :size].reshape(lead + tuple(shape)))
        row += rows
    return out


HBM = pl.BlockSpec(memory_space=pl.ANY)


def _other_chips(x, y):
    return [(1 - x, y), (x, 1 - y), (1 - x, 1 - y)]


def _all_gather(blocks):
    n = len(blocks)

    def body(*refs):
        x_refs, out_refs, token = refs[:n], refs[n:2 * n], refs[2 * n]
        send_sems, recv_sems, local_sems = refs[2 * n + 1:]
        token[...] = jnp.zeros_like(token)
        x, y, c = lax.axis_index("x"), lax.axis_index("y"), lax.axis_index("c")
        me, sibling = (x, y, c), (x, y, 1 - c)
        chips = _other_chips(x, y)

        def slab(a, px, py, pc):
            return out_refs[a].at[4 * px + 2 * py + pc]

        def copy(a, k, blk, to, src=None):
            return pltpu.make_async_remote_copy(src_ref=slab(a, *blk) if src is None else src, dst_ref=slab(a, *blk),
                                                send_sem=send_sems.at[7 * a + k], recv_sem=recv_sems.at[7 * a + k],
                                                device_id=to, device_id_type=MESH)

        mine = [pltpu.make_async_copy(x_refs[a], slab(a, *me), local_sems.at[a]) for a in range(n)]
        started = []
        for a in range(n):
            mine[a].start()
            started.append(copy(a, 0, me, sibling, src=x_refs[a]))
            started += [copy(a, 1 + j, me, (*chip, c), src=x_refs[a]) for j, chip in enumerate(chips)]
        for cp in started:
            cp.start()
        for j, chip in enumerate(chips):
            for a in range(n):
                copy(a, 1 + j, (*chip, c), me).wait_recv()
                passed = copy(a, 4 + j, (*chip, c), sibling)
                passed.start()
                started.append(passed)
        for a in range(n):
            copy(a, 0, sibling, me).wait_recv()
        for j, chip in enumerate(chips):
            for a in range(n):
                copy(a, 4 + j, (*chip, 1 - c), me).wait_recv()
        for cp in started:
            cp.wait_send()
        for a in range(n):
            mine[a].wait()

    out = pl.pallas_call(
        body, name="all_gather_weights",
        out_shape=[jax.ShapeDtypeStruct((N_DEV,) + b.shape, b.dtype) for b in blocks] + [jax.ShapeDtypeStruct((8, LANES), F32)],
        in_specs=[HBM] * n, out_specs=[HBM] * n + [pl.BlockSpec(memory_space=pltpu.VMEM)],
        scratch_shapes=[pltpu.SemaphoreType.DMA((7 * n,)), pltpu.SemaphoreType.DMA((7 * n,)), pltpu.SemaphoreType.DMA((n,))],
    )(*blocks)
    return list(out[:n]), out[n][0, 0]


FLIPS = [(0, 0, 1), (1, 0, 0), (1, 0, 1), (0, 1, 0), (0, 1, 1), (1, 1, 0), (1, 1, 1)]


def _peers(x, y, c):
    flip = lambda v, f: 1 - v if f else v
    return [(flip(x, fx), flip(y, fy), flip(c, fc)) for fx, fy, fc in FLIPS]


def _direct_copies(src_refs, land_refs, send_sems, recv_sems, scatter):
    x, y, c = lax.axis_index("x"), lax.axis_index("y"), lax.axis_index("c")
    me = 4 * x + 2 * y + c
    starts, waits = [], []
    for a in range(len(src_refs)):
        for k, (px, py, pc) in enumerate(_peers(x, y, c)):
            peer = 4 * px + 2 * py + pc
            sems = dict(send_sem=send_sems.at[7 * a + k], recv_sem=recv_sems.at[7 * a + k], device_id=(px, py, pc), device_id_type=MESH)
            src = src_refs[a].at[peer] if scatter else src_refs[a]
            starts.append(pltpu.make_async_remote_copy(src_ref=src, dst_ref=land_refs[a].at[me], **sems))
            waits.append(pltpu.make_async_remote_copy(src_ref=src, dst_ref=land_refs[a].at[peer], **sems))
    return starts, waits


def _landing(src, scatter):
    block = src.shape[1:] if scatter else src.shape
    return jax.ShapeDtypeStruct((N_DEV,) + block, src.dtype)


HBM_SPACE = pl.BlockSpec(memory_space=pltpu.HBM)
SEMAPHORES = pl.BlockSpec(memory_space=pltpu.SEMAPHORE)
SPLIT_EFFECT = pltpu.SideEffectType.DATAFLOW_SIDE_EFFECTING


def _start_exchange(name, srcs, scatter):
    n = len(srcs)
    lands = [lax.empty(s.shape, s.dtype) for s in (_landing(s, scatter) for s in srcs)]

    def body(*refs):
        starts, _ = _direct_copies(refs[:n], refs[n:2 * n], refs[2 * n], refs[2 * n + 1], scatter)
        for cp in starts:
            cp.start()
        refs[-1][...] = jnp.zeros_like(refs[-1])

    held = [pltpu.with_memory_space_constraint(a, pltpu.HBM) for a in list(srcs) + lands]
    out = pl.pallas_call(
        body, name=name + "_start",
        out_shape=(pltpu.SemaphoreType.DMA((7 * n,)), pltpu.SemaphoreType.DMA((7 * n,)), *[pltpu.HBM(a.shape, a.dtype) for a in held],
                   jax.ShapeDtypeStruct((8, LANES), F32)),
        in_specs=[HBM_SPACE] * (2 * n), out_specs=(SEMAPHORES, SEMAPHORES, *[HBM_SPACE] * (2 * n), pl.BlockSpec(memory_space=pltpu.VMEM)),
        input_output_aliases={i: 2 + i for i in range(2 * n)},
        compiler_params=pltpu.CompilerParams(has_side_effects=SPLIT_EFFECT),
    )(*held)
    return out[0], out[1], list(out[2:2 + n]), list(out[2 + n:2 + 2 * n]), out[-1][0, 0], out[-1]


def _wait_exchange(name, started, after, scatter):
    send_sems, recv_sems, srcs, lands = started[:4]
    n = len(srcs)

    def body(*refs):
        _, waits = _direct_copies(refs[:n], refs[n:2 * n], refs[2 * n], refs[2 * n + 1], scatter)
        for cp in waits:
            cp.wait_send()
        for cp in waits:
            cp.wait_recv()

    out = pl.pallas_call(
        body, name=name + "_wait", out_shape=tuple(pltpu.HBM(a.shape, a.dtype) for a in srcs + lands),
        in_specs=[HBM_SPACE] * (2 * n) + [SEMAPHORES, SEMAPHORES, HBM], out_specs=tuple([HBM_SPACE] * (2 * n)),
        input_output_aliases={i: i for i in range(2 * n)},
        compiler_params=pltpu.CompilerParams(has_side_effects=SPLIT_EFFECT),
    )(*srcs, *lands, send_sems, recv_sems, after)
    return list(out[:n]), list(out[n:])


def _row_tile(rows):
    return rows // 2 if (rows // 2) % SLAB_ROWS == 0 else rows


def _sum_and_adamw(me, landed, own, wts, m, v, name, layer=None, into=None):
    layers, r, n = wts.shape
    first = 0 if layer is None else layer
    count = layers if layer is None else 1
    tr = _row_tile(r)
    blk = pl.BlockSpec((1, tr, n), lambda li, ri, me_ref: (first + li, ri, 0))
    c1 = 1.0 / (1.0 - ADAM_B1 ** ADAM_STEP)
    c2 = 1.0 / (1.0 - ADAM_B2 ** ADAM_STEP)
    held = [] if into is None else list(into)

    def body(me_ref, l_ref, own_ref, w_ref, m_ref, v_ref, *rest):
        g_out, d_out, m_out, v_out = rest[len(held):]
        mine = own_ref[0].astype(F32)
        g = jnp.where(me_ref[0] == 0, mine, l_ref[0].astype(F32))
        for dev in range(1, N_DEV):
            g = g + jnp.where(me_ref[0] == dev, mine, l_ref[dev].astype(F32))
        m_new = ADAM_B1 * m_ref[...] + (1.0 - ADAM_B1) * g
        v_new = ADAM_B2 * v_ref[...] + (1.0 - ADAM_B2) * (g * g)
        g_out[...] = g
        m_out[...] = m_new
        v_out[...] = v_new
        d_out[...] = -ADAM_LR * ((m_new * c1) / (jnp.sqrt(v_new * c2) + ADAM_EPS) + ADAM_WD * w_ref[...])

    return pl.pallas_call(
        body, name="adamw_" + name,
        grid_spec=pltpu.PrefetchScalarGridSpec(
            num_scalar_prefetch=1, grid=(count, r // tr),
            in_specs=[pl.BlockSpec((N_DEV, 1, tr, n), lambda li, ri, me_ref: (0, li, ri, 0)),
                      pl.BlockSpec((1, 1, tr, n), lambda li, ri, me_ref: (me_ref[0], li, ri, 0)), blk, blk, blk] + [HBM] * len(held),
            out_specs=[blk] * 4),
        out_shape=[_sds((layers, r, n))] * 4, input_output_aliases={6 + i: i for i in range(len(held))},
        compiler_params=_params(2))(me, landed, own, wts, m, v, *held)


EARLY = ['ab_w_in']
LATE_STAGES = {
    'out0': [('ab_w_out', None, 'ab_w_out')],
    'ffn0': [('ffn_w_gate', 0, 'Wg'), ('ffn_w_up', 0, 'Wu'), ('ffn_w_down', 0, 'Wd')],
    'mix1': [('c_w_in', None, 'c_w_in'), ('c_w_out', None, 'c_w_out')],
    'ffn1': [('ffn_w_gate', 1, 'Wg'), ('ffn_w_up', 1, 'Wu'), ('ffn_w_down', 1, 'Wd')],
}
GRAD_STAGES = {
    'late1': ([('c_w_in', None), ('c_w_out', None), ('ffn_w_gate', 1), ('ffn_w_up', 1), ('ffn_w_down', 1)],
              ['c_norm', 'c_ln_g', 'c_ln_b', 'c_w_s', 'c_b_s', 'final_norm']),
    'late0': ([('ffn_w_gate', 0), ('ffn_w_up', 0), ('ffn_w_down', 0)], ['ffn_norm', 'ffn_conv_w', 'ffn_conv_b']),
    'mid': ([('ab_w_out', None)], ['ab_conv_w', 'ab_conv_b', 'ab_w_rg_a', 'ab_b_rg_a', 'ab_w_rg_x', 'ab_b_rg_x', 'ab_lambda']),
    'last': ([('ab_w_in', None)], ['ab_norm', 'ab_q_norm', 'ab_w_q_b', 'ab_kv_norm', 'ab_w_kv_b']),
}


def _gather_early(local):
    small = [_bf(local[n]) if n in MATRICES else lax.bitcast_convert_type(local[n], BF16) for n in SMALL_SHARDED]
    gathered, zero = _all_gather([_bf(local[n]) for n in EARLY] + [_pack_slabs(small, ())])
    full = {n: local[n] for n in REPLICATED}
    for n, g in zip(EARLY, gathered):
        full[n] = _from_chunks(g, SHARD_AXIS[n])
    for n, p in zip(SMALL_SHARDED, _unpack_slabs(gathered[-1], [s.shape for s in small])):
        full[n] = _from_chunks(p if n in MATRICES else lax.bitcast_convert_type(p, F32), SHARD_AXIS[n])
    return full, zero


def kernel(x, positions, ab_norm, ab_w_in, ab_q_norm, ab_w_q_b, ab_kv_norm, ab_w_kv_b, ab_conv_w, ab_conv_b, ab_w_rg_a, ab_b_rg_a, ab_w_rg_x, ab_b_rg_x, ab_lambda, ab_w_out, c_norm, c_w_in, c_ln_g, c_ln_b, c_w_s, c_b_s, c_w_out, ffn_norm, ffn_w_gate, ffn_w_up, ffn_conv_w, ffn_conv_b, ffn_w_down, final_norm, loss_target, m_ab_norm, m_ab_w_in, m_ab_q_norm, m_ab_w_q_b, m_ab_kv_norm, m_ab_w_kv_b, m_ab_conv_w, m_ab_conv_b, m_ab_w_rg_a, m_ab_b_rg_a, m_ab_w_rg_x, m_ab_b_rg_x, m_ab_lambda, m_ab_w_out, m_c_norm, m_c_w_in, m_c_ln_g, m_c_ln_b, m_c_w_s, m_c_b_s, m_c_w_out, m_ffn_norm, m_ffn_w_gate, m_ffn_w_up, m_ffn_conv_w, m_ffn_conv_b, m_ffn_w_down, m_final_norm, v_ab_norm, v_ab_w_in, v_ab_q_norm, v_ab_w_q_b, v_ab_kv_norm, v_ab_w_kv_b, v_ab_conv_w, v_ab_conv_b, v_ab_w_rg_a, v_ab_b_rg_a, v_ab_w_rg_x, v_ab_b_rg_x, v_ab_lambda, v_ab_w_out, v_c_norm, v_c_w_in, v_c_ln_g, v_c_ln_b, v_c_w_s, v_c_b_s, v_c_w_out, v_ffn_norm, v_ffn_w_gate, v_ffn_w_up, v_ffn_conv_w, v_ffn_conv_b, v_ffn_w_down, v_final_norm):
    given = dict(locals())
    local = {n: given[n] for n in WEIGHTS}
    b, seq, d = x.shape
    t = b * seq

    me = (4 * lax.axis_index("x") + 2 * lax.axis_index("y") + lax.axis_index("c")).astype(jnp.int32)
    is_me = (jnp.arange(N_DEV, dtype=jnp.int32) == me).reshape(N_DEV, 1, 1, 1)

    full, zero = _gather_early(local)
    gathers = {}
    for stage, members in LATE_STAGES.items():
        srcs = [_bf((local[n] if layer is None else local[n][layer:layer + 1]) + zero) for n, layer, _ in members]
        gathers[stage] = _start_exchange('gather_' + stage, srcs, scatter=False)
        zero = gathers[stage][4]
    w = _prepare(full)
    w['ab_norm'] = w['ab_norm'] + zero

    def late_weights(stage, after):
        srcs, lands = _wait_exchange('gather_' + stage, gathers[stage], after, scatter=False)
        whole = [_from_chunks(jnp.where(is_me, s[None], l), SHARD_AXIS[n]) for (n, _, _), s, l in zip(LATE_STAGES[stage], srcs, lands)]
        if stage == 'out0':
            return _prepare_out(whole[0])
        return {key: a[0] for (_, _, key), a in zip(LATE_STAGES[stage], whole)}

    scatters = {}

    def start_scatter(stage, g):
        whole = _unprepare(g)
        big, small = GRAD_STAGES[stage]
        slab = [_to_chunks(whole[n], SHARD_AXIS[n]) if n in SHARD_AXIS else jnp.broadcast_to(whole[n][None], (N_DEV,) + whole[n].shape)
                for n in small]
        own = [_bf(_to_chunks(whole[n], SHARD_AXIS[n])) for n, _ in big] + [_bf(_pack_slabs(slab, (N_DEV,)))[:, None]]
        scatters[stage] = _start_exchange('scatter_' + stage, own, scatter=True)
        return scatters[stage][4]

    posb = jnp.broadcast_to(positions.astype(F32).reshape(t, 1), (t, LANES))
    loss, dx, grads = _local_step(x.reshape(t, d), posb, loss_target.reshape(t, d), w, seq, late_weights, start_scatter)
    start_scatter('last', grads)
    after = scatters['last'][5]

    me1 = me.reshape(1)
    updated = {}
    for stage, (big, small) in GRAD_STAGES.items():
        owns, landed = _wait_exchange('scatter_' + stage, scatters[stage], after, scatter=True)
        for (n, layer), own, land in zip(big, owns, landed):
            updated[n] = _sum_and_adamw(me1, land, own, given[n], given['m_' + n], given['v_' + n], n + ('' if layer is None else str(layer)),
                                        layer, updated.get(n))
        pack_small = lambda prefix: _pack_slabs([given[prefix + n] for n in small], ())[None]
        packed = _sum_and_adamw(me1, landed[-1], owns[-1], pack_small(''), pack_small('m_'), pack_small('v_'), 'small_' + stage)
        unpacked = [_unpack_slabs(p[0], [local[n].shape for n in small]) for p in packed]
        for i, n in enumerate(small):
            updated[n] = [u[i] for u in unpacked]
        after = packed[0]
    total = lax.psum(loss[0, 0], ("x", "y", "c"))
    return (total, dx.reshape(b, seq, d), *[updated[n][kind] for kind in range(4) for n in WEIGHTS])
```

```python
import math

import jax
import jax.numpy as jnp
from jax import lax
from jax.experimental import pallas as pl
from jax.experimental.pallas import tpu as pltpu

F32 = jnp.float32
BF16 = jnp.bfloat16
MESH = pl.DeviceIdType.MESH

N_DEV = 8
LANES = 128
HALO = 8
VMEM_LIMIT = 56 << 20

NORM_EPS = 1e-6
HEADS = 8
HEAD_PAD = 128
QK_NOPE = 64
QK_ROPE = 32
ROPE_HALF = 16
ROPE_BASE = 10000.0
ATTN_SCALE = (QK_NOPE + QK_ROPE) ** -0.5
LRU_C = 8.0
LRU_W = 512
CHUNK = 128
SGU_GROUPS = 8
D_FF = 2816
FF_BLOCKS = 2

ADAM_LR, ADAM_B1, ADAM_B2, ADAM_EPS, ADAM_WD, ADAM_STEP = 0.001, 0.9, 0.999, 1e-08, 0.01, 10

WEIGHTS = ['ab_norm', 'ab_w_in', 'ab_q_norm', 'ab_w_q_b', 'ab_kv_norm', 'ab_w_kv_b', 'ab_conv_w', 'ab_conv_b',
           'ab_w_rg_a', 'ab_b_rg_a', 'ab_w_rg_x', 'ab_b_rg_x', 'ab_lambda', 'ab_w_out', 'c_norm', 'c_w_in', 'c_ln_g',
           'c_ln_b', 'c_w_s', 'c_b_s', 'c_w_out', 'ffn_norm', 'ffn_w_gate', 'ffn_w_up', 'ffn_conv_w', 'ffn_conv_b',
           'ffn_w_down', 'final_norm']
SHARD_AXIS = {'ab_w_in': 2, 'ab_w_q_b': 2, 'ab_w_kv_b': 2, 'ab_conv_w': 2, 'ab_w_out': 1, 'c_norm': 1, 'c_w_in': 2,
              'c_ln_g': 1, 'c_ln_b': 1, 'c_w_out': 1, 'ffn_w_gate': 2, 'ffn_w_up': 2, 'ffn_conv_w': 2, 'ffn_w_down': 1}
MATRICES = ['ab_w_in', 'ab_w_q_b', 'ab_w_kv_b', 'ab_w_out', 'c_w_in', 'c_w_out', 'ffn_w_gate', 'ffn_w_up', 'ffn_w_down']
BIG = ['ab_w_in', 'c_w_in', 'ffn_w_gate', 'ffn_w_up', 'ab_w_out', 'c_w_out', 'ffn_w_down']
REPLICATED = [n for n in WEIGHTS if n not in SHARD_AXIS]
SMALL_SHARDED = [n for n in WEIGHTS if n in SHARD_AXIS and n not in BIG]


def _bf(x):
    return x.astype(BF16)


def _nn(a, b):
    return lax.dot_general(_bf(a), _bf(b), (((1,), (0,)), ((), ())), preferred_element_type=F32)


def _nt(a, b):
    return lax.dot_general(_bf(a), _bf(b), (((1,), (1,)), ((), ())), preferred_element_type=F32)


def _tn(a, b):
    return lax.dot_general(_bf(a), _bf(b), (((0,), (0,)), ((), ())), preferred_element_type=F32)


def _rms(x, g):
    return x * lax.rsqrt(jnp.mean(x * x, axis=-1, keepdims=True) + NORM_EPS) * g


def _layer_norm(x, g, b):
    xc = x - jnp.mean(x, axis=-1, keepdims=True)
    return xc * lax.rsqrt(jnp.mean(xc * xc, axis=-1, keepdims=True) + NORM_EPS) * g + b


def _gelu(x):
    return jax.nn.gelu(x)


STRIP = 16
STRIP_LANES = 384
GELU_C = math.sqrt(2.0 / math.pi)
GELU_A = 0.044715


def _gelu_and_grad(x):
    x2 = x * x
    t = jnp.tanh(x * (GELU_C + (GELU_C * GELU_A) * x2))
    half_x = 0.5 * x
    one_plus_t = 1.0 + t
    return half_x * one_plus_t, 0.5 * one_plus_t + half_x * (1.0 - t * t) * (GELU_C + (3.0 * GELU_C * GELU_A) * x2)


def _colsum(x):
    return jnp.sum(x, axis=0, keepdims=True)


def _softplus(x):
    return jnp.maximum(x, 0.0) + jnp.log1p(jnp.exp(-jnp.abs(x)))


@jax.custom_vjp
def _one_minus_exp(x):
    u = jnp.exp(x)
    lg = jnp.log(u)
    near = lg == 0.0
    em1 = jnp.where(near, x, (u - 1.0) * x / jnp.where(near, 1.0, lg))
    return -jnp.where(x < -20.0, u - 1.0, em1)


def _one_minus_exp_fwd(x):
    return _one_minus_exp(x), x


def _one_minus_exp_bwd(x, ct):
    return (-jnp.exp(x) * ct,)


_one_minus_exp.defvjp(_one_minus_exp_fwd, _one_minus_exp_bwd)


def _accumulate(ref, val, first):
    @pl.when(first)
    def _():
        ref[...] = val

    @pl.when(jnp.logical_not(first))
    def _():
        ref[...] += val


def _params(n_axes=1):
    return pltpu.CompilerParams(dimension_semantics=("arbitrary",) * n_axes, vmem_limit_bytes=VMEM_LIMIT)


def _row(tm, n):
    return pl.BlockSpec((tm, n), lambda i: (i, 0))


def _const(shape):
    nd = len(shape)
    return pl.BlockSpec(shape, lambda i: (0,) * nd, pipeline_mode=pl.Buffered(1))


def _prev_halo(tm, n):
    return pl.BlockSpec((HALO, n), lambda i: (jnp.maximum(i * (tm // HALO) - 1, 0), 0))


def _next_halo(tm, n, n_tiles):
    last = n_tiles * (tm // HALO) - 1
    return pl.BlockSpec((HALO, n), lambda i: (jnp.minimum((i + 1) * (tm // HALO), last), 0))


def _sds(shape, dtype=F32):
    return jax.ShapeDtypeStruct(shape, dtype)


def _rope_tables(posb):
    lane = lax.broadcasted_iota(jnp.int32, posb.shape, 1)
    in_rope = jnp.logical_and(lane >= QK_NOPE, lane < QK_NOPE + QK_ROPE)
    j = (lane & (ROPE_HALF - 1)).astype(F32)
    inv_freq = jnp.exp((-math.log(ROPE_BASE)) * j / ROPE_HALF)
    ang = posb * inv_freq
    return jnp.where(in_rope, jnp.cos(ang), 1.0), jnp.where(in_rope, jnp.sin(ang), 0.0)


def _rot(q):
    n = q.shape[1]
    lane = lax.broadcasted_iota(jnp.int32, q.shape, 1) & (HEAD_PAD - 1)
    first_half = jnp.where(lane >= QK_NOPE, -pltpu.roll(q, n - ROPE_HALF, 1), 0.0)
    second_half = jnp.where(lane < QK_NOPE + QK_ROPE, pltpu.roll(q, ROPE_HALF, 1), 0.0)
    return jnp.where(lane < QK_NOPE + ROPE_HALF, first_half, second_half)


def _rope(q, cos_t, sin_t):
    return q * cos_t + _rot(q) * sin_t


def _rope_transpose(dq, cos_t, sin_t):
    return dq * cos_t - _rot(dq * sin_t)


def _tile_heads(t):
    return jnp.concatenate([t] * HEADS, axis=1)


Q_LORA, KV_LORA = 256, 128
Z_KPE = Q_LORA + KV_LORA
Z_LRU = Z_KPE + HEAD_PAD
Z_GATE = Z_LRU + LRU_W
Z_WIDTH = Z_GATE + LRU_W


def _ab_in_fwd(x, posb, w, tm):
    t, d = x.shape

    def body(x_ref, pos_ref, gn_ref, win_ref, qn_ref, wq_ref, kvn_ref, wk_ref, wv_ref, q_out, k_out, v_out, xl_out, gate_out):
        hn = _rms(x_ref[...], gn_ref[...])
        z = _nn(hn, win_ref[...])
        cqn = _rms(z[:, :Q_LORA], qn_ref[...])
        kvn = _rms(z[:, Q_LORA:Z_KPE], kvn_ref[...])
        cos_t, sin_t = _rope_tables(pos_ref[...])
        q_out[...] = _rope(_nn(cqn, wq_ref[...]), _tile_heads(cos_t), _tile_heads(sin_t))
        kpe = _rope(z[:, Z_KPE:Z_LRU], cos_t, sin_t)
        k_out[...] = _nn(kvn, wk_ref[...]) + _tile_heads(kpe)
        v_out[...] = _nn(kvn, wv_ref[...])
        xl_out[...] = z[:, Z_LRU:Z_GATE]
        gate_out[...] = z[:, Z_GATE:]

    hp = HEADS * HEAD_PAD
    return pl.pallas_call(
        body, name="ab_in_fwd", grid=(t // tm,),
        in_specs=[_row(tm, d), _row(tm, LANES), _const((1, d)), _const((d, Z_WIDTH)), _const((1, Q_LORA)), _const((Q_LORA, hp)),
                  _const((1, KV_LORA)), _const((KV_LORA, hp)), _const((KV_LORA, hp))],
        out_specs=[_row(tm, hp), _row(tm, hp), _row(tm, hp), _row(tm, LRU_W), _row(tm, LRU_W)],
        out_shape=[_sds((t, hp)), _sds((t, hp)), _sds((t, hp)), _sds((t, LRU_W)), _sds((t, LRU_W))],
        compiler_params=_params(),
    )(x, posb, w['ab_norm'], w['W_in'], w['ab_q_norm'], w['Wq'], w['ab_kv_norm'], w['Wk'], w['Wv'])


def _ab_in_bwd(x, posb, w, dq, dk, dv, dxl, dgate, dres, tm):
    t, d = x.shape
    hp = HEADS * HEAD_PAD

    def body(x_ref, pos_ref, gn_ref, win_ref, qn_ref, wq_ref, kvn_ref, wk_ref, wv_ref, dq_ref, dk_ref, dv_ref, dxl_ref, dgate_ref,
             dres_ref, dx_out, dgn_out, dwin_out, dqn_out, dwq_out, dkvn_out, dwk_out, dwv_out):
        first = pl.program_id(0) == 0
        hn, vjp_in = jax.vjp(_rms, x_ref[...], gn_ref[...])
        z = _nn(hn, win_ref[...])
        cqn, vjp_q = jax.vjp(_rms, z[:, :Q_LORA], qn_ref[...])
        kvn, vjp_kv = jax.vjp(_rms, z[:, Q_LORA:Z_KPE], kvn_ref[...])
        cos_t, sin_t = _rope_tables(pos_ref[...])
        dq0 = _rope_transpose(dq_ref[...], _tile_heads(cos_t), _tile_heads(sin_t))
        dk0 = dk_ref[...]
        dv0 = dv_ref[...]
        dkpe = dk0[:, :HEAD_PAD]
        for h in range(1, HEADS):
            dkpe = dkpe + dk0[:, h * HEAD_PAD:(h + 1) * HEAD_PAD]
        dkpe = _rope_transpose(dkpe, cos_t, sin_t)
        _accumulate(dwq_out, _tn(cqn, dq0), first)
        _accumulate(dwk_out, _tn(kvn, dk0), first)
        _accumulate(dwv_out, _tn(kvn, dv0), first)
        dcq, dqn = vjp_q(_nt(dq0, wq_ref[...]))
        dckv, dkvn = vjp_kv(_nt(dk0, wk_ref[...]) + _nt(dv0, wv_ref[...]))
        _accumulate(dqn_out, dqn, first)
        _accumulate(dkvn_out, dkvn, first)
        dz = jnp.concatenate([dcq, dckv, dkpe, dxl_ref[...], dgate_ref[...]], axis=1)
        _accumulate(dwin_out, _tn(hn, dz), first)
        dx, dgn = vjp_in(_nt(dz, win_ref[...]))
        _accumulate(dgn_out, dgn, first)
        dx_out[...] = dx + dres_ref[...]

    return pl.pallas_call(
        body, name="ab_in_bwd", grid=(t // tm,),
        in_specs=[_row(tm, d), _row(tm, LANES), _const((1, d)), _const((d, Z_WIDTH)), _const((1, Q_LORA)), _const((Q_LORA, hp)),
                  _const((1, KV_LORA)), _const((KV_LORA, hp)), _const((KV_LORA, hp)),
                  _row(tm, hp), _row(tm, hp), _row(tm, hp), _row(tm, LRU_W), _row(tm, LRU_W), _row(tm, d)],
        out_specs=[_row(tm, d), _const((1, d)), _const((d, Z_WIDTH)), _const((1, Q_LORA)), _const((Q_LORA, hp)),
                   _const((1, KV_LORA)), _const((KV_LORA, hp)), _const((KV_LORA, hp))],
        out_shape=[_sds((t, d)), _sds((1, d)), _sds((d, Z_WIDTH)), _sds((1, Q_LORA)), _sds((Q_LORA, hp)),
                   _sds((1, KV_LORA)), _sds((KV_LORA, hp)), _sds((KV_LORA, hp))],
        compiler_params=_params(),
    )(x, posb, w['ab_norm'], w['W_in'], w['ab_q_norm'], w['Wq'], w['ab_kv_norm'], w['Wk'], w['Wv'], dq, dk, dv, dxl, dgate, dres)


def _attn_probs(q_blk, k_ext, i, tq):
    ext = k_ext.shape[0]
    s = lax.dot_general(q_blk, k_ext, (((1,), (1,)), ((), ())), preferred_element_type=F32) * ATTN_SCALE
    row = lax.broadcasted_iota(jnp.int32, (tq, ext), 0) + i * tq
    col = lax.broadcasted_iota(jnp.int32, (tq, ext), 1)
    s = jnp.where(col <= row, s, -1e30)
    p = jnp.exp(s - jnp.max(s, axis=1, keepdims=True))
    return p / jnp.sum(p, axis=1, keepdims=True)


def _attn_fwd(q, k, v, tq):
    b, s, hp = q.shape
    blk = pl.BlockSpec((1, s, HEAD_PAD), lambda bi, h: (bi, 0, h))

    def body(q_ref, k_ref, v_ref, o_ref):
        kb = _bf(k_ref[0])
        vb = _bf(v_ref[0])
        for i in range(s // tq):
            ext = (i + 1) * tq
            p = _attn_probs(_bf(q_ref[0, i * tq:ext, :]), kb[:ext], i, tq)
            o_ref[0, i * tq:ext, :] = lax.dot_general(_bf(p), vb[:ext], (((1,), (0,)), ((), ())), preferred_element_type=F32)

    return pl.pallas_call(body, name="attn_fwd", grid=(b, HEADS), in_specs=[blk, blk, blk], out_specs=blk,
                          out_shape=_sds((b, s, hp)), compiler_params=_params(2))(q, k, v)


def _attn_bwd(q, k, v, do, tq):
    b, s, hp = q.shape
    blk = pl.BlockSpec((1, s, HEAD_PAD), lambda bi, h: (bi, 0, h))

    def body(q_ref, k_ref, v_ref, do_ref, dq_ref, dk_ref, dv_ref):
        kb = _bf(k_ref[0])
        vb = _bf(v_ref[0])
        dk_ref[...] = jnp.zeros_like(dk_ref)
        dv_ref[...] = jnp.zeros_like(dv_ref)
        for i in range(s // tq):
            ext = (i + 1) * tq
            qb = _bf(q_ref[0, i * tq:ext, :])
            dob = _bf(do_ref[0, i * tq:ext, :])
            p = _attn_probs(qb, kb[:ext], i, tq)
            dv_ref[0, :ext, :] += lax.dot_general(_bf(p), dob, (((0,), (0,)), ((), ())), preferred_element_type=F32)
            dp = lax.dot_general(dob, vb[:ext], (((1,), (1,)), ((), ())), preferred_element_type=F32)
            ds = _bf(p * (dp - jnp.sum(p * dp, axis=1, keepdims=True)) * ATTN_SCALE)
            dq_ref[0, i * tq:ext, :] = lax.dot_general(ds, kb[:ext], (((1,), (0,)), ((), ())), preferred_element_type=F32)
            dk_ref[0, :ext, :] += lax.dot_general(ds, qb, (((0,), (0,)), ((), ())), preferred_element_type=F32)

    return pl.pallas_call(body, name="attn_bwd", grid=(b, HEADS), in_specs=[blk, blk, blk, blk], out_specs=[blk, blk, blk],
                          out_shape=[_sds((b, s, hp))] * 3, compiler_params=_params(2))(q, k, v, do)


LRU_CONV = 4


def _lru_point(pre_a, pre_x, xc, lam):
    r = jax.nn.sigmoid(pre_a)
    i = jax.nn.sigmoid(pre_x)
    log_a = -LRU_C * r * _softplus(-lam)
    return jnp.exp(log_a), jnp.sqrt(_one_minus_exp(2.0 * log_a)) * (i * xc)


def _causal_conv(pad_ref, x, halo, first_in_seq, w, taps):
    tm = x.shape[0]
    pad_ref[:HALO, :] = jnp.where(first_in_seq, 0.0, halo)
    pad_ref[HALO:, :] = x
    y = w[taps - 1:taps, :] * x
    for k in range(taps - 1):
        off = HALO - (taps - 1) + k
        y = y + w[k:k + 1, :] * pad_ref[off:off + tm, :]
    return y


def _conv_taps(pad_ref, r, cols, taps):
    blocks = [pad_ref[r + j * HALO:r + (j + 1) * HALO, cols] for j in range(1 + STRIP // HALO)]
    sub = lax.broadcasted_iota(jnp.int32, blocks[0].shape, 0)
    out = []
    for k in range(taps - 1):
        s = taps - 1 - k
        rolled = [pltpu.roll(b, s, 0) for b in blocks]
        out.append(jnp.concatenate([jnp.where(sub < s, rolled[j], rolled[j + 1]) for j in range(STRIP // HALO)], axis=0))
    out.append(jnp.concatenate(blocks[1:], axis=0))
    return out


def _causal_conv_wgrad(pad_ref, dy, taps):
    tm = dy.shape[0]
    return jnp.concatenate([_colsum(dy * pad_ref[HALO - (taps - 1) + k:HALO - (taps - 1) + k + tm, :]) for k in range(taps)], axis=0)


def _causal_conv_transpose(pad_ref, dy, halo_next, last_in_seq, w, taps):
    tm = dy.shape[0]
    pad_ref[:tm, :] = dy
    pad_ref[tm:, :] = jnp.where(last_in_seq, 0.0, halo_next)
    dx = w[taps - 1:taps, :] * dy
    for k in range(taps - 1):
        off = (taps - 1) - k
        dx = dx + w[k:k + 1, :] * pad_ref[off:off + tm, :]
    return dx


def _lru_fwd(xl, gate, w, ts, seq):
    t, n = xl.shape
    tiles_per_seq = seq // ts

    def body(xl_ref, halo_ref, gate_ref, cw_ref, cb_ref, wa_ref, ba_ref, wx_ref, bx_ref, lam_ref, y_out, h_out, pad_ref, a_ref, b_ref, carry_ref):
        first_in_seq = pl.program_id(0) % tiles_per_seq == 0
        xc = _causal_conv(pad_ref, xl_ref[...], halo_ref[...], first_in_seq, cw_ref[...], LRU_CONV) + cb_ref[...]
        a, bx = _lru_point(_nn(xc, wa_ref[...]) + ba_ref[...], _nn(xc, wx_ref[...]) + bx_ref[...], xc, lam_ref[...])
        a_ref[...] = a
        b_ref[...] = bx

        @pl.when(first_in_seq)
        def _():
            carry_ref[...] = jnp.zeros_like(carry_ref)

        def step(r, h):
            h = a_ref[pl.ds(r, 1), :] * h + b_ref[pl.ds(r, 1), :]
            h_out[pl.ds(r, 1), :] = h
            return h

        carry_ref[...] = lax.fori_loop(0, ts, step, carry_ref[...], unroll=8)
        y_out[...] = h_out[...] * _gelu(gate_ref[...])

    return pl.pallas_call(
        body, name="lru_fwd", grid=(t // ts,),
        in_specs=[_row(ts, n), _prev_halo(ts, n), _row(ts, n), _const((LRU_CONV, n)), _const((1, n)), _const((n, n)), _const((1, n)),
                  _const((n, n)), _const((1, n)), _const((1, n))],
        out_specs=[_row(ts, n), _row(ts, n)], out_shape=[_sds((t, n)), _sds((t, n))],
        scratch_shapes=[pltpu.VMEM((HALO + ts, n), F32), pltpu.VMEM((ts, n), F32), pltpu.VMEM((ts, n), F32), pltpu.VMEM((1, n), F32)],
        compiler_params=_params(),
    )(xl, xl, gate, w['ab_conv_w'], w['ab_conv_b'], w['Wa'], w['ab_b_rg_a'], w['Wx'], w['ab_b_rg_x'], w['ab_lambda'])


def _lru_bwd(xl, gate, hs, dy, w, ts, seq):
    t, n = xl.shape
    tiles_per_seq = seq // ts
    n_tiles = t // ts

    def rev(i):
        return n_tiles - 1 - i

    row = pl.BlockSpec((ts, n), lambda i: (rev(i), 0))
    prev = pl.BlockSpec((HALO, n), lambda i: (jnp.maximum(rev(i) * (ts // HALO) - 1, 0), 0))
    acc = lambda shape: pl.BlockSpec(shape, lambda i: (0,) * len(shape))

    def body(xl_ref, xhalo_ref, gate_ref, h_ref, hhalo_ref, dy_ref, cw_ref, cb_ref, wa_ref, ba_ref, wx_ref, bx_ref, lam_ref,
             dxl_out, dgate_out, dcw_out, dcb_out, dwa_out, dba_out, dwx_out, dbx_out, dlam_out,
             pad_ref, padh_ref, padd_ref, a_ref, g_ref, carry_ref, dhalo_ref):
        step_id = pl.program_id(0)
        first = step_id == 0
        tile = rev(step_id)
        first_in_seq = tile % tiles_per_seq == 0
        last_in_seq = tile % tiles_per_seq == tiles_per_seq - 1
        cw = cw_ref[...]
        xc = _causal_conv(pad_ref, xl_ref[...], xhalo_ref[...], first_in_seq, cw, LRU_CONV) + cb_ref[...]
        pre_a = _nn(xc, wa_ref[...]) + ba_ref[...]
        pre_x = _nn(xc, wx_ref[...]) + bx_ref[...]
        (a, _), vjp_point = jax.vjp(_lru_point, pre_a, pre_x, xc, lam_ref[...])
        h = h_ref[...]
        _, vjp_out = jax.vjp(lambda h_, g_: h_ * _gelu(g_), h, gate_ref[...])
        dh, dgate = vjp_out(dy_ref[...])
        dgate_out[...] = dgate
        a_ref[...] = a
        g_ref[...] = dh

        @pl.when(last_in_seq)
        def _():
            carry_ref[...] = jnp.zeros_like(carry_ref)

        def step(j, c):
            r = ts - 1 - j
            g = g_ref[pl.ds(r, 1), :] + c
            g_ref[pl.ds(r, 1), :] = g
            return a_ref[pl.ds(r, 1), :] * g

        carry_ref[...] = lax.fori_loop(0, ts, step, carry_ref[...], unroll=8)
        g = g_ref[...]
        padh_ref[:HALO, :] = jnp.where(first_in_seq, 0.0, hhalo_ref[...])
        padh_ref[HALO:, :] = h
        dpre_a, dpre_x, dxc, dlam = vjp_point((g * padh_ref[HALO - 1:HALO - 1 + ts, :], g))
        dxc = dxc + _nt(dpre_a, wa_ref[...]) + _nt(dpre_x, wx_ref[...])
        _accumulate(dwa_out, _tn(xc, dpre_a), first)
        _accumulate(dwx_out, _tn(xc, dpre_x), first)
        _accumulate(dba_out, _colsum(dpre_a), first)
        _accumulate(dbx_out, _colsum(dpre_x), first)
        _accumulate(dlam_out, dlam, first)
        _accumulate(dcb_out, _colsum(dxc), first)
        _accumulate(dcw_out, _causal_conv_wgrad(pad_ref, dxc, LRU_CONV), first)
        dxl_out[...] = _causal_conv_transpose(padd_ref, dxc, dhalo_ref[...], last_in_seq, cw, LRU_CONV)
        dhalo_ref[...] = dxc[:HALO, :]

    return pl.pallas_call(
        body, name="lru_bwd", grid=(n_tiles,),
        in_specs=[row, prev, row, row, prev, row, _const((LRU_CONV, n)), _const((1, n)), _const((n, n)), _const((1, n)),
                  _const((n, n)), _const((1, n)), _const((1, n))],
        out_specs=[row, row, acc((LRU_CONV, n)), acc((1, n)), acc((n, n)), acc((1, n)), acc((n, n)), acc((1, n)), acc((1, n))],
        out_shape=[_sds((t, n)), _sds((t, n)), _sds((LRU_CONV, n)), _sds((1, n)), _sds((n, n)), _sds((1, n)), _sds((n, n)),
                   _sds((1, n)), _sds((1, n))],
        scratch_shapes=[pltpu.VMEM((HALO + ts, n), F32), pltpu.VMEM((HALO + ts, n), F32), pltpu.VMEM((ts + HALO, n), F32),
                        pltpu.VMEM((ts, n), F32), pltpu.VMEM((ts, n), F32), pltpu.VMEM((1, n), F32), pltpu.VMEM((HALO, n), F32)],
        compiler_params=_params(),
    )(xl, xl, gate, hs, hs, dy, w['ab_conv_w'], w['ab_conv_b'], w['Wa'], w['ab_b_rg_a'], w['Wx'], w['ab_b_rg_x'], w['ab_lambda'])


def _ab_out_fwd(x, o, y, w, tm):
    t, d = x.shape
    hp = o.shape[1]

    def body(x_ref, o_ref, y_ref, wa_ref, wb_ref, h_out):
        h_out[...] = x_ref[...] + _nn(o_ref[...], wa_ref[...]) + _nn(y_ref[...], wb_ref[...])

    return pl.pallas_call(body, name="ab_out_fwd", grid=(t // tm,),
                          in_specs=[_row(tm, d), _row(tm, hp), _row(tm, LRU_W), _const((hp, d)), _const((LRU_W, d))],
                          out_specs=_row(tm, d), out_shape=_sds((t, d)), compiler_params=_params())(x, o, y, w['Wo_a'], w['Wo_b'])


def _ab_out_bwd(o, y, dh, w, tm):
    t, d = dh.shape
    hp = o.shape[1]

    def body(o_ref, y_ref, dh_ref, wa_ref, wb_ref, do_out, dy_out, dwa_out, dwb_out):
        first = pl.program_id(0) == 0
        dh_t = dh_ref[...]
        do_out[...] = _nt(dh_t, wa_ref[...])
        dy_out[...] = _nt(dh_t, wb_ref[...])
        _accumulate(dwa_out, _tn(o_ref[...], dh_t), first)
        _accumulate(dwb_out, _tn(y_ref[...], dh_t), first)

    return pl.pallas_call(body, name="ab_out_bwd", grid=(t // tm,),
                          in_specs=[_row(tm, hp), _row(tm, LRU_W), _row(tm, d), _const((hp, d)), _const((LRU_W, d))],
                          out_specs=[_row(tm, hp), _row(tm, LRU_W), _const((hp, d)), _const((LRU_W, d))],
                          out_shape=[_sds((t, hp)), _sds((t, LRU_W)), _sds((hp, d)), _sds((LRU_W, d))],
                          compiler_params=_params())(o, y, dh, w['Wo_a'], w['Wo_b'])


FFN_CONV = 3


def _ffn_a_fwd(h, norm, wg, wu, tm):
    t, d = h.shape
    fb = D_FF // FF_BLOCKS

    def body(h_ref, gn_ref, wg_ref, wu_ref, g_out, u_out):
        hn = _rms(h_ref[...], gn_ref[...])
        g_out[...] = _nn(hn, wg_ref[...])
        u_out[...] = _nn(hn, wu_ref[...])

    wspec = pl.BlockSpec((d, fb), lambda f, i: (0, f))
    ospec = pl.BlockSpec((tm, fb), lambda f, i: (i, f))
    return pl.pallas_call(body, name="ffn_a_fwd", grid=(FF_BLOCKS, t // tm),
                          in_specs=[pl.BlockSpec((tm, d), lambda f, i: (i, 0)), pl.BlockSpec((1, d), lambda f, i: (0, 0)), wspec, wspec],
                          out_specs=[ospec, ospec], out_shape=[_sds((t, D_FF)), _sds((t, D_FF))],
                          compiler_params=_params(2))(h, norm, wg, wu)


def _ffn_b_fwd(g, u, h, cw, cb, wd, tm, seq):
    t, d = h.shape
    tiles_per_seq = seq // tm

    def body(g_ref, halo_ref, u_ref, h_ref, cw_ref, cb_ref, wd_ref, h_out, pad_ref):
        first_in_seq = pl.program_id(0) % tiles_per_seq == 0
        gc = _causal_conv(pad_ref, g_ref[...], halo_ref[...], first_in_seq, cw_ref[...], FFN_CONV) + cb_ref[...]
        h_out[...] = h_ref[...] + _nn(_gelu(gc) * u_ref[...], wd_ref[...])

    return pl.pallas_call(body, name="ffn_b_fwd", grid=(t // tm,),
                          in_specs=[_row(tm, D_FF), _prev_halo(tm, D_FF), _row(tm, D_FF), _row(tm, d), _const((FFN_CONV, D_FF)),
                                    _const((1, D_FF)), _const((D_FF, d))],
                          out_specs=_row(tm, d), out_shape=_sds((t, d)),
                          scratch_shapes=[pltpu.VMEM((HALO + tm, D_FF), F32)], compiler_params=_params())(g, g, u, h, cw, cb, wd)


def _ffn_b_bwd(g, u, dout, cw, cb, wd, tm, seq):
    t, d = dout.shape
    fb = D_FF // FF_BLOCKS
    tiles_per_seq = seq // tm

    def body(g_ref, halo_ref, u_ref, dout_ref, cw_ref, cb_ref, wd_ref, dgc_out, du_out, dwd_out, dcw_out, dcb_out,
             pad_ref, dact_ref, act_ref, acc_ref):
        i = pl.program_id(1)
        first = i == 0
        pad_ref[:HALO, :] = jnp.where(i % tiles_per_seq == 0, 0.0, halo_ref[...])
        pad_ref[HALO:, :] = g_ref[...]
        dout_b = _bf(dout_ref[...])
        dact_ref[...] = _nt(dout_b, wd_ref[...])
        cw = cw_ref[...]
        cb = cb_ref[...]
        fold = lambda a: a[:HALO] + a[HALO:]
        for c0 in range(0, fb, STRIP_LANES):
            cols = slice(c0, min(c0 + STRIP_LANES, fb))
            sums = [jnp.zeros((HALO, cols.stop - c0), F32) for _ in range(1 + FFN_CONV)]
            for r in range(0, tm, STRIP):
                rows = slice(r, r + STRIP)
                taps = _conv_taps(pad_ref, r, cols, FFN_CONV)
                gelu, dgelu = _gelu_and_grad(cb[:, cols] + cw[0:1, cols] * taps[0] + cw[1:2, cols] * taps[1] + cw[2:3, cols] * taps[2])
                u = u_ref[rows, cols]
                dact = dact_ref[rows, cols]
                act_ref[rows, cols] = _bf(gelu * u)
                du_out[rows, cols] = _bf(dact * gelu)
                dgc = dact * u * dgelu
                dgc_out[rows, cols] = dgc
                sums = [sums[0] + fold(dgc)] + [sums[1 + k] + fold(dgc * taps[k]) for k in range(FFN_CONV)]
            for k in range(1 + FFN_CONV):
                acc_ref[k, :, cols] = sums[k]
        _accumulate(dwd_out, _tn(act_ref[...], dout_b), first)
        _accumulate(dcb_out, _colsum(acc_ref[0]), first)
        _accumulate(dcw_out, jnp.concatenate([_colsum(acc_ref[1 + k]) for k in range(FFN_CONV)], axis=0), first)

    blk = pl.BlockSpec((tm, fb), lambda f, i: (i, f))
    halo = pl.BlockSpec((HALO, fb), lambda f, i: (jnp.maximum(i * (tm // HALO) - 1, 0), f))
    wd_blk = pl.BlockSpec((fb, d), lambda f, i: (f, 0), pipeline_mode=pl.Buffered(1))
    return pl.pallas_call(
        body, name="ffn_b_bwd", grid=(FF_BLOCKS, t // tm),
        in_specs=[blk, halo, blk, pl.BlockSpec((tm, d), lambda f, i: (i, 0)), pl.BlockSpec((FFN_CONV, fb), lambda f, i: (0, f)),
                  pl.BlockSpec((1, fb), lambda f, i: (0, f)), wd_blk],
        out_specs=[blk, blk, wd_blk, pl.BlockSpec((FFN_CONV, fb), lambda f, i: (0, f)),
                   pl.BlockSpec((1, fb), lambda f, i: (0, f))],
        out_shape=[_sds((t, D_FF)), _sds((t, D_FF), BF16), _sds((D_FF, d)), _sds((FFN_CONV, D_FF)), _sds((1, D_FF))],
        scratch_shapes=[pltpu.VMEM((HALO + tm, fb), F32), pltpu.VMEM((tm, fb), F32), pltpu.VMEM((tm, fb), BF16),
                        pltpu.VMEM((1 + FFN_CONV, HALO, fb), F32)],
        compiler_params=_params(2))(g, g, u, dout, cw, cb, wd)


def _ffn_a_dgrad(h, norm, dgc, du, dres, cw, wg, wu, tm, seq):
    t, d = h.shape
    tiles_per_seq = seq // tm
    n_tiles = t // tm

    def body(h_ref, gn_ref, dgc_ref, halo_ref, du_ref, dres_ref, cw_ref, wg_ref, wu_ref, dh_out, dg_out, dgn_out, pad_ref):
        i = pl.program_id(0)
        last_in_seq = i % tiles_per_seq == tiles_per_seq - 1
        dg = _bf(_causal_conv_transpose(pad_ref, dgc_ref[...], halo_ref[...], last_in_seq, cw_ref[...], FFN_CONV))
        dg_out[...] = dg
        _, vjp_norm = jax.vjp(_rms, h_ref[...], gn_ref[...])
        dh, dgn = vjp_norm(_nt(dg, wg_ref[...]) + _nt(du_ref[...], wu_ref[...]))
        dh_out[...] = dh + dres_ref[...]
        _accumulate(dgn_out, dgn, i == 0)

    return pl.pallas_call(
        body, name="ffn_a_dgrad", grid=(n_tiles,),
        in_specs=[_row(tm, d), _const((1, d)), _row(tm, D_FF), _next_halo(tm, D_FF, n_tiles), _row(tm, D_FF), _row(tm, d),
                  _const((FFN_CONV, D_FF)), _const((d, D_FF)), _const((d, D_FF))],
        out_specs=[_row(tm, d), _row(tm, D_FF), _const((1, d))], out_shape=[_sds((t, d)), _sds((t, D_FF), BF16), _sds((1, d))],
        scratch_shapes=[pltpu.VMEM((tm + HALO, D_FF), F32)], compiler_params=_params())(h, norm, dgc, dgc, du, dres, cw, wg, wu)


def _ffn_a_wgrad(h, norm, dg, du, tm):
    t, d = h.shape
    fb = D_FF // FF_BLOCKS

    def body(h_ref, gn_ref, dg_ref, du_ref, dwg_out, dwu_out):
        first = pl.program_id(1) == 0
        hn = _rms(h_ref[...], gn_ref[...])
        _accumulate(dwg_out, _tn(hn, dg_ref[...]), first)
        _accumulate(dwu_out, _tn(hn, du_ref[...]), first)

    blk = pl.BlockSpec((tm, fb), lambda f, i: (i, f))
    wspec = pl.BlockSpec((d, fb), lambda f, i: (0, f), pipeline_mode=pl.Buffered(1))
    return pl.pallas_call(body, name="ffn_a_wgrad", grid=(FF_BLOCKS, t // tm),
                          in_specs=[pl.BlockSpec((tm, d), lambda f, i: (i, 0)), pl.BlockSpec((1, d), lambda f, i: (0, 0)), blk, blk],
                          out_specs=[wspec, wspec], out_shape=[_sds((d, D_FF)), _sds((d, D_FF))],
                          compiler_params=_params(2))(h, norm, dg, du)


def _sgu_mix(vn, ws_ref, bst):
    tril = lax.broadcasted_iota(jnp.int32, (CHUNK, CHUNK), 0) >= lax.broadcasted_iota(jnp.int32, (CHUNK, CHUNK), 1)
    wms = [jnp.where(tril, ws_ref[g], 0.0) for g in range(SGU_GROUPS)]
    chunks = []
    for n in range(vn.shape[0] // CHUNK):
        vc = vn[n * CHUNK:(n + 1) * CHUNK, :]
        chunks.append(jnp.concatenate(
            [_nn(wms[g], vc[:, g * CHUNK:(g + 1) * CHUNK]) + bst[:, g:g + 1] for g in range(SGU_GROUPS)], axis=1))
    return jnp.concatenate(chunks, axis=0)


def _sgu_fwd(h, w, tm):
    t, d = h.shape

    def body(h_ref, cn_ref, win_ref, lg_ref, lb_ref, ws_ref, bst_ref, wout_ref, h_out):
        h_t = h_ref[...]
        z = _gelu(_nn(_rms(h_t, cn_ref[...]), win_ref[...]))
        vn = _layer_norm(z[:, d:], lg_ref[...], lb_ref[...])
        s = _sgu_mix(vn, ws_ref, bst_ref[...])
        h_out[...] = h_t + _nn(z[:, :d] * s, wout_ref[...])

    return pl.pallas_call(
        body, name="sgu_fwd", grid=(t // tm,),
        in_specs=[_row(tm, d), _const((1, d)), _const((d, 2 * d)), _const((1, d)), _const((1, d)), _const((SGU_GROUPS, CHUNK, CHUNK)),
                  _const((CHUNK, LANES)), _const((d, d))],
        out_specs=_row(tm, d), out_shape=_sds((t, d)), compiler_params=_params(),
    )(h, w['c_norm'], w['c_w_in'], w['c_ln_g'], w['c_ln_b'], w['c_w_s'], w['bsT'], w['c_w_out'])


def _sgu_bwd(h, dout, w, tm):
    t, d = h.shape

    def body(h_ref, dout_ref, cn_ref, win_ref, lg_ref, lb_ref, ws_ref, bst_ref, wout_ref,
             dh_out, dcn_out, dwin_out, dlg_out, dlb_out, dws_out, dbst_out, dwout_out):
        first = pl.program_id(0) == 0
        hn, vjp_norm = jax.vjp(_rms, h_ref[...], cn_ref[...])
        zpre = _nn(hn, win_ref[...])
        u, vjp_u = jax.vjp(_gelu, zpre[:, :d])
        vn, vjp_v = jax.vjp(lambda zp, lg, lb: _layer_norm(_gelu(zp), lg, lb), zpre[:, d:], lg_ref[...], lb_ref[...])
        s = _sgu_mix(vn, ws_ref, bst_ref[...])
        dout_t = dout_ref[...]
        dus = _nt(dout_t, wout_ref[...])
        _accumulate(dwout_out, _tn(u * s, dout_t), first)
        ds = dus * u
        tril = lax.broadcasted_iota(jnp.int32, (CHUNK, CHUNK), 0) >= lax.broadcasted_iota(jnp.int32, (CHUNK, CHUNK), 1)
        lane = lax.broadcasted_iota(jnp.int32, (CHUNK, LANES), 1)
        dws = [jnp.zeros((CHUNK, CHUNK), F32) for _ in range(SGU_GROUPS)]
        dbst = jnp.zeros((CHUNK, LANES), F32)
        dvn_chunks = []
        for n in range(tm // CHUNK):
            cols = []
            for g in range(SGU_GROUPS):
                ds_ng = ds[n * CHUNK:(n + 1) * CHUNK, g * CHUNK:(g + 1) * CHUNK]
                vc_ng = vn[n * CHUNK:(n + 1) * CHUNK, g * CHUNK:(g + 1) * CHUNK]
                cols.append(_tn(jnp.where(tril, ws_ref[g], 0.0), ds_ng))
                dws[g] = dws[g] + _nt(ds_ng, vc_ng)
                dbst = dbst + jnp.where(lane == g, jnp.sum(ds_ng, axis=1, keepdims=True), 0.0)
            dvn_chunks.append(jnp.concatenate(cols, axis=1))
        dvn = jnp.concatenate(dvn_chunks, axis=0)
        for g in range(SGU_GROUPS):
            val = jnp.where(tril, dws[g], 0.0)

            @pl.when(first)
            def _():
                dws_out[g] = val

            @pl.when(jnp.logical_not(first))
            def _():
                dws_out[g] += val
        _accumulate(dbst_out, dbst, first)
        (dzu,) = vjp_u(dus * s)
        dzv, dlg, dlb = vjp_v(dvn)
        _accumulate(dlg_out, dlg, first)
        _accumulate(dlb_out, dlb, first)
        dzpre = jnp.concatenate([dzu, dzv], axis=1)
        _accumulate(dwin_out, _tn(hn, dzpre), first)
        dh, dcn = vjp_norm(_nt(dzpre, win_ref[...]))
        _accumulate(dcn_out, dcn, first)
        dh_out[...] = dh + dout_t

    return pl.pallas_call(
        body, name="sgu_bwd", grid=(t // tm,),
        in_specs=[_row(tm, d), _row(tm, d), _const((1, d)), _const((d, 2 * d)), _const((1, d)), _const((1, d)),
                  _const((SGU_GROUPS, CHUNK, CHUNK)), _const((CHUNK, LANES)), _const((d, d))],
        out_specs=[_row(tm, d), _const((1, d)), _const((d, 2 * d)), _const((1, d)), _const((1, d)), _const((SGU_GROUPS, CHUNK, CHUNK)),
                   _const((CHUNK, LANES)), _const((d, d))],
        out_shape=[_sds((t, d)), _sds((1, d)), _sds((d, 2 * d)), _sds((1, d)), _sds((1, d)), _sds((SGU_GROUPS, CHUNK, CHUNK)),
                   _sds((CHUNK, LANES)), _sds((d, d))],
        compiler_params=_params(),
    )(h, dout, w['c_norm'], w['c_w_in'], w['c_ln_g'], w['c_ln_b'], w['c_w_s'], w['bsT'], w['c_w_out'])


def _final_loss(h, target, norm, tm):
    t, d = h.shape

    def body(h_ref, tgt_ref, gn_ref, dh_out, loss_out, dgn_out):
        first = pl.program_id(0) == 0
        tgt = tgt_ref[...]

        def loss_fn(h_, g_):
            err = _rms(h_, g_) - tgt
            return 0.5 * jnp.sum(jnp.mean(err * err, axis=-1, keepdims=True), axis=0, keepdims=True)

        loss, vjp_loss = jax.vjp(loss_fn, h_ref[...], gn_ref[...])
        dh, dgn = vjp_loss(jnp.ones((1, 1), F32))
        dh_out[...] = dh
        _accumulate(loss_out, loss, first)
        _accumulate(dgn_out, dgn, first)

    return pl.pallas_call(body, name="final_loss", grid=(t // tm,), in_specs=[_row(tm, d), _row(tm, d), _const((1, d))],
                          out_specs=[_row(tm, d), _const((1, 1)), _const((1, d))],
                          out_shape=[_sds((t, d)), _sds((1, 1)), _sds((1, d))], compiler_params=_params())(h, target, norm)


def _tile(t, seq, want):
    tm = min(want, seq)
    assert seq % tm == 0 and t % tm == 0 and tm % CHUNK == 0
    return tm


def _local_step(x, posb, target, w, seq, late_weights, on_grads):
    t, d = x.shape
    b = t // seq
    hp = HEADS * HEAD_PAD
    tm_big, tm_mid = _tile(t, seq, 512), _tile(t, seq, 256)
    tq = _tile(t, seq, 512)

    q, k, v, xl, gate = _ab_in_fwd(x, posb, w, tm_big)
    o = _attn_fwd(q.reshape(b, seq, hp), k.reshape(b, seq, hp), v.reshape(b, seq, hp), tq).reshape(t, hp)
    y, hs = _lru_fwd(xl, gate, w, tm_big, seq)
    w = {**w, **late_weights('out0', y)}
    h1 = _ab_out_fwd(x, o, y, w, tm_big)
    hcur = h1
    saved = []
    for l in range(2):
        if l == 1:
            w = {**w, **late_weights('mix1', hcur)}
            saved_h2 = hcur
            hcur = _sgu_fwd(hcur, w, tm_mid)
        wl = late_weights('ffn%d' % l, hcur)
        g, u = _ffn_a_fwd(hcur, w['ffn_norm'][l], wl['Wg'], wl['Wu'], tm_big)
        hnext = _ffn_b_fwd(g, u, hcur, w['ffn_conv_w'][l], w['ffn_conv_b'][l], wl['Wd'], tm_mid, seq)
        saved.append((hcur, g, u, wl))
        hcur = hnext
    dh, loss, d_final = _final_loss(hcur, target, w['final_norm'], tm_big)

    ffn = {}
    conv_b = list(w['ffn_conv_b'])
    for l in (1, 0):
        hin, g, u, wl = saved[l]
        dgc, du, d_wd, d_cw, d_cb = _ffn_b_bwd(g, u, dh, w['ffn_conv_w'][l], conv_b[l], wl['Wd'], tm_big, seq)
        dh, dg, d_norm = _ffn_a_dgrad(hin, w['ffn_norm'][l], dgc, du, dh, w['ffn_conv_w'][l], wl['Wg'], wl['Wu'], tm_mid, seq)
        d_wg, d_wu = _ffn_a_wgrad(hin, w['ffn_norm'][l], dg, du, _tile(t, seq, 1024))
        ffn[l] = dict(ffn_norm=d_norm, ffn_conv_w=d_cw, ffn_conv_b=d_cb, Wg=d_wg, Wu=d_wu, Wd=d_wd)
        if l == 1:
            dh, d_cn, d_cwin, d_lg, d_lb, d_ws, d_bst, d_cwout = _sgu_bwd(saved_h2, dh, w, tm_mid)
            zero = on_grads('late1', dict(final_norm=d_final, c_norm=d_cn, c_ln_g=d_lg, c_ln_b=d_lb, c_w_s=d_ws, bsT=d_bst, c_w_in=d_cwin,
                                          c_w_out=d_cwout, Wg=[d_wg], Wu=[d_wu], Wd=[d_wd]))
            conv_b[0] = conv_b[0] + zero
    late0 = {name: [ffn[0][name], ffn[1][name]] for name in ('ffn_norm', 'ffn_conv_w', 'ffn_conv_b')}
    zero = on_grads('late0', dict(late0, Wg=[ffn[0]['Wg']], Wu=[ffn[0]['Wu']], Wd=[ffn[0]['Wd']]))
    w = {**w, 'Wo_b': w['Wo_b'] + zero.astype(w['Wo_b'].dtype)}
    do, dy, d_woa, d_wob = _ab_out_bwd(o, y, dh, w, tm_big)
    dxl, dgate, d_cw, d_cb, d_wa, d_ba, d_wx, d_bx, d_lam = _lru_bwd(xl, gate, hs, dy, w, tm_big, seq)
    zero = on_grads('mid', dict(Wo_a=d_woa, Wo_b=d_wob, ab_conv_w=d_cw, ab_conv_b=d_cb, Wa=d_wa, ab_b_rg_a=d_ba, Wx=d_wx,
                                ab_b_rg_x=d_bx, ab_lambda=d_lam))
    w = {**w, 'ab_norm': w['ab_norm'] + zero}
    dq, dk, dv = _attn_bwd(q.reshape(b, seq, hp), k.reshape(b, seq, hp), v.reshape(b, seq, hp), do.reshape(b, seq, hp), tq)
    dx, d_gn, d_win, d_qn, d_wq, d_kvn, d_wk, d_wv = _ab_in_bwd(
        x, posb, w, dq.reshape(t, hp), dk.reshape(t, hp), dv.reshape(t, hp), dxl, dgate, dh, tm_mid)
    return loss, dx, dict(ab_norm=d_gn, W_in=d_win, ab_q_norm=d_qn, Wq=d_wq, ab_kv_norm=d_kvn, Wk=d_wk, Wv=d_wv)


def _block_diag(wg):
    g, n, _ = wg.shape
    return jnp.einsum('gij,gh->gihj', wg, jnp.eye(g, dtype=wg.dtype)).reshape(g * n, g * n)


def _prepare_out(w_out):
    d = w_out.shape[2]
    mla = HEADS * QK_NOPE
    return {'Wo_a': jnp.pad(w_out[0, :mla].reshape(HEADS, QK_NOPE, d), ((0, 0), (0, HEAD_PAD - QK_NOPE), (0, 0))).reshape(HEADS * HEAD_PAD, d),
            'Wo_b': w_out[0, mla:]}


def _prepare(full):
    d = full['ab_w_in'].shape[1]
    w_in = full['ab_w_in'][0]
    zeros = lambda n: jnp.zeros((d, n), w_in.dtype)
    wq = full['ab_w_q_b'][0].reshape(Q_LORA, HEADS, QK_NOPE + QK_ROPE)
    wkv = full['ab_w_kv_b'][0].reshape(KV_LORA, HEADS, 2 * QK_NOPE)
    pad_head = lambda a: jnp.pad(a, ((0, 0), (0, 0), (0, HEAD_PAD - a.shape[2]))).reshape(a.shape[0], HEADS * HEAD_PAD)
    w = {
        'W_in': jnp.concatenate([w_in[:, :Z_KPE], zeros(QK_NOPE), w_in[:, Z_KPE:Z_KPE + QK_ROPE],
                                 zeros(HEAD_PAD - QK_NOPE - QK_ROPE), w_in[:, Z_KPE + QK_ROPE:]], axis=1),
        'Wq': pad_head(wq), 'Wk': pad_head(wkv[:, :, :QK_NOPE]), 'Wv': pad_head(wkv[:, :, QK_NOPE:]),
        'Wa': _bf(_block_diag(full['ab_w_rg_a'][0])), 'Wx': _bf(_block_diag(full['ab_w_rg_x'][0])),
        'c_w_s': full['c_w_s'][0],
        'bsT': jnp.pad(full['c_b_s'][0].T, ((0, 0), (0, LANES - SGU_GROUPS))),
        'ffn_norm': [full['ffn_norm'][l:l + 1] for l in range(2)], 'ffn_conv_w': [full['ffn_conv_w'][l] for l in range(2)],
        'ffn_conv_b': [full['ffn_conv_b'][l:l + 1] for l in range(2)],
        'ab_conv_w': full['ab_conv_w'][0], 'final_norm': full['final_norm'][None, :],
    }
    for name in ('ab_norm', 'ab_q_norm', 'ab_kv_norm', 'ab_conv_b', 'ab_b_rg_a', 'ab_b_rg_x', 'ab_lambda', 'c_norm', 'c_ln_g', 'c_ln_b'):
        w[name] = full[name]
    return w


def _unprepare(g):
    unpad_head = lambda a, n: a.reshape(a.shape[0], HEADS, HEAD_PAD)[:, :, :n]
    diag = lambda a: jnp.einsum('gigj->gij', a.reshape(HEADS, LRU_W // HEADS, HEADS, LRU_W // HEADS))
    rules = {
        'ab_w_in': (('W_in',), lambda a: jnp.concatenate([a[:, :Z_KPE], a[:, Z_KPE + QK_NOPE:Z_KPE + QK_NOPE + QK_ROPE], a[:, Z_LRU:]], axis=1)[None]),
        'ab_w_q_b': (('Wq',), lambda a: unpad_head(a, QK_NOPE + QK_ROPE).reshape(1, Q_LORA, -1)),
        'ab_w_kv_b': (('Wk', 'Wv'), lambda a, b: jnp.concatenate([unpad_head(a, QK_NOPE), unpad_head(b, QK_NOPE)], axis=2).reshape(1, KV_LORA, -1)),
        'ab_w_out': (('Wo_a', 'Wo_b'), lambda a, b: jnp.concatenate(
            [a.reshape(HEADS, HEAD_PAD, -1)[:, :QK_NOPE].reshape(HEADS * QK_NOPE, -1), b], axis=0)[None]),
        'ab_w_rg_a': (('Wa',), lambda a: diag(a)[None]), 'ab_w_rg_x': (('Wx',), lambda a: diag(a)[None]),
        'c_w_in': (('c_w_in',), lambda a: a[None]), 'c_w_out': (('c_w_out',), lambda a: a[None]), 'c_w_s': (('c_w_s',), lambda a: a[None]),
        'c_b_s': (('bsT',), lambda a: a[:, :SGU_GROUPS].T[None]),
        'ffn_w_gate': (('Wg',), jnp.stack), 'ffn_w_up': (('Wu',), jnp.stack), 'ffn_w_down': (('Wd',), jnp.stack),
        'ffn_norm': (('ffn_norm',), lambda a: jnp.concatenate(a, axis=0)), 'ffn_conv_w': (('ffn_conv_w',), jnp.stack),
        'ffn_conv_b': (('ffn_conv_b',), lambda a: jnp.concatenate(a, axis=0)),
        'ab_conv_w': (('ab_conv_w',), lambda a: a[None]), 'final_norm': (('final_norm',), lambda a: a[0]),
    }
    for name in ('ab_norm', 'ab_q_norm', 'ab_kv_norm', 'ab_conv_b', 'ab_b_rg_a', 'ab_b_rg_x', 'ab_lambda', 'c_norm', 'c_ln_g', 'c_ln_b'):
        rules[name] = ((name,), lambda a: a)
    return {name: fn(*[g[k] for k in keys]) for name, (keys, fn) in rules.items() if all(k in g for k in keys)}


SLAB_ROWS = 16


def _round_up(n, m):
    return -(-n // m) * m


def _to_chunks(full, axis):
    s = full.shape
    return jnp.moveaxis(full.reshape(s[:axis] + (N_DEV, s[axis] // N_DEV) + s[axis + 1:]), axis, 0)


def _from_chunks(chunks, axis):
    local = chunks.shape[1:]
    return jnp.moveaxis(chunks, 0, axis).reshape(local[:axis] + (N_DEV * local[axis],) + local[axis + 1:])


def _slab_rows(n):
    return _round_up(-(-n // LANES), SLAB_ROWS)


def _to_slab(a, lead):
    a = a.reshape(lead + (-1,))
    rows = _slab_rows(a.shape[-1])
    a = jnp.pad(a, [(0, 0)] * len(lead) + [(0, rows * LANES - a.shape[-1])])
    return a.reshape(lead + (rows, LANES))


def _pack_slabs(parts, lead):
    return jnp.concatenate([_to_slab(p, lead) for p in parts], axis=len(lead))


def _unpack_slabs(packed, shapes):
    lead = packed.shape[:-2]
    out, row = [], 0
    for shape in shapes:
        size = math.prod(shape)
        rows = _slab_rows(size)
        piece = lax.slice_in_dim(packed, row, row + rows, axis=len(lead))
        out.append(piece.reshape(lead + (rows * LANES,))[..., :size].reshape(lead + tuple(shape)))
        row += rows
    return out


HBM = pl.BlockSpec(memory_space=pl.ANY)


def _other_chips(x, y):
    return [(1 - x, y), (x, 1 - y), (1 - x, 1 - y)]


def _all_gather(blocks):
    n = len(blocks)

    def body(*refs):
        x_refs, out_refs, token = refs[:n], refs[n:2 * n], refs[2 * n]
        send_sems, recv_sems, local_sems = refs[2 * n + 1:]
        token[...] = jnp.zeros_like(token)
        x, y, c = lax.axis_index("x"), lax.axis_index("y"), lax.axis_index("c")
        me, sibling = (x, y, c), (x, y, 1 - c)
        chips = _other_chips(x, y)

        def slab(a, px, py, pc):
            return out_refs[a].at[4 * px + 2 * py + pc]

        def copy(a, k, blk, to, src=None):
            return pltpu.make_async_remote_copy(src_ref=slab(a, *blk) if src is None else src, dst_ref=slab(a, *blk),
                                                send_sem=send_sems.at[7 * a + k], recv_sem=recv_sems.at[7 * a + k],
                                                device_id=to, device_id_type=MESH)

        mine = [pltpu.make_async_copy(x_refs[a], slab(a, *me), local_sems.at[a]) for a in range(n)]
        started = []
        for a in range(n):
            mine[a].start()
            started.append(copy(a, 0, me, sibling, src=x_refs[a]))
            started += [copy(a, 1 + j, me, (*chip, c), src=x_refs[a]) for j, chip in enumerate(chips)]
        for cp in started:
            cp.start()
        for j, chip in enumerate(chips):
            for a in range(n):
                copy(a, 1 + j, (*chip, c), me).wait_recv()
                passed = copy(a, 4 + j, (*chip, c), sibling)
                passed.start()
                started.append(passed)
        for a in range(n):
            copy(a, 0, sibling, me).wait_recv()
        for j, chip in enumerate(chips):
            for a in range(n):
                copy(a, 4 + j, (*chip, 1 - c), me).wait_recv()
        for cp in started:
            cp.wait_send()
        for a in range(n):
            mine[a].wait()

    out = pl.pallas_call(
        body, name="all_gather_weights",
        out_shape=[jax.ShapeDtypeStruct((N_DEV,) + b.shape, b.dtype) for b in blocks] + [jax.ShapeDtypeStruct((8, LANES), F32)],
        in_specs=[HBM] * n, out_specs=[HBM] * n + [pl.BlockSpec(memory_space=pltpu.VMEM)],
        scratch_shapes=[pltpu.SemaphoreType.DMA((7 * n,)), pltpu.SemaphoreType.DMA((7 * n,)), pltpu.SemaphoreType.DMA((n,))],
    )(*blocks)
    return list(out[:n]), out[n][0, 0]


FLIPS = [(0, 0, 1), (1, 0, 0), (1, 0, 1), (0, 1, 0), (0, 1, 1), (1, 1, 0), (1, 1, 1)]


def _peers(x, y, c):
    flip = lambda v, f: 1 - v if f else v
    return [(flip(x, fx), flip(y, fy), flip(c, fc)) for fx, fy, fc in FLIPS]


def _direct_copies(src_refs, land_refs, send_sems, recv_sems, scatter):
    x, y, c = lax.axis_index("x"), lax.axis_index("y"), lax.axis_index("c")
    me = 4 * x + 2 * y + c
    starts, waits = [], []
    for a in range(len(src_refs)):
        for k, (px, py, pc) in enumerate(_peers(x, y, c)):
            peer = 4 * px + 2 * py + pc
            sems = dict(send_sem=send_sems.at[7 * a + k], recv_sem=recv_sems.at[7 * a + k], device_id=(px, py, pc), device_id_type=MESH)
            src = src_refs[a].at[peer] if scatter else src_refs[a]
            starts.append(pltpu.make_async_remote_copy(src_ref=src, dst_ref=land_refs[a].at[me], **sems))
            waits.append(pltpu.make_async_remote_copy(src_ref=src, dst_ref=land_refs[a].at[peer], **sems))
    return starts, waits


def _landing(src, scatter):
    block = src.shape[1:] if scatter else src.shape
    return jax.ShapeDtypeStruct((N_DEV,) + block, src.dtype)


HBM_SPACE = pl.BlockSpec(memory_space=pltpu.HBM)
SEMAPHORES = pl.BlockSpec(memory_space=pltpu.SEMAPHORE)
SPLIT_EFFECT = pltpu.SideEffectType.DATAFLOW_SIDE_EFFECTING


def _start_exchange(name, srcs, scatter):
    n = len(srcs)
    lands = [lax.empty(s.shape, s.dtype) for s in (_landing(s, scatter) for s in srcs)]

    def body(*refs):
        starts, _ = _direct_copies(refs[:n], refs[n:2 * n], refs[2 * n], refs[2 * n + 1], scatter)
        for cp in starts:
            cp.start()
        refs[-1][...] = jnp.zeros_like(refs[-1])

    held = [pltpu.with_memory_space_constraint(a, pltpu.HBM) for a in list(srcs) + lands]
    out = pl.pallas_call(
        body, name=name + "_start",
        out_shape=(pltpu.SemaphoreType.DMA((7 * n,)), pltpu.SemaphoreType.DMA((7 * n,)), *[pltpu.HBM(a.shape, a.dtype) for a in held],
                   jax.ShapeDtypeStruct((8, LANES), F32)),
        in_specs=[HBM_SPACE] * (2 * n), out_specs=(SEMAPHORES, SEMAPHORES, *[HBM_SPACE] * (2 * n), pl.BlockSpec(memory_space=pltpu.VMEM)),
        input_output_aliases={i: 2 + i for i in range(2 * n)},
        compiler_params=pltpu.CompilerParams(has_side_effects=SPLIT_EFFECT),
    )(*held)
    return out[0], out[1], list(out[2:2 + n]), list(out[2 + n:2 + 2 * n]), out[-1][0, 0], out[-1]


def _wait_exchange(name, started, after, scatter):
    send_sems, recv_sems, srcs, lands = started[:4]
    n = len(srcs)

    def body(*refs):
        _, waits = _direct_copies(refs[:n], refs[n:2 * n], refs[2 * n], refs[2 * n + 1], scatter)
        for cp in waits:
            cp.wait_send()
        for cp in waits:
            cp.wait_recv()

    out = pl.pallas_call(
        body, name=name + "_wait", out_shape=tuple(pltpu.HBM(a.shape, a.dtype) for a in srcs + lands),
        in_specs=[HBM_SPACE] * (2 * n) + [SEMAPHORES, SEMAPHORES, HBM], out_specs=tuple([HBM_SPACE] * (2 * n)),
        input_output_aliases={i: i for i in range(2 * n)},
        compiler_params=pltpu.CompilerParams(has_side_effects=SPLIT_EFFECT),
    )(*srcs, *lands, send_sems, recv_sems, after)
    return list(out[:n]), list(out[n:])


def _row_tile(rows):
    return rows // 2 if (rows // 2) % SLAB_ROWS == 0 else rows


def _sum_and_adamw(me, landed, own, wts, m, v, name, layer=None, into=None):
    layers, r, n = wts.shape
    first = 0 if layer is None else layer
    count = layers if layer is None else 1
    tr = _row_tile(r)
    blk = pl.BlockSpec((1, tr, n), lambda li, ri, me_ref: (first + li, ri, 0))
    c1 = 1.0 / (1.0 - ADAM_B1 ** ADAM_STEP)
    c2 = 1.0 / (1.0 - ADAM_B2 ** ADAM_STEP)
    held = [] if into is None else list(into)

    def body(me_ref, l_ref, own_ref, w_ref, m_ref, v_ref, *rest):
        g_out, d_out, m_out, v_out = rest[len(held):]
        mine = own_ref[0].astype(F32)
        g = jnp.where(me_ref[0] == 0, mine, l_ref[0].astype(F32))
        for dev in range(1, N_DEV):
            g = g + jnp.where(me_ref[0] == dev, mine, l_ref[dev].astype(F32))
        m_new = ADAM_B1 * m_ref[...] + (1.0 - ADAM_B1) * g
        v_new = ADAM_B2 * v_ref[...] + (1.0 - ADAM_B2) * (g * g)
        g_out[...] = g
        m_out[...] = m_new
        v_out[...] = v_new
        d_out[...] = -ADAM_LR * ((m_new * c1) / (jnp.sqrt(v_new * c2) + ADAM_EPS) + ADAM_WD * w_ref[...])

    return pl.pallas_call(
        body, name="adamw_" + name,
        grid_spec=pltpu.PrefetchScalarGridSpec(
            num_scalar_prefetch=1, grid=(count, r // tr),
            in_specs=[pl.BlockSpec((N_DEV, 1, tr, n), lambda li, ri, me_ref: (0, li, ri, 0)),
                      pl.BlockSpec((1, 1, tr, n), lambda li, ri, me_ref: (me_ref[0], li, ri, 0)), blk, blk, blk] + [HBM] * len(held),
            out_specs=[blk] * 4),
        out_shape=[_sds((layers, r, n))] * 4, input_output_aliases={6 + i: i for i in range(len(held))},
        compiler_params=_params(2))(me, landed, own, wts, m, v, *held)


EARLY = ['ab_w_in']
LATE_STAGES = {
    'out0': [('ab_w_out', None, 'ab_w_out')],
    'ffn0': [('ffn_w_gate', 0, 'Wg'), ('ffn_w_up', 0, 'Wu'), ('ffn_w_down', 0, 'Wd')],
    'mix1': [('c_w_in', None, 'c_w_in'), ('c_w_out', None, 'c_w_out')],
    'ffn1': [('ffn_w_gate', 1, 'Wg'), ('ffn_w_up', 1, 'Wu'), ('ffn_w_down', 1, 'Wd')],
}
GRAD_STAGES = {
    'late1': ([('c_w_in', None), ('c_w_out', None), ('ffn_w_gate', 1), ('ffn_w_up', 1), ('ffn_w_down', 1)],
              ['c_norm', 'c_ln_g', 'c_ln_b', 'c_w_s', 'c_b_s', 'final_norm']),
    'late0': ([('ffn_w_gate', 0), ('ffn_w_up', 0), ('ffn_w_down', 0)], ['ffn_norm', 'ffn_conv_w', 'ffn_conv_b']),
    'mid': ([('ab_w_out', None)], ['ab_conv_w', 'ab_conv_b', 'ab_w_rg_a', 'ab_b_rg_a', 'ab_w_rg_x', 'ab_b_rg_x', 'ab_lambda']),
    'last': ([('ab_w_in', None)], ['ab_norm', 'ab_q_norm', 'ab_w_q_b', 'ab_kv_norm', 'ab_w_kv_b']),
}


def _gather_early(local):
    small = [_bf(local[n]) if n in MATRICES else lax.bitcast_convert_type(local[n], BF16) for n in SMALL_SHARDED]
    gathered, zero = _all_gather([_bf(local[n]) for n in EARLY] + [_pack_slabs(small, ())])
    full = {n: local[n] for n in REPLICATED}
    for n, g in zip(EARLY, gathered):
        full[n] = _from_chunks(g, SHARD_AXIS[n])
    for n, p in zip(SMALL_SHARDED, _unpack_slabs(gathered[-1], [s.shape for s in small])):
        full[n] = _from_chunks(p if n in MATRICES else lax.bitcast_convert_type(p, F32), SHARD_AXIS[n])
    return full, zero


def kernel(x, positions, ab_norm, ab_w_in, ab_q_norm, ab_w_q_b, ab_kv_norm, ab_w_kv_b, ab_conv_w, ab_conv_b, ab_w_rg_a, ab_b_rg_a, ab_w_rg_x, ab_b_rg_x, ab_lambda, ab_w_out, c_norm, c_w_in, c_ln_g, c_ln_b, c_w_s, c_b_s, c_w_out, ffn_norm, ffn_w_gate, ffn_w_up, ffn_conv_w, ffn_conv_b, ffn_w_down, final_norm, loss_target, m_ab_norm, m_ab_w_in, m_ab_q_norm, m_ab_w_q_b, m_ab_kv_norm, m_ab_w_kv_b, m_ab_conv_w, m_ab_conv_b, m_ab_w_rg_a, m_ab_b_rg_a, m_ab_w_rg_x, m_ab_b_rg_x, m_ab_lambda, m_ab_w_out, m_c_norm, m_c_w_in, m_c_ln_g, m_c_ln_b, m_c_w_s, m_c_b_s, m_c_w_out, m_ffn_norm, m_ffn_w_gate, m_ffn_w_up, m_ffn_conv_w, m_ffn_conv_b, m_ffn_w_down, m_final_norm, v_ab_norm, v_ab_w_in, v_ab_q_norm, v_ab_w_q_b, v_ab_kv_norm, v_ab_w_kv_b, v_ab_conv_w, v_ab_conv_b, v_ab_w_rg_a, v_ab_b_rg_a, v_ab_w_rg_x, v_ab_b_rg_x, v_ab_lambda, v_ab_w_out, v_c_norm, v_c_w_in, v_c_ln_g, v_c_ln_b, v_c_w_s, v_c_b_s, v_c_w_out, v_ffn_norm, v_ffn_w_gate, v_ffn_w_up, v_ffn_conv_w, v_ffn_conv_b, v_ffn_w_down, v_final_norm):
    given = dict(locals())
    local = {n: given[n] for n in WEIGHTS}
    b, seq, d = x.shape
    t = b * seq

    me = (4 * lax.axis_index("x") + 2 * lax.axis_index("y") + lax.axis_index("c")).astype(jnp.int32)
    is_me = (jnp.arange(N_DEV, dtype=jnp.int32) == me).reshape(N_DEV, 1, 1, 1)

    full, zero = _gather_early(local)
    gathers = {}
    for stage, members in LATE_STAGES.items():
        srcs = [_bf((local[n] if layer is None else local[n][layer:layer + 1]) + zero) for n, layer, _ in members]
        gathers[stage] = _start_exchange('gather_' + stage, srcs, scatter=False)
        zero = gathers[stage][4]
    w = _prepare(full)
    w['ab_norm'] = w['ab_norm'] + zero

    def late_weights(stage, after):
        srcs, lands = _wait_exchange('gather_' + stage, gathers[stage], after, scatter=False)
        whole = [_from_chunks(jnp.where(is_me, s[None], l), SHARD_AXIS[n]) for (n, _, _), s, l in zip(LATE_STAGES[stage], srcs, lands)]
        if stage == 'out0':
            return _prepare_out(whole[0])
        return {key: a[0] for (_, _, key), a in zip(LATE_STAGES[stage], whole)}

    scatters = {}

    def start_scatter(stage, g):
        whole = _unprepare(g)
        big, small = GRAD_STAGES[stage]
        slab = [_to_chunks(whole[n], SHARD_AXIS[n]) if n in SHARD_AXIS else jnp.broadcast_to(whole[n][None], (N_DEV,) + whole[n].shape)
                for n in small]
        own = [_bf(_to_chunks(whole[n], SHARD_AXIS[n])) for n, _ in big] + [_bf(_pack_slabs(slab, (N_DEV,)))[:, None]]
        scatters[stage] = _start_exchange('scatter_' + stage, own, scatter=True)
        return scatters[stage][4]

    posb = jnp.broadcast_to(positions.astype(F32).reshape(t, 1), (t, LANES))
    loss, dx, grads = _local_step(x.reshape(t, d), posb, loss_target.reshape(t, d), w, seq, late_weights, start_scatter)
    start_scatter('last', grads)
    after = scatters['last'][5]

    me1 = me.reshape(1)
    updated = {}
    for stage, (big, small) in GRAD_STAGES.items():
        owns, landed = _wait_exchange('scatter_' + stage, scatters[stage], after, scatter=True)
        for (n, layer), own, land in zip(big, owns, landed):
            updated[n] = _sum_and_adamw(me1, land, own, given[n], given['m_' + n], given['v_' + n], n + ('' if layer is None else str(layer)),
                                        layer, updated.get(n))
        pack_small = lambda prefix: _pack_slabs([given[prefix + n] for n in small], ())[None]
        packed = _sum_and_adamw(me1, landed[-1], owns[-1], pack_small(''), pack_small('m_'), pack_small('v_'), 'small_' + stage)
        unpacked = [_unpack_slabs(p[0], [local[n].shape for n in small]) for p in packed]
        for i, n in enumerate(small):
            updated[n] = [u[i] for u in unpacked]
        after = packed[0]
    total = lax.psum(loss[0, 0], ("x", "y", "c"))
    return (total, dx.reshape(b, seq, d), *[updated[n][kind] for kind in range(4) for n in WEIGHTS])
```

```python
import math

import jax
import jax.numpy as jnp
from jax import lax
from jax.experimental import pallas as pl
from jax.experimental.pallas import tpu as pltpu

F32 = jnp.float32
BF16 = jnp.bfloat16
MESH = pl.DeviceIdType.MESH

N_DEV = 8
LANES = 128
HALO = 8
VMEM_LIMIT = 56 << 20

NORM_EPS = 1e-6
HEADS = 8
HEAD_PAD = 128
QK_NOPE = 64
QK_ROPE = 32
ROPE_HALF = 16
ROPE_BASE = 10000.0
ATTN_SCALE = (QK_NOPE + QK_ROPE) ** -0.5
LRU_C = 8.0
LRU_W = 512
CHUNK = 128
SGU_GROUPS = 8
D_FF = 2816
FF_BLOCKS = 2

ADAM_LR, ADAM_B1, ADAM_B2, ADAM_EPS, ADAM_WD, ADAM_STEP = 0.001, 0.9, 0.999, 1e-08, 0.01, 10

WEIGHTS = ['ab_norm', 'ab_w_in', 'ab_q_norm', 'ab_w_q_b', 'ab_kv_norm', 'ab_w_kv_b', 'ab_conv_w', 'ab_conv_b',
           'ab_w_rg_a', 'ab_b_rg_a', 'ab_w_rg_x', 'ab_b_rg_x', 'ab_lambda', 'ab_w_out', 'c_norm', 'c_w_in', 'c_ln_g',
           'c_ln_b', 'c_w_s', 'c_b_s', 'c_w_out', 'ffn_norm', 'ffn_w_gate', 'ffn_w_up', 'ffn_conv_w', 'ffn_conv_b',
           'ffn_w_down', 'final_norm']
SHARD_AXIS = {'ab_w_in': 2, 'ab_w_q_b': 2, 'ab_w_kv_b': 2, 'ab_conv_w': 2, 'ab_w_out': 1, 'c_norm': 1, 'c_w_in': 2,
              'c_ln_g': 1, 'c_ln_b': 1, 'c_w_out': 1, 'ffn_w_gate': 2, 'ffn_w_up': 2, 'ffn_conv_w': 2, 'ffn_w_down': 1}
MATRICES = ['ab_w_in', 'ab_w_q_b', 'ab_w_kv_b', 'ab_w_out', 'c_w_in', 'c_w_out', 'ffn_w_gate', 'ffn_w_up', 'ffn_w_down']
BIG = ['ab_w_in', 'c_w_in', 'ffn_w_gate', 'ffn_w_up', 'ab_w_out', 'c_w_out', 'ffn_w_down']
REPLICATED = [n for n in WEIGHTS if n not in SHARD_AXIS]
SMALL_SHARDED = [n for n in WEIGHTS if n in SHARD_AXIS and n not in BIG]


def _bf(x):
    return x.astype(BF16)


def _nn(a, b):
    return lax.dot_general(_bf(a), _bf(b), (((1,), (0,)), ((), ())), preferred_element_type=F32)


def _nt(a, b):
    return lax.dot_general(_bf(a), _bf(b), (((1,), (1,)), ((), ())), preferred_element_type=F32)


def _tn(a, b):
    return lax.dot_general(_bf(a), _bf(b), (((0,), (0,)), ((), ())), preferred_element_type=F32)


def _rms(x, g):
    return x * lax.rsqrt(jnp.mean(x * x, axis=-1, keepdims=True) + NORM_EPS) * g


def _layer_norm(x, g, b):
    xc = x - jnp.mean(x, axis=-1, keepdims=True)
    return xc * lax.rsqrt(jnp.mean(xc * xc, axis=-1, keepdims=True) + NORM_EPS) * g + b


def _gelu(x):
    return jax.nn.gelu(x)


STRIP = 16
STRIP_LANES = 384
GELU_C = math.sqrt(2.0 / math.pi)
GELU_A = 0.044715


def _gelu_and_grad(x):
    x2 = x * x
    t = jnp.tanh(x * (GELU_C + (GELU_C * GELU_A) * x2))
    half_x = 0.5 * x
    one_plus_t = 1.0 + t
    return half_x * one_plus_t, 0.5 * one_plus_t + half_x * (1.0 - t * t) * (GELU_C + (3.0 * GELU_C * GELU_A) * x2)


def _colsum(x):
    return jnp.sum(x, axis=0, keepdims=True)


def _softplus(x):
    return jnp.maximum(x, 0.0) + jnp.log1p(jnp.exp(-jnp.abs(x)))


@jax.custom_vjp
def _decay(x):
    a = jnp.exp(x)
    y = 2.0 * x
    series = -y * (1.0 + y * (1 / 2 + y * (1 / 6 + y * (1 / 24 + y * (1 / 120 + y * (1 / 720))))))
    return a, jnp.where(y < -0.3, 1.0 - a * a, series)


def _decay_fwd(x):
    a, gap = _decay(x)
    return (a, gap), a


def _decay_bwd(a, cts):
    return (a * (cts[0] - 2.0 * a * cts[1]),)


_decay.defvjp(_decay_fwd, _decay_bwd)


def _accumulate(ref, val, first):
    @pl.when(first)
    def _():
        ref[...] = val

    @pl.when(jnp.logical_not(first))
    def _():
        ref[...] += val


def _params(n_axes=1):
    return pltpu.CompilerParams(dimension_semantics=("arbitrary",) * n_axes, vmem_limit_bytes=VMEM_LIMIT)


def _row(tm, n):
    return pl.BlockSpec((tm, n), lambda i: (i, 0))


def _const(shape):
    nd = len(shape)
    return pl.BlockSpec(shape, lambda i: (0,) * nd, pipeline_mode=pl.Buffered(1))


def _prev_halo(tm, n):
    return pl.BlockSpec((HALO, n), lambda i: (jnp.maximum(i * (tm // HALO) - 1, 0), 0))


def _next_halo(tm, n, n_tiles):
    last = n_tiles * (tm // HALO) - 1
    return pl.BlockSpec((HALO, n), lambda i: (jnp.minimum((i + 1) * (tm // HALO), last), 0))


def _sds(shape, dtype=F32):
    return jax.ShapeDtypeStruct(shape, dtype)


def _rope_tables(posb):
    lane = lax.broadcasted_iota(jnp.int32, posb.shape, 1)
    in_rope = jnp.logical_and(lane >= QK_NOPE, lane < QK_NOPE + QK_ROPE)
    j = (lane & (ROPE_HALF - 1)).astype(F32)
    inv_freq = jnp.exp((-math.log(ROPE_BASE)) * j / ROPE_HALF)
    ang = posb * inv_freq
    return jnp.where(in_rope, jnp.cos(ang), 1.0), jnp.where(in_rope, jnp.sin(ang), 0.0)


def _rot(q):
    n = q.shape[1]
    lane = lax.broadcasted_iota(jnp.int32, q.shape, 1) & (HEAD_PAD - 1)
    first_half = jnp.where(lane >= QK_NOPE, -pltpu.roll(q, n - ROPE_HALF, 1), 0.0)
    second_half = jnp.where(lane < QK_NOPE + QK_ROPE, pltpu.roll(q, ROPE_HALF, 1), 0.0)
    return jnp.where(lane < QK_NOPE + ROPE_HALF, first_half, second_half)


def _rope(q, cos_t, sin_t):
    return q * cos_t + _rot(q) * sin_t


def _rope_transpose(dq, cos_t, sin_t):
    return dq * cos_t - _rot(dq * sin_t)


def _tile_heads(t):
    return jnp.concatenate([t] * HEADS, axis=1)


Q_LORA, KV_LORA = 256, 128
Z_KPE = Q_LORA + KV_LORA
Z_LRU = Z_KPE + HEAD_PAD
Z_GATE = Z_LRU + LRU_W
Z_WIDTH = Z_GATE + LRU_W


def _ab_in_fwd(x, posb, w, tm):
    t, d = x.shape

    def body(x_ref, pos_ref, gn_ref, win_ref, qn_ref, wq_ref, kvn_ref, wk_ref, wv_ref, q_out, k_out, v_out, xl_out, gate_out):
        hn = _rms(x_ref[...], gn_ref[...])
        z = _nn(hn, win_ref[...])
        cqn = _rms(z[:, :Q_LORA], qn_ref[...])
        kvn = _rms(z[:, Q_LORA:Z_KPE], kvn_ref[...])
        cos_t, sin_t = _rope_tables(pos_ref[...])
        q_out[...] = _rope(_nn(cqn, wq_ref[...]), _tile_heads(cos_t), _tile_heads(sin_t))
        kpe = _rope(z[:, Z_KPE:Z_LRU], cos_t, sin_t)
        k_out[...] = _nn(kvn, wk_ref[...]) + _tile_heads(kpe)
        v_out[...] = _nn(kvn, wv_ref[...])
        xl_out[...] = z[:, Z_LRU:Z_GATE]
        gate_out[...] = z[:, Z_GATE:]

    hp = HEADS * HEAD_PAD
    return pl.pallas_call(
        body, name="ab_in_fwd", grid=(t // tm,),
        in_specs=[_row(tm, d), _row(tm, LANES), _const((1, d)), _const((d, Z_WIDTH)), _const((1, Q_LORA)), _const((Q_LORA, hp)),
                  _const((1, KV_LORA)), _const((KV_LORA, hp)), _const((KV_LORA, hp))],
        out_specs=[_row(tm, hp), _row(tm, hp), _row(tm, hp), _row(tm, LRU_W), _row(tm, LRU_W)],
        out_shape=[_sds((t, hp)), _sds((t, hp)), _sds((t, hp)), _sds((t, LRU_W)), _sds((t, LRU_W))],
        compiler_params=_params(),
    )(x, posb, w['ab_norm'], w['W_in'], w['ab_q_norm'], w['Wq'], w['ab_kv_norm'], w['Wk'], w['Wv'])


def _ab_in_bwd(x, posb, w, dq, dk, dv, dxl, dgate, dres, tm):
    t, d = x.shape
    hp = HEADS * HEAD_PAD

    def body(x_ref, pos_ref, gn_ref, win_ref, qn_ref, wq_ref, kvn_ref, wk_ref, wv_ref, dq_ref, dk_ref, dv_ref, dxl_ref, dgate_ref,
             dres_ref, dx_out, dgn_out, dwin_out, dqn_out, dwq_out, dkvn_out, dwk_out, dwv_out):
        first = pl.program_id(0) == 0
        hn, vjp_in = jax.vjp(_rms, x_ref[...], gn_ref[...])
        z = _nn(hn, win_ref[...])
        cqn, vjp_q = jax.vjp(_rms, z[:, :Q_LORA], qn_ref[...])
        kvn, vjp_kv = jax.vjp(_rms, z[:, Q_LORA:Z_KPE], kvn_ref[...])
        cos_t, sin_t = _rope_tables(pos_ref[...])
        dq0 = _rope_transpose(dq_ref[...], _tile_heads(cos_t), _tile_heads(sin_t))
        dk0 = dk_ref[...]
        dv0 = dv_ref[...]
        dkpe = dk0[:, :HEAD_PAD]
        for h in range(1, HEADS):
            dkpe = dkpe + dk0[:, h * HEAD_PAD:(h + 1) * HEAD_PAD]
        dkpe = _rope_transpose(dkpe, cos_t, sin_t)
        _accumulate(dwq_out, _tn(cqn, dq0), first)
        _accumulate(dwk_out, _tn(kvn, dk0), first)
        _accumulate(dwv_out, _tn(kvn, dv0), first)
        dcq, dqn = vjp_q(_nt(dq0, wq_ref[...]))
        dckv, dkvn = vjp_kv(_nt(dk0, wk_ref[...]) + _nt(dv0, wv_ref[...]))
        _accumulate(dqn_out, dqn, first)
        _accumulate(dkvn_out, dkvn, first)
        dz = jnp.concatenate([dcq, dckv, dkpe, dxl_ref[...], dgate_ref[...]], axis=1)
        _accumulate(dwin_out, _tn(hn, dz), first)
        dx, dgn = vjp_in(_nt(dz, win_ref[...]))
        _accumulate(dgn_out, dgn, first)
        dx_out[...] = dx + dres_ref[...]

    return pl.pallas_call(
        body, name="ab_in_bwd", grid=(t // tm,),
        in_specs=[_row(tm, d), _row(tm, LANES), _const((1, d)), _const((d, Z_WIDTH)), _const((1, Q_LORA)), _const((Q_LORA, hp)),
                  _const((1, KV_LORA)), _const((KV_LORA, hp)), _const((KV_LORA, hp)),
                  _row(tm, hp), _row(tm, hp), _row(tm, hp), _row(tm, LRU_W), _row(tm, LRU_W), _row(tm, d)],
        out_specs=[_row(tm, d), _const((1, d)), _const((d, Z_WIDTH)), _const((1, Q_LORA)), _const((Q_LORA, hp)),
                   _const((1, KV_LORA)), _const((KV_LORA, hp)), _const((KV_LORA, hp))],
        out_shape=[_sds((t, d)), _sds((1, d)), _sds((d, Z_WIDTH)), _sds((1, Q_LORA)), _sds((Q_LORA, hp)),
                   _sds((1, KV_LORA)), _sds((KV_LORA, hp)), _sds((KV_LORA, hp))],
        compiler_params=_params(),
    )(x, posb, w['ab_norm'], w['W_in'], w['ab_q_norm'], w['Wq'], w['ab_kv_norm'], w['Wk'], w['Wv'], dq, dk, dv, dxl, dgate, dres)


def _attn_probs(q_blk, k_ext, i, tq):
    ext = k_ext.shape[0]
    s = lax.dot_general(q_blk, k_ext, (((1,), (1,)), ((), ())), preferred_element_type=F32) * ATTN_SCALE
    causal = lax.broadcasted_iota(jnp.int32, (tq, tq), 1) <= lax.broadcasted_iota(jnp.int32, (tq, tq), 0)
    diag = jnp.where(causal, s[:, ext - tq:], -1e30)
    s = diag if ext == tq else jnp.concatenate([s[:, :ext - tq], diag], axis=1)
    p = jnp.exp(s - jnp.max(s, axis=1, keepdims=True))
    return p / jnp.sum(p, axis=1, keepdims=True)


def _attn_fwd(q, k, v, tq):
    b, s, hp = q.shape
    blk = pl.BlockSpec((1, s, HEAD_PAD), lambda bi, h: (bi, 0, h))

    def body(q_ref, k_ref, v_ref, o_ref):
        kb = _bf(k_ref[0])
        vb = _bf(v_ref[0])
        for i in range(s // tq):
            ext = (i + 1) * tq
            p = _attn_probs(_bf(q_ref[0, i * tq:ext, :]), kb[:ext], i, tq)
            o_ref[0, i * tq:ext, :] = lax.dot_general(_bf(p), vb[:ext], (((1,), (0,)), ((), ())), preferred_element_type=F32)

    return pl.pallas_call(body, name="attn_fwd", grid=(b, HEADS), in_specs=[blk, blk, blk], out_specs=blk,
                          out_shape=_sds((b, s, hp)), compiler_params=_params(2))(q, k, v)


def _attn_bwd(q, k, v, do, tq):
    b, s, hp = q.shape
    blk = pl.BlockSpec((1, s, HEAD_PAD), lambda bi, h: (bi, 0, h))

    def body(q_ref, k_ref, v_ref, do_ref, dq_ref, dk_ref, dv_ref):
        kb = _bf(k_ref[0])
        vb = _bf(v_ref[0])
        dk_ref[...] = jnp.zeros_like(dk_ref)
        dv_ref[...] = jnp.zeros_like(dv_ref)
        for i in range(s // tq):
            ext = (i + 1) * tq
            qb = _bf(q_ref[0, i * tq:ext, :])
            dob = _bf(do_ref[0, i * tq:ext, :])
            p = _attn_probs(qb, kb[:ext], i, tq)
            dv_ref[0, :ext, :] += lax.dot_general(_bf(p), dob, (((0,), (0,)), ((), ())), preferred_element_type=F32)
            dp = lax.dot_general(dob, vb[:ext], (((1,), (1,)), ((), ())), preferred_element_type=F32)
            ds = _bf(p * (dp - jnp.sum(p * dp, axis=1, keepdims=True)) * ATTN_SCALE)
            dq_ref[0, i * tq:ext, :] = lax.dot_general(ds, kb[:ext], (((1,), (0,)), ((), ())), preferred_element_type=F32)
            dk_ref[0, :ext, :] += lax.dot_general(ds, qb, (((0,), (0,)), ((), ())), preferred_element_type=F32)

    return pl.pallas_call(body, name="attn_bwd", grid=(b, HEADS), in_specs=[blk, blk, blk, blk], out_specs=[blk, blk, blk],
                          out_shape=[_sds((b, s, hp))] * 3, compiler_params=_params(2))(q, k, v, do)


LRU_CONV = 4


def _lru_point(pre_a, pre_x, xc, lam):
    r = jax.nn.sigmoid(pre_a)
    i = jax.nn.sigmoid(pre_x)
    a, gap = _decay(-LRU_C * r * _softplus(-lam))
    return a, jnp.sqrt(gap) * (i * xc)


def _causal_conv(pad_ref, x, halo, first_in_seq, w, taps):
    tm = x.shape[0]
    pad_ref[:HALO, :] = jnp.where(first_in_seq, 0.0, halo)
    pad_ref[HALO:, :] = x
    y = w[taps - 1:taps, :] * x
    for k in range(taps - 1):
        off = HALO - (taps - 1) + k
        y = y + w[k:k + 1, :] * pad_ref[off:off + tm, :]
    return y


def _conv_taps(pad_ref, r, cols, taps):
    blocks = [pad_ref[r + j * HALO:r + (j + 1) * HALO, cols] for j in range(1 + STRIP // HALO)]
    sub = lax.broadcasted_iota(jnp.int32, blocks[0].shape, 0)
    out = []
    for k in range(taps - 1):
        s = taps - 1 - k
        rolled = [pltpu.roll(b, s, 0) for b in blocks]
        out.append(jnp.concatenate([jnp.where(sub < s, rolled[j], rolled[j + 1]) for j in range(STRIP // HALO)], axis=0))
    out.append(jnp.concatenate(blocks[1:], axis=0))
    return out


def _causal_conv_wgrad(pad_ref, dy, taps):
    tm = dy.shape[0]
    return jnp.concatenate([_colsum(dy * pad_ref[HALO - (taps - 1) + k:HALO - (taps - 1) + k + tm, :]) for k in range(taps)], axis=0)


def _causal_conv_transpose(pad_ref, dy, halo_next, last_in_seq, w, taps):
    tm = dy.shape[0]
    pad_ref[:tm, :] = dy
    pad_ref[tm:, :] = jnp.where(last_in_seq, 0.0, halo_next)
    dx = w[taps - 1:taps, :] * dy
    for k in range(taps - 1):
        off = (taps - 1) - k
        dx = dx + w[k:k + 1, :] * pad_ref[off:off + tm, :]
    return dx


def _lru_fwd(xl, gate, w, ts, seq):
    t, n = xl.shape
    tiles_per_seq = seq // ts

    def body(xl_ref, halo_ref, gate_ref, cw_ref, cb_ref, wa_ref, ba_ref, wx_ref, bx_ref, lam_ref, y_out, h_out, pad_ref, a_ref, b_ref, carry_ref):
        first_in_seq = pl.program_id(0) % tiles_per_seq == 0
        xc = _causal_conv(pad_ref, xl_ref[...], halo_ref[...], first_in_seq, cw_ref[...], LRU_CONV) + cb_ref[...]
        a, bx = _lru_point(_nn(xc, wa_ref[...]) + ba_ref[...], _nn(xc, wx_ref[...]) + bx_ref[...], xc, lam_ref[...])
        a_ref[...] = a
        b_ref[...] = bx

        @pl.when(first_in_seq)
        def _():
            carry_ref[...] = jnp.zeros_like(carry_ref)

        def step(r, h):
            h = a_ref[pl.ds(r, 1), :] * h + b_ref[pl.ds(r, 1), :]
            h_out[pl.ds(r, 1), :] = h
            return h

        carry_ref[...] = lax.fori_loop(0, ts, step, carry_ref[...], unroll=8)
        y_out[...] = h_out[...] * _gelu(gate_ref[...])

    return pl.pallas_call(
        body, name="lru_fwd", grid=(t // ts,),
        in_specs=[_row(ts, n), _prev_halo(ts, n), _row(ts, n), _const((LRU_CONV, n)), _const((1, n)), _const((n, n)), _const((1, n)),
                  _const((n, n)), _const((1, n)), _const((1, n))],
        out_specs=[_row(ts, n), _row(ts, n)], out_shape=[_sds((t, n)), _sds((t, n))],
        scratch_shapes=[pltpu.VMEM((HALO + ts, n), F32), pltpu.VMEM((ts, n), F32), pltpu.VMEM((ts, n), F32), pltpu.VMEM((1, n), F32)],
        compiler_params=_params(),
    )(xl, xl, gate, w['ab_conv_w'], w['ab_conv_b'], w['Wa'], w['ab_b_rg_a'], w['Wx'], w['ab_b_rg_x'], w['ab_lambda'])


def _lru_bwd(xl, gate, hs, dy, w, ts, seq):
    t, n = xl.shape
    tiles_per_seq = seq // ts
    n_tiles = t // ts

    def rev(i):
        return n_tiles - 1 - i

    row = pl.BlockSpec((ts, n), lambda i: (rev(i), 0))
    prev = pl.BlockSpec((HALO, n), lambda i: (jnp.maximum(rev(i) * (ts // HALO) - 1, 0), 0))
    acc = lambda shape: pl.BlockSpec(shape, lambda i: (0,) * len(shape))

    def body(xl_ref, xhalo_ref, gate_ref, h_ref, hhalo_ref, dy_ref, cw_ref, cb_ref, wa_ref, ba_ref, wx_ref, bx_ref, lam_ref,
             dxl_out, dgate_out, dcw_out, dcb_out, dwa_out, dba_out, dwx_out, dbx_out, dlam_out,
             pad_ref, padh_ref, padd_ref, a_ref, g_ref, carry_ref, dhalo_ref):
        step_id = pl.program_id(0)
        first = step_id == 0
        tile = rev(step_id)
        first_in_seq = tile % tiles_per_seq == 0
        last_in_seq = tile % tiles_per_seq == tiles_per_seq - 1
        cw = cw_ref[...]
        xc = _causal_conv(pad_ref, xl_ref[...], xhalo_ref[...], first_in_seq, cw, LRU_CONV) + cb_ref[...]
        pre_a = _nn(xc, wa_ref[...]) + ba_ref[...]
        pre_x = _nn(xc, wx_ref[...]) + bx_ref[...]
        (a, _), vjp_point = jax.vjp(_lru_point, pre_a, pre_x, xc, lam_ref[...])
        h = h_ref[...]
        _, vjp_out = jax.vjp(lambda h_, g_: h_ * _gelu(g_), h, gate_ref[...])
        dh, dgate = vjp_out(dy_ref[...])
        dgate_out[...] = dgate
        a_ref[...] = a
        g_ref[...] = dh

        @pl.when(last_in_seq)
        def _():
            carry_ref[...] = jnp.zeros_like(carry_ref)

        def step(j, c):
            r = ts - 1 - j
            g = g_ref[pl.ds(r, 1), :] + c
            g_ref[pl.ds(r, 1), :] = g
            return a_ref[pl.ds(r, 1), :] * g

        carry_ref[...] = lax.fori_loop(0, ts, step, carry_ref[...], unroll=8)
        g = g_ref[...]
        padh_ref[:HALO, :] = jnp.where(first_in_seq, 0.0, hhalo_ref[...])
        padh_ref[HALO:, :] = h
        dpre_a, dpre_x, dxc, dlam = vjp_point((g * padh_ref[HALO - 1:HALO - 1 + ts, :], g))
        dxc = dxc + _nt(dpre_a, wa_ref[...]) + _nt(dpre_x, wx_ref[...])
        _accumulate(dwa_out, _tn(xc, dpre_a), first)
        _accumulate(dwx_out, _tn(xc, dpre_x), first)
        _accumulate(dba_out, _colsum(dpre_a), first)
        _accumulate(dbx_out, _colsum(dpre_x), first)
        _accumulate(dlam_out, dlam, first)
        _accumulate(dcb_out, _colsum(dxc), first)
        _accumulate(dcw_out, _causal_conv_wgrad(pad_ref, dxc, LRU_CONV), first)
        dxl_out[...] = _causal_conv_transpose(padd_ref, dxc, dhalo_ref[...], last_in_seq, cw, LRU_CONV)
        dhalo_ref[...] = dxc[:HALO, :]

    return pl.pallas_call(
        body, name="lru_bwd", grid=(n_tiles,),
        in_specs=[row, prev, row, row, prev, row, _const((LRU_CONV, n)), _const((1, n)), _const((n, n)), _const((1, n)),
                  _const((n, n)), _const((1, n)), _const((1, n))],
        out_specs=[row, row, acc((LRU_CONV, n)), acc((1, n)), acc((n, n)), acc((1, n)), acc((n, n)), acc((1, n)), acc((1, n))],
        out_shape=[_sds((t, n)), _sds((t, n)), _sds((LRU_CONV, n)), _sds((1, n)), _sds((n, n)), _sds((1, n)), _sds((n, n)),
                   _sds((1, n)), _sds((1, n))],
        scratch_shapes=[pltpu.VMEM((HALO + ts, n), F32), pltpu.VMEM((HALO + ts, n), F32), pltpu.VMEM((ts + HALO, n), F32),
                        pltpu.VMEM((ts, n), F32), pltpu.VMEM((ts, n), F32), pltpu.VMEM((1, n), F32), pltpu.VMEM((HALO, n), F32)],
        compiler_params=_params(),
    )(xl, xl, gate, hs, hs, dy, w['ab_conv_w'], w['ab_conv_b'], w['Wa'], w['ab_b_rg_a'], w['Wx'], w['ab_b_rg_x'], w['ab_lambda'])


def _ab_out_fwd(x, o, y, w, tm):
    t, d = x.shape
    hp = o.shape[1]

    def body(x_ref, o_ref, y_ref, wa_ref, wb_ref, h_out):
        h_out[...] = x_ref[...] + _nn(o_ref[...], wa_ref[...]) + _nn(y_ref[...], wb_ref[...])

    return pl.pallas_call(body, name="ab_out_fwd", grid=(t // tm,),
                          in_specs=[_row(tm, d), _row(tm, hp), _row(tm, LRU_W), _const((hp, d)), _const((LRU_W, d))],
                          out_specs=_row(tm, d), out_shape=_sds((t, d)), compiler_params=_params())(x, o, y, w['Wo_a'], w['Wo_b'])


def _ab_out_bwd(o, y, dh, w, tm):
    t, d = dh.shape
    hp = o.shape[1]

    def body(o_ref, y_ref, dh_ref, wa_ref, wb_ref, do_out, dy_out, dwa_out, dwb_out):
        first = pl.program_id(0) == 0
        dh_t = dh_ref[...]
        do_out[...] = _nt(dh_t, wa_ref[...])
        dy_out[...] = _nt(dh_t, wb_ref[...])
        _accumulate(dwa_out, _tn(o_ref[...], dh_t), first)
        _accumulate(dwb_out, _tn(y_ref[...], dh_t), first)

    return pl.pallas_call(body, name="ab_out_bwd", grid=(t // tm,),
                          in_specs=[_row(tm, hp), _row(tm, LRU_W), _row(tm, d), _const((hp, d)), _const((LRU_W, d))],
                          out_specs=[_row(tm, hp), _row(tm, LRU_W), _const((hp, d)), _const((LRU_W, d))],
                          out_shape=[_sds((t, hp)), _sds((t, LRU_W)), _sds((hp, d)), _sds((LRU_W, d))],
                          compiler_params=_params())(o, y, dh, w['Wo_a'], w['Wo_b'])


FFN_CONV = 3


def _ffn_a_fwd(h, norm, wg, wu, tm):
    t, d = h.shape
    fb = D_FF // FF_BLOCKS

    def body(h_ref, gn_ref, wg_ref, wu_ref, g_out, u_out):
        hn = _rms(h_ref[...], gn_ref[...])
        g_out[...] = _nn(hn, wg_ref[...])
        u_out[...] = _nn(hn, wu_ref[...])

    wspec = pl.BlockSpec((d, fb), lambda f, i: (0, f))
    ospec = pl.BlockSpec((tm, fb), lambda f, i: (i, f))
    return pl.pallas_call(body, name="ffn_a_fwd", grid=(FF_BLOCKS, t // tm),
                          in_specs=[pl.BlockSpec((tm, d), lambda f, i: (i, 0)), pl.BlockSpec((1, d), lambda f, i: (0, 0)), wspec, wspec],
                          out_specs=[ospec, ospec], out_shape=[_sds((t, D_FF)), _sds((t, D_FF))],
                          compiler_params=_params(2))(h, norm, wg, wu)


def _ffn_b_fwd(g, u, h, cw, cb, wd, tm, seq):
    t, d = h.shape
    tiles_per_seq = seq // tm

    def body(g_ref, halo_ref, u_ref, h_ref, cw_ref, cb_ref, wd_ref, h_out, pad_ref):
        first_in_seq = pl.program_id(0) % tiles_per_seq == 0
        gc = _causal_conv(pad_ref, g_ref[...], halo_ref[...], first_in_seq, cw_ref[...], FFN_CONV) + cb_ref[...]
        h_out[...] = h_ref[...] + _nn(_gelu(gc) * u_ref[...], wd_ref[...])

    return pl.pallas_call(body, name="ffn_b_fwd", grid=(t // tm,),
                          in_specs=[_row(tm, D_FF), _prev_halo(tm, D_FF), _row(tm, D_FF), _row(tm, d), _const((FFN_CONV, D_FF)),
                                    _const((1, D_FF)), _const((D_FF, d))],
                          out_specs=_row(tm, d), out_shape=_sds((t, d)),
                          scratch_shapes=[pltpu.VMEM((HALO + tm, D_FF), F32)], compiler_params=_params())(g, g, u, h, cw, cb, wd)


def _ffn_b_bwd(g, u, dout, cw, cb, wd, tm, seq):
    t, d = dout.shape
    fb = D_FF // FF_BLOCKS
    tiles_per_seq = seq // tm

    def body(g_ref, halo_ref, u_ref, dout_ref, cw_ref, cb_ref, wd_ref, dgc_out, du_out, dwd_out, dcw_out, dcb_out,
             pad_ref, dact_ref, act_ref, acc_ref):
        i = pl.program_id(1)
        first = i == 0
        pad_ref[:HALO, :] = jnp.where(i % tiles_per_seq == 0, 0.0, halo_ref[...])
        pad_ref[HALO:, :] = g_ref[...]
        dout_b = _bf(dout_ref[...])
        dact_ref[...] = _nt(dout_b, wd_ref[...])
        cw = cw_ref[...]
        cb = cb_ref[...]
        fold = lambda a: a[:HALO] + a[HALO:]
        for c0 in range(0, fb, STRIP_LANES):
            cols = slice(c0, min(c0 + STRIP_LANES, fb))
            sums = [jnp.zeros((HALO, cols.stop - c0), F32) for _ in range(1 + FFN_CONV)]
            for r in range(0, tm, STRIP):
                rows = slice(r, r + STRIP)
                taps = _conv_taps(pad_ref, r, cols, FFN_CONV)
                gelu, dgelu = _gelu_and_grad(cb[:, cols] + cw[0:1, cols] * taps[0] + cw[1:2, cols] * taps[1] + cw[2:3, cols] * taps[2])
                u = u_ref[rows, cols]
                dact = dact_ref[rows, cols]
                act_ref[rows, cols] = _bf(gelu * u)
                du_out[rows, cols] = _bf(dact * gelu)
                dgc = dact * u * dgelu
                dgc_out[rows, cols] = dgc
                sums = [sums[0] + fold(dgc)] + [sums[1 + k] + fold(dgc * taps[k]) for k in range(FFN_CONV)]
            for k in range(1 + FFN_CONV):
                acc_ref[k, :, cols] = sums[k]
        _accumulate(dwd_out, _tn(act_ref[...], dout_b), first)
        _accumulate(dcb_out, _colsum(acc_ref[0]), first)
        _accumulate(dcw_out, jnp.concatenate([_colsum(acc_ref[1 + k]) for k in range(FFN_CONV)], axis=0), first)

    blk = pl.BlockSpec((tm, fb), lambda f, i: (i, f))
    halo = pl.BlockSpec((HALO, fb), lambda f, i: (jnp.maximum(i * (tm // HALO) - 1, 0), f))
    wd_blk = pl.BlockSpec((fb, d), lambda f, i: (f, 0), pipeline_mode=pl.Buffered(1))
    return pl.pallas_call(
        body, name="ffn_b_bwd", grid=(FF_BLOCKS, t // tm),
        in_specs=[blk, halo, blk, pl.BlockSpec((tm, d), lambda f, i: (i, 0)), pl.BlockSpec((FFN_CONV, fb), lambda f, i: (0, f)),
                  pl.BlockSpec((1, fb), lambda f, i: (0, f)), wd_blk],
        out_specs=[blk, blk, wd_blk, pl.BlockSpec((FFN_CONV, fb), lambda f, i: (0, f)),
                   pl.BlockSpec((1, fb), lambda f, i: (0, f))],
        out_shape=[_sds((t, D_FF)), _sds((t, D_FF), BF16), _sds((D_FF, d)), _sds((FFN_CONV, D_FF)), _sds((1, D_FF))],
        scratch_shapes=[pltpu.VMEM((HALO + tm, fb), F32), pltpu.VMEM((tm, fb), F32), pltpu.VMEM((tm, fb), BF16),
                        pltpu.VMEM((1 + FFN_CONV, HALO, fb), F32)],
        compiler_params=_params(2))(g, g, u, dout, cw, cb, wd)


def _ffn_a_dgrad(h, norm, dgc, du, dres, cw, wg, wu, tm, seq):
    t, d = h.shape
    tiles_per_seq = seq // tm
    n_tiles = t // tm

    def body(h_ref, gn_ref, dgc_ref, halo_ref, du_ref, dres_ref, cw_ref, wg_ref, wu_ref, dh_out, dg_out, dgn_out, pad_ref):
        i = pl.program_id(0)
        last_in_seq = i % tiles_per_seq == tiles_per_seq - 1
        dg = _bf(_causal_conv_transpose(pad_ref, dgc_ref[...], halo_ref[...], last_in_seq, cw_ref[...], FFN_CONV))
        dg_out[...] = dg
        _, vjp_norm = jax.vjp(_rms, h_ref[...], gn_ref[...])
        dh, dgn = vjp_norm(_nt(dg, wg_ref[...]) + _nt(du_ref[...], wu_ref[...]))
        dh_out[...] = dh + dres_ref[...]
        _accumulate(dgn_out, dgn, i == 0)

    return pl.pallas_call(
        body, name="ffn_a_dgrad", grid=(n_tiles,),
        in_specs=[_row(tm, d), _const((1, d)), _row(tm, D_FF), _next_halo(tm, D_FF, n_tiles), _row(tm, D_FF), _row(tm, d),
                  _const((FFN_CONV, D_FF)), _const((d, D_FF)), _const((d, D_FF))],
        out_specs=[_row(tm, d), _row(tm, D_FF), _const((1, d))], out_shape=[_sds((t, d)), _sds((t, D_FF), BF16), _sds((1, d))],
        scratch_shapes=[pltpu.VMEM((tm + HALO, D_FF), F32)], compiler_params=_params())(h, norm, dgc, dgc, du, dres, cw, wg, wu)


def _ffn_a_wgrad(h, norm, dg, du, tm):
    t, d = h.shape
    fb = D_FF // FF_BLOCKS

    def body(h_ref, gn_ref, dg_ref, du_ref, dwg_out, dwu_out):
        first = pl.program_id(1) == 0
        hn = _rms(h_ref[...], gn_ref[...])
        _accumulate(dwg_out, _tn(hn, dg_ref[...]), first)
        _accumulate(dwu_out, _tn(hn, du_ref[...]), first)

    blk = pl.BlockSpec((tm, fb), lambda f, i: (i, f))
    wspec = pl.BlockSpec((d, fb), lambda f, i: (0, f), pipeline_mode=pl.Buffered(1))
    return pl.pallas_call(body, name="ffn_a_wgrad", grid=(FF_BLOCKS, t // tm),
                          in_specs=[pl.BlockSpec((tm, d), lambda f, i: (i, 0)), pl.BlockSpec((1, d), lambda f, i: (0, 0)), blk, blk],
                          out_specs=[wspec, wspec], out_shape=[_sds((d, D_FF)), _sds((d, D_FF))],
                          compiler_params=_params(2))(h, norm, dg, du)


def _sgu_mix(vn, ws_ref, bst):
    tril = lax.broadcasted_iota(jnp.int32, (CHUNK, CHUNK), 0) >= lax.broadcasted_iota(jnp.int32, (CHUNK, CHUNK), 1)
    wms = [jnp.where(tril, ws_ref[g], 0.0) for g in range(SGU_GROUPS)]
    chunks = []
    for n in range(vn.shape[0] // CHUNK):
        vc = vn[n * CHUNK:(n + 1) * CHUNK, :]
        chunks.append(jnp.concatenate(
            [_nn(wms[g], vc[:, g * CHUNK:(g + 1) * CHUNK]) + bst[:, g:g + 1] for g in range(SGU_GROUPS)], axis=1))
    return jnp.concatenate(chunks, axis=0)


def _sgu_fwd(h, w, tm):
    t, d = h.shape

    def body(h_ref, cn_ref, win_ref, lg_ref, lb_ref, ws_ref, bst_ref, wout_ref, h_out):
        h_t = h_ref[...]
        z = _gelu(_nn(_rms(h_t, cn_ref[...]), win_ref[...]))
        vn = _layer_norm(z[:, d:], lg_ref[...], lb_ref[...])
        s = _sgu_mix(vn, ws_ref, bst_ref[...])
        h_out[...] = h_t + _nn(z[:, :d] * s, wout_ref[...])

    return pl.pallas_call(
        body, name="sgu_fwd", grid=(t // tm,),
        in_specs=[_row(tm, d), _const((1, d)), _const((d, 2 * d)), _const((1, d)), _const((1, d)), _const((SGU_GROUPS, CHUNK, CHUNK)),
                  _const((CHUNK, LANES)), _const((d, d))],
        out_specs=_row(tm, d), out_shape=_sds((t, d)), compiler_params=_params(),
    )(h, w['c_norm'], w['c_w_in'], w['c_ln_g'], w['c_ln_b'], w['c_w_s'], w['bsT'], w['c_w_out'])


def _sgu_bwd(h, dout, w, tm):
    t, d = h.shape

    def body(h_ref, dout_ref, cn_ref, win_ref, lg_ref, lb_ref, ws_ref, bst_ref, wout_ref,
             dh_out, dcn_out, dwin_out, dlg_out, dlb_out, dws_out, dbst_out, dwout_out):
        first = pl.program_id(0) == 0
        hn, vjp_norm = jax.vjp(_rms, h_ref[...], cn_ref[...])
        zpre = _nn(hn, win_ref[...])
        u, vjp_u = jax.vjp(_gelu, zpre[:, :d])
        vn, vjp_v = jax.vjp(lambda zp, lg, lb: _layer_norm(_gelu(zp), lg, lb), zpre[:, d:], lg_ref[...], lb_ref[...])
        s = _sgu_mix(vn, ws_ref, bst_ref[...])
        dout_t = dout_ref[...]
        dus = _nt(dout_t, wout_ref[...])
        _accumulate(dwout_out, _tn(u * s, dout_t), first)
        ds = dus * u
        tril = lax.broadcasted_iota(jnp.int32, (CHUNK, CHUNK), 0) >= lax.broadcasted_iota(jnp.int32, (CHUNK, CHUNK), 1)
        lane = lax.broadcasted_iota(jnp.int32, (CHUNK, LANES), 1)
        dws = [jnp.zeros((CHUNK, CHUNK), F32) for _ in range(SGU_GROUPS)]
        dbst = jnp.zeros((CHUNK, LANES), F32)
        dvn_chunks = []
        for n in range(tm // CHUNK):
            cols = []
            for g in range(SGU_GROUPS):
                ds_ng = ds[n * CHUNK:(n + 1) * CHUNK, g * CHUNK:(g + 1) * CHUNK]
                vc_ng = vn[n * CHUNK:(n + 1) * CHUNK, g * CHUNK:(g + 1) * CHUNK]
                cols.append(_tn(jnp.where(tril, ws_ref[g], 0.0), ds_ng))
                dws[g] = dws[g] + _nt(ds_ng, vc_ng)
                dbst = dbst + jnp.where(lane == g, jnp.sum(ds_ng, axis=1, keepdims=True), 0.0)
            dvn_chunks.append(jnp.concatenate(cols, axis=1))
        dvn = jnp.concatenate(dvn_chunks, axis=0)
        for g in range(SGU_GROUPS):
            val = jnp.where(tril, dws[g], 0.0)

            @pl.when(first)
            def _():
                dws_out[g] = val

            @pl.when(jnp.logical_not(first))
            def _():
                dws_out[g] += val
        _accumulate(dbst_out, dbst, first)
        (dzu,) = vjp_u(dus * s)
        dzv, dlg, dlb = vjp_v(dvn)
        _accumulate(dlg_out, dlg, first)
        _accumulate(dlb_out, dlb, first)
        dzpre = jnp.concatenate([dzu, dzv], axis=1)
        _accumulate(dwin_out, _tn(hn, dzpre), first)
        dh, dcn = vjp_norm(_nt(dzpre, win_ref[...]))
        _accumulate(dcn_out, dcn, first)
        dh_out[...] = dh + dout_t

    return pl.pallas_call(
        body, name="sgu_bwd", grid=(t // tm,),
        in_specs=[_row(tm, d), _row(tm, d), _const((1, d)), _const((d, 2 * d)), _const((1, d)), _const((1, d)),
                  _const((SGU_GROUPS, CHUNK, CHUNK)), _const((CHUNK, LANES)), _const((d, d))],
        out_specs=[_row(tm, d), _const((1, d)), _const((d, 2 * d)), _const((1, d)), _const((1, d)), _const((SGU_GROUPS, CHUNK, CHUNK)),
                   _const((CHUNK, LANES)), _const((d, d))],
        out_shape=[_sds((t, d)), _sds((1, d)), _sds((d, 2 * d)), _sds((1, d)), _sds((1, d)), _sds((SGU_GROUPS, CHUNK, CHUNK)),
                   _sds((CHUNK, LANES)), _sds((d, d))],
        compiler_params=_params(),
    )(h, dout, w['c_norm'], w['c_w_in'], w['c_ln_g'], w['c_ln_b'], w['c_w_s'], w['bsT'], w['c_w_out'])


def _final_loss(h, target, norm, tm):
    t, d = h.shape

    def body(h_ref, tgt_ref, gn_ref, dh_out, loss_out, dgn_out):
        first = pl.program_id(0) == 0
        tgt = tgt_ref[...]

        def loss_fn(h_, g_):
            err = _rms(h_, g_) - tgt
            return 0.5 * jnp.sum(jnp.mean(err * err, axis=-1, keepdims=True), axis=0, keepdims=True)

        loss, vjp_loss = jax.vjp(loss_fn, h_ref[...], gn_ref[...])
        dh, dgn = vjp_loss(jnp.ones((1, 1), F32))
        dh_out[...] = dh
        _accumulate(loss_out, loss, first)
        _accumulate(dgn_out, dgn, first)

    return pl.pallas_call(body, name="final_loss", grid=(t // tm,), in_specs=[_row(tm, d), _row(tm, d), _const((1, d))],
                          out_specs=[_row(tm, d), _const((1, 1)), _const((1, d))],
                          out_shape=[_sds((t, d)), _sds((1, 1)), _sds((1, d))], compiler_params=_params())(h, target, norm)


def _tile(t, seq, want):
    tm = min(want, seq)
    assert seq % tm == 0 and t % tm == 0 and tm % CHUNK == 0
    return tm


def _local_step(x, posb, target, w, seq, late_weights, on_grads):
    t, d = x.shape
    b = t // seq
    hp = HEADS * HEAD_PAD
    tm_big, tm_mid = _tile(t, seq, 512), _tile(t, seq, 256)
    tq = _tile(t, seq, 512)

    q, k, v, xl, gate = _ab_in_fwd(x, posb, w, tm_big)
    o = _attn_fwd(q.reshape(b, seq, hp), k.reshape(b, seq, hp), v.reshape(b, seq, hp), tq).reshape(t, hp)
    y, hs = _lru_fwd(xl, gate, w, tm_big, seq)
    w = {**w, **late_weights('out0', y)}
    h1 = _ab_out_fwd(x, o, y, w, tm_big)
    hcur = h1
    saved = []
    for l in range(2):
        if l == 1:
            w = {**w, **late_weights('mix1', hcur)}
            saved_h2 = hcur
            hcur = _sgu_fwd(hcur, w, tm_mid)
        wl = late_weights('ffn%d' % l, hcur)
        g, u = _ffn_a_fwd(hcur, w['ffn_norm'][l], wl['Wg'], wl['Wu'], tm_big)
        hnext = _ffn_b_fwd(g, u, hcur, w['ffn_conv_w'][l], w['ffn_conv_b'][l], wl['Wd'], tm_mid, seq)
        saved.append((hcur, g, u, wl))
        hcur = hnext
    dh, loss, d_final = _final_loss(hcur, target, w['final_norm'], tm_big)

    ffn = {}
    conv_b = list(w['ffn_conv_b'])
    for l in (1, 0):
        hin, g, u, wl = saved[l]
        dgc, du, d_wd, d_cw, d_cb = _ffn_b_bwd(g, u, dh, w['ffn_conv_w'][l], conv_b[l], wl['Wd'], tm_big, seq)
        dh, dg, d_norm = _ffn_a_dgrad(hin, w['ffn_norm'][l], dgc, du, dh, w['ffn_conv_w'][l], wl['Wg'], wl['Wu'], tm_mid, seq)
        d_wg, d_wu = _ffn_a_wgrad(hin, w['ffn_norm'][l], dg, du, _tile(t, seq, 1024))
        ffn[l] = dict(ffn_norm=d_norm, ffn_conv_w=d_cw, ffn_conv_b=d_cb, Wg=d_wg, Wu=d_wu, Wd=d_wd)
        if l == 1:
            dh, d_cn, d_cwin, d_lg, d_lb, d_ws, d_bst, d_cwout = _sgu_bwd(saved_h2, dh, w, tm_mid)
            zero = on_grads('late1', dict(final_norm=d_final, c_norm=d_cn, c_ln_g=d_lg, c_ln_b=d_lb, c_w_s=d_ws, bsT=d_bst, c_w_in=d_cwin,
                                          c_w_out=d_cwout, Wg=[d_wg], Wu=[d_wu], Wd=[d_wd]))
            conv_b[0] = conv_b[0] + zero
    late0 = {name: [ffn[0][name], ffn[1][name]] for name in ('ffn_norm', 'ffn_conv_w', 'ffn_conv_b')}
    zero = on_grads('late0', dict(late0, Wg=[ffn[0]['Wg']], Wu=[ffn[0]['Wu']], Wd=[ffn[0]['Wd']]))
    w = {**w, 'Wo_b': w['Wo_b'] + zero.astype(w['Wo_b'].dtype)}
    do, dy, d_woa, d_wob = _ab_out_bwd(o, y, dh, w, tm_big)
    dxl, dgate, d_cw, d_cb, d_wa, d_ba, d_wx, d_bx, d_lam = _lru_bwd(xl, gate, hs, dy, w, tm_big, seq)
    zero = on_grads('mid', dict(Wo_a=d_woa, Wo_b=d_wob, ab_conv_w=d_cw, ab_conv_b=d_cb, Wa=d_wa, ab_b_rg_a=d_ba, Wx=d_wx,
                                ab_b_rg_x=d_bx, ab_lambda=d_lam))
    w = {**w, 'ab_norm': w['ab_norm'] + zero}
    dq, dk, dv = _attn_bwd(q.reshape(b, seq, hp), k.reshape(b, seq, hp), v.reshape(b, seq, hp), do.reshape(b, seq, hp), tq)
    dx, d_gn, d_win, d_qn, d_wq, d_kvn, d_wk, d_wv = _ab_in_bwd(
        x, posb, w, dq.reshape(t, hp), dk.reshape(t, hp), dv.reshape(t, hp), dxl, dgate, dh, tm_mid)
    return loss, dx, dict(ab_norm=d_gn, W_in=d_win, ab_q_norm=d_qn, Wq=d_wq, ab_kv_norm=d_kvn, Wk=d_wk, Wv=d_wv)


def _block_diag(wg):
    g, n, _ = wg.shape
    return jnp.einsum('gij,gh->gihj', wg, jnp.eye(g, dtype=wg.dtype)).reshape(g * n, g * n)


def _prepare_out(w_out):
    d = w_out.shape[2]
    mla = HEADS * QK_NOPE
    return {'Wo_a': jnp.pad(w_out[0, :mla].reshape(HEADS, QK_NOPE, d), ((0, 0), (0, HEAD_PAD - QK_NOPE), (0, 0))).reshape(HEADS * HEAD_PAD, d),
            'Wo_b': w_out[0, mla:]}


def _prepare(full):
    d = full['ab_w_in'].shape[1]
    w_in = full['ab_w_in'][0]
    zeros = lambda n: jnp.zeros((d, n), w_in.dtype)
    wq = full['ab_w_q_b'][0].reshape(Q_LORA, HEADS, QK_NOPE + QK_ROPE)
    wkv = full['ab_w_kv_b'][0].reshape(KV_LORA, HEADS, 2 * QK_NOPE)
    pad_head = lambda a: jnp.pad(a, ((0, 0), (0, 0), (0, HEAD_PAD - a.shape[2]))).reshape(a.shape[0], HEADS * HEAD_PAD)
    w = {
        'W_in': jnp.concatenate([w_in[:, :Z_KPE], zeros(QK_NOPE), w_in[:, Z_KPE:Z_KPE + QK_ROPE],
                                 zeros(HEAD_PAD - QK_NOPE - QK_ROPE), w_in[:, Z_KPE + QK_ROPE:]], axis=1),
        'Wq': pad_head(wq), 'Wk': pad_head(wkv[:, :, :QK_NOPE]), 'Wv': pad_head(wkv[:, :, QK_NOPE:]),
        'Wa': _bf(_block_diag(full['ab_w_rg_a'][0])), 'Wx': _bf(_block_diag(full['ab_w_rg_x'][0])),
        'c_w_s': full['c_w_s'][0],
        'bsT': jnp.pad(full['c_b_s'][0].T, ((0, 0), (0, LANES - SGU_GROUPS))),
        'ffn_norm': [full['ffn_norm'][l:l + 1] for l in range(2)], 'ffn_conv_w': [full['ffn_conv_w'][l] for l in range(2)],
        'ffn_conv_b': [full['ffn_conv_b'][l:l + 1] for l in range(2)],
        'ab_conv_w': full['ab_conv_w'][0], 'final_norm': full['final_norm'][None, :],
    }
    for name in ('ab_norm', 'ab_q_norm', 'ab_kv_norm', 'ab_conv_b', 'ab_b_rg_a', 'ab_b_rg_x', 'ab_lambda', 'c_norm', 'c_ln_g', 'c_ln_b'):
        w[name] = full[name]
    return w


def _unprepare(g):
    unpad_head = lambda a, n: a.reshape(a.shape[0], HEADS, HEAD_PAD)[:, :, :n]
    diag = lambda a: jnp.einsum('gigj->gij', a.reshape(HEADS, LRU_W // HEADS, HEADS, LRU_W // HEADS))
    rules = {
        'ab_w_in': (('W_in',), lambda a: jnp.concatenate([a[:, :Z_KPE], a[:, Z_KPE + QK_NOPE:Z_KPE + QK_NOPE + QK_ROPE], a[:, Z_LRU:]], axis=1)[None]),
        'ab_w_q_b': (('Wq',), lambda a: unpad_head(a, QK_NOPE + QK_ROPE).reshape(1, Q_LORA, -1)),
        'ab_w_kv_b': (('Wk', 'Wv'), lambda a, b: jnp.concatenate([unpad_head(a, QK_NOPE), unpad_head(b, QK_NOPE)], axis=2).reshape(1, KV_LORA, -1)),
        'ab_w_out': (('Wo_a', 'Wo_b'), lambda a, b: jnp.concatenate(
            [a.reshape(HEADS, HEAD_PAD, -1)[:, :QK_NOPE].reshape(HEADS * QK_NOPE, -1), b], axis=0)[None]),
        'ab_w_rg_a': (('Wa',), lambda a: diag(a)[None]), 'ab_w_rg_x': (('Wx',), lambda a: diag(a)[None]),
        'c_w_in': (('c_w_in',), lambda a: a[None]), 'c_w_out': (('c_w_out',), lambda a: a[None]), 'c_w_s': (('c_w_s',), lambda a: a[None]),
        'c_b_s': (('bsT',), lambda a: a[:, :SGU_GROUPS].T[None]),
        'ffn_w_gate': (('Wg',), jnp.stack), 'ffn_w_up': (('Wu',), jnp.stack), 'ffn_w_down': (('Wd',), jnp.stack),
        'ffn_norm': (('ffn_norm',), lambda a: jnp.concatenate(a, axis=0)), 'ffn_conv_w': (('ffn_conv_w',), jnp.stack),
        'ffn_conv_b': (('ffn_conv_b',), lambda a: jnp.concatenate(a, axis=0)),
        'ab_conv_w': (('ab_conv_w',), lambda a: a[None]), 'final_norm': (('final_norm',), lambda a: a[0]),
    }
    for name in ('ab_norm', 'ab_q_norm', 'ab_kv_norm', 'ab_conv_b', 'ab_b_rg_a', 'ab_b_rg_x', 'ab_lambda', 'c_norm', 'c_ln_g', 'c_ln_b'):
        rules[name] = ((name,), lambda a: a)
    return {name: fn(*[g[k] for k in keys]) for name, (keys, fn) in rules.items() if all(k in g for k in keys)}


SLAB_ROWS = 16


def _round_up(n, m):
    return -(-n // m) * m


def _to_chunks(full, axis):
    s = full.shape
    return jnp.moveaxis(full.reshape(s[:axis] + (N_DEV, s[axis] // N_DEV) + s[axis + 1:]), axis, 0)


def _from_chunks(chunks, axis):
    local = chunks.shape[1:]
    return jnp.moveaxis(chunks, 0, axis).reshape(local[:axis] + (N_DEV * local[axis],) + local[axis + 1:])


def _slab_rows(n):
    return _round_up(-(-n // LANES), SLAB_ROWS)


def _to_slab(a, lead):
    a = a.reshape(lead + (-1,))
    rows = _slab_rows(a.shape[-1])
    a = jnp.pad(a, [(0, 0)] * len(lead) + [(0, rows * LANES - a.shape[-1])])
    return a.reshape(lead + (rows, LANES))


def _pack_slabs(parts, lead):
    return jnp.concatenate([_to_slab(p, lead) for p in parts], axis=len(lead))


def _unpack_slabs(packed, shapes):
    lead = packed.shape[:-2]
    out, row = [], 0
    for shape in shapes:
        size = math.prod(shape)
        rows = _slab_rows(size)
        piece = lax.slice_in_dim(packed, row, row + rows, axis=len(lead))
        out.append(piece.reshape(lead + (rows * LANES,))[..., :size].reshape(lead + tuple(shape)))
        row += rows
    return out


HBM = pl.BlockSpec(memory_space=pl.ANY)


def _other_chips(x, y):
    return [(1 - x, y), (x, 1 - y), (1 - x, 1 - y)]


def _all_gather(blocks):
    n = len(blocks)

    def body(*refs):
        x_refs, out_refs, token = refs[:n], refs[n:2 * n], refs[2 * n]
        send_sems, recv_sems, local_sems = refs[2 * n + 1:]
        token[...] = jnp.zeros_like(token)
        x, y, c = lax.axis_index("x"), lax.axis_index("y"), lax.axis_index("c")
        me, sibling = (x, y, c), (x, y, 1 - c)
        chips = _other_chips(x, y)

        def slab(a, px, py, pc):
            return out_refs[a].at[4 * px + 2 * py + pc]

        def copy(a, k, blk, to, src=None):
            return pltpu.make_async_remote_copy(src_ref=slab(a, *blk) if src is None else src, dst_ref=slab(a, *blk),
                                                send_sem=send_sems.at[7 * a + k], recv_sem=recv_sems.at[7 * a + k],
                                                device_id=to, device_id_type=MESH)

        mine = [pltpu.make_async_copy(x_refs[a], slab(a, *me), local_sems.at[a]) for a in range(n)]
        started = []
        for a in range(n):
            mine[a].start()
            started.append(copy(a, 0, me, sibling, src=x_refs[a]))
            started += [copy(a, 1 + j, me, (*chip, c), src=x_refs[a]) for j, chip in enumerate(chips)]
        for cp in started:
            cp.start()
        for j, chip in enumerate(chips):
            for a in range(n):
                copy(a, 1 + j, (*chip, c), me).wait_recv()
                passed = copy(a, 4 + j, (*chip, c), sibling)
                passed.start()
                started.append(passed)
        for a in range(n):
            copy(a, 0, sibling, me).wait_recv()
        for j, chip in enumerate(chips):
            for a in range(n):
                copy(a, 4 + j, (*chip, 1 - c), me).wait_recv()
        for cp in started:
            cp.wait_send()
        for a in range(n):
            mine[a].wait()

    out = pl.pallas_call(
        body, name="all_gather_weights",
        out_shape=[jax.ShapeDtypeStruct((N_DEV,) + b.shape, b.dtype) for b in blocks] + [jax.ShapeDtypeStruct((8, LANES), F32)],
        in_specs=[HBM] * n, out_specs=[HBM] * n + [pl.BlockSpec(memory_space=pltpu.VMEM)],
        scratch_shapes=[pltpu.SemaphoreType.DMA((7 * n,)), pltpu.SemaphoreType.DMA((7 * n,)), pltpu.SemaphoreType.DMA((n,))],
    )(*blocks)
    return list(out[:n]), out[n][0, 0]


FLIPS = [(0, 0, 1), (1, 0, 0), (1, 0, 1), (0, 1, 0), (0, 1, 1), (1, 1, 0), (1, 1, 1)]


def _peers(x, y, c):
    flip = lambda v, f: 1 - v if f else v
    return [(flip(x, fx), flip(y, fy), flip(c, fc)) for fx, fy, fc in FLIPS]


def _direct_copies(src_refs, land_refs, send_sems, recv_sems, scatter):
    x, y, c = lax.axis_index("x"), lax.axis_index("y"), lax.axis_index("c")
    me = 4 * x + 2 * y + c
    starts, waits = [], []
    for a in range(len(src_refs)):
        for k, (px, py, pc) in enumerate(_peers(x, y, c)):
            peer = 4 * px + 2 * py + pc
            sems = dict(send_sem=send_sems.at[7 * a + k], recv_sem=recv_sems.at[7 * a + k], device_id=(px, py, pc), device_id_type=MESH)
            src = src_refs[a].at[peer] if scatter else src_refs[a]
            starts.append(pltpu.make_async_remote_copy(src_ref=src, dst_ref=land_refs[a].at[me], **sems))
            waits.append(pltpu.make_async_remote_copy(src_ref=src, dst_ref=land_refs[a].at[peer], **sems))
    return starts, waits


def _landing(src, scatter):
    block = src.shape[1:] if scatter else src.shape
    return jax.ShapeDtypeStruct((N_DEV,) + block, src.dtype)


HBM_SPACE = pl.BlockSpec(memory_space=pltpu.HBM)
SEMAPHORES = pl.BlockSpec(memory_space=pltpu.SEMAPHORE)
SPLIT_EFFECT = pltpu.SideEffectType.DATAFLOW_SIDE_EFFECTING


def _start_exchange(name, srcs, scatter):
    n = len(srcs)
    lands = [lax.empty(s.shape, s.dtype) for s in (_landing(s, scatter) for s in srcs)]

    def body(*refs):
        starts, _ = _direct_copies(refs[:n], refs[n:2 * n], refs[2 * n], refs[2 * n + 1], scatter)
        for cp in starts:
            cp.start()
        refs[-1][...] = jnp.zeros_like(refs[-1])

    held = [pltpu.with_memory_space_constraint(a, pltpu.HBM) for a in list(srcs) + lands]
    out = pl.pallas_call(
        body, name=name + "_start",
        out_shape=(pltpu.SemaphoreType.DMA((7 * n,)), pltpu.SemaphoreType.DMA((7 * n,)), *[pltpu.HBM(a.shape, a.dtype) for a in held],
                   jax.ShapeDtypeStruct((8, LANES), F32)),
        in_specs=[HBM_SPACE] * (2 * n), out_specs=(SEMAPHORES, SEMAPHORES, *[HBM_SPACE] * (2 * n), pl.BlockSpec(memory_space=pltpu.VMEM)),
        input_output_aliases={i: 2 + i for i in range(2 * n)},
        compiler_params=pltpu.CompilerParams(has_side_effects=SPLIT_EFFECT),
    )(*held)
    return out[0], out[1], list(out[2:2 + n]), list(out[2 + n:2 + 2 * n]), out[-1][0, 0], out[-1]


def _wait_exchange(name, started, after, scatter):
    send_sems, recv_sems, srcs, lands = started[:4]
    n = len(srcs)

    def body(*refs):
        _, waits = _direct_copies(refs[:n], refs[n:2 * n], refs[2 * n], refs[2 * n + 1], scatter)
        for cp in waits:
            cp.wait_send()
        for cp in waits:
            cp.wait_recv()

    out = pl.pallas_call(
        body, name=name + "_wait", out_shape=tuple(pltpu.HBM(a.shape, a.dtype) for a in srcs + lands),
        in_specs=[HBM_SPACE] * (2 * n) + [SEMAPHORES, SEMAPHORES, HBM], out_specs=tuple([HBM_SPACE] * (2 * n)),
        input_output_aliases={i: i for i in range(2 * n)},
        compiler_params=pltpu.CompilerParams(has_side_effects=SPLIT_EFFECT),
    )(*srcs, *lands, send_sems, recv_sems, after)
    return list(out[:n]), list(out[n:])


def _row_tile(rows):
    return rows // 2 if (rows // 2) % SLAB_ROWS == 0 else rows


def _sum_and_adamw(me, landed, own, wts, m, v, name, layer=None, into=None):
    layers, r, n = wts.shape
    first = 0 if layer is None else layer
    count = layers if layer is None else 1
    tr = _row_tile(r)
    blk = pl.BlockSpec((1, tr, n), lambda li, ri, me_ref: (first + li, ri, 0))
    c1 = 1.0 / (1.0 - ADAM_B1 ** ADAM_STEP)
    c2 = 1.0 / (1.0 - ADAM_B2 ** ADAM_STEP)
    held = [] if into is None else list(into)

    def body(me_ref, l_ref, own_ref, w_ref, m_ref, v_ref, *rest):
        g_out, d_out, m_out, v_out = rest[len(held):]
        mine = own_ref[0].astype(F32)
        g = jnp.where(me_ref[0] == 0, mine, l_ref[0].astype(F32))
        for dev in range(1, N_DEV):
            g = g + jnp.where(me_ref[0] == dev, mine, l_ref[dev].astype(F32))
        m_new = ADAM_B1 * m_ref[...] + (1.0 - ADAM_B1) * g
        v_new = ADAM_B2 * v_ref[...] + (1.0 - ADAM_B2) * (g * g)
        g_out[...] = g
        m_out[...] = m_new
        v_out[...] = v_new
        d_out[...] = -ADAM_LR * ((m_new * c1) / (jnp.sqrt(v_new * c2) + ADAM_EPS) + ADAM_WD * w_ref[...])

    return pl.pallas_call(
        body, name="adamw_" + name,
        grid_spec=pltpu.PrefetchScalarGridSpec(
            num_scalar_prefetch=1, grid=(count, r // tr),
            in_specs=[pl.BlockSpec((N_DEV, 1, tr, n), lambda li, ri, me_ref: (0, li, ri, 0)),
                      pl.BlockSpec((1, 1, tr, n), lambda li, ri, me_ref: (me_ref[0], li, ri, 0)), blk, blk, blk] + [HBM] * len(held),
            out_specs=[blk] * 4),
        out_shape=[_sds((layers, r, n))] * 4, input_output_aliases={6 + i: i for i in range(len(held))},
        compiler_params=_params(2))(me, landed, own, wts, m, v, *held)


EARLY = ['ab_w_in']
LATE_STAGES = {
    'out0': [('ab_w_out', None, 'ab_w_out')],
    'ffn0': [('ffn_w_gate', 0, 'Wg'), ('ffn_w_up', 0, 'Wu'), ('ffn_w_down', 0, 'Wd')],
    'mix1': [('c_w_in', None, 'c_w_in'), ('c_w_out', None, 'c_w_out')],
    'ffn1': [('ffn_w_gate', 1, 'Wg'), ('ffn_w_up', 1, 'Wu'), ('ffn_w_down', 1, 'Wd')],
}
GRAD_STAGES = {
    'late1': ([('c_w_in', None), ('c_w_out', None), ('ffn_w_gate', 1), ('ffn_w_up', 1), ('ffn_w_down', 1)],
              ['c_norm', 'c_ln_g', 'c_ln_b', 'c_w_s', 'c_b_s', 'final_norm']),
    'late0': ([('ffn_w_gate', 0), ('ffn_w_up', 0), ('ffn_w_down', 0)], ['ffn_norm', 'ffn_conv_w', 'ffn_conv_b']),
    'mid': ([('ab_w_out', None)], ['ab_conv_w', 'ab_conv_b', 'ab_w_rg_a', 'ab_b_rg_a', 'ab_w_rg_x', 'ab_b_rg_x', 'ab_lambda']),
    'last': ([('ab_w_in', None)], ['ab_norm', 'ab_q_norm', 'ab_w_q_b', 'ab_kv_norm', 'ab_w_kv_b']),
}


def _gather_early(local):
    small = [_bf(local[n]) if n in MATRICES else lax.bitcast_convert_type(local[n], BF16) for n in SMALL_SHARDED]
    gathered, zero = _all_gather([_bf(local[n]) for n in EARLY] + [_pack_slabs(small, ())])
    full = {n: local[n] for n in REPLICATED}
    for n, g in zip(EARLY, gathered):
        full[n] = _from_chunks(g, SHARD_AXIS[n])
    for n, p in zip(SMALL_SHARDED, _unpack_slabs(gathered[-1], [s.shape for s in small])):
        full[n] = _from_chunks(p if n in MATRICES else lax.bitcast_convert_type(p, F32), SHARD_AXIS[n])
    return full, zero


def kernel(x, positions, ab_norm, ab_w_in, ab_q_norm, ab_w_q_b, ab_kv_norm, ab_w_kv_b, ab_conv_w, ab_conv_b, ab_w_rg_a, ab_b_rg_a, ab_w_rg_x, ab_b_rg_x, ab_lambda, ab_w_out, c_norm, c_w_in, c_ln_g, c_ln_b, c_w_s, c_b_s, c_w_out, ffn_norm, ffn_w_gate, ffn_w_up, ffn_conv_w, ffn_conv_b, ffn_w_down, final_norm, loss_target, m_ab_norm, m_ab_w_in, m_ab_q_norm, m_ab_w_q_b, m_ab_kv_norm, m_ab_w_kv_b, m_ab_conv_w, m_ab_conv_b, m_ab_w_rg_a, m_ab_b_rg_a, m_ab_w_rg_x, m_ab_b_rg_x, m_ab_lambda, m_ab_w_out, m_c_norm, m_c_w_in, m_c_ln_g, m_c_ln_b, m_c_w_s, m_c_b_s, m_c_w_out, m_ffn_norm, m_ffn_w_gate, m_ffn_w_up, m_ffn_conv_w, m_ffn_conv_b, m_ffn_w_down, m_final_norm, v_ab_norm, v_ab_w_in, v_ab_q_norm, v_ab_w_q_b, v_ab_kv_norm, v_ab_w_kv_b, v_ab_conv_w, v_ab_conv_b, v_ab_w_rg_a, v_ab_b_rg_a, v_ab_w_rg_x, v_ab_b_rg_x, v_ab_lambda, v_ab_w_out, v_c_norm, v_c_w_in, v_c_ln_g, v_c_ln_b, v_c_w_s, v_c_b_s, v_c_w_out, v_ffn_norm, v_ffn_w_gate, v_ffn_w_up, v_ffn_conv_w, v_ffn_conv_b, v_ffn_w_down, v_final_norm):
    given = dict(locals())
    local = {n: given[n] for n in WEIGHTS}
    b, seq, d = x.shape
    t = b * seq

    me = (4 * lax.axis_index("x") + 2 * lax.axis_index("y") + lax.axis_index("c")).astype(jnp.int32)
    is_me = (jnp.arange(N_DEV, dtype=jnp.int32) == me).reshape(N_DEV, 1, 1, 1)

    full, zero = _gather_early(local)
    gathers = {}
    for stage, members in LATE_STAGES.items():
        srcs = [_bf((local[n] if layer is None else local[n][layer:layer + 1]) + zero) for n, layer, _ in members]
        gathers[stage] = _start_exchange('gather_' + stage, srcs, scatter=False)
        zero = gathers[stage][4]
    w = _prepare(full)
    w['ab_norm'] = w['ab_norm'] + zero

    def late_weights(stage, after):
        srcs, lands = _wait_exchange('gather_' + stage, gathers[stage], after, scatter=False)
        whole = [_from_chunks(jnp.where(is_me, s[None], l), SHARD_AXIS[n]) for (n, _, _), s, l in zip(LATE_STAGES[stage], srcs, lands)]
        if stage == 'out0':
            return _prepare_out(whole[0])
        return {key: a[0] for (_, _, key), a in zip(LATE_STAGES[stage], whole)}

    scatters = {}

    def start_scatter(stage, g):
        whole = _unprepare(g)
        big, small = GRAD_STAGES[stage]
        slab = [_to_chunks(whole[n], SHARD_AXIS[n]) if n in SHARD_AXIS else jnp.broadcast_to(whole[n][None], (N_DEV,) + whole[n].shape)
                for n in small]
        own = [_bf(_to_chunks(whole[n], SHARD_AXIS[n])) for n, _ in big] + [_bf(_pack_slabs(slab, (N_DEV,)))[:, None]]
        scatters[stage] = _start_exchange('scatter_' + stage, own, scatter=True)
        return scatters[stage][4]

    posb = jnp.broadcast_to(positions.astype(F32).reshape(t, 1), (t, LANES))
    loss, dx, grads = _local_step(x.reshape(t, d), posb, loss_target.reshape(t, d), w, seq, late_weights, start_scatter)
    start_scatter('last', grads)
    after = scatters['last'][5]

    me1 = me.reshape(1)
    updated = {}
    for stage, (big, small) in GRAD_STAGES.items():
        owns, landed = _wait_exchange('scatter_' + stage, scatters[stage], after, scatter=True)
        for (n, layer), own, land in zip(big, owns, landed):
            updated[n] = _sum_and_adamw(me1, land, own, given[n], given['m_' + n], given['v_' + n], n + ('' if layer is None else str(layer)),
                                        layer, updated.get(n))
        pack_small = lambda prefix: _pack_slabs([given[prefix + n] for n in small], ())[None]
        packed = _sum_and_adamw(me1, landed[-1], owns[-1], pack_small(''), pack_small('m_'), pack_small('v_'), 'small_' + stage)
        unpacked = [_unpack_slabs(p[0], [local[n].shape for n in small]) for p in packed]
        for i, n in enumerate(small):
            updated[n] = [u[i] for u in unpacked]
        after = sum([updated[n][1][:1, :1, :1] for n, _ in big], packed[1][:1, :1, :1])
    total = lax.psum(loss[0, 0], ("x", "y", "c"))
    return (total, dx.reshape(b, seq, d), *[updated[n][kind] for kind in range(4) for n in WEIGHTS])
```

```python
import math

import jax
import jax.numpy as jnp
from jax import lax
from jax.experimental import pallas as pl
from jax.experimental.pallas import tpu as pltpu

F32 = jnp.float32
BF16 = jnp.bfloat16
MESH = pl.DeviceIdType.MESH

N_DEV = 8
LANES = 128
HALO = 8
VMEM_LIMIT = 56 << 20

NORM_EPS = 1e-6
HEADS = 8
HEAD_PAD = 128
QK_NOPE = 64
QK_ROPE = 32
ROPE_HALF = 16
ROPE_BASE = 10000.0
ATTN_SCALE = (QK_NOPE + QK_ROPE) ** -0.5
LRU_C = 8.0
LRU_W = 512
CHUNK = 128
SGU_GROUPS = 8
D_FF = 2816
FF_BLOCKS = 2

ADAM_LR, ADAM_B1, ADAM_B2, ADAM_EPS, ADAM_WD, ADAM_STEP = 0.001, 0.9, 0.999, 1e-08, 0.01, 10

WEIGHTS = ['ab_norm', 'ab_w_in', 'ab_q_norm', 'ab_w_q_b', 'ab_kv_norm', 'ab_w_kv_b', 'ab_conv_w', 'ab_conv_b',
           'ab_w_rg_a', 'ab_b_rg_a', 'ab_w_rg_x', 'ab_b_rg_x', 'ab_lambda', 'ab_w_out', 'c_norm', 'c_w_in', 'c_ln_g',
           'c_ln_b', 'c_w_s', 'c_b_s', 'c_w_out', 'ffn_norm', 'ffn_w_gate', 'ffn_w_up', 'ffn_conv_w', 'ffn_conv_b',
           'ffn_w_down', 'final_norm']
SHARD_AXIS = {'ab_w_in': 2, 'ab_w_q_b': 2, 'ab_w_kv_b': 2, 'ab_conv_w': 2, 'ab_w_out': 1, 'c_norm': 1, 'c_w_in': 2,
              'c_ln_g': 1, 'c_ln_b': 1, 'c_w_out': 1, 'ffn_w_gate': 2, 'ffn_w_up': 2, 'ffn_conv_w': 2, 'ffn_w_down': 1}
MATRICES = ['ab_w_in', 'ab_w_q_b', 'ab_w_kv_b', 'ab_w_out', 'c_w_in', 'c_w_out', 'ffn_w_gate', 'ffn_w_up', 'ffn_w_down']
BIG = ['ab_w_in', 'c_w_in', 'ffn_w_gate', 'ffn_w_up', 'ab_w_out', 'c_w_out', 'ffn_w_down']
REPLICATED = [n for n in WEIGHTS if n not in SHARD_AXIS]
SMALL_SHARDED = [n for n in WEIGHTS if n in SHARD_AXIS and n not in BIG]


def _bf(x):
    return x.astype(BF16)


def _nn(a, b):
    return lax.dot_general(_bf(a), _bf(b), (((1,), (0,)), ((), ())), preferred_element_type=F32)


def _nt(a, b):
    return lax.dot_general(_bf(a), _bf(b), (((1,), (1,)), ((), ())), preferred_element_type=F32)


def _tn(a, b):
    return lax.dot_general(_bf(a), _bf(b), (((0,), (0,)), ((), ())), preferred_element_type=F32)


def _rms(x, g):
    return x * lax.rsqrt(jnp.mean(x * x, axis=-1, keepdims=True) + NORM_EPS) * g


def _layer_norm(x, g, b):
    xc = x - jnp.mean(x, axis=-1, keepdims=True)
    return xc * lax.rsqrt(jnp.mean(xc * xc, axis=-1, keepdims=True) + NORM_EPS) * g + b


def _gelu(x):
    return jax.nn.gelu(x)


STRIP = 16
STRIP_LANES = 384
GELU_C = math.sqrt(2.0 / math.pi)
GELU_A = 0.044715


def _gelu_and_grad(x):
    x2 = x * x
    t = jnp.tanh(x * (GELU_C + (GELU_C * GELU_A) * x2))
    half_x = 0.5 * x
    one_plus_t = 1.0 + t
    return half_x * one_plus_t, 0.5 * one_plus_t + half_x * (1.0 - t * t) * (GELU_C + (3.0 * GELU_C * GELU_A) * x2)


def _colsum(x):
    return jnp.sum(x, axis=0, keepdims=True)


def _softplus(x):
    return jnp.maximum(x, 0.0) + jnp.log1p(jnp.exp(-jnp.abs(x)))


@jax.custom_vjp
def _decay(x):
    a = jnp.exp(x)
    y = 2.0 * x
    series = -y * (1.0 + y * (1 / 2 + y * (1 / 6 + y * (1 / 24 + y * (1 / 120 + y * (1 / 720))))))
    return a, jnp.where(y < -0.3, 1.0 - a * a, series)


def _decay_fwd(x):
    a, gap = _decay(x)
    return (a, gap), a


def _decay_bwd(a, cts):
    return (a * (cts[0] - 2.0 * a * cts[1]),)


_decay.defvjp(_decay_fwd, _decay_bwd)


def _accumulate(ref, val, first):
    @pl.when(first)
    def _():
        ref[...] = val

    @pl.when(jnp.logical_not(first))
    def _():
        ref[...] += val


def _params(n_axes=1):
    return pltpu.CompilerParams(dimension_semantics=("arbitrary",) * n_axes, vmem_limit_bytes=VMEM_LIMIT)


def _row(tm, n):
    return pl.BlockSpec((tm, n), lambda i: (i, 0))


def _const(shape):
    nd = len(shape)
    return pl.BlockSpec(shape, lambda i: (0,) * nd, pipeline_mode=pl.Buffered(1))


def _prev_halo(tm, n):
    return pl.BlockSpec((HALO, n), lambda i: (jnp.maximum(i * (tm // HALO) - 1, 0), 0))


def _next_halo(tm, n, n_tiles):
    last = n_tiles * (tm // HALO) - 1
    return pl.BlockSpec((HALO, n), lambda i: (jnp.minimum((i + 1) * (tm // HALO), last), 0))


def _sds(shape, dtype=F32):
    return jax.ShapeDtypeStruct(shape, dtype)


def _rope_tables(posb):
    lane = lax.broadcasted_iota(jnp.int32, posb.shape, 1)
    in_rope = jnp.logical_and(lane >= QK_NOPE, lane < QK_NOPE + QK_ROPE)
    j = (lane & (ROPE_HALF - 1)).astype(F32)
    inv_freq = jnp.exp((-math.log(ROPE_BASE)) * j / ROPE_HALF)
    ang = posb * inv_freq
    return jnp.where(in_rope, jnp.cos(ang), 1.0), jnp.where(in_rope, jnp.sin(ang), 0.0)


def _rot(q):
    n = q.shape[1]
    lane = lax.broadcasted_iota(jnp.int32, q.shape, 1) & (HEAD_PAD - 1)
    first_half = jnp.where(lane >= QK_NOPE, -pltpu.roll(q, n - ROPE_HALF, 1), 0.0)
    second_half = jnp.where(lane < QK_NOPE + QK_ROPE, pltpu.roll(q, ROPE_HALF, 1), 0.0)
    return jnp.where(lane < QK_NOPE + ROPE_HALF, first_half, second_half)


def _rope(q, cos_t, sin_t):
    return q * cos_t + _rot(q) * sin_t


def _rope_transpose(dq, cos_t, sin_t):
    return dq * cos_t - _rot(dq * sin_t)


def _tile_heads(t):
    return jnp.concatenate([t] * HEADS, axis=1)


Q_LORA, KV_LORA = 256, 128
Z_KPE = Q_LORA + KV_LORA
Z_LRU = Z_KPE + HEAD_PAD
Z_GATE = Z_LRU + LRU_W
Z_WIDTH = Z_GATE + LRU_W


def _ab_in_fwd(x, posb, w, tm):
    t, d = x.shape

    def body(x_ref, pos_ref, gn_ref, win_ref, qn_ref, wq_ref, kvn_ref, wk_ref, wv_ref, q_out, k_out, v_out, xl_out, gate_out):
        hn = _rms(x_ref[...], gn_ref[...])
        z = _nn(hn, win_ref[...])
        cqn = _rms(z[:, :Q_LORA], qn_ref[...])
        kvn = _rms(z[:, Q_LORA:Z_KPE], kvn_ref[...])
        cos_t, sin_t = _rope_tables(pos_ref[...])
        q_out[...] = _rope(_nn(cqn, wq_ref[...]), _tile_heads(cos_t), _tile_heads(sin_t))
        kpe = _rope(z[:, Z_KPE:Z_LRU], cos_t, sin_t)
        k_out[...] = _nn(kvn, wk_ref[...]) + _tile_heads(kpe)
        v_out[...] = _nn(kvn, wv_ref[...])
        xl_out[...] = z[:, Z_LRU:Z_GATE]
        gate_out[...] = z[:, Z_GATE:]

    hp = HEADS * HEAD_PAD
    return pl.pallas_call(
        body, name="ab_in_fwd", grid=(t // tm,),
        in_specs=[_row(tm, d), _row(tm, LANES), _const((1, d)), _const((d, Z_WIDTH)), _const((1, Q_LORA)), _const((Q_LORA, hp)),
                  _const((1, KV_LORA)), _const((KV_LORA, hp)), _const((KV_LORA, hp))],
        out_specs=[_row(tm, hp), _row(tm, hp), _row(tm, hp), _row(tm, LRU_W), _row(tm, LRU_W)],
        out_shape=[_sds((t, hp)), _sds((t, hp)), _sds((t, hp)), _sds((t, LRU_W)), _sds((t, LRU_W))],
        compiler_params=_params(),
    )(x, posb, w['ab_norm'], w['W_in'], w['ab_q_norm'], w['Wq'], w['ab_kv_norm'], w['Wk'], w['Wv'])


def _ab_in_bwd(x, posb, w, dq, dk, dv, dxl, dgate, dres, tm):
    t, d = x.shape
    hp = HEADS * HEAD_PAD

    def body(x_ref, pos_ref, gn_ref, win_ref, qn_ref, wq_ref, kvn_ref, wk_ref, wv_ref, dq_ref, dk_ref, dv_ref, dxl_ref, dgate_ref,
             dres_ref, dx_out, dgn_out, dwin_out, dqn_out, dwq_out, dkvn_out, dwk_out, dwv_out):
        first = pl.program_id(0) == 0
        hn, vjp_in = jax.vjp(_rms, x_ref[...], gn_ref[...])
        z = _nn(hn, win_ref[...])
        cqn, vjp_q = jax.vjp(_rms, z[:, :Q_LORA], qn_ref[...])
        kvn, vjp_kv = jax.vjp(_rms, z[:, Q_LORA:Z_KPE], kvn_ref[...])
        cos_t, sin_t = _rope_tables(pos_ref[...])
        dq0 = _rope_transpose(dq_ref[...], _tile_heads(cos_t), _tile_heads(sin_t))
        dk0 = dk_ref[...]
        dv0 = dv_ref[...]
        dkpe = dk0[:, :HEAD_PAD]
        for h in range(1, HEADS):
            dkpe = dkpe + dk0[:, h * HEAD_PAD:(h + 1) * HEAD_PAD]
        dkpe = _rope_transpose(dkpe, cos_t, sin_t)
        _accumulate(dwq_out, _tn(cqn, dq0), first)
        _accumulate(dwk_out, _tn(kvn, dk0), first)
        _accumulate(dwv_out, _tn(kvn, dv0), first)
        dcq, dqn = vjp_q(_nt(dq0, wq_ref[...]))
        dckv, dkvn = vjp_kv(_nt(dk0, wk_ref[...]) + _nt(dv0, wv_ref[...]))
        _accumulate(dqn_out, dqn, first)
        _accumulate(dkvn_out, dkvn, first)
        dz = jnp.concatenate([dcq, dckv, dkpe, dxl_ref[...], dgate_ref[...]], axis=1)
        _accumulate(dwin_out, _tn(hn, dz), first)
        dx, dgn = vjp_in(_nt(dz, win_ref[...]))
        _accumulate(dgn_out, dgn, first)
        dx_out[...] = dx + dres_ref[...]

    return pl.pallas_call(
        body, name="ab_in_bwd", grid=(t // tm,),
        in_specs=[_row(tm, d), _row(tm, LANES), _const((1, d)), _const((d, Z_WIDTH)), _const((1, Q_LORA)), _const((Q_LORA, hp)),
                  _const((1, KV_LORA)), _const((KV_LORA, hp)), _const((KV_LORA, hp)),
                  _row(tm, hp), _row(tm, hp), _row(tm, hp), _row(tm, LRU_W), _row(tm, LRU_W), _row(tm, d)],
        out_specs=[_row(tm, d), _const((1, d)), _const((d, Z_WIDTH)), _const((1, Q_LORA)), _const((Q_LORA, hp)),
                   _const((1, KV_LORA)), _const((KV_LORA, hp)), _const((KV_LORA, hp))],
        out_shape=[_sds((t, d)), _sds((1, d)), _sds((d, Z_WIDTH)), _sds((1, Q_LORA)), _sds((Q_LORA, hp)),
                   _sds((1, KV_LORA)), _sds((KV_LORA, hp)), _sds((KV_LORA, hp))],
        compiler_params=_params(),
    )(x, posb, w['ab_norm'], w['W_in'], w['ab_q_norm'], w['Wq'], w['ab_kv_norm'], w['Wk'], w['Wv'], dq, dk, dv, dxl, dgate, dres)


def _attn_probs(q_blk, k_ext, i, tq):
    ext = k_ext.shape[0]
    s = lax.dot_general(q_blk, k_ext, (((1,), (1,)), ((), ())), preferred_element_type=F32) * ATTN_SCALE
    causal = lax.broadcasted_iota(jnp.int32, (tq, tq), 1) <= lax.broadcasted_iota(jnp.int32, (tq, tq), 0)
    diag = jnp.where(causal, s[:, ext - tq:], -1e30)
    s = diag if ext == tq else jnp.concatenate([s[:, :ext - tq], diag], axis=1)
    p = jnp.exp(s - jnp.max(s, axis=1, keepdims=True))
    return p / jnp.sum(p, axis=1, keepdims=True)


def _attn_fwd(q, k, v, tq):
    b, s, hp = q.shape
    blk = pl.BlockSpec((1, s, HEAD_PAD), lambda bi, h: (bi, 0, h))

    def body(q_ref, k_ref, v_ref, o_ref):
        kb = _bf(k_ref[0])
        vb = _bf(v_ref[0])
        for i in range(s // tq):
            ext = (i + 1) * tq
            p = _attn_probs(_bf(q_ref[0, i * tq:ext, :]), kb[:ext], i, tq)
            o_ref[0, i * tq:ext, :] = lax.dot_general(_bf(p), vb[:ext], (((1,), (0,)), ((), ())), preferred_element_type=F32)

    return pl.pallas_call(body, name="attn_fwd", grid=(b, HEADS), in_specs=[blk, blk, blk], out_specs=blk,
                          out_shape=_sds((b, s, hp)), compiler_params=_params(2))(q, k, v)


def _attn_bwd(q, k, v, do, tq):
    b, s, hp = q.shape
    blk = pl.BlockSpec((1, s, HEAD_PAD), lambda bi, h: (bi, 0, h))

    def body(q_ref, k_ref, v_ref, do_ref, dq_ref, dk_ref, dv_ref):
        kb = _bf(k_ref[0])
        vb = _bf(v_ref[0])
        dk_ref[...] = jnp.zeros_like(dk_ref)
        dv_ref[...] = jnp.zeros_like(dv_ref)
        for i in range(s // tq):
            ext = (i + 1) * tq
            qb = _bf(q_ref[0, i * tq:ext, :])
            dob = _bf(do_ref[0, i * tq:ext, :])
            p = _attn_probs(qb, kb[:ext], i, tq)
            dv_ref[0, :ext, :] += lax.dot_general(_bf(p), dob, (((0,), (0,)), ((), ())), preferred_element_type=F32)
            dp = lax.dot_general(dob, vb[:ext], (((1,), (1,)), ((), ())), preferred_element_type=F32)
            ds = _bf(p * (dp - jnp.sum(p * dp, axis=1, keepdims=True)) * ATTN_SCALE)
            dq_ref[0, i * tq:ext, :] = lax.dot_general(ds, kb[:ext], (((1,), (0,)), ((), ())), preferred_element_type=F32)
            dk_ref[0, :ext, :] += lax.dot_general(ds, qb, (((0,), (0,)), ((), ())), preferred_element_type=F32)

    return pl.pallas_call(body, name="attn_bwd", grid=(b, HEADS), in_specs=[blk, blk, blk, blk], out_specs=[blk, blk, blk],
                          out_shape=[_sds((b, s, hp))] * 3, compiler_params=_params(2))(q, k, v, do)


LRU_CONV = 4


def _lru_point(pre_a, pre_x, xc, lam):
    r = jax.nn.sigmoid(pre_a)
    i = jax.nn.sigmoid(pre_x)
    a, gap = _decay(-LRU_C * r * _softplus(-lam))
    return a, jnp.sqrt(gap) * (i * xc)


def _causal_conv(pad_ref, x, halo, first_in_seq, w, taps):
    tm = x.shape[0]
    pad_ref[:HALO, :] = jnp.where(first_in_seq, 0.0, halo)
    pad_ref[HALO:, :] = x
    y = w[taps - 1:taps, :] * x
    for k in range(taps - 1):
        off = HALO - (taps - 1) + k
        y = y + w[k:k + 1, :] * pad_ref[off:off + tm, :]
    return y


def _conv_taps(pad_ref, r, cols, taps):
    blocks = [pad_ref[r + j * HALO:r + (j + 1) * HALO, cols] for j in range(1 + STRIP // HALO)]
    sub = lax.broadcasted_iota(jnp.int32, blocks[0].shape, 0)
    out = []
    for k in range(taps - 1):
        s = taps - 1 - k
        rolled = [pltpu.roll(b, s, 0) for b in blocks]
        out.append(jnp.concatenate([jnp.where(sub < s, rolled[j], rolled[j + 1]) for j in range(STRIP // HALO)], axis=0))
    out.append(jnp.concatenate(blocks[1:], axis=0))
    return out


def _causal_conv_wgrad(pad_ref, dy, taps):
    tm = dy.shape[0]
    return jnp.concatenate([_colsum(dy * pad_ref[HALO - (taps - 1) + k:HALO - (taps - 1) + k + tm, :]) for k in range(taps)], axis=0)


def _causal_conv_transpose(pad_ref, dy, halo_next, last_in_seq, w, taps):
    tm = dy.shape[0]
    pad_ref[:tm, :] = dy
    pad_ref[tm:, :] = jnp.where(last_in_seq, 0.0, halo_next)
    dx = w[taps - 1:taps, :] * dy
    for k in range(taps - 1):
        off = (taps - 1) - k
        dx = dx + w[k:k + 1, :] * pad_ref[off:off + tm, :]
    return dx


def _lru_fwd(xl, gate, w, ts, seq):
    t, n = xl.shape
    tiles_per_seq = seq // ts

    def body(xl_ref, halo_ref, gate_ref, cw_ref, cb_ref, wa_ref, ba_ref, wx_ref, bx_ref, lam_ref, y_out, h_out, pad_ref, a_ref, b_ref, carry_ref):
        first_in_seq = pl.program_id(0) % tiles_per_seq == 0
        xc = _causal_conv(pad_ref, xl_ref[...], halo_ref[...], first_in_seq, cw_ref[...], LRU_CONV) + cb_ref[...]
        a, bx = _lru_point(_nn(xc, wa_ref[...]) + ba_ref[...], _nn(xc, wx_ref[...]) + bx_ref[...], xc, lam_ref[...])
        a_ref[...] = a
        b_ref[...] = bx

        @pl.when(first_in_seq)
        def _():
            carry_ref[...] = jnp.zeros_like(carry_ref)

        def step(r, h):
            h = a_ref[pl.ds(r, 1), :] * h + b_ref[pl.ds(r, 1), :]
            h_out[pl.ds(r, 1), :] = h
            return h

        carry_ref[...] = lax.fori_loop(0, ts, step, carry_ref[...], unroll=8)
        y_out[...] = h_out[...] * _gelu(gate_ref[...])

    return pl.pallas_call(
        body, name="lru_fwd", grid=(t // ts,),
        in_specs=[_row(ts, n), _prev_halo(ts, n), _row(ts, n), _const((LRU_CONV, n)), _const((1, n)), _const((n, n)), _const((1, n)),
                  _const((n, n)), _const((1, n)), _const((1, n))],
        out_specs=[_row(ts, n), _row(ts, n)], out_shape=[_sds((t, n)), _sds((t, n))],
        scratch_shapes=[pltpu.VMEM((HALO + ts, n), F32), pltpu.VMEM((ts, n), F32), pltpu.VMEM((ts, n), F32), pltpu.VMEM((1, n), F32)],
        compiler_params=_params(),
    )(xl, xl, gate, w['ab_conv_w'], w['ab_conv_b'], w['Wa'], w['ab_b_rg_a'], w['Wx'], w['ab_b_rg_x'], w['ab_lambda'])


def _lru_bwd(xl, gate, hs, dy, w, ts, seq):
    t, n = xl.shape
    tiles_per_seq = seq // ts
    n_tiles = t // ts

    def rev(i):
        return n_tiles - 1 - i

    row = pl.BlockSpec((ts, n), lambda i: (rev(i), 0))
    prev = pl.BlockSpec((HALO, n), lambda i: (jnp.maximum(rev(i) * (ts // HALO) - 1, 0), 0))
    acc = lambda shape: pl.BlockSpec(shape, lambda i: (0,) * len(shape))

    def body(xl_ref, xhalo_ref, gate_ref, h_ref, hhalo_ref, dy_ref, cw_ref, cb_ref, wa_ref, ba_ref, wx_ref, bx_ref, lam_ref,
             dxl_out, dgate_out, dcw_out, dcb_out, dwa_out, dba_out, dwx_out, dbx_out, dlam_out,
             pad_ref, padh_ref, padd_ref, a_ref, g_ref, carry_ref, dhalo_ref):
        step_id = pl.program_id(0)
        first = step_id == 0
        tile = rev(step_id)
        first_in_seq = tile % tiles_per_seq == 0
        last_in_seq = tile % tiles_per_seq == tiles_per_seq - 1
        cw = cw_ref[...]
        xc = _causal_conv(pad_ref, xl_ref[...], xhalo_ref[...], first_in_seq, cw, LRU_CONV) + cb_ref[...]
        pre_a = _nn(xc, wa_ref[...]) + ba_ref[...]
        pre_x = _nn(xc, wx_ref[...]) + bx_ref[...]
        (a, _), vjp_point = jax.vjp(_lru_point, pre_a, pre_x, xc, lam_ref[...])
        h = h_ref[...]
        _, vjp_out = jax.vjp(lambda h_, g_: h_ * _gelu(g_), h, gate_ref[...])
        dh, dgate = vjp_out(dy_ref[...])
        dgate_out[...] = dgate
        a_ref[...] = a
        g_ref[...] = dh

        @pl.when(last_in_seq)
        def _():
            carry_ref[...] = jnp.zeros_like(carry_ref)

        def step(j, c):
            r = ts - 1 - j
            g = g_ref[pl.ds(r, 1), :] + c
            g_ref[pl.ds(r, 1), :] = g
            return a_ref[pl.ds(r, 1), :] * g

        carry_ref[...] = lax.fori_loop(0, ts, step, carry_ref[...], unroll=8)
        g = g_ref[...]
        padh_ref[:HALO, :] = jnp.where(first_in_seq, 0.0, hhalo_ref[...])
        padh_ref[HALO:, :] = h
        dpre_a, dpre_x, dxc, dlam = vjp_point((g * padh_ref[HALO - 1:HALO - 1 + ts, :], g))
        dxc = dxc + _nt(dpre_a, wa_ref[...]) + _nt(dpre_x, wx_ref[...])
        _accumulate(dwa_out, _tn(xc, dpre_a), first)
        _accumulate(dwx_out, _tn(xc, dpre_x), first)
        _accumulate(dba_out, _colsum(dpre_a), first)
        _accumulate(dbx_out, _colsum(dpre_x), first)
        _accumulate(dlam_out, dlam, first)
        _accumulate(dcb_out, _colsum(dxc), first)
        _accumulate(dcw_out, _causal_conv_wgrad(pad_ref, dxc, LRU_CONV), first)
        dxl_out[...] = _causal_conv_transpose(padd_ref, dxc, dhalo_ref[...], last_in_seq, cw, LRU_CONV)
        dhalo_ref[...] = dxc[:HALO, :]

    return pl.pallas_call(
        body, name="lru_bwd", grid=(n_tiles,),
        in_specs=[row, prev, row, row, prev, row, _const((LRU_CONV, n)), _const((1, n)), _const((n, n)), _const((1, n)),
                  _const((n, n)), _const((1, n)), _const((1, n))],
        out_specs=[row, row, acc((LRU_CONV, n)), acc((1, n)), acc((n, n)), acc((1, n)), acc((n, n)), acc((1, n)), acc((1, n))],
        out_shape=[_sds((t, n)), _sds((t, n)), _sds((LRU_CONV, n)), _sds((1, n)), _sds((n, n)), _sds((1, n)), _sds((n, n)),
                   _sds((1, n)), _sds((1, n))],
        scratch_shapes=[pltpu.VMEM((HALO + ts, n), F32), pltpu.VMEM((HALO + ts, n), F32), pltpu.VMEM((ts + HALO, n), F32),
                        pltpu.VMEM((ts, n), F32), pltpu.VMEM((ts, n), F32), pltpu.VMEM((1, n), F32), pltpu.VMEM((HALO, n), F32)],
        compiler_params=_params(),
    )(xl, xl, gate, hs, hs, dy, w['ab_conv_w'], w['ab_conv_b'], w['Wa'], w['ab_b_rg_a'], w['Wx'], w['ab_b_rg_x'], w['ab_lambda'])


def _ab_out_fwd(x, o, y, w, tm):
    t, d = x.shape
    hp = o.shape[1]

    def body(x_ref, o_ref, y_ref, wa_ref, wb_ref, h_out):
        h_out[...] = x_ref[...] + _nn(o_ref[...], wa_ref[...]) + _nn(y_ref[...], wb_ref[...])

    return pl.pallas_call(body, name="ab_out_fwd", grid=(t // tm,),
                          in_specs=[_row(tm, d), _row(tm, hp), _row(tm, LRU_W), _const((hp, d)), _const((LRU_W, d))],
                          out_specs=_row(tm, d), out_shape=_sds((t, d)), compiler_params=_params())(x, o, y, w['Wo_a'], w['Wo_b'])


def _ab_out_bwd(o, y, dh, w, tm):
    t, d = dh.shape
    hp = o.shape[1]

    def body(o_ref, y_ref, dh_ref, wa_ref, wb_ref, do_out, dy_out, dwa_out, dwb_out):
        first = pl.program_id(0) == 0
        dh_t = dh_ref[...]
        do_out[...] = _nt(dh_t, wa_ref[...])
        dy_out[...] = _nt(dh_t, wb_ref[...])
        _accumulate(dwa_out, _tn(o_ref[...], dh_t), first)
        _accumulate(dwb_out, _tn(y_ref[...], dh_t), first)

    return pl.pallas_call(body, name="ab_out_bwd", grid=(t // tm,),
                          in_specs=[_row(tm, hp), _row(tm, LRU_W), _row(tm, d), _const((hp, d)), _const((LRU_W, d))],
                          out_specs=[_row(tm, hp), _row(tm, LRU_W), _const((hp, d)), _const((LRU_W, d))],
                          out_shape=[_sds((t, hp)), _sds((t, LRU_W)), _sds((hp, d)), _sds((LRU_W, d))],
                          compiler_params=_params())(o, y, dh, w['Wo_a'], w['Wo_b'])


FFN_CONV = 3


def _ffn_a_fwd(h, norm, wg, wu, tm):
    t, d = h.shape
    fb = D_FF // FF_BLOCKS

    def body(h_ref, gn_ref, wg_ref, wu_ref, g_out, u_out):
        hn = _rms(h_ref[...], gn_ref[...])
        g_out[...] = _nn(hn, wg_ref[...])
        u_out[...] = _nn(hn, wu_ref[...])

    wspec = pl.BlockSpec((d, fb), lambda f, i: (0, f))
    ospec = pl.BlockSpec((tm, fb), lambda f, i: (i, f))
    return pl.pallas_call(body, name="ffn_a_fwd", grid=(FF_BLOCKS, t // tm),
                          in_specs=[pl.BlockSpec((tm, d), lambda f, i: (i, 0)), pl.BlockSpec((1, d), lambda f, i: (0, 0)), wspec, wspec],
                          out_specs=[ospec, ospec], out_shape=[_sds((t, D_FF)), _sds((t, D_FF))],
                          compiler_params=_params(2))(h, norm, wg, wu)


def _ffn_b_fwd(g, u, h, cw, cb, wd, tm, seq):
    t, d = h.shape
    tiles_per_seq = seq // tm

    def body(g_ref, halo_ref, u_ref, h_ref, cw_ref, cb_ref, wd_ref, h_out, pad_ref):
        first_in_seq = pl.program_id(0) % tiles_per_seq == 0
        gc = _causal_conv(pad_ref, g_ref[...], halo_ref[...], first_in_seq, cw_ref[...], FFN_CONV) + cb_ref[...]
        h_out[...] = h_ref[...] + _nn(_gelu(gc) * u_ref[...], wd_ref[...])

    return pl.pallas_call(body, name="ffn_b_fwd", grid=(t // tm,),
                          in_specs=[_row(tm, D_FF), _prev_halo(tm, D_FF), _row(tm, D_FF), _row(tm, d), _const((FFN_CONV, D_FF)),
                                    _const((1, D_FF)), _const((D_FF, d))],
                          out_specs=_row(tm, d), out_shape=_sds((t, d)),
                          scratch_shapes=[pltpu.VMEM((HALO + tm, D_FF), F32)], compiler_params=_params())(g, g, u, h, cw, cb, wd)


def _ffn_b_bwd(g, u, dout, cw, cb, wd, tm, seq):
    t, d = dout.shape
    fb = D_FF // FF_BLOCKS
    tiles_per_seq = seq // tm

    def body(g_ref, halo_ref, u_ref, dout_ref, cw_ref, cb_ref, wd_ref, dgc_out, du_out, dwd_out, dcw_out, dcb_out,
             pad_ref, dact_ref, act_ref, acc_ref):
        i = pl.program_id(1)
        first = i == 0
        pad_ref[:HALO, :] = jnp.where(i % tiles_per_seq == 0, 0.0, halo_ref[...])
        pad_ref[HALO:, :] = g_ref[...]
        dout_b = _bf(dout_ref[...])
        dact_ref[...] = _nt(dout_b, wd_ref[...])
        cw = cw_ref[...]
        cb = cb_ref[...]
        fold = lambda a: a[:HALO] + a[HALO:]
        for c0 in range(0, fb, STRIP_LANES):
            cols = slice(c0, min(c0 + STRIP_LANES, fb))
            sums = [jnp.zeros((HALO, cols.stop - c0), F32) for _ in range(1 + FFN_CONV)]
            for r in range(0, tm, STRIP):
                rows = slice(r, r + STRIP)
                taps = _conv_taps(pad_ref, r, cols, FFN_CONV)
                gelu, dgelu = _gelu_and_grad(cb[:, cols] + cw[0:1, cols] * taps[0] + cw[1:2, cols] * taps[1] + cw[2:3, cols] * taps[2])
                u = u_ref[rows, cols]
                dact = dact_ref[rows, cols]
                act_ref[rows, cols] = _bf(gelu * u)
                du_out[rows, cols] = _bf(dact * gelu)
                dgc = dact * u * dgelu
                dgc_out[rows, cols] = dgc
                sums = [sums[0] + fold(dgc)] + [sums[1 + k] + fold(dgc * taps[k]) for k in range(FFN_CONV)]
            for k in range(1 + FFN_CONV):
                acc_ref[k, :, cols] = sums[k]
        _accumulate(dwd_out, _tn(act_ref[...], dout_b), first)
        _accumulate(dcb_out, _colsum(acc_ref[0]), first)
        _accumulate(dcw_out, jnp.concatenate([_colsum(acc_ref[1 + k]) for k in range(FFN_CONV)], axis=0), first)

    blk = pl.BlockSpec((tm, fb), lambda f, i: (i, f))
    halo = pl.BlockSpec((HALO, fb), lambda f, i: (jnp.maximum(i * (tm // HALO) - 1, 0), f))
    wd_blk = pl.BlockSpec((fb, d), lambda f, i: (f, 0), pipeline_mode=pl.Buffered(1))
    return pl.pallas_call(
        body, name="ffn_b_bwd", grid=(FF_BLOCKS, t // tm),
        in_specs=[blk, halo, blk, pl.BlockSpec((tm, d), lambda f, i: (i, 0)), pl.BlockSpec((FFN_CONV, fb), lambda f, i: (0, f)),
                  pl.BlockSpec((1, fb), lambda f, i: (0, f)), wd_blk],
        out_specs=[blk, blk, wd_blk, pl.BlockSpec((FFN_CONV, fb), lambda f, i: (0, f)),
                   pl.BlockSpec((1, fb), lambda f, i: (0, f))],
        out_shape=[_sds((t, D_FF)), _sds((t, D_FF), BF16), _sds((D_FF, d)), _sds((FFN_CONV, D_FF)), _sds((1, D_FF))],
        scratch_shapes=[pltpu.VMEM((HALO + tm, fb), F32), pltpu.VMEM((tm, fb), F32), pltpu.VMEM((tm, fb), BF16),
                        pltpu.VMEM((1 + FFN_CONV, HALO, fb), F32)],
        compiler_params=_params(2))(g, g, u, dout, cw, cb, wd)


def _ffn_a_dgrad(h, norm, dgc, du, dres, cw, wg, wu, tm, seq):
    t, d = h.shape
    tiles_per_seq = seq // tm
    n_tiles = t // tm

    def body(h_ref, gn_ref, dgc_ref, halo_ref, du_ref, dres_ref, cw_ref, wg_ref, wu_ref, dh_out, dg_out, dgn_out, pad_ref):
        i = pl.program_id(0)
        last_in_seq = i % tiles_per_seq == tiles_per_seq - 1
        dg = _bf(_causal_conv_transpose(pad_ref, dgc_ref[...], halo_ref[...], last_in_seq, cw_ref[...], FFN_CONV))
        dg_out[...] = dg
        _, vjp_norm = jax.vjp(_rms, h_ref[...], gn_ref[...])
        dh, dgn = vjp_norm(_nt(dg, wg_ref[...]) + _nt(du_ref[...], wu_ref[...]))
        dh_out[...] = dh + dres_ref[...]
        _accumulate(dgn_out, dgn, i == 0)

    return pl.pallas_call(
        body, name="ffn_a_dgrad", grid=(n_tiles,),
        in_specs=[_row(tm, d), _const((1, d)), _row(tm, D_FF), _next_halo(tm, D_FF, n_tiles), _row(tm, D_FF), _row(tm, d),
                  _const((FFN_CONV, D_FF)), _const((d, D_FF)), _const((d, D_FF))],
        out_specs=[_row(tm, d), _row(tm, D_FF), _const((1, d))], out_shape=[_sds((t, d)), _sds((t, D_FF), BF16), _sds((1, d))],
        scratch_shapes=[pltpu.VMEM((tm + HALO, D_FF), F32)], compiler_params=_params())(h, norm, dgc, dgc, du, dres, cw, wg, wu)


def _ffn_a_wgrad(h, norm, dg, du, tm):
    t, d = h.shape
    fb = D_FF // FF_BLOCKS

    def body(h_ref, gn_ref, dg_ref, du_ref, dwg_out, dwu_out):
        first = pl.program_id(1) == 0
        hn = _rms(h_ref[...], gn_ref[...])
        _accumulate(dwg_out, _tn(hn, dg_ref[...]), first)
        _accumulate(dwu_out, _tn(hn, du_ref[...]), first)

    blk = pl.BlockSpec((tm, fb), lambda f, i: (i, f))
    wspec = pl.BlockSpec((d, fb), lambda f, i: (0, f), pipeline_mode=pl.Buffered(1))
    return pl.pallas_call(body, name="ffn_a_wgrad", grid=(FF_BLOCKS, t // tm),
                          in_specs=[pl.BlockSpec((tm, d), lambda f, i: (i, 0)), pl.BlockSpec((1, d), lambda f, i: (0, 0)), blk, blk],
                          out_specs=[wspec, wspec], out_shape=[_sds((d, D_FF)), _sds((d, D_FF))],
                          compiler_params=_params(2))(h, norm, dg, du)


def _sgu_mix(vn, ws_ref, bst):
    tril = lax.broadcasted_iota(jnp.int32, (CHUNK, CHUNK), 0) >= lax.broadcasted_iota(jnp.int32, (CHUNK, CHUNK), 1)
    wms = [jnp.where(tril, ws_ref[g], 0.0) for g in range(SGU_GROUPS)]
    chunks = []
    for n in range(vn.shape[0] // CHUNK):
        vc = vn[n * CHUNK:(n + 1) * CHUNK, :]
        chunks.append(jnp.concatenate(
            [_nn(wms[g], vc[:, g * CHUNK:(g + 1) * CHUNK]) + bst[:, g:g + 1] for g in range(SGU_GROUPS)], axis=1))
    return jnp.concatenate(chunks, axis=0)


def _sgu_fwd(h, w, tm):
    t, d = h.shape

    def body(h_ref, cn_ref, win_ref, lg_ref, lb_ref, ws_ref, bst_ref, wout_ref, h_out):
        h_t = h_ref[...]
        z = _gelu(_nn(_rms(h_t, cn_ref[...]), win_ref[...]))
        vn = _layer_norm(z[:, d:], lg_ref[...], lb_ref[...])
        s = _sgu_mix(vn, ws_ref, bst_ref[...])
        h_out[...] = h_t + _nn(z[:, :d] * s, wout_ref[...])

    return pl.pallas_call(
        body, name="sgu_fwd", grid=(t // tm,),
        in_specs=[_row(tm, d), _const((1, d)), _const((d, 2 * d)), _const((1, d)), _const((1, d)), _const((SGU_GROUPS, CHUNK, CHUNK)),
                  _const((CHUNK, LANES)), _const((d, d))],
        out_specs=_row(tm, d), out_shape=_sds((t, d)), compiler_params=_params(),
    )(h, w['c_norm'], w['c_w_in'], w['c_ln_g'], w['c_ln_b'], w['c_w_s'], w['bsT'], w['c_w_out'])


def _sgu_bwd(h, dout, w, tm):
    t, d = h.shape

    def body(h_ref, dout_ref, cn_ref, win_ref, lg_ref, lb_ref, ws_ref, bst_ref, wout_ref,
             dh_out, dcn_out, dwin_out, dlg_out, dlb_out, dws_out, dbst_out, dwout_out):
        first = pl.program_id(0) == 0
        hn, vjp_norm = jax.vjp(_rms, h_ref[...], cn_ref[...])
        zpre = _nn(hn, win_ref[...])
        u, vjp_u = jax.vjp(_gelu, zpre[:, :d])
        vn, vjp_v = jax.vjp(lambda zp, lg, lb: _layer_norm(_gelu(zp), lg, lb), zpre[:, d:], lg_ref[...], lb_ref[...])
        s = _sgu_mix(vn, ws_ref, bst_ref[...])
        dout_t = dout_ref[...]
        dus = _nt(dout_t, wout_ref[...])
        _accumulate(dwout_out, _tn(u * s, dout_t), first)
        ds = dus * u
        tril = lax.broadcasted_iota(jnp.int32, (CHUNK, CHUNK), 0) >= lax.broadcasted_iota(jnp.int32, (CHUNK, CHUNK), 1)
        lane = lax.broadcasted_iota(jnp.int32, (CHUNK, LANES), 1)
        dws = [jnp.zeros((CHUNK, CHUNK), F32) for _ in range(SGU_GROUPS)]
        dbst = jnp.zeros((CHUNK, LANES), F32)
        dvn_chunks = []
        for n in range(tm // CHUNK):
            cols = []
            for g in range(SGU_GROUPS):
                ds_ng = ds[n * CHUNK:(n + 1) * CHUNK, g * CHUNK:(g + 1) * CHUNK]
                vc_ng = vn[n * CHUNK:(n + 1) * CHUNK, g * CHUNK:(g + 1) * CHUNK]
                cols.append(_tn(jnp.where(tril, ws_ref[g], 0.0), ds_ng))
                dws[g] = dws[g] + _nt(ds_ng, vc_ng)
                dbst = dbst + jnp.where(lane == g, jnp.sum(ds_ng, axis=1, keepdims=True), 0.0)
            dvn_chunks.append(jnp.concatenate(cols, axis=1))
        dvn = jnp.concatenate(dvn_chunks, axis=0)
        for g in range(SGU_GROUPS):
            val = jnp.where(tril, dws[g], 0.0)

            @pl.when(first)
            def _():
                dws_out[g] = val

            @pl.when(jnp.logical_not(first))
            def _():
                dws_out[g] += val
        _accumulate(dbst_out, dbst, first)
        (dzu,) = vjp_u(dus * s)
        dzv, dlg, dlb = vjp_v(dvn)
        _accumulate(dlg_out, dlg, first)
        _accumulate(dlb_out, dlb, first)
        dzpre = jnp.concatenate([dzu, dzv], axis=1)
        _accumulate(dwin_out, _tn(hn, dzpre), first)
        dh, dcn = vjp_norm(_nt(dzpre, win_ref[...]))
        _accumulate(dcn_out, dcn, first)
        dh_out[...] = dh + dout_t

    return pl.pallas_call(
        body, name="sgu_bwd", grid=(t // tm,),
        in_specs=[_row(tm, d), _row(tm, d), _const((1, d)), _const((d, 2 * d)), _const((1, d)), _const((1, d)),
                  _const((SGU_GROUPS, CHUNK, CHUNK)), _const((CHUNK, LANES)), _const((d, d))],
        out_specs=[_row(tm, d), _const((1, d)), _const((d, 2 * d)), _const((1, d)), _const((1, d)), _const((SGU_GROUPS, CHUNK, CHUNK)),
                   _const((CHUNK, LANES)), _const((d, d))],
        out_shape=[_sds((t, d)), _sds((1, d)), _sds((d, 2 * d)), _sds((1, d)), _sds((1, d)), _sds((SGU_GROUPS, CHUNK, CHUNK)),
                   _sds((CHUNK, LANES)), _sds((d, d))],
        compiler_params=_params(),
    )(h, dout, w['c_norm'], w['c_w_in'], w['c_ln_g'], w['c_ln_b'], w['c_w_s'], w['bsT'], w['c_w_out'])


def _final_loss(h, target, norm, tm):
    t, d = h.shape

    def body(h_ref, tgt_ref, gn_ref, dh_out, loss_out, dgn_out):
        first = pl.program_id(0) == 0
        tgt = tgt_ref[...]

        def loss_fn(h_, g_):
            err = _rms(h_, g_) - tgt
            return 0.5 * jnp.sum(jnp.mean(err * err, axis=-1, keepdims=True), axis=0, keepdims=True)

        loss, vjp_loss = jax.vjp(loss_fn, h_ref[...], gn_ref[...])
        dh, dgn = vjp_loss(jnp.ones((1, 1), F32))
        dh_out[...] = dh
        _accumulate(loss_out, loss, first)
        _accumulate(dgn_out, dgn, first)

    return pl.pallas_call(body, name="final_loss", grid=(t // tm,), in_specs=[_row(tm, d), _row(tm, d), _const((1, d))],
                          out_specs=[_row(tm, d), _const((1, 1)), _const((1, d))],
                          out_shape=[_sds((t, d)), _sds((1, 1)), _sds((1, d))], compiler_params=_params())(h, target, norm)


def _tile(t, seq, want):
    tm = min(want, seq)
    assert seq % tm == 0 and t % tm == 0 and tm % CHUNK == 0
    return tm


def _local_step(x, posb, target, w, seq, late_weights, on_grads):
    t, d = x.shape
    b = t // seq
    hp = HEADS * HEAD_PAD
    tm_big, tm_mid = _tile(t, seq, 512), _tile(t, seq, 256)
    tq = _tile(t, seq, 512)

    q, k, v, xl, gate = _ab_in_fwd(x, posb, w, tm_big)
    o = _attn_fwd(q.reshape(b, seq, hp), k.reshape(b, seq, hp), v.reshape(b, seq, hp), tq).reshape(t, hp)
    y, hs = _lru_fwd(xl, gate, w, tm_big, seq)
    w = {**w, **late_weights('out0', y)}
    h1 = _ab_out_fwd(x, o, y, w, tm_big)
    hcur = h1
    saved = []
    for l in range(2):
        if l == 1:
            w = {**w, **late_weights('mix1', hcur)}
            saved_h2 = hcur
            hcur = _sgu_fwd(hcur, w, tm_mid)
        wl = late_weights('ffn%d' % l, hcur)
        g, u = _ffn_a_fwd(hcur, w['ffn_norm'][l], wl['Wg'], wl['Wu'], tm_big)
        hnext = _ffn_b_fwd(g, u, hcur, w['ffn_conv_w'][l], w['ffn_conv_b'][l], wl['Wd'], tm_mid, seq)
        saved.append((hcur, g, u, wl))
        hcur = hnext
    dh, loss, d_final = _final_loss(hcur, target, w['final_norm'], tm_big)

    ffn = {}
    conv_b = list(w['ffn_conv_b'])
    for l in (1, 0):
        hin, g, u, wl = saved[l]
        dgc, du, d_wd, d_cw, d_cb = _ffn_b_bwd(g, u, dh, w['ffn_conv_w'][l], conv_b[l], wl['Wd'], tm_big, seq)
        dh, dg, d_norm = _ffn_a_dgrad(hin, w['ffn_norm'][l], dgc, du, dh, w['ffn_conv_w'][l], wl['Wg'], wl['Wu'], tm_mid, seq)
        d_wg, d_wu = _ffn_a_wgrad(hin, w['ffn_norm'][l], dg, du, _tile(t, seq, 1024))
        ffn[l] = dict(ffn_norm=d_norm, ffn_conv_w=d_cw, ffn_conv_b=d_cb, Wg=d_wg, Wu=d_wu, Wd=d_wd)
        if l == 1:
            dh, d_cn, d_cwin, d_lg, d_lb, d_ws, d_bst, d_cwout = _sgu_bwd(saved_h2, dh, w, tm_mid)
            zero = on_grads('late1', dict(final_norm=d_final, c_norm=d_cn, c_ln_g=d_lg, c_ln_b=d_lb, c_w_s=d_ws, bsT=d_bst, c_w_in=d_cwin,
                                          c_w_out=d_cwout, Wg=[d_wg], Wu=[d_wu], Wd=[d_wd]))
            conv_b[0] = conv_b[0] + zero
    late0 = {name: [ffn[0][name], ffn[1][name]] for name in ('ffn_norm', 'ffn_conv_w', 'ffn_conv_b')}
    zero = on_grads('late0', dict(late0, Wg=[ffn[0]['Wg']], Wu=[ffn[0]['Wu']], Wd=[ffn[0]['Wd']]))
    w = {**w, 'Wo_b': w['Wo_b'] + zero.astype(w['Wo_b'].dtype)}
    do, dy, d_woa, d_wob = _ab_out_bwd(o, y, dh, w, tm_big)
    dxl, dgate, d_cw, d_cb, d_wa, d_ba, d_wx, d_bx, d_lam = _lru_bwd(xl, gate, hs, dy, w, tm_big, seq)
    zero = on_grads('mid', dict(Wo_a=d_woa, Wo_b=d_wob, ab_conv_w=d_cw, ab_conv_b=d_cb, Wa=d_wa, ab_b_rg_a=d_ba, Wx=d_wx,
                                ab_b_rg_x=d_bx, ab_lambda=d_lam))
    w = {**w, 'ab_norm': w['ab_norm'] + zero}
    dq, dk, dv = _attn_bwd(q.reshape(b, seq, hp), k.reshape(b, seq, hp), v.reshape(b, seq, hp), do.reshape(b, seq, hp), tq)
    dx, d_gn, d_win, d_qn, d_wq, d_kvn, d_wk, d_wv = _ab_in_bwd(
        x, posb, w, dq.reshape(t, hp), dk.reshape(t, hp), dv.reshape(t, hp), dxl, dgate, dh, tm_mid)
    return loss, dx, dict(ab_norm=d_gn, W_in=d_win, ab_q_norm=d_qn, Wq=d_wq, ab_kv_norm=d_kvn, Wk=d_wk, Wv=d_wv)


def _block_diag(wg):
    g, n, _ = wg.shape
    return jnp.einsum('gij,gh->gihj', wg, jnp.eye(g, dtype=wg.dtype)).reshape(g * n, g * n)


def _prepare_out(w_out):
    d = w_out.shape[2]
    mla = HEADS * QK_NOPE
    return {'Wo_a': jnp.pad(w_out[0, :mla].reshape(HEADS, QK_NOPE, d), ((0, 0), (0, HEAD_PAD - QK_NOPE), (0, 0))).reshape(HEADS * HEAD_PAD, d),
            'Wo_b': w_out[0, mla:]}


def _prepare(full):
    d = full['ab_w_in'].shape[1]
    w_in = full['ab_w_in'][0]
    zeros = lambda n: jnp.zeros((d, n), w_in.dtype)
    wq = full['ab_w_q_b'][0].reshape(Q_LORA, HEADS, QK_NOPE + QK_ROPE)
    wkv = full['ab_w_kv_b'][0].reshape(KV_LORA, HEADS, 2 * QK_NOPE)
    pad_head = lambda a: jnp.pad(a, ((0, 0), (0, 0), (0, HEAD_PAD - a.shape[2]))).reshape(a.shape[0], HEADS * HEAD_PAD)
    w = {
        'W_in': jnp.concatenate([w_in[:, :Z_KPE], zeros(QK_NOPE), w_in[:, Z_KPE:Z_KPE + QK_ROPE],
                                 zeros(HEAD_PAD - QK_NOPE - QK_ROPE), w_in[:, Z_KPE + QK_ROPE:]], axis=1),
        'Wq': pad_head(wq), 'Wk': pad_head(wkv[:, :, :QK_NOPE]), 'Wv': pad_head(wkv[:, :, QK_NOPE:]),
        'Wa': _bf(_block_diag(full['ab_w_rg_a'][0])), 'Wx': _bf(_block_diag(full['ab_w_rg_x'][0])),
        'c_w_s': full['c_w_s'][0],
        'bsT': jnp.pad(full['c_b_s'][0].T, ((0, 0), (0, LANES - SGU_GROUPS))),
        'ffn_norm': [full['ffn_norm'][l:l + 1] for l in range(2)], 'ffn_conv_w': [full['ffn_conv_w'][l] for l in range(2)],
        'ffn_conv_b': [full['ffn_conv_b'][l:l + 1] for l in range(2)],
        'ab_conv_w': full['ab_conv_w'][0], 'final_norm': full['final_norm'][None, :],
    }
    for name in ('ab_norm', 'ab_q_norm', 'ab_kv_norm', 'ab_conv_b', 'ab_b_rg_a', 'ab_b_rg_x', 'ab_lambda', 'c_norm', 'c_ln_g', 'c_ln_b'):
        w[name] = full[name]
    return w


def _unprepare(g):
    unpad_head = lambda a, n: a.reshape(a.shape[0], HEADS, HEAD_PAD)[:, :, :n]
    diag = lambda a: jnp.einsum('gigj->gij', a.reshape(HEADS, LRU_W // HEADS, HEADS, LRU_W // HEADS))
    rules = {
        'ab_w_in': (('W_in',), lambda a: jnp.concatenate([a[:, :Z_KPE], a[:, Z_KPE + QK_NOPE:Z_KPE + QK_NOPE + QK_ROPE], a[:, Z_LRU:]], axis=1)[None]),
        'ab_w_q_b': (('Wq',), lambda a: unpad_head(a, QK_NOPE + QK_ROPE).reshape(1, Q_LORA, -1)),
        'ab_w_kv_b': (('Wk', 'Wv'), lambda a, b: jnp.concatenate([unpad_head(a, QK_NOPE), unpad_head(b, QK_NOPE)], axis=2).reshape(1, KV_LORA, -1)),
        'ab_w_out': (('Wo_a', 'Wo_b'), lambda a, b: jnp.concatenate(
            [a.reshape(HEADS, HEAD_PAD, -1)[:, :QK_NOPE].reshape(HEADS * QK_NOPE, -1), b], axis=0)[None]),
        'ab_w_rg_a': (('Wa',), lambda a: diag(a)[None]), 'ab_w_rg_x': (('Wx',), lambda a: diag(a)[None]),
        'c_w_in': (('c_w_in',), lambda a: a[None]), 'c_w_out': (('c_w_out',), lambda a: a[None]), 'c_w_s': (('c_w_s',), lambda a: a[None]),
        'c_b_s': (('bsT',), lambda a: a[:, :SGU_GROUPS].T[None]),
        'ffn_w_gate': (('Wg',), jnp.stack), 'ffn_w_up': (('Wu',), jnp.stack), 'ffn_w_down': (('Wd',), jnp.stack),
        'ffn_norm': (('ffn_norm',), lambda a: jnp.concatenate(a, axis=0)), 'ffn_conv_w': (('ffn_conv_w',), jnp.stack),
        'ffn_conv_b': (('ffn_conv_b',), lambda a: jnp.concatenate(a, axis=0)),
        'ab_conv_w': (('ab_conv_w',), lambda a: a[None]), 'final_norm': (('final_norm',), lambda a: a[0]),
    }
    for name in ('ab_norm', 'ab_q_norm', 'ab_kv_norm', 'ab_conv_b', 'ab_b_rg_a', 'ab_b_rg_x', 'ab_lambda', 'c_norm', 'c_ln_g', 'c_ln_b'):
        rules[name] = ((name,), lambda a: a)
    return {name: fn(*[g[k] for k in keys]) for name, (keys, fn) in rules.items() if all(k in g for k in keys)}


SLAB_ROWS = 16


def _round_up(n, m):
    return -(-n // m) * m


def _to_chunks(full, axis):
    s = full.shape
    return jnp.moveaxis(full.reshape(s[:axis] + (N_DEV, s[axis] // N_DEV) + s[axis + 1:]), axis, 0)


def _from_chunks(chunks, axis):
    local = chunks.shape[1:]
    return jnp.moveaxis(chunks, 0, axis).reshape(local[:axis] + (N_DEV * local[axis],) + local[axis + 1:])


def _merge_chunks(me, own, landed, axis, name):
    _, r, n = own.shape
    if axis == 1:
        def body(me_ref, own_ref, l_ref, o_ref):
            o_ref[...] = jnp.where(me_ref[0] == pl.program_id(0), own_ref[...], l_ref[0])

        grid, out_shape = (N_DEV,), (1, N_DEV * r, n)
        specs = [pl.BlockSpec((1, r, n), lambda dev, me_ref: (0, 0, 0)), pl.BlockSpec((1, 1, r, n), lambda dev, me_ref: (dev, 0, 0, 0))]
        out_spec = pl.BlockSpec((1, r, n), lambda dev, me_ref: (0, dev, 0))
    else:
        tr = r // 4

        def body(me_ref, own_ref, l_ref, o_ref):
            o_ref[0] = jnp.concatenate([jnp.where(me_ref[0] == dev, own_ref[0], l_ref[dev, 0]) for dev in range(N_DEV)], axis=1)

        grid, out_shape = (r // tr,), (1, r, N_DEV * n)
        specs = [pl.BlockSpec((1, tr, n), lambda i, me_ref: (0, i, 0)), pl.BlockSpec((N_DEV, 1, tr, n), lambda i, me_ref: (0, 0, i, 0))]
        out_spec = pl.BlockSpec((1, tr, N_DEV * n), lambda i, me_ref: (0, i, 0))
    return pl.pallas_call(
        body, name="merge_" + name,
        grid_spec=pltpu.PrefetchScalarGridSpec(num_scalar_prefetch=1, grid=grid, in_specs=specs, out_specs=out_spec),
        out_shape=jax.ShapeDtypeStruct(out_shape, own.dtype), compiler_params=_params())(me, own, landed)


def _split_chunks(whole, axis, name):
    _, rows, cols = whole.shape
    if axis == 1:
        r = rows // N_DEV

        def body(x_ref, o_ref):
            o_ref[0] = _bf(x_ref[...])

        grid, out_shape = (N_DEV,), (N_DEV, 1, r, cols)
        spec, out_spec = pl.BlockSpec((1, r, cols), lambda dev: (0, dev, 0)), pl.BlockSpec((1, 1, r, cols), lambda dev: (dev, 0, 0, 0))
    else:
        n, tr = cols // N_DEV, rows // 4

        def body(x_ref, o_ref):
            x = x_ref[0]
            for dev in range(N_DEV):
                o_ref[dev, 0] = _bf(x[:, dev * n:(dev + 1) * n])

        grid, out_shape = (rows // tr,), (N_DEV, 1, rows, n)
        spec, out_spec = pl.BlockSpec((1, tr, cols), lambda i: (0, i, 0)), pl.BlockSpec((N_DEV, 1, tr, n), lambda i: (0, 0, i, 0))
    return pl.pallas_call(body, name="split_" + name, grid=grid, in_specs=[spec], out_specs=out_spec,
                          out_shape=jax.ShapeDtypeStruct(out_shape, BF16), compiler_params=_params())(whole)


def _slab_rows(n):
    return _round_up(-(-n // LANES), SLAB_ROWS)


def _to_slab(a, lead):
    a = a.reshape(lead + (-1,))
    rows = _slab_rows(a.shape[-1])
    a = jnp.pad(a, [(0, 0)] * len(lead) + [(0, rows * LANES - a.shape[-1])])
    return a.reshape(lead + (rows, LANES))


def _pack_slabs(parts, lead):
    return jnp.concatenate([_to_slab(p, lead) for p in parts], axis=len(lead))


def _unpack_slabs(packed, shapes):
    lead = packed.shape[:-2]
    out, row = [], 0
    for shape in shapes:
        size = math.prod(shape)
        rows = _slab_rows(size)
        piece = lax.slice_in_dim(packed, row, row + rows, axis=len(lead))
        out.append(piece.reshape(lead + (rows * LANES,))[..., :size].reshape(lead + tuple(shape)))
        row += rows
    return out


HBM = pl.BlockSpec(memory_space=pl.ANY)


def _other_chips(x, y):
    return [(1 - x, y), (x, 1 - y), (1 - x, 1 - y)]


def _all_gather(blocks):
    n = len(blocks)

    def body(*refs):
        x_refs, out_refs, token = refs[:n], refs[n:2 * n], refs[2 * n]
        send_sems, recv_sems, local_sems = refs[2 * n + 1:]
        token[...] = jnp.zeros_like(token)
        x, y, c = lax.axis_index("x"), lax.axis_index("y"), lax.axis_index("c")
        me, sibling = (x, y, c), (x, y, 1 - c)
        chips = _other_chips(x, y)

        def slab(a, px, py, pc):
            return out_refs[a].at[4 * px + 2 * py + pc]

        def copy(a, k, blk, to, src=None):
            return pltpu.make_async_remote_copy(src_ref=slab(a, *blk) if src is None else src, dst_ref=slab(a, *blk),
                                                send_sem=send_sems.at[7 * a + k], recv_sem=recv_sems.at[7 * a + k],
                                                device_id=to, device_id_type=MESH)

        mine = [pltpu.make_async_copy(x_refs[a], slab(a, *me), local_sems.at[a]) for a in range(n)]
        started = []
        for a in range(n):
            mine[a].start()
            started.append(copy(a, 0, me, sibling, src=x_refs[a]))
            started += [copy(a, 1 + j, me, (*chip, c), src=x_refs[a]) for j, chip in enumerate(chips)]
        for cp in started:
            cp.start()
        for j, chip in enumerate(chips):
            for a in range(n):
                copy(a, 1 + j, (*chip, c), me).wait_recv()
                passed = copy(a, 4 + j, (*chip, c), sibling)
                passed.start()
                started.append(passed)
        for a in range(n):
            copy(a, 0, sibling, me).wait_recv()
        for j, chip in enumerate(chips):
            for a in range(n):
                copy(a, 4 + j, (*chip, 1 - c), me).wait_recv()
        for cp in started:
            cp.wait_send()
        for a in range(n):
            mine[a].wait()

    out = pl.pallas_call(
        body, name="all_gather_weights",
        out_shape=[jax.ShapeDtypeStruct((N_DEV,) + b.shape, b.dtype) for b in blocks] + [jax.ShapeDtypeStruct((8, LANES), F32)],
        in_specs=[HBM] * n, out_specs=[HBM] * n + [pl.BlockSpec(memory_space=pltpu.VMEM)],
        scratch_shapes=[pltpu.SemaphoreType.DMA((7 * n,)), pltpu.SemaphoreType.DMA((7 * n,)), pltpu.SemaphoreType.DMA((n,))],
    )(*blocks)
    return list(out[:n]), out[n][0, 0]


FLIPS = [(0, 0, 1), (1, 0, 0), (1, 0, 1), (0, 1, 0), (0, 1, 1), (1, 1, 0), (1, 1, 1)]


def _peers(x, y, c):
    flip = lambda v, f: 1 - v if f else v
    return [(flip(x, fx), flip(y, fy), flip(c, fc)) for fx, fy, fc in FLIPS]


def _direct_copies(src_refs, land_refs, send_sems, recv_sems, scatter):
    x, y, c = lax.axis_index("x"), lax.axis_index("y"), lax.axis_index("c")
    me = 4 * x + 2 * y + c
    starts, waits = [], []
    for a in range(len(src_refs)):
        for k, (px, py, pc) in enumerate(_peers(x, y, c)):
            peer = 4 * px + 2 * py + pc
            sems = dict(send_sem=send_sems.at[7 * a + k], recv_sem=recv_sems.at[7 * a + k], device_id=(px, py, pc), device_id_type=MESH)
            src = src_refs[a].at[peer] if scatter else src_refs[a]
            starts.append(pltpu.make_async_remote_copy(src_ref=src, dst_ref=land_refs[a].at[me], **sems))
            waits.append(pltpu.make_async_remote_copy(src_ref=src, dst_ref=land_refs[a].at[peer], **sems))
    return starts, waits


def _landing(src, scatter):
    block = src.shape[1:] if scatter else src.shape
    return jax.ShapeDtypeStruct((N_DEV,) + block, src.dtype)


HBM_SPACE = pl.BlockSpec(memory_space=pltpu.HBM)
SEMAPHORES = pl.BlockSpec(memory_space=pltpu.SEMAPHORE)
SPLIT_EFFECT = pltpu.SideEffectType.DATAFLOW_SIDE_EFFECTING


def _start_exchange(name, srcs, scatter):
    n = len(srcs)
    lands = [lax.empty(s.shape, s.dtype) for s in (_landing(s, scatter) for s in srcs)]

    def body(*refs):
        starts, _ = _direct_copies(refs[:n], refs[n:2 * n], refs[2 * n], refs[2 * n + 1], scatter)
        for cp in starts:
            cp.start()
        refs[-1][...] = jnp.zeros_like(refs[-1])

    held = [pltpu.with_memory_space_constraint(a, pltpu.HBM) for a in list(srcs) + lands]
    out = pl.pallas_call(
        body, name=name + "_start",
        out_shape=(pltpu.SemaphoreType.DMA((7 * n,)), pltpu.SemaphoreType.DMA((7 * n,)), *[pltpu.HBM(a.shape, a.dtype) for a in held],
                   jax.ShapeDtypeStruct((8, LANES), F32)),
        in_specs=[HBM_SPACE] * (2 * n), out_specs=(SEMAPHORES, SEMAPHORES, *[HBM_SPACE] * (2 * n), pl.BlockSpec(memory_space=pltpu.VMEM)),
        input_output_aliases={i: 2 + i for i in range(2 * n)},
        compiler_params=pltpu.CompilerParams(has_side_effects=SPLIT_EFFECT),
    )(*held)
    return out[0], out[1], list(out[2:2 + n]), list(out[2 + n:2 + 2 * n]), out[-1][0, 0], out[-1]


def _wait_exchange(name, started, after, scatter):
    send_sems, recv_sems, srcs, lands = started[:4]
    n = len(srcs)

    def body(*refs):
        _, waits = _direct_copies(refs[:n], refs[n:2 * n], refs[2 * n], refs[2 * n + 1], scatter)
        for cp in waits:
            cp.wait_send()
        for cp in waits:
            cp.wait_recv()

    out = pl.pallas_call(
        body, name=name + "_wait", out_shape=tuple(pltpu.HBM(a.shape, a.dtype) for a in srcs + lands),
        in_specs=[HBM_SPACE] * (2 * n) + [SEMAPHORES, SEMAPHORES, HBM], out_specs=tuple([HBM_SPACE] * (2 * n)),
        input_output_aliases={i: i for i in range(2 * n)},
        compiler_params=pltpu.CompilerParams(has_side_effects=SPLIT_EFFECT),
    )(*srcs, *lands, send_sems, recv_sems, after)
    return list(out[:n]), list(out[n:])


def _row_tile(rows):
    return rows // 2 if (rows // 2) % SLAB_ROWS == 0 else rows


def _sum_and_adamw(me, landed, own, wts, m, v, name, layer=None, into=None):
    layers, r, n = wts.shape
    first = 0 if layer is None else layer
    count = layers if layer is None else 1
    tr = _row_tile(r)
    blk = pl.BlockSpec((1, tr, n), lambda li, ri, me_ref: (first + li, ri, 0))
    c1 = 1.0 / (1.0 - ADAM_B1 ** ADAM_STEP)
    c2 = 1.0 / (1.0 - ADAM_B2 ** ADAM_STEP)
    held = [] if into is None else list(into)

    def body(me_ref, l_ref, own_ref, w_ref, m_ref, v_ref, *rest):
        g_out, d_out, m_out, v_out = rest[len(held):]
        mine = own_ref[0].astype(F32)
        g = jnp.where(me_ref[0] == 0, mine, l_ref[0].astype(F32))
        for dev in range(1, N_DEV):
            g = g + jnp.where(me_ref[0] == dev, mine, l_ref[dev].astype(F32))
        m_new = ADAM_B1 * m_ref[...] + (1.0 - ADAM_B1) * g
        v_new = ADAM_B2 * v_ref[...] + (1.0 - ADAM_B2) * (g * g)
        g_out[...] = g
        m_out[...] = m_new
        v_out[...] = v_new
        d_out[...] = -ADAM_LR * ((m_new * c1) / (jnp.sqrt(v_new * c2) + ADAM_EPS) + ADAM_WD * w_ref[...])

    return pl.pallas_call(
        body, name="adamw_" + name,
        grid_spec=pltpu.PrefetchScalarGridSpec(
            num_scalar_prefetch=1, grid=(count, r // tr),
            in_specs=[pl.BlockSpec((N_DEV, 1, tr, n), lambda li, ri, me_ref: (0, li, ri, 0)),
                      pl.BlockSpec((1, 1, tr, n), lambda li, ri, me_ref: (me_ref[0], li, ri, 0)), blk, blk, blk] + [HBM] * len(held),
            out_specs=[blk] * 4),
        out_shape=[_sds((layers, r, n))] * 4, input_output_aliases={6 + i: i for i in range(len(held))},
        compiler_params=_params(2))(me, landed, own, wts, m, v, *held)


EARLY = ['ab_w_in']
LATE_STAGES = {
    'out0': [('ab_w_out', None, 'ab_w_out')],
    'ffn0': [('ffn_w_gate', 0, 'Wg'), ('ffn_w_up', 0, 'Wu'), ('ffn_w_down', 0, 'Wd')],
    'mix1': [('c_w_in', None, 'c_w_in'), ('c_w_out', None, 'c_w_out')],
    'ffn1': [('ffn_w_gate', 1, 'Wg'), ('ffn_w_up', 1, 'Wu'), ('ffn_w_down', 1, 'Wd')],
}
GRAD_STAGES = {
    'late1': ([('c_w_in', None), ('c_w_out', None), ('ffn_w_gate', 1), ('ffn_w_up', 1), ('ffn_w_down', 1)],
              ['c_norm', 'c_ln_g', 'c_ln_b', 'c_w_s', 'c_b_s', 'final_norm']),
    'late0': ([('ffn_w_gate', 0), ('ffn_w_up', 0), ('ffn_w_down', 0)], ['ffn_norm', 'ffn_conv_w', 'ffn_conv_b']),
    'mid': ([('ab_w_out', None)], ['ab_conv_w', 'ab_conv_b', 'ab_w_rg_a', 'ab_b_rg_a', 'ab_w_rg_x', 'ab_b_rg_x', 'ab_lambda']),
    'last': ([('ab_w_in', None)], ['ab_norm', 'ab_q_norm', 'ab_w_q_b', 'ab_kv_norm', 'ab_w_kv_b']),
}


def _gather_early(local):
    small = [_bf(local[n]) if n in MATRICES else lax.bitcast_convert_type(local[n], BF16) for n in SMALL_SHARDED]
    gathered, zero = _all_gather([_bf(local[n]) for n in EARLY] + [_pack_slabs(small, ())])
    full = {n: local[n] for n in REPLICATED}
    for n, g in zip(EARLY, gathered):
        full[n] = _from_chunks(g, SHARD_AXIS[n])
    for n, p in zip(SMALL_SHARDED, _unpack_slabs(gathered[-1], [s.shape for s in small])):
        full[n] = _from_chunks(p if n in MATRICES else lax.bitcast_convert_type(p, F32), SHARD_AXIS[n])
    return full, zero


def kernel(x, positions, ab_norm, ab_w_in, ab_q_norm, ab_w_q_b, ab_kv_norm, ab_w_kv_b, ab_conv_w, ab_conv_b, ab_w_rg_a, ab_b_rg_a, ab_w_rg_x, ab_b_rg_x, ab_lambda, ab_w_out, c_norm, c_w_in, c_ln_g, c_ln_b, c_w_s, c_b_s, c_w_out, ffn_norm, ffn_w_gate, ffn_w_up, ffn_conv_w, ffn_conv_b, ffn_w_down, final_norm, loss_target, m_ab_norm, m_ab_w_in, m_ab_q_norm, m_ab_w_q_b, m_ab_kv_norm, m_ab_w_kv_b, m_ab_conv_w, m_ab_conv_b, m_ab_w_rg_a, m_ab_b_rg_a, m_ab_w_rg_x, m_ab_b_rg_x, m_ab_lambda, m_ab_w_out, m_c_norm, m_c_w_in, m_c_ln_g, m_c_ln_b, m_c_w_s, m_c_b_s, m_c_w_out, m_ffn_norm, m_ffn_w_gate, m_ffn_w_up, m_ffn_conv_w, m_ffn_conv_b, m_ffn_w_down, m_final_norm, v_ab_norm, v_ab_w_in, v_ab_q_norm, v_ab_w_q_b, v_ab_kv_norm, v_ab_w_kv_b, v_ab_conv_w, v_ab_conv_b, v_ab_w_rg_a, v_ab_b_rg_a, v_ab_w_rg_x, v_ab_b_rg_x, v_ab_lambda, v_ab_w_out, v_c_norm, v_c_w_in, v_c_ln_g, v_c_ln_b, v_c_w_s, v_c_b_s, v_c_w_out, v_ffn_norm, v_ffn_w_gate, v_ffn_w_up, v_ffn_conv_w, v_ffn_conv_b, v_ffn_w_down, v_final_norm):
    given = dict(locals())
    local = {n: given[n] for n in WEIGHTS}
    b, seq, d = x.shape
    t = b * seq

    me = (4 * lax.axis_index("x") + 2 * lax.axis_index("y") + lax.axis_index("c")).astype(jnp.int32)
    me1 = me.reshape(1)

    full, zero = _gather_early(local)
    gathers = {}
    for stage, members in LATE_STAGES.items():
        srcs = [_bf((local[n] if layer is None else local[n][layer:layer + 1]) + zero) for n, layer, _ in members]
        gathers[stage] = _start_exchange('gather_' + stage, srcs, scatter=False)
        zero = gathers[stage][4]
    w = _prepare(full)
    w['ab_norm'] = w['ab_norm'] + zero

    def late_weights(stage, after):
        srcs, lands = _wait_exchange('gather_' + stage, gathers[stage], after, scatter=False)
        whole = [_merge_chunks(me1, s, l, SHARD_AXIS[n], n + ('' if layer is None else str(layer)))
                 for (n, layer, _), s, l in zip(LATE_STAGES[stage], srcs, lands)]
        if stage == 'out0':
            return _prepare_out(whole[0])
        return {key: a[0] for (_, _, key), a in zip(LATE_STAGES[stage], whole)}

    scatters = {}

    def start_scatter(stage, g):
        whole = _unprepare(g)
        big, small = GRAD_STAGES[stage]
        slab = [_to_chunks(whole[n], SHARD_AXIS[n]) if n in SHARD_AXIS else jnp.broadcast_to(whole[n][None], (N_DEV,) + whole[n].shape)
                for n in small]
        own = [_split_chunks(whole[n], SHARD_AXIS[n], n + ('' if layer is None else str(layer))) for n, layer in big]
        own.append(_bf(_pack_slabs(slab, (N_DEV,)))[:, None])
        scatters[stage] = _start_exchange('scatter_' + stage, own, scatter=True)
        return scatters[stage][4]

    posb = jnp.broadcast_to(positions.astype(F32).reshape(t, 1), (t, LANES))
    loss, dx, grads = _local_step(x.reshape(t, d), posb, loss_target.reshape(t, d), w, seq, late_weights, start_scatter)
    start_scatter('last', grads)
    after = scatters['last'][5]

    me1 = me.reshape(1)
    updated = {}
    for stage, (big, small) in GRAD_STAGES.items():
        owns, landed = _wait_exchange('scatter_' + stage, scatters[stage], after, scatter=True)
        for (n, layer), own, land in zip(big, owns, landed):
            updated[n] = _sum_and_adamw(me1, land, own, given[n], given['m_' + n], given['v_' + n], n + ('' if layer is None else str(layer)),
                                        layer, updated.get(n))
        pack_small = lambda prefix: _pack_slabs([given[prefix + n] for n in small], ())[None]
        packed = _sum_and_adamw(me1, landed[-1], owns[-1], pack_small(''), pack_small('m_'), pack_small('v_'), 'small_' + stage)
        unpacked = [_unpack_slabs(p[0], [local[n].shape for n in small]) for p in packed]
        for i, n in enumerate(small):
            updated[n] = [u[i] for u in unpacked]
        after = sum([updated[n][1][:1, :1, :1] for n, _ in big], packed[1][:1, :1, :1])
    total = lax.psum(loss[0, 0], ("x", "y", "c"))
    return (total, dx.reshape(b, seq, d), *[updated[n][kind] for kind in range(4) for n in WEIGHTS])
```

```python
import math

import jax
import jax.numpy as jnp
from jax import lax
from jax.experimental import pallas as pl
from jax.experimental.pallas import tpu as pltpu

F32 = jnp.float32
BF16 = jnp.bfloat16
MESH = pl.DeviceIdType.MESH

N_DEV = 8
LANES = 128
HALO = 8
VMEM_LIMIT = 56 << 20

NORM_EPS = 1e-6
HEADS = 8
HEAD_PAD = 128
QK_NOPE = 64
QK_ROPE = 32
ROPE_HALF = 16
ROPE_BASE = 10000.0
ATTN_SCALE = (QK_NOPE + QK_ROPE) ** -0.5
LRU_C = 8.0
LRU_W = 512
CHUNK = 128
SGU_GROUPS = 8
D_FF = 2816
FF_BLOCKS = 2

ADAM_LR, ADAM_B1, ADAM_B2, ADAM_EPS, ADAM_WD, ADAM_STEP = 0.001, 0.9, 0.999, 1e-08, 0.01, 10

WEIGHTS = ['ab_norm', 'ab_w_in', 'ab_q_norm', 'ab_w_q_b', 'ab_kv_norm', 'ab_w_kv_b', 'ab_conv_w', 'ab_conv_b',
           'ab_w_rg_a', 'ab_b_rg_a', 'ab_w_rg_x', 'ab_b_rg_x', 'ab_lambda', 'ab_w_out', 'c_norm', 'c_w_in', 'c_ln_g',
           'c_ln_b', 'c_w_s', 'c_b_s', 'c_w_out', 'ffn_norm', 'ffn_w_gate', 'ffn_w_up', 'ffn_conv_w', 'ffn_conv_b',
           'ffn_w_down', 'final_norm']
SHARD_AXIS = {'ab_w_in': 2, 'ab_w_q_b': 2, 'ab_w_kv_b': 2, 'ab_conv_w': 2, 'ab_w_out': 1, 'c_norm': 1, 'c_w_in': 2,
              'c_ln_g': 1, 'c_ln_b': 1, 'c_w_out': 1, 'ffn_w_gate': 2, 'ffn_w_up': 2, 'ffn_conv_w': 2, 'ffn_w_down': 1}
MATRICES = ['ab_w_in', 'ab_w_q_b', 'ab_w_kv_b', 'ab_w_out', 'c_w_in', 'c_w_out', 'ffn_w_gate', 'ffn_w_up', 'ffn_w_down']
BIG = ['ab_w_in', 'c_w_in', 'ffn_w_gate', 'ffn_w_up', 'ab_w_out', 'c_w_out', 'ffn_w_down']
REPLICATED = [n for n in WEIGHTS if n not in SHARD_AXIS]
SMALL_SHARDED = [n for n in WEIGHTS if n in SHARD_AXIS and n not in BIG]


def _bf(x):
    return x.astype(BF16)


def _nn(a, b):
    return lax.dot_general(_bf(a), _bf(b), (((1,), (0,)), ((), ())), preferred_element_type=F32)


def _nt(a, b):
    return lax.dot_general(_bf(a), _bf(b), (((1,), (1,)), ((), ())), preferred_element_type=F32)


def _tn(a, b):
    return lax.dot_general(_bf(a), _bf(b), (((0,), (0,)), ((), ())), preferred_element_type=F32)


def _rms(x, g):
    return x * lax.rsqrt(jnp.mean(x * x, axis=-1, keepdims=True) + NORM_EPS) * g


def _layer_norm(x, g, b):
    xc = x - jnp.mean(x, axis=-1, keepdims=True)
    return xc * lax.rsqrt(jnp.mean(xc * xc, axis=-1, keepdims=True) + NORM_EPS) * g + b


def _gelu(x):
    return jax.nn.gelu(x)


STRIP = 16
STRIP_LANES = 384
GELU_C = math.sqrt(2.0 / math.pi)
GELU_A = 0.044715


def _gelu_and_grad(x):
    x2 = x * x
    t = jnp.tanh(x * (GELU_C + (GELU_C * GELU_A) * x2))
    half_x = 0.5 * x
    one_plus_t = 1.0 + t
    return half_x * one_plus_t, 0.5 * one_plus_t + half_x * (1.0 - t * t) * (GELU_C + (3.0 * GELU_C * GELU_A) * x2)


def _colsum(x):
    return jnp.sum(x, axis=0, keepdims=True)


def _softplus(x):
    return jnp.maximum(x, 0.0) + jnp.log1p(jnp.exp(-jnp.abs(x)))


@jax.custom_vjp
def _decay(x):
    a = jnp.exp(x)
    y = 2.0 * x
    series = -y * (1.0 + y * (1 / 2 + y * (1 / 6 + y * (1 / 24 + y * (1 / 120 + y * (1 / 720))))))
    return a, jnp.where(y < -0.3, 1.0 - a * a, series)


def _decay_fwd(x):
    a, gap = _decay(x)
    return (a, gap), a


def _decay_bwd(a, cts):
    return (a * (cts[0] - 2.0 * a * cts[1]),)


_decay.defvjp(_decay_fwd, _decay_bwd)


def _accumulate(ref, val, first):
    @pl.when(first)
    def _():
        ref[...] = val

    @pl.when(jnp.logical_not(first))
    def _():
        ref[...] += val


def _params(n_axes=1):
    return pltpu.CompilerParams(dimension_semantics=("arbitrary",) * n_axes, vmem_limit_bytes=VMEM_LIMIT)


def _row(tm, n):
    return pl.BlockSpec((tm, n), lambda i: (i, 0))


def _const(shape):
    nd = len(shape)
    return pl.BlockSpec(shape, lambda i: (0,) * nd, pipeline_mode=pl.Buffered(1))


def _prev_halo(tm, n):
    return pl.BlockSpec((HALO, n), lambda i: (jnp.maximum(i * (tm // HALO) - 1, 0), 0))


def _next_halo(tm, n, n_tiles):
    last = n_tiles * (tm // HALO) - 1
    return pl.BlockSpec((HALO, n), lambda i: (jnp.minimum((i + 1) * (tm // HALO), last), 0))


def _sds(shape, dtype=F32):
    return jax.ShapeDtypeStruct(shape, dtype)


def _rope_tables(posb):
    lane = lax.broadcasted_iota(jnp.int32, posb.shape, 1)
    in_rope = jnp.logical_and(lane >= QK_NOPE, lane < QK_NOPE + QK_ROPE)
    j = (lane & (ROPE_HALF - 1)).astype(F32)
    inv_freq = jnp.exp((-math.log(ROPE_BASE)) * j / ROPE_HALF)
    ang = posb * inv_freq
    return jnp.where(in_rope, jnp.cos(ang), 1.0), jnp.where(in_rope, jnp.sin(ang), 0.0)


def _rot(q):
    n = q.shape[1]
    lane = lax.broadcasted_iota(jnp.int32, q.shape, 1) & (HEAD_PAD - 1)
    first_half = jnp.where(lane >= QK_NOPE, -pltpu.roll(q, n - ROPE_HALF, 1), 0.0)
    second_half = jnp.where(lane < QK_NOPE + QK_ROPE, pltpu.roll(q, ROPE_HALF, 1), 0.0)
    return jnp.where(lane < QK_NOPE + ROPE_HALF, first_half, second_half)


def _rope(q, cos_t, sin_t):
    return q * cos_t + _rot(q) * sin_t


def _rope_transpose(dq, cos_t, sin_t):
    return dq * cos_t - _rot(dq * sin_t)


def _tile_heads(t):
    return jnp.concatenate([t] * HEADS, axis=1)


Q_LORA, KV_LORA = 256, 128
Z_KPE = Q_LORA + KV_LORA
Z_LRU = Z_KPE + HEAD_PAD
Z_GATE = Z_LRU + LRU_W
Z_WIDTH = Z_GATE + LRU_W


def _ab_in_fwd(x, posb, w, tm):
    t, d = x.shape

    def body(x_ref, pos_ref, gn_ref, win_ref, qn_ref, wq_ref, kvn_ref, wk_ref, wv_ref, q_out, k_out, v_out, xl_out, gate_out):
        hn = _rms(x_ref[...], gn_ref[...])
        z = _nn(hn, win_ref[...])
        cqn = _rms(z[:, :Q_LORA], qn_ref[...])
        kvn = _rms(z[:, Q_LORA:Z_KPE], kvn_ref[...])
        cos_t, sin_t = _rope_tables(pos_ref[...])
        q_out[...] = _rope(_nn(cqn, wq_ref[...]), _tile_heads(cos_t), _tile_heads(sin_t))
        kpe = _rope(z[:, Z_KPE:Z_LRU], cos_t, sin_t)
        k_out[...] = _nn(kvn, wk_ref[...]) + _tile_heads(kpe)
        v_out[...] = _nn(kvn, wv_ref[...])
        xl_out[...] = z[:, Z_LRU:Z_GATE]
        gate_out[...] = z[:, Z_GATE:]

    hp = HEADS * HEAD_PAD
    return pl.pallas_call(
        body, name="ab_in_fwd", grid=(t // tm,),
        in_specs=[_row(tm, d), _row(tm, LANES), _const((1, d)), _const((d, Z_WIDTH)), _const((1, Q_LORA)), _const((Q_LORA, hp)),
                  _const((1, KV_LORA)), _const((KV_LORA, hp)), _const((KV_LORA, hp))],
        out_specs=[_row(tm, hp), _row(tm, hp), _row(tm, hp), _row(tm, LRU_W), _row(tm, LRU_W)],
        out_shape=[_sds((t, hp)), _sds((t, hp)), _sds((t, hp)), _sds((t, LRU_W)), _sds((t, LRU_W))],
        compiler_params=_params(),
    )(x, posb, w['ab_norm'], w['W_in'], w['ab_q_norm'], w['Wq'], w['ab_kv_norm'], w['Wk'], w['Wv'])


def _ab_in_bwd(x, posb, w, dq, dk, dv, dxl, dgate, dres, tm):
    t, d = x.shape
    hp = HEADS * HEAD_PAD

    def body(x_ref, pos_ref, gn_ref, win_ref, qn_ref, wq_ref, kvn_ref, wk_ref, wv_ref, dq_ref, dk_ref, dv_ref, dxl_ref, dgate_ref,
             dres_ref, dx_out, dgn_out, dwin_out, dqn_out, dwq_out, dkvn_out, dwk_out, dwv_out):
        first = pl.program_id(0) == 0
        hn, vjp_in = jax.vjp(_rms, x_ref[...], gn_ref[...])
        z = _nn(hn, win_ref[...])
        cqn, vjp_q = jax.vjp(_rms, z[:, :Q_LORA], qn_ref[...])
        kvn, vjp_kv = jax.vjp(_rms, z[:, Q_LORA:Z_KPE], kvn_ref[...])
        cos_t, sin_t = _rope_tables(pos_ref[...])
        dq0 = _rope_transpose(dq_ref[...], _tile_heads(cos_t), _tile_heads(sin_t))
        dk0 = dk_ref[...]
        dv0 = dv_ref[...]
        dkpe = dk0[:, :HEAD_PAD]
        for h in range(1, HEADS):
            dkpe = dkpe + dk0[:, h * HEAD_PAD:(h + 1) * HEAD_PAD]
        dkpe = _rope_transpose(dkpe, cos_t, sin_t)
        _accumulate(dwq_out, _tn(cqn, dq0), first)
        _accumulate(dwk_out, _tn(kvn, dk0), first)
        _accumulate(dwv_out, _tn(kvn, dv0), first)
        dcq, dqn = vjp_q(_nt(dq0, wq_ref[...]))
        dckv, dkvn = vjp_kv(_nt(dk0, wk_ref[...]) + _nt(dv0, wv_ref[...]))
        _accumulate(dqn_out, dqn, first)
        _accumulate(dkvn_out, dkvn, first)
        dz = jnp.concatenate([dcq, dckv, dkpe, dxl_ref[...], dgate_ref[...]], axis=1)
        _accumulate(dwin_out, _tn(hn, dz), first)
        dx, dgn = vjp_in(_nt(dz, win_ref[...]))
        _accumulate(dgn_out, dgn, first)
        dx_out[...] = dx + dres_ref[...]

    return pl.pallas_call(
        body, name="ab_in_bwd", grid=(t // tm,),
        in_specs=[_row(tm, d), _row(tm, LANES), _const((1, d)), _const((d, Z_WIDTH)), _const((1, Q_LORA)), _const((Q_LORA, hp)),
                  _const((1, KV_LORA)), _const((KV_LORA, hp)), _const((KV_LORA, hp)),
                  _row(tm, hp), _row(tm, hp), _row(tm, hp), _row(tm, LRU_W), _row(tm, LRU_W), _row(tm, d)],
        out_specs=[_row(tm, d), _const((1, d)), _const((d, Z_WIDTH)), _const((1, Q_LORA)), _const((Q_LORA, hp)),
                   _const((1, KV_LORA)), _const((KV_LORA, hp)), _const((KV_LORA, hp))],
        out_shape=[_sds((t, d)), _sds((1, d)), _sds((d, Z_WIDTH)), _sds((1, Q_LORA)), _sds((Q_LORA, hp)),
                   _sds((1, KV_LORA)), _sds((KV_LORA, hp)), _sds((KV_LORA, hp))],
        compiler_params=_params(),
    )(x, posb, w['ab_norm'], w['W_in'], w['ab_q_norm'], w['Wq'], w['ab_kv_norm'], w['Wk'], w['Wv'], dq, dk, dv, dxl, dgate, dres)


def _attn_probs(q_blk, k_ext, i, tq):
    ext = k_ext.shape[0]
    s = lax.dot_general(q_blk, k_ext, (((1,), (1,)), ((), ())), preferred_element_type=F32) * ATTN_SCALE
    causal = lax.broadcasted_iota(jnp.int32, (tq, tq), 1) <= lax.broadcasted_iota(jnp.int32, (tq, tq), 0)
    diag = jnp.where(causal, s[:, ext - tq:], -1e30)
    s = diag if ext == tq else jnp.concatenate([s[:, :ext - tq], diag], axis=1)
    p = jnp.exp(s - jnp.max(s, axis=1, keepdims=True))
    return p / jnp.sum(p, axis=1, keepdims=True)


def _attn_fwd(q, k, v, tq):
    b, s, hp = q.shape
    blk = pl.BlockSpec((1, s, HEAD_PAD), lambda bi, h: (bi, 0, h))

    def body(q_ref, k_ref, v_ref, o_ref):
        kb = _bf(k_ref[0])
        vb = _bf(v_ref[0])
        for i in range(s // tq):
            ext = (i + 1) * tq
            p = _attn_probs(_bf(q_ref[0, i * tq:ext, :]), kb[:ext], i, tq)
            o_ref[0, i * tq:ext, :] = lax.dot_general(_bf(p), vb[:ext], (((1,), (0,)), ((), ())), preferred_element_type=F32)

    return pl.pallas_call(body, name="attn_fwd", grid=(b, HEADS), in_specs=[blk, blk, blk], out_specs=blk,
                          out_shape=_sds((b, s, hp)), compiler_params=_params(2))(q, k, v)


def _attn_bwd(q, k, v, do, tq):
    b, s, hp = q.shape
    blk = pl.BlockSpec((1, s, HEAD_PAD), lambda bi, h: (bi, 0, h))

    def body(q_ref, k_ref, v_ref, do_ref, dq_ref, dk_ref, dv_ref):
        kb = _bf(k_ref[0])
        vb = _bf(v_ref[0])
        dk_ref[...] = jnp.zeros_like(dk_ref)
        dv_ref[...] = jnp.zeros_like(dv_ref)
        for i in range(s // tq):
            ext = (i + 1) * tq
            qb = _bf(q_ref[0, i * tq:ext, :])
            dob = _bf(do_ref[0, i * tq:ext, :])
            p = _attn_probs(qb, kb[:ext], i, tq)
            dv_ref[0, :ext, :] += lax.dot_general(_bf(p), dob, (((0,), (0,)), ((), ())), preferred_element_type=F32)
            dp = lax.dot_general(dob, vb[:ext], (((1,), (1,)), ((), ())), preferred_element_type=F32)
            ds = _bf(p * (dp - jnp.sum(p * dp, axis=1, keepdims=True)) * ATTN_SCALE)
            dq_ref[0, i * tq:ext, :] = lax.dot_general(ds, kb[:ext], (((1,), (0,)), ((), ())), preferred_element_type=F32)
            dk_ref[0, :ext, :] += lax.dot_general(ds, qb, (((0,), (0,)), ((), ())), preferred_element_type=F32)

    return pl.pallas_call(body, name="attn_bwd", grid=(b, HEADS), in_specs=[blk, blk, blk, blk], out_specs=[blk, blk, blk],
                          out_shape=[_sds((b, s, hp))] * 3, compiler_params=_params(2))(q, k, v, do)


LRU_CONV = 4


def _lru_point(pre_a, pre_x, xc, lam):
    r = jax.nn.sigmoid(pre_a)
    i = jax.nn.sigmoid(pre_x)
    a, gap = _decay(-LRU_C * r * _softplus(-lam))
    return a, jnp.sqrt(gap) * (i * xc)


def _causal_conv(pad_ref, x, halo, first_in_seq, w, taps):
    tm = x.shape[0]
    pad_ref[:HALO, :] = jnp.where(first_in_seq, 0.0, halo)
    pad_ref[HALO:, :] = x
    y = w[taps - 1:taps, :] * x
    for k in range(taps - 1):
        off = HALO - (taps - 1) + k
        y = y + w[k:k + 1, :] * pad_ref[off:off + tm, :]
    return y


def _conv_taps(pad_ref, r, cols, taps):
    blocks = [pad_ref[r + j * HALO:r + (j + 1) * HALO, cols] for j in range(1 + STRIP // HALO)]
    sub = lax.broadcasted_iota(jnp.int32, blocks[0].shape, 0)
    out = []
    for k in range(taps - 1):
        s = taps - 1 - k
        rolled = [pltpu.roll(b, s, 0) for b in blocks]
        out.append(jnp.concatenate([jnp.where(sub < s, rolled[j], rolled[j + 1]) for j in range(STRIP // HALO)], axis=0))
    out.append(jnp.concatenate(blocks[1:], axis=0))
    return out


def _causal_conv_wgrad(pad_ref, dy, taps):
    tm = dy.shape[0]
    return jnp.concatenate([_colsum(dy * pad_ref[HALO - (taps - 1) + k:HALO - (taps - 1) + k + tm, :]) for k in range(taps)], axis=0)


def _causal_conv_transpose(pad_ref, dy, halo_next, last_in_seq, w, taps):
    tm = dy.shape[0]
    pad_ref[:tm, :] = dy
    pad_ref[tm:, :] = jnp.where(last_in_seq, 0.0, halo_next)
    dx = w[taps - 1:taps, :] * dy
    for k in range(taps - 1):
        off = (taps - 1) - k
        dx = dx + w[k:k + 1, :] * pad_ref[off:off + tm, :]
    return dx


def _lru_fwd(xl, gate, w, ts, seq):
    t, n = xl.shape
    tiles_per_seq = seq // ts

    def body(xl_ref, halo_ref, gate_ref, cw_ref, cb_ref, wa_ref, ba_ref, wx_ref, bx_ref, lam_ref, y_out, h_out, pad_ref, a_ref, b_ref, carry_ref):
        first_in_seq = pl.program_id(0) % tiles_per_seq == 0
        xc = _causal_conv(pad_ref, xl_ref[...], halo_ref[...], first_in_seq, cw_ref[...], LRU_CONV) + cb_ref[...]
        a, bx = _lru_point(_nn(xc, wa_ref[...]) + ba_ref[...], _nn(xc, wx_ref[...]) + bx_ref[...], xc, lam_ref[...])
        a_ref[...] = a
        b_ref[...] = bx

        @pl.when(first_in_seq)
        def _():
            carry_ref[...] = jnp.zeros_like(carry_ref)

        def step(r, h):
            h = a_ref[pl.ds(r, 1), :] * h + b_ref[pl.ds(r, 1), :]
            h_out[pl.ds(r, 1), :] = h
            return h

        carry_ref[...] = lax.fori_loop(0, ts, step, carry_ref[...], unroll=8)
        y_out[...] = h_out[...] * _gelu(gate_ref[...])

    return pl.pallas_call(
        body, name="lru_fwd", grid=(t // ts,),
        in_specs=[_row(ts, n), _prev_halo(ts, n), _row(ts, n), _const((LRU_CONV, n)), _const((1, n)), _const((n, n)), _const((1, n)),
                  _const((n, n)), _const((1, n)), _const((1, n))],
        out_specs=[_row(ts, n), _row(ts, n)], out_shape=[_sds((t, n)), _sds((t, n))],
        scratch_shapes=[pltpu.VMEM((HALO + ts, n), F32), pltpu.VMEM((ts, n), F32), pltpu.VMEM((ts, n), F32), pltpu.VMEM((1, n), F32)],
        compiler_params=_params(),
    )(xl, xl, gate, w['ab_conv_w'], w['ab_conv_b'], w['Wa'], w['ab_b_rg_a'], w['Wx'], w['ab_b_rg_x'], w['ab_lambda'])


def _lru_bwd(xl, gate, hs, dy, w, ts, seq):
    t, n = xl.shape
    tiles_per_seq = seq // ts
    n_tiles = t // ts

    def rev(i):
        return n_tiles - 1 - i

    row = pl.BlockSpec((ts, n), lambda i: (rev(i), 0))
    prev = pl.BlockSpec((HALO, n), lambda i: (jnp.maximum(rev(i) * (ts // HALO) - 1, 0), 0))
    acc = lambda shape: pl.BlockSpec(shape, lambda i: (0,) * len(shape))

    def body(xl_ref, xhalo_ref, gate_ref, h_ref, hhalo_ref, dy_ref, cw_ref, cb_ref, wa_ref, ba_ref, wx_ref, bx_ref, lam_ref,
             dxl_out, dgate_out, dcw_out, dcb_out, dwa_out, dba_out, dwx_out, dbx_out, dlam_out,
             pad_ref, padh_ref, padd_ref, a_ref, g_ref, carry_ref, dhalo_ref):
        step_id = pl.program_id(0)
        first = step_id == 0
        tile = rev(step_id)
        first_in_seq = tile % tiles_per_seq == 0
        last_in_seq = tile % tiles_per_seq == tiles_per_seq - 1
        cw = cw_ref[...]
        xc = _causal_conv(pad_ref, xl_ref[...], xhalo_ref[...], first_in_seq, cw, LRU_CONV) + cb_ref[...]
        pre_a = _nn(xc, wa_ref[...]) + ba_ref[...]
        pre_x = _nn(xc, wx_ref[...]) + bx_ref[...]
        (a, _), vjp_point = jax.vjp(_lru_point, pre_a, pre_x, xc, lam_ref[...])
        h = h_ref[...]
        _, vjp_out = jax.vjp(lambda h_, g_: h_ * _gelu(g_), h, gate_ref[...])
        dh, dgate = vjp_out(dy_ref[...])
        dgate_out[...] = dgate
        a_ref[...] = a
        g_ref[...] = dh

        @pl.when(last_in_seq)
        def _():
            carry_ref[...] = jnp.zeros_like(carry_ref)

        def step(j, c):
            r = ts - 1 - j
            g = g_ref[pl.ds(r, 1), :] + c
            g_ref[pl.ds(r, 1), :] = g
            return a_ref[pl.ds(r, 1), :] * g

        carry_ref[...] = lax.fori_loop(0, ts, step, carry_ref[...], unroll=8)
        g = g_ref[...]
        padh_ref[:HALO, :] = jnp.where(first_in_seq, 0.0, hhalo_ref[...])
        padh_ref[HALO:, :] = h
        dpre_a, dpre_x, dxc, dlam = vjp_point((g * padh_ref[HALO - 1:HALO - 1 + ts, :], g))
        dxc = dxc + _nt(dpre_a, wa_ref[...]) + _nt(dpre_x, wx_ref[...])
        _accumulate(dwa_out, _tn(xc, dpre_a), first)
        _accumulate(dwx_out, _tn(xc, dpre_x), first)
        _accumulate(dba_out, _colsum(dpre_a), first)
        _accumulate(dbx_out, _colsum(dpre_x), first)
        _accumulate(dlam_out, dlam, first)
        _accumulate(dcb_out, _colsum(dxc), first)
        _accumulate(dcw_out, _causal_conv_wgrad(pad_ref, dxc, LRU_CONV), first)
        dxl_out[...] = _causal_conv_transpose(padd_ref, dxc, dhalo_ref[...], last_in_seq, cw, LRU_CONV)
        dhalo_ref[...] = dxc[:HALO, :]

    return pl.pallas_call(
        body, name="lru_bwd", grid=(n_tiles,),
        in_specs=[row, prev, row, row, prev, row, _const((LRU_CONV, n)), _const((1, n)), _const((n, n)), _const((1, n)),
                  _const((n, n)), _const((1, n)), _const((1, n))],
        out_specs=[row, row, acc((LRU_CONV, n)), acc((1, n)), acc((n, n)), acc((1, n)), acc((n, n)), acc((1, n)), acc((1, n))],
        out_shape=[_sds((t, n)), _sds((t, n)), _sds((LRU_CONV, n)), _sds((1, n)), _sds((n, n)), _sds((1, n)), _sds((n, n)),
                   _sds((1, n)), _sds((1, n))],
        scratch_shapes=[pltpu.VMEM((HALO + ts, n), F32), pltpu.VMEM((HALO + ts, n), F32), pltpu.VMEM((ts + HALO, n), F32),
                        pltpu.VMEM((ts, n), F32), pltpu.VMEM((ts, n), F32), pltpu.VMEM((1, n), F32), pltpu.VMEM((HALO, n), F32)],
        compiler_params=_params(),
    )(xl, xl, gate, hs, hs, dy, w['ab_conv_w'], w['ab_conv_b'], w['Wa'], w['ab_b_rg_a'], w['Wx'], w['ab_b_rg_x'], w['ab_lambda'])


def _ab_out_fwd(x, o, y, w, tm):
    t, d = x.shape
    hp = o.shape[1]

    def body(x_ref, o_ref, y_ref, wa_ref, wb_ref, h_out):
        h_out[...] = x_ref[...] + _nn(o_ref[...], wa_ref[...]) + _nn(y_ref[...], wb_ref[...])

    return pl.pallas_call(body, name="ab_out_fwd", grid=(t // tm,),
                          in_specs=[_row(tm, d), _row(tm, hp), _row(tm, LRU_W), _const((hp, d)), _const((LRU_W, d))],
                          out_specs=_row(tm, d), out_shape=_sds((t, d)), compiler_params=_params())(x, o, y, w['Wo_a'], w['Wo_b'])


def _ab_out_bwd(o, y, dh, w, tm):
    t, d = dh.shape
    hp = o.shape[1]

    def body(o_ref, y_ref, dh_ref, wa_ref, wb_ref, do_out, dy_out, dwa_out, dwb_out):
        first = pl.program_id(0) == 0
        dh_t = dh_ref[...]
        do_out[...] = _nt(dh_t, wa_ref[...])
        dy_out[...] = _nt(dh_t, wb_ref[...])
        _accumulate(dwa_out, _tn(o_ref[...], dh_t), first)
        _accumulate(dwb_out, _tn(y_ref[...], dh_t), first)

    return pl.pallas_call(body, name="ab_out_bwd", grid=(t // tm,),
                          in_specs=[_row(tm, hp), _row(tm, LRU_W), _row(tm, d), _const((hp, d)), _const((LRU_W, d))],
                          out_specs=[_row(tm, hp), _row(tm, LRU_W), _const((hp, d)), _const((LRU_W, d))],
                          out_shape=[_sds((t, hp)), _sds((t, LRU_W)), _sds((hp, d)), _sds((LRU_W, d))],
                          compiler_params=_params())(o, y, dh, w['Wo_a'], w['Wo_b'])


FFN_CONV = 3


def _ffn_a_fwd(h, norm, wg, wu, tm):
    t, d = h.shape
    fb = D_FF // FF_BLOCKS

    def body(h_ref, gn_ref, wg_ref, wu_ref, g_out, u_out, hn_out):
        hn = _bf(_rms(h_ref[...], gn_ref[...]))
        hn_out[0] = hn
        g_out[...] = _nn(hn, wg_ref[...])
        u_out[...] = _nn(hn, wu_ref[...])

    wspec = pl.BlockSpec((d, fb), lambda f, i: (0, f))
    ospec = pl.BlockSpec((tm, fb), lambda f, i: (i, f))
    return pl.pallas_call(
        body, name="ffn_a_fwd", grid=(FF_BLOCKS, t // tm),
        in_specs=[pl.BlockSpec((tm, d), lambda f, i: (i, 0)), pl.BlockSpec((1, d), lambda f, i: (0, 0)), wspec, wspec],
        out_specs=[ospec, ospec, pl.BlockSpec((1, tm, d), lambda f, i: (f, i, 0))],
        out_shape=[_sds((t, D_FF)), _sds((t, D_FF)), _sds((FF_BLOCKS, t, d), BF16)], compiler_params=_params(2))(h, norm, wg, wu)


def _ffn_b_fwd(g, u, h, cw, cb, wd, tm, seq):
    t, d = h.shape
    tiles_per_seq = seq // tm

    def body(g_ref, halo_ref, u_ref, h_ref, cw_ref, cb_ref, wd_ref, h_out, pad_ref, act_ref):
        pad_ref[:HALO, :] = jnp.where(pl.program_id(0) % tiles_per_seq == 0, 0.0, halo_ref[...])
        pad_ref[HALO:, :] = g_ref[...]
        cw = cw_ref[...]
        cb = cb_ref[...]
        for c0 in range(0, D_FF, STRIP_LANES):
            cols = slice(c0, min(c0 + STRIP_LANES, D_FF))
            for r in range(0, tm, STRIP):
                taps = _conv_taps(pad_ref, r, cols, FFN_CONV)
                gc = cb[:, cols] + cw[0:1, cols] * taps[0] + cw[1:2, cols] * taps[1] + cw[2:3, cols] * taps[2]
                act_ref[r:r + STRIP, cols] = _bf(_gelu(gc) * u_ref[r:r + STRIP, cols])
        h_out[...] = h_ref[...] + _nn(act_ref[...], wd_ref[...])

    return pl.pallas_call(body, name="ffn_b_fwd", grid=(t // tm,),
                          in_specs=[_row(tm, D_FF), _prev_halo(tm, D_FF), _row(tm, D_FF), _row(tm, d), _const((FFN_CONV, D_FF)),
                                    _const((1, D_FF)), _const((D_FF, d))],
                          out_specs=_row(tm, d), out_shape=_sds((t, d)),
                          scratch_shapes=[pltpu.VMEM((HALO + tm, D_FF), F32), pltpu.VMEM((tm, D_FF), BF16)],
                          compiler_params=_params())(g, g, u, h, cw, cb, wd)


def _ffn_b_bwd(g, u, dout, cw, cb, wd, tm, seq):
    t, d = dout.shape
    fb = D_FF // FF_BLOCKS
    tiles_per_seq = seq // tm

    def body(g_ref, halo_ref, u_ref, dout_ref, cw_ref, cb_ref, wd_ref, dgc_out, du_out, dwd_out, dcw_out, dcb_out,
             pad_ref, dact_ref, act_ref, acc_ref):
        i = pl.program_id(1)
        first = i == 0
        pad_ref[:HALO, :] = jnp.where(i % tiles_per_seq == 0, 0.0, halo_ref[...])
        pad_ref[HALO:, :] = g_ref[...]
        dout_b = _bf(dout_ref[...])
        dact_ref[...] = _nt(dout_b, wd_ref[...])
        cw = cw_ref[...]
        cb = cb_ref[...]
        fold = lambda a: a[:HALO] + a[HALO:]
        for c0 in range(0, fb, STRIP_LANES):
            cols = slice(c0, min(c0 + STRIP_LANES, fb))
            sums = [jnp.zeros((HALO, cols.stop - c0), F32) for _ in range(1 + FFN_CONV)]
            for r in range(0, tm, STRIP):
                rows = slice(r, r + STRIP)
                taps = _conv_taps(pad_ref, r, cols, FFN_CONV)
                gelu, dgelu = _gelu_and_grad(cb[:, cols] + cw[0:1, cols] * taps[0] + cw[1:2, cols] * taps[1] + cw[2:3, cols] * taps[2])
                u = u_ref[rows, cols]
                dact = dact_ref[rows, cols]
                act_ref[rows, cols] = _bf(gelu * u)
                du_out[rows, cols] = _bf(dact * gelu)
                dgc = dact * u * dgelu
                dgc_out[rows, cols] = dgc
                sums = [sums[0] + fold(dgc)] + [sums[1 + k] + fold(dgc * taps[k]) for k in range(FFN_CONV)]
            for k in range(1 + FFN_CONV):
                acc_ref[k, :, cols] = sums[k]
        _accumulate(dwd_out, _tn(act_ref[...], dout_b), first)
        _accumulate(dcb_out, _colsum(acc_ref[0]), first)
        _accumulate(dcw_out, jnp.concatenate([_colsum(acc_ref[1 + k]) for k in range(FFN_CONV)], axis=0), first)

    blk = pl.BlockSpec((tm, fb), lambda f, i: (i, f))
    halo = pl.BlockSpec((HALO, fb), lambda f, i: (jnp.maximum(i * (tm // HALO) - 1, 0), f))
    wd_blk = pl.BlockSpec((fb, d), lambda f, i: (f, 0), pipeline_mode=pl.Buffered(1))
    return pl.pallas_call(
        body, name="ffn_b_bwd", grid=(FF_BLOCKS, t // tm),
        in_specs=[blk, halo, blk, pl.BlockSpec((tm, d), lambda f, i: (i, 0)), pl.BlockSpec((FFN_CONV, fb), lambda f, i: (0, f)),
                  pl.BlockSpec((1, fb), lambda f, i: (0, f)), wd_blk],
        out_specs=[blk, blk, wd_blk, pl.BlockSpec((FFN_CONV, fb), lambda f, i: (0, f)),
                   pl.BlockSpec((1, fb), lambda f, i: (0, f))],
        out_shape=[_sds((t, D_FF)), _sds((t, D_FF), BF16), _sds((D_FF, d)), _sds((FFN_CONV, D_FF)), _sds((1, D_FF))],
        scratch_shapes=[pltpu.VMEM((HALO + tm, fb), F32), pltpu.VMEM((tm, fb), F32), pltpu.VMEM((tm, fb), BF16),
                        pltpu.VMEM((1 + FFN_CONV, HALO, fb), F32)],
        compiler_params=_params(2))(g, g, u, dout, cw, cb, wd)


def _ffn_a_dgrad(h, norm, dgc, du, dres, cw, wg, wu, tm, seq):
    t, d = h.shape
    tiles_per_seq = seq // tm
    n_tiles = t // tm

    def body(h_ref, gn_ref, dgc_ref, halo_ref, du_ref, dres_ref, cw_ref, wg_ref, wu_ref, dh_out, dg_out, dgn_out, pad_ref):
        i = pl.program_id(0)
        last_in_seq = i % tiles_per_seq == tiles_per_seq - 1
        dg = _bf(_causal_conv_transpose(pad_ref, dgc_ref[...], halo_ref[...], last_in_seq, cw_ref[...], FFN_CONV))
        dg_out[...] = dg
        _, vjp_norm = jax.vjp(_rms, h_ref[...], gn_ref[...])
        dh, dgn = vjp_norm(_nt(dg, wg_ref[...]) + _nt(du_ref[...], wu_ref[...]))
        dh_out[...] = dh + dres_ref[...]
        _accumulate(dgn_out, dgn, i == 0)

    return pl.pallas_call(
        body, name="ffn_a_dgrad", grid=(n_tiles,),
        in_specs=[_row(tm, d), _const((1, d)), _row(tm, D_FF), _next_halo(tm, D_FF, n_tiles), _row(tm, D_FF), _row(tm, d),
                  _const((FFN_CONV, D_FF)), _const((d, D_FF)), _const((d, D_FF))],
        out_specs=[_row(tm, d), _row(tm, D_FF), _const((1, d))], out_shape=[_sds((t, d)), _sds((t, D_FF), BF16), _sds((1, d))],
        scratch_shapes=[pltpu.VMEM((tm + HALO, D_FF), F32)], compiler_params=_params())(h, norm, dgc, dgc, du, dres, cw, wg, wu)


def _ffn_a_wgrad(hn, dg, du, tm):
    _, t, d = hn.shape
    fb = D_FF // FF_BLOCKS

    def body(hn_ref, dg_ref, du_ref, dwg_out, dwu_out):
        first = pl.program_id(1) == 0
        hn_t = hn_ref[0]
        _accumulate(dwg_out, _tn(hn_t, dg_ref[...]), first)
        _accumulate(dwu_out, _tn(hn_t, du_ref[...]), first)

    blk = pl.BlockSpec((tm, fb), lambda f, i: (i, f))
    wspec = pl.BlockSpec((d, fb), lambda f, i: (0, f), pipeline_mode=pl.Buffered(1))
    return pl.pallas_call(body, name="ffn_a_wgrad", grid=(FF_BLOCKS, t // tm),
                          in_specs=[pl.BlockSpec((1, tm, d), lambda f, i: (0, i, 0)), blk, blk],
                          out_specs=[wspec, wspec], out_shape=[_sds((d, D_FF)), _sds((d, D_FF))],
                          compiler_params=_params(2))(hn, dg, du)


def _sgu_mix(vn, ws_ref, bst):
    tril = lax.broadcasted_iota(jnp.int32, (CHUNK, CHUNK), 0) >= lax.broadcasted_iota(jnp.int32, (CHUNK, CHUNK), 1)
    wms = [jnp.where(tril, ws_ref[g], 0.0) for g in range(SGU_GROUPS)]
    chunks = []
    for n in range(vn.shape[0] // CHUNK):
        vc = vn[n * CHUNK:(n + 1) * CHUNK, :]
        chunks.append(jnp.concatenate(
            [_nn(wms[g], vc[:, g * CHUNK:(g + 1) * CHUNK]) + bst[:, g:g + 1] for g in range(SGU_GROUPS)], axis=1))
    return jnp.concatenate(chunks, axis=0)


def _sgu_fwd(h, w, tm):
    t, d = h.shape

    def body(h_ref, cn_ref, win_ref, lg_ref, lb_ref, ws_ref, bst_ref, wout_ref, h_out):
        h_t = h_ref[...]
        z = _gelu(_nn(_rms(h_t, cn_ref[...]), win_ref[...]))
        vn = _layer_norm(z[:, d:], lg_ref[...], lb_ref[...])
        s = _sgu_mix(vn, ws_ref, bst_ref[...])
        h_out[...] = h_t + _nn(z[:, :d] * s, wout_ref[...])

    return pl.pallas_call(
        body, name="sgu_fwd", grid=(t // tm,),
        in_specs=[_row(tm, d), _const((1, d)), _const((d, 2 * d)), _const((1, d)), _const((1, d)), _const((SGU_GROUPS, CHUNK, CHUNK)),
                  _const((CHUNK, LANES)), _const((d, d))],
        out_specs=_row(tm, d), out_shape=_sds((t, d)), compiler_params=_params(),
    )(h, w['c_norm'], w['c_w_in'], w['c_ln_g'], w['c_ln_b'], w['c_w_s'], w['bsT'], w['c_w_out'])


def _sgu_bwd(h, dout, w, tm):
    t, d = h.shape

    def body(h_ref, dout_ref, cn_ref, win_ref, lg_ref, lb_ref, ws_ref, bst_ref, wout_ref,
             dh_out, dcn_out, dwin_out, dlg_out, dlb_out, dws_out, dbst_out, dwout_out):
        first = pl.program_id(0) == 0
        hn, vjp_norm = jax.vjp(_rms, h_ref[...], cn_ref[...])
        zpre = _nn(hn, win_ref[...])
        u, vjp_u = jax.vjp(_gelu, zpre[:, :d])
        vn, vjp_v = jax.vjp(lambda zp, lg, lb: _layer_norm(_gelu(zp), lg, lb), zpre[:, d:], lg_ref[...], lb_ref[...])
        s = _sgu_mix(vn, ws_ref, bst_ref[...])
        dout_t = dout_ref[...]
        dus = _nt(dout_t, wout_ref[...])
        _accumulate(dwout_out, _tn(u * s, dout_t), first)
        ds = dus * u
        tril = lax.broadcasted_iota(jnp.int32, (CHUNK, CHUNK), 0) >= lax.broadcasted_iota(jnp.int32, (CHUNK, CHUNK), 1)
        lane = lax.broadcasted_iota(jnp.int32, (CHUNK, LANES), 1)
        dws = [jnp.zeros((CHUNK, CHUNK), F32) for _ in range(SGU_GROUPS)]
        dbst = jnp.zeros((CHUNK, LANES), F32)
        dvn_chunks = []
        for n in range(tm // CHUNK):
            cols = []
            for g in range(SGU_GROUPS):
                ds_ng = ds[n * CHUNK:(n + 1) * CHUNK, g * CHUNK:(g + 1) * CHUNK]
                vc_ng = vn[n * CHUNK:(n + 1) * CHUNK, g * CHUNK:(g + 1) * CHUNK]
                cols.append(_tn(jnp.where(tril, ws_ref[g], 0.0), ds_ng))
                dws[g] = dws[g] + _nt(ds_ng, vc_ng)
                dbst = dbst + jnp.where(lane == g, jnp.sum(ds_ng, axis=1, keepdims=True), 0.0)
            dvn_chunks.append(jnp.concatenate(cols, axis=1))
        dvn = jnp.concatenate(dvn_chunks, axis=0)
        for g in range(SGU_GROUPS):
            val = jnp.where(tril, dws[g], 0.0)

            @pl.when(first)
            def _():
                dws_out[g] = val

            @pl.when(jnp.logical_not(first))
            def _():
                dws_out[g] += val
        _accumulate(dbst_out, dbst, first)
        (dzu,) = vjp_u(dus * s)
        dzv, dlg, dlb = vjp_v(dvn)
        _accumulate(dlg_out, dlg, first)
        _accumulate(dlb_out, dlb, first)
        dzpre = jnp.concatenate([dzu, dzv], axis=1)
        _accumulate(dwin_out, _tn(hn, dzpre), first)
        dh, dcn = vjp_norm(_nt(dzpre, win_ref[...]))
        _accumulate(dcn_out, dcn, first)
        dh_out[...] = dh + dout_t

    return pl.pallas_call(
        body, name="sgu_bwd", grid=(t // tm,),
        in_specs=[_row(tm, d), _row(tm, d), _const((1, d)), _const((d, 2 * d)), _const((1, d)), _const((1, d)),
                  _const((SGU_GROUPS, CHUNK, CHUNK)), _const((CHUNK, LANES)), _const((d, d))],
        out_specs=[_row(tm, d), _const((1, d)), _const((d, 2 * d)), _const((1, d)), _const((1, d)), _const((SGU_GROUPS, CHUNK, CHUNK)),
                   _const((CHUNK, LANES)), _const((d, d))],
        out_shape=[_sds((t, d)), _sds((1, d)), _sds((d, 2 * d)), _sds((1, d)), _sds((1, d)), _sds((SGU_GROUPS, CHUNK, CHUNK)),
                   _sds((CHUNK, LANES)), _sds((d, d))],
        compiler_params=_params(),
    )(h, dout, w['c_norm'], w['c_w_in'], w['c_ln_g'], w['c_ln_b'], w['c_w_s'], w['bsT'], w['c_w_out'])


def _final_loss(h, target, norm, tm):
    t, d = h.shape

    def body(h_ref, tgt_ref, gn_ref, dh_out, loss_out, dgn_out):
        first = pl.program_id(0) == 0
        tgt = tgt_ref[...]

        def loss_fn(h_, g_):
            err = _rms(h_, g_) - tgt
            return 0.5 * jnp.sum(jnp.mean(err * err, axis=-1, keepdims=True), axis=0, keepdims=True)

        loss, vjp_loss = jax.vjp(loss_fn, h_ref[...], gn_ref[...])
        dh, dgn = vjp_loss(jnp.ones((1, 1), F32))
        dh_out[...] = dh
        _accumulate(loss_out, loss, first)
        _accumulate(dgn_out, dgn, first)

    return pl.pallas_call(body, name="final_loss", grid=(t // tm,), in_specs=[_row(tm, d), _row(tm, d), _const((1, d))],
                          out_specs=[_row(tm, d), _const((1, 1)), _const((1, d))],
                          out_shape=[_sds((t, d)), _sds((1, 1)), _sds((1, d))], compiler_params=_params())(h, target, norm)


def _tile(t, seq, want):
    tm = min(want, seq)
    assert seq % tm == 0 and t % tm == 0 and tm % CHUNK == 0
    return tm


def _local_step(x, posb, target, w, seq, late_weights, on_grads):
    t, d = x.shape
    b = t // seq
    hp = HEADS * HEAD_PAD
    tm_big, tm_mid = _tile(t, seq, 512), _tile(t, seq, 256)
    tq = _tile(t, seq, 512)

    q, k, v, xl, gate = _ab_in_fwd(x, posb, w, tm_big)
    o = _attn_fwd(q.reshape(b, seq, hp), k.reshape(b, seq, hp), v.reshape(b, seq, hp), tq).reshape(t, hp)
    y, hs = _lru_fwd(xl, gate, w, tm_big, seq)
    w = {**w, **late_weights('out0', y)}
    h1 = _ab_out_fwd(x, o, y, w, tm_big)
    hcur = h1
    saved = []
    for l in range(2):
        if l == 1:
            w = {**w, **late_weights('mix1', hcur)}
            saved_h2 = hcur
            hcur = _sgu_fwd(hcur, w, tm_mid)
        wl = late_weights('ffn%d' % l, hcur)
        g, u, hn = _ffn_a_fwd(hcur, w['ffn_norm'][l], wl['Wg'], wl['Wu'], tm_big)
        hnext = _ffn_b_fwd(g, u, hcur, w['ffn_conv_w'][l], w['ffn_conv_b'][l], wl['Wd'], tm_mid, seq)
        saved.append((hcur, g, u, wl, hn))
        hcur = hnext
    dh, loss, d_final = _final_loss(hcur, target, w['final_norm'], tm_big)

    ffn = {}
    conv_b = list(w['ffn_conv_b'])
    for l in (1, 0):
        hin, g, u, wl, hn = saved[l]
        dgc, du, d_wd, d_cw, d_cb = _ffn_b_bwd(g, u, dh, w['ffn_conv_w'][l], conv_b[l], wl['Wd'], tm_big, seq)
        dh, dg, d_norm = _ffn_a_dgrad(hin, w['ffn_norm'][l], dgc, du, dh, w['ffn_conv_w'][l], wl['Wg'], wl['Wu'], tm_mid, seq)
        d_wg, d_wu = _ffn_a_wgrad(hn, dg, du, _tile(t, seq, 2048))
        ffn[l] = dict(ffn_norm=d_norm, ffn_conv_w=d_cw, ffn_conv_b=d_cb, Wg=d_wg, Wu=d_wu, Wd=d_wd)
        if l == 1:
            dh, d_cn, d_cwin, d_lg, d_lb, d_ws, d_bst, d_cwout = _sgu_bwd(saved_h2, dh, w, tm_mid)
            zero = on_grads('late1', dict(final_norm=d_final, c_norm=d_cn, c_ln_g=d_lg, c_ln_b=d_lb, c_w_s=d_ws, bsT=d_bst, c_w_in=d_cwin,
                                          c_w_out=d_cwout, Wg=[d_wg], Wu=[d_wu], Wd=[d_wd]))
            conv_b[0] = conv_b[0] + zero
    late0 = {name: [ffn[0][name], ffn[1][name]] for name in ('ffn_norm', 'ffn_conv_w', 'ffn_conv_b')}
    zero = on_grads('late0', dict(late0, Wg=[ffn[0]['Wg']], Wu=[ffn[0]['Wu']], Wd=[ffn[0]['Wd']]))
    w = {**w, 'Wo_b': w['Wo_b'] + zero.astype(w['Wo_b'].dtype)}
    do, dy, d_woa, d_wob = _ab_out_bwd(o, y, dh, w, tm_big)
    dxl, dgate, d_cw, d_cb, d_wa, d_ba, d_wx, d_bx, d_lam = _lru_bwd(xl, gate, hs, dy, w, tm_big, seq)
    zero = on_grads('mid', dict(Wo_a=d_woa, Wo_b=d_wob, ab_conv_w=d_cw, ab_conv_b=d_cb, Wa=d_wa, ab_b_rg_a=d_ba, Wx=d_wx,
                                ab_b_rg_x=d_bx, ab_lambda=d_lam))
    w = {**w, 'ab_norm': w['ab_norm'] + zero}
    dq, dk, dv = _attn_bwd(q.reshape(b, seq, hp), k.reshape(b, seq, hp), v.reshape(b, seq, hp), do.reshape(b, seq, hp), tq)
    dx, d_gn, d_win, d_qn, d_wq, d_kvn, d_wk, d_wv = _ab_in_bwd(
        x, posb, w, dq.reshape(t, hp), dk.reshape(t, hp), dv.reshape(t, hp), dxl, dgate, dh, tm_mid)
    return loss, dx, dict(ab_norm=d_gn, W_in=d_win, ab_q_norm=d_qn, Wq=d_wq, ab_kv_norm=d_kvn, Wk=d_wk, Wv=d_wv)


def _block_diag(wg):
    g, n, _ = wg.shape
    return jnp.einsum('gij,gh->gihj', wg, jnp.eye(g, dtype=wg.dtype)).reshape(g * n, g * n)


def _prepare_out(w_out):
    d = w_out.shape[2]
    mla = HEADS * QK_NOPE
    return {'Wo_a': jnp.pad(w_out[0, :mla].reshape(HEADS, QK_NOPE, d), ((0, 0), (0, HEAD_PAD - QK_NOPE), (0, 0))).reshape(HEADS * HEAD_PAD, d),
            'Wo_b': w_out[0, mla:]}


def _prepare(full):
    d = full['ab_w_in'].shape[1]
    w_in = full['ab_w_in'][0]
    zeros = lambda n: jnp.zeros((d, n), w_in.dtype)
    wq = full['ab_w_q_b'][0].reshape(Q_LORA, HEADS, QK_NOPE + QK_ROPE)
    wkv = full['ab_w_kv_b'][0].reshape(KV_LORA, HEADS, 2 * QK_NOPE)
    pad_head = lambda a: jnp.pad(a, ((0, 0), (0, 0), (0, HEAD_PAD - a.shape[2]))).reshape(a.shape[0], HEADS * HEAD_PAD)
    w = {
        'W_in': jnp.concatenate([w_in[:, :Z_KPE], zeros(QK_NOPE), w_in[:, Z_KPE:Z_KPE + QK_ROPE],
                                 zeros(HEAD_PAD - QK_NOPE - QK_ROPE), w_in[:, Z_KPE + QK_ROPE:]], axis=1),
        'Wq': pad_head(wq), 'Wk': pad_head(wkv[:, :, :QK_NOPE]), 'Wv': pad_head(wkv[:, :, QK_NOPE:]),
        'Wa': _bf(_block_diag(full['ab_w_rg_a'][0])), 'Wx': _bf(_block_diag(full['ab_w_rg_x'][0])),
        'c_w_s': full['c_w_s'][0],
        'bsT': jnp.pad(full['c_b_s'][0].T, ((0, 0), (0, LANES - SGU_GROUPS))),
        'ffn_norm': [full['ffn_norm'][l:l + 1] for l in range(2)], 'ffn_conv_w': [full['ffn_conv_w'][l] for l in range(2)],
        'ffn_conv_b': [full['ffn_conv_b'][l:l + 1] for l in range(2)],
        'ab_conv_w': full['ab_conv_w'][0], 'final_norm': full['final_norm'][None, :],
    }
    for name in ('ab_norm', 'ab_q_norm', 'ab_kv_norm', 'ab_conv_b', 'ab_b_rg_a', 'ab_b_rg_x', 'ab_lambda', 'c_norm', 'c_ln_g', 'c_ln_b'):
        w[name] = full[name]
    return w


def _unprepare(g):
    unpad_head = lambda a, n: a.reshape(a.shape[0], HEADS, HEAD_PAD)[:, :, :n]
    diag = lambda a: jnp.einsum('gigj->gij', a.reshape(HEADS, LRU_W // HEADS, HEADS, LRU_W // HEADS))
    rules = {
        'ab_w_in': (('W_in',), lambda a: jnp.concatenate([a[:, :Z_KPE], a[:, Z_KPE + QK_NOPE:Z_KPE + QK_NOPE + QK_ROPE], a[:, Z_LRU:]], axis=1)[None]),
        'ab_w_q_b': (('Wq',), lambda a: unpad_head(a, QK_NOPE + QK_ROPE).reshape(1, Q_LORA, -1)),
        'ab_w_kv_b': (('Wk', 'Wv'), lambda a, b: jnp.concatenate([unpad_head(a, QK_NOPE), unpad_head(b, QK_NOPE)], axis=2).reshape(1, KV_LORA, -1)),
        'ab_w_out': (('Wo_a', 'Wo_b'), lambda a, b: jnp.concatenate(
            [a.reshape(HEADS, HEAD_PAD, -1)[:, :QK_NOPE].reshape(HEADS * QK_NOPE, -1), b], axis=0)[None]),
        'ab_w_rg_a': (('Wa',), lambda a: diag(a)[None]), 'ab_w_rg_x': (('Wx',), lambda a: diag(a)[None]),
        'c_w_in': (('c_w_in',), lambda a: a[None]), 'c_w_out': (('c_w_out',), lambda a: a[None]), 'c_w_s': (('c_w_s',), lambda a: a[None]),
        'c_b_s': (('bsT',), lambda a: a[:, :SGU_GROUPS].T[None]),
        'ffn_w_gate': (('Wg',), jnp.stack), 'ffn_w_up': (('Wu',), jnp.stack), 'ffn_w_down': (('Wd',), jnp.stack),
        'ffn_norm': (('ffn_norm',), lambda a: jnp.concatenate(a, axis=0)), 'ffn_conv_w': (('ffn_conv_w',), jnp.stack),
        'ffn_conv_b': (('ffn_conv_b',), lambda a: jnp.concatenate(a, axis=0)),
        'ab_conv_w': (('ab_conv_w',), lambda a: a[None]), 'final_norm': (('final_norm',), lambda a: a[0]),
    }
    for name in ('ab_norm', 'ab_q_norm', 'ab_kv_norm', 'ab_conv_b', 'ab_b_rg_a', 'ab_b_rg_x', 'ab_lambda', 'c_norm', 'c_ln_g', 'c_ln_b'):
        rules[name] = ((name,), lambda a: a)
    return {name: fn(*[g[k] for k in keys]) for name, (keys, fn) in rules.items() if all(k in g for k in keys)}


SLAB_ROWS = 16


def _round_up(n, m):
    return -(-n // m) * m


def _to_chunks(full, axis):
    s = full.shape
    return jnp.moveaxis(full.reshape(s[:axis] + (N_DEV, s[axis] // N_DEV) + s[axis + 1:]), axis, 0)


def _from_chunks(chunks, axis):
    local = chunks.shape[1:]
    return jnp.moveaxis(chunks, 0, axis).reshape(local[:axis] + (N_DEV * local[axis],) + local[axis + 1:])


def _merge_chunks(me, own, landed, axis, name):
    _, r, n = own.shape
    if axis == 1:
        def body(me_ref, own_ref, l_ref, o_ref):
            o_ref[...] = jnp.where(me_ref[0] == pl.program_id(0), own_ref[...], l_ref[0])

        grid, out_shape = (N_DEV,), (1, N_DEV * r, n)
        specs = [pl.BlockSpec((1, r, n), lambda dev, me_ref: (0, 0, 0)), pl.BlockSpec((1, 1, r, n), lambda dev, me_ref: (dev, 0, 0, 0))]
        out_spec = pl.BlockSpec((1, r, n), lambda dev, me_ref: (0, dev, 0))
    else:
        tr = r // 4

        def body(me_ref, own_ref, l_ref, o_ref):
            o_ref[0] = jnp.concatenate([jnp.where(me_ref[0] == dev, own_ref[0], l_ref[dev, 0]) for dev in range(N_DEV)], axis=1)

        grid, out_shape = (r // tr,), (1, r, N_DEV * n)
        specs = [pl.BlockSpec((1, tr, n), lambda i, me_ref: (0, i, 0)), pl.BlockSpec((N_DEV, 1, tr, n), lambda i, me_ref: (0, 0, i, 0))]
        out_spec = pl.BlockSpec((1, tr, N_DEV * n), lambda i, me_ref: (0, i, 0))
    return pl.pallas_call(
        body, name="merge_" + name,
        grid_spec=pltpu.PrefetchScalarGridSpec(num_scalar_prefetch=1, grid=grid, in_specs=specs, out_specs=out_spec),
        out_shape=jax.ShapeDtypeStruct(out_shape, own.dtype), compiler_params=_params())(me, own, landed)


def _split_chunks(whole, axis, name):
    _, rows, cols = whole.shape
    if axis == 1:
        r = rows // N_DEV

        def body(x_ref, o_ref):
            o_ref[0] = _bf(x_ref[...])

        grid, out_shape = (N_DEV,), (N_DEV, 1, r, cols)
        spec, out_spec = pl.BlockSpec((1, r, cols), lambda dev: (0, dev, 0)), pl.BlockSpec((1, 1, r, cols), lambda dev: (dev, 0, 0, 0))
    else:
        n, tr = cols // N_DEV, rows // 4

        def body(x_ref, o_ref):
            x = x_ref[0]
            for dev in range(N_DEV):
                o_ref[dev, 0] = _bf(x[:, dev * n:(dev + 1) * n])

        grid, out_shape = (rows // tr,), (N_DEV, 1, rows, n)
        spec, out_spec = pl.BlockSpec((1, tr, cols), lambda i: (0, i, 0)), pl.BlockSpec((N_DEV, 1, tr, n), lambda i: (0, 0, i, 0))
    return pl.pallas_call(body, name="split_" + name, grid=grid, in_specs=[spec], out_specs=out_spec,
                          out_shape=jax.ShapeDtypeStruct(out_shape, BF16), compiler_params=_params())(whole)


def _slab_rows(n):
    return _round_up(-(-n // LANES), SLAB_ROWS)


def _to_slab(a, lead):
    a = a.reshape(lead + (-1,))
    rows = _slab_rows(a.shape[-1])
    a = jnp.pad(a, [(0, 0)] * len(lead) + [(0, rows * LANES - a.shape[-1])])
    return a.reshape(lead + (rows, LANES))


def _pack_slabs(parts, lead):
    return jnp.concatenate([_to_slab(p, lead) for p in parts], axis=len(lead))


def _unpack_slabs(packed, shapes):
    lead = packed.shape[:-2]
    out, row = [], 0
    for shape in shapes:
        size = math.prod(shape)
        rows = _slab_rows(size)
        piece = lax.slice_in_dim(packed, row, row + rows, axis=len(lead))
        out.append(piece.reshape(lead + (rows * LANES,))[..., :size].reshape(lead + tuple(shape)))
        row += rows
    return out


HBM = pl.BlockSpec(memory_space=pl.ANY)


def _other_chips(x, y):
    return [(1 - x, y), (x, 1 - y), (1 - x, 1 - y)]


def _all_gather(blocks):
    n = len(blocks)

    def body(*refs):
        x_refs, out_refs, token = refs[:n], refs[n:2 * n], refs[2 * n]
        send_sems, recv_sems, local_sems = refs[2 * n + 1:]
        token[...] = jnp.zeros_like(token)
        x, y, c = lax.axis_index("x"), lax.axis_index("y"), lax.axis_index("c")
        me, sibling = (x, y, c), (x, y, 1 - c)
        chips = _other_chips(x, y)

        def slab(a, px, py, pc):
            return out_refs[a].at[4 * px + 2 * py + pc]

        def copy(a, k, blk, to, src=None):
            return pltpu.make_async_remote_copy(src_ref=slab(a, *blk) if src is None else src, dst_ref=slab(a, *blk),
                                                send_sem=send_sems.at[7 * a + k], recv_sem=recv_sems.at[7 * a + k],
                                                device_id=to, device_id_type=MESH)

        mine = [pltpu.make_async_copy(x_refs[a], slab(a, *me), local_sems.at[a]) for a in range(n)]
        started = []
        for a in range(n):
            mine[a].start()
            started.append(copy(a, 0, me, sibling, src=x_refs[a]))
            started += [copy(a, 1 + j, me, (*chip, c), src=x_refs[a]) for j, chip in enumerate(chips)]
        for cp in started:
            cp.start()
        for j, chip in enumerate(chips):
            for a in range(n):
                copy(a, 1 + j, (*chip, c), me).wait_recv()
                passed = copy(a, 4 + j, (*chip, c), sibling)
                passed.start()
                started.append(passed)
        for a in range(n):
            copy(a, 0, sibling, me).wait_recv()
        for j, chip in enumerate(chips):
            for a in range(n):
                copy(a, 4 + j, (*chip, 1 - c), me).wait_recv()
        for cp in started:
            cp.wait_send()
        for a in range(n):
            mine[a].wait()

    out = pl.pallas_call(
        body, name="all_gather_weights",
        out_shape=[jax.ShapeDtypeStruct((N_DEV,) + b.shape, b.dtype) for b in blocks] + [jax.ShapeDtypeStruct((8, LANES), F32)],
        in_specs=[HBM] * n, out_specs=[HBM] * n + [pl.BlockSpec(memory_space=pltpu.VMEM)],
        scratch_shapes=[pltpu.SemaphoreType.DMA((7 * n,)), pltpu.SemaphoreType.DMA((7 * n,)), pltpu.SemaphoreType.DMA((n,))],
    )(*blocks)
    return list(out[:n]), out[n][0, 0]


FLIPS = [(0, 0, 1), (1, 0, 0), (1, 0, 1), (0, 1, 0), (0, 1, 1), (1, 1, 0), (1, 1, 1)]


def _peers(x, y, c):
    flip = lambda v, f: 1 - v if f else v
    return [(flip(x, fx), flip(y, fy), flip(c, fc)) for fx, fy, fc in FLIPS]


def _direct_copies(src_refs, land_refs, send_sems, recv_sems, scatter):
    x, y, c = lax.axis_index("x"), lax.axis_index("y"), lax.axis_index("c")
    me = 4 * x + 2 * y + c
    starts, waits = [], []
    for a in range(len(src_refs)):
        for k, (px, py, pc) in enumerate(_peers(x, y, c)):
            peer = 4 * px + 2 * py + pc
            sems = dict(send_sem=send_sems.at[7 * a + k], recv_sem=recv_sems.at[7 * a + k], device_id=(px, py, pc), device_id_type=MESH)
            src = src_refs[a].at[peer] if scatter else src_refs[a]
            starts.append(pltpu.make_async_remote_copy(src_ref=src, dst_ref=land_refs[a].at[me], **sems))
            waits.append(pltpu.make_async_remote_copy(src_ref=src, dst_ref=land_refs[a].at[peer], **sems))
    return starts, waits


def _landing(src, scatter):
    block = src.shape[1:] if scatter else src.shape
    return jax.ShapeDtypeStruct((N_DEV,) + block, src.dtype)


HBM_SPACE = pl.BlockSpec(memory_space=pltpu.HBM)
SEMAPHORES = pl.BlockSpec(memory_space=pltpu.SEMAPHORE)
SPLIT_EFFECT = pltpu.SideEffectType.DATAFLOW_SIDE_EFFECTING


def _start_exchange(name, srcs, scatter):
    n = len(srcs)
    lands = [lax.empty(s.shape, s.dtype) for s in (_landing(s, scatter) for s in srcs)]

    def body(*refs):
        starts, _ = _direct_copies(refs[:n], refs[n:2 * n], refs[2 * n], refs[2 * n + 1], scatter)
        for cp in starts:
            cp.start()
        refs[-1][...] = jnp.zeros_like(refs[-1])

    held = [pltpu.with_memory_space_constraint(a, pltpu.HBM) for a in list(srcs) + lands]
    out = pl.pallas_call(
        body, name=name + "_start",
        out_shape=(pltpu.SemaphoreType.DMA((7 * n,)), pltpu.SemaphoreType.DMA((7 * n,)), *[pltpu.HBM(a.shape, a.dtype) for a in held],
                   jax.ShapeDtypeStruct((8, LANES), F32)),
        in_specs=[HBM_SPACE] * (2 * n), out_specs=(SEMAPHORES, SEMAPHORES, *[HBM_SPACE] * (2 * n), pl.BlockSpec(memory_space=pltpu.VMEM)),
        input_output_aliases={i: 2 + i for i in range(2 * n)},
        compiler_params=pltpu.CompilerParams(has_side_effects=SPLIT_EFFECT),
    )(*held)
    return out[0], out[1], list(out[2:2 + n]), list(out[2 + n:2 + 2 * n]), out[-1][0, 0], out[-1]


def _wait_exchange(name, started, after, scatter):
    send_sems, recv_sems, srcs, lands = started[:4]
    n = len(srcs)

    def body(*refs):
        _, waits = _direct_copies(refs[:n], refs[n:2 * n], refs[2 * n], refs[2 * n + 1], scatter)
        for cp in waits:
            cp.wait_send()
        for cp in waits:
            cp.wait_recv()

    out = pl.pallas_call(
        body, name=name + "_wait", out_shape=tuple(pltpu.HBM(a.shape, a.dtype) for a in srcs + lands),
        in_specs=[HBM_SPACE] * (2 * n) + [SEMAPHORES, SEMAPHORES, HBM], out_specs=tuple([HBM_SPACE] * (2 * n)),
        input_output_aliases={i: i for i in range(2 * n)},
        compiler_params=pltpu.CompilerParams(has_side_effects=SPLIT_EFFECT),
    )(*srcs, *lands, send_sems, recv_sems, after)
    return list(out[:n]), list(out[n:])


def _row_tile(rows):
    return rows // 2 if (rows // 2) % SLAB_ROWS == 0 else rows


def _sum_and_adamw(me, landed, own, wts, m, v, name, layer=None, into=None):
    layers, r, n = wts.shape
    first = 0 if layer is None else layer
    count = layers if layer is None else 1
    tr = _row_tile(r)
    blk = pl.BlockSpec((1, tr, n), lambda li, ri, me_ref: (first + li, ri, 0))
    c1 = 1.0 / (1.0 - ADAM_B1 ** ADAM_STEP)
    c2 = 1.0 / (1.0 - ADAM_B2 ** ADAM_STEP)
    held = [] if into is None else list(into)

    def body(me_ref, l_ref, own_ref, w_ref, m_ref, v_ref, *rest):
        g_out, d_out, m_out, v_out = rest[len(held):]
        mine = own_ref[0].astype(F32)
        g = jnp.where(me_ref[0] == 0, mine, l_ref[0].astype(F32))
        for dev in range(1, N_DEV):
            g = g + jnp.where(me_ref[0] == dev, mine, l_ref[dev].astype(F32))
        m_new = ADAM_B1 * m_ref[...] + (1.0 - ADAM_B1) * g
        v_new = ADAM_B2 * v_ref[...] + (1.0 - ADAM_B2) * (g * g)
        g_out[...] = g
        m_out[...] = m_new
        v_out[...] = v_new
        d_out[...] = -ADAM_LR * ((m_new * c1) / (jnp.sqrt(v_new * c2) + ADAM_EPS) + ADAM_WD * w_ref[...])

    return pl.pallas_call(
        body, name="adamw_" + name,
        grid_spec=pltpu.PrefetchScalarGridSpec(
            num_scalar_prefetch=1, grid=(count, r // tr),
            in_specs=[pl.BlockSpec((N_DEV, 1, tr, n), lambda li, ri, me_ref: (0, li, ri, 0)),
                      pl.BlockSpec((1, 1, tr, n), lambda li, ri, me_ref: (me_ref[0], li, ri, 0)), blk, blk, blk] + [HBM] * len(held),
            out_specs=[blk] * 4),
        out_shape=[_sds((layers, r, n))] * 4, input_output_aliases={6 + i: i for i in range(len(held))},
        compiler_params=_params(2))(me, landed, own, wts, m, v, *held)


EARLY = ['ab_w_in']
LATE_STAGES = {
    'out0': [('ab_w_out', None, 'ab_w_out')],
    'ffn0': [('ffn_w_gate', 0, 'Wg'), ('ffn_w_up', 0, 'Wu'), ('ffn_w_down', 0, 'Wd')],
    'mix1': [('c_w_in', None, 'c_w_in'), ('c_w_out', None, 'c_w_out')],
    'ffn1': [('ffn_w_gate', 1, 'Wg'), ('ffn_w_up', 1, 'Wu'), ('ffn_w_down', 1, 'Wd')],
}
GRAD_STAGES = {
    'late1': ([('c_w_in', None), ('c_w_out', None), ('ffn_w_gate', 1), ('ffn_w_up', 1), ('ffn_w_down', 1)],
              ['c_norm', 'c_ln_g', 'c_ln_b', 'c_w_s', 'c_b_s', 'final_norm']),
    'late0': ([('ffn_w_gate', 0), ('ffn_w_up', 0), ('ffn_w_down', 0)], ['ffn_norm', 'ffn_conv_w', 'ffn_conv_b']),
    'mid': ([('ab_w_out', None)], ['ab_conv_w', 'ab_conv_b', 'ab_w_rg_a', 'ab_b_rg_a', 'ab_w_rg_x', 'ab_b_rg_x', 'ab_lambda']),
    'last': ([('ab_w_in', None)], ['ab_norm', 'ab_q_norm', 'ab_w_q_b', 'ab_kv_norm', 'ab_w_kv_b']),
}


def _gather_early(local):
    small = [_bf(local[n]) if n in MATRICES else lax.bitcast_convert_type(local[n], BF16) for n in SMALL_SHARDED]
    gathered, zero = _all_gather([_bf(local[n]) for n in EARLY] + [_pack_slabs(small, ())])
    full = {n: local[n] for n in REPLICATED}
    for n, g in zip(EARLY, gathered):
        full[n] = _from_chunks(g, SHARD_AXIS[n])
    for n, p in zip(SMALL_SHARDED, _unpack_slabs(gathered[-1], [s.shape for s in small])):
        full[n] = _from_chunks(p if n in MATRICES else lax.bitcast_convert_type(p, F32), SHARD_AXIS[n])
    return full, zero


def kernel(x, positions, ab_norm, ab_w_in, ab_q_norm, ab_w_q_b, ab_kv_norm, ab_w_kv_b, ab_conv_w, ab_conv_b, ab_w_rg_a, ab_b_rg_a, ab_w_rg_x, ab_b_rg_x, ab_lambda, ab_w_out, c_norm, c_w_in, c_ln_g, c_ln_b, c_w_s, c_b_s, c_w_out, ffn_norm, ffn_w_gate, ffn_w_up, ffn_conv_w, ffn_conv_b, ffn_w_down, final_norm, loss_target, m_ab_norm, m_ab_w_in, m_ab_q_norm, m_ab_w_q_b, m_ab_kv_norm, m_ab_w_kv_b, m_ab_conv_w, m_ab_conv_b, m_ab_w_rg_a, m_ab_b_rg_a, m_ab_w_rg_x, m_ab_b_rg_x, m_ab_lambda, m_ab_w_out, m_c_norm, m_c_w_in, m_c_ln_g, m_c_ln_b, m_c_w_s, m_c_b_s, m_c_w_out, m_ffn_norm, m_ffn_w_gate, m_ffn_w_up, m_ffn_conv_w, m_ffn_conv_b, m_ffn_w_down, m_final_norm, v_ab_norm, v_ab_w_in, v_ab_q_norm, v_ab_w_q_b, v_ab_kv_norm, v_ab_w_kv_b, v_ab_conv_w, v_ab_conv_b, v_ab_w_rg_a, v_ab_b_rg_a, v_ab_w_rg_x, v_ab_b_rg_x, v_ab_lambda, v_ab_w_out, v_c_norm, v_c_w_in, v_c_ln_g, v_c_ln_b, v_c_w_s, v_c_b_s, v_c_w_out, v_ffn_norm, v_ffn_w_gate, v_ffn_w_up, v_ffn_conv_w, v_ffn_conv_b, v_ffn_w_down, v_final_norm):
    given = dict(locals())
    local = {n: given[n] for n in WEIGHTS}
    b, seq, d = x.shape
    t = b * seq

    me = (4 * lax.axis_index("x") + 2 * lax.axis_index("y") + lax.axis_index("c")).astype(jnp.int32)
    me1 = me.reshape(1)

    full, zero = _gather_early(local)
    gathers = {}
    for stage, members in LATE_STAGES.items():
        srcs = [_bf((local[n] if layer is None else local[n][layer:layer + 1]) + zero) for n, layer, _ in members]
        gathers[stage] = _start_exchange('gather_' + stage, srcs, scatter=False)
        zero = gathers[stage][4]
    w = _prepare(full)
    w['ab_norm'] = w['ab_norm'] + zero

    def late_weights(stage, after):
        srcs, lands = _wait_exchange('gather_' + stage, gathers[stage], after, scatter=False)
        whole = [_merge_chunks(me1, s, l, SHARD_AXIS[n], n + ('' if layer is None else str(layer)))
                 for (n, layer, _), s, l in zip(LATE_STAGES[stage], srcs, lands)]
        if stage == 'out0':
            return _prepare_out(whole[0])
        return {key: a[0] for (_, _, key), a in zip(LATE_STAGES[stage], whole)}

    scatters = {}

    def start_scatter(stage, g):
        whole = _unprepare(g)
        big, small = GRAD_STAGES[stage]
        slab = [_to_chunks(whole[n], SHARD_AXIS[n]) if n in SHARD_AXIS else jnp.broadcast_to(whole[n][None], (N_DEV,) + whole[n].shape)
                for n in small]
        own = [_split_chunks(whole[n], SHARD_AXIS[n], n + ('' if layer is None else str(layer))) for n, layer in big]
        own.append(_bf(_pack_slabs(slab, (N_DEV,)))[:, None])
        scatters[stage] = _start_exchange('scatter_' + stage, own, scatter=True)
        return scatters[stage][4]

    posb = jnp.broadcast_to(positions.astype(F32).reshape(t, 1), (t, LANES))
    loss, dx, grads = _local_step(x.reshape(t, d), posb, loss_target.reshape(t, d), w, seq, late_weights, start_scatter)
    start_scatter('last', grads)
    after = scatters['last'][5]

    me1 = me.reshape(1)
    updated = {}
    for stage, (big, small) in GRAD_STAGES.items():
        owns, landed = _wait_exchange('scatter_' + stage, scatters[stage], after, scatter=True)
        for (n, layer), own, land in zip(big, owns, landed):
            updated[n] = _sum_and_adamw(me1, land, own, given[n], given['m_' + n], given['v_' + n], n + ('' if layer is None else str(layer)),
                                        layer, updated.get(n))
        pack_small = lambda prefix: _pack_slabs([given[prefix + n] for n in small], ())[None]
        packed = _sum_and_adamw(me1, landed[-1], owns[-1], pack_small(''), pack_small('m_'), pack_small('v_'), 'small_' + stage)
        unpacked = [_unpack_slabs(p[0], [local[n].shape for n in small]) for p in packed]
        for i, n in enumerate(small):
            updated[n] = [u[i] for u in unpacked]
        after = sum([updated[n][1][:1, :1, :1] for n, _ in big], packed[1][:1, :1, :1])
    total = lax.psum(loss[0, 0], ("x", "y", "c"))
    return (total, dx.reshape(b, seq, d), *[updated[n][kind] for kind in range(4) for n in WEIGHTS])
```

```python
import math

import jax
import jax.numpy as jnp
from jax import lax
from jax.experimental import pallas as pl
from jax.experimental.pallas import tpu as pltpu

F32 = jnp.float32
BF16 = jnp.bfloat16
MESH = pl.DeviceIdType.MESH

N_DEV = 8
LANES = 128
HALO = 8
VMEM_LIMIT = 56 << 20

NORM_EPS = 1e-6
HEADS = 8
HEAD_PAD = 128
QK_NOPE = 64
QK_ROPE = 32
ROPE_HALF = 16
ROPE_BASE = 10000.0
ATTN_SCALE = (QK_NOPE + QK_ROPE) ** -0.5
LRU_C = 8.0
LRU_W = 512
CHUNK = 128
SGU_GROUPS = 8
D_FF = 2816
FF_BLOCKS = 2

ADAM_LR, ADAM_B1, ADAM_B2, ADAM_EPS, ADAM_WD, ADAM_STEP = 0.001, 0.9, 0.999, 1e-08, 0.01, 10

WEIGHTS = ['ab_norm', 'ab_w_in', 'ab_q_norm', 'ab_w_q_b', 'ab_kv_norm', 'ab_w_kv_b', 'ab_conv_w', 'ab_conv_b',
           'ab_w_rg_a', 'ab_b_rg_a', 'ab_w_rg_x', 'ab_b_rg_x', 'ab_lambda', 'ab_w_out', 'c_norm', 'c_w_in', 'c_ln_g',
           'c_ln_b', 'c_w_s', 'c_b_s', 'c_w_out', 'ffn_norm', 'ffn_w_gate', 'ffn_w_up', 'ffn_conv_w', 'ffn_conv_b',
           'ffn_w_down', 'final_norm']
SHARD_AXIS = {'ab_w_in': 2, 'ab_w_q_b': 2, 'ab_w_kv_b': 2, 'ab_conv_w': 2, 'ab_w_out': 1, 'c_norm': 1, 'c_w_in': 2,
              'c_ln_g': 1, 'c_ln_b': 1, 'c_w_out': 1, 'ffn_w_gate': 2, 'ffn_w_up': 2, 'ffn_conv_w': 2, 'ffn_w_down': 1}
MATRICES = ['ab_w_in', 'ab_w_q_b', 'ab_w_kv_b', 'ab_w_out', 'c_w_in', 'c_w_out', 'ffn_w_gate', 'ffn_w_up', 'ffn_w_down']
BIG = ['ab_w_in', 'c_w_in', 'ffn_w_gate', 'ffn_w_up', 'ab_w_out', 'c_w_out', 'ffn_w_down']
REPLICATED = [n for n in WEIGHTS if n not in SHARD_AXIS]
SMALL_SHARDED = [n for n in WEIGHTS if n in SHARD_AXIS and n not in BIG]


def _bf(x):
    return x.astype(BF16)


def _nn(a, b):
    return lax.dot_general(_bf(a), _bf(b), (((1,), (0,)), ((), ())), preferred_element_type=F32)


def _nt(a, b):
    return lax.dot_general(_bf(a), _bf(b), (((1,), (1,)), ((), ())), preferred_element_type=F32)


def _tn(a, b):
    return lax.dot_general(_bf(a), _bf(b), (((0,), (0,)), ((), ())), preferred_element_type=F32)


def _rms(x, g):
    return x * lax.rsqrt(jnp.mean(x * x, axis=-1, keepdims=True) + NORM_EPS) * g


def _layer_norm(x, g, b):
    xc = x - jnp.mean(x, axis=-1, keepdims=True)
    return xc * lax.rsqrt(jnp.mean(xc * xc, axis=-1, keepdims=True) + NORM_EPS) * g + b


def _gelu(x):
    return jax.nn.gelu(x)


STRIP = 16
STRIP_LANES = 384
GELU_C = math.sqrt(2.0 / math.pi)
GELU_A = 0.044715


def _gelu_and_grad(x):
    x2 = x * x
    t = jnp.tanh(x * (GELU_C + (GELU_C * GELU_A) * x2))
    half_x = 0.5 * x
    one_plus_t = 1.0 + t
    return half_x * one_plus_t, 0.5 * one_plus_t + half_x * (1.0 - t * t) * (GELU_C + (3.0 * GELU_C * GELU_A) * x2)


def _colsum(x):
    return jnp.sum(x, axis=0, keepdims=True)


def _softplus(x):
    return jnp.maximum(x, 0.0) + jnp.log1p(jnp.exp(-jnp.abs(x)))


@jax.custom_vjp
def _decay(x):
    a = jnp.exp(x)
    y = 2.0 * x
    series = -y * (1.0 + y * (1 / 2 + y * (1 / 6 + y * (1 / 24 + y * (1 / 120 + y * (1 / 720))))))
    return a, jnp.where(y < -0.3, 1.0 - a * a, series)


def _decay_fwd(x):
    a, gap = _decay(x)
    return (a, gap), a


def _decay_bwd(a, cts):
    return (a * (cts[0] - 2.0 * a * cts[1]),)


_decay.defvjp(_decay_fwd, _decay_bwd)


def _accumulate(ref, val, first):
    @pl.when(first)
    def _():
        ref[...] = val

    @pl.when(jnp.logical_not(first))
    def _():
        ref[...] += val


def _params(n_axes=1):
    return pltpu.CompilerParams(dimension_semantics=("arbitrary",) * n_axes, vmem_limit_bytes=VMEM_LIMIT)


def _row(tm, n):
    return pl.BlockSpec((tm, n), lambda i: (i, 0))


def _const(shape):
    nd = len(shape)
    return pl.BlockSpec(shape, lambda i: (0,) * nd, pipeline_mode=pl.Buffered(1))


def _prev_halo(tm, n):
    return pl.BlockSpec((HALO, n), lambda i: (jnp.maximum(i * (tm // HALO) - 1, 0), 0))


def _next_halo(tm, n, n_tiles):
    last = n_tiles * (tm // HALO) - 1
    return pl.BlockSpec((HALO, n), lambda i: (jnp.minimum((i + 1) * (tm // HALO), last), 0))


def _sds(shape, dtype=F32):
    return jax.ShapeDtypeStruct(shape, dtype)


def _rope_tables(posb):
    lane = lax.broadcasted_iota(jnp.int32, posb.shape, 1)
    in_rope = jnp.logical_and(lane >= QK_NOPE, lane < QK_NOPE + QK_ROPE)
    j = (lane & (ROPE_HALF - 1)).astype(F32)
    inv_freq = jnp.exp((-math.log(ROPE_BASE)) * j / ROPE_HALF)
    ang = posb * inv_freq
    return jnp.where(in_rope, jnp.cos(ang), 1.0), jnp.where(in_rope, jnp.sin(ang), 0.0)


def _rot(q):
    n = q.shape[1]
    lane = lax.broadcasted_iota(jnp.int32, q.shape, 1) & (HEAD_PAD - 1)
    first_half = jnp.where(lane >= QK_NOPE, -pltpu.roll(q, n - ROPE_HALF, 1), 0.0)
    second_half = jnp.where(lane < QK_NOPE + QK_ROPE, pltpu.roll(q, ROPE_HALF, 1), 0.0)
    return jnp.where(lane < QK_NOPE + ROPE_HALF, first_half, second_half)


def _rope(q, cos_t, sin_t):
    return q * cos_t + _rot(q) * sin_t


def _rope_transpose(dq, cos_t, sin_t):
    return dq * cos_t - _rot(dq * sin_t)


def _tile_heads(t):
    return jnp.concatenate([t] * HEADS, axis=1)


Q_LORA, KV_LORA = 256, 128
Z_KPE = Q_LORA + KV_LORA
Z_LRU = Z_KPE + HEAD_PAD
Z_GATE = Z_LRU + LRU_W
Z_WIDTH = Z_GATE + LRU_W


def _ab_in_fwd(x, posb, w, tm):
    t, d = x.shape

    def body(x_ref, pos_ref, gn_ref, win_ref, qn_ref, wq_ref, kvn_ref, wk_ref, wv_ref, q_out, k_out, v_out, xl_out, gate_out):
        hn = _rms(x_ref[...], gn_ref[...])
        z = _nn(hn, win_ref[...])
        cqn = _rms(z[:, :Q_LORA], qn_ref[...])
        kvn = _rms(z[:, Q_LORA:Z_KPE], kvn_ref[...])
        cos_t, sin_t = _rope_tables(pos_ref[...])
        q_out[...] = _rope(_nn(cqn, wq_ref[...]), _tile_heads(cos_t), _tile_heads(sin_t))
        kpe = _rope(z[:, Z_KPE:Z_LRU], cos_t, sin_t)
        k_out[...] = _nn(kvn, wk_ref[...]) + _tile_heads(kpe)
        v_out[...] = _nn(kvn, wv_ref[...])
        xl_out[...] = z[:, Z_LRU:Z_GATE]
        gate_out[...] = z[:, Z_GATE:]

    hp = HEADS * HEAD_PAD
    return pl.pallas_call(
        body, name="ab_in_fwd", grid=(t // tm,),
        in_specs=[_row(tm, d), _row(tm, LANES), _const((1, d)), _const((d, Z_WIDTH)), _const((1, Q_LORA)), _const((Q_LORA, hp)),
                  _const((1, KV_LORA)), _const((KV_LORA, hp)), _const((KV_LORA, hp))],
        out_specs=[_row(tm, hp), _row(tm, hp), _row(tm, hp), _row(tm, LRU_W), _row(tm, LRU_W)],
        out_shape=[_sds((t, hp)), _sds((t, hp)), _sds((t, hp)), _sds((t, LRU_W)), _sds((t, LRU_W))],
        compiler_params=_params(),
    )(x, posb, w['ab_norm'], w['W_in'], w['ab_q_norm'], w['Wq'], w['ab_kv_norm'], w['Wk'], w['Wv'])


def _ab_in_bwd(x, posb, w, dq, dk, dv, dxl, dgate, dres, tm):
    t, d = x.shape
    hp = HEADS * HEAD_PAD

    def body(x_ref, pos_ref, gn_ref, win_ref, qn_ref, wq_ref, kvn_ref, wk_ref, wv_ref, dq_ref, dk_ref, dv_ref, dxl_ref, dgate_ref,
             dres_ref, dx_out, dgn_out, dwin_out, dqn_out, dwq_out, dkvn_out, dwk_out, dwv_out):
        first = pl.program_id(0) == 0
        hn, vjp_in = jax.vjp(_rms, x_ref[...], gn_ref[...])
        z = _nn(hn, win_ref[...])
        cqn, vjp_q = jax.vjp(_rms, z[:, :Q_LORA], qn_ref[...])
        kvn, vjp_kv = jax.vjp(_rms, z[:, Q_LORA:Z_KPE], kvn_ref[...])
        cos_t, sin_t = _rope_tables(pos_ref[...])
        dq0 = _rope_transpose(dq_ref[...], _tile_heads(cos_t), _tile_heads(sin_t))
        dk0 = dk_ref[...]
        dv0 = dv_ref[...]
        dkpe = dk0[:, :HEAD_PAD]
        for h in range(1, HEADS):
            dkpe = dkpe + dk0[:, h * HEAD_PAD:(h + 1) * HEAD_PAD]
        dkpe = _rope_transpose(dkpe, cos_t, sin_t)
        _accumulate(dwq_out, _tn(cqn, dq0), first)
        _accumulate(dwk_out, _tn(kvn, dk0), first)
        _accumulate(dwv_out, _tn(kvn, dv0), first)
        dcq, dqn = vjp_q(_nt(dq0, wq_ref[...]))
        dckv, dkvn = vjp_kv(_nt(dk0, wk_ref[...]) + _nt(dv0, wv_ref[...]))
        _accumulate(dqn_out, dqn, first)
        _accumulate(dkvn_out, dkvn, first)
        dz = jnp.concatenate([dcq, dckv, dkpe, dxl_ref[...], dgate_ref[...]], axis=1)
        _accumulate(dwin_out, _tn(hn, dz), first)
        dx, dgn = vjp_in(_nt(dz, win_ref[...]))
        _accumulate(dgn_out, dgn, first)
        dx_out[...] = dx + dres_ref[...]

    return pl.pallas_call(
        body, name="ab_in_bwd", grid=(t // tm,),
        in_specs=[_row(tm, d), _row(tm, LANES), _const((1, d)), _const((d, Z_WIDTH)), _const((1, Q_LORA)), _const((Q_LORA, hp)),
                  _const((1, KV_LORA)), _const((KV_LORA, hp)), _const((KV_LORA, hp)),
                  _row(tm, hp), _row(tm, hp), _row(tm, hp), _row(tm, LRU_W), _row(tm, LRU_W), _row(tm, d)],
        out_specs=[_row(tm, d), _const((1, d)), _const((d, Z_WIDTH)), _const((1, Q_LORA)), _const((Q_LORA, hp)),
                   _const((1, KV_LORA)), _const((KV_LORA, hp)), _const((KV_LORA, hp))],
        out_shape=[_sds((t, d)), _sds((1, d)), _sds((d, Z_WIDTH)), _sds((1, Q_LORA)), _sds((Q_LORA, hp)),
                   _sds((1, KV_LORA)), _sds((KV_LORA, hp)), _sds((KV_LORA, hp))],
        compiler_params=_params(),
    )(x, posb, w['ab_norm'], w['W_in'], w['ab_q_norm'], w['Wq'], w['ab_kv_norm'], w['Wk'], w['Wv'], dq, dk, dv, dxl, dgate, dres)


def _attn_probs(q_blk, k_ext, i, tq):
    ext = k_ext.shape[0]
    s = lax.dot_general(q_blk, k_ext, (((1,), (1,)), ((), ())), preferred_element_type=F32) * ATTN_SCALE
    causal = lax.broadcasted_iota(jnp.int32, (tq, tq), 1) <= lax.broadcasted_iota(jnp.int32, (tq, tq), 0)
    diag = jnp.where(causal, s[:, ext - tq:], -1e30)
    s = diag if ext == tq else jnp.concatenate([s[:, :ext - tq], diag], axis=1)
    p = jnp.exp(s - jnp.max(s, axis=1, keepdims=True))
    return p / jnp.sum(p, axis=1, keepdims=True)


def _attn_fwd(q, k, v, tq):
    b, s, hp = q.shape
    blk = pl.BlockSpec((1, s, HEAD_PAD), lambda bi, h: (bi, 0, h))

    def body(q_ref, k_ref, v_ref, o_ref):
        kb = _bf(k_ref[0])
        vb = _bf(v_ref[0])
        for i in range(s // tq):
            ext = (i + 1) * tq
            p = _attn_probs(_bf(q_ref[0, i * tq:ext, :]), kb[:ext], i, tq)
            o_ref[0, i * tq:ext, :] = lax.dot_general(_bf(p), vb[:ext], (((1,), (0,)), ((), ())), preferred_element_type=F32)

    return pl.pallas_call(body, name="attn_fwd", grid=(b, HEADS), in_specs=[blk, blk, blk], out_specs=blk,
                          out_shape=_sds((b, s, hp)), compiler_params=_params(2))(q, k, v)


def _attn_bwd(q, k, v, do, tq):
    b, s, hp = q.shape
    blk = pl.BlockSpec((1, s, HEAD_PAD), lambda bi, h: (bi, 0, h))

    def body(q_ref, k_ref, v_ref, do_ref, dq_ref, dk_ref, dv_ref):
        kb = _bf(k_ref[0])
        vb = _bf(v_ref[0])
        dk_ref[...] = jnp.zeros_like(dk_ref)
        dv_ref[...] = jnp.zeros_like(dv_ref)
        for i in range(s // tq):
            ext = (i + 1) * tq
            qb = _bf(q_ref[0, i * tq:ext, :])
            dob = _bf(do_ref[0, i * tq:ext, :])
            p = _attn_probs(qb, kb[:ext], i, tq)
            dv_ref[0, :ext, :] += lax.dot_general(_bf(p), dob, (((0,), (0,)), ((), ())), preferred_element_type=F32)
            dp = lax.dot_general(dob, vb[:ext], (((1,), (1,)), ((), ())), preferred_element_type=F32)
            ds = _bf(p * (dp - jnp.sum(p * dp, axis=1, keepdims=True)) * ATTN_SCALE)
            dq_ref[0, i * tq:ext, :] = lax.dot_general(ds, kb[:ext], (((1,), (0,)), ((), ())), preferred_element_type=F32)
            dk_ref[0, :ext, :] += lax.dot_general(ds, qb, (((0,), (0,)), ((), ())), preferred_element_type=F32)

    return pl.pallas_call(body, name="attn_bwd", grid=(b, HEADS), in_specs=[blk, blk, blk, blk], out_specs=[blk, blk, blk],
                          out_shape=[_sds((b, s, hp))] * 3, compiler_params=_params(2))(q, k, v, do)


LRU_CONV = 4


def _lru_point(pre_a, pre_x, xc, lam):
    r = jax.nn.sigmoid(pre_a)
    i = jax.nn.sigmoid(pre_x)
    a, gap = _decay(-LRU_C * r * _softplus(-lam))
    return a, jnp.sqrt(gap) * (i * xc)


def _causal_conv(pad_ref, x, halo, first_in_seq, w, taps):
    tm = x.shape[0]
    pad_ref[:HALO, :] = jnp.where(first_in_seq, 0.0, halo)
    pad_ref[HALO:, :] = x
    y = w[taps - 1:taps, :] * x
    for k in range(taps - 1):
        off = HALO - (taps - 1) + k
        y = y + w[k:k + 1, :] * pad_ref[off:off + tm, :]
    return y


def _conv_taps(pad_ref, r, cols, taps):
    blocks = [pad_ref[r + j * HALO:r + (j + 1) * HALO, cols] for j in range(1 + STRIP // HALO)]
    sub = lax.broadcasted_iota(jnp.int32, blocks[0].shape, 0)
    out = []
    for k in range(taps - 1):
        s = taps - 1 - k
        rolled = [pltpu.roll(b, s, 0) for b in blocks]
        out.append(jnp.concatenate([jnp.where(sub < s, rolled[j], rolled[j + 1]) for j in range(STRIP // HALO)], axis=0))
    out.append(jnp.concatenate(blocks[1:], axis=0))
    return out


def _causal_conv_wgrad(pad_ref, dy, taps):
    tm = dy.shape[0]
    return jnp.concatenate([_colsum(dy * pad_ref[HALO - (taps - 1) + k:HALO - (taps - 1) + k + tm, :]) for k in range(taps)], axis=0)


def _causal_conv_transpose(pad_ref, dy, halo_next, last_in_seq, w, taps):
    tm = dy.shape[0]
    pad_ref[:tm, :] = dy
    pad_ref[tm:, :] = jnp.where(last_in_seq, 0.0, halo_next)
    dx = w[taps - 1:taps, :] * dy
    for k in range(taps - 1):
        off = (taps - 1) - k
        dx = dx + w[k:k + 1, :] * pad_ref[off:off + tm, :]
    return dx


def _lru_fwd(xl, gate, w, ts, seq):
    t, n = xl.shape
    tiles_per_seq = seq // ts

    def body(xl_ref, halo_ref, gate_ref, cw_ref, cb_ref, wa_ref, ba_ref, wx_ref, bx_ref, lam_ref, y_out, h_out, pad_ref, a_ref, b_ref, carry_ref):
        first_in_seq = pl.program_id(0) % tiles_per_seq == 0
        xc = _causal_conv(pad_ref, xl_ref[...], halo_ref[...], first_in_seq, cw_ref[...], LRU_CONV) + cb_ref[...]
        a, bx = _lru_point(_nn(xc, wa_ref[...]) + ba_ref[...], _nn(xc, wx_ref[...]) + bx_ref[...], xc, lam_ref[...])
        a_ref[...] = a
        b_ref[...] = bx

        @pl.when(first_in_seq)
        def _():
            carry_ref[...] = jnp.zeros_like(carry_ref)

        def step(r, h):
            h = a_ref[pl.ds(r, 1), :] * h + b_ref[pl.ds(r, 1), :]
            h_out[pl.ds(r, 1), :] = h
            return h

        carry_ref[...] = lax.fori_loop(0, ts, step, carry_ref[...], unroll=8)
        y_out[...] = h_out[...] * _gelu(gate_ref[...])

    return pl.pallas_call(
        body, name="lru_fwd", grid=(t // ts,),
        in_specs=[_row(ts, n), _prev_halo(ts, n), _row(ts, n), _const((LRU_CONV, n)), _const((1, n)), _const((n, n)), _const((1, n)),
                  _const((n, n)), _const((1, n)), _const((1, n))],
        out_specs=[_row(ts, n), _row(ts, n)], out_shape=[_sds((t, n)), _sds((t, n))],
        scratch_shapes=[pltpu.VMEM((HALO + ts, n), F32), pltpu.VMEM((ts, n), F32), pltpu.VMEM((ts, n), F32), pltpu.VMEM((1, n), F32)],
        compiler_params=_params(),
    )(xl, xl, gate, w['ab_conv_w'], w['ab_conv_b'], w['Wa'], w['ab_b_rg_a'], w['Wx'], w['ab_b_rg_x'], w['ab_lambda'])


def _lru_bwd(xl, gate, hs, dy, w, ts, seq):
    t, n = xl.shape
    tiles_per_seq = seq // ts
    n_tiles = t // ts

    def rev(i):
        return n_tiles - 1 - i

    row = pl.BlockSpec((ts, n), lambda i: (rev(i), 0))
    prev = pl.BlockSpec((HALO, n), lambda i: (jnp.maximum(rev(i) * (ts // HALO) - 1, 0), 0))
    acc = lambda shape: pl.BlockSpec(shape, lambda i: (0,) * len(shape))

    def body(xl_ref, xhalo_ref, gate_ref, h_ref, hhalo_ref, dy_ref, cw_ref, cb_ref, wa_ref, ba_ref, wx_ref, bx_ref, lam_ref,
             dxl_out, dgate_out, dcw_out, dcb_out, dwa_out, dba_out, dwx_out, dbx_out, dlam_out,
             pad_ref, padh_ref, padd_ref, a_ref, g_ref, carry_ref, dhalo_ref):
        step_id = pl.program_id(0)
        first = step_id == 0
        tile = rev(step_id)
        first_in_seq = tile % tiles_per_seq == 0
        last_in_seq = tile % tiles_per_seq == tiles_per_seq - 1
        cw = cw_ref[...]
        xc = _causal_conv(pad_ref, xl_ref[...], xhalo_ref[...], first_in_seq, cw, LRU_CONV) + cb_ref[...]
        pre_a = _nn(xc, wa_ref[...]) + ba_ref[...]
        pre_x = _nn(xc, wx_ref[...]) + bx_ref[...]
        (a, _), vjp_point = jax.vjp(_lru_point, pre_a, pre_x, xc, lam_ref[...])
        h = h_ref[...]
        _, vjp_out = jax.vjp(lambda h_, g_: h_ * _gelu(g_), h, gate_ref[...])
        dh, dgate = vjp_out(dy_ref[...])
        dgate_out[...] = dgate
        a_ref[...] = a
        g_ref[...] = dh

        @pl.when(last_in_seq)
        def _():
            carry_ref[...] = jnp.zeros_like(carry_ref)

        def step(j, c):
            r = ts - 1 - j
            g = g_ref[pl.ds(r, 1), :] + c
            g_ref[pl.ds(r, 1), :] = g
            return a_ref[pl.ds(r, 1), :] * g

        carry_ref[...] = lax.fori_loop(0, ts, step, carry_ref[...], unroll=8)
        g = g_ref[...]
        padh_ref[:HALO, :] = jnp.where(first_in_seq, 0.0, hhalo_ref[...])
        padh_ref[HALO:, :] = h
        dpre_a, dpre_x, dxc, dlam = vjp_point((g * padh_ref[HALO - 1:HALO - 1 + ts, :], g))
        dxc = dxc + _nt(dpre_a, wa_ref[...]) + _nt(dpre_x, wx_ref[...])
        _accumulate(dwa_out, _tn(xc, dpre_a), first)
        _accumulate(dwx_out, _tn(xc, dpre_x), first)
        _accumulate(dba_out, _colsum(dpre_a), first)
        _accumulate(dbx_out, _colsum(dpre_x), first)
        _accumulate(dlam_out, dlam, first)
        _accumulate(dcb_out, _colsum(dxc), first)
        _accumulate(dcw_out, _causal_conv_wgrad(pad_ref, dxc, LRU_CONV), first)
        dxl_out[...] = _causal_conv_transpose(padd_ref, dxc, dhalo_ref[...], last_in_seq, cw, LRU_CONV)
        dhalo_ref[...] = dxc[:HALO, :]

    return pl.pallas_call(
        body, name="lru_bwd", grid=(n_tiles,),
        in_specs=[row, prev, row, row, prev, row, _const((LRU_CONV, n)), _const((1, n)), _const((n, n)), _const((1, n)),
                  _const((n, n)), _const((1, n)), _const((1, n))],
        out_specs=[row, row, acc((LRU_CONV, n)), acc((1, n)), acc((n, n)), acc((1, n)), acc((n, n)), acc((1, n)), acc((1, n))],
        out_shape=[_sds((t, n)), _sds((t, n)), _sds((LRU_CONV, n)), _sds((1, n)), _sds((n, n)), _sds((1, n)), _sds((n, n)),
                   _sds((1, n)), _sds((1, n))],
        scratch_shapes=[pltpu.VMEM((HALO + ts, n), F32), pltpu.VMEM((HALO + ts, n), F32), pltpu.VMEM((ts + HALO, n), F32),
                        pltpu.VMEM((ts, n), F32), pltpu.VMEM((ts, n), F32), pltpu.VMEM((1, n), F32), pltpu.VMEM((HALO, n), F32)],
        compiler_params=_params(),
    )(xl, xl, gate, hs, hs, dy, w['ab_conv_w'], w['ab_conv_b'], w['Wa'], w['ab_b_rg_a'], w['Wx'], w['ab_b_rg_x'], w['ab_lambda'])


def _ab_out_fwd(x, o, y, w, tm):
    t, d = x.shape
    hp = o.shape[1]

    def body(x_ref, o_ref, y_ref, wa_ref, wb_ref, h_out):
        h_out[...] = x_ref[...] + _nn(o_ref[...], wa_ref[...]) + _nn(y_ref[...], wb_ref[...])

    return pl.pallas_call(body, name="ab_out_fwd", grid=(t // tm,),
                          in_specs=[_row(tm, d), _row(tm, hp), _row(tm, LRU_W), _const((hp, d)), _const((LRU_W, d))],
                          out_specs=_row(tm, d), out_shape=_sds((t, d)), compiler_params=_params())(x, o, y, w['Wo_a'], w['Wo_b'])


def _ab_out_bwd(o, y, dh, w, tm):
    t, d = dh.shape
    hp = o.shape[1]

    def body(o_ref, y_ref, dh_ref, wa_ref, wb_ref, do_out, dy_out, dwa_out, dwb_out):
        first = pl.program_id(0) == 0
        dh_t = dh_ref[...]
        do_out[...] = _nt(dh_t, wa_ref[...])
        dy_out[...] = _nt(dh_t, wb_ref[...])
        _accumulate(dwa_out, _tn(o_ref[...], dh_t), first)
        _accumulate(dwb_out, _tn(y_ref[...], dh_t), first)

    return pl.pallas_call(body, name="ab_out_bwd", grid=(t // tm,),
                          in_specs=[_row(tm, hp), _row(tm, LRU_W), _row(tm, d), _const((hp, d)), _const((LRU_W, d))],
                          out_specs=[_row(tm, hp), _row(tm, LRU_W), _const((hp, d)), _const((LRU_W, d))],
                          out_shape=[_sds((t, hp)), _sds((t, LRU_W)), _sds((hp, d)), _sds((LRU_W, d))],
                          compiler_params=_params())(o, y, dh, w['Wo_a'], w['Wo_b'])


FFN_CONV = 3


def _ffn_a_fwd(h, norm, wg, wu, tm):
    t, d = h.shape
    fb = D_FF // FF_BLOCKS

    def body(h_ref, gn_ref, wg_ref, wu_ref, g_out, u_out, hn_out):
        hn = _bf(_rms(h_ref[...], gn_ref[...]))
        hn_out[0] = hn
        g_out[...] = _nt(hn, wg_ref[...])
        u_out[...] = _nt(hn, wu_ref[...])

    wspec = pl.BlockSpec((fb, d), lambda f, i: (f, 0))
    ospec = pl.BlockSpec((tm, fb), lambda f, i: (i, f))
    return pl.pallas_call(
        body, name="ffn_a_fwd", grid=(FF_BLOCKS, t // tm),
        in_specs=[pl.BlockSpec((tm, d), lambda f, i: (i, 0)), pl.BlockSpec((1, d), lambda f, i: (0, 0)), wspec, wspec],
        out_specs=[ospec, ospec, pl.BlockSpec((1, tm, d), lambda f, i: (f, i, 0))],
        out_shape=[_sds((t, D_FF)), _sds((t, D_FF)), _sds((FF_BLOCKS, t, d), BF16)], compiler_params=_params(2))(h, norm, wg, wu)


def _ffn_b_fwd(g, u, h, cw, cb, wd, tm, seq):
    t, d = h.shape
    tiles_per_seq = seq // tm

    def body(g_ref, halo_ref, u_ref, h_ref, cw_ref, cb_ref, wd_ref, h_out, pad_ref, act_ref):
        pad_ref[:HALO, :] = jnp.where(pl.program_id(0) % tiles_per_seq == 0, 0.0, halo_ref[...])
        pad_ref[HALO:, :] = g_ref[...]
        cw = cw_ref[...]
        cb = cb_ref[...]
        for c0 in range(0, D_FF, STRIP_LANES):
            cols = slice(c0, min(c0 + STRIP_LANES, D_FF))
            for r in range(0, tm, STRIP):
                taps = _conv_taps(pad_ref, r, cols, FFN_CONV)
                gc = cb[:, cols] + cw[0:1, cols] * taps[0] + cw[1:2, cols] * taps[1] + cw[2:3, cols] * taps[2]
                act_ref[r:r + STRIP, cols] = _bf(_gelu(gc) * u_ref[r:r + STRIP, cols])
        h_out[...] = h_ref[...] + _nn(act_ref[...], wd_ref[...])

    return pl.pallas_call(body, name="ffn_b_fwd", grid=(t // tm,),
                          in_specs=[_row(tm, D_FF), _prev_halo(tm, D_FF), _row(tm, D_FF), _row(tm, d), _const((FFN_CONV, D_FF)),
                                    _const((1, D_FF)), _const((D_FF, d))],
                          out_specs=_row(tm, d), out_shape=_sds((t, d)),
                          scratch_shapes=[pltpu.VMEM((HALO + tm, D_FF), F32), pltpu.VMEM((tm, D_FF), BF16)],
                          compiler_params=_params())(g, g, u, h, cw, cb, wd)


def _ffn_b_bwd(g, u, dout, cw, cb, wd, tm, seq):
    t, d = dout.shape
    fb = D_FF // FF_BLOCKS
    tiles_per_seq = seq // tm

    def body(g_ref, halo_ref, u_ref, dout_ref, cw_ref, cb_ref, wd_ref, dgc_out, du_out, dwd_out, dcw_out, dcb_out,
             pad_ref, dact_ref, act_ref, acc_ref):
        i = pl.program_id(1)
        first = i == 0
        pad_ref[:HALO, :] = jnp.where(i % tiles_per_seq == 0, 0.0, halo_ref[...])
        pad_ref[HALO:, :] = g_ref[...]
        dout_b = _bf(dout_ref[...])
        dact_ref[...] = _nt(dout_b, wd_ref[...])
        cw = cw_ref[...]
        cb = cb_ref[...]
        fold = lambda a: a[:HALO] + a[HALO:]
        for c0 in range(0, fb, STRIP_LANES):
            cols = slice(c0, min(c0 + STRIP_LANES, fb))
            sums = [jnp.zeros((HALO, cols.stop - c0), F32) for _ in range(1 + FFN_CONV)]
            for r in range(0, tm, STRIP):
                rows = slice(r, r + STRIP)
                taps = _conv_taps(pad_ref, r, cols, FFN_CONV)
                gelu, dgelu = _gelu_and_grad(cb[:, cols] + cw[0:1, cols] * taps[0] + cw[1:2, cols] * taps[1] + cw[2:3, cols] * taps[2])
                u = u_ref[rows, cols]
                dact = dact_ref[rows, cols]
                act_ref[rows, cols] = _bf(gelu * u)
                du_out[rows, cols] = _bf(dact * gelu)
                dgc = dact * u * dgelu
                dgc_out[rows, cols] = dgc
                sums = [sums[0] + fold(dgc)] + [sums[1 + k] + fold(dgc * taps[k]) for k in range(FFN_CONV)]
            for k in range(1 + FFN_CONV):
                acc_ref[k, :, cols] = sums[k]
        _accumulate(dwd_out, _tn(act_ref[...], dout_b), first)
        _accumulate(dcb_out, _colsum(acc_ref[0]), first)
        _accumulate(dcw_out, jnp.concatenate([_colsum(acc_ref[1 + k]) for k in range(FFN_CONV)], axis=0), first)

    blk = pl.BlockSpec((tm, fb), lambda f, i: (i, f))
    halo = pl.BlockSpec((HALO, fb), lambda f, i: (jnp.maximum(i * (tm // HALO) - 1, 0), f))
    wd_blk = pl.BlockSpec((fb, d), lambda f, i: (f, 0), pipeline_mode=pl.Buffered(1))
    return pl.pallas_call(
        body, name="ffn_b_bwd", grid=(FF_BLOCKS, t // tm),
        in_specs=[blk, halo, blk, pl.BlockSpec((tm, d), lambda f, i: (i, 0)), pl.BlockSpec((FFN_CONV, fb), lambda f, i: (0, f)),
                  pl.BlockSpec((1, fb), lambda f, i: (0, f)), wd_blk],
        out_specs=[blk, blk, wd_blk, pl.BlockSpec((FFN_CONV, fb), lambda f, i: (0, f)),
                   pl.BlockSpec((1, fb), lambda f, i: (0, f))],
        out_shape=[_sds((t, D_FF)), _sds((t, D_FF), BF16), _sds((D_FF, d)), _sds((FFN_CONV, D_FF)), _sds((1, D_FF))],
        scratch_shapes=[pltpu.VMEM((HALO + tm, fb), F32), pltpu.VMEM((tm, fb), F32), pltpu.VMEM((tm, fb), BF16),
                        pltpu.VMEM((1 + FFN_CONV, HALO, fb), F32)],
        compiler_params=_params(2))(g, g, u, dout, cw, cb, wd)


def _ffn_a_dgrad(h, norm, dgc, du, dres, cw, wg, wu, tm, seq):
    t, d = h.shape
    tiles_per_seq = seq // tm
    n_tiles = t // tm

    def body(h_ref, gn_ref, dgc_ref, halo_ref, du_ref, dres_ref, cw_ref, wg_ref, wu_ref, dh_out, dg_out, dgn_out, pad_ref):
        i = pl.program_id(0)
        last_in_seq = i % tiles_per_seq == tiles_per_seq - 1
        dg = _bf(_causal_conv_transpose(pad_ref, dgc_ref[...], halo_ref[...], last_in_seq, cw_ref[...], FFN_CONV))
        dg_out[...] = dg
        _, vjp_norm = jax.vjp(_rms, h_ref[...], gn_ref[...])
        dh, dgn = vjp_norm(_nn(dg, wg_ref[...]) + _nn(du_ref[...], wu_ref[...]))
        dh_out[...] = dh + dres_ref[...]
        _accumulate(dgn_out, dgn, i == 0)

    return pl.pallas_call(
        body, name="ffn_a_dgrad", grid=(n_tiles,),
        in_specs=[_row(tm, d), _const((1, d)), _row(tm, D_FF), _next_halo(tm, D_FF, n_tiles), _row(tm, D_FF), _row(tm, d),
                  _const((FFN_CONV, D_FF)), _const((D_FF, d)), _const((D_FF, d))],
        out_specs=[_row(tm, d), _row(tm, D_FF), _const((1, d))], out_shape=[_sds((t, d)), _sds((t, D_FF), BF16), _sds((1, d))],
        scratch_shapes=[pltpu.VMEM((tm + HALO, D_FF), F32)], compiler_params=_params())(h, norm, dgc, dgc, du, dres, cw, wg, wu)


def _ffn_a_wgrad(hn, dg, du, tm):
    _, t, d = hn.shape
    fb = D_FF // FF_BLOCKS

    def body(hn_ref, dg_ref, du_ref, dwg_out, dwu_out):
        first = pl.program_id(1) == 0
        hn_t = hn_ref[0]
        _accumulate(dwg_out, _tn(dg_ref[...], hn_t), first)
        _accumulate(dwu_out, _tn(du_ref[...], hn_t), first)

    blk = pl.BlockSpec((tm, fb), lambda f, i: (i, f))
    wspec = pl.BlockSpec((fb, d), lambda f, i: (f, 0), pipeline_mode=pl.Buffered(1))
    return pl.pallas_call(body, name="ffn_a_wgrad", grid=(FF_BLOCKS, t // tm),
                          in_specs=[pl.BlockSpec((1, tm, d), lambda f, i: (0, i, 0)), blk, blk],
                          out_specs=[wspec, wspec], out_shape=[_sds((D_FF, d)), _sds((D_FF, d))],
                          compiler_params=_params(2))(hn, dg, du)


def _sgu_mix(vn, ws_ref, bst):
    tril = lax.broadcasted_iota(jnp.int32, (CHUNK, CHUNK), 0) >= lax.broadcasted_iota(jnp.int32, (CHUNK, CHUNK), 1)
    wms = [jnp.where(tril, ws_ref[g], 0.0) for g in range(SGU_GROUPS)]
    chunks = []
    for n in range(vn.shape[0] // CHUNK):
        vc = vn[n * CHUNK:(n + 1) * CHUNK, :]
        chunks.append(jnp.concatenate(
            [_nn(wms[g], vc[:, g * CHUNK:(g + 1) * CHUNK]) + bst[:, g:g + 1] for g in range(SGU_GROUPS)], axis=1))
    return jnp.concatenate(chunks, axis=0)


def _sgu_fwd(h, w, tm):
    t, d = h.shape

    def body(h_ref, cn_ref, win_ref, lg_ref, lb_ref, ws_ref, bst_ref, wout_ref, h_out):
        h_t = h_ref[...]
        z = _gelu(_nn(_rms(h_t, cn_ref[...]), win_ref[...]))
        vn = _layer_norm(z[:, d:], lg_ref[...], lb_ref[...])
        s = _sgu_mix(vn, ws_ref, bst_ref[...])
        h_out[...] = h_t + _nn(z[:, :d] * s, wout_ref[...])

    return pl.pallas_call(
        body, name="sgu_fwd", grid=(t // tm,),
        in_specs=[_row(tm, d), _const((1, d)), _const((d, 2 * d)), _const((1, d)), _const((1, d)), _const((SGU_GROUPS, CHUNK, CHUNK)),
                  _const((CHUNK, LANES)), _const((d, d))],
        out_specs=_row(tm, d), out_shape=_sds((t, d)), compiler_params=_params(),
    )(h, w['c_norm'], w['c_w_in'], w['c_ln_g'], w['c_ln_b'], w['c_w_s'], w['bsT'], w['c_w_out'])


def _sgu_bwd(h, dout, w, tm):
    t, d = h.shape

    def body(h_ref, dout_ref, cn_ref, win_ref, lg_ref, lb_ref, ws_ref, bst_ref, wout_ref,
             dh_out, dcn_out, dwin_out, dlg_out, dlb_out, dws_out, dbst_out, dwout_out):
        first = pl.program_id(0) == 0
        hn, vjp_norm = jax.vjp(_rms, h_ref[...], cn_ref[...])
        zpre = _nn(hn, win_ref[...])
        u, vjp_u = jax.vjp(_gelu, zpre[:, :d])
        vn, vjp_v = jax.vjp(lambda zp, lg, lb: _layer_norm(_gelu(zp), lg, lb), zpre[:, d:], lg_ref[...], lb_ref[...])
        s = _sgu_mix(vn, ws_ref, bst_ref[...])
        dout_t = dout_ref[...]
        dus = _nt(dout_t, wout_ref[...])
        _accumulate(dwout_out, _tn(u * s, dout_t), first)
        ds = dus * u
        tril = lax.broadcasted_iota(jnp.int32, (CHUNK, CHUNK), 0) >= lax.broadcasted_iota(jnp.int32, (CHUNK, CHUNK), 1)
        lane = lax.broadcasted_iota(jnp.int32, (CHUNK, LANES), 1)
        dws = [jnp.zeros((CHUNK, CHUNK), F32) for _ in range(SGU_GROUPS)]
        dbst = jnp.zeros((CHUNK, LANES), F32)
        dvn_chunks = []
        for n in range(tm // CHUNK):
            cols = []
            for g in range(SGU_GROUPS):
                ds_ng = ds[n * CHUNK:(n + 1) * CHUNK, g * CHUNK:(g + 1) * CHUNK]
                vc_ng = vn[n * CHUNK:(n + 1) * CHUNK, g * CHUNK:(g + 1) * CHUNK]
                cols.append(_tn(jnp.where(tril, ws_ref[g], 0.0), ds_ng))
                dws[g] = dws[g] + _nt(ds_ng, vc_ng)
                dbst = dbst + jnp.where(lane == g, jnp.sum(ds_ng, axis=1, keepdims=True), 0.0)
            dvn_chunks.append(jnp.concatenate(cols, axis=1))
        dvn = jnp.concatenate(dvn_chunks, axis=0)
        for g in range(SGU_GROUPS):
            val = jnp.where(tril, dws[g], 0.0)

            @pl.when(first)
            def _():
                dws_out[g] = val

            @pl.when(jnp.logical_not(first))
            def _():
                dws_out[g] += val
        _accumulate(dbst_out, dbst, first)
        (dzu,) = vjp_u(dus * s)
        dzv, dlg, dlb = vjp_v(dvn)
        _accumulate(dlg_out, dlg, first)
        _accumulate(dlb_out, dlb, first)
        dzpre = jnp.concatenate([dzu, dzv], axis=1)
        _accumulate(dwin_out, _tn(hn, dzpre), first)
        dh, dcn = vjp_norm(_nt(dzpre, win_ref[...]))
        _accumulate(dcn_out, dcn, first)
        dh_out[...] = dh + dout_t

    return pl.pallas_call(
        body, name="sgu_bwd", grid=(t // tm,),
        in_specs=[_row(tm, d), _row(tm, d), _const((1, d)), _const((d, 2 * d)), _const((1, d)), _const((1, d)),
                  _const((SGU_GROUPS, CHUNK, CHUNK)), _const((CHUNK, LANES)), _const((d, d))],
        out_specs=[_row(tm, d), _const((1, d)), _const((d, 2 * d)), _const((1, d)), _const((1, d)), _const((SGU_GROUPS, CHUNK, CHUNK)),
                   _const((CHUNK, LANES)), _const((d, d))],
        out_shape=[_sds((t, d)), _sds((1, d)), _sds((d, 2 * d)), _sds((1, d)), _sds((1, d)), _sds((SGU_GROUPS, CHUNK, CHUNK)),
                   _sds((CHUNK, LANES)), _sds((d, d))],
        compiler_params=_params(),
    )(h, dout, w['c_norm'], w['c_w_in'], w['c_ln_g'], w['c_ln_b'], w['c_w_s'], w['bsT'], w['c_w_out'])


def _final_loss(h, target, norm, tm):
    t, d = h.shape

    def body(h_ref, tgt_ref, gn_ref, dh_out, loss_out, dgn_out):
        first = pl.program_id(0) == 0
        tgt = tgt_ref[...]

        def loss_fn(h_, g_):
            err = _rms(h_, g_) - tgt
            return 0.5 * jnp.sum(jnp.mean(err * err, axis=-1, keepdims=True), axis=0, keepdims=True)

        loss, vjp_loss = jax.vjp(loss_fn, h_ref[...], gn_ref[...])
        dh, dgn = vjp_loss(jnp.ones((1, 1), F32))
        dh_out[...] = dh
        _accumulate(loss_out, loss, first)
        _accumulate(dgn_out, dgn, first)

    return pl.pallas_call(body, name="final_loss", grid=(t // tm,), in_specs=[_row(tm, d), _row(tm, d), _const((1, d))],
                          out_specs=[_row(tm, d), _const((1, 1)), _const((1, d))],
                          out_shape=[_sds((t, d)), _sds((1, 1)), _sds((1, d))], compiler_params=_params())(h, target, norm)


def _tile(t, seq, want):
    tm = min(want, seq)
    assert seq % tm == 0 and t % tm == 0 and tm % CHUNK == 0
    return tm


def _local_step(x, posb, target, w, seq, late_weights, on_grads):
    t, d = x.shape
    b = t // seq
    hp = HEADS * HEAD_PAD
    tm_big, tm_mid = _tile(t, seq, 512), _tile(t, seq, 256)
    tq = _tile(t, seq, 512)

    q, k, v, xl, gate = _ab_in_fwd(x, posb, w, tm_big)
    o = _attn_fwd(q.reshape(b, seq, hp), k.reshape(b, seq, hp), v.reshape(b, seq, hp), tq).reshape(t, hp)
    y, hs = _lru_fwd(xl, gate, w, tm_big, seq)
    w = {**w, **late_weights('out0', y)}
    h1 = _ab_out_fwd(x, o, y, w, tm_big)
    hcur = h1
    saved = []
    for l in range(2):
        if l == 1:
            w = {**w, **late_weights('mix1', hcur)}
            saved_h2 = hcur
            hcur = _sgu_fwd(hcur, w, tm_mid)
        wl = late_weights('ffn%d' % l, hcur)
        g, u, hn = _ffn_a_fwd(hcur, w['ffn_norm'][l], wl['Wg'], wl['Wu'], tm_big)
        hnext = _ffn_b_fwd(g, u, hcur, w['ffn_conv_w'][l], w['ffn_conv_b'][l], wl['Wd'], tm_mid, seq)
        saved.append((hcur, g, u, wl, hn))
        hcur = hnext
    dh, loss, d_final = _final_loss(hcur, target, w['final_norm'], tm_big)

    ffn = {}
    conv_b = list(w['ffn_conv_b'])
    for l in (1, 0):
        hin, g, u, wl, hn = saved[l]
        dgc, du, d_wd, d_cw, d_cb = _ffn_b_bwd(g, u, dh, w['ffn_conv_w'][l], conv_b[l], wl['Wd'], tm_big, seq)
        dh, dg, d_norm = _ffn_a_dgrad(hin, w['ffn_norm'][l], dgc, du, dh, w['ffn_conv_w'][l], wl['Wg'], wl['Wu'], tm_mid, seq)
        d_wg, d_wu = _ffn_a_wgrad(hn, dg, du, _tile(t, seq, 2048))
        ffn[l] = dict(ffn_norm=d_norm, ffn_conv_w=d_cw, ffn_conv_b=d_cb, Wg=d_wg, Wu=d_wu, Wd=d_wd)
        if l == 1:
            dh, d_cn, d_cwin, d_lg, d_lb, d_ws, d_bst, d_cwout = _sgu_bwd(saved_h2, dh, w, tm_mid)
            zero = on_grads('late1', dict(final_norm=d_final, c_norm=d_cn, c_ln_g=d_lg, c_ln_b=d_lb, c_w_s=d_ws, bsT=d_bst, c_w_in=d_cwin,
                                          c_w_out=d_cwout, Wg=[d_wg], Wu=[d_wu], Wd=[d_wd]))
            conv_b[0] = conv_b[0] + zero
    late0 = {name: [ffn[0][name], ffn[1][name]] for name in ('ffn_norm', 'ffn_conv_w', 'ffn_conv_b')}
    zero = on_grads('late0', dict(late0, Wg=[ffn[0]['Wg']], Wu=[ffn[0]['Wu']], Wd=[ffn[0]['Wd']]))
    w = {**w, 'Wo_b': w['Wo_b'] + zero.astype(w['Wo_b'].dtype)}
    do, dy, d_woa, d_wob = _ab_out_bwd(o, y, dh, w, tm_big)
    dxl, dgate, d_cw, d_cb, d_wa, d_ba, d_wx, d_bx, d_lam = _lru_bwd(xl, gate, hs, dy, w, tm_big, seq)
    zero = on_grads('mid', dict(Wo_a=d_woa, Wo_b=d_wob, ab_conv_w=d_cw, ab_conv_b=d_cb, Wa=d_wa, ab_b_rg_a=d_ba, Wx=d_wx,
                                ab_b_rg_x=d_bx, ab_lambda=d_lam))
    w = {**w, 'ab_norm': w['ab_norm'] + zero}
    dq, dk, dv = _attn_bwd(q.reshape(b, seq, hp), k.reshape(b, seq, hp), v.reshape(b, seq, hp), do.reshape(b, seq, hp), tq)
    dx, d_gn, d_win, d_qn, d_wq, d_kvn, d_wk, d_wv = _ab_in_bwd(
        x, posb, w, dq.reshape(t, hp), dk.reshape(t, hp), dv.reshape(t, hp), dxl, dgate, dh, tm_mid)
    return loss, dx, dict(ab_norm=d_gn, W_in=d_win, ab_q_norm=d_qn, Wq=d_wq, ab_kv_norm=d_kvn, Wk=d_wk, Wv=d_wv)


def _block_diag(wg):
    g, n, _ = wg.shape
    return jnp.einsum('gij,gh->gihj', wg, jnp.eye(g, dtype=wg.dtype)).reshape(g * n, g * n)


def _prepare_out(w_out):
    d = w_out.shape[2]
    mla = HEADS * QK_NOPE
    return {'Wo_a': jnp.pad(w_out[0, :mla].reshape(HEADS, QK_NOPE, d), ((0, 0), (0, HEAD_PAD - QK_NOPE), (0, 0))).reshape(HEADS * HEAD_PAD, d),
            'Wo_b': w_out[0, mla:]}


def _prepare(full):
    d = full['ab_w_in'].shape[1]
    w_in = full['ab_w_in'][0]
    zeros = lambda n: jnp.zeros((d, n), w_in.dtype)
    wq = full['ab_w_q_b'][0].reshape(Q_LORA, HEADS, QK_NOPE + QK_ROPE)
    wkv = full['ab_w_kv_b'][0].reshape(KV_LORA, HEADS, 2 * QK_NOPE)
    pad_head = lambda a: jnp.pad(a, ((0, 0), (0, 0), (0, HEAD_PAD - a.shape[2]))).reshape(a.shape[0], HEADS * HEAD_PAD)
    w = {
        'W_in': jnp.concatenate([w_in[:, :Z_KPE], zeros(QK_NOPE), w_in[:, Z_KPE:Z_KPE + QK_ROPE],
                                 zeros(HEAD_PAD - QK_NOPE - QK_ROPE), w_in[:, Z_KPE + QK_ROPE:]], axis=1),
        'Wq': pad_head(wq), 'Wk': pad_head(wkv[:, :, :QK_NOPE]), 'Wv': pad_head(wkv[:, :, QK_NOPE:]),
        'Wa': _bf(_block_diag(full['ab_w_rg_a'][0])), 'Wx': _bf(_block_diag(full['ab_w_rg_x'][0])),
        'c_w_s': full['c_w_s'][0],
        'bsT': jnp.pad(full['c_b_s'][0].T, ((0, 0), (0, LANES - SGU_GROUPS))),
        'ffn_norm': [full['ffn_norm'][l:l + 1] for l in range(2)], 'ffn_conv_w': [full['ffn_conv_w'][l] for l in range(2)],
        'ffn_conv_b': [full['ffn_conv_b'][l:l + 1] for l in range(2)],
        'ab_conv_w': full['ab_conv_w'][0], 'final_norm': full['final_norm'][None, :],
    }
    for name in ('ab_norm', 'ab_q_norm', 'ab_kv_norm', 'ab_conv_b', 'ab_b_rg_a', 'ab_b_rg_x', 'ab_lambda', 'c_norm', 'c_ln_g', 'c_ln_b'):
        w[name] = full[name]
    return w


def _unprepare(g):
    unpad_head = lambda a, n: a.reshape(a.shape[0], HEADS, HEAD_PAD)[:, :, :n]
    diag = lambda a: jnp.einsum('gigj->gij', a.reshape(HEADS, LRU_W // HEADS, HEADS, LRU_W // HEADS))
    rules = {
        'ab_w_in': (('W_in',), lambda a: jnp.concatenate([a[:, :Z_KPE], a[:, Z_KPE + QK_NOPE:Z_KPE + QK_NOPE + QK_ROPE], a[:, Z_LRU:]], axis=1)[None]),
        'ab_w_q_b': (('Wq',), lambda a: unpad_head(a, QK_NOPE + QK_ROPE).reshape(1, Q_LORA, -1)),
        'ab_w_kv_b': (('Wk', 'Wv'), lambda a, b: jnp.concatenate([unpad_head(a, QK_NOPE), unpad_head(b, QK_NOPE)], axis=2).reshape(1, KV_LORA, -1)),
        'ab_w_out': (('Wo_a', 'Wo_b'), lambda a, b: jnp.concatenate(
            [a.reshape(HEADS, HEAD_PAD, -1)[:, :QK_NOPE].reshape(HEADS * QK_NOPE, -1), b], axis=0)[None]),
        'ab_w_rg_a': (('Wa',), lambda a: diag(a)[None]), 'ab_w_rg_x': (('Wx',), lambda a: diag(a)[None]),
        'c_w_in': (('c_w_in',), lambda a: a[None]), 'c_w_out': (('c_w_out',), lambda a: a[None]), 'c_w_s': (('c_w_s',), lambda a: a[None]),
        'c_b_s': (('bsT',), lambda a: a[:, :SGU_GROUPS].T[None]),
        'ffn_w_gate': (('Wg',), jnp.stack), 'ffn_w_up': (('Wu',), jnp.stack), 'ffn_w_down': (('Wd',), jnp.stack),
        'ffn_norm': (('ffn_norm',), lambda a: jnp.concatenate(a, axis=0)), 'ffn_conv_w': (('ffn_conv_w',), jnp.stack),
        'ffn_conv_b': (('ffn_conv_b',), lambda a: jnp.concatenate(a, axis=0)),
        'ab_conv_w': (('ab_conv_w',), lambda a: a[None]), 'final_norm': (('final_norm',), lambda a: a[0]),
    }
    for name in ('ab_norm', 'ab_q_norm', 'ab_kv_norm', 'ab_conv_b', 'ab_b_rg_a', 'ab_b_rg_x', 'ab_lambda', 'c_norm', 'c_ln_g', 'c_ln_b'):
        rules[name] = ((name,), lambda a: a)
    return {name: fn(*[g[k] for k in keys]) for name, (keys, fn) in rules.items() if all(k in g for k in keys)}


SLAB_ROWS = 16


def _round_up(n, m):
    return -(-n // m) * m


def _to_chunks(full, axis):
    s = full.shape
    return jnp.moveaxis(full.reshape(s[:axis] + (N_DEV, s[axis] // N_DEV) + s[axis + 1:]), axis, 0)


def _from_chunks(chunks, axis):
    local = chunks.shape[1:]
    return jnp.moveaxis(chunks, 0, axis).reshape(local[:axis] + (N_DEV * local[axis],) + local[axis + 1:])


def _merge_chunks(me, own, landed, axis, name):
    _, r, n = own.shape
    if axis == 1:
        def body(me_ref, own_ref, l_ref, o_ref):
            o_ref[...] = jnp.where(me_ref[0] == pl.program_id(0), own_ref[...], l_ref[0])

        grid, out_shape = (N_DEV,), (1, N_DEV * r, n)
        specs = [pl.BlockSpec((1, r, n), lambda dev, me_ref: (0, 0, 0)), pl.BlockSpec((1, 1, r, n), lambda dev, me_ref: (dev, 0, 0, 0))]
        out_spec = pl.BlockSpec((1, r, n), lambda dev, me_ref: (0, dev, 0))
    else:
        tr = r // 4

        def body(me_ref, own_ref, l_ref, o_ref):
            o_ref[0] = jnp.concatenate([jnp.where(me_ref[0] == dev, own_ref[0], l_ref[dev, 0]) for dev in range(N_DEV)], axis=1)

        grid, out_shape = (r // tr,), (1, r, N_DEV * n)
        specs = [pl.BlockSpec((1, tr, n), lambda i, me_ref: (0, i, 0)), pl.BlockSpec((N_DEV, 1, tr, n), lambda i, me_ref: (0, 0, i, 0))]
        out_spec = pl.BlockSpec((1, tr, N_DEV * n), lambda i, me_ref: (0, i, 0))
    return pl.pallas_call(
        body, name="merge_" + name,
        grid_spec=pltpu.PrefetchScalarGridSpec(num_scalar_prefetch=1, grid=grid, in_specs=specs, out_specs=out_spec),
        out_shape=jax.ShapeDtypeStruct(out_shape, own.dtype), compiler_params=_params())(me, own, landed)


def _split_chunks(whole, axis, name):
    _, rows, cols = whole.shape
    if axis == 1:
        r = rows // N_DEV

        def body(x_ref, o_ref):
            o_ref[0] = _bf(x_ref[...])

        grid, out_shape = (N_DEV,), (N_DEV, 1, r, cols)
        spec, out_spec = pl.BlockSpec((1, r, cols), lambda dev: (0, dev, 0)), pl.BlockSpec((1, 1, r, cols), lambda dev: (dev, 0, 0, 0))
    else:
        n, tr = cols // N_DEV, rows // 4

        def body(x_ref, o_ref):
            x = x_ref[0]
            for dev in range(N_DEV):
                o_ref[dev, 0] = _bf(x[:, dev * n:(dev + 1) * n])

        grid, out_shape = (rows // tr,), (N_DEV, 1, rows, n)
        spec, out_spec = pl.BlockSpec((1, tr, cols), lambda i: (0, i, 0)), pl.BlockSpec((N_DEV, 1, tr, n), lambda i: (0, 0, i, 0))
    return pl.pallas_call(body, name="split_" + name, grid=grid, in_specs=[spec], out_specs=out_spec,
                          out_shape=jax.ShapeDtypeStruct(out_shape, BF16), compiler_params=_params())(whole)


def _slab_rows(n):
    return _round_up(-(-n // LANES), SLAB_ROWS)


def _to_slab(a, lead):
    a = a.reshape(lead + (-1,))
    rows = _slab_rows(a.shape[-1])
    a = jnp.pad(a, [(0, 0)] * len(lead) + [(0, rows * LANES - a.shape[-1])])
    return a.reshape(lead + (rows, LANES))


def _pack_slabs(parts, lead):
    return jnp.concatenate([_to_slab(p, lead) for p in parts], axis=len(lead))


def _unpack_slabs(packed, shapes):
    lead = packed.shape[:-2]
    out, row = [], 0
    for shape in shapes:
        size = math.prod(shape)
        rows = _slab_rows(size)
        piece = lax.slice_in_dim(packed, row, row + rows, axis=len(lead))
        out.append(piece.reshape(lead + (rows * LANES,))[..., :size].reshape(lead + tuple(shape)))
        row += rows
    return out


HBM = pl.BlockSpec(memory_space=pl.ANY)


def _other_chips(x, y):
    return [(1 - x, y), (x, 1 - y), (1 - x, 1 - y)]


def _all_gather(blocks):
    n = len(blocks)

    def body(*refs):
        x_refs, out_refs, token = refs[:n], refs[n:2 * n], refs[2 * n]
        send_sems, recv_sems, local_sems = refs[2 * n + 1:]
        token[...] = jnp.zeros_like(token)
        x, y, c = lax.axis_index("x"), lax.axis_index("y"), lax.axis_index("c")
        me, sibling = (x, y, c), (x, y, 1 - c)
        chips = _other_chips(x, y)

        def slab(a, px, py, pc):
            return out_refs[a].at[4 * px + 2 * py + pc]

        def copy(a, k, blk, to, src=None):
            return pltpu.make_async_remote_copy(src_ref=slab(a, *blk) if src is None else src, dst_ref=slab(a, *blk),
                                                send_sem=send_sems.at[7 * a + k], recv_sem=recv_sems.at[7 * a + k],
                                                device_id=to, device_id_type=MESH)

        mine = [pltpu.make_async_copy(x_refs[a], slab(a, *me), local_sems.at[a]) for a in range(n)]
        started = []
        for a in range(n):
            mine[a].start()
            started.append(copy(a, 0, me, sibling, src=x_refs[a]))
            started += [copy(a, 1 + j, me, (*chip, c), src=x_refs[a]) for j, chip in enumerate(chips)]
        for cp in started:
            cp.start()
        for j, chip in enumerate(chips):
            for a in range(n):
                copy(a, 1 + j, (*chip, c), me).wait_recv()
                passed = copy(a, 4 + j, (*chip, c), sibling)
                passed.start()
                started.append(passed)
        for a in range(n):
            copy(a, 0, sibling, me).wait_recv()
        for j, chip in enumerate(chips):
            for a in range(n):
                copy(a, 4 + j, (*chip, 1 - c), me).wait_recv()
        for cp in started:
            cp.wait_send()
        for a in range(n):
            mine[a].wait()

    out = pl.pallas_call(
        body, name="all_gather_weights",
        out_shape=[jax.ShapeDtypeStruct((N_DEV,) + b.shape, b.dtype) for b in blocks] + [jax.ShapeDtypeStruct((8, LANES), F32)],
        in_specs=[HBM] * n, out_specs=[HBM] * n + [pl.BlockSpec(memory_space=pltpu.VMEM)],
        scratch_shapes=[pltpu.SemaphoreType.DMA((7 * n,)), pltpu.SemaphoreType.DMA((7 * n,)), pltpu.SemaphoreType.DMA((n,))],
    )(*blocks)
    return list(out[:n]), out[n][0, 0]


FLIPS = [(0, 0, 1), (1, 0, 0), (1, 0, 1), (0, 1, 0), (0, 1, 1), (1, 1, 0), (1, 1, 1)]


def _peers(x, y, c):
    flip = lambda v, f: 1 - v if f else v
    return [(flip(x, fx), flip(y, fy), flip(c, fc)) for fx, fy, fc in FLIPS]


def _direct_copies(src_refs, land_refs, send_sems, recv_sems, scatter):
    x, y, c = lax.axis_index("x"), lax.axis_index("y"), lax.axis_index("c")
    me = 4 * x + 2 * y + c
    starts, waits = [], []
    for a in range(len(src_refs)):
        for k, (px, py, pc) in enumerate(_peers(x, y, c)):
            peer = 4 * px + 2 * py + pc
            sems = dict(send_sem=send_sems.at[7 * a + k], recv_sem=recv_sems.at[7 * a + k], device_id=(px, py, pc), device_id_type=MESH)
            src = src_refs[a].at[peer] if scatter else src_refs[a]
            starts.append(pltpu.make_async_remote_copy(src_ref=src, dst_ref=land_refs[a].at[me], **sems))
            waits.append(pltpu.make_async_remote_copy(src_ref=src, dst_ref=land_refs[a].at[peer], **sems))
    return starts, waits


def _landing(src, scatter):
    block = src.shape[1:] if scatter else src.shape
    return jax.ShapeDtypeStruct((N_DEV,) + block, src.dtype)


HBM_SPACE = pl.BlockSpec(memory_space=pltpu.HBM)
SEMAPHORES = pl.BlockSpec(memory_space=pltpu.SEMAPHORE)
SPLIT_EFFECT = pltpu.SideEffectType.DATAFLOW_SIDE_EFFECTING


def _start_exchange(name, srcs, scatter):
    n = len(srcs)
    lands = [lax.empty(s.shape, s.dtype) for s in (_landing(s, scatter) for s in srcs)]

    def body(*refs):
        starts, _ = _direct_copies(refs[:n], refs[n:2 * n], refs[2 * n], refs[2 * n + 1], scatter)
        for cp in starts:
            cp.start()
        refs[-1][...] = jnp.zeros_like(refs[-1])

    held = [pltpu.with_memory_space_constraint(a, pltpu.HBM) for a in list(srcs) + lands]
    out = pl.pallas_call(
        body, name=name + "_start",
        out_shape=(pltpu.SemaphoreType.DMA((7 * n,)), pltpu.SemaphoreType.DMA((7 * n,)), *[pltpu.HBM(a.shape, a.dtype) for a in held],
                   jax.ShapeDtypeStruct((8, LANES), F32)),
        in_specs=[HBM_SPACE] * (2 * n), out_specs=(SEMAPHORES, SEMAPHORES, *[HBM_SPACE] * (2 * n), pl.BlockSpec(memory_space=pltpu.VMEM)),
        input_output_aliases={i: 2 + i for i in range(2 * n)},
        compiler_params=pltpu.CompilerParams(has_side_effects=SPLIT_EFFECT),
    )(*held)
    return out[0], out[1], list(out[2:2 + n]), list(out[2 + n:2 + 2 * n]), out[-1][0, 0], out[-1]


def _wait_exchange(name, started, after, scatter):
    send_sems, recv_sems, srcs, lands = started[:4]
    n = len(srcs)

    def body(*refs):
        _, waits = _direct_copies(refs[:n], refs[n:2 * n], refs[2 * n], refs[2 * n + 1], scatter)
        for cp in waits:
            cp.wait_send()
        for cp in waits:
            cp.wait_recv()

    out = pl.pallas_call(
        body, name=name + "_wait", out_shape=tuple(pltpu.HBM(a.shape, a.dtype) for a in srcs + lands),
        in_specs=[HBM_SPACE] * (2 * n) + [SEMAPHORES, SEMAPHORES, HBM], out_specs=tuple([HBM_SPACE] * (2 * n)),
        input_output_aliases={i: i for i in range(2 * n)},
        compiler_params=pltpu.CompilerParams(has_side_effects=SPLIT_EFFECT),
    )(*srcs, *lands, send_sems, recv_sems, after)
    return list(out[:n]), list(out[n:])


def _row_tile(rows):
    return rows // 2 if (rows // 2) % SLAB_ROWS == 0 else rows


def _sum_and_adamw(me, landed, own, wts, m, v, name, layer=None, into=None):
    layers, r, n = wts.shape
    first = 0 if layer is None else layer
    count = layers if layer is None else 1
    tr = _row_tile(r)
    blk = pl.BlockSpec((1, tr, n), lambda li, ri, me_ref: (first + li, ri, 0))
    c1 = 1.0 / (1.0 - ADAM_B1 ** ADAM_STEP)
    c2 = 1.0 / (1.0 - ADAM_B2 ** ADAM_STEP)
    held = [] if into is None else list(into)

    def body(me_ref, l_ref, own_ref, w_ref, m_ref, v_ref, *rest):
        g_out, d_out, m_out, v_out = rest[len(held):]
        mine = own_ref[0].astype(F32)
        g = jnp.where(me_ref[0] == 0, mine, l_ref[0].astype(F32))
        for dev in range(1, N_DEV):
            g = g + jnp.where(me_ref[0] == dev, mine, l_ref[dev].astype(F32))
        m_new = ADAM_B1 * m_ref[...] + (1.0 - ADAM_B1) * g
        v_new = ADAM_B2 * v_ref[...] + (1.0 - ADAM_B2) * (g * g)
        g_out[...] = g
        m_out[...] = m_new
        v_out[...] = v_new
        d_out[...] = -ADAM_LR * ((m_new * c1) / (jnp.sqrt(v_new * c2) + ADAM_EPS) + ADAM_WD * w_ref[...])

    return pl.pallas_call(
        body, name="adamw_" + name,
        grid_spec=pltpu.PrefetchScalarGridSpec(
            num_scalar_prefetch=1, grid=(count, r // tr),
            in_specs=[pl.BlockSpec((N_DEV, 1, tr, n), lambda li, ri, me_ref: (0, li, ri, 0)),
                      pl.BlockSpec((1, 1, tr, n), lambda li, ri, me_ref: (me_ref[0], li, ri, 0)), blk, blk, blk] + [HBM] * len(held),
            out_specs=[blk] * 4),
        out_shape=[_sds((layers, r, n))] * 4, input_output_aliases={6 + i: i for i in range(len(held))},
        compiler_params=_params(2))(me, landed, own, wts, m, v, *held)


EARLY = ['ab_w_in']
LATE_STAGES = {
    'out0': [('ab_w_out', None, 'ab_w_out')],
    'ffn0': [('ffn_w_gate', 0, 'Wg'), ('ffn_w_up', 0, 'Wu'), ('ffn_w_down', 0, 'Wd')],
    'mix1': [('c_w_in', None, 'c_w_in'), ('c_w_out', None, 'c_w_out')],
    'ffn1': [('ffn_w_gate', 1, 'Wg'), ('ffn_w_up', 1, 'Wu'), ('ffn_w_down', 1, 'Wd')],
}
TRANSPOSED = ('ffn_w_gate', 'ffn_w_up')


def _stored(name, a):
    return jnp.swapaxes(a, 1, 2) if name in TRANSPOSED else a


def _stored_axis(name):
    return 1 if name in TRANSPOSED else SHARD_AXIS[name]


GRAD_STAGES = {
    'late1': ([('c_w_in', None), ('c_w_out', None), ('ffn_w_gate', 1), ('ffn_w_up', 1), ('ffn_w_down', 1)],
              ['c_norm', 'c_ln_g', 'c_ln_b', 'c_w_s', 'c_b_s', 'final_norm']),
    'late0': ([('ffn_w_gate', 0), ('ffn_w_up', 0), ('ffn_w_down', 0)], ['ffn_norm', 'ffn_conv_w', 'ffn_conv_b']),
    'mid': ([('ab_w_out', None)], ['ab_conv_w', 'ab_conv_b', 'ab_w_rg_a', 'ab_b_rg_a', 'ab_w_rg_x', 'ab_b_rg_x', 'ab_lambda']),
    'last': ([('ab_w_in', None)], ['ab_norm', 'ab_q_norm', 'ab_w_q_b', 'ab_kv_norm', 'ab_w_kv_b']),
}


def _gather_early(local):
    small = [_bf(local[n]) if n in MATRICES else lax.bitcast_convert_type(local[n], BF16) for n in SMALL_SHARDED]
    gathered, zero = _all_gather([_bf(local[n]) for n in EARLY] + [_pack_slabs(small, ())])
    full = {n: local[n] for n in REPLICATED}
    for n, g in zip(EARLY, gathered):
        full[n] = _from_chunks(g, SHARD_AXIS[n])
    for n, p in zip(SMALL_SHARDED, _unpack_slabs(gathered[-1], [s.shape for s in small])):
        full[n] = _from_chunks(p if n in MATRICES else lax.bitcast_convert_type(p, F32), SHARD_AXIS[n])
    return full, zero


def kernel(x, positions, ab_norm, ab_w_in, ab_q_norm, ab_w_q_b, ab_kv_norm, ab_w_kv_b, ab_conv_w, ab_conv_b, ab_w_rg_a, ab_b_rg_a, ab_w_rg_x, ab_b_rg_x, ab_lambda, ab_w_out, c_norm, c_w_in, c_ln_g, c_ln_b, c_w_s, c_b_s, c_w_out, ffn_norm, ffn_w_gate, ffn_w_up, ffn_conv_w, ffn_conv_b, ffn_w_down, final_norm, loss_target, m_ab_norm, m_ab_w_in, m_ab_q_norm, m_ab_w_q_b, m_ab_kv_norm, m_ab_w_kv_b, m_ab_conv_w, m_ab_conv_b, m_ab_w_rg_a, m_ab_b_rg_a, m_ab_w_rg_x, m_ab_b_rg_x, m_ab_lambda, m_ab_w_out, m_c_norm, m_c_w_in, m_c_ln_g, m_c_ln_b, m_c_w_s, m_c_b_s, m_c_w_out, m_ffn_norm, m_ffn_w_gate, m_ffn_w_up, m_ffn_conv_w, m_ffn_conv_b, m_ffn_w_down, m_final_norm, v_ab_norm, v_ab_w_in, v_ab_q_norm, v_ab_w_q_b, v_ab_kv_norm, v_ab_w_kv_b, v_ab_conv_w, v_ab_conv_b, v_ab_w_rg_a, v_ab_b_rg_a, v_ab_w_rg_x, v_ab_b_rg_x, v_ab_lambda, v_ab_w_out, v_c_norm, v_c_w_in, v_c_ln_g, v_c_ln_b, v_c_w_s, v_c_b_s, v_c_w_out, v_ffn_norm, v_ffn_w_gate, v_ffn_w_up, v_ffn_conv_w, v_ffn_conv_b, v_ffn_w_down, v_final_norm):
    given = dict(locals())
    local = {n: given[n] for n in WEIGHTS}
    b, seq, d = x.shape
    t = b * seq

    me = (4 * lax.axis_index("x") + 2 * lax.axis_index("y") + lax.axis_index("c")).astype(jnp.int32)
    me1 = me.reshape(1)

    full, zero = _gather_early(local)
    gathers = {}
    for stage, members in LATE_STAGES.items():
        srcs = [_bf(_stored(n, local[n] if layer is None else local[n][layer:layer + 1]) + zero) for n, layer, _ in members]
        gathers[stage] = _start_exchange('gather_' + stage, srcs, scatter=False)
        zero = gathers[stage][4]
    w = _prepare(full)
    w['ab_norm'] = w['ab_norm'] + zero

    def late_weights(stage, after):
        srcs, lands = _wait_exchange('gather_' + stage, gathers[stage], after, scatter=False)
        whole = [_merge_chunks(me1, s, l, _stored_axis(n), n + ('' if layer is None else str(layer)))
                 for (n, layer, _), s, l in zip(LATE_STAGES[stage], srcs, lands)]
        if stage == 'out0':
            return _prepare_out(whole[0])
        return {key: a[0] for (_, _, key), a in zip(LATE_STAGES[stage], whole)}

    scatters = {}

    def start_scatter(stage, g):
        whole = _unprepare(g)
        big, small = GRAD_STAGES[stage]
        slab = [_to_chunks(whole[n], SHARD_AXIS[n]) if n in SHARD_AXIS else jnp.broadcast_to(whole[n][None], (N_DEV,) + whole[n].shape)
                for n in small]
        own = [_split_chunks(whole[n], _stored_axis(n), n + ('' if layer is None else str(layer))) for n, layer in big]
        own.append(_bf(_pack_slabs(slab, (N_DEV,)))[:, None])
        scatters[stage] = _start_exchange('scatter_' + stage, own, scatter=True)
        return scatters[stage][4]

    posb = jnp.broadcast_to(positions.astype(F32).reshape(t, 1), (t, LANES))
    loss, dx, grads = _local_step(x.reshape(t, d), posb, loss_target.reshape(t, d), w, seq, late_weights, start_scatter)
    start_scatter('last', grads)
    after = scatters['last'][5]

    me1 = me.reshape(1)
    updated = {}
    for stage, (big, small) in GRAD_STAGES.items():
        owns, landed = _wait_exchange('scatter_' + stage, scatters[stage], after, scatter=True)
        for (n, layer), own, land in zip(big, owns, landed):
            updated[n] = _sum_and_adamw(me1, land, own, _stored(n, given[n]), _stored(n, given['m_' + n]), _stored(n, given['v_' + n]),
                                        n + ('' if layer is None else str(layer)), layer, updated.get(n))
        pack_small = lambda prefix: _pack_slabs([given[prefix + n] for n in small], ())[None]
        packed = _sum_and_adamw(me1, landed[-1], owns[-1], pack_small(''), pack_small('m_'), pack_small('v_'), 'small_' + stage)
        unpacked = [_unpack_slabs(p[0], [local[n].shape for n in small]) for p in packed]
        for i, n in enumerate(small):
            updated[n] = [u[i] for u in unpacked]
        after = sum([updated[n][1][:1, :1, :1] for n, _ in big], packed[1][:1, :1, :1])
    total = lax.psum(loss[0, 0], ("x", "y", "c"))
    return (total, dx.reshape(b, seq, d), *[_stored(n, updated[n][kind]) for kind in range(4) for n in WEIGHTS])
```

```python
import math

import jax
import jax.numpy as jnp
from jax import lax
from jax.experimental import pallas as pl
from jax.experimental.pallas import tpu as pltpu

F32 = jnp.float32
BF16 = jnp.bfloat16
MESH = pl.DeviceIdType.MESH

N_DEV = 8
LANES = 128
HALO = 8
VMEM_LIMIT = 56 << 20

NORM_EPS = 1e-6
HEADS = 8
HEAD_PAD = 128
QK_NOPE = 64
QK_ROPE = 32
ROPE_HALF = 16
ROPE_BASE = 10000.0
ATTN_SCALE = (QK_NOPE + QK_ROPE) ** -0.5
LRU_C = 8.0
LRU_W = 512
CHUNK = 128
SGU_GROUPS = 8
D_FF = 2816
FF_BLOCKS = 2

ADAM_LR, ADAM_B1, ADAM_B2, ADAM_EPS, ADAM_WD, ADAM_STEP = 0.001, 0.9, 0.999, 1e-08, 0.01, 10

WEIGHTS = ['ab_norm', 'ab_w_in', 'ab_q_norm', 'ab_w_q_b', 'ab_kv_norm', 'ab_w_kv_b', 'ab_conv_w', 'ab_conv_b',
           'ab_w_rg_a', 'ab_b_rg_a', 'ab_w_rg_x', 'ab_b_rg_x', 'ab_lambda', 'ab_w_out', 'c_norm', 'c_w_in', 'c_ln_g',
           'c_ln_b', 'c_w_s', 'c_b_s', 'c_w_out', 'ffn_norm', 'ffn_w_gate', 'ffn_w_up', 'ffn_conv_w', 'ffn_conv_b',
           'ffn_w_down', 'final_norm']
SHARD_AXIS = {'ab_w_in': 2, 'ab_w_q_b': 2, 'ab_w_kv_b': 2, 'ab_conv_w': 2, 'ab_w_out': 1, 'c_norm': 1, 'c_w_in': 2,
              'c_ln_g': 1, 'c_ln_b': 1, 'c_w_out': 1, 'ffn_w_gate': 2, 'ffn_w_up': 2, 'ffn_conv_w': 2, 'ffn_w_down': 1}
MATRICES = ['ab_w_in', 'ab_w_q_b', 'ab_w_kv_b', 'ab_w_out', 'c_w_in', 'c_w_out', 'ffn_w_gate', 'ffn_w_up', 'ffn_w_down']
BIG = ['ab_w_in', 'c_w_in', 'ffn_w_gate', 'ffn_w_up', 'ab_w_out', 'c_w_out', 'ffn_w_down']
REPLICATED = [n for n in WEIGHTS if n not in SHARD_AXIS]
SMALL_SHARDED = [n for n in WEIGHTS if n in SHARD_AXIS and n not in BIG]


def _bf(x):
    return x.astype(BF16)


def _nn(a, b):
    return lax.dot_general(_bf(a), _bf(b), (((1,), (0,)), ((), ())), preferred_element_type=F32)


def _nt(a, b):
    return lax.dot_general(_bf(a), _bf(b), (((1,), (1,)), ((), ())), preferred_element_type=F32)


def _tn(a, b):
    return lax.dot_general(_bf(a), _bf(b), (((0,), (0,)), ((), ())), preferred_element_type=F32)


def _rms(x, g):
    return x * lax.rsqrt(jnp.mean(x * x, axis=-1, keepdims=True) + NORM_EPS) * g


def _layer_norm(x, g, b):
    xc = x - jnp.mean(x, axis=-1, keepdims=True)
    return xc * lax.rsqrt(jnp.mean(xc * xc, axis=-1, keepdims=True) + NORM_EPS) * g + b


def _gelu(x):
    return jax.nn.gelu(x)


STRIP = 16
STRIP_LANES = 384
GELU_C = math.sqrt(2.0 / math.pi)
GELU_A = 0.044715


def _gelu_and_grad(x):
    x2 = x * x
    t = jnp.tanh(x * (GELU_C + (GELU_C * GELU_A) * x2))
    half_x = 0.5 * x
    one_plus_t = 1.0 + t
    return half_x * one_plus_t, 0.5 * one_plus_t + half_x * (1.0 - t * t) * (GELU_C + (3.0 * GELU_C * GELU_A) * x2)


def _colsum(x):
    return jnp.sum(x, axis=0, keepdims=True)


def _softplus(x):
    return jnp.maximum(x, 0.0) + jnp.log1p(jnp.exp(-jnp.abs(x)))


@jax.custom_vjp
def _decay(x):
    a = jnp.exp(x)
    y = 2.0 * x
    series = -y * (1.0 + y * (1 / 2 + y * (1 / 6 + y * (1 / 24 + y * (1 / 120 + y * (1 / 720))))))
    return a, jnp.where(y < -0.3, 1.0 - a * a, series)


def _decay_fwd(x):
    a, gap = _decay(x)
    return (a, gap), a


def _decay_bwd(a, cts):
    return (a * (cts[0] - 2.0 * a * cts[1]),)


_decay.defvjp(_decay_fwd, _decay_bwd)


def _accumulate(ref, val, first):
    @pl.when(first)
    def _():
        ref[...] = val

    @pl.when(jnp.logical_not(first))
    def _():
        ref[...] += val


def _params(n_axes=1):
    return pltpu.CompilerParams(dimension_semantics=("arbitrary",) * n_axes, vmem_limit_bytes=VMEM_LIMIT)


def _row(tm, n):
    return pl.BlockSpec((tm, n), lambda i: (i, 0))


def _const(shape):
    nd = len(shape)
    return pl.BlockSpec(shape, lambda i: (0,) * nd, pipeline_mode=pl.Buffered(1))


def _prev_halo(tm, n):
    return pl.BlockSpec((HALO, n), lambda i: (jnp.maximum(i * (tm // HALO) - 1, 0), 0))


def _next_halo(tm, n, n_tiles):
    last = n_tiles * (tm // HALO) - 1
    return pl.BlockSpec((HALO, n), lambda i: (jnp.minimum((i + 1) * (tm // HALO), last), 0))


def _sds(shape, dtype=F32):
    return jax.ShapeDtypeStruct(shape, dtype)


def _rope_tables(posb):
    lane = lax.broadcasted_iota(jnp.int32, posb.shape, 1)
    in_rope = jnp.logical_and(lane >= QK_NOPE, lane < QK_NOPE + QK_ROPE)
    j = (lane & (ROPE_HALF - 1)).astype(F32)
    inv_freq = jnp.exp((-math.log(ROPE_BASE)) * j / ROPE_HALF)
    ang = posb * inv_freq
    return jnp.where(in_rope, jnp.cos(ang), 1.0), jnp.where(in_rope, jnp.sin(ang), 0.0)


def _rot(q):
    n = q.shape[1]
    lane = lax.broadcasted_iota(jnp.int32, q.shape, 1) & (HEAD_PAD - 1)
    first_half = jnp.where(lane >= QK_NOPE, -pltpu.roll(q, n - ROPE_HALF, 1), 0.0)
    second_half = jnp.where(lane < QK_NOPE + QK_ROPE, pltpu.roll(q, ROPE_HALF, 1), 0.0)
    return jnp.where(lane < QK_NOPE + ROPE_HALF, first_half, second_half)


def _rope(q, cos_t, sin_t):
    return q * cos_t + _rot(q) * sin_t


def _rope_transpose(dq, cos_t, sin_t):
    return dq * cos_t - _rot(dq * sin_t)


def _tile_heads(t):
    return jnp.concatenate([t] * HEADS, axis=1)


Q_LORA, KV_LORA = 256, 128
Z_KPE = Q_LORA + KV_LORA
Z_LRU = Z_KPE + HEAD_PAD
Z_GATE = Z_LRU + LRU_W
Z_WIDTH = Z_GATE + LRU_W


def _ab_in_fwd(x, posb, w, tm):
    t, d = x.shape

    def body(x_ref, pos_ref, gn_ref, win_ref, qn_ref, wq_ref, kvn_ref, wk_ref, wv_ref, q_out, k_out, v_out, xl_out, gate_out):
        hn = _rms(x_ref[...], gn_ref[...])
        z = _nn(hn, win_ref[...])
        cqn = _rms(z[:, :Q_LORA], qn_ref[...])
        kvn = _rms(z[:, Q_LORA:Z_KPE], kvn_ref[...])
        cos_t, sin_t = _rope_tables(pos_ref[...])
        q_out[...] = _rope(_nn(cqn, wq_ref[...]), _tile_heads(cos_t), _tile_heads(sin_t))
        kpe = _rope(z[:, Z_KPE:Z_LRU], cos_t, sin_t)
        k_out[...] = _nn(kvn, wk_ref[...]) + _tile_heads(kpe)
        v_out[...] = _nn(kvn, wv_ref[...])
        xl_out[...] = z[:, Z_LRU:Z_GATE]
        gate_out[...] = z[:, Z_GATE:]

    hp = HEADS * HEAD_PAD
    return pl.pallas_call(
        body, name="ab_in_fwd", grid=(t // tm,),
        in_specs=[_row(tm, d), _row(tm, LANES), _const((1, d)), _const((d, Z_WIDTH)), _const((1, Q_LORA)), _const((Q_LORA, hp)),
                  _const((1, KV_LORA)), _const((KV_LORA, hp)), _const((KV_LORA, hp))],
        out_specs=[_row(tm, hp), _row(tm, hp), _row(tm, hp), _row(tm, LRU_W), _row(tm, LRU_W)],
        out_shape=[_sds((t, hp)), _sds((t, hp)), _sds((t, hp)), _sds((t, LRU_W)), _sds((t, LRU_W))],
        compiler_params=_params(),
    )(x, posb, w['ab_norm'], w['W_in'], w['ab_q_norm'], w['Wq'], w['ab_kv_norm'], w['Wk'], w['Wv'])


def _ab_in_bwd(x, posb, w, dq, dk, dv, dxl, dgate, dres, tm):
    t, d = x.shape
    hp = HEADS * HEAD_PAD

    def body(x_ref, pos_ref, gn_ref, win_ref, qn_ref, wq_ref, kvn_ref, wk_ref, wv_ref, dq_ref, dk_ref, dv_ref, dxl_ref, dgate_ref,
             dres_ref, dx_out, dgn_out, dwin_out, dqn_out, dwq_out, dkvn_out, dwk_out, dwv_out):
        first = pl.program_id(0) == 0
        hn, vjp_in = jax.vjp(_rms, x_ref[...], gn_ref[...])
        z = _nn(hn, win_ref[...])
        cqn, vjp_q = jax.vjp(_rms, z[:, :Q_LORA], qn_ref[...])
        kvn, vjp_kv = jax.vjp(_rms, z[:, Q_LORA:Z_KPE], kvn_ref[...])
        cos_t, sin_t = _rope_tables(pos_ref[...])
        dq0 = _rope_transpose(dq_ref[...], _tile_heads(cos_t), _tile_heads(sin_t))
        dk0 = dk_ref[...]
        dv0 = dv_ref[...]
        dkpe = dk0[:, :HEAD_PAD]
        for h in range(1, HEADS):
            dkpe = dkpe + dk0[:, h * HEAD_PAD:(h + 1) * HEAD_PAD]
        dkpe = _rope_transpose(dkpe, cos_t, sin_t)
        _accumulate(dwq_out, _tn(cqn, dq0), first)
        _accumulate(dwk_out, _tn(kvn, dk0), first)
        _accumulate(dwv_out, _tn(kvn, dv0), first)
        dcq, dqn = vjp_q(_nt(dq0, wq_ref[...]))
        dckv, dkvn = vjp_kv(_nt(dk0, wk_ref[...]) + _nt(dv0, wv_ref[...]))
        _accumulate(dqn_out, dqn, first)
        _accumulate(dkvn_out, dkvn, first)
        dz = jnp.concatenate([dcq, dckv, dkpe, dxl_ref[...], dgate_ref[...]], axis=1)
        _accumulate(dwin_out, _tn(hn, dz), first)
        dx, dgn = vjp_in(_nt(dz, win_ref[...]))
        _accumulate(dgn_out, dgn, first)
        dx_out[...] = dx + dres_ref[...]

    return pl.pallas_call(
        body, name="ab_in_bwd", grid=(t // tm,),
        in_specs=[_row(tm, d), _row(tm, LANES), _const((1, d)), _const((d, Z_WIDTH)), _const((1, Q_LORA)), _const((Q_LORA, hp)),
                  _const((1, KV_LORA)), _const((KV_LORA, hp)), _const((KV_LORA, hp)),
                  _row(tm, hp), _row(tm, hp), _row(tm, hp), _row(tm, LRU_W), _row(tm, LRU_W), _row(tm, d)],
        out_specs=[_row(tm, d), _const((1, d)), _const((d, Z_WIDTH)), _const((1, Q_LORA)), _const((Q_LORA, hp)),
                   _const((1, KV_LORA)), _const((KV_LORA, hp)), _const((KV_LORA, hp))],
        out_shape=[_sds((t, d)), _sds((1, d)), _sds((d, Z_WIDTH)), _sds((1, Q_LORA)), _sds((Q_LORA, hp)),
                   _sds((1, KV_LORA)), _sds((KV_LORA, hp)), _sds((KV_LORA, hp))],
        compiler_params=_params(),
    )(x, posb, w['ab_norm'], w['W_in'], w['ab_q_norm'], w['Wq'], w['ab_kv_norm'], w['Wk'], w['Wv'], dq, dk, dv, dxl, dgate, dres)


def _attn_probs(q_blk, k_ext, i, tq):
    ext = k_ext.shape[0]
    s = lax.dot_general(q_blk, k_ext, (((1,), (1,)), ((), ())), preferred_element_type=F32) * ATTN_SCALE
    causal = lax.broadcasted_iota(jnp.int32, (tq, tq), 1) <= lax.broadcasted_iota(jnp.int32, (tq, tq), 0)
    diag = jnp.where(causal, s[:, ext - tq:], -1e30)
    s = diag if ext == tq else jnp.concatenate([s[:, :ext - tq], diag], axis=1)
    p = jnp.exp(s - jnp.max(s, axis=1, keepdims=True))
    return p / jnp.sum(p, axis=1, keepdims=True)


def _attn_fwd(q, k, v, tq):
    b, s, hp = q.shape
    blk = pl.BlockSpec((1, s, HEAD_PAD), lambda bi, h: (bi, 0, h))

    def body(q_ref, k_ref, v_ref, o_ref):
        kb = _bf(k_ref[0])
        vb = _bf(v_ref[0])
        for i in range(s // tq):
            ext = (i + 1) * tq
            p = _attn_probs(_bf(q_ref[0, i * tq:ext, :]), kb[:ext], i, tq)
            o_ref[0, i * tq:ext, :] = lax.dot_general(_bf(p), vb[:ext], (((1,), (0,)), ((), ())), preferred_element_type=F32)

    return pl.pallas_call(body, name="attn_fwd", grid=(b, HEADS), in_specs=[blk, blk, blk], out_specs=blk,
                          out_shape=_sds((b, s, hp)), compiler_params=_params(2))(q, k, v)


def _attn_bwd(q, k, v, do, tq):
    b, s, hp = q.shape
    blk = pl.BlockSpec((1, s, HEAD_PAD), lambda bi, h: (bi, 0, h))

    def body(q_ref, k_ref, v_ref, do_ref, dq_ref, dk_ref, dv_ref):
        kb = _bf(k_ref[0])
        vb = _bf(v_ref[0])
        dk_ref[...] = jnp.zeros_like(dk_ref)
        dv_ref[...] = jnp.zeros_like(dv_ref)
        for i in range(s // tq):
            ext = (i + 1) * tq
            qb = _bf(q_ref[0, i * tq:ext, :])
            dob = _bf(do_ref[0, i * tq:ext, :])
            p = _attn_probs(qb, kb[:ext], i, tq)
            dv_ref[0, :ext, :] += lax.dot_general(_bf(p), dob, (((0,), (0,)), ((), ())), preferred_element_type=F32)
            dp = lax.dot_general(dob, vb[:ext], (((1,), (1,)), ((), ())), preferred_element_type=F32)
            ds = _bf(p * (dp - jnp.sum(p * dp, axis=1, keepdims=True)) * ATTN_SCALE)
            dq_ref[0, i * tq:ext, :] = lax.dot_general(ds, kb[:ext], (((1,), (0,)), ((), ())), preferred_element_type=F32)
            dk_ref[0, :ext, :] += lax.dot_general(ds, qb, (((0,), (0,)), ((), ())), preferred_element_type=F32)

    return pl.pallas_call(body, name="attn_bwd", grid=(b, HEADS), in_specs=[blk, blk, blk, blk], out_specs=[blk, blk, blk],
                          out_shape=[_sds((b, s, hp))] * 3, compiler_params=_params(2))(q, k, v, do)


LRU_CONV = 4


def _lru_point(pre_a, pre_x, xc, lam):
    r = jax.nn.sigmoid(pre_a)
    i = jax.nn.sigmoid(pre_x)
    a, gap = _decay(-LRU_C * r * _softplus(-lam))
    return a, jnp.sqrt(gap) * (i * xc)


def _causal_conv(pad_ref, x, halo, first_in_seq, w, taps):
    tm = x.shape[0]
    pad_ref[:HALO, :] = jnp.where(first_in_seq, 0.0, halo)
    pad_ref[HALO:, :] = x
    y = w[taps - 1:taps, :] * x
    for k in range(taps - 1):
        off = HALO - (taps - 1) + k
        y = y + w[k:k + 1, :] * pad_ref[off:off + tm, :]
    return y


def _conv_taps(pad_ref, r, cols, taps):
    blocks = [pad_ref[r + j * HALO:r + (j + 1) * HALO, cols] for j in range(1 + STRIP // HALO)]
    sub = lax.broadcasted_iota(jnp.int32, blocks[0].shape, 0)
    out = []
    for k in range(taps - 1):
        s = taps - 1 - k
        rolled = [pltpu.roll(b, s, 0) for b in blocks]
        out.append(jnp.concatenate([jnp.where(sub < s, rolled[j], rolled[j + 1]) for j in range(STRIP // HALO)], axis=0))
    out.append(jnp.concatenate(blocks[1:], axis=0))
    return out


def _causal_conv_wgrad(pad_ref, dy, taps):
    tm = dy.shape[0]
    return jnp.concatenate([_colsum(dy * pad_ref[HALO - (taps - 1) + k:HALO - (taps - 1) + k + tm, :]) for k in range(taps)], axis=0)


def _causal_conv_transpose(pad_ref, dy, halo_next, last_in_seq, w, taps):
    tm = dy.shape[0]
    pad_ref[:tm, :] = dy
    pad_ref[tm:, :] = jnp.where(last_in_seq, 0.0, halo_next)
    dx = w[taps - 1:taps, :] * dy
    for k in range(taps - 1):
        off = (taps - 1) - k
        dx = dx + w[k:k + 1, :] * pad_ref[off:off + tm, :]
    return dx


def _lru_fwd(xl, gate, w, ts, seq):
    t, n = xl.shape
    tiles_per_seq = seq // ts

    def body(xl_ref, halo_ref, gate_ref, cw_ref, cb_ref, wa_ref, ba_ref, wx_ref, bx_ref, lam_ref, y_out, h_out, pad_ref, a_ref, b_ref, carry_ref):
        first_in_seq = pl.program_id(0) % tiles_per_seq == 0
        xc = _causal_conv(pad_ref, xl_ref[...], halo_ref[...], first_in_seq, cw_ref[...], LRU_CONV) + cb_ref[...]
        a, bx = _lru_point(_nn(xc, wa_ref[...]) + ba_ref[...], _nn(xc, wx_ref[...]) + bx_ref[...], xc, lam_ref[...])
        a_ref[...] = a
        b_ref[...] = bx

        @pl.when(first_in_seq)
        def _():
            carry_ref[...] = jnp.zeros_like(carry_ref)

        def step(r, h):
            h = a_ref[pl.ds(r, 1), :] * h + b_ref[pl.ds(r, 1), :]
            h_out[pl.ds(r, 1), :] = h
            return h

        carry_ref[...] = lax.fori_loop(0, ts, step, carry_ref[...], unroll=8)
        y_out[...] = h_out[...] * _gelu(gate_ref[...])

    return pl.pallas_call(
        body, name="lru_fwd", grid=(t // ts,),
        in_specs=[_row(ts, n), _prev_halo(ts, n), _row(ts, n), _const((LRU_CONV, n)), _const((1, n)), _const((n, n)), _const((1, n)),
                  _const((n, n)), _const((1, n)), _const((1, n))],
        out_specs=[_row(ts, n), _row(ts, n)], out_shape=[_sds((t, n)), _sds((t, n))],
        scratch_shapes=[pltpu.VMEM((HALO + ts, n), F32), pltpu.VMEM((ts, n), F32), pltpu.VMEM((ts, n), F32), pltpu.VMEM((1, n), F32)],
        compiler_params=_params(),
    )(xl, xl, gate, w['ab_conv_w'], w['ab_conv_b'], w['Wa'], w['ab_b_rg_a'], w['Wx'], w['ab_b_rg_x'], w['ab_lambda'])


def _lru_bwd(xl, gate, hs, dy, w, ts, seq):
    t, n = xl.shape
    tiles_per_seq = seq // ts
    n_tiles = t // ts

    def rev(i):
        return n_tiles - 1 - i

    row = pl.BlockSpec((ts, n), lambda i: (rev(i), 0))
    prev = pl.BlockSpec((HALO, n), lambda i: (jnp.maximum(rev(i) * (ts // HALO) - 1, 0), 0))
    acc = lambda shape: pl.BlockSpec(shape, lambda i: (0,) * len(shape))

    def body(xl_ref, xhalo_ref, gate_ref, h_ref, hhalo_ref, dy_ref, cw_ref, cb_ref, wa_ref, ba_ref, wx_ref, bx_ref, lam_ref,
             dxl_out, dgate_out, dcw_out, dcb_out, dwa_out, dba_out, dwx_out, dbx_out, dlam_out,
             pad_ref, padh_ref, padd_ref, a_ref, g_ref, carry_ref, dhalo_ref):
        step_id = pl.program_id(0)
        first = step_id == 0
        tile = rev(step_id)
        first_in_seq = tile % tiles_per_seq == 0
        last_in_seq = tile % tiles_per_seq == tiles_per_seq - 1
        cw = cw_ref[...]
        xc = _causal_conv(pad_ref, xl_ref[...], xhalo_ref[...], first_in_seq, cw, LRU_CONV) + cb_ref[...]
        pre_a = _nn(xc, wa_ref[...]) + ba_ref[...]
        pre_x = _nn(xc, wx_ref[...]) + bx_ref[...]
        (a, _), vjp_point = jax.vjp(_lru_point, pre_a, pre_x, xc, lam_ref[...])
        h = h_ref[...]
        _, vjp_out = jax.vjp(lambda h_, g_: h_ * _gelu(g_), h, gate_ref[...])
        dh, dgate = vjp_out(dy_ref[...])
        dgate_out[...] = dgate
        a_ref[...] = a
        g_ref[...] = dh

        @pl.when(last_in_seq)
        def _():
            carry_ref[...] = jnp.zeros_like(carry_ref)

        def step(j, c):
            r = ts - 1 - j
            g = g_ref[pl.ds(r, 1), :] + c
            g_ref[pl.ds(r, 1), :] = g
            return a_ref[pl.ds(r, 1), :] * g

        carry_ref[...] = lax.fori_loop(0, ts, step, carry_ref[...], unroll=8)
        g = g_ref[...]
        padh_ref[:HALO, :] = jnp.where(first_in_seq, 0.0, hhalo_ref[...])
        padh_ref[HALO:, :] = h
        dpre_a, dpre_x, dxc, dlam = vjp_point((g * padh_ref[HALO - 1:HALO - 1 + ts, :], g))
        dxc = dxc + _nt(dpre_a, wa_ref[...]) + _nt(dpre_x, wx_ref[...])
        _accumulate(dwa_out, _tn(xc, dpre_a), first)
        _accumulate(dwx_out, _tn(xc, dpre_x), first)
        _accumulate(dba_out, _colsum(dpre_a), first)
        _accumulate(dbx_out, _colsum(dpre_x), first)
        _accumulate(dlam_out, dlam, first)
        _accumulate(dcb_out, _colsum(dxc), first)
        _accumulate(dcw_out, _causal_conv_wgrad(pad_ref, dxc, LRU_CONV), first)
        dxl_out[...] = _causal_conv_transpose(padd_ref, dxc, dhalo_ref[...], last_in_seq, cw, LRU_CONV)
        dhalo_ref[...] = dxc[:HALO, :]

    return pl.pallas_call(
        body, name="lru_bwd", grid=(n_tiles,),
        in_specs=[row, prev, row, row, prev, row, _const((LRU_CONV, n)), _const((1, n)), _const((n, n)), _const((1, n)),
                  _const((n, n)), _const((1, n)), _const((1, n))],
        out_specs=[row, row, acc((LRU_CONV, n)), acc((1, n)), acc((n, n)), acc((1, n)), acc((n, n)), acc((1, n)), acc((1, n))],
        out_shape=[_sds((t, n)), _sds((t, n)), _sds((LRU_CONV, n)), _sds((1, n)), _sds((n, n)), _sds((1, n)), _sds((n, n)),
                   _sds((1, n)), _sds((1, n))],
        scratch_shapes=[pltpu.VMEM((HALO + ts, n), F32), pltpu.VMEM((HALO + ts, n), F32), pltpu.VMEM((ts + HALO, n), F32),
                        pltpu.VMEM((ts, n), F32), pltpu.VMEM((ts, n), F32), pltpu.VMEM((1, n), F32), pltpu.VMEM((HALO, n), F32)],
        compiler_params=_params(),
    )(xl, xl, gate, hs, hs, dy, w['ab_conv_w'], w['ab_conv_b'], w['Wa'], w['ab_b_rg_a'], w['Wx'], w['ab_b_rg_x'], w['ab_lambda'])


def _ab_out_fwd(x, o, y, w, tm):
    t, d = x.shape
    hp = o.shape[1]

    def body(x_ref, o_ref, y_ref, wa_ref, wb_ref, h_out):
        h_out[...] = x_ref[...] + _nn(o_ref[...], wa_ref[...]) + _nn(y_ref[...], wb_ref[...])

    return pl.pallas_call(body, name="ab_out_fwd", grid=(t // tm,),
                          in_specs=[_row(tm, d), _row(tm, hp), _row(tm, LRU_W), _const((hp, d)), _const((LRU_W, d))],
                          out_specs=_row(tm, d), out_shape=_sds((t, d)), compiler_params=_params())(x, o, y, w['Wo_a'], w['Wo_b'])


def _ab_out_bwd(o, y, dh, w, tm):
    t, d = dh.shape
    hp = o.shape[1]

    def body(o_ref, y_ref, dh_ref, wa_ref, wb_ref, do_out, dy_out, dwa_out, dwb_out):
        first = pl.program_id(0) == 0
        dh_t = dh_ref[...]
        do_out[...] = _nt(dh_t, wa_ref[...])
        dy_out[...] = _nt(dh_t, wb_ref[...])
        _accumulate(dwa_out, _tn(o_ref[...], dh_t), first)
        _accumulate(dwb_out, _tn(y_ref[...], dh_t), first)

    return pl.pallas_call(body, name="ab_out_bwd", grid=(t // tm,),
                          in_specs=[_row(tm, hp), _row(tm, LRU_W), _row(tm, d), _const((hp, d)), _const((LRU_W, d))],
                          out_specs=[_row(tm, hp), _row(tm, LRU_W), _const((hp, d)), _const((LRU_W, d))],
                          out_shape=[_sds((t, hp)), _sds((t, LRU_W)), _sds((hp, d)), _sds((LRU_W, d))],
                          compiler_params=_params())(o, y, dh, w['Wo_a'], w['Wo_b'])


FFN_CONV = 3


def _ffn_a_fwd(h, norm, wg, wu, tm):
    t, d = h.shape
    fb = D_FF // FF_BLOCKS

    def body(h_ref, gn_ref, wg_ref, wu_ref, g_out, u_out, hn_out):
        hn = _bf(_rms(h_ref[...], gn_ref[...]))
        hn_out[0] = hn
        g_out[...] = _nt(hn, wg_ref[...])
        u_out[...] = _nt(hn, wu_ref[...])

    wspec = pl.BlockSpec((fb, d), lambda f, i: (f, 0))
    ospec = pl.BlockSpec((tm, fb), lambda f, i: (i, f))
    return pl.pallas_call(
        body, name="ffn_a_fwd", grid=(FF_BLOCKS, t // tm),
        in_specs=[pl.BlockSpec((tm, d), lambda f, i: (i, 0)), pl.BlockSpec((1, d), lambda f, i: (0, 0)), wspec, wspec],
        out_specs=[ospec, ospec, pl.BlockSpec((1, tm, d), lambda f, i: (f, i, 0))],
        out_shape=[_sds((t, D_FF)), _sds((t, D_FF)), _sds((FF_BLOCKS, t, d), BF16)], compiler_params=_params(2))(h, norm, wg, wu)


def _ffn_b_fwd(g, u, h, cw, cb, wd, tm, seq):
    t, d = h.shape
    tiles_per_seq = seq // tm

    def body(g_ref, halo_ref, u_ref, h_ref, cw_ref, cb_ref, wd_ref, h_out, pad_ref, act_ref):
        pad_ref[:HALO, :] = jnp.where(pl.program_id(0) % tiles_per_seq == 0, 0.0, halo_ref[...])
        pad_ref[HALO:, :] = g_ref[...]
        cw = cw_ref[...]
        cb = cb_ref[...]
        for c0 in range(0, D_FF, STRIP_LANES):
            cols = slice(c0, min(c0 + STRIP_LANES, D_FF))
            for r in range(0, tm, STRIP):
                taps = _conv_taps(pad_ref, r, cols, FFN_CONV)
                gc = cb[:, cols] + cw[0:1, cols] * taps[0] + cw[1:2, cols] * taps[1] + cw[2:3, cols] * taps[2]
                act_ref[r:r + STRIP, cols] = _bf(_gelu(gc) * u_ref[r:r + STRIP, cols])
        h_out[...] = h_ref[...] + _nn(act_ref[...], wd_ref[...])

    return pl.pallas_call(body, name="ffn_b_fwd", grid=(t // tm,),
                          in_specs=[_row(tm, D_FF), _prev_halo(tm, D_FF), _row(tm, D_FF), _row(tm, d), _const((FFN_CONV, D_FF)),
                                    _const((1, D_FF)), _const((D_FF, d))],
                          out_specs=_row(tm, d), out_shape=_sds((t, d)),
                          scratch_shapes=[pltpu.VMEM((HALO + tm, D_FF), F32), pltpu.VMEM((tm, D_FF), BF16)],
                          compiler_params=_params())(g, g, u, h, cw, cb, wd)


def _ffn_b_bwd(g, u, dout, cw, cb, wd, tm, seq):
    t, d = dout.shape
    fb = D_FF // FF_BLOCKS
    tiles_per_seq = seq // tm

    def body(g_ref, halo_ref, u_ref, dout_ref, cw_ref, cb_ref, wd_ref, dgc_out, du_out, dwd_out, dcw_out, dcb_out,
             pad_ref, dact_ref, act_ref, acc_ref, dwd_acc):
        i = pl.program_id(1)
        first = i == 0
        pad_ref[:HALO, :] = jnp.where(i % tiles_per_seq == 0, 0.0, halo_ref[...])
        pad_ref[HALO:, :] = g_ref[...]
        dout_b = _bf(dout_ref[...])
        dact_ref[...] = _nt(dout_b, wd_ref[...])
        cw = cw_ref[...]
        cb = cb_ref[...]
        fold = lambda a: a[:HALO] + a[HALO:]
        for c0 in range(0, fb, STRIP_LANES):
            cols = slice(c0, min(c0 + STRIP_LANES, fb))
            sums = [jnp.zeros((HALO, cols.stop - c0), F32) for _ in range(1 + FFN_CONV)]
            for r in range(0, tm, STRIP):
                rows = slice(r, r + STRIP)
                taps = _conv_taps(pad_ref, r, cols, FFN_CONV)
                gelu, dgelu = _gelu_and_grad(cb[:, cols] + cw[0:1, cols] * taps[0] + cw[1:2, cols] * taps[1] + cw[2:3, cols] * taps[2])
                u = u_ref[rows, cols]
                dact = dact_ref[rows, cols]
                act_ref[rows, cols] = _bf(gelu * u)
                du_out[rows, cols] = _bf(dact * gelu)
                dgc = dact * u * dgelu
                dgc_out[rows, cols] = dgc
                sums = [sums[0] + fold(dgc)] + [sums[1 + k] + fold(dgc * taps[k]) for k in range(FFN_CONV)]
            for k in range(1 + FFN_CONV):
                acc_ref[k, :, cols] = sums[k]
        _accumulate(dwd_acc, _tn(act_ref[...], dout_b), first)

        @pl.when(i == t // tm - 1)
        def _():
            dwd_out[...] = _bf(dwd_acc[...])

        _accumulate(dcb_out, _colsum(acc_ref[0]), first)
        _accumulate(dcw_out, jnp.concatenate([_colsum(acc_ref[1 + k]) for k in range(FFN_CONV)], axis=0), first)

    blk = pl.BlockSpec((tm, fb), lambda f, i: (i, f))
    halo = pl.BlockSpec((HALO, fb), lambda f, i: (jnp.maximum(i * (tm // HALO) - 1, 0), f))
    wd_blk = pl.BlockSpec((fb, d), lambda f, i: (f, 0), pipeline_mode=pl.Buffered(1))
    return pl.pallas_call(
        body, name="ffn_b_bwd", grid=(FF_BLOCKS, t // tm),
        in_specs=[blk, halo, blk, pl.BlockSpec((tm, d), lambda f, i: (i, 0)), pl.BlockSpec((FFN_CONV, fb), lambda f, i: (0, f)),
                  pl.BlockSpec((1, fb), lambda f, i: (0, f)), wd_blk],
        out_specs=[blk, blk, wd_blk, pl.BlockSpec((FFN_CONV, fb), lambda f, i: (0, f)),
                   pl.BlockSpec((1, fb), lambda f, i: (0, f))],
        out_shape=[_sds((t, D_FF)), _sds((t, D_FF), BF16), _sds((D_FF, d), BF16), _sds((FFN_CONV, D_FF)), _sds((1, D_FF))],
        scratch_shapes=[pltpu.VMEM((HALO + tm, fb), F32), pltpu.VMEM((tm, fb), F32), pltpu.VMEM((tm, fb), BF16),
                        pltpu.VMEM((1 + FFN_CONV, HALO, fb), F32), pltpu.VMEM((fb, d), F32)],
        compiler_params=_params(2))(g, g, u, dout, cw, cb, wd)


def _ffn_a_dgrad(h, norm, dgc, du, dres, cw, wg, wu, tm, seq):
    t, d = h.shape
    tiles_per_seq = seq // tm
    n_tiles = t // tm

    def body(h_ref, gn_ref, dgc_ref, halo_ref, du_ref, dres_ref, cw_ref, wg_ref, wu_ref, dh_out, dg_out, dgn_out, pad_ref):
        i = pl.program_id(0)
        last_in_seq = i % tiles_per_seq == tiles_per_seq - 1
        dg = _bf(_causal_conv_transpose(pad_ref, dgc_ref[...], halo_ref[...], last_in_seq, cw_ref[...], FFN_CONV))
        dg_out[...] = dg
        _, vjp_norm = jax.vjp(_rms, h_ref[...], gn_ref[...])
        dh, dgn = vjp_norm(_nn(dg, wg_ref[...]) + _nn(du_ref[...], wu_ref[...]))
        dh_out[...] = dh + dres_ref[...]
        _accumulate(dgn_out, dgn, i == 0)

    return pl.pallas_call(
        body, name="ffn_a_dgrad", grid=(n_tiles,),
        in_specs=[_row(tm, d), _const((1, d)), _row(tm, D_FF), _next_halo(tm, D_FF, n_tiles), _row(tm, D_FF), _row(tm, d),
                  _const((FFN_CONV, D_FF)), _const((D_FF, d)), _const((D_FF, d))],
        out_specs=[_row(tm, d), _row(tm, D_FF), _const((1, d))], out_shape=[_sds((t, d)), _sds((t, D_FF), BF16), _sds((1, d))],
        scratch_shapes=[pltpu.VMEM((tm + HALO, D_FF), F32)], compiler_params=_params())(h, norm, dgc, dgc, du, dres, cw, wg, wu)


def _ffn_a_wgrad(hn, dg, du, tm):
    _, t, d = hn.shape
    fb = D_FF // FF_BLOCKS

    n_tiles = t // tm

    def body(hn_ref, dg_ref, du_ref, dwg_out, dwu_out, acc_g, acc_u):
        i = pl.program_id(1)
        hn_t = hn_ref[0]
        _accumulate(acc_g, _tn(dg_ref[...], hn_t), i == 0)
        _accumulate(acc_u, _tn(du_ref[...], hn_t), i == 0)

        @pl.when(i == n_tiles - 1)
        def _():
            dwg_out[...] = _bf(acc_g[...])
            dwu_out[...] = _bf(acc_u[...])

    blk = pl.BlockSpec((tm, fb), lambda f, i: (i, f))
    wspec = pl.BlockSpec((fb, d), lambda f, i: (f, 0), pipeline_mode=pl.Buffered(1))
    return pl.pallas_call(body, name="ffn_a_wgrad", grid=(FF_BLOCKS, n_tiles),
                          in_specs=[pl.BlockSpec((1, tm, d), lambda f, i: (0, i, 0)), blk, blk],
                          out_specs=[wspec, wspec], out_shape=[_sds((D_FF, d), BF16), _sds((D_FF, d), BF16)],
                          scratch_shapes=[pltpu.VMEM((fb, d), F32), pltpu.VMEM((fb, d), F32)],
                          compiler_params=_params(2))(hn, dg, du)


def _sgu_mix(vn, ws_ref, bst):
    tril = lax.broadcasted_iota(jnp.int32, (CHUNK, CHUNK), 0) >= lax.broadcasted_iota(jnp.int32, (CHUNK, CHUNK), 1)
    wms = [jnp.where(tril, ws_ref[g], 0.0) for g in range(SGU_GROUPS)]
    chunks = []
    for n in range(vn.shape[0] // CHUNK):
        vc = vn[n * CHUNK:(n + 1) * CHUNK, :]
        chunks.append(jnp.concatenate(
            [_nn(wms[g], vc[:, g * CHUNK:(g + 1) * CHUNK]) + bst[:, g:g + 1] for g in range(SGU_GROUPS)], axis=1))
    return jnp.concatenate(chunks, axis=0)


def _sgu_fwd(h, w, tm):
    t, d = h.shape

    def body(h_ref, cn_ref, win_ref, lg_ref, lb_ref, ws_ref, bst_ref, wout_ref, h_out):
        h_t = h_ref[...]
        z = _gelu(_nn(_rms(h_t, cn_ref[...]), win_ref[...]))
        vn = _layer_norm(z[:, d:], lg_ref[...], lb_ref[...])
        s = _sgu_mix(vn, ws_ref, bst_ref[...])
        h_out[...] = h_t + _nn(z[:, :d] * s, wout_ref[...])

    return pl.pallas_call(
        body, name="sgu_fwd", grid=(t // tm,),
        in_specs=[_row(tm, d), _const((1, d)), _const((d, 2 * d)), _const((1, d)), _const((1, d)), _const((SGU_GROUPS, CHUNK, CHUNK)),
                  _const((CHUNK, LANES)), _const((d, d))],
        out_specs=_row(tm, d), out_shape=_sds((t, d)), compiler_params=_params(),
    )(h, w['c_norm'], w['c_w_in'], w['c_ln_g'], w['c_ln_b'], w['c_w_s'], w['bsT'], w['c_w_out'])


def _sgu_bwd(h, dout, w, tm):
    t, d = h.shape

    def body(h_ref, dout_ref, cn_ref, win_ref, lg_ref, lb_ref, ws_ref, bst_ref, wout_ref,
             dh_out, dcn_out, dwin_out, dlg_out, dlb_out, dws_out, dbst_out, dwout_out):
        first = pl.program_id(0) == 0
        hn, vjp_norm = jax.vjp(_rms, h_ref[...], cn_ref[...])
        zpre = _nn(hn, win_ref[...])
        u, vjp_u = jax.vjp(_gelu, zpre[:, :d])
        vn, vjp_v = jax.vjp(lambda zp, lg, lb: _layer_norm(_gelu(zp), lg, lb), zpre[:, d:], lg_ref[...], lb_ref[...])
        s = _sgu_mix(vn, ws_ref, bst_ref[...])
        dout_t = dout_ref[...]
        dus = _nt(dout_t, wout_ref[...])
        _accumulate(dwout_out, _tn(u * s, dout_t), first)
        ds = dus * u
        tril = lax.broadcasted_iota(jnp.int32, (CHUNK, CHUNK), 0) >= lax.broadcasted_iota(jnp.int32, (CHUNK, CHUNK), 1)
        lane = lax.broadcasted_iota(jnp.int32, (CHUNK, LANES), 1)
        dws = [jnp.zeros((CHUNK, CHUNK), F32) for _ in range(SGU_GROUPS)]
        dbst = jnp.zeros((CHUNK, LANES), F32)
        dvn_chunks = []
        for n in range(tm // CHUNK):
            cols = []
            for g in range(SGU_GROUPS):
                ds_ng = ds[n * CHUNK:(n + 1) * CHUNK, g * CHUNK:(g + 1) * CHUNK]
                vc_ng = vn[n * CHUNK:(n + 1) * CHUNK, g * CHUNK:(g + 1) * CHUNK]
                cols.append(_tn(jnp.where(tril, ws_ref[g], 0.0), ds_ng))
                dws[g] = dws[g] + _nt(ds_ng, vc_ng)
                dbst = dbst + jnp.where(lane == g, jnp.sum(ds_ng, axis=1, keepdims=True), 0.0)
            dvn_chunks.append(jnp.concatenate(cols, axis=1))
        dvn = jnp.concatenate(dvn_chunks, axis=0)
        for g in range(SGU_GROUPS):
            val = jnp.where(tril, dws[g], 0.0)

            @pl.when(first)
            def _():
                dws_out[g] = val

            @pl.when(jnp.logical_not(first))
            def _():
                dws_out[g] += val
        _accumulate(dbst_out, dbst, first)
        (dzu,) = vjp_u(dus * s)
        dzv, dlg, dlb = vjp_v(dvn)
        _accumulate(dlg_out, dlg, first)
        _accumulate(dlb_out, dlb, first)
        dzpre = jnp.concatenate([dzu, dzv], axis=1)
        _accumulate(dwin_out, _tn(hn, dzpre), first)
        dh, dcn = vjp_norm(_nt(dzpre, win_ref[...]))
        _accumulate(dcn_out, dcn, first)
        dh_out[...] = dh + dout_t

    return pl.pallas_call(
        body, name="sgu_bwd", grid=(t // tm,),
        in_specs=[_row(tm, d), _row(tm, d), _const((1, d)), _const((d, 2 * d)), _const((1, d)), _const((1, d)),
                  _const((SGU_GROUPS, CHUNK, CHUNK)), _const((CHUNK, LANES)), _const((d, d))],
        out_specs=[_row(tm, d), _const((1, d)), _const((d, 2 * d)), _const((1, d)), _const((1, d)), _const((SGU_GROUPS, CHUNK, CHUNK)),
                   _const((CHUNK, LANES)), _const((d, d))],
        out_shape=[_sds((t, d)), _sds((1, d)), _sds((d, 2 * d)), _sds((1, d)), _sds((1, d)), _sds((SGU_GROUPS, CHUNK, CHUNK)),
                   _sds((CHUNK, LANES)), _sds((d, d))],
        compiler_params=_params(),
    )(h, dout, w['c_norm'], w['c_w_in'], w['c_ln_g'], w['c_ln_b'], w['c_w_s'], w['bsT'], w['c_w_out'])


def _final_loss(h, target, norm, tm):
    t, d = h.shape

    def body(h_ref, tgt_ref, gn_ref, dh_out, loss_out, dgn_out):
        first = pl.program_id(0) == 0
        tgt = tgt_ref[...]

        def loss_fn(h_, g_):
            err = _rms(h_, g_) - tgt
            return 0.5 * jnp.sum(jnp.mean(err * err, axis=-1, keepdims=True), axis=0, keepdims=True)

        loss, vjp_loss = jax.vjp(loss_fn, h_ref[...], gn_ref[...])
        dh, dgn = vjp_loss(jnp.ones((1, 1), F32))
        dh_out[...] = dh
        _accumulate(loss_out, loss, first)
        _accumulate(dgn_out, dgn, first)

    return pl.pallas_call(body, name="final_loss", grid=(t // tm,), in_specs=[_row(tm, d), _row(tm, d), _const((1, d))],
                          out_specs=[_row(tm, d), _const((1, 1)), _const((1, d))],
                          out_shape=[_sds((t, d)), _sds((1, 1)), _sds((1, d))], compiler_params=_params())(h, target, norm)


def _tile(t, seq, want):
    tm = min(want, seq)
    assert seq % tm == 0 and t % tm == 0 and tm % CHUNK == 0
    return tm


def _local_step(x, posb, target, w, seq, late_weights, on_grads):
    t, d = x.shape
    b = t // seq
    hp = HEADS * HEAD_PAD
    tm_big, tm_mid = _tile(t, seq, 512), _tile(t, seq, 256)
    tq = _tile(t, seq, 512)

    q, k, v, xl, gate = _ab_in_fwd(x, posb, w, tm_big)
    o = _attn_fwd(q.reshape(b, seq, hp), k.reshape(b, seq, hp), v.reshape(b, seq, hp), tq).reshape(t, hp)
    y, hs = _lru_fwd(xl, gate, w, tm_big, seq)
    w = {**w, **late_weights('out0', y)}
    h1 = _ab_out_fwd(x, o, y, w, tm_big)
    hcur = h1
    saved = []
    for l in range(2):
        if l == 1:
            w = {**w, **late_weights('mix1', hcur)}
            saved_h2 = hcur
            hcur = _sgu_fwd(hcur, w, tm_mid)
        wl = late_weights('ffn%d' % l, hcur)
        g, u, hn = _ffn_a_fwd(hcur, w['ffn_norm'][l], wl['Wg'], wl['Wu'], tm_big)
        hnext = _ffn_b_fwd(g, u, hcur, w['ffn_conv_w'][l], w['ffn_conv_b'][l], wl['Wd'], tm_mid, seq)
        saved.append((hcur, g, u, wl, hn))
        hcur = hnext
    dh, loss, d_final = _final_loss(hcur, target, w['final_norm'], tm_big)

    ffn = {}
    conv_b = list(w['ffn_conv_b'])
    for l in (1, 0):
        hin, g, u, wl, hn = saved[l]
        dgc, du, d_wd, d_cw, d_cb = _ffn_b_bwd(g, u, dh, w['ffn_conv_w'][l], conv_b[l], wl['Wd'], tm_big, seq)
        dh, dg, d_norm = _ffn_a_dgrad(hin, w['ffn_norm'][l], dgc, du, dh, w['ffn_conv_w'][l], wl['Wg'], wl['Wu'], tm_mid, seq)
        d_wg, d_wu = _ffn_a_wgrad(hn, dg, du, _tile(t, seq, 1024))
        ffn[l] = dict(ffn_norm=d_norm, ffn_conv_w=d_cw, ffn_conv_b=d_cb, Wg=d_wg, Wu=d_wu, Wd=d_wd)
        if l == 1:
            dh, d_cn, d_cwin, d_lg, d_lb, d_ws, d_bst, d_cwout = _sgu_bwd(saved_h2, dh, w, tm_mid)
            zero = on_grads('late1', dict(final_norm=d_final, c_norm=d_cn, c_ln_g=d_lg, c_ln_b=d_lb, c_w_s=d_ws, bsT=d_bst, c_w_in=d_cwin,
                                          c_w_out=d_cwout, Wg=[d_wg], Wu=[d_wu], Wd=[d_wd]))
            conv_b[0] = conv_b[0] + zero
    late0 = {name: [ffn[0][name], ffn[1][name]] for name in ('ffn_norm', 'ffn_conv_w', 'ffn_conv_b')}
    zero = on_grads('late0', dict(late0, Wg=[ffn[0]['Wg']], Wu=[ffn[0]['Wu']], Wd=[ffn[0]['Wd']]))
    w = {**w, 'Wo_b': w['Wo_b'] + zero.astype(w['Wo_b'].dtype)}
    do, dy, d_woa, d_wob = _ab_out_bwd(o, y, dh, w, tm_big)
    dxl, dgate, d_cw, d_cb, d_wa, d_ba, d_wx, d_bx, d_lam = _lru_bwd(xl, gate, hs, dy, w, tm_big, seq)
    zero = on_grads('mid', dict(Wo_a=d_woa, Wo_b=d_wob, ab_conv_w=d_cw, ab_conv_b=d_cb, Wa=d_wa, ab_b_rg_a=d_ba, Wx=d_wx,
                                ab_b_rg_x=d_bx, ab_lambda=d_lam))
    w = {**w, 'ab_norm': w['ab_norm'] + zero}
    dq, dk, dv = _attn_bwd(q.reshape(b, seq, hp), k.reshape(b, seq, hp), v.reshape(b, seq, hp), do.reshape(b, seq, hp), tq)
    dx, d_gn, d_win, d_qn, d_wq, d_kvn, d_wk, d_wv = _ab_in_bwd(
        x, posb, w, dq.reshape(t, hp), dk.reshape(t, hp), dv.reshape(t, hp), dxl, dgate, dh, tm_mid)
    return loss, dx, dict(ab_norm=d_gn, W_in=d_win, ab_q_norm=d_qn, Wq=d_wq, ab_kv_norm=d_kvn, Wk=d_wk, Wv=d_wv)


def _block_diag(wg):
    g, n, _ = wg.shape
    return jnp.einsum('gij,gh->gihj', wg, jnp.eye(g, dtype=wg.dtype)).reshape(g * n, g * n)


def _prepare_out(w_out):
    d = w_out.shape[2]
    mla = HEADS * QK_NOPE
    return {'Wo_a': jnp.pad(w_out[0, :mla].reshape(HEADS, QK_NOPE, d), ((0, 0), (0, HEAD_PAD - QK_NOPE), (0, 0))).reshape(HEADS * HEAD_PAD, d),
            'Wo_b': w_out[0, mla:]}


def _prepare(full):
    d = full['ab_w_in'].shape[1]
    w_in = full['ab_w_in'][0]
    zeros = lambda n: jnp.zeros((d, n), w_in.dtype)
    wq = full['ab_w_q_b'][0].reshape(Q_LORA, HEADS, QK_NOPE + QK_ROPE)
    wkv = full['ab_w_kv_b'][0].reshape(KV_LORA, HEADS, 2 * QK_NOPE)
    pad_head = lambda a: jnp.pad(a, ((0, 0), (0, 0), (0, HEAD_PAD - a.shape[2]))).reshape(a.shape[0], HEADS * HEAD_PAD)
    w = {
        'W_in': jnp.concatenate([w_in[:, :Z_KPE], zeros(QK_NOPE), w_in[:, Z_KPE:Z_KPE + QK_ROPE],
                                 zeros(HEAD_PAD - QK_NOPE - QK_ROPE), w_in[:, Z_KPE + QK_ROPE:]], axis=1),
        'Wq': pad_head(wq), 'Wk': pad_head(wkv[:, :, :QK_NOPE]), 'Wv': pad_head(wkv[:, :, QK_NOPE:]),
        'Wa': _bf(_block_diag(full['ab_w_rg_a'][0])), 'Wx': _bf(_block_diag(full['ab_w_rg_x'][0])),
        'c_w_s': full['c_w_s'][0],
        'bsT': jnp.pad(full['c_b_s'][0].T, ((0, 0), (0, LANES - SGU_GROUPS))),
        'ffn_norm': [full['ffn_norm'][l:l + 1] for l in range(2)], 'ffn_conv_w': [full['ffn_conv_w'][l] for l in range(2)],
        'ffn_conv_b': [full['ffn_conv_b'][l:l + 1] for l in range(2)],
        'ab_conv_w': full['ab_conv_w'][0], 'final_norm': full['final_norm'][None, :],
    }
    for name in ('ab_norm', 'ab_q_norm', 'ab_kv_norm', 'ab_conv_b', 'ab_b_rg_a', 'ab_b_rg_x', 'ab_lambda', 'c_norm', 'c_ln_g', 'c_ln_b'):
        w[name] = full[name]
    return w


def _unprepare(g):
    unpad_head = lambda a, n: a.reshape(a.shape[0], HEADS, HEAD_PAD)[:, :, :n]
    diag = lambda a: jnp.einsum('gigj->gij', a.reshape(HEADS, LRU_W // HEADS, HEADS, LRU_W // HEADS))
    rules = {
        'ab_w_in': (('W_in',), lambda a: jnp.concatenate([a[:, :Z_KPE], a[:, Z_KPE + QK_NOPE:Z_KPE + QK_NOPE + QK_ROPE], a[:, Z_LRU:]], axis=1)[None]),
        'ab_w_q_b': (('Wq',), lambda a: unpad_head(a, QK_NOPE + QK_ROPE).reshape(1, Q_LORA, -1)),
        'ab_w_kv_b': (('Wk', 'Wv'), lambda a, b: jnp.concatenate([unpad_head(a, QK_NOPE), unpad_head(b, QK_NOPE)], axis=2).reshape(1, KV_LORA, -1)),
        'ab_w_out': (('Wo_a', 'Wo_b'), lambda a, b: jnp.concatenate(
            [a.reshape(HEADS, HEAD_PAD, -1)[:, :QK_NOPE].reshape(HEADS * QK_NOPE, -1), b], axis=0)[None]),
        'ab_w_rg_a': (('Wa',), lambda a: diag(a)[None]), 'ab_w_rg_x': (('Wx',), lambda a: diag(a)[None]),
        'c_w_in': (('c_w_in',), lambda a: a[None]), 'c_w_out': (('c_w_out',), lambda a: a[None]), 'c_w_s': (('c_w_s',), lambda a: a[None]),
        'c_b_s': (('bsT',), lambda a: a[:, :SGU_GROUPS].T[None]),
        'ffn_w_gate': (('Wg',), jnp.stack), 'ffn_w_up': (('Wu',), jnp.stack), 'ffn_w_down': (('Wd',), jnp.stack),
        'ffn_norm': (('ffn_norm',), lambda a: jnp.concatenate(a, axis=0)), 'ffn_conv_w': (('ffn_conv_w',), jnp.stack),
        'ffn_conv_b': (('ffn_conv_b',), lambda a: jnp.concatenate(a, axis=0)),
        'ab_conv_w': (('ab_conv_w',), lambda a: a[None]), 'final_norm': (('final_norm',), lambda a: a[0]),
    }
    for name in ('ab_norm', 'ab_q_norm', 'ab_kv_norm', 'ab_conv_b', 'ab_b_rg_a', 'ab_b_rg_x', 'ab_lambda', 'c_norm', 'c_ln_g', 'c_ln_b'):
        rules[name] = ((name,), lambda a: a)
    return {name: fn(*[g[k] for k in keys]) for name, (keys, fn) in rules.items() if all(k in g for k in keys)}


SLAB_ROWS = 16


def _round_up(n, m):
    return -(-n // m) * m


def _to_chunks(full, axis):
    s = full.shape
    return jnp.moveaxis(full.reshape(s[:axis] + (N_DEV, s[axis] // N_DEV) + s[axis + 1:]), axis, 0)


def _from_chunks(chunks, axis):
    local = chunks.shape[1:]
    return jnp.moveaxis(chunks, 0, axis).reshape(local[:axis] + (N_DEV * local[axis],) + local[axis + 1:])


def _merge_chunks(me, own, landed, axis, name):
    _, r, n = own.shape
    if axis == 1:
        def body(me_ref, own_ref, l_ref, o_ref):
            o_ref[...] = jnp.where(me_ref[0] == pl.program_id(0), own_ref[...], l_ref[0])

        grid, out_shape = (N_DEV,), (1, N_DEV * r, n)
        specs = [pl.BlockSpec((1, r, n), lambda dev, me_ref: (0, 0, 0)), pl.BlockSpec((1, 1, r, n), lambda dev, me_ref: (dev, 0, 0, 0))]
        out_spec = pl.BlockSpec((1, r, n), lambda dev, me_ref: (0, dev, 0))
    else:
        tr = r // 4

        def body(me_ref, own_ref, l_ref, o_ref):
            o_ref[0] = jnp.concatenate([jnp.where(me_ref[0] == dev, own_ref[0], l_ref[dev, 0]) for dev in range(N_DEV)], axis=1)

        grid, out_shape = (r // tr,), (1, r, N_DEV * n)
        specs = [pl.BlockSpec((1, tr, n), lambda i, me_ref: (0, i, 0)), pl.BlockSpec((N_DEV, 1, tr, n), lambda i, me_ref: (0, 0, i, 0))]
        out_spec = pl.BlockSpec((1, tr, N_DEV * n), lambda i, me_ref: (0, i, 0))
    return pl.pallas_call(
        body, name="merge_" + name,
        grid_spec=pltpu.PrefetchScalarGridSpec(num_scalar_prefetch=1, grid=grid, in_specs=specs, out_specs=out_spec),
        out_shape=jax.ShapeDtypeStruct(out_shape, own.dtype), compiler_params=_params())(me, own, landed)


def _split_chunks(whole, axis, name):
    _, rows, cols = whole.shape
    if axis == 1:
        r = rows // N_DEV

        def body(x_ref, o_ref):
            o_ref[0] = _bf(x_ref[...])

        grid, out_shape = (N_DEV,), (N_DEV, 1, r, cols)
        spec, out_spec = pl.BlockSpec((1, r, cols), lambda dev: (0, dev, 0)), pl.BlockSpec((1, 1, r, cols), lambda dev: (dev, 0, 0, 0))
    else:
        n, tr = cols // N_DEV, rows // 4

        def body(x_ref, o_ref):
            x = x_ref[0]
            for dev in range(N_DEV):
                o_ref[dev, 0] = _bf(x[:, dev * n:(dev + 1) * n])

        grid, out_shape = (rows // tr,), (N_DEV, 1, rows, n)
        spec, out_spec = pl.BlockSpec((1, tr, cols), lambda i: (0, i, 0)), pl.BlockSpec((N_DEV, 1, tr, n), lambda i: (0, 0, i, 0))
    return pl.pallas_call(body, name="split_" + name, grid=grid, in_specs=[spec], out_specs=out_spec,
                          out_shape=jax.ShapeDtypeStruct(out_shape, BF16), compiler_params=_params())(whole)


def _slab_rows(n):
    return _round_up(-(-n // LANES), SLAB_ROWS)


def _to_slab(a, lead):
    a = a.reshape(lead + (-1,))
    rows = _slab_rows(a.shape[-1])
    a = jnp.pad(a, [(0, 0)] * len(lead) + [(0, rows * LANES - a.shape[-1])])
    return a.reshape(lead + (rows, LANES))


def _pack_slabs(parts, lead):
    return jnp.concatenate([_to_slab(p, lead) for p in parts], axis=len(lead))


def _unpack_slabs(packed, shapes):
    lead = packed.shape[:-2]
    out, row = [], 0
    for shape in shapes:
        size = math.prod(shape)
        rows = _slab_rows(size)
        piece = lax.slice_in_dim(packed, row, row + rows, axis=len(lead))
        out.append(piece.reshape(lead + (rows * LANES,))[..., :size].reshape(lead + tuple(shape)))
        row += rows
    return out


HBM = pl.BlockSpec(memory_space=pl.ANY)


def _other_chips(x, y):
    return [(1 - x, y), (x, 1 - y), (1 - x, 1 - y)]


def _all_gather(blocks):
    n = len(blocks)

    def body(*refs):
        x_refs, out_refs, token = refs[:n], refs[n:2 * n], refs[2 * n]
        send_sems, recv_sems, local_sems = refs[2 * n + 1:]
        token[...] = jnp.zeros_like(token)
        x, y, c = lax.axis_index("x"), lax.axis_index("y"), lax.axis_index("c")
        me, sibling = (x, y, c), (x, y, 1 - c)
        chips = _other_chips(x, y)

        def slab(a, px, py, pc):
            return out_refs[a].at[4 * px + 2 * py + pc]

        def copy(a, k, blk, to, src=None):
            return pltpu.make_async_remote_copy(src_ref=slab(a, *blk) if src is None else src, dst_ref=slab(a, *blk),
                                                send_sem=send_sems.at[7 * a + k], recv_sem=recv_sems.at[7 * a + k],
                                                device_id=to, device_id_type=MESH)

        mine = [pltpu.make_async_copy(x_refs[a], slab(a, *me), local_sems.at[a]) for a in range(n)]
        started = []
        for a in range(n):
            mine[a].start()
            started.append(copy(a, 0, me, sibling, src=x_refs[a]))
            started += [copy(a, 1 + j, me, (*chip, c), src=x_refs[a]) for j, chip in enumerate(chips)]
        for cp in started:
            cp.start()
        for j, chip in enumerate(chips):
            for a in range(n):
                copy(a, 1 + j, (*chip, c), me).wait_recv()
                passed = copy(a, 4 + j, (*chip, c), sibling)
                passed.start()
                started.append(passed)
        for a in range(n):
            copy(a, 0, sibling, me).wait_recv()
        for j, chip in enumerate(chips):
            for a in range(n):
                copy(a, 4 + j, (*chip, 1 - c), me).wait_recv()
        for cp in started:
            cp.wait_send()
        for a in range(n):
            mine[a].wait()

    out = pl.pallas_call(
        body, name="all_gather_weights",
        out_shape=[jax.ShapeDtypeStruct((N_DEV,) + b.shape, b.dtype) for b in blocks] + [jax.ShapeDtypeStruct((8, LANES), F32)],
        in_specs=[HBM] * n, out_specs=[HBM] * n + [pl.BlockSpec(memory_space=pltpu.VMEM)],
        scratch_shapes=[pltpu.SemaphoreType.DMA((7 * n,)), pltpu.SemaphoreType.DMA((7 * n,)), pltpu.SemaphoreType.DMA((n,))],
    )(*blocks)
    return list(out[:n]), out[n][0, 0]


FLIPS = [(0, 0, 1), (1, 0, 0), (1, 0, 1), (0, 1, 0), (0, 1, 1), (1, 1, 0), (1, 1, 1)]


def _peers(x, y, c):
    flip = lambda v, f: 1 - v if f else v
    return [(flip(x, fx), flip(y, fy), flip(c, fc)) for fx, fy, fc in FLIPS]


def _direct_copies(src_refs, land_refs, send_sems, recv_sems, scatter):
    x, y, c = lax.axis_index("x"), lax.axis_index("y"), lax.axis_index("c")
    me = 4 * x + 2 * y + c
    starts, waits = [], []
    for a in range(len(src_refs)):
        for k, (px, py, pc) in enumerate(_peers(x, y, c)):
            peer = 4 * px + 2 * py + pc
            sems = dict(send_sem=send_sems.at[7 * a + k], recv_sem=recv_sems.at[7 * a + k], device_id=(px, py, pc), device_id_type=MESH)
            src = src_refs[a].at[peer] if scatter else src_refs[a]
            starts.append(pltpu.make_async_remote_copy(src_ref=src, dst_ref=land_refs[a].at[me], **sems))
            waits.append(pltpu.make_async_remote_copy(src_ref=src, dst_ref=land_refs[a].at[peer], **sems))
    n = len(src_refs)
    keeps = [] if scatter else [pltpu.make_async_copy(src_refs[a], land_refs[a].at[me], send_sems.at[7 * n + a]) for a in range(n)]
    return starts, waits, keeps


def _landing(src, scatter):
    block = src.shape[1:] if scatter else src.shape
    return jax.ShapeDtypeStruct((N_DEV,) + block, src.dtype)


HBM_SPACE = pl.BlockSpec(memory_space=pltpu.HBM)
SEMAPHORES = pl.BlockSpec(memory_space=pltpu.SEMAPHORE)
SPLIT_EFFECT = pltpu.SideEffectType.DATAFLOW_SIDE_EFFECTING


def _start_exchange(name, srcs, scatter):
    n = len(srcs)
    lands = [lax.empty(s.shape, s.dtype) for s in (_landing(s, scatter) for s in srcs)]

    def body(*refs):
        starts, _, keeps = _direct_copies(refs[:n], refs[n:2 * n], refs[2 * n], refs[2 * n + 1], scatter)
        for cp in starts + keeps:
            cp.start()
        refs[-1][...] = jnp.zeros_like(refs[-1])

    held = [pltpu.with_memory_space_constraint(a, pltpu.HBM) for a in list(srcs) + lands]
    out = pl.pallas_call(
        body, name=name + "_start",
        out_shape=(pltpu.SemaphoreType.DMA(((7 if scatter else 8) * n,)), pltpu.SemaphoreType.DMA((7 * n,)),
                   *[pltpu.HBM(a.shape, a.dtype) for a in held],
                   jax.ShapeDtypeStruct((8, LANES), F32)),
        in_specs=[HBM_SPACE] * (2 * n), out_specs=(SEMAPHORES, SEMAPHORES, *[HBM_SPACE] * (2 * n), pl.BlockSpec(memory_space=pltpu.VMEM)),
        input_output_aliases={i: 2 + i for i in range(2 * n)},
        compiler_params=pltpu.CompilerParams(has_side_effects=SPLIT_EFFECT),
    )(*held)
    return out[0], out[1], list(out[2:2 + n]), list(out[2 + n:2 + 2 * n]), out[-1][0, 0], out[-1]


def _wait_exchange(name, started, after, scatter):
    send_sems, recv_sems, srcs, lands = started[:4]
    n = len(srcs)

    def body(*refs):
        _, waits, keeps = _direct_copies(refs[:n], refs[n:2 * n], refs[2 * n], refs[2 * n + 1], scatter)
        for cp in waits:
            cp.wait_send()
        for cp in waits:
            cp.wait_recv()
        for cp in keeps:
            cp.wait()

    out = pl.pallas_call(
        body, name=name + "_wait", out_shape=tuple(pltpu.HBM(a.shape, a.dtype) for a in srcs + lands),
        in_specs=[HBM_SPACE] * (2 * n) + [SEMAPHORES, SEMAPHORES, HBM], out_specs=tuple([HBM_SPACE] * (2 * n)),
        input_output_aliases={i: i for i in range(2 * n)},
        compiler_params=pltpu.CompilerParams(has_side_effects=SPLIT_EFFECT),
    )(*srcs, *lands, send_sems, recv_sems, after)
    return list(out[:n]), list(out[n:])


def _row_tile(rows):
    return rows // 2 if (rows // 2) % SLAB_ROWS == 0 else rows


def _sum_and_adamw(me, landed, own, wts, m, v, name, layer=None, into=None):
    layers, r, n = wts.shape
    first = 0 if layer is None else layer
    count = layers if layer is None else 1
    tr = _row_tile(r)
    blk = pl.BlockSpec((1, tr, n), lambda li, ri, me_ref: (first + li, ri, 0))
    c1 = 1.0 / (1.0 - ADAM_B1 ** ADAM_STEP)
    c2 = 1.0 / (1.0 - ADAM_B2 ** ADAM_STEP)
    held = [] if into is None else list(into)

    def body(me_ref, l_ref, own_ref, w_ref, m_ref, v_ref, *rest):
        g_out, d_out, m_out, v_out = rest[len(held):]
        mine = own_ref[0].astype(F32)
        g = jnp.where(me_ref[0] == 0, mine, l_ref[0].astype(F32))
        for dev in range(1, N_DEV):
            g = g + jnp.where(me_ref[0] == dev, mine, l_ref[dev].astype(F32))
        m_new = ADAM_B1 * m_ref[...] + (1.0 - ADAM_B1) * g
        v_new = ADAM_B2 * v_ref[...] + (1.0 - ADAM_B2) * (g * g)
        g_out[...] = g
        m_out[...] = m_new
        v_out[...] = v_new
        d_out[...] = -ADAM_LR * ((m_new * c1) / (jnp.sqrt(v_new * c2) + ADAM_EPS) + ADAM_WD * w_ref[...])

    return pl.pallas_call(
        body, name="adamw_" + name,
        grid_spec=pltpu.PrefetchScalarGridSpec(
            num_scalar_prefetch=1, grid=(count, r // tr),
            in_specs=[pl.BlockSpec((N_DEV, 1, tr, n), lambda li, ri, me_ref: (0, li, ri, 0)),
                      pl.BlockSpec((1, 1, tr, n), lambda li, ri, me_ref: (me_ref[0], li, ri, 0)), blk, blk, blk] + [HBM] * len(held),
            out_specs=[blk] * 4),
        out_shape=[_sds((layers, r, n))] * 4, input_output_aliases={6 + i: i for i in range(len(held))},
        compiler_params=_params(2))(me, landed, own, wts, m, v, *held)


EARLY = ['ab_w_in']
LATE_STAGES = {
    'out0': [('ab_w_out', None, 'ab_w_out')],
    'ffn0': [('ffn_w_gate', 0, 'Wg'), ('ffn_w_up', 0, 'Wu'), ('ffn_w_down', 0, 'Wd')],
    'mix1': [('c_w_in', None, 'c_w_in'), ('c_w_out', None, 'c_w_out')],
    'ffn1': [('ffn_w_gate', 1, 'Wg'), ('ffn_w_up', 1, 'Wu'), ('ffn_w_down', 1, 'Wd')],
}
TRANSPOSED = ('ffn_w_gate', 'ffn_w_up')


def _stored(name, a):
    return jnp.swapaxes(a, 1, 2) if name in TRANSPOSED else a


def _stored_axis(name):
    return 1 if name in TRANSPOSED else SHARD_AXIS[name]


GRAD_STAGES = {
    'late1': ([('c_w_in', None), ('c_w_out', None), ('ffn_w_gate', 1), ('ffn_w_up', 1), ('ffn_w_down', 1)],
              ['c_norm', 'c_ln_g', 'c_ln_b', 'c_w_s', 'c_b_s', 'final_norm']),
    'late0': ([('ffn_w_gate', 0), ('ffn_w_up', 0), ('ffn_w_down', 0)], ['ffn_norm', 'ffn_conv_w', 'ffn_conv_b']),
    'mid': ([('ab_w_out', None)], ['ab_conv_w', 'ab_conv_b', 'ab_w_rg_a', 'ab_b_rg_a', 'ab_w_rg_x', 'ab_b_rg_x', 'ab_lambda']),
    'last': ([('ab_w_in', None)], ['ab_norm', 'ab_q_norm', 'ab_w_q_b', 'ab_kv_norm', 'ab_w_kv_b']),
}


def _gather_early(local):
    small = [_bf(local[n]) if n in MATRICES else lax.bitcast_convert_type(local[n], BF16) for n in SMALL_SHARDED]
    gathered, zero = _all_gather([_bf(local[n]) for n in EARLY] + [_pack_slabs(small, ())])
    full = {n: local[n] for n in REPLICATED}
    for n, g in zip(EARLY, gathered):
        full[n] = _from_chunks(g, SHARD_AXIS[n])
    for n, p in zip(SMALL_SHARDED, _unpack_slabs(gathered[-1], [s.shape for s in small])):
        full[n] = _from_chunks(p if n in MATRICES else lax.bitcast_convert_type(p, F32), SHARD_AXIS[n])
    return full, zero


def kernel(x, positions, ab_norm, ab_w_in, ab_q_norm, ab_w_q_b, ab_kv_norm, ab_w_kv_b, ab_conv_w, ab_conv_b, ab_w_rg_a, ab_b_rg_a, ab_w_rg_x, ab_b_rg_x, ab_lambda, ab_w_out, c_norm, c_w_in, c_ln_g, c_ln_b, c_w_s, c_b_s, c_w_out, ffn_norm, ffn_w_gate, ffn_w_up, ffn_conv_w, ffn_conv_b, ffn_w_down, final_norm, loss_target, m_ab_norm, m_ab_w_in, m_ab_q_norm, m_ab_w_q_b, m_ab_kv_norm, m_ab_w_kv_b, m_ab_conv_w, m_ab_conv_b, m_ab_w_rg_a, m_ab_b_rg_a, m_ab_w_rg_x, m_ab_b_rg_x, m_ab_lambda, m_ab_w_out, m_c_norm, m_c_w_in, m_c_ln_g, m_c_ln_b, m_c_w_s, m_c_b_s, m_c_w_out, m_ffn_norm, m_ffn_w_gate, m_ffn_w_up, m_ffn_conv_w, m_ffn_conv_b, m_ffn_w_down, m_final_norm, v_ab_norm, v_ab_w_in, v_ab_q_norm, v_ab_w_q_b, v_ab_kv_norm, v_ab_w_kv_b, v_ab_conv_w, v_ab_conv_b, v_ab_w_rg_a, v_ab_b_rg_a, v_ab_w_rg_x, v_ab_b_rg_x, v_ab_lambda, v_ab_w_out, v_c_norm, v_c_w_in, v_c_ln_g, v_c_ln_b, v_c_w_s, v_c_b_s, v_c_w_out, v_ffn_norm, v_ffn_w_gate, v_ffn_w_up, v_ffn_conv_w, v_ffn_conv_b, v_ffn_w_down, v_final_norm):
    given = dict(locals())
    local = {n: given[n] for n in WEIGHTS}
    b, seq, d = x.shape
    t = b * seq

    me = (4 * lax.axis_index("x") + 2 * lax.axis_index("y") + lax.axis_index("c")).astype(jnp.int32)
    me1 = me.reshape(1)

    full, zero = _gather_early(local)
    gathers = {}
    for stage, members in LATE_STAGES.items():
        srcs = [_bf(_stored(n, local[n] if layer is None else local[n][layer:layer + 1]) + zero) for n, layer, _ in members]
        gathers[stage] = _start_exchange('gather_' + stage, srcs, scatter=False)
        zero = gathers[stage][4]
    w = _prepare(full)
    w['ab_norm'] = w['ab_norm'] + zero

    def late_weights(stage, after):
        srcs, lands = _wait_exchange('gather_' + stage, gathers[stage], after, scatter=False)
        whole = [l.reshape(1, -1, l.shape[-1]) if _stored_axis(n) == 1 else
                 _merge_chunks(me1, s, l, _stored_axis(n), n + ('' if layer is None else str(layer)))
                 for (n, layer, _), s, l in zip(LATE_STAGES[stage], srcs, lands)]
        if stage == 'out0':
            return _prepare_out(whole[0])
        return {key: a[0] for (_, _, key), a in zip(LATE_STAGES[stage], whole)}

    scatters = {}

    def start_scatter(stage, g):
        whole = _unprepare(g)
        big, small = GRAD_STAGES[stage]
        slab = [_to_chunks(whole[n], SHARD_AXIS[n]) if n in SHARD_AXIS else jnp.broadcast_to(whole[n][None], (N_DEV,) + whole[n].shape)
                for n in small]
        own = [whole[n].reshape(N_DEV, 1, whole[n].shape[1] // N_DEV, whole[n].shape[2])
               if whole[n].dtype == BF16 and _stored_axis(n) == 1 else
               _split_chunks(whole[n], _stored_axis(n), n + ('' if layer is None else str(layer))) for n, layer in big]
        own.append(_bf(_pack_slabs(slab, (N_DEV,)))[:, None])
        scatters[stage] = _start_exchange('scatter_' + stage, own, scatter=True)
        return scatters[stage][4]

    posb = jnp.broadcast_to(positions.astype(F32).reshape(t, 1), (t, LANES))
    loss, dx, grads = _local_step(x.reshape(t, d), posb, loss_target.reshape(t, d), w, seq, late_weights, start_scatter)
    start_scatter('last', grads)
    after = scatters['last'][5]

    me1 = me.reshape(1)
    updated = {}
    for stage, (big, small) in GRAD_STAGES.items():
        owns, landed = _wait_exchange('scatter_' + stage, scatters[stage], after, scatter=True)
        for (n, layer), own, land in zip(big, owns, landed):
            updated[n] = _sum_and_adamw(me1, land, own, _stored(n, given[n]), _stored(n, given['m_' + n]), _stored(n, given['v_' + n]),
                                        n + ('' if layer is None else str(layer)), layer, updated.get(n))
        pack_small = lambda prefix: _pack_slabs([given[prefix + n] for n in small], ())[None]
        packed = _sum_and_adamw(me1, landed[-1], owns[-1], pack_small(''), pack_small('m_'), pack_small('v_'), 'small_' + stage)
        unpacked = [_unpack_slabs(p[0], [local[n].shape for n in small]) for p in packed]
        for i, n in enumerate(small):
            updated[n] = [u[i] for u in unpacked]
        after = sum([updated[n][1][:1, :1, :1] for n, _ in big], packed[1][:1, :1, :1])
    total = lax.psum(loss[0, 0], ("x", "y", "c"))
    return (total, dx.reshape(b, seq, d), *[_stored(n, updated[n][kind]) for kind in range(4) for n in WEIGHTS])
```

```python
import math

import jax
import jax.numpy as jnp
from jax import lax
from jax.experimental import pallas as pl
from jax.experimental.pallas import tpu as pltpu

F32 = jnp.float32
BF16 = jnp.bfloat16
MESH = pl.DeviceIdType.MESH

N_DEV = 8
LANES = 128
HALO = 8
VMEM_LIMIT = 56 << 20

NORM_EPS = 1e-6
HEADS = 8
HEAD_PAD = 128
QK_NOPE = 64
QK_ROPE = 32
ROPE_HALF = 16
ROPE_BASE = 10000.0
ATTN_SCALE = (QK_NOPE + QK_ROPE) ** -0.5
LRU_C = 8.0
LRU_W = 512
CHUNK = 128
SGU_GROUPS = 8
D_FF = 2816
FF_BLOCKS = 2

ADAM_LR, ADAM_B1, ADAM_B2, ADAM_EPS, ADAM_WD, ADAM_STEP = 0.001, 0.9, 0.999, 1e-08, 0.01, 10

WEIGHTS = ['ab_norm', 'ab_w_in', 'ab_q_norm', 'ab_w_q_b', 'ab_kv_norm', 'ab_w_kv_b', 'ab_conv_w', 'ab_conv_b',
           'ab_w_rg_a', 'ab_b_rg_a', 'ab_w_rg_x', 'ab_b_rg_x', 'ab_lambda', 'ab_w_out', 'c_norm', 'c_w_in', 'c_ln_g',
           'c_ln_b', 'c_w_s', 'c_b_s', 'c_w_out', 'ffn_norm', 'ffn_w_gate', 'ffn_w_up', 'ffn_conv_w', 'ffn_conv_b',
           'ffn_w_down', 'final_norm']
SHARD_AXIS = {'ab_w_in': 2, 'ab_w_q_b': 2, 'ab_w_kv_b': 2, 'ab_conv_w': 2, 'ab_w_out': 1, 'c_norm': 1, 'c_w_in': 2,
              'c_ln_g': 1, 'c_ln_b': 1, 'c_w_out': 1, 'ffn_w_gate': 2, 'ffn_w_up': 2, 'ffn_conv_w': 2, 'ffn_w_down': 1}
MATRICES = ['ab_w_in', 'ab_w_q_b', 'ab_w_kv_b', 'ab_w_out', 'c_w_in', 'c_w_out', 'ffn_w_gate', 'ffn_w_up', 'ffn_w_down']
BIG = ['ab_w_in', 'c_w_in', 'ffn_w_gate', 'ffn_w_up', 'ab_w_out', 'c_w_out', 'ffn_w_down']
REPLICATED = [n for n in WEIGHTS if n not in SHARD_AXIS]
SMALL_SHARDED = [n for n in WEIGHTS if n in SHARD_AXIS and n not in BIG]


def _bf(x):
    return x.astype(BF16)


def _nn(a, b):
    return lax.dot_general(_bf(a), _bf(b), (((1,), (0,)), ((), ())), preferred_element_type=F32)


def _nt(a, b):
    return lax.dot_general(_bf(a), _bf(b), (((1,), (1,)), ((), ())), preferred_element_type=F32)


def _tn(a, b):
    return lax.dot_general(_bf(a), _bf(b), (((0,), (0,)), ((), ())), preferred_element_type=F32)


def _rms(x, g):
    return x * lax.rsqrt(jnp.mean(x * x, axis=-1, keepdims=True) + NORM_EPS) * g


def _layer_norm(x, g, b):
    xc = x - jnp.mean(x, axis=-1, keepdims=True)
    return xc * lax.rsqrt(jnp.mean(xc * xc, axis=-1, keepdims=True) + NORM_EPS) * g + b


def _gelu(x):
    return jax.nn.gelu(x)


STRIP = 16
STRIP_LANES = 384
GELU_C = math.sqrt(2.0 / math.pi)
GELU_A = 0.044715


def _gelu_and_grad(x):
    x2 = x * x
    t = jnp.tanh(x * (GELU_C + (GELU_C * GELU_A) * x2))
    half_x = 0.5 * x
    one_plus_t = 1.0 + t
    return half_x * one_plus_t, 0.5 * one_plus_t + half_x * (1.0 - t * t) * (GELU_C + (3.0 * GELU_C * GELU_A) * x2)


def _colsum(x):
    return jnp.sum(x, axis=0, keepdims=True)


def _softplus(x):
    return jnp.maximum(x, 0.0) + jnp.log1p(jnp.exp(-jnp.abs(x)))


@jax.custom_vjp
def _decay(x):
    a = jnp.exp(x)
    y = 2.0 * x
    series = -y * (1.0 + y * (1 / 2 + y * (1 / 6 + y * (1 / 24 + y * (1 / 120 + y * (1 / 720))))))
    return a, jnp.where(y < -0.3, 1.0 - a * a, series)


def _decay_fwd(x):
    a, gap = _decay(x)
    return (a, gap), a


def _decay_bwd(a, cts):
    return (a * (cts[0] - 2.0 * a * cts[1]),)


_decay.defvjp(_decay_fwd, _decay_bwd)


def _accumulate(ref, val, first):
    @pl.when(first)
    def _():
        ref[...] = val

    @pl.when(jnp.logical_not(first))
    def _():
        ref[...] += val


def _params(n_axes=1):
    return pltpu.CompilerParams(dimension_semantics=("arbitrary",) * n_axes, vmem_limit_bytes=VMEM_LIMIT)


def _row(tm, n):
    return pl.BlockSpec((tm, n), lambda i: (i, 0))


def _const(shape):
    nd = len(shape)
    return pl.BlockSpec(shape, lambda i: (0,) * nd, pipeline_mode=pl.Buffered(1))


def _prev_halo(tm, n):
    return pl.BlockSpec((HALO, n), lambda i: (jnp.maximum(i * (tm // HALO) - 1, 0), 0))


def _next_halo(tm, n, n_tiles):
    last = n_tiles * (tm // HALO) - 1
    return pl.BlockSpec((HALO, n), lambda i: (jnp.minimum((i + 1) * (tm // HALO), last), 0))


def _sds(shape, dtype=F32):
    return jax.ShapeDtypeStruct(shape, dtype)


def _rope_tables(posb):
    lane = lax.broadcasted_iota(jnp.int32, posb.shape, 1)
    in_rope = jnp.logical_and(lane >= QK_NOPE, lane < QK_NOPE + QK_ROPE)
    j = (lane & (ROPE_HALF - 1)).astype(F32)
    inv_freq = jnp.exp((-math.log(ROPE_BASE)) * j / ROPE_HALF)
    ang = posb * inv_freq
    return jnp.where(in_rope, jnp.cos(ang), 1.0), jnp.where(in_rope, jnp.sin(ang), 0.0)


def _rot(q):
    n = q.shape[1]
    lane = lax.broadcasted_iota(jnp.int32, q.shape, 1) & (HEAD_PAD - 1)
    first_half = jnp.where(lane >= QK_NOPE, -pltpu.roll(q, n - ROPE_HALF, 1), 0.0)
    second_half = jnp.where(lane < QK_NOPE + QK_ROPE, pltpu.roll(q, ROPE_HALF, 1), 0.0)
    return jnp.where(lane < QK_NOPE + ROPE_HALF, first_half, second_half)


def _rope(q, cos_t, sin_t):
    return q * cos_t + _rot(q) * sin_t


def _rope_transpose(dq, cos_t, sin_t):
    return dq * cos_t - _rot(dq * sin_t)


def _tile_heads(t):
    return jnp.concatenate([t] * HEADS, axis=1)


Q_LORA, KV_LORA = 256, 128
Z_KPE = Q_LORA + KV_LORA
Z_LRU = Z_KPE + HEAD_PAD
Z_GATE = Z_LRU + LRU_W
Z_WIDTH = Z_GATE + LRU_W


def _ab_in_fwd(x, posb, w, tm):
    t, d = x.shape

    def body(x_ref, pos_ref, gn_ref, win_ref, qn_ref, wq_ref, kvn_ref, wk_ref, wv_ref, q_out, k_out, v_out, xl_out, gate_out):
        hn = _rms(x_ref[...], gn_ref[...])
        z = _nn(hn, win_ref[...])
        cqn = _rms(z[:, :Q_LORA], qn_ref[...])
        kvn = _rms(z[:, Q_LORA:Z_KPE], kvn_ref[...])
        cos_t, sin_t = _rope_tables(pos_ref[...])
        q_out[...] = _rope(_nn(cqn, wq_ref[...]), _tile_heads(cos_t), _tile_heads(sin_t))
        kpe = _rope(z[:, Z_KPE:Z_LRU], cos_t, sin_t)
        k_out[...] = _nn(kvn, wk_ref[...]) + _tile_heads(kpe)
        v_out[...] = _nn(kvn, wv_ref[...])
        xl_out[...] = z[:, Z_LRU:Z_GATE]
        gate_out[...] = z[:, Z_GATE:]

    hp = HEADS * HEAD_PAD
    return pl.pallas_call(
        body, name="ab_in_fwd", grid=(t // tm,),
        in_specs=[_row(tm, d), _row(tm, LANES), _const((1, d)), _const((d, Z_WIDTH)), _const((1, Q_LORA)), _const((Q_LORA, hp)),
                  _const((1, KV_LORA)), _const((KV_LORA, hp)), _const((KV_LORA, hp))],
        out_specs=[_row(tm, hp), _row(tm, hp), _row(tm, hp), _row(tm, LRU_W), _row(tm, LRU_W)],
        out_shape=[_sds((t, hp)), _sds((t, hp)), _sds((t, hp)), _sds((t, LRU_W)), _sds((t, LRU_W))],
        compiler_params=_params(),
    )(x, posb, w['ab_norm'], w['W_in'], w['ab_q_norm'], w['Wq'], w['ab_kv_norm'], w['Wk'], w['Wv'])


def _ab_in_bwd(x, posb, w, dq, dk, dv, dxl, dgate, dres, tm):
    t, d = x.shape
    hp = HEADS * HEAD_PAD

    def body(x_ref, pos_ref, gn_ref, win_ref, qn_ref, wq_ref, kvn_ref, wk_ref, wv_ref, dq_ref, dk_ref, dv_ref, dxl_ref, dgate_ref,
             dres_ref, dx_out, dgn_out, dwin_out, dqn_out, dwq_out, dkvn_out, dwk_out, dwv_out):
        first = pl.program_id(0) == 0
        hn, vjp_in = jax.vjp(_rms, x_ref[...], gn_ref[...])
        z = _nn(hn, win_ref[...])
        cqn, vjp_q = jax.vjp(_rms, z[:, :Q_LORA], qn_ref[...])
        kvn, vjp_kv = jax.vjp(_rms, z[:, Q_LORA:Z_KPE], kvn_ref[...])
        cos_t, sin_t = _rope_tables(pos_ref[...])
        dq0 = _rope_transpose(dq_ref[...], _tile_heads(cos_t), _tile_heads(sin_t))
        dk0 = dk_ref[...]
        dv0 = dv_ref[...]
        dkpe = dk0[:, :HEAD_PAD]
        for h in range(1, HEADS):
            dkpe = dkpe + dk0[:, h * HEAD_PAD:(h + 1) * HEAD_PAD]
        dkpe = _rope_transpose(dkpe, cos_t, sin_t)
        _accumulate(dwq_out, _tn(cqn, dq0), first)
        _accumulate(dwk_out, _tn(kvn, dk0), first)
        _accumulate(dwv_out, _tn(kvn, dv0), first)
        dcq, dqn = vjp_q(_nt(dq0, wq_ref[...]))
        dckv, dkvn = vjp_kv(_nt(dk0, wk_ref[...]) + _nt(dv0, wv_ref[...]))
        _accumulate(dqn_out, dqn, first)
        _accumulate(dkvn_out, dkvn, first)
        dz = jnp.concatenate([dcq, dckv, dkpe, dxl_ref[...], dgate_ref[...]], axis=1)
        _accumulate(dwin_out, _tn(hn, dz), first)
        dx, dgn = vjp_in(_nt(dz, win_ref[...]))
        _accumulate(dgn_out, dgn, first)
        dx_out[...] = dx + dres_ref[...]

    return pl.pallas_call(
        body, name="ab_in_bwd", grid=(t // tm,),
        in_specs=[_row(tm, d), _row(tm, LANES), _const((1, d)), _const((d, Z_WIDTH)), _const((1, Q_LORA)), _const((Q_LORA, hp)),
                  _const((1, KV_LORA)), _const((KV_LORA, hp)), _const((KV_LORA, hp)),
                  _row(tm, hp), _row(tm, hp), _row(tm, hp), _row(tm, LRU_W), _row(tm, LRU_W), _row(tm, d)],
        out_specs=[_row(tm, d), _const((1, d)), _const((d, Z_WIDTH)), _const((1, Q_LORA)), _const((Q_LORA, hp)),
                   _const((1, KV_LORA)), _const((KV_LORA, hp)), _const((KV_LORA, hp))],
        out_shape=[_sds((t, d)), _sds((1, d)), _sds((d, Z_WIDTH)), _sds((1, Q_LORA)), _sds((Q_LORA, hp)),
                   _sds((1, KV_LORA)), _sds((KV_LORA, hp)), _sds((KV_LORA, hp))],
        compiler_params=_params(),
    )(x, posb, w['ab_norm'], w['W_in'], w['ab_q_norm'], w['Wq'], w['ab_kv_norm'], w['Wk'], w['Wv'], dq, dk, dv, dxl, dgate, dres)


def _attn_probs(q_blk, k_ext, tq):
    ext = k_ext.shape[0]
    s = lax.dot_general(q_blk, k_ext, (((1,), (1,)), ((), ())), preferred_element_type=F32) * ATTN_SCALE
    causal = lax.broadcasted_iota(jnp.int32, (tq, tq), 1) <= lax.broadcasted_iota(jnp.int32, (tq, tq), 0)
    diag = jnp.where(causal, s[:, ext - tq:], -1e30)
    s = diag if ext == tq else jnp.concatenate([s[:, :ext - tq], diag], axis=1)
    p = jnp.exp(s - jnp.max(s, axis=1, keepdims=True))
    return p / jnp.sum(p, axis=1, keepdims=True)


def _attn_fwd(q, k, v, tq):
    b, s, hp = q.shape
    blk = pl.BlockSpec((1, s, HEAD_PAD), lambda bi, h: (bi, 0, h))

    def body(q_ref, k_ref, v_ref, o_ref):
        kb = _bf(k_ref[0])
        vb = _bf(v_ref[0])
        for i in range(s // tq):
            ext = (i + 1) * tq
            p = _attn_probs(_bf(q_ref[0, i * tq:ext, :]), kb[:ext], tq)
            o_ref[0, i * tq:ext, :] = lax.dot_general(_bf(p), vb[:ext], (((1,), (0,)), ((), ())), preferred_element_type=F32)

    return pl.pallas_call(body, name="attn_fwd", grid=(b, HEADS), in_specs=[blk, blk, blk], out_specs=blk,
                          out_shape=_sds((b, s, hp)), compiler_params=_params(2))(q, k, v)


def _attn_bwd(q, k, v, do, tq):
    b, s, hp = q.shape
    blk = pl.BlockSpec((1, s, HEAD_PAD), lambda bi, h: (bi, 0, h))

    def body(q_ref, k_ref, v_ref, do_ref, dq_ref, dk_ref, dv_ref):
        kb = _bf(k_ref[0])
        vb = _bf(v_ref[0])
        dk_ref[...] = jnp.zeros_like(dk_ref)
        dv_ref[...] = jnp.zeros_like(dv_ref)
        for i in range(s // tq):
            ext = (i + 1) * tq
            qb = _bf(q_ref[0, i * tq:ext, :])
            dob = _bf(do_ref[0, i * tq:ext, :])
            p = _attn_probs(qb, kb[:ext], tq)
            dv_ref[0, :ext, :] += lax.dot_general(_bf(p), dob, (((0,), (0,)), ((), ())), preferred_element_type=F32)
            dp = lax.dot_general(dob, vb[:ext], (((1,), (1,)), ((), ())), preferred_element_type=F32)
            ds = _bf(p * (dp - jnp.sum(p * dp, axis=1, keepdims=True)) * ATTN_SCALE)
            dq_ref[0, i * tq:ext, :] = lax.dot_general(ds, kb[:ext], (((1,), (0,)), ((), ())), preferred_element_type=F32)
            dk_ref[0, :ext, :] += lax.dot_general(ds, qb, (((0,), (0,)), ((), ())), preferred_element_type=F32)

    return pl.pallas_call(body, name="attn_bwd", grid=(b, HEADS), in_specs=[blk, blk, blk, blk], out_specs=[blk, blk, blk],
                          out_shape=[_sds((b, s, hp))] * 3, compiler_params=_params(2))(q, k, v, do)


LRU_CONV = 4


def _lru_point(pre_a, pre_x, xc, lam):
    r = jax.nn.sigmoid(pre_a)
    i = jax.nn.sigmoid(pre_x)
    a, gap = _decay(-LRU_C * r * _softplus(-lam))
    return a, jnp.sqrt(gap) * (i * xc)


def _causal_conv(pad_ref, x, halo, first_in_seq, w, taps):
    tm = x.shape[0]
    pad_ref[:HALO, :] = jnp.where(first_in_seq, 0.0, halo)
    pad_ref[HALO:, :] = x
    y = w[taps - 1:taps, :] * x
    for k in range(taps - 1):
        off = HALO - (taps - 1) + k
        y = y + w[k:k + 1, :] * pad_ref[off:off + tm, :]
    return y


def _conv_taps(pad_ref, r, cols, taps):
    blocks = [pad_ref[r + j * HALO:r + (j + 1) * HALO, cols] for j in range(1 + STRIP // HALO)]
    sub = lax.broadcasted_iota(jnp.int32, blocks[0].shape, 0)
    out = []
    for k in range(taps - 1):
        s = taps - 1 - k
        rolled = [pltpu.roll(b, s, 0) for b in blocks]
        out.append(jnp.concatenate([jnp.where(sub < s, rolled[j], rolled[j + 1]) for j in range(STRIP // HALO)], axis=0))
    out.append(jnp.concatenate(blocks[1:], axis=0))
    return out


def _causal_conv_wgrad(pad_ref, dy, taps):
    tm = dy.shape[0]
    return jnp.concatenate([_colsum(dy * pad_ref[HALO - (taps - 1) + k:HALO - (taps - 1) + k + tm, :]) for k in range(taps)], axis=0)


def _causal_conv_transpose(pad_ref, dy, halo_next, last_in_seq, w, taps):
    tm = dy.shape[0]
    pad_ref[:tm, :] = dy
    pad_ref[tm:, :] = jnp.where(last_in_seq, 0.0, halo_next)
    dx = w[taps - 1:taps, :] * dy
    for k in range(taps - 1):
        off = (taps - 1) - k
        dx = dx + w[k:k + 1, :] * pad_ref[off:off + tm, :]
    return dx


def _lru_fwd(xl, gate, w, ts, seq):
    t, n = xl.shape
    tiles_per_seq = seq // ts

    def body(xl_ref, halo_ref, gate_ref, cw_ref, cb_ref, wa_ref, ba_ref, wx_ref, bx_ref, lam_ref, y_out, h_out, pad_ref, a_ref, b_ref, carry_ref):
        first_in_seq = pl.program_id(0) % tiles_per_seq == 0
        xc = _causal_conv(pad_ref, xl_ref[...], halo_ref[...], first_in_seq, cw_ref[...], LRU_CONV) + cb_ref[...]
        a, bx = _lru_point(_nn(xc, wa_ref[...]) + ba_ref[...], _nn(xc, wx_ref[...]) + bx_ref[...], xc, lam_ref[...])
        a_ref[...] = a
        b_ref[...] = bx

        @pl.when(first_in_seq)
        def _():
            carry_ref[...] = jnp.zeros_like(carry_ref)

        def step(r, h):
            h = a_ref[pl.ds(r, 1), :] * h + b_ref[pl.ds(r, 1), :]
            h_out[pl.ds(r, 1), :] = h
            return h

        carry_ref[...] = lax.fori_loop(0, ts, step, carry_ref[...], unroll=8)
        y_out[...] = h_out[...] * _gelu(gate_ref[...])

    return pl.pallas_call(
        body, name="lru_fwd", grid=(t // ts,),
        in_specs=[_row(ts, n), _prev_halo(ts, n), _row(ts, n), _const((LRU_CONV, n)), _const((1, n)), _const((n, n)), _const((1, n)),
                  _const((n, n)), _const((1, n)), _const((1, n))],
        out_specs=[_row(ts, n), _row(ts, n)], out_shape=[_sds((t, n)), _sds((t, n))],
        scratch_shapes=[pltpu.VMEM((HALO + ts, n), F32), pltpu.VMEM((ts, n), F32), pltpu.VMEM((ts, n), F32), pltpu.VMEM((1, n), F32)],
        compiler_params=_params(),
    )(xl, xl, gate, w['ab_conv_w'], w['ab_conv_b'], w['Wa'], w['ab_b_rg_a'], w['Wx'], w['ab_b_rg_x'], w['ab_lambda'])


def _lru_bwd(xl, gate, hs, dy, w, ts, seq):
    t, n = xl.shape
    tiles_per_seq = seq // ts
    n_tiles = t // ts

    def rev(i):
        return n_tiles - 1 - i

    row = pl.BlockSpec((ts, n), lambda i: (rev(i), 0))
    prev = pl.BlockSpec((HALO, n), lambda i: (jnp.maximum(rev(i) * (ts // HALO) - 1, 0), 0))
    acc = lambda shape: pl.BlockSpec(shape, lambda i: (0,) * len(shape))

    def body(xl_ref, xhalo_ref, gate_ref, h_ref, hhalo_ref, dy_ref, cw_ref, cb_ref, wa_ref, ba_ref, wx_ref, bx_ref, lam_ref,
             dxl_out, dgate_out, dcw_out, dcb_out, dwa_out, dba_out, dwx_out, dbx_out, dlam_out,
             pad_ref, padh_ref, padd_ref, a_ref, g_ref, carry_ref, dhalo_ref):
        step_id = pl.program_id(0)
        first = step_id == 0
        tile = rev(step_id)
        first_in_seq = tile % tiles_per_seq == 0
        last_in_seq = tile % tiles_per_seq == tiles_per_seq - 1
        cw = cw_ref[...]
        xc = _causal_conv(pad_ref, xl_ref[...], xhalo_ref[...], first_in_seq, cw, LRU_CONV) + cb_ref[...]
        pre_a = _nn(xc, wa_ref[...]) + ba_ref[...]
        pre_x = _nn(xc, wx_ref[...]) + bx_ref[...]
        (a, _), vjp_point = jax.vjp(_lru_point, pre_a, pre_x, xc, lam_ref[...])
        h = h_ref[...]
        _, vjp_out = jax.vjp(lambda h_, g_: h_ * _gelu(g_), h, gate_ref[...])
        dh, dgate = vjp_out(dy_ref[...])
        dgate_out[...] = dgate
        a_ref[...] = a
        g_ref[...] = dh

        @pl.when(last_in_seq)
        def _():
            carry_ref[...] = jnp.zeros_like(carry_ref)

        def step(j, c):
            r = ts - 1 - j
            g = g_ref[pl.ds(r, 1), :] + c
            g_ref[pl.ds(r, 1), :] = g
            return a_ref[pl.ds(r, 1), :] * g

        carry_ref[...] = lax.fori_loop(0, ts, step, carry_ref[...], unroll=8)
        g = g_ref[...]
        padh_ref[:HALO, :] = jnp.where(first_in_seq, 0.0, hhalo_ref[...])
        padh_ref[HALO:, :] = h
        dpre_a, dpre_x, dxc, dlam = vjp_point((g * padh_ref[HALO - 1:HALO - 1 + ts, :], g))
        dxc = dxc + _nt(dpre_a, wa_ref[...]) + _nt(dpre_x, wx_ref[...])
        _accumulate(dwa_out, _tn(xc, dpre_a), first)
        _accumulate(dwx_out, _tn(xc, dpre_x), first)
        _accumulate(dba_out, _colsum(dpre_a), first)
        _accumulate(dbx_out, _colsum(dpre_x), first)
        _accumulate(dlam_out, dlam, first)
        _accumulate(dcb_out, _colsum(dxc), first)
        _accumulate(dcw_out, _causal_conv_wgrad(pad_ref, dxc, LRU_CONV), first)
        dxl_out[...] = _causal_conv_transpose(padd_ref, dxc, dhalo_ref[...], last_in_seq, cw, LRU_CONV)
        dhalo_ref[...] = dxc[:HALO, :]

    return pl.pallas_call(
        body, name="lru_bwd", grid=(n_tiles,),
        in_specs=[row, prev, row, row, prev, row, _const((LRU_CONV, n)), _const((1, n)), _const((n, n)), _const((1, n)),
                  _const((n, n)), _const((1, n)), _const((1, n))],
        out_specs=[row, row, acc((LRU_CONV, n)), acc((1, n)), acc((n, n)), acc((1, n)), acc((n, n)), acc((1, n)), acc((1, n))],
        out_shape=[_sds((t, n)), _sds((t, n)), _sds((LRU_CONV, n)), _sds((1, n)), _sds((n, n)), _sds((1, n)), _sds((n, n)),
                   _sds((1, n)), _sds((1, n))],
        scratch_shapes=[pltpu.VMEM((HALO + ts, n), F32), pltpu.VMEM((HALO + ts, n), F32), pltpu.VMEM((ts + HALO, n), F32),
                        pltpu.VMEM((ts, n), F32), pltpu.VMEM((ts, n), F32), pltpu.VMEM((1, n), F32), pltpu.VMEM((HALO, n), F32)],
        compiler_params=_params(),
    )(xl, xl, gate, hs, hs, dy, w['ab_conv_w'], w['ab_conv_b'], w['Wa'], w['ab_b_rg_a'], w['Wx'], w['ab_b_rg_x'], w['ab_lambda'])


def _ab_out_fwd(x, o, y, w, tm):
    t, d = x.shape
    hp = o.shape[1]

    def body(x_ref, o_ref, y_ref, wa_ref, wb_ref, h_out):
        h_out[...] = x_ref[...] + _nn(o_ref[...], wa_ref[...]) + _nn(y_ref[...], wb_ref[...])

    return pl.pallas_call(body, name="ab_out_fwd", grid=(t // tm,),
                          in_specs=[_row(tm, d), _row(tm, hp), _row(tm, LRU_W), _const((hp, d)), _const((LRU_W, d))],
                          out_specs=_row(tm, d), out_shape=_sds((t, d)), compiler_params=_params())(x, o, y, w['Wo_a'], w['Wo_b'])


def _ab_out_bwd(o, y, dh, w, tm):
    t, d = dh.shape
    hp = o.shape[1]

    def body(o_ref, y_ref, dh_ref, wa_ref, wb_ref, do_out, dy_out, dwa_out, dwb_out):
        first = pl.program_id(0) == 0
        dh_t = dh_ref[...]
        do_out[...] = _nt(dh_t, wa_ref[...])
        dy_out[...] = _nt(dh_t, wb_ref[...])
        _accumulate(dwa_out, _tn(o_ref[...], dh_t), first)
        _accumulate(dwb_out, _tn(y_ref[...], dh_t), first)

    return pl.pallas_call(body, name="ab_out_bwd", grid=(t // tm,),
                          in_specs=[_row(tm, hp), _row(tm, LRU_W), _row(tm, d), _const((hp, d)), _const((LRU_W, d))],
                          out_specs=[_row(tm, hp), _row(tm, LRU_W), _const((hp, d)), _const((LRU_W, d))],
                          out_shape=[_sds((t, hp)), _sds((t, LRU_W)), _sds((hp, d)), _sds((LRU_W, d))],
                          compiler_params=_params())(o, y, dh, w['Wo_a'], w['Wo_b'])


FFN_CONV = 3


def _ffn_a_fwd(h, norm, wg, wu, tm):
    t, d = h.shape
    fb = D_FF // FF_BLOCKS

    def body(h_ref, gn_ref, wg_ref, wu_ref, g_out, u_out, hn_out):
        hn = _bf(_rms(h_ref[...], gn_ref[...]))
        hn_out[0] = hn
        g_out[...] = _nt(hn, wg_ref[...])
        u_out[...] = _nt(hn, wu_ref[...])

    wspec = pl.BlockSpec((fb, d), lambda f, i: (f, 0))
    ospec = pl.BlockSpec((tm, fb), lambda f, i: (i, f))
    return pl.pallas_call(
        body, name="ffn_a_fwd", grid=(FF_BLOCKS, t // tm),
        in_specs=[pl.BlockSpec((tm, d), lambda f, i: (i, 0)), pl.BlockSpec((1, d), lambda f, i: (0, 0)), wspec, wspec],
        out_specs=[ospec, ospec, pl.BlockSpec((1, tm, d), lambda f, i: (f, i, 0))],
        out_shape=[_sds((t, D_FF)), _sds((t, D_FF)), _sds((FF_BLOCKS, t, d), BF16)], compiler_params=_params(2))(h, norm, wg, wu)


def _ffn_b_fwd(g, u, h, cw, cb, wd, tm, seq):
    t, d = h.shape
    tiles_per_seq = seq // tm

    def body(g_ref, halo_ref, u_ref, h_ref, cw_ref, cb_ref, wd_ref, h_out, pad_ref, act_ref):
        pad_ref[:HALO, :] = jnp.where(pl.program_id(0) % tiles_per_seq == 0, 0.0, halo_ref[...])
        pad_ref[HALO:, :] = g_ref[...]
        cw = cw_ref[...]
        cb = cb_ref[...]
        for c0 in range(0, D_FF, STRIP_LANES):
            cols = slice(c0, min(c0 + STRIP_LANES, D_FF))
            for r in range(0, tm, STRIP):
                taps = _conv_taps(pad_ref, r, cols, FFN_CONV)
                gc = cb[:, cols] + cw[0:1, cols] * taps[0] + cw[1:2, cols] * taps[1] + cw[2:3, cols] * taps[2]
                act_ref[r:r + STRIP, cols] = _bf(_gelu(gc) * u_ref[r:r + STRIP, cols])
        h_out[...] = h_ref[...] + _nn(act_ref[...], wd_ref[...])

    return pl.pallas_call(body, name="ffn_b_fwd", grid=(t // tm,),
                          in_specs=[_row(tm, D_FF), _prev_halo(tm, D_FF), _row(tm, D_FF), _row(tm, d), _const((FFN_CONV, D_FF)),
                                    _const((1, D_FF)), _const((D_FF, d))],
                          out_specs=_row(tm, d), out_shape=_sds((t, d)),
                          scratch_shapes=[pltpu.VMEM((HALO + tm, D_FF), F32), pltpu.VMEM((tm, D_FF), BF16)],
                          compiler_params=_params())(g, g, u, h, cw, cb, wd)


def _ffn_b_bwd(g, u, dout, cw, cb, wd, tm, seq):
    t, d = dout.shape
    fb = D_FF // FF_BLOCKS
    tiles_per_seq = seq // tm

    def body(g_ref, halo_ref, u_ref, dout_ref, cw_ref, cb_ref, wd_ref, dgc_out, du_out, dwd_out, dcw_out, dcb_out,
             pad_ref, dact_ref, act_ref, acc_ref, dwd_acc):
        i = pl.program_id(1)
        first = i == 0
        pad_ref[:HALO, :] = jnp.where(i % tiles_per_seq == 0, 0.0, halo_ref[...])
        pad_ref[HALO:, :] = g_ref[...]
        dout_b = _bf(dout_ref[...])
        dact_ref[...] = _nt(dout_b, wd_ref[...])
        cw = cw_ref[...]
        cb = cb_ref[...]
        fold = lambda a: a[:HALO] + a[HALO:]
        for c0 in range(0, fb, STRIP_LANES):
            cols = slice(c0, min(c0 + STRIP_LANES, fb))
            sums = [jnp.zeros((HALO, cols.stop - c0), F32) for _ in range(1 + FFN_CONV)]
            for r in range(0, tm, STRIP):
                rows = slice(r, r + STRIP)
                taps = _conv_taps(pad_ref, r, cols, FFN_CONV)
                gelu, dgelu = _gelu_and_grad(cb[:, cols] + cw[0:1, cols] * taps[0] + cw[1:2, cols] * taps[1] + cw[2:3, cols] * taps[2])
                u = u_ref[rows, cols]
                dact = dact_ref[rows, cols]
                act_ref[rows, cols] = _bf(gelu * u)
                du_out[rows, cols] = _bf(dact * gelu)
                dgc = dact * u * dgelu
                dgc_out[rows, cols] = dgc
                sums = [sums[0] + fold(dgc)] + [sums[1 + k] + fold(dgc * taps[k]) for k in range(FFN_CONV)]
            for k in range(1 + FFN_CONV):
                acc_ref[k, :, cols] = sums[k]
        _accumulate(dwd_acc, _tn(act_ref[...], dout_b), first)

        @pl.when(i == t // tm - 1)
        def _():
            dwd_out[...] = _bf(dwd_acc[...])

        _accumulate(dcb_out, _colsum(acc_ref[0]), first)
        _accumulate(dcw_out, jnp.concatenate([_colsum(acc_ref[1 + k]) for k in range(FFN_CONV)], axis=0), first)

    blk = pl.BlockSpec((tm, fb), lambda f, i: (i, f))
    halo = pl.BlockSpec((HALO, fb), lambda f, i: (jnp.maximum(i * (tm // HALO) - 1, 0), f))
    wd_blk = pl.BlockSpec((fb, d), lambda f, i: (f, 0), pipeline_mode=pl.Buffered(1))
    return pl.pallas_call(
        body, name="ffn_b_bwd", grid=(FF_BLOCKS, t // tm),
        in_specs=[blk, halo, blk, pl.BlockSpec((tm, d), lambda f, i: (i, 0)), pl.BlockSpec((FFN_CONV, fb), lambda f, i: (0, f)),
                  pl.BlockSpec((1, fb), lambda f, i: (0, f)), wd_blk],
        out_specs=[blk, blk, wd_blk, pl.BlockSpec((FFN_CONV, fb), lambda f, i: (0, f)),
                   pl.BlockSpec((1, fb), lambda f, i: (0, f))],
        out_shape=[_sds((t, D_FF)), _sds((t, D_FF), BF16), _sds((D_FF, d), BF16), _sds((FFN_CONV, D_FF)), _sds((1, D_FF))],
        scratch_shapes=[pltpu.VMEM((HALO + tm, fb), F32), pltpu.VMEM((tm, fb), F32), pltpu.VMEM((tm, fb), BF16),
                        pltpu.VMEM((1 + FFN_CONV, HALO, fb), F32), pltpu.VMEM((fb, d), F32)],
        compiler_params=_params(2))(g, g, u, dout, cw, cb, wd)


def _ffn_a_dgrad(h, norm, dgc, du, dres, cw, wg, wu, tm, seq):
    t, d = h.shape
    tiles_per_seq = seq // tm
    n_tiles = t // tm

    def body(h_ref, gn_ref, dgc_ref, halo_ref, du_ref, dres_ref, cw_ref, wg_ref, wu_ref, dh_out, dg_out, dgn_out, pad_ref):
        i = pl.program_id(0)
        last_in_seq = i % tiles_per_seq == tiles_per_seq - 1
        dg = _bf(_causal_conv_transpose(pad_ref, dgc_ref[...], halo_ref[...], last_in_seq, cw_ref[...], FFN_CONV))
        dg_out[...] = dg
        _, vjp_norm = jax.vjp(_rms, h_ref[...], gn_ref[...])
        dh, dgn = vjp_norm(_nn(dg, wg_ref[...]) + _nn(du_ref[...], wu_ref[...]))
        dh_out[...] = dh + dres_ref[...]
        _accumulate(dgn_out, dgn, i == 0)

    return pl.pallas_call(
        body, name="ffn_a_dgrad", grid=(n_tiles,),
        in_specs=[_row(tm, d), _const((1, d)), _row(tm, D_FF), _next_halo(tm, D_FF, n_tiles), _row(tm, D_FF), _row(tm, d),
                  _const((FFN_CONV, D_FF)), _const((D_FF, d)), _const((D_FF, d))],
        out_specs=[_row(tm, d), _row(tm, D_FF), _const((1, d))], out_shape=[_sds((t, d)), _sds((t, D_FF), BF16), _sds((1, d))],
        scratch_shapes=[pltpu.VMEM((tm + HALO, D_FF), F32)], compiler_params=_params())(h, norm, dgc, dgc, du, dres, cw, wg, wu)


def _ffn_a_wgrad(hn, dg, du, tm):
    _, t, d = hn.shape
    fb = D_FF // FF_BLOCKS

    n_tiles = t // tm

    def body(hn_ref, dg_ref, du_ref, dwg_out, dwu_out, acc_g, acc_u):
        i = pl.program_id(1)
        hn_t = hn_ref[0]
        _accumulate(acc_g, _tn(dg_ref[...], hn_t), i == 0)
        _accumulate(acc_u, _tn(du_ref[...], hn_t), i == 0)

        @pl.when(i == n_tiles - 1)
        def _():
            dwg_out[...] = _bf(acc_g[...])
            dwu_out[...] = _bf(acc_u[...])

    blk = pl.BlockSpec((tm, fb), lambda f, i: (i, f))
    wspec = pl.BlockSpec((fb, d), lambda f, i: (f, 0), pipeline_mode=pl.Buffered(1))
    return pl.pallas_call(body, name="ffn_a_wgrad", grid=(FF_BLOCKS, n_tiles),
                          in_specs=[pl.BlockSpec((1, tm, d), lambda f, i: (0, i, 0)), blk, blk],
                          out_specs=[wspec, wspec], out_shape=[_sds((D_FF, d), BF16), _sds((D_FF, d), BF16)],
                          scratch_shapes=[pltpu.VMEM((fb, d), F32), pltpu.VMEM((fb, d), F32)],
                          compiler_params=_params(2))(hn, dg, du)


def _sgu_mix(vn, ws_ref, bst):
    tril = lax.broadcasted_iota(jnp.int32, (CHUNK, CHUNK), 0) >= lax.broadcasted_iota(jnp.int32, (CHUNK, CHUNK), 1)
    wms = [jnp.where(tril, ws_ref[g], 0.0) for g in range(SGU_GROUPS)]
    chunks = []
    for n in range(vn.shape[0] // CHUNK):
        vc = vn[n * CHUNK:(n + 1) * CHUNK, :]
        chunks.append(jnp.concatenate(
            [_nn(wms[g], vc[:, g * CHUNK:(g + 1) * CHUNK]) + bst[:, g:g + 1] for g in range(SGU_GROUPS)], axis=1))
    return jnp.concatenate(chunks, axis=0)


def _sgu_fwd(h, w, tm):
    t, d = h.shape

    def body(h_ref, cn_ref, win_ref, lg_ref, lb_ref, ws_ref, bst_ref, wout_ref, h_out):
        h_t = h_ref[...]
        z = _gelu(_nn(_rms(h_t, cn_ref[...]), win_ref[...]))
        vn = _layer_norm(z[:, d:], lg_ref[...], lb_ref[...])
        s = _sgu_mix(vn, ws_ref, bst_ref[...])
        h_out[...] = h_t + _nn(z[:, :d] * s, wout_ref[...])

    return pl.pallas_call(
        body, name="sgu_fwd", grid=(t // tm,),
        in_specs=[_row(tm, d), _const((1, d)), _const((d, 2 * d)), _const((1, d)), _const((1, d)), _const((SGU_GROUPS, CHUNK, CHUNK)),
                  _const((CHUNK, LANES)), _const((d, d))],
        out_specs=_row(tm, d), out_shape=_sds((t, d)), compiler_params=_params(),
    )(h, w['c_norm'], w['c_w_in'], w['c_ln_g'], w['c_ln_b'], w['c_w_s'], w['bsT'], w['c_w_out'])


def _sgu_bwd(h, dout, w, tm):
    t, d = h.shape

    def body(h_ref, dout_ref, cn_ref, win_ref, lg_ref, lb_ref, ws_ref, bst_ref, wout_ref,
             dh_out, dcn_out, dwin_out, dlg_out, dlb_out, dws_out, dbst_out, dwout_out):
        first = pl.program_id(0) == 0
        hn, vjp_norm = jax.vjp(_rms, h_ref[...], cn_ref[...])
        zpre = _nn(hn, win_ref[...])
        u, vjp_u = jax.vjp(_gelu, zpre[:, :d])
        vn, vjp_v = jax.vjp(lambda zp, lg, lb: _layer_norm(_gelu(zp), lg, lb), zpre[:, d:], lg_ref[...], lb_ref[...])
        s = _sgu_mix(vn, ws_ref, bst_ref[...])
        dout_t = dout_ref[...]
        dus = _nt(dout_t, wout_ref[...])
        _accumulate(dwout_out, _tn(u * s, dout_t), first)
        ds = dus * u
        tril = lax.broadcasted_iota(jnp.int32, (CHUNK, CHUNK), 0) >= lax.broadcasted_iota(jnp.int32, (CHUNK, CHUNK), 1)
        lane = lax.broadcasted_iota(jnp.int32, (CHUNK, LANES), 1)
        dws = [jnp.zeros((CHUNK, CHUNK), F32) for _ in range(SGU_GROUPS)]
        dbst = jnp.zeros((CHUNK, LANES), F32)
        dvn_chunks = []
        for n in range(tm // CHUNK):
            cols = []
            for g in range(SGU_GROUPS):
                ds_ng = ds[n * CHUNK:(n + 1) * CHUNK, g * CHUNK:(g + 1) * CHUNK]
                vc_ng = vn[n * CHUNK:(n + 1) * CHUNK, g * CHUNK:(g + 1) * CHUNK]
                cols.append(_tn(jnp.where(tril, ws_ref[g], 0.0), ds_ng))
                dws[g] = dws[g] + _nt(ds_ng, vc_ng)
                dbst = dbst + jnp.where(lane == g, jnp.sum(ds_ng, axis=1, keepdims=True), 0.0)
            dvn_chunks.append(jnp.concatenate(cols, axis=1))
        dvn = jnp.concatenate(dvn_chunks, axis=0)
        for g in range(SGU_GROUPS):
            val = jnp.where(tril, dws[g], 0.0)

            @pl.when(first)
            def _():
                dws_out[g] = val

            @pl.when(jnp.logical_not(first))
            def _():
                dws_out[g] += val
        _accumulate(dbst_out, dbst, first)
        (dzu,) = vjp_u(dus * s)
        dzv, dlg, dlb = vjp_v(dvn)
        _accumulate(dlg_out, dlg, first)
        _accumulate(dlb_out, dlb, first)
        dzpre = jnp.concatenate([dzu, dzv], axis=1)
        _accumulate(dwin_out, _tn(hn, dzpre), first)
        dh, dcn = vjp_norm(_nt(dzpre, win_ref[...]))
        _accumulate(dcn_out, dcn, first)
        dh_out[...] = dh + dout_t

    return pl.pallas_call(
        body, name="sgu_bwd", grid=(t // tm,),
        in_specs=[_row(tm, d), _row(tm, d), _const((1, d)), _const((d, 2 * d)), _const((1, d)), _const((1, d)),
                  _const((SGU_GROUPS, CHUNK, CHUNK)), _const((CHUNK, LANES)), _const((d, d))],
        out_specs=[_row(tm, d), _const((1, d)), _const((d, 2 * d)), _const((1, d)), _const((1, d)), _const((SGU_GROUPS, CHUNK, CHUNK)),
                   _const((CHUNK, LANES)), _const((d, d))],
        out_shape=[_sds((t, d)), _sds((1, d)), _sds((d, 2 * d)), _sds((1, d)), _sds((1, d)), _sds((SGU_GROUPS, CHUNK, CHUNK)),
                   _sds((CHUNK, LANES)), _sds((d, d))],
        compiler_params=_params(),
    )(h, dout, w['c_norm'], w['c_w_in'], w['c_ln_g'], w['c_ln_b'], w['c_w_s'], w['bsT'], w['c_w_out'])


def _final_loss(h, target, norm, tm):
    t, d = h.shape

    def body(h_ref, tgt_ref, gn_ref, dh_out, loss_out, dgn_out):
        first = pl.program_id(0) == 0
        tgt = tgt_ref[...]

        def loss_fn(h_, g_):
            err = _rms(h_, g_) - tgt
            return 0.5 * jnp.sum(jnp.mean(err * err, axis=-1, keepdims=True), axis=0, keepdims=True)

        loss, vjp_loss = jax.vjp(loss_fn, h_ref[...], gn_ref[...])
        dh, dgn = vjp_loss(jnp.ones((1, 1), F32))
        dh_out[...] = dh
        _accumulate(loss_out, loss, first)
        _accumulate(dgn_out, dgn, first)

    return pl.pallas_call(body, name="final_loss", grid=(t // tm,), in_specs=[_row(tm, d), _row(tm, d), _const((1, d))],
                          out_specs=[_row(tm, d), _const((1, 1)), _const((1, d))],
                          out_shape=[_sds((t, d)), _sds((1, 1)), _sds((1, d))], compiler_params=_params())(h, target, norm)


def _tile(t, seq, want):
    tm = min(want, seq)
    assert seq % tm == 0 and t % tm == 0 and tm % CHUNK == 0
    return tm


def _local_step(x, posb, target, w, seq, late_weights, on_grads):
    t, d = x.shape
    b = t // seq
    hp = HEADS * HEAD_PAD
    tm_big, tm_mid = _tile(t, seq, 512), _tile(t, seq, 256)
    tq = _tile(t, seq, 512)

    q, k, v, xl, gate = _ab_in_fwd(x, posb, w, tm_big)
    o = _attn_fwd(q.reshape(b, seq, hp), k.reshape(b, seq, hp), v.reshape(b, seq, hp), tq).reshape(t, hp)
    y, hs = _lru_fwd(xl, gate, w, tm_big, seq)
    w = {**w, **late_weights('out0', y)}
    h1 = _ab_out_fwd(x, o, y, w, tm_big)
    hcur = h1
    saved = []
    for l in range(2):
        if l == 1:
            w = {**w, **late_weights('mix1', hcur)}
            saved_h2 = hcur
            hcur = _sgu_fwd(hcur, w, tm_mid)
        wl = late_weights('ffn%d' % l, hcur)
        g, u, hn = _ffn_a_fwd(hcur, w['ffn_norm'][l], wl['Wg'], wl['Wu'], tm_big)
        hnext = _ffn_b_fwd(g, u, hcur, w['ffn_conv_w'][l], w['ffn_conv_b'][l], wl['Wd'], tm_mid, seq)
        saved.append((hcur, g, u, wl, hn))
        hcur = hnext
    dh, loss, d_final = _final_loss(hcur, target, w['final_norm'], tm_big)

    ffn = {}
    conv_b = list(w['ffn_conv_b'])
    for l in (1, 0):
        hin, g, u, wl, hn = saved[l]
        dgc, du, d_wd, d_cw, d_cb = _ffn_b_bwd(g, u, dh, w['ffn_conv_w'][l], conv_b[l], wl['Wd'], tm_big, seq)
        dh, dg, d_norm = _ffn_a_dgrad(hin, w['ffn_norm'][l], dgc, du, dh, w['ffn_conv_w'][l], wl['Wg'], wl['Wu'], tm_mid, seq)
        d_wg, d_wu = _ffn_a_wgrad(hn, dg, du, _tile(t, seq, 1024))
        ffn[l] = dict(ffn_norm=d_norm, ffn_conv_w=d_cw, ffn_conv_b=d_cb, Wg=d_wg, Wu=d_wu, Wd=d_wd)
        if l == 1:
            dh, d_cn, d_cwin, d_lg, d_lb, d_ws, d_bst, d_cwout = _sgu_bwd(saved_h2, dh, w, tm_mid)
            zero = on_grads('late1', dict(final_norm=d_final, c_norm=d_cn, c_ln_g=d_lg, c_ln_b=d_lb, c_w_s=d_ws, bsT=d_bst, c_w_in=d_cwin,
                                          c_w_out=d_cwout, Wg=[d_wg], Wu=[d_wu], Wd=[d_wd]))
            conv_b[0] = conv_b[0] + zero
    late0 = {name: [ffn[0][name], ffn[1][name]] for name in ('ffn_norm', 'ffn_conv_w', 'ffn_conv_b')}
    zero = on_grads('late0', dict(late0, Wg=[ffn[0]['Wg']], Wu=[ffn[0]['Wu']], Wd=[ffn[0]['Wd']]))
    w = {**w, 'Wo_b': w['Wo_b'] + zero.astype(w['Wo_b'].dtype)}
    do, dy, d_woa, d_wob = _ab_out_bwd(o, y, dh, w, tm_big)
    dxl, dgate, d_cw, d_cb, d_wa, d_ba, d_wx, d_bx, d_lam = _lru_bwd(xl, gate, hs, dy, w, tm_big, seq)
    zero = on_grads('mid', dict(Wo_a=d_woa, Wo_b=d_wob, ab_conv_w=d_cw, ab_conv_b=d_cb, Wa=d_wa, ab_b_rg_a=d_ba, Wx=d_wx,
                                ab_b_rg_x=d_bx, ab_lambda=d_lam))
    w = {**w, 'ab_norm': w['ab_norm'] + zero}
    dq, dk, dv = _attn_bwd(q.reshape(b, seq, hp), k.reshape(b, seq, hp), v.reshape(b, seq, hp), do.reshape(b, seq, hp), tq)
    dx, d_gn, d_win, d_qn, d_wq, d_kvn, d_wk, d_wv = _ab_in_bwd(
        x, posb, w, dq.reshape(t, hp), dk.reshape(t, hp), dv.reshape(t, hp), dxl, dgate, dh, tm_mid)
    return loss, dx, dict(ab_norm=d_gn, W_in=d_win, ab_q_norm=d_qn, Wq=d_wq, ab_kv_norm=d_kvn, Wk=d_wk, Wv=d_wv)


def _block_diag(wg):
    g, n, _ = wg.shape
    return jnp.einsum('gij,gh->gihj', wg, jnp.eye(g, dtype=wg.dtype)).reshape(g * n, g * n)


def _prepare_out(w_out):
    d = w_out.shape[2]
    mla = HEADS * QK_NOPE
    return {'Wo_a': jnp.pad(w_out[0, :mla].reshape(HEADS, QK_NOPE, d), ((0, 0), (0, HEAD_PAD - QK_NOPE), (0, 0))).reshape(HEADS * HEAD_PAD, d),
            'Wo_b': w_out[0, mla:]}


def _prepare(full):
    d = full['ab_w_in'].shape[1]
    w_in = full['ab_w_in'][0]
    zeros = lambda n: jnp.zeros((d, n), w_in.dtype)
    wq = full['ab_w_q_b'][0].reshape(Q_LORA, HEADS, QK_NOPE + QK_ROPE)
    wkv = full['ab_w_kv_b'][0].reshape(KV_LORA, HEADS, 2 * QK_NOPE)
    pad_head = lambda a: jnp.pad(a, ((0, 0), (0, 0), (0, HEAD_PAD - a.shape[2]))).reshape(a.shape[0], HEADS * HEAD_PAD)
    w = {
        'W_in': jnp.concatenate([w_in[:, :Z_KPE], zeros(QK_NOPE), w_in[:, Z_KPE:Z_KPE + QK_ROPE],
                                 zeros(HEAD_PAD - QK_NOPE - QK_ROPE), w_in[:, Z_KPE + QK_ROPE:]], axis=1),
        'Wq': pad_head(wq), 'Wk': pad_head(wkv[:, :, :QK_NOPE]), 'Wv': pad_head(wkv[:, :, QK_NOPE:]),
        'Wa': _bf(_block_diag(full['ab_w_rg_a'][0])), 'Wx': _bf(_block_diag(full['ab_w_rg_x'][0])),
        'c_w_s': full['c_w_s'][0],
        'bsT': jnp.pad(full['c_b_s'][0].T, ((0, 0), (0, LANES - SGU_GROUPS))),
        'ffn_norm': [full['ffn_norm'][l:l + 1] for l in range(2)], 'ffn_conv_w': [full['ffn_conv_w'][l] for l in range(2)],
        'ffn_conv_b': [full['ffn_conv_b'][l:l + 1] for l in range(2)],
        'ab_conv_w': full['ab_conv_w'][0], 'final_norm': full['final_norm'][None, :],
    }
    for name in ('ab_norm', 'ab_q_norm', 'ab_kv_norm', 'ab_conv_b', 'ab_b_rg_a', 'ab_b_rg_x', 'ab_lambda', 'c_norm', 'c_ln_g', 'c_ln_b'):
        w[name] = full[name]
    return w


def _unprepare(g):
    unpad_head = lambda a, n: a.reshape(a.shape[0], HEADS, HEAD_PAD)[:, :, :n]
    diag = lambda a: jnp.einsum('gigj->gij', a.reshape(HEADS, LRU_W // HEADS, HEADS, LRU_W // HEADS))
    rules = {
        'ab_w_in': (('W_in',), lambda a: jnp.concatenate([a[:, :Z_KPE], a[:, Z_KPE + QK_NOPE:Z_KPE + QK_NOPE + QK_ROPE], a[:, Z_LRU:]], axis=1)[None]),
        'ab_w_q_b': (('Wq',), lambda a: unpad_head(a, QK_NOPE + QK_ROPE).reshape(1, Q_LORA, -1)),
        'ab_w_kv_b': (('Wk', 'Wv'), lambda a, b: jnp.concatenate([unpad_head(a, QK_NOPE), unpad_head(b, QK_NOPE)], axis=2).reshape(1, KV_LORA, -1)),
        'ab_w_out': (('Wo_a', 'Wo_b'), lambda a, b: jnp.concatenate(
            [a.reshape(HEADS, HEAD_PAD, -1)[:, :QK_NOPE].reshape(HEADS * QK_NOPE, -1), b], axis=0)[None]),
        'ab_w_rg_a': (('Wa',), lambda a: diag(a)[None]), 'ab_w_rg_x': (('Wx',), lambda a: diag(a)[None]),
        'c_w_in': (('c_w_in',), lambda a: a[None]), 'c_w_out': (('c_w_out',), lambda a: a[None]), 'c_w_s': (('c_w_s',), lambda a: a[None]),
        'c_b_s': (('bsT',), lambda a: a[:, :SGU_GROUPS].T[None]),
        'ffn_w_gate': (('Wg',), jnp.stack), 'ffn_w_up': (('Wu',), jnp.stack), 'ffn_w_down': (('Wd',), jnp.stack),
        'ffn_norm': (('ffn_norm',), lambda a: jnp.concatenate(a, axis=0)), 'ffn_conv_w': (('ffn_conv_w',), jnp.stack),
        'ffn_conv_b': (('ffn_conv_b',), lambda a: jnp.concatenate(a, axis=0)),
        'ab_conv_w': (('ab_conv_w',), lambda a: a[None]), 'final_norm': (('final_norm',), lambda a: a[0]),
    }
    for name in ('ab_norm', 'ab_q_norm', 'ab_kv_norm', 'ab_conv_b', 'ab_b_rg_a', 'ab_b_rg_x', 'ab_lambda', 'c_norm', 'c_ln_g', 'c_ln_b'):
        rules[name] = ((name,), lambda a: a)
    return {name: fn(*[g[k] for k in keys]) for name, (keys, fn) in rules.items() if all(k in g for k in keys)}


SLAB_ROWS = 16


def _round_up(n, m):
    return -(-n // m) * m


def _to_chunks(full, axis):
    s = full.shape
    return jnp.moveaxis(full.reshape(s[:axis] + (N_DEV, s[axis] // N_DEV) + s[axis + 1:]), axis, 0)


def _from_chunks(chunks, axis):
    local = chunks.shape[1:]
    return jnp.moveaxis(chunks, 0, axis).reshape(local[:axis] + (N_DEV * local[axis],) + local[axis + 1:])


def _merge_columns(landed, name):
    _, _, r, n = landed.shape
    tr = r // 4

    def body(l_ref, o_ref):
        o_ref[0] = jnp.concatenate([l_ref[dev, 0] for dev in range(N_DEV)], axis=1)

    return pl.pallas_call(body, name="merge_" + name, grid=(r // tr,),
                          in_specs=[pl.BlockSpec((N_DEV, 1, tr, n), lambda i: (0, 0, i, 0))],
                          out_specs=pl.BlockSpec((1, tr, N_DEV * n), lambda i: (0, i, 0)),
                          out_shape=jax.ShapeDtypeStruct((1, r, N_DEV * n), landed.dtype), compiler_params=_params())(landed)


def _split_chunks(whole, axis, name):
    _, rows, cols = whole.shape
    if axis == 1:
        r = rows // N_DEV

        def body(x_ref, o_ref):
            o_ref[0] = _bf(x_ref[...])

        grid, out_shape = (N_DEV,), (N_DEV, 1, r, cols)
        spec, out_spec = pl.BlockSpec((1, r, cols), lambda dev: (0, dev, 0)), pl.BlockSpec((1, 1, r, cols), lambda dev: (dev, 0, 0, 0))
    else:
        n, tr = cols // N_DEV, rows // 4

        def body(x_ref, o_ref):
            x = x_ref[0]
            for dev in range(N_DEV):
                o_ref[dev, 0] = _bf(x[:, dev * n:(dev + 1) * n])

        grid, out_shape = (rows // tr,), (N_DEV, 1, rows, n)
        spec, out_spec = pl.BlockSpec((1, tr, cols), lambda i: (0, i, 0)), pl.BlockSpec((N_DEV, 1, tr, n), lambda i: (0, 0, i, 0))
    return pl.pallas_call(body, name="split_" + name, grid=grid, in_specs=[spec], out_specs=out_spec,
                          out_shape=jax.ShapeDtypeStruct(out_shape, BF16), compiler_params=_params())(whole)


def _slab_rows(n):
    return _round_up(-(-n // LANES), SLAB_ROWS)


def _to_slab(a, lead):
    a = a.reshape(lead + (-1,))
    rows = _slab_rows(a.shape[-1])
    a = jnp.pad(a, [(0, 0)] * len(lead) + [(0, rows * LANES - a.shape[-1])])
    return a.reshape(lead + (rows, LANES))


def _pack_slabs(parts, lead):
    return jnp.concatenate([_to_slab(p, lead) for p in parts], axis=len(lead))


def _unpack_slabs(packed, shapes):
    lead = packed.shape[:-2]
    out, row = [], 0
    for shape in shapes:
        size = math.prod(shape)
        rows = _slab_rows(size)
        piece = lax.slice_in_dim(packed, row, row + rows, axis=len(lead))
        out.append(piece.reshape(lead + (rows * LANES,))[..., :size].reshape(lead + tuple(shape)))
        row += rows
    return out


HBM = pl.BlockSpec(memory_space=pl.ANY)


def _other_chips(x, y):
    return [(1 - x, y), (x, 1 - y), (1 - x, 1 - y)]


def _all_gather(blocks):
    n = len(blocks)

    def body(*refs):
        x_refs, out_refs, token = refs[:n], refs[n:2 * n], refs[2 * n]
        send_sems, recv_sems, local_sems = refs[2 * n + 1:]
        token[...] = jnp.zeros_like(token)
        x, y, c = lax.axis_index("x"), lax.axis_index("y"), lax.axis_index("c")
        me, sibling = (x, y, c), (x, y, 1 - c)
        chips = _other_chips(x, y)

        def slab(a, px, py, pc):
            return out_refs[a].at[4 * px + 2 * py + pc]

        def copy(a, k, blk, to, src=None):
            return pltpu.make_async_remote_copy(src_ref=slab(a, *blk) if src is None else src, dst_ref=slab(a, *blk),
                                                send_sem=send_sems.at[7 * a + k], recv_sem=recv_sems.at[7 * a + k],
                                                device_id=to, device_id_type=MESH)

        mine = [pltpu.make_async_copy(x_refs[a], slab(a, *me), local_sems.at[a]) for a in range(n)]
        started = []
        for a in range(n):
            mine[a].start()
            started.append(copy(a, 0, me, sibling, src=x_refs[a]))
            started += [copy(a, 1 + j, me, (*chip, c), src=x_refs[a]) for j, chip in enumerate(chips)]
        for cp in started:
            cp.start()
        for j, chip in enumerate(chips):
            for a in range(n):
                copy(a, 1 + j, (*chip, c), me).wait_recv()
                passed = copy(a, 4 + j, (*chip, c), sibling)
                passed.start()
                started.append(passed)
        for a in range(n):
            copy(a, 0, sibling, me).wait_recv()
        for j, chip in enumerate(chips):
            for a in range(n):
                copy(a, 4 + j, (*chip, 1 - c), me).wait_recv()
        for cp in started:
            cp.wait_send()
        for a in range(n):
            mine[a].wait()

    out = pl.pallas_call(
        body, name="all_gather_weights",
        out_shape=[jax.ShapeDtypeStruct((N_DEV,) + b.shape, b.dtype) for b in blocks] + [jax.ShapeDtypeStruct((8, LANES), F32)],
        in_specs=[HBM] * n, out_specs=[HBM] * n + [pl.BlockSpec(memory_space=pltpu.VMEM)],
        scratch_shapes=[pltpu.SemaphoreType.DMA((7 * n,)), pltpu.SemaphoreType.DMA((7 * n,)), pltpu.SemaphoreType.DMA((n,))],
    )(*blocks)
    return list(out[:n]), out[n][0, 0]


FLIPS = [(0, 0, 1), (1, 0, 0), (1, 0, 1), (0, 1, 0), (0, 1, 1), (1, 1, 0), (1, 1, 1)]


def _peers(x, y, c):
    flip = lambda v, f: 1 - v if f else v
    return [(flip(x, fx), flip(y, fy), flip(c, fc)) for fx, fy, fc in FLIPS]


def _direct_copies(src_refs, land_refs, send_sems, recv_sems, scatter):
    x, y, c = lax.axis_index("x"), lax.axis_index("y"), lax.axis_index("c")
    me = 4 * x + 2 * y + c
    starts, waits = [], []
    for a in range(len(src_refs)):
        for k, (px, py, pc) in enumerate(_peers(x, y, c)):
            peer = 4 * px + 2 * py + pc
            sems = dict(send_sem=send_sems.at[7 * a + k], recv_sem=recv_sems.at[7 * a + k], device_id=(px, py, pc), device_id_type=MESH)
            src = src_refs[a].at[peer] if scatter else src_refs[a]
            starts.append(pltpu.make_async_remote_copy(src_ref=src, dst_ref=land_refs[a].at[me], **sems))
            waits.append(pltpu.make_async_remote_copy(src_ref=src, dst_ref=land_refs[a].at[peer], **sems))
    n = len(src_refs)
    keeps = [] if scatter else [pltpu.make_async_copy(src_refs[a], land_refs[a].at[me], send_sems.at[7 * n + a]) for a in range(n)]
    return starts, waits, keeps


def _landing(src, scatter):
    block = src.shape[1:] if scatter else src.shape
    return jax.ShapeDtypeStruct((N_DEV,) + block, src.dtype)


HBM_SPACE = pl.BlockSpec(memory_space=pltpu.HBM)
SEMAPHORES = pl.BlockSpec(memory_space=pltpu.SEMAPHORE)
SPLIT_EFFECT = pltpu.SideEffectType.DATAFLOW_SIDE_EFFECTING


def _start_exchange(name, srcs, scatter):
    n = len(srcs)
    lands = [lax.empty(s.shape, s.dtype) for s in (_landing(s, scatter) for s in srcs)]

    def body(*refs):
        starts, _, keeps = _direct_copies(refs[:n], refs[n:2 * n], refs[2 * n], refs[2 * n + 1], scatter)
        for cp in starts + keeps:
            cp.start()
        refs[-1][...] = jnp.zeros_like(refs[-1])

    held = [pltpu.with_memory_space_constraint(a, pltpu.HBM) for a in list(srcs) + lands]
    out = pl.pallas_call(
        body, name=name + "_start",
        out_shape=(pltpu.SemaphoreType.DMA(((7 if scatter else 8) * n,)), pltpu.SemaphoreType.DMA((7 * n,)),
                   *[pltpu.HBM(a.shape, a.dtype) for a in held],
                   jax.ShapeDtypeStruct((8, LANES), F32)),
        in_specs=[HBM_SPACE] * (2 * n), out_specs=(SEMAPHORES, SEMAPHORES, *[HBM_SPACE] * (2 * n), pl.BlockSpec(memory_space=pltpu.VMEM)),
        input_output_aliases={i: 2 + i for i in range(2 * n)},
        compiler_params=pltpu.CompilerParams(has_side_effects=SPLIT_EFFECT),
    )(*held)
    return out[0], out[1], list(out[2:2 + n]), list(out[2 + n:2 + 2 * n]), out[-1][0, 0], out[-1]


def _wait_exchange(name, started, after, scatter):
    send_sems, recv_sems, srcs, lands = started[:4]
    n = len(srcs)

    def body(*refs):
        _, waits, keeps = _direct_copies(refs[:n], refs[n:2 * n], refs[2 * n], refs[2 * n + 1], scatter)
        for cp in waits:
            cp.wait_send()
        for cp in waits:
            cp.wait_recv()
        for cp in keeps:
            cp.wait()

    out = pl.pallas_call(
        body, name=name + "_wait", out_shape=tuple(pltpu.HBM(a.shape, a.dtype) for a in srcs + lands),
        in_specs=[HBM_SPACE] * (2 * n) + [SEMAPHORES, SEMAPHORES, HBM], out_specs=tuple([HBM_SPACE] * (2 * n)),
        input_output_aliases={i: i for i in range(2 * n)},
        compiler_params=pltpu.CompilerParams(has_side_effects=SPLIT_EFFECT),
    )(*srcs, *lands, send_sems, recv_sems, after)
    return list(out[:n]), list(out[n:])


def _row_tile(rows):
    return rows // 2 if (rows // 2) % SLAB_ROWS == 0 else rows


def _sum_and_adamw(me, landed, own, wts, m, v, name, layer=None, into=None):
    layers, r, n = wts.shape
    first = 0 if layer is None else layer
    count = layers if layer is None else 1
    tr = _row_tile(r)
    blk = pl.BlockSpec((1, tr, n), lambda li, ri, me_ref: (first + li, ri, 0))
    c1 = 1.0 / (1.0 - ADAM_B1 ** ADAM_STEP)
    c2 = 1.0 / (1.0 - ADAM_B2 ** ADAM_STEP)
    held = [] if into is None else list(into)

    def body(me_ref, l_ref, own_ref, w_ref, m_ref, v_ref, *rest):
        g_out, d_out, m_out, v_out = rest[len(held):]
        mine = own_ref[0].astype(F32)
        g = jnp.where(me_ref[0] == 0, mine, l_ref[0].astype(F32))
        for dev in range(1, N_DEV):
            g = g + jnp.where(me_ref[0] == dev, mine, l_ref[dev].astype(F32))
        m_new = ADAM_B1 * m_ref[...] + (1.0 - ADAM_B1) * g
        v_new = ADAM_B2 * v_ref[...] + (1.0 - ADAM_B2) * (g * g)
        g_out[...] = g
        m_out[...] = m_new
        v_out[...] = v_new
        d_out[...] = -ADAM_LR * ((m_new * c1) / (jnp.sqrt(v_new * c2) + ADAM_EPS) + ADAM_WD * w_ref[...])

    return pl.pallas_call(
        body, name="adamw_" + name,
        grid_spec=pltpu.PrefetchScalarGridSpec(
            num_scalar_prefetch=1, grid=(count, r // tr),
            in_specs=[pl.BlockSpec((N_DEV, 1, tr, n), lambda li, ri, me_ref: (0, li, ri, 0)),
                      pl.BlockSpec((1, 1, tr, n), lambda li, ri, me_ref: (me_ref[0], li, ri, 0)), blk, blk, blk] + [HBM] * len(held),
            out_specs=[blk] * 4),
        out_shape=[_sds((layers, r, n))] * 4, input_output_aliases={6 + i: i for i in range(len(held))},
        compiler_params=_params(2))(me, landed, own, wts, m, v, *held)


EARLY = ['ab_w_in']
LATE_STAGES = {
    'out0': [('ab_w_out', None, 'ab_w_out')],
    'ffn0': [('ffn_w_gate', 0, 'Wg'), ('ffn_w_up', 0, 'Wu'), ('ffn_w_down', 0, 'Wd')],
    'mix1': [('c_w_in', None, 'c_w_in'), ('c_w_out', None, 'c_w_out')],
    'ffn1': [('ffn_w_gate', 1, 'Wg'), ('ffn_w_up', 1, 'Wu'), ('ffn_w_down', 1, 'Wd')],
}
TRANSPOSED = ('ffn_w_gate', 'ffn_w_up')


def _stored(name, a):
    return jnp.swapaxes(a, 1, 2) if name in TRANSPOSED else a


def _stored_axis(name):
    return 1 if name in TRANSPOSED else SHARD_AXIS[name]


GRAD_STAGES = {
    'late1': ([('c_w_in', None), ('c_w_out', None), ('ffn_w_gate', 1), ('ffn_w_up', 1), ('ffn_w_down', 1)],
              ['c_norm', 'c_ln_g', 'c_ln_b', 'c_w_s', 'c_b_s', 'final_norm']),
    'late0': ([('ffn_w_gate', 0), ('ffn_w_up', 0), ('ffn_w_down', 0)], ['ffn_norm', 'ffn_conv_w', 'ffn_conv_b']),
    'mid': ([('ab_w_out', None)], ['ab_conv_w', 'ab_conv_b', 'ab_w_rg_a', 'ab_b_rg_a', 'ab_w_rg_x', 'ab_b_rg_x', 'ab_lambda']),
    'last': ([('ab_w_in', None)], ['ab_norm', 'ab_q_norm', 'ab_w_q_b', 'ab_kv_norm', 'ab_w_kv_b']),
}


def _gather_early(local):
    small = [_bf(local[n]) if n in MATRICES else lax.bitcast_convert_type(local[n], BF16) for n in SMALL_SHARDED]
    gathered, zero = _all_gather([_bf(local[n]) for n in EARLY] + [_pack_slabs(small, ())])
    full = {n: local[n] for n in REPLICATED}
    for n, g in zip(EARLY, gathered):
        full[n] = _from_chunks(g, SHARD_AXIS[n])
    for n, p in zip(SMALL_SHARDED, _unpack_slabs(gathered[-1], [s.shape for s in small])):
        full[n] = _from_chunks(p if n in MATRICES else lax.bitcast_convert_type(p, F32), SHARD_AXIS[n])
    return full, zero


def kernel(x, positions, ab_norm, ab_w_in, ab_q_norm, ab_w_q_b, ab_kv_norm, ab_w_kv_b, ab_conv_w, ab_conv_b, ab_w_rg_a, ab_b_rg_a, ab_w_rg_x, ab_b_rg_x, ab_lambda, ab_w_out, c_norm, c_w_in, c_ln_g, c_ln_b, c_w_s, c_b_s, c_w_out, ffn_norm, ffn_w_gate, ffn_w_up, ffn_conv_w, ffn_conv_b, ffn_w_down, final_norm, loss_target, m_ab_norm, m_ab_w_in, m_ab_q_norm, m_ab_w_q_b, m_ab_kv_norm, m_ab_w_kv_b, m_ab_conv_w, m_ab_conv_b, m_ab_w_rg_a, m_ab_b_rg_a, m_ab_w_rg_x, m_ab_b_rg_x, m_ab_lambda, m_ab_w_out, m_c_norm, m_c_w_in, m_c_ln_g, m_c_ln_b, m_c_w_s, m_c_b_s, m_c_w_out, m_ffn_norm, m_ffn_w_gate, m_ffn_w_up, m_ffn_conv_w, m_ffn_conv_b, m_ffn_w_down, m_final_norm, v_ab_norm, v_ab_w_in, v_ab_q_norm, v_ab_w_q_b, v_ab_kv_norm, v_ab_w_kv_b, v_ab_conv_w, v_ab_conv_b, v_ab_w_rg_a, v_ab_b_rg_a, v_ab_w_rg_x, v_ab_b_rg_x, v_ab_lambda, v_ab_w_out, v_c_norm, v_c_w_in, v_c_ln_g, v_c_ln_b, v_c_w_s, v_c_b_s, v_c_w_out, v_ffn_norm, v_ffn_w_gate, v_ffn_w_up, v_ffn_conv_w, v_ffn_conv_b, v_ffn_w_down, v_final_norm):
    given = dict(locals())
    local = {n: given[n] for n in WEIGHTS}
    b, seq, d = x.shape
    t = b * seq

    me = (4 * lax.axis_index("x") + 2 * lax.axis_index("y") + lax.axis_index("c")).astype(jnp.int32)
    me1 = me.reshape(1)

    full, zero = _gather_early(local)
    gathers = {}
    for stage, members in LATE_STAGES.items():
        srcs = [_bf(_stored(n, local[n] if layer is None else local[n][layer:layer + 1]) + zero) for n, layer, _ in members]
        gathers[stage] = _start_exchange('gather_' + stage, srcs, scatter=False)
        zero = gathers[stage][4]
    w = _prepare(full)
    w['ab_norm'] = w['ab_norm'] + zero

    def late_weights(stage, after):
        _, lands = _wait_exchange('gather_' + stage, gathers[stage], after, scatter=False)
        whole = [l.reshape(1, -1, l.shape[-1]) if _stored_axis(n) == 1 else _merge_columns(l, n)
                 for (n, _, _), l in zip(LATE_STAGES[stage], lands)]
        if stage == 'out0':
            return _prepare_out(whole[0])
        return {key: a[0] for (_, _, key), a in zip(LATE_STAGES[stage], whole)}

    scatters = {}

    def start_scatter(stage, g):
        whole = _unprepare(g)
        big, small = GRAD_STAGES[stage]
        slab = [_to_chunks(whole[n], SHARD_AXIS[n]) if n in SHARD_AXIS else jnp.broadcast_to(whole[n][None], (N_DEV,) + whole[n].shape)
                for n in small]
        own = [whole[n].reshape(N_DEV, 1, whole[n].shape[1] // N_DEV, whole[n].shape[2])
               if whole[n].dtype == BF16 and _stored_axis(n) == 1 else
               _split_chunks(whole[n], _stored_axis(n), n + ('' if layer is None else str(layer))) for n, layer in big]
        own.append(_bf(_pack_slabs(slab, (N_DEV,)))[:, None])
        scatters[stage] = _start_exchange('scatter_' + stage, own, scatter=True)
        return scatters[stage][4]

    posb = jnp.broadcast_to(positions.astype(F32).reshape(t, 1), (t, LANES))
    loss, dx, grads = _local_step(x.reshape(t, d), posb, loss_target.reshape(t, d), w, seq, late_weights, start_scatter)
    start_scatter('last', grads)
    after = scatters['last'][5]

    me1 = me.reshape(1)
    updated = {}
    for stage, (big, small) in GRAD_STAGES.items():
        owns, landed = _wait_exchange('scatter_' + stage, scatters[stage], after, scatter=True)
        for (n, layer), own, land in zip(big, owns, landed):
            updated[n] = _sum_and_adamw(me1, land, own, _stored(n, given[n]), _stored(n, given['m_' + n]), _stored(n, given['v_' + n]),
                                        n + ('' if layer is None else str(layer)), layer, updated.get(n))
        pack_small = lambda prefix: _pack_slabs([given[prefix + n] for n in small], ())[None]
        packed = _sum_and_adamw(me1, landed[-1], owns[-1], pack_small(''), pack_small('m_'), pack_small('v_'), 'small_' + stage)
        unpacked = [_unpack_slabs(p[0], [local[n].shape for n in small]) for p in packed]
        for i, n in enumerate(small):
            updated[n] = [u[i] for u in unpacked]
        after = sum([updated[n][1][:1, :1, :1] for n, _ in big], packed[1][:1, :1, :1])
    total = lax.psum(loss[0, 0], ("x", "y", "c"))
    return (total, dx.reshape(b, seq, d), *[_stored(n, updated[n][kind]) for kind in range(4) for n in WEIGHTS])
```

```python
import math

import jax
import jax.numpy as jnp
from jax import lax
from jax.experimental import pallas as pl
from jax.experimental.pallas import tpu as pltpu

F32 = jnp.float32
BF16 = jnp.bfloat16
MESH = pl.DeviceIdType.MESH

N_DEV = 8
LANES = 128
HALO = 8
VMEM_LIMIT = 56 << 20

NORM_EPS = 1e-6
HEADS = 8
HEAD_PAD = 128
QK_NOPE = 64
QK_ROPE = 32
ROPE_HALF = 16
ROPE_BASE = 10000.0
ATTN_SCALE = (QK_NOPE + QK_ROPE) ** -0.5
LRU_C = 8.0
LRU_W = 512
CHUNK = 128
SGU_GROUPS = 8
D_FF = 2816
FF_BLOCKS = 2

ADAM_LR, ADAM_B1, ADAM_B2, ADAM_EPS, ADAM_WD, ADAM_STEP = 0.001, 0.9, 0.999, 1e-08, 0.01, 10

WEIGHTS = ['ab_norm', 'ab_w_in', 'ab_q_norm', 'ab_w_q_b', 'ab_kv_norm', 'ab_w_kv_b', 'ab_conv_w', 'ab_conv_b',
           'ab_w_rg_a', 'ab_b_rg_a', 'ab_w_rg_x', 'ab_b_rg_x', 'ab_lambda', 'ab_w_out', 'c_norm', 'c_w_in', 'c_ln_g',
           'c_ln_b', 'c_w_s', 'c_b_s', 'c_w_out', 'ffn_norm', 'ffn_w_gate', 'ffn_w_up', 'ffn_conv_w', 'ffn_conv_b',
           'ffn_w_down', 'final_norm']
SHARD_AXIS = {'ab_w_in': 2, 'ab_w_q_b': 2, 'ab_w_kv_b': 2, 'ab_conv_w': 2, 'ab_w_out': 1, 'c_norm': 1, 'c_w_in': 2,
              'c_ln_g': 1, 'c_ln_b': 1, 'c_w_out': 1, 'ffn_w_gate': 2, 'ffn_w_up': 2, 'ffn_conv_w': 2, 'ffn_w_down': 1}
MATRICES = ['ab_w_in', 'ab_w_q_b', 'ab_w_kv_b', 'ab_w_out', 'c_w_in', 'c_w_out', 'ffn_w_gate', 'ffn_w_up', 'ffn_w_down']
BIG = ['ab_w_in', 'c_w_in', 'ffn_w_gate', 'ffn_w_up', 'ab_w_out', 'c_w_out', 'ffn_w_down']
REPLICATED = [n for n in WEIGHTS if n not in SHARD_AXIS]
SMALL_SHARDED = [n for n in WEIGHTS if n in SHARD_AXIS and n not in BIG]


def _bf(x):
    return x.astype(BF16)


def _nn(a, b):
    return lax.dot_general(_bf(a), _bf(b), (((1,), (0,)), ((), ())), preferred_element_type=F32)


def _nt(a, b):
    return lax.dot_general(_bf(a), _bf(b), (((1,), (1,)), ((), ())), preferred_element_type=F32)


def _tn(a, b):
    return lax.dot_general(_bf(a), _bf(b), (((0,), (0,)), ((), ())), preferred_element_type=F32)


def _rms(x, g):
    return x * lax.rsqrt(jnp.mean(x * x, axis=-1, keepdims=True) + NORM_EPS) * g


def _layer_norm(x, g, b):
    xc = x - jnp.mean(x, axis=-1, keepdims=True)
    return xc * lax.rsqrt(jnp.mean(xc * xc, axis=-1, keepdims=True) + NORM_EPS) * g + b


def _gelu(x):
    return jax.nn.gelu(x)


STRIP = 16
STRIP_LANES = 384
GELU_C = math.sqrt(2.0 / math.pi)
GELU_A = 0.044715


def _gelu_and_grad(x):
    x2 = x * x
    t = jnp.tanh(x * (GELU_C + (GELU_C * GELU_A) * x2))
    half_x = 0.5 * x
    one_plus_t = 1.0 + t
    return half_x * one_plus_t, 0.5 * one_plus_t + half_x * (1.0 - t * t) * (GELU_C + (3.0 * GELU_C * GELU_A) * x2)


def _colsum(x):
    return jnp.sum(x, axis=0, keepdims=True)


def _softplus(x):
    return jnp.maximum(x, 0.0) + jnp.log1p(jnp.exp(-jnp.abs(x)))


@jax.custom_vjp
def _decay(x):
    a = jnp.exp(x)
    y = 2.0 * x
    series = -y * (1.0 + y * (1 / 2 + y * (1 / 6 + y * (1 / 24 + y * (1 / 120 + y * (1 / 720))))))
    return a, jnp.where(y < -0.3, 1.0 - a * a, series)


def _decay_fwd(x):
    a, gap = _decay(x)
    return (a, gap), a


def _decay_bwd(a, cts):
    return (a * (cts[0] - 2.0 * a * cts[1]),)


_decay.defvjp(_decay_fwd, _decay_bwd)


def _accumulate(ref, val, first):
    @pl.when(first)
    def _():
        ref[...] = val

    @pl.when(jnp.logical_not(first))
    def _():
        ref[...] += val


def _params(n_axes=1):
    return pltpu.CompilerParams(dimension_semantics=("arbitrary",) * n_axes, vmem_limit_bytes=VMEM_LIMIT)


def _row(tm, n):
    return pl.BlockSpec((tm, n), lambda i: (i, 0))


def _const(shape):
    nd = len(shape)
    return pl.BlockSpec(shape, lambda i: (0,) * nd, pipeline_mode=pl.Buffered(1))


def _prev_halo(tm, n):
    return pl.BlockSpec((HALO, n), lambda i: (jnp.maximum(i * (tm // HALO) - 1, 0), 0))


HALO_BF16 = 16


def _sds(shape, dtype=F32):
    return jax.ShapeDtypeStruct(shape, dtype)


def _rope_tables(posb):
    lane = lax.broadcasted_iota(jnp.int32, posb.shape, 1)
    in_rope = jnp.logical_and(lane >= QK_NOPE, lane < QK_NOPE + QK_ROPE)
    j = (lane & (ROPE_HALF - 1)).astype(F32)
    inv_freq = jnp.exp((-math.log(ROPE_BASE)) * j / ROPE_HALF)
    ang = posb * inv_freq
    return jnp.where(in_rope, jnp.cos(ang), 1.0), jnp.where(in_rope, jnp.sin(ang), 0.0)


def _rot(q):
    n = q.shape[1]
    lane = lax.broadcasted_iota(jnp.int32, q.shape, 1) & (HEAD_PAD - 1)
    first_half = jnp.where(lane >= QK_NOPE, -pltpu.roll(q, n - ROPE_HALF, 1), 0.0)
    second_half = jnp.where(lane < QK_NOPE + QK_ROPE, pltpu.roll(q, ROPE_HALF, 1), 0.0)
    return jnp.where(lane < QK_NOPE + ROPE_HALF, first_half, second_half)


def _rope(q, cos_t, sin_t):
    return q * cos_t + _rot(q) * sin_t


def _rope_transpose(dq, cos_t, sin_t):
    return dq * cos_t - _rot(dq * sin_t)


def _tile_heads(t):
    return jnp.concatenate([t] * HEADS, axis=1)


Q_LORA, KV_LORA = 256, 128
Z_KPE = Q_LORA + KV_LORA
Z_LRU = Z_KPE + HEAD_PAD
Z_GATE = Z_LRU + LRU_W
Z_WIDTH = Z_GATE + LRU_W


def _ab_in_fwd(x, posb, w, tm):
    t, d = x.shape

    def body(x_ref, pos_ref, gn_ref, win_ref, qn_ref, wq_ref, kvn_ref, wk_ref, wv_ref, q_out, k_out, v_out, xl_out, gate_out):
        hn = _rms(x_ref[...], gn_ref[...])
        z = _nn(hn, win_ref[...])
        cqn = _rms(z[:, :Q_LORA], qn_ref[...])
        kvn = _rms(z[:, Q_LORA:Z_KPE], kvn_ref[...])
        cos_t, sin_t = _rope_tables(pos_ref[...])
        q_out[...] = _rope(_nn(cqn, wq_ref[...]), _tile_heads(cos_t), _tile_heads(sin_t))
        kpe = _rope(z[:, Z_KPE:Z_LRU], cos_t, sin_t)
        k_out[...] = _nn(kvn, wk_ref[...]) + _tile_heads(kpe)
        v_out[...] = _nn(kvn, wv_ref[...])
        xl_out[...] = z[:, Z_LRU:Z_GATE]
        gate_out[...] = z[:, Z_GATE:]

    hp = HEADS * HEAD_PAD
    return pl.pallas_call(
        body, name="ab_in_fwd", grid=(t // tm,),
        in_specs=[_row(tm, d), _row(tm, LANES), _const((1, d)), _const((d, Z_WIDTH)), _const((1, Q_LORA)), _const((Q_LORA, hp)),
                  _const((1, KV_LORA)), _const((KV_LORA, hp)), _const((KV_LORA, hp))],
        out_specs=[_row(tm, hp), _row(tm, hp), _row(tm, hp), _row(tm, LRU_W), _row(tm, LRU_W)],
        out_shape=[_sds((t, hp)), _sds((t, hp)), _sds((t, hp)), _sds((t, LRU_W)), _sds((t, LRU_W))],
        compiler_params=_params(),
    )(x, posb, w['ab_norm'], w['W_in'], w['ab_q_norm'], w['Wq'], w['ab_kv_norm'], w['Wk'], w['Wv'])


def _ab_in_bwd(x, posb, w, dq, dk, dv, dxl, dgate, dres, tm):
    t, d = x.shape
    hp = HEADS * HEAD_PAD

    def body(x_ref, pos_ref, gn_ref, win_ref, qn_ref, wq_ref, kvn_ref, wk_ref, wv_ref, dq_ref, dk_ref, dv_ref, dxl_ref, dgate_ref,
             dres_ref, dx_out, dgn_out, dwin_out, dqn_out, dwq_out, dkvn_out, dwk_out, dwv_out):
        first = pl.program_id(0) == 0
        hn, vjp_in = jax.vjp(_rms, x_ref[...], gn_ref[...])
        z = _nn(hn, win_ref[...])
        cqn, vjp_q = jax.vjp(_rms, z[:, :Q_LORA], qn_ref[...])
        kvn, vjp_kv = jax.vjp(_rms, z[:, Q_LORA:Z_KPE], kvn_ref[...])
        cos_t, sin_t = _rope_tables(pos_ref[...])
        dq0 = _rope_transpose(dq_ref[...], _tile_heads(cos_t), _tile_heads(sin_t))
        dk0 = dk_ref[...]
        dv0 = dv_ref[...]
        dkpe = dk0[:, :HEAD_PAD]
        for h in range(1, HEADS):
            dkpe = dkpe + dk0[:, h * HEAD_PAD:(h + 1) * HEAD_PAD]
        dkpe = _rope_transpose(dkpe, cos_t, sin_t)
        _accumulate(dwq_out, _tn(cqn, dq0), first)
        _accumulate(dwk_out, _tn(kvn, dk0), first)
        _accumulate(dwv_out, _tn(kvn, dv0), first)
        dcq, dqn = vjp_q(_nt(dq0, wq_ref[...]))
        dckv, dkvn = vjp_kv(_nt(dk0, wk_ref[...]) + _nt(dv0, wv_ref[...]))
        _accumulate(dqn_out, dqn, first)
        _accumulate(dkvn_out, dkvn, first)
        dz = jnp.concatenate([dcq, dckv, dkpe, dxl_ref[...], dgate_ref[...]], axis=1)
        _accumulate(dwin_out, _tn(hn, dz), first)
        dx, dgn = vjp_in(_nt(dz, win_ref[...]))
        _accumulate(dgn_out, dgn, first)
        dx_out[...] = dx + dres_ref[...]

    return pl.pallas_call(
        body, name="ab_in_bwd", grid=(t // tm,),
        in_specs=[_row(tm, d), _row(tm, LANES), _const((1, d)), _const((d, Z_WIDTH)), _const((1, Q_LORA)), _const((Q_LORA, hp)),
                  _const((1, KV_LORA)), _const((KV_LORA, hp)), _const((KV_LORA, hp)),
                  _row(tm, hp), _row(tm, hp), _row(tm, hp), _row(tm, LRU_W), _row(tm, LRU_W), _row(tm, d)],
        out_specs=[_row(tm, d), _const((1, d)), _const((d, Z_WIDTH)), _const((1, Q_LORA)), _const((Q_LORA, hp)),
                   _const((1, KV_LORA)), _const((KV_LORA, hp)), _const((KV_LORA, hp))],
        out_shape=[_sds((t, d)), _sds((1, d)), _sds((d, Z_WIDTH)), _sds((1, Q_LORA)), _sds((Q_LORA, hp)),
                   _sds((1, KV_LORA)), _sds((KV_LORA, hp)), _sds((KV_LORA, hp))],
        compiler_params=_params(),
    )(x, posb, w['ab_norm'], w['W_in'], w['ab_q_norm'], w['Wq'], w['ab_kv_norm'], w['Wk'], w['Wv'], dq, dk, dv, dxl, dgate, dres)


def _attn_probs(q_blk, k_ext, tq):
    ext = k_ext.shape[0]
    s = lax.dot_general(q_blk, k_ext, (((1,), (1,)), ((), ())), preferred_element_type=F32) * ATTN_SCALE
    causal = lax.broadcasted_iota(jnp.int32, (tq, tq), 1) <= lax.broadcasted_iota(jnp.int32, (tq, tq), 0)
    diag = jnp.where(causal, s[:, ext - tq:], -1e30)
    s = diag if ext == tq else jnp.concatenate([s[:, :ext - tq], diag], axis=1)
    p = jnp.exp(s - jnp.max(s, axis=1, keepdims=True))
    return p / jnp.sum(p, axis=1, keepdims=True)


def _attn_fwd(q, k, v, tq):
    b, s, hp = q.shape
    blk = pl.BlockSpec((1, s, HEAD_PAD), lambda bi, h: (bi, 0, h))

    def body(q_ref, k_ref, v_ref, o_ref):
        kb = _bf(k_ref[0])
        vb = _bf(v_ref[0])
        for i in range(s // tq):
            ext = (i + 1) * tq
            p = _attn_probs(_bf(q_ref[0, i * tq:ext, :]), kb[:ext], tq)
            o_ref[0, i * tq:ext, :] = lax.dot_general(_bf(p), vb[:ext], (((1,), (0,)), ((), ())), preferred_element_type=F32)

    return pl.pallas_call(body, name="attn_fwd", grid=(b, HEADS), in_specs=[blk, blk, blk], out_specs=blk,
                          out_shape=_sds((b, s, hp)), compiler_params=_params(2))(q, k, v)


def _attn_bwd(q, k, v, do, tq):
    b, s, hp = q.shape
    blk = pl.BlockSpec((1, s, HEAD_PAD), lambda bi, h: (bi, 0, h))

    def body(q_ref, k_ref, v_ref, do_ref, dq_ref, dk_ref, dv_ref):
        kb = _bf(k_ref[0])
        vb = _bf(v_ref[0])
        dk_ref[...] = jnp.zeros_like(dk_ref)
        dv_ref[...] = jnp.zeros_like(dv_ref)
        for i in range(s // tq):
            ext = (i + 1) * tq
            qb = _bf(q_ref[0, i * tq:ext, :])
            dob = _bf(do_ref[0, i * tq:ext, :])
            p = _attn_probs(qb, kb[:ext], tq)
            dv_ref[0, :ext, :] += lax.dot_general(_bf(p), dob, (((0,), (0,)), ((), ())), preferred_element_type=F32)
            dp = lax.dot_general(dob, vb[:ext], (((1,), (1,)), ((), ())), preferred_element_type=F32)
            ds = _bf(p * (dp - jnp.sum(p * dp, axis=1, keepdims=True)) * ATTN_SCALE)
            dq_ref[0, i * tq:ext, :] = lax.dot_general(ds, kb[:ext], (((1,), (0,)), ((), ())), preferred_element_type=F32)
            dk_ref[0, :ext, :] += lax.dot_general(ds, qb, (((0,), (0,)), ((), ())), preferred_element_type=F32)

    return pl.pallas_call(body, name="attn_bwd", grid=(b, HEADS), in_specs=[blk, blk, blk, blk], out_specs=[blk, blk, blk],
                          out_shape=[_sds((b, s, hp))] * 3, compiler_params=_params(2))(q, k, v, do)


LRU_CONV = 4


def _lru_point(pre_a, pre_x, xc, lam):
    r = jax.nn.sigmoid(pre_a)
    i = jax.nn.sigmoid(pre_x)
    a, gap = _decay(-LRU_C * r * _softplus(-lam))
    return a, jnp.sqrt(gap) * (i * xc)


def _causal_conv(pad_ref, x, halo, first_in_seq, w, taps):
    tm = x.shape[0]
    pad_ref[:HALO, :] = jnp.where(first_in_seq, 0.0, halo)
    pad_ref[HALO:, :] = x
    y = w[taps - 1:taps, :] * x
    for k in range(taps - 1):
        off = HALO - (taps - 1) + k
        y = y + w[k:k + 1, :] * pad_ref[off:off + tm, :]
    return y


def _conv_taps(pad_ref, r, cols, taps):
    blocks = [pad_ref[r + j * HALO:r + (j + 1) * HALO, cols] for j in range(1 + STRIP // HALO)]
    sub = lax.broadcasted_iota(jnp.int32, blocks[0].shape, 0)
    out = []
    for k in range(taps - 1):
        s = taps - 1 - k
        rolled = [pltpu.roll(b, s, 0) for b in blocks]
        out.append(jnp.concatenate([jnp.where(sub < s, rolled[j], rolled[j + 1]) for j in range(STRIP // HALO)], axis=0))
    out.append(jnp.concatenate(blocks[1:], axis=0))
    return out


def _causal_conv_wgrad(pad_ref, dy, taps):
    tm = dy.shape[0]
    return jnp.concatenate([_colsum(dy * pad_ref[HALO - (taps - 1) + k:HALO - (taps - 1) + k + tm, :]) for k in range(taps)], axis=0)


def _causal_conv_transpose(pad_ref, dy, halo_next, last_in_seq, w, taps):
    tm = dy.shape[0]
    pad_ref[:tm, :] = dy
    pad_ref[tm:, :] = jnp.where(last_in_seq, 0.0, halo_next)
    dx = w[taps - 1:taps, :] * dy
    for k in range(taps - 1):
        off = (taps - 1) - k
        dx = dx + w[k:k + 1, :] * pad_ref[off:off + tm, :]
    return dx


def _lru_fwd(xl, gate, w, ts, seq):
    t, n = xl.shape
    tiles_per_seq = seq // ts

    def body(xl_ref, halo_ref, gate_ref, cw_ref, cb_ref, wa_ref, ba_ref, wx_ref, bx_ref, lam_ref, y_out, h_out, pad_ref, a_ref, b_ref, carry_ref):
        first_in_seq = pl.program_id(0) % tiles_per_seq == 0
        xc = _causal_conv(pad_ref, xl_ref[...], halo_ref[...], first_in_seq, cw_ref[...], LRU_CONV) + cb_ref[...]
        a, bx = _lru_point(_nn(xc, wa_ref[...]) + ba_ref[...], _nn(xc, wx_ref[...]) + bx_ref[...], xc, lam_ref[...])
        a_ref[...] = a
        b_ref[...] = bx

        @pl.when(first_in_seq)
        def _():
            carry_ref[...] = jnp.zeros_like(carry_ref)

        def step(r, h):
            h = a_ref[pl.ds(r, 1), :] * h + b_ref[pl.ds(r, 1), :]
            h_out[pl.ds(r, 1), :] = h
            return h

        carry_ref[...] = lax.fori_loop(0, ts, step, carry_ref[...], unroll=8)
        y_out[...] = h_out[...] * _gelu(gate_ref[...])

    return pl.pallas_call(
        body, name="lru_fwd", grid=(t // ts,),
        in_specs=[_row(ts, n), _prev_halo(ts, n), _row(ts, n), _const((LRU_CONV, n)), _const((1, n)), _const((n, n)), _const((1, n)),
                  _const((n, n)), _const((1, n)), _const((1, n))],
        out_specs=[_row(ts, n), _row(ts, n)], out_shape=[_sds((t, n)), _sds((t, n))],
        scratch_shapes=[pltpu.VMEM((HALO + ts, n), F32), pltpu.VMEM((ts, n), F32), pltpu.VMEM((ts, n), F32), pltpu.VMEM((1, n), F32)],
        compiler_params=_params(),
    )(xl, xl, gate, w['ab_conv_w'], w['ab_conv_b'], w['Wa'], w['ab_b_rg_a'], w['Wx'], w['ab_b_rg_x'], w['ab_lambda'])


def _lru_bwd(xl, gate, hs, dy, w, ts, seq):
    t, n = xl.shape
    tiles_per_seq = seq // ts
    n_tiles = t // ts

    def rev(i):
        return n_tiles - 1 - i

    row = pl.BlockSpec((ts, n), lambda i: (rev(i), 0))
    prev = pl.BlockSpec((HALO, n), lambda i: (jnp.maximum(rev(i) * (ts // HALO) - 1, 0), 0))
    acc = lambda shape: pl.BlockSpec(shape, lambda i: (0,) * len(shape))

    def body(xl_ref, xhalo_ref, gate_ref, h_ref, hhalo_ref, dy_ref, cw_ref, cb_ref, wa_ref, ba_ref, wx_ref, bx_ref, lam_ref,
             dxl_out, dgate_out, dcw_out, dcb_out, dwa_out, dba_out, dwx_out, dbx_out, dlam_out,
             pad_ref, padh_ref, padd_ref, a_ref, g_ref, carry_ref, dhalo_ref):
        step_id = pl.program_id(0)
        first = step_id == 0
        tile = rev(step_id)
        first_in_seq = tile % tiles_per_seq == 0
        last_in_seq = tile % tiles_per_seq == tiles_per_seq - 1
        cw = cw_ref[...]
        xc = _causal_conv(pad_ref, xl_ref[...], xhalo_ref[...], first_in_seq, cw, LRU_CONV) + cb_ref[...]
        pre_a = _nn(xc, wa_ref[...]) + ba_ref[...]
        pre_x = _nn(xc, wx_ref[...]) + bx_ref[...]
        (a, _), vjp_point = jax.vjp(_lru_point, pre_a, pre_x, xc, lam_ref[...])
        h = h_ref[...]
        _, vjp_out = jax.vjp(lambda h_, g_: h_ * _gelu(g_), h, gate_ref[...])
        dh, dgate = vjp_out(dy_ref[...])
        dgate_out[...] = dgate
        a_ref[...] = a
        g_ref[...] = dh

        @pl.when(last_in_seq)
        def _():
            carry_ref[...] = jnp.zeros_like(carry_ref)

        def step(j, c):
            r = ts - 1 - j
            g = g_ref[pl.ds(r, 1), :] + c
            g_ref[pl.ds(r, 1), :] = g
            return a_ref[pl.ds(r, 1), :] * g

        carry_ref[...] = lax.fori_loop(0, ts, step, carry_ref[...], unroll=8)
        g = g_ref[...]
        padh_ref[:HALO, :] = jnp.where(first_in_seq, 0.0, hhalo_ref[...])
        padh_ref[HALO:, :] = h
        dpre_a, dpre_x, dxc, dlam = vjp_point((g * padh_ref[HALO - 1:HALO - 1 + ts, :], g))
        dxc = dxc + _nt(dpre_a, wa_ref[...]) + _nt(dpre_x, wx_ref[...])
        _accumulate(dwa_out, _tn(xc, dpre_a), first)
        _accumulate(dwx_out, _tn(xc, dpre_x), first)
        _accumulate(dba_out, _colsum(dpre_a), first)
        _accumulate(dbx_out, _colsum(dpre_x), first)
        _accumulate(dlam_out, dlam, first)
        _accumulate(dcb_out, _colsum(dxc), first)
        _accumulate(dcw_out, _causal_conv_wgrad(pad_ref, dxc, LRU_CONV), first)
        dxl_out[...] = _causal_conv_transpose(padd_ref, dxc, dhalo_ref[...], last_in_seq, cw, LRU_CONV)
        dhalo_ref[...] = dxc[:HALO, :]

    return pl.pallas_call(
        body, name="lru_bwd", grid=(n_tiles,),
        in_specs=[row, prev, row, row, prev, row, _const((LRU_CONV, n)), _const((1, n)), _const((n, n)), _const((1, n)),
                  _const((n, n)), _const((1, n)), _const((1, n))],
        out_specs=[row, row, acc((LRU_CONV, n)), acc((1, n)), acc((n, n)), acc((1, n)), acc((n, n)), acc((1, n)), acc((1, n))],
        out_shape=[_sds((t, n)), _sds((t, n)), _sds((LRU_CONV, n)), _sds((1, n)), _sds((n, n)), _sds((1, n)), _sds((n, n)),
                   _sds((1, n)), _sds((1, n))],
        scratch_shapes=[pltpu.VMEM((HALO + ts, n), F32), pltpu.VMEM((HALO + ts, n), F32), pltpu.VMEM((ts + HALO, n), F32),
                        pltpu.VMEM((ts, n), F32), pltpu.VMEM((ts, n), F32), pltpu.VMEM((1, n), F32), pltpu.VMEM((HALO, n), F32)],
        compiler_params=_params(),
    )(xl, xl, gate, hs, hs, dy, w['ab_conv_w'], w['ab_conv_b'], w['Wa'], w['ab_b_rg_a'], w['Wx'], w['ab_b_rg_x'], w['ab_lambda'])


def _ab_out_fwd(x, o, y, w, tm):
    t, d = x.shape
    hp = o.shape[1]

    def body(x_ref, o_ref, y_ref, wa_ref, wb_ref, h_out):
        h_out[...] = x_ref[...] + _nn(o_ref[...], wa_ref[...]) + _nn(y_ref[...], wb_ref[...])

    return pl.pallas_call(body, name="ab_out_fwd", grid=(t // tm,),
                          in_specs=[_row(tm, d), _row(tm, hp), _row(tm, LRU_W), _const((hp, d)), _const((LRU_W, d))],
                          out_specs=_row(tm, d), out_shape=_sds((t, d)), compiler_params=_params())(x, o, y, w['Wo_a'], w['Wo_b'])


def _ab_out_bwd(o, y, dh, w, tm):
    t, d = dh.shape
    hp = o.shape[1]

    def body(o_ref, y_ref, dh_ref, wa_ref, wb_ref, do_out, dy_out, dwa_out, dwb_out):
        first = pl.program_id(0) == 0
        dh_t = dh_ref[...]
        do_out[...] = _nt(dh_t, wa_ref[...])
        dy_out[...] = _nt(dh_t, wb_ref[...])
        _accumulate(dwa_out, _tn(o_ref[...], dh_t), first)
        _accumulate(dwb_out, _tn(y_ref[...], dh_t), first)

    return pl.pallas_call(body, name="ab_out_bwd", grid=(t // tm,),
                          in_specs=[_row(tm, hp), _row(tm, LRU_W), _row(tm, d), _const((hp, d)), _const((LRU_W, d))],
                          out_specs=[_row(tm, hp), _row(tm, LRU_W), _const((hp, d)), _const((LRU_W, d))],
                          out_shape=[_sds((t, hp)), _sds((t, LRU_W)), _sds((hp, d)), _sds((LRU_W, d))],
                          compiler_params=_params())(o, y, dh, w['Wo_a'], w['Wo_b'])


FFN_CONV = 3


def _ffn_a_fwd(h, norm, wg, wu, tm):
    t, d = h.shape
    fb = D_FF // FF_BLOCKS

    def body(h_ref, gn_ref, wg_ref, wu_ref, g_out, u_out, hn_out):
        hn = _bf(_rms(h_ref[...], gn_ref[...]))
        hn_out[0] = hn
        g_out[...] = _bf(_nt(hn, wg_ref[...]))
        u_out[...] = _bf(_nt(hn, wu_ref[...]))

    wspec = pl.BlockSpec((fb, d), lambda f, i: (f, 0))
    ospec = pl.BlockSpec((tm, fb), lambda f, i: (i, f))
    return pl.pallas_call(
        body, name="ffn_a_fwd", grid=(FF_BLOCKS, t // tm),
        in_specs=[pl.BlockSpec((tm, d), lambda f, i: (i, 0)), pl.BlockSpec((1, d), lambda f, i: (0, 0)), wspec, wspec],
        out_specs=[ospec, ospec, pl.BlockSpec((1, tm, d), lambda f, i: (f, i, 0))],
        out_shape=[_sds((t, D_FF), BF16), _sds((t, D_FF), BF16), _sds((FF_BLOCKS, t, d), BF16)], compiler_params=_params(2))(h, norm, wg, wu)


def _ffn_b_fwd(g, u, h, cw, cb, wd, tm, seq):
    t, d = h.shape
    tiles_per_seq = seq // tm

    def body(g_ref, halo_ref, u_ref, h_ref, cw_ref, cb_ref, wd_ref, h_out, pad_ref, act_ref):
        pad_ref[:HALO, :] = jnp.where(pl.program_id(0) % tiles_per_seq == 0, 0.0, halo_ref[HALO_BF16 - HALO:, :].astype(F32))
        pad_ref[HALO:, :] = g_ref[...].astype(F32)
        cw = cw_ref[...]
        cb = cb_ref[...]
        for c0 in range(0, D_FF, STRIP_LANES):
            cols = slice(c0, min(c0 + STRIP_LANES, D_FF))
            for r in range(0, tm, STRIP):
                taps = _conv_taps(pad_ref, r, cols, FFN_CONV)
                gc = cb[:, cols] + cw[0:1, cols] * taps[0] + cw[1:2, cols] * taps[1] + cw[2:3, cols] * taps[2]
                act_ref[r:r + STRIP, cols] = _bf(_gelu(gc) * u_ref[r:r + STRIP, cols].astype(F32))
        h_out[...] = h_ref[...] + _nn(act_ref[...], wd_ref[...])

    halo = pl.BlockSpec((HALO_BF16, D_FF), lambda i: (jnp.maximum(i * (tm // HALO_BF16) - 1, 0), 0))
    return pl.pallas_call(body, name="ffn_b_fwd", grid=(t // tm,),
                          in_specs=[_row(tm, D_FF), halo, _row(tm, D_FF), _row(tm, d), _const((FFN_CONV, D_FF)),
                                    _const((1, D_FF)), _const((D_FF, d))],
                          out_specs=_row(tm, d), out_shape=_sds((t, d)),
                          scratch_shapes=[pltpu.VMEM((HALO + tm, D_FF), F32), pltpu.VMEM((tm, D_FF), BF16)],
                          compiler_params=_params())(g, g, u, h, cw, cb, wd)


def _ffn_b_bwd(g, u, dout, cw, cb, wd, tm, seq):
    t, d = dout.shape
    fb = D_FF // FF_BLOCKS
    tiles_per_seq = seq // tm

    def body(g_ref, halo_ref, u_ref, dout_ref, cw_ref, cb_ref, wd_ref, dgc_out, du_out, dwd_out, dcw_out, dcb_out,
             pad_ref, dact_ref, act_ref, acc_ref, dwd_acc):
        i = pl.program_id(1)
        first = i == 0
        pad_ref[:HALO, :] = jnp.where(i % tiles_per_seq == 0, 0.0, halo_ref[HALO_BF16 - HALO:, :].astype(F32))
        pad_ref[HALO:, :] = g_ref[...].astype(F32)
        dout_b = _bf(dout_ref[...])
        dact_ref[...] = _nt(dout_b, wd_ref[...])
        cw = cw_ref[...]
        cb = cb_ref[...]
        fold = lambda a: a[:HALO] + a[HALO:]
        for c0 in range(0, fb, STRIP_LANES):
            cols = slice(c0, min(c0 + STRIP_LANES, fb))
            sums = [jnp.zeros((HALO, cols.stop - c0), F32) for _ in range(1 + FFN_CONV)]
            for r in range(0, tm, STRIP):
                rows = slice(r, r + STRIP)
                taps = _conv_taps(pad_ref, r, cols, FFN_CONV)
                gelu, dgelu = _gelu_and_grad(cb[:, cols] + cw[0:1, cols] * taps[0] + cw[1:2, cols] * taps[1] + cw[2:3, cols] * taps[2])
                u = u_ref[rows, cols].astype(F32)
                dact = dact_ref[rows, cols]
                act_ref[rows, cols] = _bf(gelu * u)
                du_out[rows, cols] = _bf(dact * gelu)
                dgc = dact * u * dgelu
                dgc_out[rows, cols] = _bf(dgc)
                sums = [sums[0] + fold(dgc)] + [sums[1 + k] + fold(dgc * taps[k]) for k in range(FFN_CONV)]
            for k in range(1 + FFN_CONV):
                acc_ref[k, :, cols] = sums[k]
        _accumulate(dwd_acc, _tn(act_ref[...], dout_b), first)

        @pl.when(i == t // tm - 1)
        def _():
            dwd_out[...] = _bf(dwd_acc[...])

        _accumulate(dcb_out, _colsum(acc_ref[0]), first)
        _accumulate(dcw_out, jnp.concatenate([_colsum(acc_ref[1 + k]) for k in range(FFN_CONV)], axis=0), first)

    blk = pl.BlockSpec((tm, fb), lambda f, i: (i, f))
    halo = pl.BlockSpec((HALO_BF16, fb), lambda f, i: (jnp.maximum(i * (tm // HALO_BF16) - 1, 0), f))
    wd_blk = pl.BlockSpec((fb, d), lambda f, i: (f, 0), pipeline_mode=pl.Buffered(1))
    return pl.pallas_call(
        body, name="ffn_b_bwd", grid=(FF_BLOCKS, t // tm),
        in_specs=[blk, halo, blk, pl.BlockSpec((tm, d), lambda f, i: (i, 0)), pl.BlockSpec((FFN_CONV, fb), lambda f, i: (0, f)),
                  pl.BlockSpec((1, fb), lambda f, i: (0, f)), wd_blk],
        out_specs=[blk, blk, wd_blk, pl.BlockSpec((FFN_CONV, fb), lambda f, i: (0, f)),
                   pl.BlockSpec((1, fb), lambda f, i: (0, f))],
        out_shape=[_sds((t, D_FF), BF16), _sds((t, D_FF), BF16), _sds((D_FF, d), BF16), _sds((FFN_CONV, D_FF)), _sds((1, D_FF))],
        scratch_shapes=[pltpu.VMEM((HALO + tm, fb), F32), pltpu.VMEM((tm, fb), F32), pltpu.VMEM((tm, fb), BF16),
                        pltpu.VMEM((1 + FFN_CONV, HALO, fb), F32), pltpu.VMEM((fb, d), F32)],
        compiler_params=_params(2))(g, g, u, dout, cw, cb, wd)


def _ffn_a_dgrad(h, norm, dgc, du, dres, cw, wg, wu, tm, seq):
    t, d = h.shape
    tiles_per_seq = seq // tm
    n_tiles = t // tm

    def body(h_ref, gn_ref, dgc_ref, halo_ref, du_ref, dres_ref, cw_ref, wg_ref, wu_ref, dh_out, dg_out, dgn_out, pad_ref):
        i = pl.program_id(0)
        last_in_seq = i % tiles_per_seq == tiles_per_seq - 1
        dg = _bf(_causal_conv_transpose(pad_ref, dgc_ref[...].astype(F32), halo_ref[:HALO, :].astype(F32), last_in_seq, cw_ref[...], FFN_CONV))
        dg_out[...] = dg
        _, vjp_norm = jax.vjp(_rms, h_ref[...], gn_ref[...])
        dh, dgn = vjp_norm(_nn(dg, wg_ref[...]) + _nn(du_ref[...], wu_ref[...]))
        dh_out[...] = dh + dres_ref[...]
        _accumulate(dgn_out, dgn, i == 0)

    last = n_tiles * (tm // HALO_BF16) - 1
    halo = pl.BlockSpec((HALO_BF16, D_FF), lambda i: (jnp.minimum((i + 1) * (tm // HALO_BF16), last), 0))
    return pl.pallas_call(
        body, name="ffn_a_dgrad", grid=(n_tiles,),
        in_specs=[_row(tm, d), _const((1, d)), _row(tm, D_FF), halo, _row(tm, D_FF), _row(tm, d),
                  _const((FFN_CONV, D_FF)), _const((D_FF, d)), _const((D_FF, d))],
        out_specs=[_row(tm, d), _row(tm, D_FF), _const((1, d))], out_shape=[_sds((t, d)), _sds((t, D_FF), BF16), _sds((1, d))],
        scratch_shapes=[pltpu.VMEM((tm + HALO, D_FF), F32)], compiler_params=_params())(h, norm, dgc, dgc, du, dres, cw, wg, wu)


def _ffn_a_wgrad(hn, dg, du, tm):
    _, t, d = hn.shape
    fb = D_FF // FF_BLOCKS

    n_tiles = t // tm

    def body(hn_ref, dg_ref, du_ref, dwg_out, dwu_out, acc_g, acc_u):
        i = pl.program_id(1)
        hn_t = hn_ref[0]
        _accumulate(acc_g, _tn(dg_ref[...], hn_t), i == 0)
        _accumulate(acc_u, _tn(du_ref[...], hn_t), i == 0)

        @pl.when(i == n_tiles - 1)
        def _():
            dwg_out[...] = _bf(acc_g[...])
            dwu_out[...] = _bf(acc_u[...])

    blk = pl.BlockSpec((tm, fb), lambda f, i: (i, f))
    wspec = pl.BlockSpec((fb, d), lambda f, i: (f, 0), pipeline_mode=pl.Buffered(1))
    return pl.pallas_call(body, name="ffn_a_wgrad", grid=(FF_BLOCKS, n_tiles),
                          in_specs=[pl.BlockSpec((1, tm, d), lambda f, i: (0, i, 0)), blk, blk],
                          out_specs=[wspec, wspec], out_shape=[_sds((D_FF, d), BF16), _sds((D_FF, d), BF16)],
                          scratch_shapes=[pltpu.VMEM((fb, d), F32), pltpu.VMEM((fb, d), F32)],
                          compiler_params=_params(2))(hn, dg, du)


def _sgu_mix(vn, ws_ref, bst):
    tril = lax.broadcasted_iota(jnp.int32, (CHUNK, CHUNK), 0) >= lax.broadcasted_iota(jnp.int32, (CHUNK, CHUNK), 1)
    wms = [jnp.where(tril, ws_ref[g], 0.0) for g in range(SGU_GROUPS)]
    chunks = []
    for n in range(vn.shape[0] // CHUNK):
        vc = vn[n * CHUNK:(n + 1) * CHUNK, :]
        chunks.append(jnp.concatenate(
            [_nn(wms[g], vc[:, g * CHUNK:(g + 1) * CHUNK]) + bst[:, g:g + 1] for g in range(SGU_GROUPS)], axis=1))
    return jnp.concatenate(chunks, axis=0)


def _sgu_fwd(h, w, tm):
    t, d = h.shape

    def body(h_ref, cn_ref, win_ref, lg_ref, lb_ref, ws_ref, bst_ref, wout_ref, h_out):
        h_t = h_ref[...]
        z = _gelu(_nn(_rms(h_t, cn_ref[...]), win_ref[...]))
        vn = _layer_norm(z[:, d:], lg_ref[...], lb_ref[...])
        s = _sgu_mix(vn, ws_ref, bst_ref[...])
        h_out[...] = h_t + _nn(z[:, :d] * s, wout_ref[...])

    return pl.pallas_call(
        body, name="sgu_fwd", grid=(t // tm,),
        in_specs=[_row(tm, d), _const((1, d)), _const((d, 2 * d)), _const((1, d)), _const((1, d)), _const((SGU_GROUPS, CHUNK, CHUNK)),
                  _const((CHUNK, LANES)), _const((d, d))],
        out_specs=_row(tm, d), out_shape=_sds((t, d)), compiler_params=_params(),
    )(h, w['c_norm'], w['c_w_in'], w['c_ln_g'], w['c_ln_b'], w['c_w_s'], w['bsT'], w['c_w_out'])


def _sgu_bwd(h, dout, w, tm):
    t, d = h.shape

    def body(h_ref, dout_ref, cn_ref, win_ref, lg_ref, lb_ref, ws_ref, bst_ref, wout_ref,
             dh_out, dcn_out, dwin_out, dlg_out, dlb_out, dws_out, dbst_out, dwout_out):
        first = pl.program_id(0) == 0
        hn, vjp_norm = jax.vjp(_rms, h_ref[...], cn_ref[...])
        zpre = _nn(hn, win_ref[...])
        u, vjp_u = jax.vjp(_gelu, zpre[:, :d])
        vn, vjp_v = jax.vjp(lambda zp, lg, lb: _layer_norm(_gelu(zp), lg, lb), zpre[:, d:], lg_ref[...], lb_ref[...])
        s = _sgu_mix(vn, ws_ref, bst_ref[...])
        dout_t = dout_ref[...]
        dus = _nt(dout_t, wout_ref[...])
        _accumulate(dwout_out, _tn(u * s, dout_t), first)
        ds = dus * u
        tril = lax.broadcasted_iota(jnp.int32, (CHUNK, CHUNK), 0) >= lax.broadcasted_iota(jnp.int32, (CHUNK, CHUNK), 1)
        lane = lax.broadcasted_iota(jnp.int32, (CHUNK, LANES), 1)
        dws = [jnp.zeros((CHUNK, CHUNK), F32) for _ in range(SGU_GROUPS)]
        dbst = jnp.zeros((CHUNK, LANES), F32)
        dvn_chunks = []
        for n in range(tm // CHUNK):
            cols = []
            for g in range(SGU_GROUPS):
                ds_ng = ds[n * CHUNK:(n + 1) * CHUNK, g * CHUNK:(g + 1) * CHUNK]
                vc_ng = vn[n * CHUNK:(n + 1) * CHUNK, g * CHUNK:(g + 1) * CHUNK]
                cols.append(_tn(jnp.where(tril, ws_ref[g], 0.0), ds_ng))
                dws[g] = dws[g] + _nt(ds_ng, vc_ng)
                dbst = dbst + jnp.where(lane == g, jnp.sum(ds_ng, axis=1, keepdims=True), 0.0)
            dvn_chunks.append(jnp.concatenate(cols, axis=1))
        dvn = jnp.concatenate(dvn_chunks, axis=0)
        for g in range(SGU_GROUPS):
            val = jnp.where(tril, dws[g], 0.0)

            @pl.when(first)
            def _():
                dws_out[g] = val

            @pl.when(jnp.logical_not(first))
            def _():
                dws_out[g] += val
        _accumulate(dbst_out, dbst, first)
        (dzu,) = vjp_u(dus * s)
        dzv, dlg, dlb = vjp_v(dvn)
        _accumulate(dlg_out, dlg, first)
        _accumulate(dlb_out, dlb, first)
        dzpre = jnp.concatenate([dzu, dzv], axis=1)
        _accumulate(dwin_out, _tn(hn, dzpre), first)
        dh, dcn = vjp_norm(_nt(dzpre, win_ref[...]))
        _accumulate(dcn_out, dcn, first)
        dh_out[...] = dh + dout_t

    return pl.pallas_call(
        body, name="sgu_bwd", grid=(t // tm,),
        in_specs=[_row(tm, d), _row(tm, d), _const((1, d)), _const((d, 2 * d)), _const((1, d)), _const((1, d)),
                  _const((SGU_GROUPS, CHUNK, CHUNK)), _const((CHUNK, LANES)), _const((d, d))],
        out_specs=[_row(tm, d), _const((1, d)), _const((d, 2 * d)), _const((1, d)), _const((1, d)), _const((SGU_GROUPS, CHUNK, CHUNK)),
                   _const((CHUNK, LANES)), _const((d, d))],
        out_shape=[_sds((t, d)), _sds((1, d)), _sds((d, 2 * d)), _sds((1, d)), _sds((1, d)), _sds((SGU_GROUPS, CHUNK, CHUNK)),
                   _sds((CHUNK, LANES)), _sds((d, d))],
        compiler_params=_params(),
    )(h, dout, w['c_norm'], w['c_w_in'], w['c_ln_g'], w['c_ln_b'], w['c_w_s'], w['bsT'], w['c_w_out'])


def _final_loss(h, target, norm, tm):
    t, d = h.shape

    def body(h_ref, tgt_ref, gn_ref, dh_out, loss_out, dgn_out):
        first = pl.program_id(0) == 0
        tgt = tgt_ref[...]

        def loss_fn(h_, g_):
            err = _rms(h_, g_) - tgt
            return 0.5 * jnp.sum(jnp.mean(err * err, axis=-1, keepdims=True), axis=0, keepdims=True)

        loss, vjp_loss = jax.vjp(loss_fn, h_ref[...], gn_ref[...])
        dh, dgn = vjp_loss(jnp.ones((1, 1), F32))
        dh_out[...] = dh
        _accumulate(loss_out, loss, first)
        _accumulate(dgn_out, dgn, first)

    return pl.pallas_call(body, name="final_loss", grid=(t // tm,), in_specs=[_row(tm, d), _row(tm, d), _const((1, d))],
                          out_specs=[_row(tm, d), _const((1, 1)), _const((1, d))],
                          out_shape=[_sds((t, d)), _sds((1, 1)), _sds((1, d))], compiler_params=_params())(h, target, norm)


def _tile(t, seq, want):
    tm = min(want, seq)
    assert seq % tm == 0 and t % tm == 0 and tm % CHUNK == 0
    return tm


def _local_step(x, posb, target, w, seq, late_weights, on_grads):
    t, d = x.shape
    b = t // seq
    hp = HEADS * HEAD_PAD
    tm_big, tm_mid = _tile(t, seq, 512), _tile(t, seq, 256)
    tq = _tile(t, seq, 512)

    q, k, v, xl, gate = _ab_in_fwd(x, posb, w, tm_big)
    o = _attn_fwd(q.reshape(b, seq, hp), k.reshape(b, seq, hp), v.reshape(b, seq, hp), tq).reshape(t, hp)
    y, hs = _lru_fwd(xl, gate, w, tm_big, seq)
    w = {**w, **late_weights('out0', y)}
    h1 = _ab_out_fwd(x, o, y, w, tm_big)
    hcur = h1
    saved = []
    for l in range(2):
        if l == 1:
            w = {**w, **late_weights('mix1', hcur)}
            saved_h2 = hcur
            hcur = _sgu_fwd(hcur, w, tm_mid)
        wl = late_weights('ffn%d' % l, hcur)
        g, u, hn = _ffn_a_fwd(hcur, w['ffn_norm'][l], wl['Wg'], wl['Wu'], tm_big)
        hnext = _ffn_b_fwd(g, u, hcur, w['ffn_conv_w'][l], w['ffn_conv_b'][l], wl['Wd'], tm_mid, seq)
        saved.append((hcur, g, u, wl, hn))
        hcur = hnext
    dh, loss, d_final = _final_loss(hcur, target, w['final_norm'], tm_big)

    ffn = {}
    conv_b = list(w['ffn_conv_b'])
    for l in (1, 0):
        hin, g, u, wl, hn = saved[l]
        dgc, du, d_wd, d_cw, d_cb = _ffn_b_bwd(g, u, dh, w['ffn_conv_w'][l], conv_b[l], wl['Wd'], tm_big, seq)
        dh, dg, d_norm = _ffn_a_dgrad(hin, w['ffn_norm'][l], dgc, du, dh, w['ffn_conv_w'][l], wl['Wg'], wl['Wu'], tm_mid, seq)
        d_wg, d_wu = _ffn_a_wgrad(hn, dg, du, _tile(t, seq, 1024))
        ffn[l] = dict(ffn_norm=d_norm, ffn_conv_w=d_cw, ffn_conv_b=d_cb, Wg=d_wg, Wu=d_wu, Wd=d_wd)
        if l == 1:
            dh, d_cn, d_cwin, d_lg, d_lb, d_ws, d_bst, d_cwout = _sgu_bwd(saved_h2, dh, w, tm_mid)
            zero = on_grads('late1', dict(final_norm=d_final, c_norm=d_cn, c_ln_g=d_lg, c_ln_b=d_lb, c_w_s=d_ws, bsT=d_bst, c_w_in=d_cwin,
                                          c_w_out=d_cwout, Wg=[d_wg], Wu=[d_wu], Wd=[d_wd]))
            conv_b[0] = conv_b[0] + zero
    late0 = {name: [ffn[0][name], ffn[1][name]] for name in ('ffn_norm', 'ffn_conv_w', 'ffn_conv_b')}
    zero = on_grads('late0', dict(late0, Wg=[ffn[0]['Wg']], Wu=[ffn[0]['Wu']], Wd=[ffn[0]['Wd']]))
    w = {**w, 'Wo_b': w['Wo_b'] + zero.astype(w['Wo_b'].dtype)}
    do, dy, d_woa, d_wob = _ab_out_bwd(o, y, dh, w, tm_big)
    dxl, dgate, d_cw, d_cb, d_wa, d_ba, d_wx, d_bx, d_lam = _lru_bwd(xl, gate, hs, dy, w, tm_big, seq)
    zero = on_grads('mid', dict(Wo_a=d_woa, Wo_b=d_wob, ab_conv_w=d_cw, ab_conv_b=d_cb, Wa=d_wa, ab_b_rg_a=d_ba, Wx=d_wx,
                                ab_b_rg_x=d_bx, ab_lambda=d_lam))
    w = {**w, 'ab_norm': w['ab_norm'] + zero}
    dq, dk, dv = _attn_bwd(q.reshape(b, seq, hp), k.reshape(b, seq, hp), v.reshape(b, seq, hp), do.reshape(b, seq, hp), tq)
    dx, d_gn, d_win, d_qn, d_wq, d_kvn, d_wk, d_wv = _ab_in_bwd(
        x, posb, w, dq.reshape(t, hp), dk.reshape(t, hp), dv.reshape(t, hp), dxl, dgate, dh, tm_mid)
    return loss, dx, dict(ab_norm=d_gn, W_in=d_win, ab_q_norm=d_qn, Wq=d_wq, ab_kv_norm=d_kvn, Wk=d_wk, Wv=d_wv)


def _block_diag(wg):
    g, n, _ = wg.shape
    return jnp.einsum('gij,gh->gihj', wg, jnp.eye(g, dtype=wg.dtype)).reshape(g * n, g * n)


def _prepare_out(w_out):
    d = w_out.shape[2]
    mla = HEADS * QK_NOPE
    return {'Wo_a': jnp.pad(w_out[0, :mla].reshape(HEADS, QK_NOPE, d), ((0, 0), (0, HEAD_PAD - QK_NOPE), (0, 0))).reshape(HEADS * HEAD_PAD, d),
            'Wo_b': w_out[0, mla:]}


def _prepare(full):
    d = full['ab_w_in'].shape[1]
    w_in = full['ab_w_in'][0]
    zeros = lambda n: jnp.zeros((d, n), w_in.dtype)
    wq = full['ab_w_q_b'][0].reshape(Q_LORA, HEADS, QK_NOPE + QK_ROPE)
    wkv = full['ab_w_kv_b'][0].reshape(KV_LORA, HEADS, 2 * QK_NOPE)
    pad_head = lambda a: jnp.pad(a, ((0, 0), (0, 0), (0, HEAD_PAD - a.shape[2]))).reshape(a.shape[0], HEADS * HEAD_PAD)
    w = {
        'W_in': jnp.concatenate([w_in[:, :Z_KPE], zeros(QK_NOPE), w_in[:, Z_KPE:Z_KPE + QK_ROPE],
                                 zeros(HEAD_PAD - QK_NOPE - QK_ROPE), w_in[:, Z_KPE + QK_ROPE:]], axis=1),
        'Wq': pad_head(wq), 'Wk': pad_head(wkv[:, :, :QK_NOPE]), 'Wv': pad_head(wkv[:, :, QK_NOPE:]),
        'Wa': _bf(_block_diag(full['ab_w_rg_a'][0])), 'Wx': _bf(_block_diag(full['ab_w_rg_x'][0])),
        'c_w_s': full['c_w_s'][0],
        'bsT': jnp.pad(full['c_b_s'][0].T, ((0, 0), (0, LANES - SGU_GROUPS))),
        'ffn_norm': [full['ffn_norm'][l:l + 1] for l in range(2)], 'ffn_conv_w': [full['ffn_conv_w'][l] for l in range(2)],
        'ffn_conv_b': [full['ffn_conv_b'][l:l + 1] for l in range(2)],
        'ab_conv_w': full['ab_conv_w'][0], 'final_norm': full['final_norm'][None, :],
    }
    for name in ('ab_norm', 'ab_q_norm', 'ab_kv_norm', 'ab_conv_b', 'ab_b_rg_a', 'ab_b_rg_x', 'ab_lambda', 'c_norm', 'c_ln_g', 'c_ln_b'):
        w[name] = full[name]
    return w


def _unprepare(g):
    unpad_head = lambda a, n: a.reshape(a.shape[0], HEADS, HEAD_PAD)[:, :, :n]
    diag = lambda a: jnp.einsum('gigj->gij', a.reshape(HEADS, LRU_W // HEADS, HEADS, LRU_W // HEADS))
    rules = {
        'ab_w_in': (('W_in',), lambda a: jnp.concatenate([a[:, :Z_KPE], a[:, Z_KPE + QK_NOPE:Z_KPE + QK_NOPE + QK_ROPE], a[:, Z_LRU:]], axis=1)[None]),
        'ab_w_q_b': (('Wq',), lambda a: unpad_head(a, QK_NOPE + QK_ROPE).reshape(1, Q_LORA, -1)),
        'ab_w_kv_b': (('Wk', 'Wv'), lambda a, b: jnp.concatenate([unpad_head(a, QK_NOPE), unpad_head(b, QK_NOPE)], axis=2).reshape(1, KV_LORA, -1)),
        'ab_w_out': (('Wo_a', 'Wo_b'), lambda a, b: jnp.concatenate(
            [a.reshape(HEADS, HEAD_PAD, -1)[:, :QK_NOPE].reshape(HEADS * QK_NOPE, -1), b], axis=0)[None]),
        'ab_w_rg_a': (('Wa',), lambda a: diag(a)[None]), 'ab_w_rg_x': (('Wx',), lambda a: diag(a)[None]),
        'c_w_in': (('c_w_in',), lambda a: a[None]), 'c_w_out': (('c_w_out',), lambda a: a[None]), 'c_w_s': (('c_w_s',), lambda a: a[None]),
        'c_b_s': (('bsT',), lambda a: a[:, :SGU_GROUPS].T[None]),
        'ffn_w_gate': (('Wg',), jnp.stack), 'ffn_w_up': (('Wu',), jnp.stack), 'ffn_w_down': (('Wd',), jnp.stack),
        'ffn_norm': (('ffn_norm',), lambda a: jnp.concatenate(a, axis=0)), 'ffn_conv_w': (('ffn_conv_w',), jnp.stack),
        'ffn_conv_b': (('ffn_conv_b',), lambda a: jnp.concatenate(a, axis=0)),
        'ab_conv_w': (('ab_conv_w',), lambda a: a[None]), 'final_norm': (('final_norm',), lambda a: a[0]),
    }
    for name in ('ab_norm', 'ab_q_norm', 'ab_kv_norm', 'ab_conv_b', 'ab_b_rg_a', 'ab_b_rg_x', 'ab_lambda', 'c_norm', 'c_ln_g', 'c_ln_b'):
        rules[name] = ((name,), lambda a: a)
    return {name: fn(*[g[k] for k in keys]) for name, (keys, fn) in rules.items() if all(k in g for k in keys)}


SLAB_ROWS = 16


def _round_up(n, m):
    return -(-n // m) * m


def _to_chunks(full, axis):
    s = full.shape
    return jnp.moveaxis(full.reshape(s[:axis] + (N_DEV, s[axis] // N_DEV) + s[axis + 1:]), axis, 0)


def _from_chunks(chunks, axis):
    local = chunks.shape[1:]
    return jnp.moveaxis(chunks, 0, axis).reshape(local[:axis] + (N_DEV * local[axis],) + local[axis + 1:])


def _merge_columns(landed, name):
    _, _, r, n = landed.shape
    tr = r // 4

    def body(l_ref, o_ref):
        o_ref[0] = jnp.concatenate([l_ref[dev, 0] for dev in range(N_DEV)], axis=1)

    return pl.pallas_call(body, name="merge_" + name, grid=(r // tr,),
                          in_specs=[pl.BlockSpec((N_DEV, 1, tr, n), lambda i: (0, 0, i, 0))],
                          out_specs=pl.BlockSpec((1, tr, N_DEV * n), lambda i: (0, i, 0)),
                          out_shape=jax.ShapeDtypeStruct((1, r, N_DEV * n), landed.dtype), compiler_params=_params())(landed)


def _split_chunks(whole, axis, name):
    _, rows, cols = whole.shape
    if axis == 1:
        r = rows // N_DEV

        def body(x_ref, o_ref):
            o_ref[0] = _bf(x_ref[...])

        grid, out_shape = (N_DEV,), (N_DEV, 1, r, cols)
        spec, out_spec = pl.BlockSpec((1, r, cols), lambda dev: (0, dev, 0)), pl.BlockSpec((1, 1, r, cols), lambda dev: (dev, 0, 0, 0))
    else:
        n, tr = cols // N_DEV, rows // 4

        def body(x_ref, o_ref):
            x = x_ref[0]
            for dev in range(N_DEV):
                o_ref[dev, 0] = _bf(x[:, dev * n:(dev + 1) * n])

        grid, out_shape = (rows // tr,), (N_DEV, 1, rows, n)
        spec, out_spec = pl.BlockSpec((1, tr, cols), lambda i: (0, i, 0)), pl.BlockSpec((N_DEV, 1, tr, n), lambda i: (0, 0, i, 0))
    return pl.pallas_call(body, name="split_" + name, grid=grid, in_specs=[spec], out_specs=out_spec,
                          out_shape=jax.ShapeDtypeStruct(out_shape, BF16), compiler_params=_params())(whole)


def _slab_rows(n):
    return _round_up(-(-n // LANES), SLAB_ROWS)


def _to_slab(a, lead):
    a = a.reshape(lead + (-1,))
    rows = _slab_rows(a.shape[-1])
    a = jnp.pad(a, [(0, 0)] * len(lead) + [(0, rows * LANES - a.shape[-1])])
    return a.reshape(lead + (rows, LANES))


def _pack_slabs(parts, lead):
    return jnp.concatenate([_to_slab(p, lead) for p in parts], axis=len(lead))


def _unpack_slabs(packed, shapes):
    lead = packed.shape[:-2]
    out, row = [], 0
    for shape in shapes:
        size = math.prod(shape)
        rows = _slab_rows(size)
        piece = lax.slice_in_dim(packed, row, row + rows, axis=len(lead))
        out.append(piece.reshape(lead + (rows * LANES,))[..., :size].reshape(lead + tuple(shape)))
        row += rows
    return out


HBM = pl.BlockSpec(memory_space=pl.ANY)


def _other_chips(x, y):
    return [(1 - x, y), (x, 1 - y), (1 - x, 1 - y)]


def _all_gather(blocks):
    n = len(blocks)

    def body(*refs):
        x_refs, out_refs, token = refs[:n], refs[n:2 * n], refs[2 * n]
        send_sems, recv_sems, local_sems = refs[2 * n + 1:]
        token[...] = jnp.zeros_like(token)
        x, y, c = lax.axis_index("x"), lax.axis_index("y"), lax.axis_index("c")
        me, sibling = (x, y, c), (x, y, 1 - c)
        chips = _other_chips(x, y)

        def slab(a, px, py, pc):
            return out_refs[a].at[4 * px + 2 * py + pc]

        def copy(a, k, blk, to, src=None):
            return pltpu.make_async_remote_copy(src_ref=slab(a, *blk) if src is None else src, dst_ref=slab(a, *blk),
                                                send_sem=send_sems.at[7 * a + k], recv_sem=recv_sems.at[7 * a + k],
                                                device_id=to, device_id_type=MESH)

        mine = [pltpu.make_async_copy(x_refs[a], slab(a, *me), local_sems.at[a]) for a in range(n)]
        started = []
        for a in range(n):
            mine[a].start()
            started.append(copy(a, 0, me, sibling, src=x_refs[a]))
            started += [copy(a, 1 + j, me, (*chip, c), src=x_refs[a]) for j, chip in enumerate(chips)]
        for cp in started:
            cp.start()
        for j, chip in enumerate(chips):
            for a in range(n):
                copy(a, 1 + j, (*chip, c), me).wait_recv()
                passed = copy(a, 4 + j, (*chip, c), sibling)
                passed.start()
                started.append(passed)
        for a in range(n):
            copy(a, 0, sibling, me).wait_recv()
        for j, chip in enumerate(chips):
            for a in range(n):
                copy(a, 4 + j, (*chip, 1 - c), me).wait_recv()
        for cp in started:
            cp.wait_send()
        for a in range(n):
            mine[a].wait()

    out = pl.pallas_call(
        body, name="all_gather_weights",
        out_shape=[jax.ShapeDtypeStruct((N_DEV,) + b.shape, b.dtype) for b in blocks] + [jax.ShapeDtypeStruct((8, LANES), F32)],
        in_specs=[HBM] * n, out_specs=[HBM] * n + [pl.BlockSpec(memory_space=pltpu.VMEM)],
        scratch_shapes=[pltpu.SemaphoreType.DMA((7 * n,)), pltpu.SemaphoreType.DMA((7 * n,)), pltpu.SemaphoreType.DMA((n,))],
    )(*blocks)
    return list(out[:n]), out[n][0, 0]


FLIPS = [(0, 0, 1), (1, 0, 0), (1, 0, 1), (0, 1, 0), (0, 1, 1), (1, 1, 0), (1, 1, 1)]


def _peers(x, y, c):
    flip = lambda v, f: 1 - v if f else v
    return [(flip(x, fx), flip(y, fy), flip(c, fc)) for fx, fy, fc in FLIPS]


def _direct_copies(src_refs, land_refs, send_sems, recv_sems, scatter):
    x, y, c = lax.axis_index("x"), lax.axis_index("y"), lax.axis_index("c")
    me = 4 * x + 2 * y + c
    starts, waits = [], []
    for a in range(len(src_refs)):
        for k, (px, py, pc) in enumerate(_peers(x, y, c)):
            peer = 4 * px + 2 * py + pc
            sems = dict(send_sem=send_sems.at[7 * a + k], recv_sem=recv_sems.at[7 * a + k], device_id=(px, py, pc), device_id_type=MESH)
            src = src_refs[a].at[peer] if scatter else src_refs[a]
            starts.append(pltpu.make_async_remote_copy(src_ref=src, dst_ref=land_refs[a].at[me], **sems))
            waits.append(pltpu.make_async_remote_copy(src_ref=src, dst_ref=land_refs[a].at[peer], **sems))
    n = len(src_refs)
    keeps = [] if scatter else [pltpu.make_async_copy(src_refs[a], land_refs[a].at[me], send_sems.at[7 * n + a]) for a in range(n)]
    return starts, waits, keeps


def _landing(src, scatter):
    block = src.shape[1:] if scatter else src.shape
    return jax.ShapeDtypeStruct((N_DEV,) + block, src.dtype)


HBM_SPACE = pl.BlockSpec(memory_space=pltpu.HBM)
SEMAPHORES = pl.BlockSpec(memory_space=pltpu.SEMAPHORE)
SPLIT_EFFECT = pltpu.SideEffectType.DATAFLOW_SIDE_EFFECTING


def _start_exchange(name, srcs, scatter):
    n = len(srcs)
    lands = [lax.empty(s.shape, s.dtype) for s in (_landing(s, scatter) for s in srcs)]

    def body(*refs):
        starts, _, keeps = _direct_copies(refs[:n], refs[n:2 * n], refs[2 * n], refs[2 * n + 1], scatter)
        for cp in starts + keeps:
            cp.start()
        refs[-1][...] = jnp.zeros_like(refs[-1])

    held = [pltpu.with_memory_space_constraint(a, pltpu.HBM) for a in list(srcs) + lands]
    out = pl.pallas_call(
        body, name=name + "_start",
        out_shape=(pltpu.SemaphoreType.DMA(((7 if scatter else 8) * n,)), pltpu.SemaphoreType.DMA((7 * n,)),
                   *[pltpu.HBM(a.shape, a.dtype) for a in held],
                   jax.ShapeDtypeStruct((8, LANES), F32)),
        in_specs=[HBM_SPACE] * (2 * n), out_specs=(SEMAPHORES, SEMAPHORES, *[HBM_SPACE] * (2 * n), pl.BlockSpec(memory_space=pltpu.VMEM)),
        input_output_aliases={i: 2 + i for i in range(2 * n)},
        compiler_params=pltpu.CompilerParams(has_side_effects=SPLIT_EFFECT),
    )(*held)
    return out[0], out[1], list(out[2:2 + n]), list(out[2 + n:2 + 2 * n]), out[-1][0, 0], out[-1]


def _wait_exchange(name, started, after, scatter):
    send_sems, recv_sems, srcs, lands = started[:4]
    n = len(srcs)

    def body(*refs):
        _, waits, keeps = _direct_copies(refs[:n], refs[n:2 * n], refs[2 * n], refs[2 * n + 1], scatter)
        for cp in waits:
            cp.wait_send()
        for cp in waits:
            cp.wait_recv()
        for cp in keeps:
            cp.wait()

    out = pl.pallas_call(
        body, name=name + "_wait", out_shape=tuple(pltpu.HBM(a.shape, a.dtype) for a in srcs + lands),
        in_specs=[HBM_SPACE] * (2 * n) + [SEMAPHORES, SEMAPHORES, HBM], out_specs=tuple([HBM_SPACE] * (2 * n)),
        input_output_aliases={i: i for i in range(2 * n)},
        compiler_params=pltpu.CompilerParams(has_side_effects=SPLIT_EFFECT),
    )(*srcs, *lands, send_sems, recv_sems, after)
    return list(out[:n]), list(out[n:])


def _row_tile(rows):
    return rows // 2 if (rows // 2) % SLAB_ROWS == 0 else rows


def _sum_and_adamw(me, landed, own, wts, m, v, name, layer=None, into=None):
    layers, r, n = wts.shape
    first = 0 if layer is None else layer
    count = layers if layer is None else 1
    tr = _row_tile(r)
    blk = pl.BlockSpec((1, tr, n), lambda li, ri, me_ref: (first + li, ri, 0))
    c1 = 1.0 / (1.0 - ADAM_B1 ** ADAM_STEP)
    c2 = 1.0 / (1.0 - ADAM_B2 ** ADAM_STEP)
    held = [] if into is None else list(into)

    def body(me_ref, l_ref, own_ref, w_ref, m_ref, v_ref, *rest):
        g_out, d_out, m_out, v_out = rest[len(held):]
        mine = own_ref[0].astype(F32)
        g = jnp.where(me_ref[0] == 0, mine, l_ref[0].astype(F32))
        for dev in range(1, N_DEV):
            g = g + jnp.where(me_ref[0] == dev, mine, l_ref[dev].astype(F32))
        m_new = ADAM_B1 * m_ref[...] + (1.0 - ADAM_B1) * g
        v_new = ADAM_B2 * v_ref[...] + (1.0 - ADAM_B2) * (g * g)
        g_out[...] = g
        m_out[...] = m_new
        v_out[...] = v_new
        d_out[...] = -ADAM_LR * ((m_new * c1) / (jnp.sqrt(v_new * c2) + ADAM_EPS) + ADAM_WD * w_ref[...])

    return pl.pallas_call(
        body, name="adamw_" + name,
        grid_spec=pltpu.PrefetchScalarGridSpec(
            num_scalar_prefetch=1, grid=(count, r // tr),
            in_specs=[pl.BlockSpec((N_DEV, 1, tr, n), lambda li, ri, me_ref: (0, li, ri, 0)),
                      pl.BlockSpec((1, 1, tr, n), lambda li, ri, me_ref: (me_ref[0], li, ri, 0)), blk, blk, blk] + [HBM] * len(held),
            out_specs=[blk] * 4),
        out_shape=[_sds((layers, r, n))] * 4, input_output_aliases={6 + i: i for i in range(len(held))},
        compiler_params=_params(2))(me, landed, own, wts, m, v, *held)


EARLY = ['ab_w_in']
LATE_STAGES = {
    'out0': [('ab_w_out', None, 'ab_w_out')],
    'ffn0': [('ffn_w_gate', 0, 'Wg'), ('ffn_w_up', 0, 'Wu'), ('ffn_w_down', 0, 'Wd')],
    'mix1': [('c_w_in', None, 'c_w_in'), ('c_w_out', None, 'c_w_out')],
    'ffn1': [('ffn_w_gate', 1, 'Wg'), ('ffn_w_up', 1, 'Wu'), ('ffn_w_down', 1, 'Wd')],
}
TRANSPOSED = ('ffn_w_gate', 'ffn_w_up')


def _stored(name, a):
    return jnp.swapaxes(a, 1, 2) if name in TRANSPOSED else a


def _stored_axis(name):
    return 1 if name in TRANSPOSED else SHARD_AXIS[name]


GRAD_STAGES = {
    'late1': ([('c_w_in', None), ('c_w_out', None), ('ffn_w_gate', 1), ('ffn_w_up', 1), ('ffn_w_down', 1)],
              ['c_norm', 'c_ln_g', 'c_ln_b', 'c_w_s', 'c_b_s', 'final_norm']),
    'late0': ([('ffn_w_gate', 0), ('ffn_w_up', 0), ('ffn_w_down', 0)], ['ffn_norm', 'ffn_conv_w', 'ffn_conv_b']),
    'mid': ([('ab_w_out', None)], ['ab_conv_w', 'ab_conv_b', 'ab_w_rg_a', 'ab_b_rg_a', 'ab_w_rg_x', 'ab_b_rg_x', 'ab_lambda']),
    'last': ([('ab_w_in', None)], ['ab_norm', 'ab_q_norm', 'ab_w_q_b', 'ab_kv_norm', 'ab_w_kv_b']),
}


def _gather_early(local):
    small = [_bf(local[n]) if n in MATRICES else lax.bitcast_convert_type(local[n], BF16) for n in SMALL_SHARDED]
    gathered, zero = _all_gather([_bf(local[n]) for n in EARLY] + [_pack_slabs(small, ())])
    full = {n: local[n] for n in REPLICATED}
    for n, g in zip(EARLY, gathered):
        full[n] = _from_chunks(g, SHARD_AXIS[n])
    for n, p in zip(SMALL_SHARDED, _unpack_slabs(gathered[-1], [s.shape for s in small])):
        full[n] = _from_chunks(p if n in MATRICES else lax.bitcast_convert_type(p, F32), SHARD_AXIS[n])
    return full, zero


def kernel(x, positions, ab_norm, ab_w_in, ab_q_norm, ab_w_q_b, ab_kv_norm, ab_w_kv_b, ab_conv_w, ab_conv_b, ab_w_rg_a, ab_b_rg_a, ab_w_rg_x, ab_b_rg_x, ab_lambda, ab_w_out, c_norm, c_w_in, c_ln_g, c_ln_b, c_w_s, c_b_s, c_w_out, ffn_norm, ffn_w_gate, ffn_w_up, ffn_conv_w, ffn_conv_b, ffn_w_down, final_norm, loss_target, m_ab_norm, m_ab_w_in, m_ab_q_norm, m_ab_w_q_b, m_ab_kv_norm, m_ab_w_kv_b, m_ab_conv_w, m_ab_conv_b, m_ab_w_rg_a, m_ab_b_rg_a, m_ab_w_rg_x, m_ab_b_rg_x, m_ab_lambda, m_ab_w_out, m_c_norm, m_c_w_in, m_c_ln_g, m_c_ln_b, m_c_w_s, m_c_b_s, m_c_w_out, m_ffn_norm, m_ffn_w_gate, m_ffn_w_up, m_ffn_conv_w, m_ffn_conv_b, m_ffn_w_down, m_final_norm, v_ab_norm, v_ab_w_in, v_ab_q_norm, v_ab_w_q_b, v_ab_kv_norm, v_ab_w_kv_b, v_ab_conv_w, v_ab_conv_b, v_ab_w_rg_a, v_ab_b_rg_a, v_ab_w_rg_x, v_ab_b_rg_x, v_ab_lambda, v_ab_w_out, v_c_norm, v_c_w_in, v_c_ln_g, v_c_ln_b, v_c_w_s, v_c_b_s, v_c_w_out, v_ffn_norm, v_ffn_w_gate, v_ffn_w_up, v_ffn_conv_w, v_ffn_conv_b, v_ffn_w_down, v_final_norm):
    given = dict(locals())
    local = {n: given[n] for n in WEIGHTS}
    b, seq, d = x.shape
    t = b * seq

    me = (4 * lax.axis_index("x") + 2 * lax.axis_index("y") + lax.axis_index("c")).astype(jnp.int32)
    me1 = me.reshape(1)

    full, zero = _gather_early(local)
    gathers = {}
    for stage, members in LATE_STAGES.items():
        srcs = [_bf(_stored(n, local[n] if layer is None else local[n][layer:layer + 1]) + zero) for n, layer, _ in members]
        gathers[stage] = _start_exchange('gather_' + stage, srcs, scatter=False)
        zero = gathers[stage][4]
    w = _prepare(full)
    w['ab_norm'] = w['ab_norm'] + zero

    def late_weights(stage, after):
        _, lands = _wait_exchange('gather_' + stage, gathers[stage], after, scatter=False)
        whole = [l.reshape(1, -1, l.shape[-1]) if _stored_axis(n) == 1 else _merge_columns(l, n)
                 for (n, _, _), l in zip(LATE_STAGES[stage], lands)]
        if stage == 'out0':
            return _prepare_out(whole[0])
        return {key: a[0] for (_, _, key), a in zip(LATE_STAGES[stage], whole)}

    scatters = {}

    def start_scatter(stage, g):
        whole = _unprepare(g)
        big, small = GRAD_STAGES[stage]
        slab = [_to_chunks(whole[n], SHARD_AXIS[n]) if n in SHARD_AXIS else jnp.broadcast_to(whole[n][None], (N_DEV,) + whole[n].shape)
                for n in small]
        own = [whole[n].reshape(N_DEV, 1, whole[n].shape[1] // N_DEV, whole[n].shape[2])
               if whole[n].dtype == BF16 and _stored_axis(n) == 1 else
               _split_chunks(whole[n], _stored_axis(n), n + ('' if layer is None else str(layer))) for n, layer in big]
        own.append(_bf(_pack_slabs(slab, (N_DEV,)))[:, None])
        scatters[stage] = _start_exchange('scatter_' + stage, own, scatter=True)
        return scatters[stage][4]

    posb = jnp.broadcast_to(positions.astype(F32).reshape(t, 1), (t, LANES))
    loss, dx, grads = _local_step(x.reshape(t, d), posb, loss_target.reshape(t, d), w, seq, late_weights, start_scatter)
    start_scatter('last', grads)
    after = scatters['last'][5]

    me1 = me.reshape(1)
    updated = {}
    for stage, (big, small) in GRAD_STAGES.items():
        owns, landed = _wait_exchange('scatter_' + stage, scatters[stage], after, scatter=True)
        for (n, layer), own, land in zip(big, owns, landed):
            updated[n] = _sum_and_adamw(me1, land, own, _stored(n, given[n]), _stored(n, given['m_' + n]), _stored(n, given['v_' + n]),
                                        n + ('' if layer is None else str(layer)), layer, updated.get(n))
        pack_small = lambda prefix: _pack_slabs([given[prefix + n] for n in small], ())[None]
        packed = _sum_and_adamw(me1, landed[-1], owns[-1], pack_small(''), pack_small('m_'), pack_small('v_'), 'small_' + stage)
        unpacked = [_unpack_slabs(p[0], [local[n].shape for n in small]) for p in packed]
        for i, n in enumerate(small):
            updated[n] = [u[i] for u in unpacked]
        after = sum([updated[n][1][:1, :1, :1] for n, _ in big], packed[1][:1, :1, :1])
    total = lax.psum(loss[0, 0], ("x", "y", "c"))
    return (total, dx.reshape(b, seq, d), *[_stored(n, updated[n][kind]) for kind in range(4) for n in WEIGHTS])
```

```python
import math

import jax
import jax.numpy as jnp
from jax import lax
from jax.experimental import pallas as pl
from jax.experimental.pallas import tpu as pltpu

F32 = jnp.float32
BF16 = jnp.bfloat16
MESH = pl.DeviceIdType.MESH

N_DEV = 8
LANES = 128
HALO = 8
VMEM_LIMIT = 56 << 20

NORM_EPS = 1e-6
HEADS = 8
HEAD_PAD = 128
QK_NOPE = 64
QK_ROPE = 32
ROPE_HALF = 16
ROPE_BASE = 10000.0
ATTN_SCALE = (QK_NOPE + QK_ROPE) ** -0.5
LRU_C = 8.0
LRU_W = 512
CHUNK = 128
SGU_GROUPS = 8
D_FF = 2816
FF_BLOCKS = 2

ADAM_LR, ADAM_B1, ADAM_B2, ADAM_EPS, ADAM_WD, ADAM_STEP = 0.001, 0.9, 0.999, 1e-08, 0.01, 10

WEIGHTS = ['ab_norm', 'ab_w_in', 'ab_q_norm', 'ab_w_q_b', 'ab_kv_norm', 'ab_w_kv_b', 'ab_conv_w', 'ab_conv_b',
           'ab_w_rg_a', 'ab_b_rg_a', 'ab_w_rg_x', 'ab_b_rg_x', 'ab_lambda', 'ab_w_out', 'c_norm', 'c_w_in', 'c_ln_g',
           'c_ln_b', 'c_w_s', 'c_b_s', 'c_w_out', 'ffn_norm', 'ffn_w_gate', 'ffn_w_up', 'ffn_conv_w', 'ffn_conv_b',
           'ffn_w_down', 'final_norm']
SHARD_AXIS = {'ab_w_in': 2, 'ab_w_q_b': 2, 'ab_w_kv_b': 2, 'ab_conv_w': 2, 'ab_w_out': 1, 'c_norm': 1, 'c_w_in': 2,
              'c_ln_g': 1, 'c_ln_b': 1, 'c_w_out': 1, 'ffn_w_gate': 2, 'ffn_w_up': 2, 'ffn_conv_w': 2, 'ffn_w_down': 1}
MATRICES = ['ab_w_in', 'ab_w_q_b', 'ab_w_kv_b', 'ab_w_out', 'c_w_in', 'c_w_out', 'ffn_w_gate', 'ffn_w_up', 'ffn_w_down']
BIG = ['ab_w_in', 'c_w_in', 'ffn_w_gate', 'ffn_w_up', 'ab_w_out', 'c_w_out', 'ffn_w_down']
REPLICATED = [n for n in WEIGHTS if n not in SHARD_AXIS]
SMALL_SHARDED = [n for n in WEIGHTS if n in SHARD_AXIS and n not in BIG]


def _bf(x):
    return x.astype(BF16)


def _nn(a, b):
    return lax.dot_general(_bf(a), _bf(b), (((1,), (0,)), ((), ())), preferred_element_type=F32)


def _nt(a, b):
    return lax.dot_general(_bf(a), _bf(b), (((1,), (1,)), ((), ())), preferred_element_type=F32)


def _tn(a, b):
    return lax.dot_general(_bf(a), _bf(b), (((0,), (0,)), ((), ())), preferred_element_type=F32)


def _rms(x, g):
    return x * lax.rsqrt(jnp.mean(x * x, axis=-1, keepdims=True) + NORM_EPS) * g


def _layer_norm(x, g, b):
    xc = x - jnp.mean(x, axis=-1, keepdims=True)
    return xc * lax.rsqrt(jnp.mean(xc * xc, axis=-1, keepdims=True) + NORM_EPS) * g + b


def _gelu(x):
    return jax.nn.gelu(x)


STRIP = 16
STRIP_LANES = 384
GELU_C = math.sqrt(2.0 / math.pi)
GELU_A = 0.044715


def _gelu_and_grad(x):
    x2 = x * x
    t = jnp.tanh(x * (GELU_C + (GELU_C * GELU_A) * x2))
    half_x = 0.5 * x
    one_plus_t = 1.0 + t
    return half_x * one_plus_t, 0.5 * one_plus_t + half_x * (1.0 - t * t) * (GELU_C + (3.0 * GELU_C * GELU_A) * x2)


def _colsum(x):
    return jnp.sum(x, axis=0, keepdims=True)


def _softplus(x):
    return jnp.maximum(x, 0.0) + jnp.log1p(jnp.exp(-jnp.abs(x)))


@jax.custom_vjp
def _decay(x):
    a = jnp.exp(x)
    y = 2.0 * x
    series = -y * (1.0 + y * (1 / 2 + y * (1 / 6 + y * (1 / 24 + y * (1 / 120 + y * (1 / 720))))))
    return a, jnp.where(y < -0.3, 1.0 - a * a, series)


def _decay_fwd(x):
    a, gap = _decay(x)
    return (a, gap), a


def _decay_bwd(a, cts):
    return (a * (cts[0] - 2.0 * a * cts[1]),)


_decay.defvjp(_decay_fwd, _decay_bwd)


def _accumulate(ref, val, first):
    @pl.when(first)
    def _():
        ref[...] = val

    @pl.when(jnp.logical_not(first))
    def _():
        ref[...] += val


def _params(n_axes=1):
    return pltpu.CompilerParams(dimension_semantics=("arbitrary",) * n_axes, vmem_limit_bytes=VMEM_LIMIT)


def _row(tm, n):
    return pl.BlockSpec((tm, n), lambda i: (i, 0))


def _const(shape):
    nd = len(shape)
    return pl.BlockSpec(shape, lambda i: (0,) * nd, pipeline_mode=pl.Buffered(1))


def _prev_halo(tm, n):
    return pl.BlockSpec((HALO, n), lambda i: (jnp.maximum(i * (tm // HALO) - 1, 0), 0))


def _next_halo(tm, n, n_tiles):
    last = n_tiles * (tm // HALO) - 1
    return pl.BlockSpec((HALO, n), lambda i: (jnp.minimum((i + 1) * (tm // HALO), last), 0))


def _sds(shape, dtype=F32):
    return jax.ShapeDtypeStruct(shape, dtype)


def _rope_tables(posb):
    lane = lax.broadcasted_iota(jnp.int32, posb.shape, 1)
    in_rope = jnp.logical_and(lane >= QK_NOPE, lane < QK_NOPE + QK_ROPE)
    j = (lane & (ROPE_HALF - 1)).astype(F32)
    inv_freq = jnp.exp((-math.log(ROPE_BASE)) * j / ROPE_HALF)
    ang = posb * inv_freq
    return jnp.where(in_rope, jnp.cos(ang), 1.0), jnp.where(in_rope, jnp.sin(ang), 0.0)


def _rot(q):
    n = q.shape[1]
    lane = lax.broadcasted_iota(jnp.int32, q.shape, 1) & (HEAD_PAD - 1)
    first_half = jnp.where(lane >= QK_NOPE, -pltpu.roll(q, n - ROPE_HALF, 1), 0.0)
    second_half = jnp.where(lane < QK_NOPE + QK_ROPE, pltpu.roll(q, ROPE_HALF, 1), 0.0)
    return jnp.where(lane < QK_NOPE + ROPE_HALF, first_half, second_half)


def _rope(q, cos_t, sin_t):
    return q * cos_t + _rot(q) * sin_t


def _rope_transpose(dq, cos_t, sin_t):
    return dq * cos_t - _rot(dq * sin_t)


def _tile_heads(t):
    return jnp.concatenate([t] * HEADS, axis=1)


Q_LORA, KV_LORA = 256, 128
Z_KPE = Q_LORA + KV_LORA
Z_LRU = Z_KPE + HEAD_PAD
Z_GATE = Z_LRU + LRU_W
Z_WIDTH = Z_GATE + LRU_W


def _ab_in_fwd(x, posb, w, tm):
    t, d = x.shape

    def body(x_ref, pos_ref, gn_ref, win_ref, qn_ref, wq_ref, kvn_ref, wk_ref, wv_ref, q_out, k_out, v_out, xl_out, gate_out):
        hn = _rms(x_ref[...], gn_ref[...])
        z = _nn(hn, win_ref[...])
        cqn = _rms(z[:, :Q_LORA], qn_ref[...])
        kvn = _rms(z[:, Q_LORA:Z_KPE], kvn_ref[...])
        cos_t, sin_t = _rope_tables(pos_ref[...])
        q_out[...] = _rope(_nn(cqn, wq_ref[...]), _tile_heads(cos_t), _tile_heads(sin_t))
        kpe = _rope(z[:, Z_KPE:Z_LRU], cos_t, sin_t)
        k_out[...] = _nn(kvn, wk_ref[...]) + _tile_heads(kpe)
        v_out[...] = _nn(kvn, wv_ref[...])
        xl_out[...] = z[:, Z_LRU:Z_GATE]
        gate_out[...] = z[:, Z_GATE:]

    hp = HEADS * HEAD_PAD
    return pl.pallas_call(
        body, name="ab_in_fwd", grid=(t // tm,),
        in_specs=[_row(tm, d), _row(tm, LANES), _const((1, d)), _const((d, Z_WIDTH)), _const((1, Q_LORA)), _const((Q_LORA, hp)),
                  _const((1, KV_LORA)), _const((KV_LORA, hp)), _const((KV_LORA, hp))],
        out_specs=[_row(tm, hp), _row(tm, hp), _row(tm, hp), _row(tm, LRU_W), _row(tm, LRU_W)],
        out_shape=[_sds((t, hp)), _sds((t, hp)), _sds((t, hp)), _sds((t, LRU_W)), _sds((t, LRU_W))],
        compiler_params=_params(),
    )(x, posb, w['ab_norm'], w['W_in'], w['ab_q_norm'], w['Wq'], w['ab_kv_norm'], w['Wk'], w['Wv'])


def _ab_in_bwd(x, posb, w, dq, dk, dv, dxl, dgate, dres, tm):
    t, d = x.shape
    hp = HEADS * HEAD_PAD

    def body(x_ref, pos_ref, gn_ref, win_ref, qn_ref, wq_ref, kvn_ref, wk_ref, wv_ref, dq_ref, dk_ref, dv_ref, dxl_ref, dgate_ref,
             dres_ref, dx_out, dgn_out, dwin_out, dqn_out, dwq_out, dkvn_out, dwk_out, dwv_out):
        first = pl.program_id(0) == 0
        hn, vjp_in = jax.vjp(_rms, x_ref[...], gn_ref[...])
        z = _nn(hn, win_ref[...])
        cqn, vjp_q = jax.vjp(_rms, z[:, :Q_LORA], qn_ref[...])
        kvn, vjp_kv = jax.vjp(_rms, z[:, Q_LORA:Z_KPE], kvn_ref[...])
        cos_t, sin_t = _rope_tables(pos_ref[...])
        dq0 = _rope_transpose(dq_ref[...], _tile_heads(cos_t), _tile_heads(sin_t))
        dk0 = dk_ref[...]
        dv0 = dv_ref[...]
        dkpe = dk0[:, :HEAD_PAD]
        for h in range(1, HEADS):
            dkpe = dkpe + dk0[:, h * HEAD_PAD:(h + 1) * HEAD_PAD]
        dkpe = _rope_transpose(dkpe, cos_t, sin_t)
        _accumulate(dwq_out, _tn(cqn, dq0), first)
        _accumulate(dwk_out, _tn(kvn, dk0), first)
        _accumulate(dwv_out, _tn(kvn, dv0), first)
        dcq, dqn = vjp_q(_nt(dq0, wq_ref[...]))
        dckv, dkvn = vjp_kv(_nt(dk0, wk_ref[...]) + _nt(dv0, wv_ref[...]))
        _accumulate(dqn_out, dqn, first)
        _accumulate(dkvn_out, dkvn, first)
        dz = jnp.concatenate([dcq, dckv, dkpe, dxl_ref[...], dgate_ref[...]], axis=1)
        _accumulate(dwin_out, _tn(hn, dz), first)
        dx, dgn = vjp_in(_nt(dz, win_ref[...]))
        _accumulate(dgn_out, dgn, first)
        dx_out[...] = dx + dres_ref[...]

    return pl.pallas_call(
        body, name="ab_in_bwd", grid=(t // tm,),
        in_specs=[_row(tm, d), _row(tm, LANES), _const((1, d)), _const((d, Z_WIDTH)), _const((1, Q_LORA)), _const((Q_LORA, hp)),
                  _const((1, KV_LORA)), _const((KV_LORA, hp)), _const((KV_LORA, hp)),
                  _row(tm, hp), _row(tm, hp), _row(tm, hp), _row(tm, LRU_W), _row(tm, LRU_W), _row(tm, d)],
        out_specs=[_row(tm, d), _const((1, d)), _const((d, Z_WIDTH)), _const((1, Q_LORA)), _const((Q_LORA, hp)),
                   _const((1, KV_LORA)), _const((KV_LORA, hp)), _const((KV_LORA, hp))],
        out_shape=[_sds((t, d)), _sds((1, d)), _sds((d, Z_WIDTH)), _sds((1, Q_LORA)), _sds((Q_LORA, hp)),
                   _sds((1, KV_LORA)), _sds((KV_LORA, hp)), _sds((KV_LORA, hp))],
        compiler_params=_params(),
    )(x, posb, w['ab_norm'], w['W_in'], w['ab_q_norm'], w['Wq'], w['ab_kv_norm'], w['Wk'], w['Wv'], dq, dk, dv, dxl, dgate, dres)


def _attn_probs(q_blk, k_ext, tq):
    ext = k_ext.shape[0]
    s = lax.dot_general(q_blk, k_ext, (((1,), (1,)), ((), ())), preferred_element_type=F32) * ATTN_SCALE
    causal = lax.broadcasted_iota(jnp.int32, (tq, tq), 1) <= lax.broadcasted_iota(jnp.int32, (tq, tq), 0)
    diag = jnp.where(causal, s[:, ext - tq:], -1e30)
    s = diag if ext == tq else jnp.concatenate([s[:, :ext - tq], diag], axis=1)
    p = jnp.exp(s - jnp.max(s, axis=1, keepdims=True))
    return p / jnp.sum(p, axis=1, keepdims=True)


def _attn_fwd(q, k, v, tq):
    b, s, hp = q.shape
    blk = pl.BlockSpec((1, s, HEAD_PAD), lambda bi, h: (bi, 0, h))

    def body(q_ref, k_ref, v_ref, o_ref):
        kb = _bf(k_ref[0])
        vb = _bf(v_ref[0])
        for i in range(s // tq):
            ext = (i + 1) * tq
            p = _attn_probs(_bf(q_ref[0, i * tq:ext, :]), kb[:ext], tq)
            o_ref[0, i * tq:ext, :] = lax.dot_general(_bf(p), vb[:ext], (((1,), (0,)), ((), ())), preferred_element_type=F32)

    return pl.pallas_call(body, name="attn_fwd", grid=(b, HEADS), in_specs=[blk, blk, blk], out_specs=blk,
                          out_shape=_sds((b, s, hp)), compiler_params=_params(2))(q, k, v)


def _attn_bwd(q, k, v, do, tq):
    b, s, hp = q.shape
    blk = pl.BlockSpec((1, s, HEAD_PAD), lambda bi, h: (bi, 0, h))

    def body(q_ref, k_ref, v_ref, do_ref, dq_ref, dk_ref, dv_ref):
        kb = _bf(k_ref[0])
        vb = _bf(v_ref[0])
        dk_ref[...] = jnp.zeros_like(dk_ref)
        dv_ref[...] = jnp.zeros_like(dv_ref)
        for i in range(s // tq):
            ext = (i + 1) * tq
            qb = _bf(q_ref[0, i * tq:ext, :])
            dob = _bf(do_ref[0, i * tq:ext, :])
            p = _attn_probs(qb, kb[:ext], tq)
            dv_ref[0, :ext, :] += lax.dot_general(_bf(p), dob, (((0,), (0,)), ((), ())), preferred_element_type=F32)
            dp = lax.dot_general(dob, vb[:ext], (((1,), (1,)), ((), ())), preferred_element_type=F32)
            ds = _bf(p * (dp - jnp.sum(p * dp, axis=1, keepdims=True)) * ATTN_SCALE)
            dq_ref[0, i * tq:ext, :] = lax.dot_general(ds, kb[:ext], (((1,), (0,)), ((), ())), preferred_element_type=F32)
            dk_ref[0, :ext, :] += lax.dot_general(ds, qb, (((0,), (0,)), ((), ())), preferred_element_type=F32)

    return pl.pallas_call(body, name="attn_bwd", grid=(b, HEADS), in_specs=[blk, blk, blk, blk], out_specs=[blk, blk, blk],
                          out_shape=[_sds((b, s, hp))] * 3, compiler_params=_params(2))(q, k, v, do)


LRU_CONV = 4


def _lru_point(pre_a, pre_x, xc, lam):
    r = jax.nn.sigmoid(pre_a)
    i = jax.nn.sigmoid(pre_x)
    a, gap = _decay(-LRU_C * r * _softplus(-lam))
    return a, jnp.sqrt(gap) * (i * xc)


def _causal_conv(pad_ref, x, halo, first_in_seq, w, taps):
    tm = x.shape[0]
    pad_ref[:HALO, :] = jnp.where(first_in_seq, 0.0, halo)
    pad_ref[HALO:, :] = x
    y = w[taps - 1:taps, :] * x
    for k in range(taps - 1):
        off = HALO - (taps - 1) + k
        y = y + w[k:k + 1, :] * pad_ref[off:off + tm, :]
    return y


def _conv_taps(pad_ref, r, cols, taps):
    blocks = [pad_ref[r + j * HALO:r + (j + 1) * HALO, cols] for j in range(1 + STRIP // HALO)]
    sub = lax.broadcasted_iota(jnp.int32, blocks[0].shape, 0)
    out = []
    for k in range(taps - 1):
        s = taps - 1 - k
        rolled = [pltpu.roll(b, s, 0) for b in blocks]
        out.append(jnp.concatenate([jnp.where(sub < s, rolled[j], rolled[j + 1]) for j in range(STRIP // HALO)], axis=0))
    out.append(jnp.concatenate(blocks[1:], axis=0))
    return out


def _causal_conv_wgrad(pad_ref, dy, taps):
    tm = dy.shape[0]
    return jnp.concatenate([_colsum(dy * pad_ref[HALO - (taps - 1) + k:HALO - (taps - 1) + k + tm, :]) for k in range(taps)], axis=0)


def _causal_conv_transpose(pad_ref, dy, halo_next, last_in_seq, w, taps):
    tm = dy.shape[0]
    pad_ref[:tm, :] = dy
    pad_ref[tm:, :] = jnp.where(last_in_seq, 0.0, halo_next)
    dx = w[taps - 1:taps, :] * dy
    for k in range(taps - 1):
        off = (taps - 1) - k
        dx = dx + w[k:k + 1, :] * pad_ref[off:off + tm, :]
    return dx


def _lru_fwd(xl, gate, w, ts, seq):
    t, n = xl.shape
    tiles_per_seq = seq // ts

    def body(xl_ref, halo_ref, gate_ref, cw_ref, cb_ref, wa_ref, ba_ref, wx_ref, bx_ref, lam_ref, y_out, h_out, pad_ref, a_ref, b_ref, carry_ref):
        first_in_seq = pl.program_id(0) % tiles_per_seq == 0
        xc = _causal_conv(pad_ref, xl_ref[...], halo_ref[...], first_in_seq, cw_ref[...], LRU_CONV) + cb_ref[...]
        a, bx = _lru_point(_nn(xc, wa_ref[...]) + ba_ref[...], _nn(xc, wx_ref[...]) + bx_ref[...], xc, lam_ref[...])
        a_ref[...] = a
        b_ref[...] = bx

        @pl.when(first_in_seq)
        def _():
            carry_ref[...] = jnp.zeros_like(carry_ref)

        def step(r, h):
            h = a_ref[pl.ds(r, 1), :] * h + b_ref[pl.ds(r, 1), :]
            h_out[pl.ds(r, 1), :] = h
            return h

        carry_ref[...] = lax.fori_loop(0, ts, step, carry_ref[...], unroll=8)
        y_out[...] = h_out[...] * _gelu(gate_ref[...])

    return pl.pallas_call(
        body, name="lru_fwd", grid=(t // ts,),
        in_specs=[_row(ts, n), _prev_halo(ts, n), _row(ts, n), _const((LRU_CONV, n)), _const((1, n)), _const((n, n)), _const((1, n)),
                  _const((n, n)), _const((1, n)), _const((1, n))],
        out_specs=[_row(ts, n), _row(ts, n)], out_shape=[_sds((t, n)), _sds((t, n))],
        scratch_shapes=[pltpu.VMEM((HALO + ts, n), F32), pltpu.VMEM((ts, n), F32), pltpu.VMEM((ts, n), F32), pltpu.VMEM((1, n), F32)],
        compiler_params=_params(),
    )(xl, xl, gate, w['ab_conv_w'], w['ab_conv_b'], w['Wa'], w['ab_b_rg_a'], w['Wx'], w['ab_b_rg_x'], w['ab_lambda'])


def _lru_bwd(xl, gate, hs, dy, w, ts, seq):
    t, n = xl.shape
    tiles_per_seq = seq // ts
    n_tiles = t // ts

    def rev(i):
        return n_tiles - 1 - i

    row = pl.BlockSpec((ts, n), lambda i: (rev(i), 0))
    prev = pl.BlockSpec((HALO, n), lambda i: (jnp.maximum(rev(i) * (ts // HALO) - 1, 0), 0))
    acc = lambda shape: pl.BlockSpec(shape, lambda i: (0,) * len(shape))

    def body(xl_ref, xhalo_ref, gate_ref, h_ref, hhalo_ref, dy_ref, cw_ref, cb_ref, wa_ref, ba_ref, wx_ref, bx_ref, lam_ref,
             dxl_out, dgate_out, dcw_out, dcb_out, dwa_out, dba_out, dwx_out, dbx_out, dlam_out,
             pad_ref, padh_ref, padd_ref, a_ref, g_ref, carry_ref, dhalo_ref):
        step_id = pl.program_id(0)
        first = step_id == 0
        tile = rev(step_id)
        first_in_seq = tile % tiles_per_seq == 0
        last_in_seq = tile % tiles_per_seq == tiles_per_seq - 1
        cw = cw_ref[...]
        xc = _causal_conv(pad_ref, xl_ref[...], xhalo_ref[...], first_in_seq, cw, LRU_CONV) + cb_ref[...]
        pre_a = _nn(xc, wa_ref[...]) + ba_ref[...]
        pre_x = _nn(xc, wx_ref[...]) + bx_ref[...]
        (a, _), vjp_point = jax.vjp(_lru_point, pre_a, pre_x, xc, lam_ref[...])
        h = h_ref[...]
        _, vjp_out = jax.vjp(lambda h_, g_: h_ * _gelu(g_), h, gate_ref[...])
        dh, dgate = vjp_out(dy_ref[...])
        dgate_out[...] = dgate
        a_ref[...] = a
        g_ref[...] = dh

        @pl.when(last_in_seq)
        def _():
            carry_ref[...] = jnp.zeros_like(carry_ref)

        def step(j, c):
            r = ts - 1 - j
            g = g_ref[pl.ds(r, 1), :] + c
            g_ref[pl.ds(r, 1), :] = g
            return a_ref[pl.ds(r, 1), :] * g

        carry_ref[...] = lax.fori_loop(0, ts, step, carry_ref[...], unroll=8)
        g = g_ref[...]
        padh_ref[:HALO, :] = jnp.where(first_in_seq, 0.0, hhalo_ref[...])
        padh_ref[HALO:, :] = h
        dpre_a, dpre_x, dxc, dlam = vjp_point((g * padh_ref[HALO - 1:HALO - 1 + ts, :], g))
        dxc = dxc + _nt(dpre_a, wa_ref[...]) + _nt(dpre_x, wx_ref[...])
        _accumulate(dwa_out, _tn(xc, dpre_a), first)
        _accumulate(dwx_out, _tn(xc, dpre_x), first)
        _accumulate(dba_out, _colsum(dpre_a), first)
        _accumulate(dbx_out, _colsum(dpre_x), first)
        _accumulate(dlam_out, dlam, first)
        _accumulate(dcb_out, _colsum(dxc), first)
        _accumulate(dcw_out, _causal_conv_wgrad(pad_ref, dxc, LRU_CONV), first)
        dxl_out[...] = _causal_conv_transpose(padd_ref, dxc, dhalo_ref[...], last_in_seq, cw, LRU_CONV)
        dhalo_ref[...] = dxc[:HALO, :]

    return pl.pallas_call(
        body, name="lru_bwd", grid=(n_tiles,),
        in_specs=[row, prev, row, row, prev, row, _const((LRU_CONV, n)), _const((1, n)), _const((n, n)), _const((1, n)),
                  _const((n, n)), _const((1, n)), _const((1, n))],
        out_specs=[row, row, acc((LRU_CONV, n)), acc((1, n)), acc((n, n)), acc((1, n)), acc((n, n)), acc((1, n)), acc((1, n))],
        out_shape=[_sds((t, n)), _sds((t, n)), _sds((LRU_CONV, n)), _sds((1, n)), _sds((n, n)), _sds((1, n)), _sds((n, n)),
                   _sds((1, n)), _sds((1, n))],
        scratch_shapes=[pltpu.VMEM((HALO + ts, n), F32), pltpu.VMEM((HALO + ts, n), F32), pltpu.VMEM((ts + HALO, n), F32),
                        pltpu.VMEM((ts, n), F32), pltpu.VMEM((ts, n), F32), pltpu.VMEM((1, n), F32), pltpu.VMEM((HALO, n), F32)],
        compiler_params=_params(),
    )(xl, xl, gate, hs, hs, dy, w['ab_conv_w'], w['ab_conv_b'], w['Wa'], w['ab_b_rg_a'], w['Wx'], w['ab_b_rg_x'], w['ab_lambda'])


def _ab_out_fwd(x, o, y, w, tm):
    t, d = x.shape
    hp = o.shape[1]

    def body(x_ref, o_ref, y_ref, wa_ref, wb_ref, h_out):
        h_out[...] = x_ref[...] + _nn(o_ref[...], wa_ref[...]) + _nn(y_ref[...], wb_ref[...])

    return pl.pallas_call(body, name="ab_out_fwd", grid=(t // tm,),
                          in_specs=[_row(tm, d), _row(tm, hp), _row(tm, LRU_W), _const((hp, d)), _const((LRU_W, d))],
                          out_specs=_row(tm, d), out_shape=_sds((t, d)), compiler_params=_params())(x, o, y, w['Wo_a'], w['Wo_b'])


def _ab_out_bwd(o, y, dh, w, tm):
    t, d = dh.shape
    hp = o.shape[1]

    def body(o_ref, y_ref, dh_ref, wa_ref, wb_ref, do_out, dy_out, dwa_out, dwb_out):
        first = pl.program_id(0) == 0
        dh_t = dh_ref[...]
        do_out[...] = _nt(dh_t, wa_ref[...])
        dy_out[...] = _nt(dh_t, wb_ref[...])
        _accumulate(dwa_out, _tn(o_ref[...], dh_t), first)
        _accumulate(dwb_out, _tn(y_ref[...], dh_t), first)

    return pl.pallas_call(body, name="ab_out_bwd", grid=(t // tm,),
                          in_specs=[_row(tm, hp), _row(tm, LRU_W), _row(tm, d), _const((hp, d)), _const((LRU_W, d))],
                          out_specs=[_row(tm, hp), _row(tm, LRU_W), _const((hp, d)), _const((LRU_W, d))],
                          out_shape=[_sds((t, hp)), _sds((t, LRU_W)), _sds((hp, d)), _sds((LRU_W, d))],
                          compiler_params=_params())(o, y, dh, w['Wo_a'], w['Wo_b'])


FFN_CONV = 3


def _ffn_a_fwd(h, norm, wg, wu, tm):
    t, d = h.shape
    fb = D_FF // FF_BLOCKS

    def body(h_ref, gn_ref, wg_ref, wu_ref, g_out, u_out, hn_out):
        hn = _bf(_rms(h_ref[...], gn_ref[...]))
        hn_out[0] = hn
        g_out[...] = _nt(hn, wg_ref[...])
        u_out[...] = _nt(hn, wu_ref[...])

    wspec = pl.BlockSpec((fb, d), lambda f, i: (f, 0))
    ospec = pl.BlockSpec((tm, fb), lambda f, i: (i, f))
    return pl.pallas_call(
        body, name="ffn_a_fwd", grid=(FF_BLOCKS, t // tm),
        in_specs=[pl.BlockSpec((tm, d), lambda f, i: (i, 0)), pl.BlockSpec((1, d), lambda f, i: (0, 0)), wspec, wspec],
        out_specs=[ospec, ospec, pl.BlockSpec((1, tm, d), lambda f, i: (f, i, 0))],
        out_shape=[_sds((t, D_FF)), _sds((t, D_FF)), _sds((FF_BLOCKS, t, d), BF16)], compiler_params=_params(2))(h, norm, wg, wu)


def _ffn_b_fwd(g, u, h, cw, cb, wd, tm, seq):
    t, d = h.shape
    tiles_per_seq = seq // tm

    def body(g_ref, halo_ref, u_ref, h_ref, cw_ref, cb_ref, wd_ref, h_out, pad_ref, act_ref):
        pad_ref[:HALO, :] = jnp.where(pl.program_id(0) % tiles_per_seq == 0, 0.0, halo_ref[...])
        pad_ref[HALO:, :] = g_ref[...]
        cw = cw_ref[...]
        cb = cb_ref[...]
        for c0 in range(0, D_FF, STRIP_LANES):
            cols = slice(c0, min(c0 + STRIP_LANES, D_FF))
            for r in range(0, tm, STRIP):
                taps = _conv_taps(pad_ref, r, cols, FFN_CONV)
                gc = cb[:, cols] + cw[0:1, cols] * taps[0] + cw[1:2, cols] * taps[1] + cw[2:3, cols] * taps[2]
                act_ref[r:r + STRIP, cols] = _bf(_gelu(gc) * u_ref[r:r + STRIP, cols])
        h_out[...] = h_ref[...] + _nn(act_ref[...], wd_ref[...])

    return pl.pallas_call(body, name="ffn_b_fwd", grid=(t // tm,),
                          in_specs=[_row(tm, D_FF), _prev_halo(tm, D_FF), _row(tm, D_FF), _row(tm, d), _const((FFN_CONV, D_FF)),
                                    _const((1, D_FF)), _const((D_FF, d))],
                          out_specs=_row(tm, d), out_shape=_sds((t, d)),
                          scratch_shapes=[pltpu.VMEM((HALO + tm, D_FF), F32), pltpu.VMEM((tm, D_FF), BF16)],
                          compiler_params=_params())(g, g, u, h, cw, cb, wd)


def _ffn_b_bwd(g, u, dout, cw, cb, wd, tm, seq):
    t, d = dout.shape
    fb = D_FF // FF_BLOCKS
    tiles_per_seq = seq // tm

    def body(g_ref, halo_ref, u_ref, dout_ref, cw_ref, cb_ref, wd_ref, dgc_out, du_out, dwd_out, dcw_out, dcb_out,
             pad_ref, dact_ref, act_ref, acc_ref, dwd_acc):
        i = pl.program_id(1)
        first = i == 0
        pad_ref[:HALO, :] = jnp.where(i % tiles_per_seq == 0, 0.0, halo_ref[...])
        pad_ref[HALO:, :] = g_ref[...]
        dout_b = _bf(dout_ref[...])
        dact_ref[...] = _nt(dout_b, wd_ref[...])
        cw = cw_ref[...]
        cb = cb_ref[...]
        fold = lambda a: a[:HALO] + a[HALO:]
        for c0 in range(0, fb, STRIP_LANES):
            cols = slice(c0, min(c0 + STRIP_LANES, fb))
            sums = [jnp.zeros((HALO, cols.stop - c0), F32) for _ in range(1 + FFN_CONV)]
            for r in range(0, tm, STRIP):
                rows = slice(r, r + STRIP)
                taps = _conv_taps(pad_ref, r, cols, FFN_CONV)
                gelu, dgelu = _gelu_and_grad(cb[:, cols] + cw[0:1, cols] * taps[0] + cw[1:2, cols] * taps[1] + cw[2:3, cols] * taps[2])
                u = u_ref[rows, cols]
                dact = dact_ref[rows, cols]
                act_ref[rows, cols] = _bf(gelu * u)
                du_out[rows, cols] = _bf(dact * gelu)
                dgc = dact * u * dgelu
                dgc_out[rows, cols] = dgc
                sums = [sums[0] + fold(dgc)] + [sums[1 + k] + fold(dgc * taps[k]) for k in range(FFN_CONV)]
            for k in range(1 + FFN_CONV):
                acc_ref[k, :, cols] = sums[k]
        _accumulate(dwd_acc, _tn(act_ref[...], dout_b), first)

        @pl.when(i == t // tm - 1)
        def _():
            dwd_out[...] = _bf(dwd_acc[...])

        _accumulate(dcb_out, _colsum(acc_ref[0]), first)
        _accumulate(dcw_out, jnp.concatenate([_colsum(acc_ref[1 + k]) for k in range(FFN_CONV)], axis=0), first)

    blk = pl.BlockSpec((tm, fb), lambda f, i: (i, f))
    halo = pl.BlockSpec((HALO, fb), lambda f, i: (jnp.maximum(i * (tm // HALO) - 1, 0), f))
    wd_blk = pl.BlockSpec((fb, d), lambda f, i: (f, 0), pipeline_mode=pl.Buffered(1))
    return pl.pallas_call(
        body, name="ffn_b_bwd", grid=(FF_BLOCKS, t // tm),
        in_specs=[blk, halo, blk, pl.BlockSpec((tm, d), lambda f, i: (i, 0)), pl.BlockSpec((FFN_CONV, fb), lambda f, i: (0, f)),
                  pl.BlockSpec((1, fb), lambda f, i: (0, f)), wd_blk],
        out_specs=[blk, blk, wd_blk, pl.BlockSpec((FFN_CONV, fb), lambda f, i: (0, f)),
                   pl.BlockSpec((1, fb), lambda f, i: (0, f))],
        out_shape=[_sds((t, D_FF)), _sds((t, D_FF), BF16), _sds((D_FF, d), BF16), _sds((FFN_CONV, D_FF)), _sds((1, D_FF))],
        scratch_shapes=[pltpu.VMEM((HALO + tm, fb), F32), pltpu.VMEM((tm, fb), F32), pltpu.VMEM((tm, fb), BF16),
                        pltpu.VMEM((1 + FFN_CONV, HALO, fb), F32), pltpu.VMEM((fb, d), F32)],
        compiler_params=_params(2))(g, g, u, dout, cw, cb, wd)


def _ffn_a_dgrad(h, norm, dgc, du, dres, cw, wg, wu, tm, seq):
    t, d = h.shape
    tiles_per_seq = seq // tm
    n_tiles = t // tm

    def body(h_ref, gn_ref, dgc_ref, halo_ref, du_ref, dres_ref, cw_ref, wg_ref, wu_ref, dh_out, dg_out, dgn_out, pad_ref):
        i = pl.program_id(0)
        last_in_seq = i % tiles_per_seq == tiles_per_seq - 1
        dg = _bf(_causal_conv_transpose(pad_ref, dgc_ref[...], halo_ref[...], last_in_seq, cw_ref[...], FFN_CONV))
        dg_out[...] = dg
        _, vjp_norm = jax.vjp(_rms, h_ref[...], gn_ref[...])
        dh, dgn = vjp_norm(_nn(dg, wg_ref[...]) + _nn(du_ref[...], wu_ref[...]))
        dh_out[...] = dh + dres_ref[...]
        _accumulate(dgn_out, dgn, i == 0)

    return pl.pallas_call(
        body, name="ffn_a_dgrad", grid=(n_tiles,),
        in_specs=[_row(tm, d), _const((1, d)), _row(tm, D_FF), _next_halo(tm, D_FF, n_tiles), _row(tm, D_FF), _row(tm, d),
                  _const((FFN_CONV, D_FF)), _const((D_FF, d)), _const((D_FF, d))],
        out_specs=[_row(tm, d), _row(tm, D_FF), _const((1, d))], out_shape=[_sds((t, d)), _sds((t, D_FF), BF16), _sds((1, d))],
        scratch_shapes=[pltpu.VMEM((tm + HALO, D_FF), F32)], compiler_params=_params())(h, norm, dgc, dgc, du, dres, cw, wg, wu)


def _ffn_a_wgrad(hn, dg, du, tm):
    _, t, d = hn.shape
    fb = D_FF // FF_BLOCKS

    n_tiles = t // tm

    def body(hn_ref, dg_ref, du_ref, dwg_out, dwu_out, acc_g, acc_u):
        i = pl.program_id(1)
        hn_t = hn_ref[0]
        _accumulate(acc_g, _tn(dg_ref[...], hn_t), i == 0)
        _accumulate(acc_u, _tn(du_ref[...], hn_t), i == 0)

        @pl.when(i == n_tiles - 1)
        def _():
            dwg_out[...] = _bf(acc_g[...])
            dwu_out[...] = _bf(acc_u[...])

    blk = pl.BlockSpec((tm, fb), lambda f, i: (i, f))
    wspec = pl.BlockSpec((fb, d), lambda f, i: (f, 0), pipeline_mode=pl.Buffered(1))
    return pl.pallas_call(body, name="ffn_a_wgrad", grid=(FF_BLOCKS, n_tiles),
                          in_specs=[pl.BlockSpec((1, tm, d), lambda f, i: (0, i, 0)), blk, blk],
                          out_specs=[wspec, wspec], out_shape=[_sds((D_FF, d), BF16), _sds((D_FF, d), BF16)],
                          scratch_shapes=[pltpu.VMEM((fb, d), F32), pltpu.VMEM((fb, d), F32)],
                          compiler_params=_params(2))(hn, dg, du)


def _sgu_mix(vn, ws_ref, bst):
    tril = lax.broadcasted_iota(jnp.int32, (CHUNK, CHUNK), 0) >= lax.broadcasted_iota(jnp.int32, (CHUNK, CHUNK), 1)
    wms = [jnp.where(tril, ws_ref[g], 0.0) for g in range(SGU_GROUPS)]
    chunks = []
    for n in range(vn.shape[0] // CHUNK):
        vc = vn[n * CHUNK:(n + 1) * CHUNK, :]
        chunks.append(jnp.concatenate(
            [_nn(wms[g], vc[:, g * CHUNK:(g + 1) * CHUNK]) + bst[:, g:g + 1] for g in range(SGU_GROUPS)], axis=1))
    return jnp.concatenate(chunks, axis=0)


def _sgu_fwd(h, w, tm):
    t, d = h.shape

    def body(h_ref, cn_ref, win_ref, lg_ref, lb_ref, ws_ref, bst_ref, wout_ref, h_out):
        h_t = h_ref[...]
        z = _gelu(_nn(_rms(h_t, cn_ref[...]), win_ref[...]))
        vn = _layer_norm(z[:, d:], lg_ref[...], lb_ref[...])
        s = _sgu_mix(vn, ws_ref, bst_ref[...])
        h_out[...] = h_t + _nn(z[:, :d] * s, wout_ref[...])

    return pl.pallas_call(
        body, name="sgu_fwd", grid=(t // tm,),
        in_specs=[_row(tm, d), _const((1, d)), _const((d, 2 * d)), _const((1, d)), _const((1, d)), _const((SGU_GROUPS, CHUNK, CHUNK)),
                  _const((CHUNK, LANES)), _const((d, d))],
        out_specs=_row(tm, d), out_shape=_sds((t, d)), compiler_params=_params(),
    )(h, w['c_norm'], w['c_w_in'], w['c_ln_g'], w['c_ln_b'], w['c_w_s'], w['bsT'], w['c_w_out'])


def _sgu_bwd(h, dout, w, tm):
    t, d = h.shape

    def body(h_ref, dout_ref, cn_ref, win_ref, lg_ref, lb_ref, ws_ref, bst_ref, wout_ref,
             dh_out, dcn_out, dwin_out, dlg_out, dlb_out, dws_out, dbst_out, dwout_out):
        first = pl.program_id(0) == 0
        hn, vjp_norm = jax.vjp(_rms, h_ref[...], cn_ref[...])
        zpre = _nn(hn, win_ref[...])
        u, vjp_u = jax.vjp(_gelu, zpre[:, :d])
        vn, vjp_v = jax.vjp(lambda zp, lg, lb: _layer_norm(_gelu(zp), lg, lb), zpre[:, d:], lg_ref[...], lb_ref[...])
        s = _sgu_mix(vn, ws_ref, bst_ref[...])
        dout_t = dout_ref[...]
        dus = _nt(dout_t, wout_ref[...])
        _accumulate(dwout_out, _tn(u * s, dout_t), first)
        ds = dus * u
        tril = lax.broadcasted_iota(jnp.int32, (CHUNK, CHUNK), 0) >= lax.broadcasted_iota(jnp.int32, (CHUNK, CHUNK), 1)
        lane = lax.broadcasted_iota(jnp.int32, (CHUNK, LANES), 1)
        dws = [jnp.zeros((CHUNK, CHUNK), F32) for _ in range(SGU_GROUPS)]
        dbst = jnp.zeros((CHUNK, LANES), F32)
        dvn_chunks = []
        for n in range(tm // CHUNK):
            cols = []
            for g in range(SGU_GROUPS):
                ds_ng = ds[n * CHUNK:(n + 1) * CHUNK, g * CHUNK:(g + 1) * CHUNK]
                vc_ng = vn[n * CHUNK:(n + 1) * CHUNK, g * CHUNK:(g + 1) * CHUNK]
                cols.append(_tn(jnp.where(tril, ws_ref[g], 0.0), ds_ng))
                dws[g] = dws[g] + _nt(ds_ng, vc_ng)
                dbst = dbst + jnp.where(lane == g, jnp.sum(ds_ng, axis=1, keepdims=True), 0.0)
            dvn_chunks.append(jnp.concatenate(cols, axis=1))
        dvn = jnp.concatenate(dvn_chunks, axis=0)
        for g in range(SGU_GROUPS):
            val = jnp.where(tril, dws[g], 0.0)

            @pl.when(first)
            def _():
                dws_out[g] = val

            @pl.when(jnp.logical_not(first))
            def _():
                dws_out[g] += val
        _accumulate(dbst_out, dbst, first)
        (dzu,) = vjp_u(dus * s)
        dzv, dlg, dlb = vjp_v(dvn)
        _accumulate(dlg_out, dlg, first)
        _accumulate(dlb_out, dlb, first)
        dzpre = jnp.concatenate([dzu, dzv], axis=1)
        _accumulate(dwin_out, _tn(hn, dzpre), first)
        dh, dcn = vjp_norm(_nt(dzpre, win_ref[...]))
        _accumulate(dcn_out, dcn, first)
        dh_out[...] = dh + dout_t

    return pl.pallas_call(
        body, name="sgu_bwd", grid=(t // tm,),
        in_specs=[_row(tm, d), _row(tm, d), _const((1, d)), _const((d, 2 * d)), _const((1, d)), _const((1, d)),
                  _const((SGU_GROUPS, CHUNK, CHUNK)), _const((CHUNK, LANES)), _const((d, d))],
        out_specs=[_row(tm, d), _const((1, d)), _const((d, 2 * d)), _const((1, d)), _const((1, d)), _const((SGU_GROUPS, CHUNK, CHUNK)),
                   _const((CHUNK, LANES)), _const((d, d))],
        out_shape=[_sds((t, d)), _sds((1, d)), _sds((d, 2 * d)), _sds((1, d)), _sds((1, d)), _sds((SGU_GROUPS, CHUNK, CHUNK)),
                   _sds((CHUNK, LANES)), _sds((d, d))],
        compiler_params=_params(),
    )(h, dout, w['c_norm'], w['c_w_in'], w['c_ln_g'], w['c_ln_b'], w['c_w_s'], w['bsT'], w['c_w_out'])


def _final_loss(h, target, norm, tm):
    t, d = h.shape

    def body(h_ref, tgt_ref, gn_ref, dh_out, loss_out, dgn_out):
        first = pl.program_id(0) == 0
        tgt = tgt_ref[...]

        def loss_fn(h_, g_):
            err = _rms(h_, g_) - tgt
            return 0.5 * jnp.sum(jnp.mean(err * err, axis=-1, keepdims=True), axis=0, keepdims=True)

        loss, vjp_loss = jax.vjp(loss_fn, h_ref[...], gn_ref[...])
        dh, dgn = vjp_loss(jnp.ones((1, 1), F32))
        dh_out[...] = dh
        _accumulate(loss_out, loss, first)
        _accumulate(dgn_out, dgn, first)

    return pl.pallas_call(body, name="final_loss", grid=(t // tm,), in_specs=[_row(tm, d), _row(tm, d), _const((1, d))],
                          out_specs=[_row(tm, d), _const((1, 1)), _const((1, d))],
                          out_shape=[_sds((t, d)), _sds((1, 1)), _sds((1, d))], compiler_params=_params())(h, target, norm)


def _tile(t, seq, want):
    tm = min(want, seq)
    assert seq % tm == 0 and t % tm == 0 and tm % CHUNK == 0
    return tm


def _local_step(x, posb, target, w, seq, late_weights, on_grads):
    t, d = x.shape
    b = t // seq
    hp = HEADS * HEAD_PAD
    tm_big, tm_mid = _tile(t, seq, 512), _tile(t, seq, 256)
    tq = _tile(t, seq, 512)

    q, k, v, xl, gate = _ab_in_fwd(x, posb, w, tm_big)
    o = _attn_fwd(q.reshape(b, seq, hp), k.reshape(b, seq, hp), v.reshape(b, seq, hp), tq).reshape(t, hp)
    y, hs = _lru_fwd(xl, gate, w, tm_big, seq)
    w = {**w, **late_weights('out0', y)}
    h1 = _ab_out_fwd(x, o, y, w, tm_big)
    hcur = h1
    saved = []
    for l in range(2):
        if l == 1:
            w = {**w, **late_weights('mix1', hcur)}
            saved_h2 = hcur
            hcur = _sgu_fwd(hcur, w, tm_mid)
        wl = late_weights('ffn%d' % l, hcur)
        g, u, hn = _ffn_a_fwd(hcur, w['ffn_norm'][l], wl['Wg'], wl['Wu'], tm_big)
        hnext = _ffn_b_fwd(g, u, hcur, w['ffn_conv_w'][l], w['ffn_conv_b'][l], wl['Wd'], tm_mid, seq)
        saved.append((hcur, g, u, wl, hn))
        hcur = hnext
    dh, loss, d_final = _final_loss(hcur, target, w['final_norm'], tm_big)

    ffn = {}
    conv_b = list(w['ffn_conv_b'])
    for l in (1, 0):
        hin, g, u, wl, hn = saved[l]
        dgc, du, d_wd, d_cw, d_cb = _ffn_b_bwd(g, u, dh, w['ffn_conv_w'][l], conv_b[l], wl['Wd'], tm_big, seq)
        dh, dg, d_norm = _ffn_a_dgrad(hin, w['ffn_norm'][l], dgc, du, dh, w['ffn_conv_w'][l], wl['Wg'], wl['Wu'], tm_mid, seq)
        d_wg, d_wu = _ffn_a_wgrad(hn, dg, du, _tile(t, seq, 1024))
        ffn[l] = dict(ffn_norm=d_norm, ffn_conv_w=d_cw, ffn_conv_b=d_cb, Wg=d_wg, Wu=d_wu, Wd=d_wd)
        if l == 1:
            dh, d_cn, d_cwin, d_lg, d_lb, d_ws, d_bst, d_cwout = _sgu_bwd(saved_h2, dh, w, tm_mid)
            zero = on_grads('late1', dict(final_norm=d_final, c_norm=d_cn, c_ln_g=d_lg, c_ln_b=d_lb, c_w_s=d_ws, bsT=d_bst, c_w_in=d_cwin,
                                          c_w_out=d_cwout, Wg=[d_wg], Wu=[d_wu], Wd=[d_wd]))
            conv_b[0] = conv_b[0] + zero
    late0 = {name: [ffn[0][name], ffn[1][name]] for name in ('ffn_norm', 'ffn_conv_w', 'ffn_conv_b')}
    zero = on_grads('late0', dict(late0, Wg=[ffn[0]['Wg']], Wu=[ffn[0]['Wu']], Wd=[ffn[0]['Wd']]))
    w = {**w, 'Wo_b': w['Wo_b'] + zero.astype(w['Wo_b'].dtype)}
    do, dy, d_woa, d_wob = _ab_out_bwd(o, y, dh, w, tm_big)
    dxl, dgate, d_cw, d_cb, d_wa, d_ba, d_wx, d_bx, d_lam = _lru_bwd(xl, gate, hs, dy, w, tm_big, seq)
    zero = on_grads('mid', dict(Wo_a=d_woa, Wo_b=d_wob, ab_conv_w=d_cw, ab_conv_b=d_cb, Wa=d_wa, ab_b_rg_a=d_ba, Wx=d_wx,
                                ab_b_rg_x=d_bx, ab_lambda=d_lam))
    w = {**w, 'ab_norm': w['ab_norm'] + zero}
    dq, dk, dv = _attn_bwd(q.reshape(b, seq, hp), k.reshape(b, seq, hp), v.reshape(b, seq, hp), do.reshape(b, seq, hp), tq)
    dx, d_gn, d_win, d_qn, d_wq, d_kvn, d_wk, d_wv = _ab_in_bwd(
        x, posb, w, dq.reshape(t, hp), dk.reshape(t, hp), dv.reshape(t, hp), dxl, dgate, dh, tm_mid)
    return loss, dx, dict(ab_norm=d_gn, W_in=d_win, ab_q_norm=d_qn, Wq=d_wq, ab_kv_norm=d_kvn, Wk=d_wk, Wv=d_wv)


def _block_diag(wg):
    g, n, _ = wg.shape
    return jnp.einsum('gij,gh->gihj', wg, jnp.eye(g, dtype=wg.dtype)).reshape(g * n, g * n)


def _prepare_out(w_out):
    d = w_out.shape[2]
    mla = HEADS * QK_NOPE
    return {'Wo_a': jnp.pad(w_out[0, :mla].reshape(HEADS, QK_NOPE, d), ((0, 0), (0, HEAD_PAD - QK_NOPE), (0, 0))).reshape(HEADS * HEAD_PAD, d),
            'Wo_b': w_out[0, mla:]}


def _prepare(full):
    d = full['ab_w_in'].shape[1]
    w_in = full['ab_w_in'][0]
    zeros = lambda n: jnp.zeros((d, n), w_in.dtype)
    wq = full['ab_w_q_b'][0].reshape(Q_LORA, HEADS, QK_NOPE + QK_ROPE)
    wkv = full['ab_w_kv_b'][0].reshape(KV_LORA, HEADS, 2 * QK_NOPE)
    pad_head = lambda a: jnp.pad(a, ((0, 0), (0, 0), (0, HEAD_PAD - a.shape[2]))).reshape(a.shape[0], HEADS * HEAD_PAD)
    w = {
        'W_in': jnp.concatenate([w_in[:, :Z_KPE], zeros(QK_NOPE), w_in[:, Z_KPE:Z_KPE + QK_ROPE],
                                 zeros(HEAD_PAD - QK_NOPE - QK_ROPE), w_in[:, Z_KPE + QK_ROPE:]], axis=1),
        'Wq': pad_head(wq), 'Wk': pad_head(wkv[:, :, :QK_NOPE]), 'Wv': pad_head(wkv[:, :, QK_NOPE:]),
        'Wa': _bf(_block_diag(full['ab_w_rg_a'][0])), 'Wx': _bf(_block_diag(full['ab_w_rg_x'][0])),
        'c_w_s': full['c_w_s'][0],
        'bsT': jnp.pad(full['c_b_s'][0].T, ((0, 0), (0, LANES - SGU_GROUPS))),
        'ffn_norm': [full['ffn_norm'][l:l + 1] for l in range(2)], 'ffn_conv_w': [full['ffn_conv_w'][l] for l in range(2)],
        'ffn_conv_b': [full['ffn_conv_b'][l:l + 1] for l in range(2)],
        'ab_conv_w': full['ab_conv_w'][0], 'final_norm': full['final_norm'][None, :],
    }
    for name in ('ab_norm', 'ab_q_norm', 'ab_kv_norm', 'ab_conv_b', 'ab_b_rg_a', 'ab_b_rg_x', 'ab_lambda', 'c_norm', 'c_ln_g', 'c_ln_b'):
        w[name] = full[name]
    return w


def _unprepare(g):
    unpad_head = lambda a, n: a.reshape(a.shape[0], HEADS, HEAD_PAD)[:, :, :n]
    diag = lambda a: jnp.einsum('gigj->gij', a.reshape(HEADS, LRU_W // HEADS, HEADS, LRU_W // HEADS))
    rules = {
        'ab_w_in': (('W_in',), lambda a: jnp.concatenate([a[:, :Z_KPE], a[:, Z_KPE + QK_NOPE:Z_KPE + QK_NOPE + QK_ROPE], a[:, Z_LRU:]], axis=1)[None]),
        'ab_w_q_b': (('Wq',), lambda a: unpad_head(a, QK_NOPE + QK_ROPE).reshape(1, Q_LORA, -1)),
        'ab_w_kv_b': (('Wk', 'Wv'), lambda a, b: jnp.concatenate([unpad_head(a, QK_NOPE), unpad_head(b, QK_NOPE)], axis=2).reshape(1, KV_LORA, -1)),
        'ab_w_out': (('Wo_a', 'Wo_b'), lambda a, b: jnp.concatenate(
            [a.reshape(HEADS, HEAD_PAD, -1)[:, :QK_NOPE].reshape(HEADS * QK_NOPE, -1), b], axis=0)[None]),
        'ab_w_rg_a': (('Wa',), lambda a: diag(a)[None]), 'ab_w_rg_x': (('Wx',), lambda a: diag(a)[None]),
        'c_w_in': (('c_w_in',), lambda a: a[None]), 'c_w_out': (('c_w_out',), lambda a: a[None]), 'c_w_s': (('c_w_s',), lambda a: a[None]),
        'c_b_s': (('bsT',), lambda a: a[:, :SGU_GROUPS].T[None]),
        'ffn_w_gate': (('Wg',), jnp.stack), 'ffn_w_up': (('Wu',), jnp.stack), 'ffn_w_down': (('Wd',), jnp.stack),
        'ffn_norm': (('ffn_norm',), lambda a: jnp.concatenate(a, axis=0)), 'ffn_conv_w': (('ffn_conv_w',), jnp.stack),
        'ffn_conv_b': (('ffn_conv_b',), lambda a: jnp.concatenate(a, axis=0)),
        'ab_conv_w': (('ab_conv_w',), lambda a: a[None]), 'final_norm': (('final_norm',), lambda a: a[0]),
    }
    for name in ('ab_norm', 'ab_q_norm', 'ab_kv_norm', 'ab_conv_b', 'ab_b_rg_a', 'ab_b_rg_x', 'ab_lambda', 'c_norm', 'c_ln_g', 'c_ln_b'):
        rules[name] = ((name,), lambda a: a)
    return {name: fn(*[g[k] for k in keys]) for name, (keys, fn) in rules.items() if all(k in g for k in keys)}


SLAB_ROWS = 16


def _round_up(n, m):
    return -(-n // m) * m


def _to_chunks(full, axis):
    s = full.shape
    return jnp.moveaxis(full.reshape(s[:axis] + (N_DEV, s[axis] // N_DEV) + s[axis + 1:]), axis, 0)


def _from_chunks(chunks, axis):
    local = chunks.shape[1:]
    return jnp.moveaxis(chunks, 0, axis).reshape(local[:axis] + (N_DEV * local[axis],) + local[axis + 1:])


def _merge_columns(landed, name):
    _, _, r, n = landed.shape
    tr = r // 4

    def body(l_ref, o_ref):
        o_ref[0] = jnp.concatenate([l_ref[dev, 0] for dev in range(N_DEV)], axis=1)

    return pl.pallas_call(body, name="merge_" + name, grid=(r // tr,),
                          in_specs=[pl.BlockSpec((N_DEV, 1, tr, n), lambda i: (0, 0, i, 0))],
                          out_specs=pl.BlockSpec((1, tr, N_DEV * n), lambda i: (0, i, 0)),
                          out_shape=jax.ShapeDtypeStruct((1, r, N_DEV * n), landed.dtype), compiler_params=_params())(landed)


def _split_chunks(whole, axis, name):
    _, rows, cols = whole.shape
    if axis == 1:
        r = rows // N_DEV

        def body(x_ref, o_ref):
            o_ref[0] = _bf(x_ref[...])

        grid, out_shape = (N_DEV,), (N_DEV, 1, r, cols)
        spec, out_spec = pl.BlockSpec((1, r, cols), lambda dev: (0, dev, 0)), pl.BlockSpec((1, 1, r, cols), lambda dev: (dev, 0, 0, 0))
    else:
        n, tr = cols // N_DEV, rows // 4

        def body(x_ref, o_ref):
            x = x_ref[0]
            for dev in range(N_DEV):
                o_ref[dev, 0] = _bf(x[:, dev * n:(dev + 1) * n])

        grid, out_shape = (rows // tr,), (N_DEV, 1, rows, n)
        spec, out_spec = pl.BlockSpec((1, tr, cols), lambda i: (0, i, 0)), pl.BlockSpec((N_DEV, 1, tr, n), lambda i: (0, 0, i, 0))
    return pl.pallas_call(body, name="split_" + name, grid=grid, in_specs=[spec], out_specs=out_spec,
                          out_shape=jax.ShapeDtypeStruct(out_shape, BF16), compiler_params=_params())(whole)


def _slab_rows(n):
    return _round_up(-(-n // LANES), SLAB_ROWS)


def _to_slab(a, lead):
    a = a.reshape(lead + (-1,))
    rows = _slab_rows(a.shape[-1])
    a = jnp.pad(a, [(0, 0)] * len(lead) + [(0, rows * LANES - a.shape[-1])])
    return a.reshape(lead + (rows, LANES))


def _pack_slabs(parts, lead):
    return jnp.concatenate([_to_slab(p, lead) for p in parts], axis=len(lead))


def _unpack_slabs(packed, shapes):
    lead = packed.shape[:-2]
    out, row = [], 0
    for shape in shapes:
        size = math.prod(shape)
        rows = _slab_rows(size)
        piece = lax.slice_in_dim(packed, row, row + rows, axis=len(lead))
        out.append(piece.reshape(lead + (rows * LANES,))[..., :size].reshape(lead + tuple(shape)))
        row += rows
    return out


HBM = pl.BlockSpec(memory_space=pl.ANY)


def _other_chips(x, y):
    return [(1 - x, y), (x, 1 - y), (1 - x, 1 - y)]


def _all_gather(blocks):
    n = len(blocks)

    def body(*refs):
        x_refs, out_refs, token = refs[:n], refs[n:2 * n], refs[2 * n]
        send_sems, recv_sems, local_sems = refs[2 * n + 1:]
        token[...] = jnp.zeros_like(token)
        x, y, c = lax.axis_index("x"), lax.axis_index("y"), lax.axis_index("c")
        me, sibling = (x, y, c), (x, y, 1 - c)
        chips = _other_chips(x, y)

        def slab(a, px, py, pc):
            return out_refs[a].at[4 * px + 2 * py + pc]

        def copy(a, k, blk, to, src=None):
            return pltpu.make_async_remote_copy(src_ref=slab(a, *blk) if src is None else src, dst_ref=slab(a, *blk),
                                                send_sem=send_sems.at[7 * a + k], recv_sem=recv_sems.at[7 * a + k],
                                                device_id=to, device_id_type=MESH)

        mine = [pltpu.make_async_copy(x_refs[a], slab(a, *me), local_sems.at[a]) for a in range(n)]
        started = []
        for a in range(n):
            mine[a].start()
            started.append(copy(a, 0, me, sibling, src=x_refs[a]))
            started += [copy(a, 1 + j, me, (*chip, c), src=x_refs[a]) for j, chip in enumerate(chips)]
        for cp in started:
            cp.start()
        for j, chip in enumerate(chips):
            for a in range(n):
                copy(a, 1 + j, (*chip, c), me).wait_recv()
                passed = copy(a, 4 + j, (*chip, c), sibling)
                passed.start()
                started.append(passed)
        for a in range(n):
            copy(a, 0, sibling, me).wait_recv()
        for j, chip in enumerate(chips):
            for a in range(n):
                copy(a, 4 + j, (*chip, 1 - c), me).wait_recv()
        for cp in started:
            cp.wait_send()
        for a in range(n):
            mine[a].wait()

    out = pl.pallas_call(
        body, name="all_gather_weights",
        out_shape=[jax.ShapeDtypeStruct((N_DEV,) + b.shape, b.dtype) for b in blocks] + [jax.ShapeDtypeStruct((8, LANES), F32)],
        in_specs=[HBM] * n, out_specs=[HBM] * n + [pl.BlockSpec(memory_space=pltpu.VMEM)],
        scratch_shapes=[pltpu.SemaphoreType.DMA((7 * n,)), pltpu.SemaphoreType.DMA((7 * n,)), pltpu.SemaphoreType.DMA((n,))],
    )(*blocks)
    return list(out[:n]), out[n][0, 0]


FLIPS = [(0, 0, 1), (1, 0, 0), (1, 0, 1), (0, 1, 0), (0, 1, 1), (1, 1, 0), (1, 1, 1)]


def _peers(x, y, c):
    flip = lambda v, f: 1 - v if f else v
    return [(flip(x, fx), flip(y, fy), flip(c, fc)) for fx, fy, fc in FLIPS]


def _direct_copies(src_refs, land_refs, send_sems, recv_sems, scatter):
    x, y, c = lax.axis_index("x"), lax.axis_index("y"), lax.axis_index("c")
    me = 4 * x + 2 * y + c
    starts, waits = [], []
    for a in range(len(src_refs)):
        for k, (px, py, pc) in enumerate(_peers(x, y, c)):
            peer = 4 * px + 2 * py + pc
            sems = dict(send_sem=send_sems.at[7 * a + k], recv_sem=recv_sems.at[7 * a + k], device_id=(px, py, pc), device_id_type=MESH)
            src = src_refs[a].at[peer] if scatter else src_refs[a]
            starts.append(pltpu.make_async_remote_copy(src_ref=src, dst_ref=land_refs[a].at[me], **sems))
            waits.append(pltpu.make_async_remote_copy(src_ref=src, dst_ref=land_refs[a].at[peer], **sems))
    n = len(src_refs)
    keeps = [] if scatter else [pltpu.make_async_copy(src_refs[a], land_refs[a].at[me], send_sems.at[7 * n + a]) for a in range(n)]
    return starts, waits, keeps


def _landing(src, scatter):
    block = src.shape[1:] if scatter else src.shape
    return jax.ShapeDtypeStruct((N_DEV,) + block, src.dtype)


HBM_SPACE = pl.BlockSpec(memory_space=pltpu.HBM)
SEMAPHORES = pl.BlockSpec(memory_space=pltpu.SEMAPHORE)
SPLIT_EFFECT = pltpu.SideEffectType.DATAFLOW_SIDE_EFFECTING


def _start_exchange(name, srcs, scatter):
    n = len(srcs)
    lands = [lax.empty(s.shape, s.dtype) for s in (_landing(s, scatter) for s in srcs)]

    def body(*refs):
        starts, _, keeps = _direct_copies(refs[:n], refs[n:2 * n], refs[2 * n], refs[2 * n + 1], scatter)
        for cp in starts + keeps:
            cp.start()
        refs[-1][...] = jnp.zeros_like(refs[-1])

    held = [pltpu.with_memory_space_constraint(a, pltpu.HBM) for a in list(srcs) + lands]
    out = pl.pallas_call(
        body, name=name + "_start",
        out_shape=(pltpu.SemaphoreType.DMA(((7 if scatter else 8) * n,)), pltpu.SemaphoreType.DMA((7 * n,)),
                   *[pltpu.HBM(a.shape, a.dtype) for a in held],
                   jax.ShapeDtypeStruct((8, LANES), F32)),
        in_specs=[HBM_SPACE] * (2 * n), out_specs=(SEMAPHORES, SEMAPHORES, *[HBM_SPACE] * (2 * n), pl.BlockSpec(memory_space=pltpu.VMEM)),
        input_output_aliases={i: 2 + i for i in range(2 * n)},
        compiler_params=pltpu.CompilerParams(has_side_effects=SPLIT_EFFECT),
    )(*held)
    return out[0], out[1], list(out[2:2 + n]), list(out[2 + n:2 + 2 * n]), out[-1][0, 0], out[-1]


def _wait_exchange(name, started, after, scatter):
    send_sems, recv_sems, srcs, lands = started[:4]
    n = len(srcs)

    def body(*refs):
        _, waits, keeps = _direct_copies(refs[:n], refs[n:2 * n], refs[2 * n], refs[2 * n + 1], scatter)
        for cp in waits:
            cp.wait_send()
        for cp in waits:
            cp.wait_recv()
        for cp in keeps:
            cp.wait()

    out = pl.pallas_call(
        body, name=name + "_wait", out_shape=tuple(pltpu.HBM(a.shape, a.dtype) for a in srcs + lands),
        in_specs=[HBM_SPACE] * (2 * n) + [SEMAPHORES, SEMAPHORES, HBM], out_specs=tuple([HBM_SPACE] * (2 * n)),
        input_output_aliases={i: i for i in range(2 * n)},
        compiler_params=pltpu.CompilerParams(has_side_effects=SPLIT_EFFECT),
    )(*srcs, *lands, send_sems, recv_sems, after)
    return list(out[:n]), list(out[n:])


def _row_tile(rows):
    return rows // 2 if (rows // 2) % SLAB_ROWS == 0 else rows


def _sum_in_device_order(me_ref, l_ref, own_ref):
    mine = own_ref[0].astype(F32)
    g = jnp.where(me_ref[0] == 0, mine, l_ref[0].astype(F32))
    for dev in range(1, N_DEV):
        g = g + jnp.where(me_ref[0] == dev, mine, l_ref[dev].astype(F32))
    return g


def _adamw(g, w, m, v):
    m_new = ADAM_B1 * m + (1.0 - ADAM_B1) * g
    v_new = ADAM_B2 * v + (1.0 - ADAM_B2) * (g * g)
    m_hat = m_new * (1.0 / (1.0 - ADAM_B1 ** ADAM_STEP))
    v_hat = v_new * (1.0 / (1.0 - ADAM_B2 ** ADAM_STEP))
    return -ADAM_LR * (m_hat / (jnp.sqrt(v_hat) + ADAM_EPS) + ADAM_WD * w), m_new, v_new


def _sum_chunks(me, landed, own, name):
    _, _, r, n = landed.shape

    def body(me_ref, l_ref, own_ref, g_out):
        g_out[...] = _sum_in_device_order(me_ref, l_ref, own_ref)[0]

    return pl.pallas_call(
        body, name="sum_" + name,
        grid_spec=pltpu.PrefetchScalarGridSpec(
            num_scalar_prefetch=1, grid=(1,),
            in_specs=[pl.BlockSpec((N_DEV, 1, r, n), lambda i, me_ref: (0, 0, 0, 0)),
                      pl.BlockSpec((1, 1, r, n), lambda i, me_ref: (me_ref[0], 0, 0, 0))],
            out_specs=pl.BlockSpec((r, n), lambda i, me_ref: (0, 0))),
        out_shape=_sds((r, n)), compiler_params=_params())(me, landed, own)


def _adamw_small(gs, ws, ms, vs):
    n = len(gs)

    def body(*refs):
        ins, outs = refs[:4 * n], refs[4 * n:]
        for i in range(n):
            outs[i][...], outs[n + i][...], outs[2 * n + i][...] = _adamw(*[ins[k * n + i][...] for k in range(4)])

    out = pl.pallas_call(body, name="adamw_small", out_shape=[_sds(w.shape) for w in ws] * 3)(*gs, *ws, *ms, *vs)
    return out[:n], out[n:2 * n], out[2 * n:]


def _sum_and_adamw(me, landed, own, wts, m, v, name, layer=None, into=None):
    layers, r, n = wts.shape
    first = 0 if layer is None else layer
    count = layers if layer is None else 1
    tr = _row_tile(r)
    blk = pl.BlockSpec((1, tr, n), lambda li, ri, me_ref: (first + li, ri, 0))
    held = [] if into is None else list(into)

    def body(me_ref, l_ref, own_ref, w_ref, m_ref, v_ref, *rest):
        g_out, d_out, m_out, v_out = rest[len(held):]
        g = _sum_in_device_order(me_ref, l_ref, own_ref)
        g_out[...] = g
        d_out[...], m_out[...], v_out[...] = _adamw(g, w_ref[...], m_ref[...], v_ref[...])

    return pl.pallas_call(
        body, name="adamw_" + name,
        grid_spec=pltpu.PrefetchScalarGridSpec(
            num_scalar_prefetch=1, grid=(count, r // tr),
            in_specs=[pl.BlockSpec((N_DEV, 1, tr, n), lambda li, ri, me_ref: (0, li, ri, 0)),
                      pl.BlockSpec((1, 1, tr, n), lambda li, ri, me_ref: (me_ref[0], li, ri, 0)), blk, blk, blk] + [HBM] * len(held),
            out_specs=[blk] * 4),
        out_shape=[_sds((layers, r, n))] * 4, input_output_aliases={6 + i: i for i in range(len(held))},
        compiler_params=_params(2))(me, landed, own, wts, m, v, *held)


EARLY = ['ab_w_in']
LATE_STAGES = {
    'out0': [('ab_w_out', None, 'ab_w_out')],
    'ffn0': [('ffn_w_gate', 0, 'Wg'), ('ffn_w_up', 0, 'Wu'), ('ffn_w_down', 0, 'Wd')],
    'mix1': [('c_w_in', None, 'c_w_in'), ('c_w_out', None, 'c_w_out')],
    'ffn1': [('ffn_w_gate', 1, 'Wg'), ('ffn_w_up', 1, 'Wu'), ('ffn_w_down', 1, 'Wd')],
}
TRANSPOSED = ('ffn_w_gate', 'ffn_w_up')


def _stored(name, a):
    return jnp.swapaxes(a, 1, 2) if name in TRANSPOSED else a


def _stored_axis(name):
    return 1 if name in TRANSPOSED else SHARD_AXIS[name]


GRAD_STAGES = {
    'late1': ([('c_w_in', None), ('c_w_out', None), ('ffn_w_gate', 1), ('ffn_w_up', 1), ('ffn_w_down', 1)],
              ['c_norm', 'c_ln_g', 'c_ln_b', 'c_w_s', 'c_b_s', 'final_norm']),
    'late0': ([('ffn_w_gate', 0), ('ffn_w_up', 0), ('ffn_w_down', 0)], ['ffn_norm', 'ffn_conv_w', 'ffn_conv_b']),
    'mid': ([('ab_w_out', None)], ['ab_conv_w', 'ab_conv_b', 'ab_w_rg_a', 'ab_b_rg_a', 'ab_w_rg_x', 'ab_b_rg_x', 'ab_lambda']),
    'last': ([('ab_w_in', None)], ['ab_norm', 'ab_q_norm', 'ab_w_q_b', 'ab_kv_norm', 'ab_w_kv_b']),
}


def _gather_early(local):
    small = [_bf(local[n]) if n in MATRICES else lax.bitcast_convert_type(local[n], BF16) for n in SMALL_SHARDED]
    gathered, zero = _all_gather([_bf(local[n]) for n in EARLY] + [_pack_slabs(small, ())])
    full = {n: local[n] for n in REPLICATED}
    for n, g in zip(EARLY, gathered):
        full[n] = _from_chunks(g, SHARD_AXIS[n])
    for n, p in zip(SMALL_SHARDED, _unpack_slabs(gathered[-1], [s.shape for s in small])):
        full[n] = _from_chunks(p if n in MATRICES else lax.bitcast_convert_type(p, F32), SHARD_AXIS[n])
    return full, zero


def kernel(x, positions, ab_norm, ab_w_in, ab_q_norm, ab_w_q_b, ab_kv_norm, ab_w_kv_b, ab_conv_w, ab_conv_b, ab_w_rg_a, ab_b_rg_a, ab_w_rg_x, ab_b_rg_x, ab_lambda, ab_w_out, c_norm, c_w_in, c_ln_g, c_ln_b, c_w_s, c_b_s, c_w_out, ffn_norm, ffn_w_gate, ffn_w_up, ffn_conv_w, ffn_conv_b, ffn_w_down, final_norm, loss_target, m_ab_norm, m_ab_w_in, m_ab_q_norm, m_ab_w_q_b, m_ab_kv_norm, m_ab_w_kv_b, m_ab_conv_w, m_ab_conv_b, m_ab_w_rg_a, m_ab_b_rg_a, m_ab_w_rg_x, m_ab_b_rg_x, m_ab_lambda, m_ab_w_out, m_c_norm, m_c_w_in, m_c_ln_g, m_c_ln_b, m_c_w_s, m_c_b_s, m_c_w_out, m_ffn_norm, m_ffn_w_gate, m_ffn_w_up, m_ffn_conv_w, m_ffn_conv_b, m_ffn_w_down, m_final_norm, v_ab_norm, v_ab_w_in, v_ab_q_norm, v_ab_w_q_b, v_ab_kv_norm, v_ab_w_kv_b, v_ab_conv_w, v_ab_conv_b, v_ab_w_rg_a, v_ab_b_rg_a, v_ab_w_rg_x, v_ab_b_rg_x, v_ab_lambda, v_ab_w_out, v_c_norm, v_c_w_in, v_c_ln_g, v_c_ln_b, v_c_w_s, v_c_b_s, v_c_w_out, v_ffn_norm, v_ffn_w_gate, v_ffn_w_up, v_ffn_conv_w, v_ffn_conv_b, v_ffn_w_down, v_final_norm):
    given = dict(locals())
    local = {n: given[n] for n in WEIGHTS}
    b, seq, d = x.shape
    t = b * seq

    me = (4 * lax.axis_index("x") + 2 * lax.axis_index("y") + lax.axis_index("c")).astype(jnp.int32)
    me1 = me.reshape(1)

    full, zero = _gather_early(local)
    gathers = {}
    for stage, members in LATE_STAGES.items():
        srcs = [_bf(_stored(n, local[n] if layer is None else local[n][layer:layer + 1]) + zero) for n, layer, _ in members]
        gathers[stage] = _start_exchange('gather_' + stage, srcs, scatter=False)
        zero = gathers[stage][4]
    w = _prepare(full)
    w['ab_norm'] = w['ab_norm'] + zero

    def late_weights(stage, after):
        _, lands = _wait_exchange('gather_' + stage, gathers[stage], after, scatter=False)
        whole = [l.reshape(1, -1, l.shape[-1]) if _stored_axis(n) == 1 else _merge_columns(l, n)
                 for (n, _, _), l in zip(LATE_STAGES[stage], lands)]
        if stage == 'out0':
            return _prepare_out(whole[0])
        return {key: a[0] for (_, _, key), a in zip(LATE_STAGES[stage], whole)}

    scatters = {}

    def start_scatter(stage, g):
        whole = _unprepare(g)
        big, small = GRAD_STAGES[stage]
        slab = [_to_chunks(whole[n], SHARD_AXIS[n]) if n in SHARD_AXIS else jnp.broadcast_to(whole[n][None], (N_DEV,) + whole[n].shape)
                for n in small]
        own = [whole[n].reshape(N_DEV, 1, whole[n].shape[1] // N_DEV, whole[n].shape[2])
               if whole[n].dtype == BF16 and _stored_axis(n) == 1 else
               _split_chunks(whole[n], _stored_axis(n), n + ('' if layer is None else str(layer))) for n, layer in big]
        own.append(_bf(_pack_slabs(slab, (N_DEV,)))[:, None])
        scatters[stage] = _start_exchange('scatter_' + stage, own, scatter=True)
        return scatters[stage][4]

    posb = jnp.broadcast_to(positions.astype(F32).reshape(t, 1), (t, LANES))
    loss, dx, grads = _local_step(x.reshape(t, d), posb, loss_target.reshape(t, d), w, seq, late_weights, start_scatter)
    start_scatter('last', grads)
    after = scatters['last'][5]

    updated, small_grads = {}, {}
    for stage, (big, small) in GRAD_STAGES.items():
        owns, landed = _wait_exchange('scatter_' + stage, scatters[stage], after, scatter=True)
        for (n, layer), own, land in zip(big, owns, landed):
            updated[n] = _sum_and_adamw(me1, land, own, _stored(n, given[n]), _stored(n, given['m_' + n]), _stored(n, given['v_' + n]),
                                        n + ('' if layer is None else str(layer)), layer, updated.get(n))
        summed = _sum_chunks(me1, landed[-1], owns[-1], stage)
        small_grads.update(zip(small, _unpack_slabs(summed, [local[n].shape for n in small])))
        after = sum([updated[n][1][:1, :1, :1] for n, _ in big], summed[:1, :1].reshape(1, 1, 1))
    names = list(small_grads)
    news = _adamw_small([small_grads[n] for n in names], *[[given[p + n] for n in names] for p in ('', 'm_', 'v_')])
    for i, n in enumerate(names):
        updated[n] = [small_grads[n], news[0][i], news[1][i], news[2][i]]
    total = lax.psum(loss[0, 0], ("x", "y", "c"))
    return (total, dx.reshape(b, seq, d), *[_stored(n, updated[n][kind]) for kind in range(4) for n in WEIGHTS])
```

```python
import math

import jax
import jax.numpy as jnp
from jax import lax
from jax.experimental import pallas as pl
from jax.experimental.pallas import tpu as pltpu

F32 = jnp.float32
BF16 = jnp.bfloat16
MESH = pl.DeviceIdType.MESH

N_DEV = 8
LANES = 128
HALO = 8
VMEM_LIMIT = 56 << 20

NORM_EPS = 1e-6
HEADS = 8
HEAD_PAD = 128
QK_NOPE = 64
QK_ROPE = 32
ROPE_HALF = 16
ROPE_BASE = 10000.0
ATTN_SCALE = (QK_NOPE + QK_ROPE) ** -0.5
LRU_C = 8.0
LRU_W = 512
CHUNK = 128
SGU_GROUPS = 8
D_FF = 2816
FF_BLOCKS = 2

ADAM_LR, ADAM_B1, ADAM_B2, ADAM_EPS, ADAM_WD, ADAM_STEP = 0.001, 0.9, 0.999, 1e-08, 0.01, 10

WEIGHTS = ['ab_norm', 'ab_w_in', 'ab_q_norm', 'ab_w_q_b', 'ab_kv_norm', 'ab_w_kv_b', 'ab_conv_w', 'ab_conv_b',
           'ab_w_rg_a', 'ab_b_rg_a', 'ab_w_rg_x', 'ab_b_rg_x', 'ab_lambda', 'ab_w_out', 'c_norm', 'c_w_in', 'c_ln_g',
           'c_ln_b', 'c_w_s', 'c_b_s', 'c_w_out', 'ffn_norm', 'ffn_w_gate', 'ffn_w_up', 'ffn_conv_w', 'ffn_conv_b',
           'ffn_w_down', 'final_norm']
SHARD_AXIS = {'ab_w_in': 2, 'ab_w_q_b': 2, 'ab_w_kv_b': 2, 'ab_conv_w': 2, 'ab_w_out': 1, 'c_norm': 1, 'c_w_in': 2,
              'c_ln_g': 1, 'c_ln_b': 1, 'c_w_out': 1, 'ffn_w_gate': 2, 'ffn_w_up': 2, 'ffn_conv_w': 2, 'ffn_w_down': 1}
MATRICES = ['ab_w_in', 'ab_w_q_b', 'ab_w_kv_b', 'ab_w_out', 'c_w_in', 'c_w_out', 'ffn_w_gate', 'ffn_w_up', 'ffn_w_down']
BIG = ['ab_w_in', 'c_w_in', 'ffn_w_gate', 'ffn_w_up', 'ab_w_out', 'c_w_out', 'ffn_w_down']
REPLICATED = [n for n in WEIGHTS if n not in SHARD_AXIS]
SMALL_SHARDED = [n for n in WEIGHTS if n in SHARD_AXIS and n not in BIG]


def _bf(x):
    return x.astype(BF16)


def _nn(a, b):
    return lax.dot_general(_bf(a), _bf(b), (((1,), (0,)), ((), ())), preferred_element_type=F32)


def _nt(a, b):
    return lax.dot_general(_bf(a), _bf(b), (((1,), (1,)), ((), ())), preferred_element_type=F32)


def _tn(a, b):
    return lax.dot_general(_bf(a), _bf(b), (((0,), (0,)), ((), ())), preferred_element_type=F32)


def _rms(x, g):
    return x * lax.rsqrt(jnp.mean(x * x, axis=-1, keepdims=True) + NORM_EPS) * g


def _layer_norm(x, g, b):
    xc = x - jnp.mean(x, axis=-1, keepdims=True)
    return xc * lax.rsqrt(jnp.mean(xc * xc, axis=-1, keepdims=True) + NORM_EPS) * g + b


def _gelu(x):
    return jax.nn.gelu(x)


STRIP = 16
STRIP_LANES = 384
GELU_C = math.sqrt(2.0 / math.pi)
GELU_A = 0.044715


def _gelu_and_grad(x):
    x2 = x * x
    t = jnp.tanh(x * (GELU_C + (GELU_C * GELU_A) * x2))
    half_x = 0.5 * x
    one_plus_t = 1.0 + t
    return half_x * one_plus_t, 0.5 * one_plus_t + half_x * (1.0 - t * t) * (GELU_C + (3.0 * GELU_C * GELU_A) * x2)


def _colsum(x):
    return jnp.sum(x, axis=0, keepdims=True)


def _softplus(x):
    return jnp.maximum(x, 0.0) + jnp.log1p(jnp.exp(-jnp.abs(x)))


@jax.custom_vjp
def _decay(x):
    a = jnp.exp(x)
    y = 2.0 * x
    series = -y * (1.0 + y * (1 / 2 + y * (1 / 6 + y * (1 / 24 + y * (1 / 120 + y * (1 / 720))))))
    return a, jnp.where(y < -0.3, 1.0 - a * a, series)


def _decay_fwd(x):
    a, gap = _decay(x)
    return (a, gap), a


def _decay_bwd(a, cts):
    return (a * (cts[0] - 2.0 * a * cts[1]),)


_decay.defvjp(_decay_fwd, _decay_bwd)


def _accumulate(ref, val, first):
    @pl.when(first)
    def _():
        ref[...] = val

    @pl.when(jnp.logical_not(first))
    def _():
        ref[...] += val


def _params(n_axes=1):
    return pltpu.CompilerParams(dimension_semantics=("arbitrary",) * n_axes, vmem_limit_bytes=VMEM_LIMIT)


def _row(tm, n):
    return pl.BlockSpec((tm, n), lambda i: (i, 0))


def _const(shape):
    nd = len(shape)
    return pl.BlockSpec(shape, lambda i: (0,) * nd, pipeline_mode=pl.Buffered(1))


def _prev_halo(tm, n):
    return pl.BlockSpec((HALO, n), lambda i: (jnp.maximum(i * (tm // HALO) - 1, 0), 0))


def _next_halo(tm, n, n_tiles):
    last = n_tiles * (tm // HALO) - 1
    return pl.BlockSpec((HALO, n), lambda i: (jnp.minimum((i + 1) * (tm // HALO), last), 0))


def _sds(shape, dtype=F32):
    return jax.ShapeDtypeStruct(shape, dtype)


def _rope_tables(posb):
    lane = lax.broadcasted_iota(jnp.int32, posb.shape, 1)
    in_rope = jnp.logical_and(lane >= QK_NOPE, lane < QK_NOPE + QK_ROPE)
    j = (lane & (ROPE_HALF - 1)).astype(F32)
    inv_freq = jnp.exp((-math.log(ROPE_BASE)) * j / ROPE_HALF)
    ang = posb * inv_freq
    return jnp.where(in_rope, jnp.cos(ang), 1.0), jnp.where(in_rope, jnp.sin(ang), 0.0)


def _rot(q):
    n = q.shape[1]
    lane = lax.broadcasted_iota(jnp.int32, q.shape, 1) & (HEAD_PAD - 1)
    first_half = jnp.where(lane >= QK_NOPE, -pltpu.roll(q, n - ROPE_HALF, 1), 0.0)
    second_half = jnp.where(lane < QK_NOPE + QK_ROPE, pltpu.roll(q, ROPE_HALF, 1), 0.0)
    return jnp.where(lane < QK_NOPE + ROPE_HALF, first_half, second_half)


def _rope(q, cos_t, sin_t):
    return q * cos_t + _rot(q) * sin_t


def _rope_transpose(dq, cos_t, sin_t):
    return dq * cos_t - _rot(dq * sin_t)


def _tile_heads(t):
    return jnp.concatenate([t] * HEADS, axis=1)


Q_LORA, KV_LORA = 256, 128
Z_KPE = Q_LORA + KV_LORA
Z_LRU = Z_KPE + HEAD_PAD
Z_GATE = Z_LRU + LRU_W
Z_WIDTH = Z_GATE + LRU_W


def _ab_in_fwd(x, posb, w, tm):
    t, d = x.shape

    def body(x_ref, pos_ref, gn_ref, win_ref, qn_ref, wq_ref, kvn_ref, wk_ref, wv_ref, q_out, k_out, v_out, xl_out, gate_out):
        hn = _rms(x_ref[...], gn_ref[...])
        z = _nn(hn, win_ref[...])
        cqn = _rms(z[:, :Q_LORA], qn_ref[...])
        kvn = _rms(z[:, Q_LORA:Z_KPE], kvn_ref[...])
        cos_t, sin_t = _rope_tables(pos_ref[...])
        q_out[...] = _rope(_nn(cqn, wq_ref[...]), _tile_heads(cos_t), _tile_heads(sin_t))
        kpe = _rope(z[:, Z_KPE:Z_LRU], cos_t, sin_t)
        k_out[...] = _nn(kvn, wk_ref[...]) + _tile_heads(kpe)
        v_out[...] = _nn(kvn, wv_ref[...])
        xl_out[...] = z[:, Z_LRU:Z_GATE]
        gate_out[...] = z[:, Z_GATE:]

    hp = HEADS * HEAD_PAD
    return pl.pallas_call(
        body, name="ab_in_fwd", grid=(t // tm,),
        in_specs=[_row(tm, d), _row(tm, LANES), _const((1, d)), _const((d, Z_WIDTH)), _const((1, Q_LORA)), _const((Q_LORA, hp)),
                  _const((1, KV_LORA)), _const((KV_LORA, hp)), _const((KV_LORA, hp))],
        out_specs=[_row(tm, hp), _row(tm, hp), _row(tm, hp), _row(tm, LRU_W), _row(tm, LRU_W)],
        out_shape=[_sds((t, hp)), _sds((t, hp)), _sds((t, hp)), _sds((t, LRU_W)), _sds((t, LRU_W))],
        compiler_params=_params(),
    )(x, posb, w['ab_norm'], w['W_in'], w['ab_q_norm'], w['Wq'], w['ab_kv_norm'], w['Wk'], w['Wv'])


def _ab_in_bwd(x, posb, w, dq, dk, dv, dxl, dgate, dres, tm):
    t, d = x.shape
    hp = HEADS * HEAD_PAD

    def body(x_ref, pos_ref, gn_ref, win_ref, qn_ref, wq_ref, kvn_ref, wk_ref, wv_ref, dq_ref, dk_ref, dv_ref, dxl_ref, dgate_ref,
             dres_ref, dx_out, dgn_out, dwin_out, dqn_out, dwq_out, dkvn_out, dwk_out, dwv_out):
        first = pl.program_id(0) == 0
        hn, vjp_in = jax.vjp(_rms, x_ref[...], gn_ref[...])
        z = _nn(hn, win_ref[...])
        cqn, vjp_q = jax.vjp(_rms, z[:, :Q_LORA], qn_ref[...])
        kvn, vjp_kv = jax.vjp(_rms, z[:, Q_LORA:Z_KPE], kvn_ref[...])
        cos_t, sin_t = _rope_tables(pos_ref[...])
        dq0 = _rope_transpose(dq_ref[...], _tile_heads(cos_t), _tile_heads(sin_t))
        dk0 = dk_ref[...]
        dv0 = dv_ref[...]
        dkpe = dk0[:, :HEAD_PAD]
        for h in range(1, HEADS):
            dkpe = dkpe + dk0[:, h * HEAD_PAD:(h + 1) * HEAD_PAD]
        dkpe = _rope_transpose(dkpe, cos_t, sin_t)
        _accumulate(dwq_out, _tn(cqn, dq0), first)
        _accumulate(dwk_out, _tn(kvn, dk0), first)
        _accumulate(dwv_out, _tn(kvn, dv0), first)
        dcq, dqn = vjp_q(_nt(dq0, wq_ref[...]))
        dckv, dkvn = vjp_kv(_nt(dk0, wk_ref[...]) + _nt(dv0, wv_ref[...]))
        _accumulate(dqn_out, dqn, first)
        _accumulate(dkvn_out, dkvn, first)
        dz = jnp.concatenate([dcq, dckv, dkpe, dxl_ref[...], dgate_ref[...]], axis=1)
        _accumulate(dwin_out, _tn(hn, dz), first)
        dx, dgn = vjp_in(_nt(dz, win_ref[...]))
        _accumulate(dgn_out, dgn, first)
        dx_out[...] = dx + dres_ref[...]

    return pl.pallas_call(
        body, name="ab_in_bwd", grid=(t // tm,),
        in_specs=[_row(tm, d), _row(tm, LANES), _const((1, d)), _const((d, Z_WIDTH)), _const((1, Q_LORA)), _const((Q_LORA, hp)),
                  _const((1, KV_LORA)), _const((KV_LORA, hp)), _const((KV_LORA, hp)),
                  _row(tm, hp), _row(tm, hp), _row(tm, hp), _row(tm, LRU_W), _row(tm, LRU_W), _row(tm, d)],
        out_specs=[_row(tm, d), _const((1, d)), _const((d, Z_WIDTH)), _const((1, Q_LORA)), _const((Q_LORA, hp)),
                   _const((1, KV_LORA)), _const((KV_LORA, hp)), _const((KV_LORA, hp))],
        out_shape=[_sds((t, d)), _sds((1, d)), _sds((d, Z_WIDTH)), _sds((1, Q_LORA)), _sds((Q_LORA, hp)),
                   _sds((1, KV_LORA)), _sds((KV_LORA, hp)), _sds((KV_LORA, hp))],
        compiler_params=_params(),
    )(x, posb, w['ab_norm'], w['W_in'], w['ab_q_norm'], w['Wq'], w['ab_kv_norm'], w['Wk'], w['Wv'], dq, dk, dv, dxl, dgate, dres)


def _attn_probs(q_blk, k_ext, tq):
    ext = k_ext.shape[0]
    s = lax.dot_general(q_blk, k_ext, (((1,), (1,)), ((), ())), preferred_element_type=F32) * ATTN_SCALE
    causal = lax.broadcasted_iota(jnp.int32, (tq, tq), 1) <= lax.broadcasted_iota(jnp.int32, (tq, tq), 0)
    diag = jnp.where(causal, s[:, ext - tq:], -1e30)
    s = diag if ext == tq else jnp.concatenate([s[:, :ext - tq], diag], axis=1)
    p = jnp.exp(s - jnp.max(s, axis=1, keepdims=True))
    return p / jnp.sum(p, axis=1, keepdims=True)


def _attn_fwd(q, k, v, tq):
    b, s, hp = q.shape
    blk = pl.BlockSpec((1, s, HEAD_PAD), lambda bi, h: (bi, 0, h))

    def body(q_ref, k_ref, v_ref, o_ref, p_ref):
        kb = _bf(k_ref[0])
        vb = _bf(v_ref[0])
        for i in range(s // tq):
            ext = (i + 1) * tq
            p = _bf(_attn_probs(_bf(q_ref[0, i * tq:ext, :]), kb[:ext], tq))
            p_ref[0, 0, i * tq:ext, :ext] = p
            o_ref[0, i * tq:ext, :] = lax.dot_general(p, vb[:ext], (((1,), (0,)), ((), ())), preferred_element_type=F32)

    return pl.pallas_call(body, name="attn_fwd", grid=(b, HEADS), in_specs=[blk, blk, blk],
                          out_specs=[blk, pl.BlockSpec((1, 1, s, s), lambda bi, h: (bi, h, 0, 0))],
                          out_shape=[_sds((b, s, hp)), _sds((b, HEADS, s, s), BF16)], compiler_params=_params(2))(q, k, v)


def _attn_bwd(q, k, v, probs, do, tq):
    b, s, hp = q.shape
    blk = pl.BlockSpec((1, s, HEAD_PAD), lambda bi, h: (bi, 0, h))

    def body(q_ref, k_ref, v_ref, p_ref, do_ref, dq_ref, dk_ref, dv_ref):
        kb = _bf(k_ref[0])
        vb = _bf(v_ref[0])
        dk_ref[...] = jnp.zeros_like(dk_ref)
        dv_ref[...] = jnp.zeros_like(dv_ref)
        for i in range(s // tq):
            ext = (i + 1) * tq
            qb = _bf(q_ref[0, i * tq:ext, :])
            dob = _bf(do_ref[0, i * tq:ext, :])
            pb = p_ref[0, 0, i * tq:ext, :ext]
            p = pb.astype(F32)
            dv_ref[0, :ext, :] += lax.dot_general(pb, dob, (((0,), (0,)), ((), ())), preferred_element_type=F32)
            dp = lax.dot_general(dob, vb[:ext], (((1,), (1,)), ((), ())), preferred_element_type=F32)
            ds = _bf(p * (dp - jnp.sum(p * dp, axis=1, keepdims=True)) * ATTN_SCALE)
            dq_ref[0, i * tq:ext, :] = lax.dot_general(ds, kb[:ext], (((1,), (0,)), ((), ())), preferred_element_type=F32)
            dk_ref[0, :ext, :] += lax.dot_general(ds, qb, (((0,), (0,)), ((), ())), preferred_element_type=F32)

    return pl.pallas_call(body, name="attn_bwd", grid=(b, HEADS),
                          in_specs=[blk, blk, blk, pl.BlockSpec((1, 1, s, s), lambda bi, h: (bi, h, 0, 0)), blk], out_specs=[blk, blk, blk],
                          out_shape=[_sds((b, s, hp))] * 3, compiler_params=_params(2))(q, k, v, probs, do)


LRU_CONV = 4


def _lru_point(pre_a, pre_x, xc, lam):
    r = jax.nn.sigmoid(pre_a)
    i = jax.nn.sigmoid(pre_x)
    a, gap = _decay(-LRU_C * r * _softplus(-lam))
    return a, jnp.sqrt(gap) * (i * xc)


def _causal_conv(pad_ref, x, halo, first_in_seq, w, taps):
    tm = x.shape[0]
    pad_ref[:HALO, :] = jnp.where(first_in_seq, 0.0, halo)
    pad_ref[HALO:, :] = x
    y = w[taps - 1:taps, :] * x
    for k in range(taps - 1):
        off = HALO - (taps - 1) + k
        y = y + w[k:k + 1, :] * pad_ref[off:off + tm, :]
    return y


def _conv_taps(pad_ref, r, cols, taps):
    blocks = [pad_ref[r + j * HALO:r + (j + 1) * HALO, cols] for j in range(1 + STRIP // HALO)]
    sub = lax.broadcasted_iota(jnp.int32, blocks[0].shape, 0)
    out = []
    for k in range(taps - 1):
        s = taps - 1 - k
        rolled = [pltpu.roll(b, s, 0) for b in blocks]
        out.append(jnp.concatenate([jnp.where(sub < s, rolled[j], rolled[j + 1]) for j in range(STRIP // HALO)], axis=0))
    out.append(jnp.concatenate(blocks[1:], axis=0))
    return out


def _causal_conv_wgrad(pad_ref, dy, taps):
    tm = dy.shape[0]
    return jnp.concatenate([_colsum(dy * pad_ref[HALO - (taps - 1) + k:HALO - (taps - 1) + k + tm, :]) for k in range(taps)], axis=0)


def _causal_conv_transpose(pad_ref, dy, halo_next, last_in_seq, w, taps):
    tm = dy.shape[0]
    pad_ref[:tm, :] = dy
    pad_ref[tm:, :] = jnp.where(last_in_seq, 0.0, halo_next)
    dx = w[taps - 1:taps, :] * dy
    for k in range(taps - 1):
        off = (taps - 1) - k
        dx = dx + w[k:k + 1, :] * pad_ref[off:off + tm, :]
    return dx


def _lru_fwd(xl, gate, w, ts, seq):
    t, n = xl.shape
    tiles_per_seq = seq // ts

    def body(xl_ref, halo_ref, gate_ref, cw_ref, cb_ref, wa_ref, ba_ref, wx_ref, bx_ref, lam_ref, y_out, h_out, pad_ref, a_ref, b_ref, carry_ref):
        first_in_seq = pl.program_id(0) % tiles_per_seq == 0
        xc = _causal_conv(pad_ref, xl_ref[...], halo_ref[...], first_in_seq, cw_ref[...], LRU_CONV) + cb_ref[...]
        a, bx = _lru_point(_nn(xc, wa_ref[...]) + ba_ref[...], _nn(xc, wx_ref[...]) + bx_ref[...], xc, lam_ref[...])
        a_ref[...] = a
        b_ref[...] = bx

        @pl.when(first_in_seq)
        def _():
            carry_ref[...] = jnp.zeros_like(carry_ref)

        def step(r, h):
            h = a_ref[pl.ds(r, 1), :] * h + b_ref[pl.ds(r, 1), :]
            h_out[pl.ds(r, 1), :] = h
            return h

        carry_ref[...] = lax.fori_loop(0, ts, step, carry_ref[...], unroll=8)
        y_out[...] = h_out[...] * _gelu(gate_ref[...])

    return pl.pallas_call(
        body, name="lru_fwd", grid=(t // ts,),
        in_specs=[_row(ts, n), _prev_halo(ts, n), _row(ts, n), _const((LRU_CONV, n)), _const((1, n)), _const((n, n)), _const((1, n)),
                  _const((n, n)), _const((1, n)), _const((1, n))],
        out_specs=[_row(ts, n), _row(ts, n)], out_shape=[_sds((t, n)), _sds((t, n))],
        scratch_shapes=[pltpu.VMEM((HALO + ts, n), F32), pltpu.VMEM((ts, n), F32), pltpu.VMEM((ts, n), F32), pltpu.VMEM((1, n), F32)],
        compiler_params=_params(),
    )(xl, xl, gate, w['ab_conv_w'], w['ab_conv_b'], w['Wa'], w['ab_b_rg_a'], w['Wx'], w['ab_b_rg_x'], w['ab_lambda'])


def _lru_bwd(xl, gate, hs, dy, w, ts, seq):
    t, n = xl.shape
    tiles_per_seq = seq // ts
    n_tiles = t // ts

    def rev(i):
        return n_tiles - 1 - i

    row = pl.BlockSpec((ts, n), lambda i: (rev(i), 0))
    prev = pl.BlockSpec((HALO, n), lambda i: (jnp.maximum(rev(i) * (ts // HALO) - 1, 0), 0))
    acc = lambda shape: pl.BlockSpec(shape, lambda i: (0,) * len(shape))

    def body(xl_ref, xhalo_ref, gate_ref, h_ref, hhalo_ref, dy_ref, cw_ref, cb_ref, wa_ref, ba_ref, wx_ref, bx_ref, lam_ref,
             dxl_out, dgate_out, dcw_out, dcb_out, dwa_out, dba_out, dwx_out, dbx_out, dlam_out,
             pad_ref, padh_ref, padd_ref, a_ref, g_ref, carry_ref, dhalo_ref):
        step_id = pl.program_id(0)
        first = step_id == 0
        tile = rev(step_id)
        first_in_seq = tile % tiles_per_seq == 0
        last_in_seq = tile % tiles_per_seq == tiles_per_seq - 1
        cw = cw_ref[...]
        xc = _causal_conv(pad_ref, xl_ref[...], xhalo_ref[...], first_in_seq, cw, LRU_CONV) + cb_ref[...]
        pre_a = _nn(xc, wa_ref[...]) + ba_ref[...]
        pre_x = _nn(xc, wx_ref[...]) + bx_ref[...]
        (a, _), vjp_point = jax.vjp(_lru_point, pre_a, pre_x, xc, lam_ref[...])
        h = h_ref[...]
        _, vjp_out = jax.vjp(lambda h_, g_: h_ * _gelu(g_), h, gate_ref[...])
        dh, dgate = vjp_out(dy_ref[...])
        dgate_out[...] = dgate
        a_ref[...] = a
        g_ref[...] = dh

        @pl.when(last_in_seq)
        def _():
            carry_ref[...] = jnp.zeros_like(carry_ref)

        def step(j, c):
            r = ts - 1 - j
            g = g_ref[pl.ds(r, 1), :] + c
            g_ref[pl.ds(r, 1), :] = g
            return a_ref[pl.ds(r, 1), :] * g

        carry_ref[...] = lax.fori_loop(0, ts, step, carry_ref[...], unroll=8)
        g = g_ref[...]
        padh_ref[:HALO, :] = jnp.where(first_in_seq, 0.0, hhalo_ref[...])
        padh_ref[HALO:, :] = h
        dpre_a, dpre_x, dxc, dlam = vjp_point((g * padh_ref[HALO - 1:HALO - 1 + ts, :], g))
        dxc = dxc + _nt(dpre_a, wa_ref[...]) + _nt(dpre_x, wx_ref[...])
        _accumulate(dwa_out, _tn(xc, dpre_a), first)
        _accumulate(dwx_out, _tn(xc, dpre_x), first)
        _accumulate(dba_out, _colsum(dpre_a), first)
        _accumulate(dbx_out, _colsum(dpre_x), first)
        _accumulate(dlam_out, dlam, first)
        _accumulate(dcb_out, _colsum(dxc), first)
        _accumulate(dcw_out, _causal_conv_wgrad(pad_ref, dxc, LRU_CONV), first)
        dxl_out[...] = _causal_conv_transpose(padd_ref, dxc, dhalo_ref[...], last_in_seq, cw, LRU_CONV)
        dhalo_ref[...] = dxc[:HALO, :]

    return pl.pallas_call(
        body, name="lru_bwd", grid=(n_tiles,),
        in_specs=[row, prev, row, row, prev, row, _const((LRU_CONV, n)), _const((1, n)), _const((n, n)), _const((1, n)),
                  _const((n, n)), _const((1, n)), _const((1, n))],
        out_specs=[row, row, acc((LRU_CONV, n)), acc((1, n)), acc((n, n)), acc((1, n)), acc((n, n)), acc((1, n)), acc((1, n))],
        out_shape=[_sds((t, n)), _sds((t, n)), _sds((LRU_CONV, n)), _sds((1, n)), _sds((n, n)), _sds((1, n)), _sds((n, n)),
                   _sds((1, n)), _sds((1, n))],
        scratch_shapes=[pltpu.VMEM((HALO + ts, n), F32), pltpu.VMEM((HALO + ts, n), F32), pltpu.VMEM((ts + HALO, n), F32),
                        pltpu.VMEM((ts, n), F32), pltpu.VMEM((ts, n), F32), pltpu.VMEM((1, n), F32), pltpu.VMEM((HALO, n), F32)],
        compiler_params=_params(),
    )(xl, xl, gate, hs, hs, dy, w['ab_conv_w'], w['ab_conv_b'], w['Wa'], w['ab_b_rg_a'], w['Wx'], w['ab_b_rg_x'], w['ab_lambda'])


def _ab_out_fwd(x, o, y, w, tm):
    t, d = x.shape
    hp = o.shape[1]

    def body(x_ref, o_ref, y_ref, wa_ref, wb_ref, h_out):
        h_out[...] = x_ref[...] + _nn(o_ref[...], wa_ref[...]) + _nn(y_ref[...], wb_ref[...])

    return pl.pallas_call(body, name="ab_out_fwd", grid=(t // tm,),
                          in_specs=[_row(tm, d), _row(tm, hp), _row(tm, LRU_W), _const((hp, d)), _const((LRU_W, d))],
                          out_specs=_row(tm, d), out_shape=_sds((t, d)), compiler_params=_params())(x, o, y, w['Wo_a'], w['Wo_b'])


def _ab_out_bwd(o, y, dh, w, tm):
    t, d = dh.shape
    hp = o.shape[1]

    def body(o_ref, y_ref, dh_ref, wa_ref, wb_ref, do_out, dy_out, dwa_out, dwb_out):
        first = pl.program_id(0) == 0
        dh_t = dh_ref[...]
        do_out[...] = _nt(dh_t, wa_ref[...])
        dy_out[...] = _nt(dh_t, wb_ref[...])
        _accumulate(dwa_out, _tn(o_ref[...], dh_t), first)
        _accumulate(dwb_out, _tn(y_ref[...], dh_t), first)

    return pl.pallas_call(body, name="ab_out_bwd", grid=(t // tm,),
                          in_specs=[_row(tm, hp), _row(tm, LRU_W), _row(tm, d), _const((hp, d)), _const((LRU_W, d))],
                          out_specs=[_row(tm, hp), _row(tm, LRU_W), _const((hp, d)), _const((LRU_W, d))],
                          out_shape=[_sds((t, hp)), _sds((t, LRU_W)), _sds((hp, d)), _sds((LRU_W, d))],
                          compiler_params=_params())(o, y, dh, w['Wo_a'], w['Wo_b'])


FFN_CONV = 3


def _ffn_a_fwd(h, norm, wg, wu, tm):
    t, d = h.shape
    fb = D_FF // FF_BLOCKS

    def body(h_ref, gn_ref, wg_ref, wu_ref, g_out, u_out, hn_out):
        hn = _bf(_rms(h_ref[...], gn_ref[...]))
        hn_out[0] = hn
        g_out[...] = _nt(hn, wg_ref[...])
        u_out[...] = _nt(hn, wu_ref[...])

    wspec = pl.BlockSpec((fb, d), lambda f, i: (f, 0))
    ospec = pl.BlockSpec((tm, fb), lambda f, i: (i, f))
    return pl.pallas_call(
        body, name="ffn_a_fwd", grid=(FF_BLOCKS, t // tm),
        in_specs=[pl.BlockSpec((tm, d), lambda f, i: (i, 0)), pl.BlockSpec((1, d), lambda f, i: (0, 0)), wspec, wspec],
        out_specs=[ospec, ospec, pl.BlockSpec((1, tm, d), lambda f, i: (f, i, 0))],
        out_shape=[_sds((t, D_FF)), _sds((t, D_FF)), _sds((FF_BLOCKS, t, d), BF16)], compiler_params=_params(2))(h, norm, wg, wu)


def _ffn_b_fwd(g, u, h, cw, cb, wd, tm, seq):
    t, d = h.shape
    tiles_per_seq = seq // tm

    def body(g_ref, halo_ref, u_ref, h_ref, cw_ref, cb_ref, wd_ref, h_out, pad_ref, act_ref):
        pad_ref[:HALO, :] = jnp.where(pl.program_id(0) % tiles_per_seq == 0, 0.0, halo_ref[...])
        pad_ref[HALO:, :] = g_ref[...]
        cw = cw_ref[...]
        cb = cb_ref[...]
        for c0 in range(0, D_FF, STRIP_LANES):
            cols = slice(c0, min(c0 + STRIP_LANES, D_FF))
            for r in range(0, tm, STRIP):
                taps = _conv_taps(pad_ref, r, cols, FFN_CONV)
                gc = cb[:, cols] + cw[0:1, cols] * taps[0] + cw[1:2, cols] * taps[1] + cw[2:3, cols] * taps[2]
                act_ref[r:r + STRIP, cols] = _bf(_gelu(gc) * u_ref[r:r + STRIP, cols])
        h_out[...] = h_ref[...] + _nn(act_ref[...], wd_ref[...])

    return pl.pallas_call(body, name="ffn_b_fwd", grid=(t // tm,),
                          in_specs=[_row(tm, D_FF), _prev_halo(tm, D_FF), _row(tm, D_FF), _row(tm, d), _const((FFN_CONV, D_FF)),
                                    _const((1, D_FF)), _const((D_FF, d))],
                          out_specs=_row(tm, d), out_shape=_sds((t, d)),
                          scratch_shapes=[pltpu.VMEM((HALO + tm, D_FF), F32), pltpu.VMEM((tm, D_FF), BF16)],
                          compiler_params=_params())(g, g, u, h, cw, cb, wd)


def _ffn_b_bwd(g, u, dout, cw, cb, wd, tm, seq):
    t, d = dout.shape
    fb = D_FF // FF_BLOCKS
    tiles_per_seq = seq // tm

    def body(g_ref, halo_ref, u_ref, dout_ref, cw_ref, cb_ref, wd_ref, dgc_out, du_out, dwd_out, dcw_out, dcb_out,
             pad_ref, dact_ref, act_ref, acc_ref, dwd_acc):
        i = pl.program_id(1)
        first = i == 0
        pad_ref[:HALO, :] = jnp.where(i % tiles_per_seq == 0, 0.0, halo_ref[...])
        pad_ref[HALO:, :] = g_ref[...]
        dout_b = _bf(dout_ref[...])
        dact_ref[...] = _nt(dout_b, wd_ref[...])
        cw = cw_ref[...]
        cb = cb_ref[...]
        fold = lambda a: a[:HALO] + a[HALO:]
        for c0 in range(0, fb, STRIP_LANES):
            cols = slice(c0, min(c0 + STRIP_LANES, fb))
            sums = [jnp.zeros((HALO, cols.stop - c0), F32) for _ in range(1 + FFN_CONV)]
            for r in range(0, tm, STRIP):
                rows = slice(r, r + STRIP)
                taps = _conv_taps(pad_ref, r, cols, FFN_CONV)
                gelu, dgelu = _gelu_and_grad(cb[:, cols] + cw[0:1, cols] * taps[0] + cw[1:2, cols] * taps[1] + cw[2:3, cols] * taps[2])
                u = u_ref[rows, cols]
                dact = dact_ref[rows, cols]
                act_ref[rows, cols] = _bf(gelu * u)
                du_out[rows, cols] = _bf(dact * gelu)
                dgc = dact * u * dgelu
                dgc_out[rows, cols] = dgc
                sums = [sums[0] + fold(dgc)] + [sums[1 + k] + fold(dgc * taps[k]) for k in range(FFN_CONV)]
            for k in range(1 + FFN_CONV):
                acc_ref[k, :, cols] = sums[k]
        _accumulate(dwd_acc, _tn(act_ref[...], dout_b), first)

        @pl.when(i == t // tm - 1)
        def _():
            dwd_out[...] = _bf(dwd_acc[...])

        _accumulate(dcb_out, _colsum(acc_ref[0]), first)
        _accumulate(dcw_out, jnp.concatenate([_colsum(acc_ref[1 + k]) for k in range(FFN_CONV)], axis=0), first)

    blk = pl.BlockSpec((tm, fb), lambda f, i: (i, f))
    halo = pl.BlockSpec((HALO, fb), lambda f, i: (jnp.maximum(i * (tm // HALO) - 1, 0), f))
    wd_blk = pl.BlockSpec((fb, d), lambda f, i: (f, 0), pipeline_mode=pl.Buffered(1))
    return pl.pallas_call(
        body, name="ffn_b_bwd", grid=(FF_BLOCKS, t // tm),
        in_specs=[blk, halo, blk, pl.BlockSpec((tm, d), lambda f, i: (i, 0)), pl.BlockSpec((FFN_CONV, fb), lambda f, i: (0, f)),
                  pl.BlockSpec((1, fb), lambda f, i: (0, f)), wd_blk],
        out_specs=[blk, blk, wd_blk, pl.BlockSpec((FFN_CONV, fb), lambda f, i: (0, f)),
                   pl.BlockSpec((1, fb), lambda f, i: (0, f))],
        out_shape=[_sds((t, D_FF)), _sds((t, D_FF), BF16), _sds((D_FF, d), BF16), _sds((FFN_CONV, D_FF)), _sds((1, D_FF))],
        scratch_shapes=[pltpu.VMEM((HALO + tm, fb), F32), pltpu.VMEM((tm, fb), F32), pltpu.VMEM((tm, fb), BF16),
                        pltpu.VMEM((1 + FFN_CONV, HALO, fb), F32), pltpu.VMEM((fb, d), F32)],
        compiler_params=_params(2))(g, g, u, dout, cw, cb, wd)


def _ffn_a_dgrad(h, norm, dgc, du, dres, cw, wg, wu, tm, seq):
    t, d = h.shape
    tiles_per_seq = seq // tm
    n_tiles = t // tm

    def body(h_ref, gn_ref, dgc_ref, halo_ref, du_ref, dres_ref, cw_ref, wg_ref, wu_ref, dh_out, dg_out, dgn_out, pad_ref):
        i = pl.program_id(0)
        last_in_seq = i % tiles_per_seq == tiles_per_seq - 1
        dg = _bf(_causal_conv_transpose(pad_ref, dgc_ref[...], halo_ref[...], last_in_seq, cw_ref[...], FFN_CONV))
        dg_out[...] = dg
        _, vjp_norm = jax.vjp(_rms, h_ref[...], gn_ref[...])
        dh, dgn = vjp_norm(_nn(dg, wg_ref[...]) + _nn(du_ref[...], wu_ref[...]))
        dh_out[...] = dh + dres_ref[...]
        _accumulate(dgn_out, dgn, i == 0)

    return pl.pallas_call(
        body, name="ffn_a_dgrad", grid=(n_tiles,),
        in_specs=[_row(tm, d), _const((1, d)), _row(tm, D_FF), _next_halo(tm, D_FF, n_tiles), _row(tm, D_FF), _row(tm, d),
                  _const((FFN_CONV, D_FF)), _const((D_FF, d)), _const((D_FF, d))],
        out_specs=[_row(tm, d), _row(tm, D_FF), _const((1, d))], out_shape=[_sds((t, d)), _sds((t, D_FF), BF16), _sds((1, d))],
        scratch_shapes=[pltpu.VMEM((tm + HALO, D_FF), F32)], compiler_params=_params())(h, norm, dgc, dgc, du, dres, cw, wg, wu)


def _ffn_a_wgrad(hn, dg, du, tm):
    _, t, d = hn.shape
    fb = D_FF // FF_BLOCKS

    n_tiles = t // tm

    def body(hn_ref, dg_ref, du_ref, dwg_out, dwu_out, acc_g, acc_u):
        i = pl.program_id(1)
        hn_t = hn_ref[0]
        _accumulate(acc_g, _tn(dg_ref[...], hn_t), i == 0)
        _accumulate(acc_u, _tn(du_ref[...], hn_t), i == 0)

        @pl.when(i == n_tiles - 1)
        def _():
            dwg_out[...] = _bf(acc_g[...])
            dwu_out[...] = _bf(acc_u[...])

    blk = pl.BlockSpec((tm, fb), lambda f, i: (i, f))
    wspec = pl.BlockSpec((fb, d), lambda f, i: (f, 0), pipeline_mode=pl.Buffered(1))
    return pl.pallas_call(body, name="ffn_a_wgrad", grid=(FF_BLOCKS, n_tiles),
                          in_specs=[pl.BlockSpec((1, tm, d), lambda f, i: (0, i, 0)), blk, blk],
                          out_specs=[wspec, wspec], out_shape=[_sds((D_FF, d), BF16), _sds((D_FF, d), BF16)],
                          scratch_shapes=[pltpu.VMEM((fb, d), F32), pltpu.VMEM((fb, d), F32)],
                          compiler_params=_params(2))(hn, dg, du)


def _sgu_mix(vn, ws_ref, bst):
    tril = lax.broadcasted_iota(jnp.int32, (CHUNK, CHUNK), 0) >= lax.broadcasted_iota(jnp.int32, (CHUNK, CHUNK), 1)
    wms = [jnp.where(tril, ws_ref[g], 0.0) for g in range(SGU_GROUPS)]
    chunks = []
    for n in range(vn.shape[0] // CHUNK):
        vc = vn[n * CHUNK:(n + 1) * CHUNK, :]
        chunks.append(jnp.concatenate(
            [_nn(wms[g], vc[:, g * CHUNK:(g + 1) * CHUNK]) + bst[:, g:g + 1] for g in range(SGU_GROUPS)], axis=1))
    return jnp.concatenate(chunks, axis=0)


def _sgu_fwd(h, w, tm):
    t, d = h.shape

    def body(h_ref, cn_ref, win_ref, lg_ref, lb_ref, ws_ref, bst_ref, wout_ref, h_out):
        h_t = h_ref[...]
        z = _gelu(_nn(_rms(h_t, cn_ref[...]), win_ref[...]))
        vn = _layer_norm(z[:, d:], lg_ref[...], lb_ref[...])
        s = _sgu_mix(vn, ws_ref, bst_ref[...])
        h_out[...] = h_t + _nn(z[:, :d] * s, wout_ref[...])

    return pl.pallas_call(
        body, name="sgu_fwd", grid=(t // tm,),
        in_specs=[_row(tm, d), _const((1, d)), _const((d, 2 * d)), _const((1, d)), _const((1, d)), _const((SGU_GROUPS, CHUNK, CHUNK)),
                  _const((CHUNK, LANES)), _const((d, d))],
        out_specs=_row(tm, d), out_shape=_sds((t, d)), compiler_params=_params(),
    )(h, w['c_norm'], w['c_w_in'], w['c_ln_g'], w['c_ln_b'], w['c_w_s'], w['bsT'], w['c_w_out'])


def _sgu_bwd(h, dout, w, tm):
    t, d = h.shape

    def body(h_ref, dout_ref, cn_ref, win_ref, lg_ref, lb_ref, ws_ref, bst_ref, wout_ref,
             dh_out, dcn_out, dwin_out, dlg_out, dlb_out, dws_out, dbst_out, dwout_out):
        first = pl.program_id(0) == 0
        hn, vjp_norm = jax.vjp(_rms, h_ref[...], cn_ref[...])
        zpre = _nn(hn, win_ref[...])
        u, vjp_u = jax.vjp(_gelu, zpre[:, :d])
        vn, vjp_v = jax.vjp(lambda zp, lg, lb: _layer_norm(_gelu(zp), lg, lb), zpre[:, d:], lg_ref[...], lb_ref[...])
        s = _sgu_mix(vn, ws_ref, bst_ref[...])
        dout_t = dout_ref[...]
        dus = _nt(dout_t, wout_ref[...])
        _accumulate(dwout_out, _tn(u * s, dout_t), first)
        ds = dus * u
        tril = lax.broadcasted_iota(jnp.int32, (CHUNK, CHUNK), 0) >= lax.broadcasted_iota(jnp.int32, (CHUNK, CHUNK), 1)
        lane = lax.broadcasted_iota(jnp.int32, (CHUNK, LANES), 1)
        dws = [jnp.zeros((CHUNK, CHUNK), F32) for _ in range(SGU_GROUPS)]
        dbst = jnp.zeros((CHUNK, LANES), F32)
        dvn_chunks = []
        for n in range(tm // CHUNK):
            cols = []
            for g in range(SGU_GROUPS):
                ds_ng = ds[n * CHUNK:(n + 1) * CHUNK, g * CHUNK:(g + 1) * CHUNK]
                vc_ng = vn[n * CHUNK:(n + 1) * CHUNK, g * CHUNK:(g + 1) * CHUNK]
                cols.append(_tn(jnp.where(tril, ws_ref[g], 0.0), ds_ng))
                dws[g] = dws[g] + _nt(ds_ng, vc_ng)
                dbst = dbst + jnp.where(lane == g, jnp.sum(ds_ng, axis=1, keepdims=True), 0.0)
            dvn_chunks.append(jnp.concatenate(cols, axis=1))
        dvn = jnp.concatenate(dvn_chunks, axis=0)
        for g in range(SGU_GROUPS):
            val = jnp.where(tril, dws[g], 0.0)

            @pl.when(first)
            def _():
                dws_out[g] = val

            @pl.when(jnp.logical_not(first))
            def _():
                dws_out[g] += val
        _accumulate(dbst_out, dbst, first)
        (dzu,) = vjp_u(dus * s)
        dzv, dlg, dlb = vjp_v(dvn)
        _accumulate(dlg_out, dlg, first)
        _accumulate(dlb_out, dlb, first)
        dzpre = jnp.concatenate([dzu, dzv], axis=1)
        _accumulate(dwin_out, _tn(hn, dzpre), first)
        dh, dcn = vjp_norm(_nt(dzpre, win_ref[...]))
        _accumulate(dcn_out, dcn, first)
        dh_out[...] = dh + dout_t

    return pl.pallas_call(
        body, name="sgu_bwd", grid=(t // tm,),
        in_specs=[_row(tm, d), _row(tm, d), _const((1, d)), _const((d, 2 * d)), _const((1, d)), _const((1, d)),
                  _const((SGU_GROUPS, CHUNK, CHUNK)), _const((CHUNK, LANES)), _const((d, d))],
        out_specs=[_row(tm, d), _const((1, d)), _const((d, 2 * d)), _const((1, d)), _const((1, d)), _const((SGU_GROUPS, CHUNK, CHUNK)),
                   _const((CHUNK, LANES)), _const((d, d))],
        out_shape=[_sds((t, d)), _sds((1, d)), _sds((d, 2 * d)), _sds((1, d)), _sds((1, d)), _sds((SGU_GROUPS, CHUNK, CHUNK)),
                   _sds((CHUNK, LANES)), _sds((d, d))],
        compiler_params=_params(),
    )(h, dout, w['c_norm'], w['c_w_in'], w['c_ln_g'], w['c_ln_b'], w['c_w_s'], w['bsT'], w['c_w_out'])


def _final_loss(h, target, norm, tm):
    t, d = h.shape

    def body(h_ref, tgt_ref, gn_ref, dh_out, loss_out, dgn_out):
        first = pl.program_id(0) == 0
        tgt = tgt_ref[...]

        def loss_fn(h_, g_):
            err = _rms(h_, g_) - tgt
            return 0.5 * jnp.sum(jnp.mean(err * err, axis=-1, keepdims=True), axis=0, keepdims=True)

        loss, vjp_loss = jax.vjp(loss_fn, h_ref[...], gn_ref[...])
        dh, dgn = vjp_loss(jnp.ones((1, 1), F32))
        dh_out[...] = dh
        _accumulate(loss_out, loss, first)
        _accumulate(dgn_out, dgn, first)

    return pl.pallas_call(body, name="final_loss", grid=(t // tm,), in_specs=[_row(tm, d), _row(tm, d), _const((1, d))],
                          out_specs=[_row(tm, d), _const((1, 1)), _const((1, d))],
                          out_shape=[_sds((t, d)), _sds((1, 1)), _sds((1, d))], compiler_params=_params())(h, target, norm)


def _tile(t, seq, want):
    tm = min(want, seq)
    assert seq % tm == 0 and t % tm == 0 and tm % CHUNK == 0
    return tm


def _local_step(x, posb, target, w, seq, late_weights, on_grads):
    t, d = x.shape
    b = t // seq
    hp = HEADS * HEAD_PAD
    tm_big, tm_mid = _tile(t, seq, 512), _tile(t, seq, 256)
    tq = _tile(t, seq, 512)

    q, k, v, xl, gate = _ab_in_fwd(x, posb, w, tm_big)
    o, probs = _attn_fwd(q.reshape(b, seq, hp), k.reshape(b, seq, hp), v.reshape(b, seq, hp), tq)
    o = o.reshape(t, hp)
    y, hs = _lru_fwd(xl, gate, w, tm_big, seq)
    w = {**w, **late_weights('out0', y)}
    h1 = _ab_out_fwd(x, o, y, w, tm_big)
    hcur = h1
    saved = []
    for l in range(2):
        if l == 1:
            w = {**w, **late_weights('mix1', hcur)}
            saved_h2 = hcur
            hcur = _sgu_fwd(hcur, w, tm_mid)
        wl = late_weights('ffn%d' % l, hcur)
        g, u, hn = _ffn_a_fwd(hcur, w['ffn_norm'][l], wl['Wg'], wl['Wu'], tm_big)
        hnext = _ffn_b_fwd(g, u, hcur, w['ffn_conv_w'][l], w['ffn_conv_b'][l], wl['Wd'], tm_mid, seq)
        saved.append((hcur, g, u, wl, hn))
        hcur = hnext
    dh, loss, d_final = _final_loss(hcur, target, w['final_norm'], tm_big)

    ffn = {}
    conv_b = list(w['ffn_conv_b'])
    for l in (1, 0):
        hin, g, u, wl, hn = saved[l]
        dgc, du, d_wd, d_cw, d_cb = _ffn_b_bwd(g, u, dh, w['ffn_conv_w'][l], conv_b[l], wl['Wd'], tm_big, seq)
        dh, dg, d_norm = _ffn_a_dgrad(hin, w['ffn_norm'][l], dgc, du, dh, w['ffn_conv_w'][l], wl['Wg'], wl['Wu'], tm_mid, seq)
        d_wg, d_wu = _ffn_a_wgrad(hn, dg, du, _tile(t, seq, 1024))
        ffn[l] = dict(ffn_norm=d_norm, ffn_conv_w=d_cw, ffn_conv_b=d_cb, Wg=d_wg, Wu=d_wu, Wd=d_wd)
        if l == 1:
            dh, d_cn, d_cwin, d_lg, d_lb, d_ws, d_bst, d_cwout = _sgu_bwd(saved_h2, dh, w, tm_mid)
            zero = on_grads('late1', dict(final_norm=d_final, c_norm=d_cn, c_ln_g=d_lg, c_ln_b=d_lb, c_w_s=d_ws, bsT=d_bst, c_w_in=d_cwin,
                                          c_w_out=d_cwout, Wg=[d_wg], Wu=[d_wu], Wd=[d_wd]))
            conv_b[0] = conv_b[0] + zero
    late0 = {name: [ffn[0][name], ffn[1][name]] for name in ('ffn_norm', 'ffn_conv_w', 'ffn_conv_b')}
    zero = on_grads('late0', dict(late0, Wg=[ffn[0]['Wg']], Wu=[ffn[0]['Wu']], Wd=[ffn[0]['Wd']]))
    w = {**w, 'Wo_b': w['Wo_b'] + zero.astype(w['Wo_b'].dtype)}
    do, dy, d_woa, d_wob = _ab_out_bwd(o, y, dh, w, tm_big)
    dxl, dgate, d_cw, d_cb, d_wa, d_ba, d_wx, d_bx, d_lam = _lru_bwd(xl, gate, hs, dy, w, tm_big, seq)
    zero = on_grads('mid', dict(Wo_a=d_woa, Wo_b=d_wob, ab_conv_w=d_cw, ab_conv_b=d_cb, Wa=d_wa, ab_b_rg_a=d_ba, Wx=d_wx,
                                ab_b_rg_x=d_bx, ab_lambda=d_lam))
    w = {**w, 'ab_norm': w['ab_norm'] + zero}
    dq, dk, dv = _attn_bwd(q.reshape(b, seq, hp), k.reshape(b, seq, hp), v.reshape(b, seq, hp), probs, do.reshape(b, seq, hp), tq)
    dx, d_gn, d_win, d_qn, d_wq, d_kvn, d_wk, d_wv = _ab_in_bwd(
        x, posb, w, dq.reshape(t, hp), dk.reshape(t, hp), dv.reshape(t, hp), dxl, dgate, dh, tm_mid)
    return loss, dx, dict(ab_norm=d_gn, W_in=d_win, ab_q_norm=d_qn, Wq=d_wq, ab_kv_norm=d_kvn, Wk=d_wk, Wv=d_wv)


def _block_diag(wg):
    g, n, _ = wg.shape
    return jnp.einsum('gij,gh->gihj', wg, jnp.eye(g, dtype=wg.dtype)).reshape(g * n, g * n)


def _prepare_out(w_out):
    d = w_out.shape[2]
    mla = HEADS * QK_NOPE
    return {'Wo_a': jnp.pad(w_out[0, :mla].reshape(HEADS, QK_NOPE, d), ((0, 0), (0, HEAD_PAD - QK_NOPE), (0, 0))).reshape(HEADS * HEAD_PAD, d),
            'Wo_b': w_out[0, mla:]}


def _prepare(full):
    d = full['ab_w_in'].shape[1]
    w_in = full['ab_w_in'][0]
    zeros = lambda n: jnp.zeros((d, n), w_in.dtype)
    wq = full['ab_w_q_b'][0].reshape(Q_LORA, HEADS, QK_NOPE + QK_ROPE)
    wkv = full['ab_w_kv_b'][0].reshape(KV_LORA, HEADS, 2 * QK_NOPE)
    pad_head = lambda a: jnp.pad(a, ((0, 0), (0, 0), (0, HEAD_PAD - a.shape[2]))).reshape(a.shape[0], HEADS * HEAD_PAD)
    w = {
        'W_in': jnp.concatenate([w_in[:, :Z_KPE], zeros(QK_NOPE), w_in[:, Z_KPE:Z_KPE + QK_ROPE],
                                 zeros(HEAD_PAD - QK_NOPE - QK_ROPE), w_in[:, Z_KPE + QK_ROPE:]], axis=1),
        'Wq': pad_head(wq), 'Wk': pad_head(wkv[:, :, :QK_NOPE]), 'Wv': pad_head(wkv[:, :, QK_NOPE:]),
        'Wa': _bf(_block_diag(full['ab_w_rg_a'][0])), 'Wx': _bf(_block_diag(full['ab_w_rg_x'][0])),
        'c_w_s': full['c_w_s'][0],
        'bsT': jnp.pad(full['c_b_s'][0].T, ((0, 0), (0, LANES - SGU_GROUPS))),
        'ffn_norm': [full['ffn_norm'][l:l + 1] for l in range(2)], 'ffn_conv_w': [full['ffn_conv_w'][l] for l in range(2)],
        'ffn_conv_b': [full['ffn_conv_b'][l:l + 1] for l in range(2)],
        'ab_conv_w': full['ab_conv_w'][0], 'final_norm': full['final_norm'][None, :],
    }
    for name in ('ab_norm', 'ab_q_norm', 'ab_kv_norm', 'ab_conv_b', 'ab_b_rg_a', 'ab_b_rg_x', 'ab_lambda', 'c_norm', 'c_ln_g', 'c_ln_b'):
        w[name] = full[name]
    return w


def _unprepare(g):
    unpad_head = lambda a, n: a.reshape(a.shape[0], HEADS, HEAD_PAD)[:, :, :n]
    diag = lambda a: jnp.einsum('gigj->gij', a.reshape(HEADS, LRU_W // HEADS, HEADS, LRU_W // HEADS))
    rules = {
        'ab_w_in': (('W_in',), lambda a: jnp.concatenate([a[:, :Z_KPE], a[:, Z_KPE + QK_NOPE:Z_KPE + QK_NOPE + QK_ROPE], a[:, Z_LRU:]], axis=1)[None]),
        'ab_w_q_b': (('Wq',), lambda a: unpad_head(a, QK_NOPE + QK_ROPE).reshape(1, Q_LORA, -1)),
        'ab_w_kv_b': (('Wk', 'Wv'), lambda a, b: jnp.concatenate([unpad_head(a, QK_NOPE), unpad_head(b, QK_NOPE)], axis=2).reshape(1, KV_LORA, -1)),
        'ab_w_out': (('Wo_a', 'Wo_b'), lambda a, b: jnp.concatenate(
            [a.reshape(HEADS, HEAD_PAD, -1)[:, :QK_NOPE].reshape(HEADS * QK_NOPE, -1), b], axis=0)[None]),
        'ab_w_rg_a': (('Wa',), lambda a: diag(a)[None]), 'ab_w_rg_x': (('Wx',), lambda a: diag(a)[None]),
        'c_w_in': (('c_w_in',), lambda a: a[None]), 'c_w_out': (('c_w_out',), lambda a: a[None]), 'c_w_s': (('c_w_s',), lambda a: a[None]),
        'c_b_s': (('bsT',), lambda a: a[:, :SGU_GROUPS].T[None]),
        'ffn_w_gate': (('Wg',), jnp.stack), 'ffn_w_up': (('Wu',), jnp.stack), 'ffn_w_down': (('Wd',), jnp.stack),
        'ffn_norm': (('ffn_norm',), lambda a: jnp.concatenate(a, axis=0)), 'ffn_conv_w': (('ffn_conv_w',), jnp.stack),
        'ffn_conv_b': (('ffn_conv_b',), lambda a: jnp.concatenate(a, axis=0)),
        'ab_conv_w': (('ab_conv_w',), lambda a: a[None]), 'final_norm': (('final_norm',), lambda a: a[0]),
    }
    for name in ('ab_norm', 'ab_q_norm', 'ab_kv_norm', 'ab_conv_b', 'ab_b_rg_a', 'ab_b_rg_x', 'ab_lambda', 'c_norm', 'c_ln_g', 'c_ln_b'):
        rules[name] = ((name,), lambda a: a)
    return {name: fn(*[g[k] for k in keys]) for name, (keys, fn) in rules.items() if all(k in g for k in keys)}


SLAB_ROWS = 16


def _round_up(n, m):
    return -(-n // m) * m


def _to_chunks(full, axis):
    s = full.shape
    return jnp.moveaxis(full.reshape(s[:axis] + (N_DEV, s[axis] // N_DEV) + s[axis + 1:]), axis, 0)


def _from_chunks(chunks, axis):
    local = chunks.shape[1:]
    return jnp.moveaxis(chunks, 0, axis).reshape(local[:axis] + (N_DEV * local[axis],) + local[axis + 1:])


def _merge_columns(landed, name):
    _, _, r, n = landed.shape
    tr = r // 4

    def body(l_ref, o_ref):
        o_ref[0] = jnp.concatenate([l_ref[dev, 0] for dev in range(N_DEV)], axis=1)

    return pl.pallas_call(body, name="merge_" + name, grid=(r // tr,),
                          in_specs=[pl.BlockSpec((N_DEV, 1, tr, n), lambda i: (0, 0, i, 0))],
                          out_specs=pl.BlockSpec((1, tr, N_DEV * n), lambda i: (0, i, 0)),
                          out_shape=jax.ShapeDtypeStruct((1, r, N_DEV * n), landed.dtype), compiler_params=_params())(landed)


def _split_chunks(whole, axis, name):
    _, rows, cols = whole.shape
    if axis == 1:
        r = rows // N_DEV

        def body(x_ref, o_ref):
            o_ref[0] = _bf(x_ref[...])

        grid, out_shape = (N_DEV,), (N_DEV, 1, r, cols)
        spec, out_spec = pl.BlockSpec((1, r, cols), lambda dev: (0, dev, 0)), pl.BlockSpec((1, 1, r, cols), lambda dev: (dev, 0, 0, 0))
    else:
        n, tr = cols // N_DEV, rows // 4

        def body(x_ref, o_ref):
            x = x_ref[0]
            for dev in range(N_DEV):
                o_ref[dev, 0] = _bf(x[:, dev * n:(dev + 1) * n])

        grid, out_shape = (rows // tr,), (N_DEV, 1, rows, n)
        spec, out_spec = pl.BlockSpec((1, tr, cols), lambda i: (0, i, 0)), pl.BlockSpec((N_DEV, 1, tr, n), lambda i: (0, 0, i, 0))
    return pl.pallas_call(body, name="split_" + name, grid=grid, in_specs=[spec], out_specs=out_spec,
                          out_shape=jax.ShapeDtypeStruct(out_shape, BF16), compiler_params=_params())(whole)


def _slab_rows(n):
    return _round_up(-(-n // LANES), SLAB_ROWS)


def _to_slab(a, lead):
    a = a.reshape(lead + (-1,))
    rows = _slab_rows(a.shape[-1])
    a = jnp.pad(a, [(0, 0)] * len(lead) + [(0, rows * LANES - a.shape[-1])])
    return a.reshape(lead + (rows, LANES))


def _pack_slabs(parts, lead):
    return jnp.concatenate([_to_slab(p, lead) for p in parts], axis=len(lead))


def _unpack_slabs(packed, shapes):
    lead = packed.shape[:-2]
    out, row = [], 0
    for shape in shapes:
        size = math.prod(shape)
        rows = _slab_rows(size)
        piece = lax.slice_in_dim(packed, row, row + rows, axis=len(lead))
        out.append(piece.reshape(lead + (rows * LANES,))[..., :size].reshape(lead + tuple(shape)))
        row += rows
    return out


HBM = pl.BlockSpec(memory_space=pl.ANY)


def _other_chips(x, y):
    return [(1 - x, y), (x, 1 - y), (1 - x, 1 - y)]


def _all_gather(blocks):
    n = len(blocks)

    def body(*refs):
        x_refs, out_refs, token = refs[:n], refs[n:2 * n], refs[2 * n]
        send_sems, recv_sems, local_sems = refs[2 * n + 1:]
        token[...] = jnp.zeros_like(token)
        x, y, c = lax.axis_index("x"), lax.axis_index("y"), lax.axis_index("c")
        me, sibling = (x, y, c), (x, y, 1 - c)
        chips = _other_chips(x, y)

        def slab(a, px, py, pc):
            return out_refs[a].at[4 * px + 2 * py + pc]

        def copy(a, k, blk, to, src=None):
            return pltpu.make_async_remote_copy(src_ref=slab(a, *blk) if src is None else src, dst_ref=slab(a, *blk),
                                                send_sem=send_sems.at[7 * a + k], recv_sem=recv_sems.at[7 * a + k],
                                                device_id=to, device_id_type=MESH)

        mine = [pltpu.make_async_copy(x_refs[a], slab(a, *me), local_sems.at[a]) for a in range(n)]
        started = []
        for a in range(n):
            mine[a].start()
            started.append(copy(a, 0, me, sibling, src=x_refs[a]))
            started += [copy(a, 1 + j, me, (*chip, c), src=x_refs[a]) for j, chip in enumerate(chips)]
        for cp in started:
            cp.start()
        for j, chip in enumerate(chips):
            for a in range(n):
                copy(a, 1 + j, (*chip, c), me).wait_recv()
                passed = copy(a, 4 + j, (*chip, c), sibling)
                passed.start()
                started.append(passed)
        for a in range(n):
            copy(a, 0, sibling, me).wait_recv()
        for j, chip in enumerate(chips):
            for a in range(n):
                copy(a, 4 + j, (*chip, 1 - c), me).wait_recv()
        for cp in started:
            cp.wait_send()
        for a in range(n):
            mine[a].wait()

    out = pl.pallas_call(
        body, name="all_gather_weights",
        out_shape=[jax.ShapeDtypeStruct((N_DEV,) + b.shape, b.dtype) for b in blocks] + [jax.ShapeDtypeStruct((8, LANES), F32)],
        in_specs=[HBM] * n, out_specs=[HBM] * n + [pl.BlockSpec(memory_space=pltpu.VMEM)],
        scratch_shapes=[pltpu.SemaphoreType.DMA((7 * n,)), pltpu.SemaphoreType.DMA((7 * n,)), pltpu.SemaphoreType.DMA((n,))],
    )(*blocks)
    return list(out[:n]), out[n][0, 0]


FLIPS = [(0, 0, 1), (1, 0, 0), (1, 0, 1), (0, 1, 0), (0, 1, 1), (1, 1, 0), (1, 1, 1)]


def _peers(x, y, c):
    flip = lambda v, f: 1 - v if f else v
    return [(flip(x, fx), flip(y, fy), flip(c, fc)) for fx, fy, fc in FLIPS]


def _direct_copies(src_refs, land_refs, send_sems, recv_sems, scatter):
    x, y, c = lax.axis_index("x"), lax.axis_index("y"), lax.axis_index("c")
    me = 4 * x + 2 * y + c
    starts, waits = [], []
    for a in range(len(src_refs)):
        for k, (px, py, pc) in enumerate(_peers(x, y, c)):
            peer = 4 * px + 2 * py + pc
            sems = dict(send_sem=send_sems.at[7 * a + k], recv_sem=recv_sems.at[7 * a + k], device_id=(px, py, pc), device_id_type=MESH)
            src = src_refs[a].at[peer] if scatter else src_refs[a]
            starts.append(pltpu.make_async_remote_copy(src_ref=src, dst_ref=land_refs[a].at[me], **sems))
            waits.append(pltpu.make_async_remote_copy(src_ref=src, dst_ref=land_refs[a].at[peer], **sems))
    n = len(src_refs)
    keeps = [] if scatter else [pltpu.make_async_copy(src_refs[a], land_refs[a].at[me], send_sems.at[7 * n + a]) for a in range(n)]
    return starts, waits, keeps


def _landing(src, scatter):
    block = src.shape[1:] if scatter else src.shape
    return jax.ShapeDtypeStruct((N_DEV,) + block, src.dtype)


HBM_SPACE = pl.BlockSpec(memory_space=pltpu.HBM)
SEMAPHORES = pl.BlockSpec(memory_space=pltpu.SEMAPHORE)
SPLIT_EFFECT = pltpu.SideEffectType.DATAFLOW_SIDE_EFFECTING


def _start_exchange(name, srcs, scatter):
    n = len(srcs)
    lands = [lax.empty(s.shape, s.dtype) for s in (_landing(s, scatter) for s in srcs)]

    def body(*refs):
        starts, _, keeps = _direct_copies(refs[:n], refs[n:2 * n], refs[2 * n], refs[2 * n + 1], scatter)
        for cp in starts + keeps:
            cp.start()
        refs[-1][...] = jnp.zeros_like(refs[-1])

    held = [pltpu.with_memory_space_constraint(a, pltpu.HBM) for a in list(srcs) + lands]
    out = pl.pallas_call(
        body, name=name + "_start",
        out_shape=(pltpu.SemaphoreType.DMA(((7 if scatter else 8) * n,)), pltpu.SemaphoreType.DMA((7 * n,)),
                   *[pltpu.HBM(a.shape, a.dtype) for a in held],
                   jax.ShapeDtypeStruct((8, LANES), F32)),
        in_specs=[HBM_SPACE] * (2 * n), out_specs=(SEMAPHORES, SEMAPHORES, *[HBM_SPACE] * (2 * n), pl.BlockSpec(memory_space=pltpu.VMEM)),
        input_output_aliases={i: 2 + i for i in range(2 * n)},
        compiler_params=pltpu.CompilerParams(has_side_effects=SPLIT_EFFECT),
    )(*held)
    return out[0], out[1], list(out[2:2 + n]), list(out[2 + n:2 + 2 * n]), out[-1][0, 0], out[-1]


def _wait_exchange(name, started, after, scatter):
    send_sems, recv_sems, srcs, lands = started[:4]
    n = len(srcs)

    def body(*refs):
        _, waits, keeps = _direct_copies(refs[:n], refs[n:2 * n], refs[2 * n], refs[2 * n + 1], scatter)
        for cp in waits:
            cp.wait_send()
        for cp in waits:
            cp.wait_recv()
        for cp in keeps:
            cp.wait()

    out = pl.pallas_call(
        body, name=name + "_wait", out_shape=tuple(pltpu.HBM(a.shape, a.dtype) for a in srcs + lands),
        in_specs=[HBM_SPACE] * (2 * n) + [SEMAPHORES, SEMAPHORES, HBM], out_specs=tuple([HBM_SPACE] * (2 * n)),
        input_output_aliases={i: i for i in range(2 * n)},
        compiler_params=pltpu.CompilerParams(has_side_effects=SPLIT_EFFECT),
    )(*srcs, *lands, send_sems, recv_sems, after)
    return list(out[:n]), list(out[n:])


def _row_tile(rows):
    return rows // 2 if (rows // 2) % SLAB_ROWS == 0 else rows


def _sum_in_device_order(me_ref, l_ref, own_ref):
    mine = own_ref[0].astype(F32)
    g = jnp.where(me_ref[0] == 0, mine, l_ref[0].astype(F32))
    for dev in range(1, N_DEV):
        g = g + jnp.where(me_ref[0] == dev, mine, l_ref[dev].astype(F32))
    return g


def _adamw(g, w, m, v):
    m_new = ADAM_B1 * m + (1.0 - ADAM_B1) * g
    v_new = ADAM_B2 * v + (1.0 - ADAM_B2) * (g * g)
    m_hat = m_new * (1.0 / (1.0 - ADAM_B1 ** ADAM_STEP))
    v_hat = v_new * (1.0 / (1.0 - ADAM_B2 ** ADAM_STEP))
    return -ADAM_LR * (m_hat / (jnp.sqrt(v_hat) + ADAM_EPS) + ADAM_WD * w), m_new, v_new


def _sum_chunks(me, landed, own, name):
    _, _, r, n = landed.shape

    def body(me_ref, l_ref, own_ref, g_out):
        g_out[...] = _sum_in_device_order(me_ref, l_ref, own_ref)[0]

    return pl.pallas_call(
        body, name="sum_" + name,
        grid_spec=pltpu.PrefetchScalarGridSpec(
            num_scalar_prefetch=1, grid=(1,),
            in_specs=[pl.BlockSpec((N_DEV, 1, r, n), lambda i, me_ref: (0, 0, 0, 0)),
                      pl.BlockSpec((1, 1, r, n), lambda i, me_ref: (me_ref[0], 0, 0, 0))],
            out_specs=pl.BlockSpec((r, n), lambda i, me_ref: (0, 0))),
        out_shape=_sds((r, n)), compiler_params=_params())(me, landed, own)


def _adamw_small(gs, ws, ms, vs):
    n = len(gs)

    def body(*refs):
        ins, outs = refs[:4 * n], refs[4 * n:]
        for i in range(n):
            outs[i][...], outs[n + i][...], outs[2 * n + i][...] = _adamw(*[ins[k * n + i][...] for k in range(4)])

    out = pl.pallas_call(body, name="adamw_small", out_shape=[_sds(w.shape) for w in ws] * 3)(*gs, *ws, *ms, *vs)
    return out[:n], out[n:2 * n], out[2 * n:]


def _sum_and_adamw(me, landed, own, wts, m, v, name, layer=None, into=None):
    layers, r, n = wts.shape
    first = 0 if layer is None else layer
    count = layers if layer is None else 1
    tr = _row_tile(r)
    blk = pl.BlockSpec((1, tr, n), lambda li, ri, me_ref: (first + li, ri, 0))
    held = [] if into is None else list(into)

    def body(me_ref, l_ref, own_ref, w_ref, m_ref, v_ref, *rest):
        g_out, d_out, m_out, v_out = rest[len(held):]
        g = _sum_in_device_order(me_ref, l_ref, own_ref)
        g_out[...] = g
        d_out[...], m_out[...], v_out[...] = _adamw(g, w_ref[...], m_ref[...], v_ref[...])

    return pl.pallas_call(
        body, name="adamw_" + name,
        grid_spec=pltpu.PrefetchScalarGridSpec(
            num_scalar_prefetch=1, grid=(count, r // tr),
            in_specs=[pl.BlockSpec((N_DEV, 1, tr, n), lambda li, ri, me_ref: (0, li, ri, 0)),
                      pl.BlockSpec((1, 1, tr, n), lambda li, ri, me_ref: (me_ref[0], li, ri, 0)), blk, blk, blk] + [HBM] * len(held),
            out_specs=[blk] * 4),
        out_shape=[_sds((layers, r, n))] * 4, input_output_aliases={6 + i: i for i in range(len(held))},
        compiler_params=_params(2))(me, landed, own, wts, m, v, *held)


EARLY = ['ab_w_in']
LATE_STAGES = {
    'out0': [('ab_w_out', None, 'ab_w_out')],
    'ffn0': [('ffn_w_gate', 0, 'Wg'), ('ffn_w_up', 0, 'Wu'), ('ffn_w_down', 0, 'Wd')],
    'mix1': [('c_w_in', None, 'c_w_in'), ('c_w_out', None, 'c_w_out')],
    'ffn1': [('ffn_w_gate', 1, 'Wg'), ('ffn_w_up', 1, 'Wu'), ('ffn_w_down', 1, 'Wd')],
}
TRANSPOSED = ('ffn_w_gate', 'ffn_w_up')


def _stored(name, a):
    return jnp.swapaxes(a, 1, 2) if name in TRANSPOSED else a


def _stored_axis(name):
    return 1 if name in TRANSPOSED else SHARD_AXIS[name]


GRAD_STAGES = {
    'late1': ([('c_w_in', None), ('c_w_out', None), ('ffn_w_gate', 1), ('ffn_w_up', 1), ('ffn_w_down', 1)],
              ['c_norm', 'c_ln_g', 'c_ln_b', 'c_w_s', 'c_b_s', 'final_norm']),
    'late0': ([('ffn_w_gate', 0), ('ffn_w_up', 0), ('ffn_w_down', 0)], ['ffn_norm', 'ffn_conv_w', 'ffn_conv_b']),
    'mid': ([('ab_w_out', None)], ['ab_conv_w', 'ab_conv_b', 'ab_w_rg_a', 'ab_b_rg_a', 'ab_w_rg_x', 'ab_b_rg_x', 'ab_lambda']),
    'last': ([('ab_w_in', None)], ['ab_norm', 'ab_q_norm', 'ab_w_q_b', 'ab_kv_norm', 'ab_w_kv_b']),
}


def _gather_early(local):
    small = [_bf(local[n]) if n in MATRICES else lax.bitcast_convert_type(local[n], BF16) for n in SMALL_SHARDED]
    gathered, zero = _all_gather([_bf(local[n]) for n in EARLY] + [_pack_slabs(small, ())])
    full = {n: local[n] for n in REPLICATED}
    for n, g in zip(EARLY, gathered):
        full[n] = _from_chunks(g, SHARD_AXIS[n])
    for n, p in zip(SMALL_SHARDED, _unpack_slabs(gathered[-1], [s.shape for s in small])):
        full[n] = _from_chunks(p if n in MATRICES else lax.bitcast_convert_type(p, F32), SHARD_AXIS[n])
    return full, zero


def kernel(x, positions, ab_norm, ab_w_in, ab_q_norm, ab_w_q_b, ab_kv_norm, ab_w_kv_b, ab_conv_w, ab_conv_b, ab_w_rg_a, ab_b_rg_a, ab_w_rg_x, ab_b_rg_x, ab_lambda, ab_w_out, c_norm, c_w_in, c_ln_g, c_ln_b, c_w_s, c_b_s, c_w_out, ffn_norm, ffn_w_gate, ffn_w_up, ffn_conv_w, ffn_conv_b, ffn_w_down, final_norm, loss_target, m_ab_norm, m_ab_w_in, m_ab_q_norm, m_ab_w_q_b, m_ab_kv_norm, m_ab_w_kv_b, m_ab_conv_w, m_ab_conv_b, m_ab_w_rg_a, m_ab_b_rg_a, m_ab_w_rg_x, m_ab_b_rg_x, m_ab_lambda, m_ab_w_out, m_c_norm, m_c_w_in, m_c_ln_g, m_c_ln_b, m_c_w_s, m_c_b_s, m_c_w_out, m_ffn_norm, m_ffn_w_gate, m_ffn_w_up, m_ffn_conv_w, m_ffn_conv_b, m_ffn_w_down, m_final_norm, v_ab_norm, v_ab_w_in, v_ab_q_norm, v_ab_w_q_b, v_ab_kv_norm, v_ab_w_kv_b, v_ab_conv_w, v_ab_conv_b, v_ab_w_rg_a, v_ab_b_rg_a, v_ab_w_rg_x, v_ab_b_rg_x, v_ab_lambda, v_ab_w_out, v_c_norm, v_c_w_in, v_c_ln_g, v_c_ln_b, v_c_w_s, v_c_b_s, v_c_w_out, v_ffn_norm, v_ffn_w_gate, v_ffn_w_up, v_ffn_conv_w, v_ffn_conv_b, v_ffn_w_down, v_final_norm):
    given = dict(locals())
    local = {n: given[n] for n in WEIGHTS}
    b, seq, d = x.shape
    t = b * seq

    me = (4 * lax.axis_index("x") + 2 * lax.axis_index("y") + lax.axis_index("c")).astype(jnp.int32)
    me1 = me.reshape(1)

    full, zero = _gather_early(local)
    gathers = {}
    for stage, members in LATE_STAGES.items():
        srcs = [_bf(_stored(n, local[n] if layer is None else local[n][layer:layer + 1]) + zero) for n, layer, _ in members]
        gathers[stage] = _start_exchange('gather_' + stage, srcs, scatter=False)
        zero = gathers[stage][4]
    w = _prepare(full)
    w['ab_norm'] = w['ab_norm'] + zero

    def late_weights(stage, after):
        _, lands = _wait_exchange('gather_' + stage, gathers[stage], after, scatter=False)
        whole = [l.reshape(1, -1, l.shape[-1]) if _stored_axis(n) == 1 else _merge_columns(l, n)
                 for (n, _, _), l in zip(LATE_STAGES[stage], lands)]
        if stage == 'out0':
            return _prepare_out(whole[0])
        return {key: a[0] for (_, _, key), a in zip(LATE_STAGES[stage], whole)}

    scatters = {}

    def start_scatter(stage, g):
        whole = _unprepare(g)
        big, small = GRAD_STAGES[stage]
        slab = [_to_chunks(whole[n], SHARD_AXIS[n]) if n in SHARD_AXIS else jnp.broadcast_to(whole[n][None], (N_DEV,) + whole[n].shape)
                for n in small]
        own = [whole[n].reshape(N_DEV, 1, whole[n].shape[1] // N_DEV, whole[n].shape[2])
               if whole[n].dtype == BF16 and _stored_axis(n) == 1 else
               _split_chunks(whole[n], _stored_axis(n), n + ('' if layer is None else str(layer))) for n, layer in big]
        own.append(_bf(_pack_slabs(slab, (N_DEV,)))[:, None])
        scatters[stage] = _start_exchange('scatter_' + stage, own, scatter=True)
        return scatters[stage][4]

    posb = jnp.broadcast_to(positions.astype(F32).reshape(t, 1), (t, LANES))
    loss, dx, grads = _local_step(x.reshape(t, d), posb, loss_target.reshape(t, d), w, seq, late_weights, start_scatter)
    start_scatter('last', grads)
    after = scatters['last'][5]

    updated, small_grads = {}, {}
    for stage, (big, small) in GRAD_STAGES.items():
        owns, landed = _wait_exchange('scatter_' + stage, scatters[stage], after, scatter=True)
        for (n, layer), own, land in zip(big, owns, landed):
            updated[n] = _sum_and_adamw(me1, land, own, _stored(n, given[n]), _stored(n, given['m_' + n]), _stored(n, given['v_' + n]),
                                        n + ('' if layer is None else str(layer)), layer, updated.get(n))
        summed = _sum_chunks(me1, landed[-1], owns[-1], stage)
        small_grads.update(zip(small, _unpack_slabs(summed, [local[n].shape for n in small])))
        after = sum([updated[n][1][:1, :1, :1] for n, _ in big], summed[:1, :1].reshape(1, 1, 1))
    names = list(small_grads)
    news = _adamw_small([small_grads[n] for n in names], *[[given[p + n] for n in names] for p in ('', 'm_', 'v_')])
    for i, n in enumerate(names):
        updated[n] = [small_grads[n], news[0][i], news[1][i], news[2][i]]
    total = lax.psum(loss[0, 0], ("x", "y", "c"))
    return (total, dx.reshape(b, seq, d), *[_stored(n, updated[n][kind]) for kind in range(4) for n in WEIGHTS])
```

```python
import math

import jax
import jax.numpy as jnp
from jax import lax
from jax.experimental import pallas as pl
from jax.experimental.pallas import tpu as pltpu

F32 = jnp.float32
BF16 = jnp.bfloat16
MESH = pl.DeviceIdType.MESH

N_DEV = 8
LANES = 128
HALO = 8
VMEM_LIMIT = 56 << 20

NORM_EPS = 1e-6
HEADS = 8
HEAD_PAD = 128
QK_NOPE = 64
QK_ROPE = 32
ROPE_HALF = 16
ROPE_BASE = 10000.0
ATTN_SCALE = (QK_NOPE + QK_ROPE) ** -0.5
LRU_C = 8.0
LRU_W = 512
CHUNK = 128
SGU_GROUPS = 8
D_FF = 2816
FF_BLOCKS = 2

ADAM_LR, ADAM_B1, ADAM_B2, ADAM_EPS, ADAM_WD, ADAM_STEP = 0.001, 0.9, 0.999, 1e-08, 0.01, 10

WEIGHTS = ['ab_norm', 'ab_w_in', 'ab_q_norm', 'ab_w_q_b', 'ab_kv_norm', 'ab_w_kv_b', 'ab_conv_w', 'ab_conv_b',
           'ab_w_rg_a', 'ab_b_rg_a', 'ab_w_rg_x', 'ab_b_rg_x', 'ab_lambda', 'ab_w_out', 'c_norm', 'c_w_in', 'c_ln_g',
           'c_ln_b', 'c_w_s', 'c_b_s', 'c_w_out', 'ffn_norm', 'ffn_w_gate', 'ffn_w_up', 'ffn_conv_w', 'ffn_conv_b',
           'ffn_w_down', 'final_norm']
SHARD_AXIS = {'ab_w_in': 2, 'ab_w_q_b': 2, 'ab_w_kv_b': 2, 'ab_conv_w': 2, 'ab_w_out': 1, 'c_norm': 1, 'c_w_in': 2,
              'c_ln_g': 1, 'c_ln_b': 1, 'c_w_out': 1, 'ffn_w_gate': 2, 'ffn_w_up': 2, 'ffn_conv_w': 2, 'ffn_w_down': 1}
MATRICES = ['ab_w_in', 'ab_w_q_b', 'ab_w_kv_b', 'ab_w_out', 'c_w_in', 'c_w_out', 'ffn_w_gate', 'ffn_w_up', 'ffn_w_down']
BIG = ['ab_w_in', 'c_w_in', 'ffn_w_gate', 'ffn_w_up', 'ab_w_out', 'c_w_out', 'ffn_w_down']
REPLICATED = [n for n in WEIGHTS if n not in SHARD_AXIS]
SMALL_SHARDED = [n for n in WEIGHTS if n in SHARD_AXIS and n not in BIG]


def _bf(x):
    return x.astype(BF16)


def _nn(a, b):
    return lax.dot_general(_bf(a), _bf(b), (((1,), (0,)), ((), ())), preferred_element_type=F32)


def _nt(a, b):
    return lax.dot_general(_bf(a), _bf(b), (((1,), (1,)), ((), ())), preferred_element_type=F32)


def _tn(a, b):
    return lax.dot_general(_bf(a), _bf(b), (((0,), (0,)), ((), ())), preferred_element_type=F32)


def _rms(x, g):
    return x * lax.rsqrt(jnp.mean(x * x, axis=-1, keepdims=True) + NORM_EPS) * g


def _layer_norm(x, g, b):
    xc = x - jnp.mean(x, axis=-1, keepdims=True)
    return xc * lax.rsqrt(jnp.mean(xc * xc, axis=-1, keepdims=True) + NORM_EPS) * g + b


def _gelu(x):
    return jax.nn.gelu(x)


STRIP = 16
STRIP_LANES = 384
GELU_C = math.sqrt(2.0 / math.pi)
GELU_A = 0.044715


def _gelu_and_grad(x):
    x2 = x * x
    t = jnp.tanh(x * (GELU_C + (GELU_C * GELU_A) * x2))
    half_x = 0.5 * x
    one_plus_t = 1.0 + t
    return half_x * one_plus_t, 0.5 * one_plus_t + half_x * (1.0 - t * t) * (GELU_C + (3.0 * GELU_C * GELU_A) * x2)


def _colsum(x):
    return jnp.sum(x, axis=0, keepdims=True)


def _softplus(x):
    return jnp.maximum(x, 0.0) + jnp.log1p(jnp.exp(-jnp.abs(x)))


@jax.custom_vjp
def _decay(x):
    a = jnp.exp(x)
    y = 2.0 * x
    series = -y * (1.0 + y * (1 / 2 + y * (1 / 6 + y * (1 / 24 + y * (1 / 120 + y * (1 / 720))))))
    return a, jnp.where(y < -0.3, 1.0 - a * a, series)


def _decay_fwd(x):
    a, gap = _decay(x)
    return (a, gap), a


def _decay_bwd(a, cts):
    return (a * (cts[0] - 2.0 * a * cts[1]),)


_decay.defvjp(_decay_fwd, _decay_bwd)


def _accumulate(ref, val, first):
    @pl.when(first)
    def _():
        ref[...] = val

    @pl.when(jnp.logical_not(first))
    def _():
        ref[...] += val


def _params(n_axes=1):
    return pltpu.CompilerParams(dimension_semantics=("arbitrary",) * n_axes, vmem_limit_bytes=VMEM_LIMIT)


def _row(tm, n):
    return pl.BlockSpec((tm, n), lambda i: (i, 0))


def _const(shape):
    nd = len(shape)
    return pl.BlockSpec(shape, lambda i: (0,) * nd, pipeline_mode=pl.Buffered(1))


def _prev_halo(tm, n):
    return pl.BlockSpec((HALO, n), lambda i: (jnp.maximum(i * (tm // HALO) - 1, 0), 0))


def _next_halo(tm, n, n_tiles):
    last = n_tiles * (tm // HALO) - 1
    return pl.BlockSpec((HALO, n), lambda i: (jnp.minimum((i + 1) * (tm // HALO), last), 0))


def _sds(shape, dtype=F32):
    return jax.ShapeDtypeStruct(shape, dtype)


def _rope_tables(posb):
    lane = lax.broadcasted_iota(jnp.int32, posb.shape, 1)
    in_rope = jnp.logical_and(lane >= QK_NOPE, lane < QK_NOPE + QK_ROPE)
    j = (lane & (ROPE_HALF - 1)).astype(F32)
    inv_freq = jnp.exp((-math.log(ROPE_BASE)) * j / ROPE_HALF)
    ang = posb * inv_freq
    return jnp.where(in_rope, jnp.cos(ang), 1.0), jnp.where(in_rope, jnp.sin(ang), 0.0)


def _rot(q):
    n = q.shape[1]
    lane = lax.broadcasted_iota(jnp.int32, q.shape, 1) & (HEAD_PAD - 1)
    first_half = jnp.where(lane >= QK_NOPE, -pltpu.roll(q, n - ROPE_HALF, 1), 0.0)
    second_half = jnp.where(lane < QK_NOPE + QK_ROPE, pltpu.roll(q, ROPE_HALF, 1), 0.0)
    return jnp.where(lane < QK_NOPE + ROPE_HALF, first_half, second_half)


def _rope(q, cos_t, sin_t):
    return q * cos_t + _rot(q) * sin_t


def _rope_transpose(dq, cos_t, sin_t):
    return dq * cos_t - _rot(dq * sin_t)


def _tile_heads(t):
    return jnp.concatenate([t] * HEADS, axis=1)


Q_LORA, KV_LORA = 256, 128
Z_KPE = Q_LORA + KV_LORA
Z_LRU = Z_KPE + HEAD_PAD
Z_GATE = Z_LRU + LRU_W
Z_WIDTH = Z_GATE + LRU_W


def _ab_in_fwd(x, posb, w, tm):
    t, d = x.shape

    def body(x_ref, pos_ref, gn_ref, win_ref, qn_ref, wq_ref, kvn_ref, wk_ref, wv_ref, q_out, k_out, v_out, xl_out, gate_out):
        hn = _rms(x_ref[...], gn_ref[...])
        z = _nn(hn, win_ref[...])
        cqn = _rms(z[:, :Q_LORA], qn_ref[...])
        kvn = _rms(z[:, Q_LORA:Z_KPE], kvn_ref[...])
        cos_t, sin_t = _rope_tables(pos_ref[...])
        q_out[...] = _rope(_nn(cqn, wq_ref[...]), _tile_heads(cos_t), _tile_heads(sin_t))
        kpe = _rope(z[:, Z_KPE:Z_LRU], cos_t, sin_t)
        k_out[...] = _nn(kvn, wk_ref[...]) + _tile_heads(kpe)
        v_out[...] = _nn(kvn, wv_ref[...])
        xl_out[...] = z[:, Z_LRU:Z_GATE]
        gate_out[...] = z[:, Z_GATE:]

    hp = HEADS * HEAD_PAD
    return pl.pallas_call(
        body, name="ab_in_fwd", grid=(t // tm,),
        in_specs=[_row(tm, d), _row(tm, LANES), _const((1, d)), _const((d, Z_WIDTH)), _const((1, Q_LORA)), _const((Q_LORA, hp)),
                  _const((1, KV_LORA)), _const((KV_LORA, hp)), _const((KV_LORA, hp))],
        out_specs=[_row(tm, hp), _row(tm, hp), _row(tm, hp), _row(tm, LRU_W), _row(tm, LRU_W)],
        out_shape=[_sds((t, hp)), _sds((t, hp)), _sds((t, hp)), _sds((t, LRU_W)), _sds((t, LRU_W))],
        compiler_params=_params(),
    )(x, posb, w['ab_norm'], w['W_in'], w['ab_q_norm'], w['Wq'], w['ab_kv_norm'], w['Wk'], w['Wv'])


def _ab_in_bwd(x, posb, w, dq, dk, dv, dxl, dgate, dres, tm):
    t, d = x.shape
    hp = HEADS * HEAD_PAD

    def body(x_ref, pos_ref, gn_ref, win_ref, qn_ref, wq_ref, kvn_ref, wk_ref, wv_ref, dq_ref, dk_ref, dv_ref, dxl_ref, dgate_ref,
             dres_ref, dx_out, dgn_out, dwin_out, dqn_out, dwq_out, dkvn_out, dwk_out, dwv_out):
        first = pl.program_id(0) == 0
        hn, vjp_in = jax.vjp(_rms, x_ref[...], gn_ref[...])
        z = _nn(hn, win_ref[...])
        cqn, vjp_q = jax.vjp(_rms, z[:, :Q_LORA], qn_ref[...])
        kvn, vjp_kv = jax.vjp(_rms, z[:, Q_LORA:Z_KPE], kvn_ref[...])
        cos_t, sin_t = _rope_tables(pos_ref[...])
        dq0 = _rope_transpose(dq_ref[...], _tile_heads(cos_t), _tile_heads(sin_t))
        dk0 = dk_ref[...]
        dv0 = dv_ref[...]
        dkpe = dk0[:, :HEAD_PAD]
        for h in range(1, HEADS):
            dkpe = dkpe + dk0[:, h * HEAD_PAD:(h + 1) * HEAD_PAD]
        dkpe = _rope_transpose(dkpe, cos_t, sin_t)
        _accumulate(dwq_out, _tn(cqn, dq0), first)
        _accumulate(dwk_out, _tn(kvn, dk0), first)
        _accumulate(dwv_out, _tn(kvn, dv0), first)
        dcq, dqn = vjp_q(_nt(dq0, wq_ref[...]))
        dckv, dkvn = vjp_kv(_nt(dk0, wk_ref[...]) + _nt(dv0, wv_ref[...]))
        _accumulate(dqn_out, dqn, first)
        _accumulate(dkvn_out, dkvn, first)
        dz = jnp.concatenate([dcq, dckv, dkpe, dxl_ref[...], dgate_ref[...]], axis=1)
        _accumulate(dwin_out, _tn(hn, dz), first)
        dx, dgn = vjp_in(_nt(dz, win_ref[...]))
        _accumulate(dgn_out, dgn, first)
        dx_out[...] = dx + dres_ref[...]

    return pl.pallas_call(
        body, name="ab_in_bwd", grid=(t // tm,),
        in_specs=[_row(tm, d), _row(tm, LANES), _const((1, d)), _const((d, Z_WIDTH)), _const((1, Q_LORA)), _const((Q_LORA, hp)),
                  _const((1, KV_LORA)), _const((KV_LORA, hp)), _const((KV_LORA, hp)),
                  _row(tm, hp), _row(tm, hp), _row(tm, hp), _row(tm, LRU_W), _row(tm, LRU_W), _row(tm, d)],
        out_specs=[_row(tm, d), _const((1, d)), _const((d, Z_WIDTH)), _const((1, Q_LORA)), _const((Q_LORA, hp)),
                   _const((1, KV_LORA)), _const((KV_LORA, hp)), _const((KV_LORA, hp))],
        out_shape=[_sds((t, d)), _sds((1, d)), _sds((d, Z_WIDTH)), _sds((1, Q_LORA)), _sds((Q_LORA, hp)),
                   _sds((1, KV_LORA)), _sds((KV_LORA, hp)), _sds((KV_LORA, hp))],
        compiler_params=_params(),
    )(x, posb, w['ab_norm'], w['W_in'], w['ab_q_norm'], w['Wq'], w['ab_kv_norm'], w['Wk'], w['Wv'], dq, dk, dv, dxl, dgate, dres)


def _attn_probs(q_blk, k_ext, tq):
    ext = k_ext.shape[0]
    s = lax.dot_general(q_blk, k_ext, (((1,), (1,)), ((), ())), preferred_element_type=F32) * ATTN_SCALE
    causal = lax.broadcasted_iota(jnp.int32, (tq, tq), 1) <= lax.broadcasted_iota(jnp.int32, (tq, tq), 0)
    diag = jnp.where(causal, s[:, ext - tq:], -1e30)
    s = diag if ext == tq else jnp.concatenate([s[:, :ext - tq], diag], axis=1)
    p = jnp.exp(s - jnp.max(s, axis=1, keepdims=True))
    return p / jnp.sum(p, axis=1, keepdims=True)


def _attn_fwd(q, k, v, tq):
    b, s, hp = q.shape
    blk = pl.BlockSpec((1, s, HEAD_PAD), lambda bi, h: (bi, 0, h))

    def body(q_ref, k_ref, v_ref, o_ref, p_ref):
        kb = _bf(k_ref[0])
        vb = _bf(v_ref[0])
        for i in range(s // tq):
            ext = (i + 1) * tq
            p = _bf(_attn_probs(_bf(q_ref[0, i * tq:ext, :]), kb[:ext], tq))
            p_ref[0, 0, i * tq:ext, :ext] = p
            o_ref[0, i * tq:ext, :] = lax.dot_general(p, vb[:ext], (((1,), (0,)), ((), ())), preferred_element_type=F32)

    return pl.pallas_call(body, name="attn_fwd", grid=(b, HEADS), in_specs=[blk, blk, blk],
                          out_specs=[blk, pl.BlockSpec((1, 1, s, s), lambda bi, h: (bi, h, 0, 0))],
                          out_shape=[_sds((b, s, hp)), _sds((b, HEADS, s, s), BF16)], compiler_params=_params(2))(q, k, v)


def _attn_bwd(q, k, v, probs, do, tq):
    b, s, hp = q.shape
    blk = pl.BlockSpec((1, s, HEAD_PAD), lambda bi, h: (bi, 0, h))

    def body(q_ref, k_ref, v_ref, p_ref, do_ref, dq_ref, dk_ref, dv_ref):
        kb = _bf(k_ref[0])
        vb = _bf(v_ref[0])
        dk_ref[...] = jnp.zeros_like(dk_ref)
        dv_ref[...] = jnp.zeros_like(dv_ref)
        for i in range(s // tq):
            ext = (i + 1) * tq
            qb = _bf(q_ref[0, i * tq:ext, :])
            dob = _bf(do_ref[0, i * tq:ext, :])
            pb = p_ref[0, 0, i * tq:ext, :ext]
            p = pb.astype(F32)
            dv_ref[0, :ext, :] += lax.dot_general(pb, dob, (((0,), (0,)), ((), ())), preferred_element_type=F32)
            dp = lax.dot_general(dob, vb[:ext], (((1,), (1,)), ((), ())), preferred_element_type=F32)
            ds = _bf(p * (dp - jnp.sum(p * dp, axis=1, keepdims=True)) * ATTN_SCALE)
            dq_ref[0, i * tq:ext, :] = lax.dot_general(ds, kb[:ext], (((1,), (0,)), ((), ())), preferred_element_type=F32)
            dk_ref[0, :ext, :] += lax.dot_general(ds, qb, (((0,), (0,)), ((), ())), preferred_element_type=F32)

    return pl.pallas_call(body, name="attn_bwd", grid=(b, HEADS),
                          in_specs=[blk, blk, blk, pl.BlockSpec((1, 1, s, s), lambda bi, h: (bi, h, 0, 0)), blk], out_specs=[blk, blk, blk],
                          out_shape=[_sds((b, s, hp))] * 3, compiler_params=_params(2))(q, k, v, probs, do)


LRU_CONV = 4


def _lru_point(pre_a, pre_x, xc, lam):
    r = jax.nn.sigmoid(pre_a)
    i = jax.nn.sigmoid(pre_x)
    a, gap = _decay(-LRU_C * r * _softplus(-lam))
    return a, jnp.sqrt(gap) * (i * xc)


def _causal_conv(pad_ref, x, halo, first_in_seq, w, taps):
    tm = x.shape[0]
    pad_ref[:HALO, :] = jnp.where(first_in_seq, 0.0, halo)
    pad_ref[HALO:, :] = x
    y = w[taps - 1:taps, :] * x
    for k in range(taps - 1):
        off = HALO - (taps - 1) + k
        y = y + w[k:k + 1, :] * pad_ref[off:off + tm, :]
    return y


def _conv_taps(pad_ref, r, cols, taps):
    blocks = [pad_ref[r + j * HALO:r + (j + 1) * HALO, cols] for j in range(1 + STRIP // HALO)]
    sub = lax.broadcasted_iota(jnp.int32, blocks[0].shape, 0)
    out = []
    for k in range(taps - 1):
        s = taps - 1 - k
        rolled = [pltpu.roll(b, s, 0) for b in blocks]
        out.append(jnp.concatenate([jnp.where(sub < s, rolled[j], rolled[j + 1]) for j in range(STRIP // HALO)], axis=0))
    out.append(jnp.concatenate(blocks[1:], axis=0))
    return out


def _causal_conv_wgrad(pad_ref, dy, taps):
    tm = dy.shape[0]
    return jnp.concatenate([_colsum(dy * pad_ref[HALO - (taps - 1) + k:HALO - (taps - 1) + k + tm, :]) for k in range(taps)], axis=0)


def _causal_conv_transpose(pad_ref, dy, halo_next, last_in_seq, w, taps):
    tm = dy.shape[0]
    pad_ref[:tm, :] = dy
    pad_ref[tm:, :] = jnp.where(last_in_seq, 0.0, halo_next)
    dx = w[taps - 1:taps, :] * dy
    for k in range(taps - 1):
        off = (taps - 1) - k
        dx = dx + w[k:k + 1, :] * pad_ref[off:off + tm, :]
    return dx


def _lru_fwd(xl, gate, w, ts, seq):
    t, n = xl.shape
    tiles_per_seq = seq // ts

    def body(xl_ref, halo_ref, gate_ref, cw_ref, cb_ref, wa_ref, ba_ref, wx_ref, bx_ref, lam_ref, y_out, h_out, pad_ref, a_ref, b_ref, carry_ref):
        first_in_seq = pl.program_id(0) % tiles_per_seq == 0
        xc = _causal_conv(pad_ref, xl_ref[...], halo_ref[...], first_in_seq, cw_ref[...], LRU_CONV) + cb_ref[...]
        a, bx = _lru_point(_nn(xc, wa_ref[...]) + ba_ref[...], _nn(xc, wx_ref[...]) + bx_ref[...], xc, lam_ref[...])
        a_ref[...] = a
        b_ref[...] = bx

        @pl.when(first_in_seq)
        def _():
            carry_ref[...] = jnp.zeros_like(carry_ref)

        def step(r, h):
            h = a_ref[pl.ds(r, 1), :] * h + b_ref[pl.ds(r, 1), :]
            h_out[pl.ds(r, 1), :] = h
            return h

        carry_ref[...] = lax.fori_loop(0, ts, step, carry_ref[...], unroll=8)
        y_out[...] = h_out[...] * _gelu(gate_ref[...])

    return pl.pallas_call(
        body, name="lru_fwd", grid=(t // ts,),
        in_specs=[_row(ts, n), _prev_halo(ts, n), _row(ts, n), _const((LRU_CONV, n)), _const((1, n)), _const((n, n)), _const((1, n)),
                  _const((n, n)), _const((1, n)), _const((1, n))],
        out_specs=[_row(ts, n), _row(ts, n)], out_shape=[_sds((t, n)), _sds((t, n))],
        scratch_shapes=[pltpu.VMEM((HALO + ts, n), F32), pltpu.VMEM((ts, n), F32), pltpu.VMEM((ts, n), F32), pltpu.VMEM((1, n), F32)],
        compiler_params=_params(),
    )(xl, xl, gate, w['ab_conv_w'], w['ab_conv_b'], w['Wa'], w['ab_b_rg_a'], w['Wx'], w['ab_b_rg_x'], w['ab_lambda'])


def _lru_bwd(xl, gate, hs, dy, w, ts, seq):
    t, n = xl.shape
    tiles_per_seq = seq // ts
    n_tiles = t // ts

    def rev(i):
        return n_tiles - 1 - i

    row = pl.BlockSpec((ts, n), lambda i: (rev(i), 0))
    prev = pl.BlockSpec((HALO, n), lambda i: (jnp.maximum(rev(i) * (ts // HALO) - 1, 0), 0))
    acc = lambda shape: pl.BlockSpec(shape, lambda i: (0,) * len(shape))

    def body(xl_ref, xhalo_ref, gate_ref, h_ref, hhalo_ref, dy_ref, cw_ref, cb_ref, wa_ref, ba_ref, wx_ref, bx_ref, lam_ref,
             dxl_out, dgate_out, dcw_out, dcb_out, dwa_out, dba_out, dwx_out, dbx_out, dlam_out,
             pad_ref, padh_ref, padd_ref, a_ref, g_ref, carry_ref, dhalo_ref):
        step_id = pl.program_id(0)
        first = step_id == 0
        tile = rev(step_id)
        first_in_seq = tile % tiles_per_seq == 0
        last_in_seq = tile % tiles_per_seq == tiles_per_seq - 1
        cw = cw_ref[...]
        xc = _causal_conv(pad_ref, xl_ref[...], xhalo_ref[...], first_in_seq, cw, LRU_CONV) + cb_ref[...]
        pre_a = _nn(xc, wa_ref[...]) + ba_ref[...]
        pre_x = _nn(xc, wx_ref[...]) + bx_ref[...]
        (a, _), vjp_point = jax.vjp(_lru_point, pre_a, pre_x, xc, lam_ref[...])
        h = h_ref[...]
        _, vjp_out = jax.vjp(lambda h_, g_: h_ * _gelu(g_), h, gate_ref[...])
        dh, dgate = vjp_out(dy_ref[...])
        dgate_out[...] = dgate
        a_ref[...] = a
        g_ref[...] = dh

        @pl.when(last_in_seq)
        def _():
            carry_ref[...] = jnp.zeros_like(carry_ref)

        def step(j, c):
            r = ts - 1 - j
            g = g_ref[pl.ds(r, 1), :] + c
            g_ref[pl.ds(r, 1), :] = g
            return a_ref[pl.ds(r, 1), :] * g

        carry_ref[...] = lax.fori_loop(0, ts, step, carry_ref[...], unroll=8)
        g = g_ref[...]
        padh_ref[:HALO, :] = jnp.where(first_in_seq, 0.0, hhalo_ref[...])
        padh_ref[HALO:, :] = h
        dpre_a, dpre_x, dxc, dlam = vjp_point((g * padh_ref[HALO - 1:HALO - 1 + ts, :], g))
        dxc = dxc + _nt(dpre_a, wa_ref[...]) + _nt(dpre_x, wx_ref[...])
        _accumulate(dwa_out, _tn(xc, dpre_a), first)
        _accumulate(dwx_out, _tn(xc, dpre_x), first)
        _accumulate(dba_out, _colsum(dpre_a), first)
        _accumulate(dbx_out, _colsum(dpre_x), first)
        _accumulate(dlam_out, dlam, first)
        _accumulate(dcb_out, _colsum(dxc), first)
        _accumulate(dcw_out, _causal_conv_wgrad(pad_ref, dxc, LRU_CONV), first)
        dxl_out[...] = _causal_conv_transpose(padd_ref, dxc, dhalo_ref[...], last_in_seq, cw, LRU_CONV)
        dhalo_ref[...] = dxc[:HALO, :]

    return pl.pallas_call(
        body, name="lru_bwd", grid=(n_tiles,),
        in_specs=[row, prev, row, row, prev, row, _const((LRU_CONV, n)), _const((1, n)), _const((n, n)), _const((1, n)),
                  _const((n, n)), _const((1, n)), _const((1, n))],
        out_specs=[row, row, acc((LRU_CONV, n)), acc((1, n)), acc((n, n)), acc((1, n)), acc((n, n)), acc((1, n)), acc((1, n))],
        out_shape=[_sds((t, n)), _sds((t, n)), _sds((LRU_CONV, n)), _sds((1, n)), _sds((n, n)), _sds((1, n)), _sds((n, n)),
                   _sds((1, n)), _sds((1, n))],
        scratch_shapes=[pltpu.VMEM((HALO + ts, n), F32), pltpu.VMEM((HALO + ts, n), F32), pltpu.VMEM((ts + HALO, n), F32),
                        pltpu.VMEM((ts, n), F32), pltpu.VMEM((ts, n), F32), pltpu.VMEM((1, n), F32), pltpu.VMEM((HALO, n), F32)],
        compiler_params=_params(),
    )(xl, xl, gate, hs, hs, dy, w['ab_conv_w'], w['ab_conv_b'], w['Wa'], w['ab_b_rg_a'], w['Wx'], w['ab_b_rg_x'], w['ab_lambda'])


def _ab_out_fwd(x, o, y, w, tm):
    t, d = x.shape
    hp = o.shape[1]

    def body(x_ref, o_ref, y_ref, wa_ref, wb_ref, h_out):
        h_out[...] = x_ref[...] + _nn(o_ref[...], wa_ref[...]) + _nn(y_ref[...], wb_ref[...])

    return pl.pallas_call(body, name="ab_out_fwd", grid=(t // tm,),
                          in_specs=[_row(tm, d), _row(tm, hp), _row(tm, LRU_W), _const((hp, d)), _const((LRU_W, d))],
                          out_specs=_row(tm, d), out_shape=_sds((t, d)), compiler_params=_params())(x, o, y, w['Wo_a'], w['Wo_b'])


def _ab_out_bwd(o, y, dh, w, tm):
    t, d = dh.shape
    hp = o.shape[1]

    def body(o_ref, y_ref, dh_ref, wa_ref, wb_ref, do_out, dy_out, dwa_out, dwb_out):
        first = pl.program_id(0) == 0
        dh_t = dh_ref[...]
        do_out[...] = _nt(dh_t, wa_ref[...])
        dy_out[...] = _nt(dh_t, wb_ref[...])
        _accumulate(dwa_out, _tn(o_ref[...], dh_t), first)
        _accumulate(dwb_out, _tn(y_ref[...], dh_t), first)

    return pl.pallas_call(body, name="ab_out_bwd", grid=(t // tm,),
                          in_specs=[_row(tm, hp), _row(tm, LRU_W), _row(tm, d), _const((hp, d)), _const((LRU_W, d))],
                          out_specs=[_row(tm, hp), _row(tm, LRU_W), _const((hp, d)), _const((LRU_W, d))],
                          out_shape=[_sds((t, hp)), _sds((t, LRU_W)), _sds((hp, d)), _sds((LRU_W, d))],
                          compiler_params=_params())(o, y, dh, w['Wo_a'], w['Wo_b'])


FFN_CONV = 3


def _ffn_a_fwd(h, norm, wg, wu, tm):
    t, d = h.shape
    fb = D_FF // FF_BLOCKS

    def body(h_ref, gn_ref, wg_ref, wu_ref, g_out, u_out, hn_out):
        hn = _bf(_rms(h_ref[...], gn_ref[...]))
        hn_out[0] = hn
        g_out[...] = _nt(hn, wg_ref[...])
        u_out[...] = _nt(hn, wu_ref[...])

    wspec = pl.BlockSpec((fb, d), lambda f, i: (f, 0))
    ospec = pl.BlockSpec((tm, fb), lambda f, i: (i, f))
    return pl.pallas_call(
        body, name="ffn_a_fwd", grid=(FF_BLOCKS, t // tm),
        in_specs=[pl.BlockSpec((tm, d), lambda f, i: (i, 0)), pl.BlockSpec((1, d), lambda f, i: (0, 0)), wspec, wspec],
        out_specs=[ospec, ospec, pl.BlockSpec((1, tm, d), lambda f, i: (f, i, 0))],
        out_shape=[_sds((t, D_FF)), _sds((t, D_FF)), _sds((FF_BLOCKS, t, d), BF16)], compiler_params=_params(2))(h, norm, wg, wu)


def _ffn_b_fwd(g, u, h, cw, cb, wd, tm, seq, final=None):
    t, d = h.shape
    tiles_per_seq = seq // tm

    def body(g_ref, halo_ref, u_ref, h_ref, cw_ref, cb_ref, wd_ref, *rest):
        pad_ref, act_ref = rest[-2:]
        pad_ref[:HALO, :] = jnp.where(pl.program_id(0) % tiles_per_seq == 0, 0.0, halo_ref[...])
        pad_ref[HALO:, :] = g_ref[...]
        cw = cw_ref[...]
        cb = cb_ref[...]
        for c0 in range(0, D_FF, STRIP_LANES):
            cols = slice(c0, min(c0 + STRIP_LANES, D_FF))
            for r in range(0, tm, STRIP):
                taps = _conv_taps(pad_ref, r, cols, FFN_CONV)
                gc = cb[:, cols] + cw[0:1, cols] * taps[0] + cw[1:2, cols] * taps[1] + cw[2:3, cols] * taps[2]
                act_ref[r:r + STRIP, cols] = _bf(_gelu(gc) * u_ref[r:r + STRIP, cols])
        h_new = h_ref[...] + _nn(act_ref[...], wd_ref[...])
        if final is None:
            rest[0][...] = h_new
        else:
            tgt_ref, fn_ref, dh_out, loss_out, dfn_out = rest[:5]
            first = pl.program_id(0) == 0
            loss, dh_out[...], dfn = _loss_and_grad(h_new, tgt_ref[...], fn_ref[...])
            _accumulate(loss_out, loss, first)
            _accumulate(dfn_out, dfn, first)

    in_specs = [_row(tm, D_FF), _prev_halo(tm, D_FF), _row(tm, D_FF), _row(tm, d), _const((FFN_CONV, D_FF)), _const((1, D_FF)), _const((D_FF, d))]
    scratch = [pltpu.VMEM((HALO + tm, D_FF), F32), pltpu.VMEM((tm, D_FF), BF16)]
    if final is None:
        return pl.pallas_call(body, name="ffn_b_fwd", grid=(t // tm,), in_specs=in_specs, out_specs=_row(tm, d), out_shape=_sds((t, d)),
                              scratch_shapes=scratch, compiler_params=_params())(g, g, u, h, cw, cb, wd)
    return pl.pallas_call(body, name="ffn_b_fwd_loss", grid=(t // tm,), in_specs=in_specs + [_row(tm, d), _const((1, d))],
                          out_specs=[_row(tm, d), _const((1, 1)), _const((1, d))],
                          out_shape=[_sds((t, d)), _sds((1, 1)), _sds((1, d))],
                          scratch_shapes=scratch, compiler_params=_params())(g, g, u, h, cw, cb, wd, *final)


def _ffn_b_bwd(g, u, dout, cw, cb, wd, tm, seq):
    t, d = dout.shape
    fb = D_FF // FF_BLOCKS
    tiles_per_seq = seq // tm

    def body(g_ref, halo_ref, u_ref, dout_ref, cw_ref, cb_ref, wd_ref, dgc_out, du_out, dwd_out, dcw_out, dcb_out,
             pad_ref, dact_ref, act_ref, acc_ref, dwd_acc):
        i = pl.program_id(1)
        first = i == 0
        pad_ref[:HALO, :] = jnp.where(i % tiles_per_seq == 0, 0.0, halo_ref[...])
        pad_ref[HALO:, :] = g_ref[...]
        dout_b = _bf(dout_ref[...])
        dact_ref[...] = _nt(dout_b, wd_ref[...])
        cw = cw_ref[...]
        cb = cb_ref[...]
        fold = lambda a: a[:HALO] + a[HALO:]
        for c0 in range(0, fb, STRIP_LANES):
            cols = slice(c0, min(c0 + STRIP_LANES, fb))
            sums = [jnp.zeros((HALO, cols.stop - c0), F32) for _ in range(1 + FFN_CONV)]
            for r in range(0, tm, STRIP):
                rows = slice(r, r + STRIP)
                taps = _conv_taps(pad_ref, r, cols, FFN_CONV)
                gelu, dgelu = _gelu_and_grad(cb[:, cols] + cw[0:1, cols] * taps[0] + cw[1:2, cols] * taps[1] + cw[2:3, cols] * taps[2])
                u = u_ref[rows, cols]
                dact = dact_ref[rows, cols]
                act_ref[rows, cols] = _bf(gelu * u)
                du_out[rows, cols] = _bf(dact * gelu)
                dgc = dact * u * dgelu
                dgc_out[rows, cols] = dgc
                sums = [sums[0] + fold(dgc)] + [sums[1 + k] + fold(dgc * taps[k]) for k in range(FFN_CONV)]
            for k in range(1 + FFN_CONV):
                acc_ref[k, :, cols] = sums[k]
        _accumulate(dwd_acc, _tn(act_ref[...], dout_b), first)

        @pl.when(i == t // tm - 1)
        def _():
            dwd_out[...] = _bf(dwd_acc[...])

        _accumulate(dcb_out, _colsum(acc_ref[0]), first)
        _accumulate(dcw_out, jnp.concatenate([_colsum(acc_ref[1 + k]) for k in range(FFN_CONV)], axis=0), first)

    blk = pl.BlockSpec((tm, fb), lambda f, i: (i, f))
    halo = pl.BlockSpec((HALO, fb), lambda f, i: (jnp.maximum(i * (tm // HALO) - 1, 0), f))
    wd_blk = pl.BlockSpec((fb, d), lambda f, i: (f, 0), pipeline_mode=pl.Buffered(1))
    return pl.pallas_call(
        body, name="ffn_b_bwd", grid=(FF_BLOCKS, t // tm),
        in_specs=[blk, halo, blk, pl.BlockSpec((tm, d), lambda f, i: (i, 0)), pl.BlockSpec((FFN_CONV, fb), lambda f, i: (0, f)),
                  pl.BlockSpec((1, fb), lambda f, i: (0, f)), wd_blk],
        out_specs=[blk, blk, wd_blk, pl.BlockSpec((FFN_CONV, fb), lambda f, i: (0, f)),
                   pl.BlockSpec((1, fb), lambda f, i: (0, f))],
        out_shape=[_sds((t, D_FF)), _sds((t, D_FF), BF16), _sds((D_FF, d), BF16), _sds((FFN_CONV, D_FF)), _sds((1, D_FF))],
        scratch_shapes=[pltpu.VMEM((HALO + tm, fb), F32), pltpu.VMEM((tm, fb), F32), pltpu.VMEM((tm, fb), BF16),
                        pltpu.VMEM((1 + FFN_CONV, HALO, fb), F32), pltpu.VMEM((fb, d), F32)],
        compiler_params=_params(2))(g, g, u, dout, cw, cb, wd)


def _ffn_a_dgrad(h, norm, dgc, du, dres, cw, wg, wu, tm, seq):
    t, d = h.shape
    tiles_per_seq = seq // tm
    n_tiles = t // tm

    def body(h_ref, gn_ref, dgc_ref, halo_ref, du_ref, dres_ref, cw_ref, wg_ref, wu_ref, dh_out, dg_out, dgn_out, pad_ref):
        i = pl.program_id(0)
        last_in_seq = i % tiles_per_seq == tiles_per_seq - 1
        dg = _bf(_causal_conv_transpose(pad_ref, dgc_ref[...], halo_ref[...], last_in_seq, cw_ref[...], FFN_CONV))
        dg_out[...] = dg
        _, vjp_norm = jax.vjp(_rms, h_ref[...], gn_ref[...])
        dh, dgn = vjp_norm(_nn(dg, wg_ref[...]) + _nn(du_ref[...], wu_ref[...]))
        dh_out[...] = dh + dres_ref[...]
        _accumulate(dgn_out, dgn, i == 0)

    return pl.pallas_call(
        body, name="ffn_a_dgrad", grid=(n_tiles,),
        in_specs=[_row(tm, d), _const((1, d)), _row(tm, D_FF), _next_halo(tm, D_FF, n_tiles), _row(tm, D_FF), _row(tm, d),
                  _const((FFN_CONV, D_FF)), _const((D_FF, d)), _const((D_FF, d))],
        out_specs=[_row(tm, d), _row(tm, D_FF), _const((1, d))], out_shape=[_sds((t, d)), _sds((t, D_FF), BF16), _sds((1, d))],
        scratch_shapes=[pltpu.VMEM((tm + HALO, D_FF), F32)], compiler_params=_params())(h, norm, dgc, dgc, du, dres, cw, wg, wu)


def _ffn_a_wgrad(hn, dg, du, tm):
    _, t, d = hn.shape
    fb = D_FF // FF_BLOCKS

    n_tiles = t // tm

    def body(hn_ref, dg_ref, du_ref, dwg_out, dwu_out, acc_g, acc_u):
        i = pl.program_id(1)
        hn_t = hn_ref[0]
        _accumulate(acc_g, _tn(dg_ref[...], hn_t), i == 0)
        _accumulate(acc_u, _tn(du_ref[...], hn_t), i == 0)

        @pl.when(i == n_tiles - 1)
        def _():
            dwg_out[...] = _bf(acc_g[...])
            dwu_out[...] = _bf(acc_u[...])

    blk = pl.BlockSpec((tm, fb), lambda f, i: (i, f))
    wspec = pl.BlockSpec((fb, d), lambda f, i: (f, 0), pipeline_mode=pl.Buffered(1))
    return pl.pallas_call(body, name="ffn_a_wgrad", grid=(FF_BLOCKS, n_tiles),
                          in_specs=[pl.BlockSpec((1, tm, d), lambda f, i: (0, i, 0)), blk, blk],
                          out_specs=[wspec, wspec], out_shape=[_sds((D_FF, d), BF16), _sds((D_FF, d), BF16)],
                          scratch_shapes=[pltpu.VMEM((fb, d), F32), pltpu.VMEM((fb, d), F32)],
                          compiler_params=_params(2))(hn, dg, du)


def _sgu_mix(vn, ws_ref, bst):
    tril = lax.broadcasted_iota(jnp.int32, (CHUNK, CHUNK), 0) >= lax.broadcasted_iota(jnp.int32, (CHUNK, CHUNK), 1)
    wms = [jnp.where(tril, ws_ref[g], 0.0) for g in range(SGU_GROUPS)]
    chunks = []
    for n in range(vn.shape[0] // CHUNK):
        vc = vn[n * CHUNK:(n + 1) * CHUNK, :]
        chunks.append(jnp.concatenate(
            [_nn(wms[g], vc[:, g * CHUNK:(g + 1) * CHUNK]) + bst[:, g:g + 1] for g in range(SGU_GROUPS)], axis=1))
    return jnp.concatenate(chunks, axis=0)


def _sgu_fwd(h, w, tm):
    t, d = h.shape

    def body(h_ref, cn_ref, win_ref, lg_ref, lb_ref, ws_ref, bst_ref, wout_ref, h_out):
        h_t = h_ref[...]
        z = _gelu(_nn(_rms(h_t, cn_ref[...]), win_ref[...]))
        vn = _layer_norm(z[:, d:], lg_ref[...], lb_ref[...])
        s = _sgu_mix(vn, ws_ref, bst_ref[...])
        h_out[...] = h_t + _nn(z[:, :d] * s, wout_ref[...])

    return pl.pallas_call(
        body, name="sgu_fwd", grid=(t // tm,),
        in_specs=[_row(tm, d), _const((1, d)), _const((d, 2 * d)), _const((1, d)), _const((1, d)), _const((SGU_GROUPS, CHUNK, CHUNK)),
                  _const((CHUNK, LANES)), _const((d, d))],
        out_specs=_row(tm, d), out_shape=_sds((t, d)), compiler_params=_params(),
    )(h, w['c_norm'], w['c_w_in'], w['c_ln_g'], w['c_ln_b'], w['c_w_s'], w['bsT'], w['c_w_out'])


def _sgu_bwd(h, dout, w, tm):
    t, d = h.shape

    def body(h_ref, dout_ref, cn_ref, win_ref, lg_ref, lb_ref, ws_ref, bst_ref, wout_ref,
             dh_out, dcn_out, dwin_out, dlg_out, dlb_out, dws_out, dbst_out, dwout_out):
        first = pl.program_id(0) == 0
        hn, vjp_norm = jax.vjp(_rms, h_ref[...], cn_ref[...])
        zpre = _nn(hn, win_ref[...])
        u, vjp_u = jax.vjp(_gelu, zpre[:, :d])
        vn, vjp_v = jax.vjp(lambda zp, lg, lb: _layer_norm(_gelu(zp), lg, lb), zpre[:, d:], lg_ref[...], lb_ref[...])
        s = _sgu_mix(vn, ws_ref, bst_ref[...])
        dout_t = dout_ref[...]
        dus = _nt(dout_t, wout_ref[...])
        _accumulate(dwout_out, _tn(u * s, dout_t), first)
        ds = dus * u
        tril = lax.broadcasted_iota(jnp.int32, (CHUNK, CHUNK), 0) >= lax.broadcasted_iota(jnp.int32, (CHUNK, CHUNK), 1)
        lane = lax.broadcasted_iota(jnp.int32, (CHUNK, LANES), 1)
        dws = [jnp.zeros((CHUNK, CHUNK), F32) for _ in range(SGU_GROUPS)]
        dbst = jnp.zeros((CHUNK, LANES), F32)
        dvn_chunks = []
        for n in range(tm // CHUNK):
            cols = []
            for g in range(SGU_GROUPS):
                ds_ng = ds[n * CHUNK:(n + 1) * CHUNK, g * CHUNK:(g + 1) * CHUNK]
                vc_ng = vn[n * CHUNK:(n + 1) * CHUNK, g * CHUNK:(g + 1) * CHUNK]
                cols.append(_tn(jnp.where(tril, ws_ref[g], 0.0), ds_ng))
                dws[g] = dws[g] + _nt(ds_ng, vc_ng)
                dbst = dbst + jnp.where(lane == g, jnp.sum(ds_ng, axis=1, keepdims=True), 0.0)
            dvn_chunks.append(jnp.concatenate(cols, axis=1))
        dvn = jnp.concatenate(dvn_chunks, axis=0)
        for g in range(SGU_GROUPS):
            val = jnp.where(tril, dws[g], 0.0)

            @pl.when(first)
            def _():
                dws_out[g] = val

            @pl.when(jnp.logical_not(first))
            def _():
                dws_out[g] += val
        _accumulate(dbst_out, dbst, first)
        (dzu,) = vjp_u(dus * s)
        dzv, dlg, dlb = vjp_v(dvn)
        _accumulate(dlg_out, dlg, first)
        _accumulate(dlb_out, dlb, first)
        dzpre = jnp.concatenate([dzu, dzv], axis=1)
        _accumulate(dwin_out, _tn(hn, dzpre), first)
        dh, dcn = vjp_norm(_nt(dzpre, win_ref[...]))
        _accumulate(dcn_out, dcn, first)
        dh_out[...] = dh + dout_t

    return pl.pallas_call(
        body, name="sgu_bwd", grid=(t // tm,),
        in_specs=[_row(tm, d), _row(tm, d), _const((1, d)), _const((d, 2 * d)), _const((1, d)), _const((1, d)),
                  _const((SGU_GROUPS, CHUNK, CHUNK)), _const((CHUNK, LANES)), _const((d, d))],
        out_specs=[_row(tm, d), _const((1, d)), _const((d, 2 * d)), _const((1, d)), _const((1, d)), _const((SGU_GROUPS, CHUNK, CHUNK)),
                   _const((CHUNK, LANES)), _const((d, d))],
        out_shape=[_sds((t, d)), _sds((1, d)), _sds((d, 2 * d)), _sds((1, d)), _sds((1, d)), _sds((SGU_GROUPS, CHUNK, CHUNK)),
                   _sds((CHUNK, LANES)), _sds((d, d))],
        compiler_params=_params(),
    )(h, dout, w['c_norm'], w['c_w_in'], w['c_ln_g'], w['c_ln_b'], w['c_w_s'], w['bsT'], w['c_w_out'])


def _loss_and_grad(h, tgt, g):
    def loss_fn(h_, g_):
        err = _rms(h_, g_) - tgt
        return 0.5 * jnp.sum(jnp.mean(err * err, axis=-1, keepdims=True), axis=0, keepdims=True)

    loss, vjp_loss = jax.vjp(loss_fn, h, g)
    return (loss,) + vjp_loss(jnp.ones((1, 1), F32))


def _tile(t, seq, want):
    tm = min(want, seq)
    assert seq % tm == 0 and t % tm == 0 and tm % CHUNK == 0
    return tm


def _local_step(x, posb, target, w, seq, late_weights, on_grads):
    t, d = x.shape
    b = t // seq
    hp = HEADS * HEAD_PAD
    tm_big, tm_mid = _tile(t, seq, 512), _tile(t, seq, 256)
    tq = _tile(t, seq, 512)

    q, k, v, xl, gate = _ab_in_fwd(x, posb, w, tm_big)
    o, probs = _attn_fwd(q.reshape(b, seq, hp), k.reshape(b, seq, hp), v.reshape(b, seq, hp), tq)
    o = o.reshape(t, hp)
    y, hs = _lru_fwd(xl, gate, w, tm_big, seq)
    w = {**w, **late_weights('out0', y)}
    h1 = _ab_out_fwd(x, o, y, w, tm_big)
    hcur = h1
    saved = []
    for l in range(2):
        if l == 1:
            w = {**w, **late_weights('mix1', hcur)}
            saved_h2 = hcur
            hcur = _sgu_fwd(hcur, w, tm_mid)
        wl = late_weights('ffn%d' % l, hcur)
        g, u, hn = _ffn_a_fwd(hcur, w['ffn_norm'][l], wl['Wg'], wl['Wu'], tm_big)
        saved.append((hcur, g, u, wl, hn))
        ffn_b = (g, u, hcur, w['ffn_conv_w'][l], w['ffn_conv_b'][l], wl['Wd'], tm_mid, seq)
        if l == 0:
            hcur = _ffn_b_fwd(*ffn_b)
    dh, loss, d_final = _ffn_b_fwd(*ffn_b, final=(target, w['final_norm']))

    ffn = {}
    conv_b = list(w['ffn_conv_b'])
    for l in (1, 0):
        hin, g, u, wl, hn = saved[l]
        dgc, du, d_wd, d_cw, d_cb = _ffn_b_bwd(g, u, dh, w['ffn_conv_w'][l], conv_b[l], wl['Wd'], tm_big, seq)
        dh, dg, d_norm = _ffn_a_dgrad(hin, w['ffn_norm'][l], dgc, du, dh, w['ffn_conv_w'][l], wl['Wg'], wl['Wu'], tm_mid, seq)
        d_wg, d_wu = _ffn_a_wgrad(hn, dg, du, _tile(t, seq, 1024))
        ffn[l] = dict(ffn_norm=d_norm, ffn_conv_w=d_cw, ffn_conv_b=d_cb, Wg=d_wg, Wu=d_wu, Wd=d_wd)
        if l == 1:
            dh, d_cn, d_cwin, d_lg, d_lb, d_ws, d_bst, d_cwout = _sgu_bwd(saved_h2, dh, w, tm_mid)
            zero = on_grads('late1', dict(final_norm=d_final, c_norm=d_cn, c_ln_g=d_lg, c_ln_b=d_lb, c_w_s=d_ws, bsT=d_bst, c_w_in=d_cwin,
                                          c_w_out=d_cwout, Wg=[d_wg], Wu=[d_wu], Wd=[d_wd]))
            conv_b[0] = conv_b[0] + zero
    late0 = {name: [ffn[0][name], ffn[1][name]] for name in ('ffn_norm', 'ffn_conv_w', 'ffn_conv_b')}
    zero = on_grads('late0', dict(late0, Wg=[ffn[0]['Wg']], Wu=[ffn[0]['Wu']], Wd=[ffn[0]['Wd']]))
    w = {**w, 'Wo_b': w['Wo_b'] + zero.astype(w['Wo_b'].dtype)}
    do, dy, d_woa, d_wob = _ab_out_bwd(o, y, dh, w, tm_big)
    dxl, dgate, d_cw, d_cb, d_wa, d_ba, d_wx, d_bx, d_lam = _lru_bwd(xl, gate, hs, dy, w, tm_big, seq)
    zero = on_grads('mid', dict(Wo_a=d_woa, Wo_b=d_wob, ab_conv_w=d_cw, ab_conv_b=d_cb, Wa=d_wa, ab_b_rg_a=d_ba, Wx=d_wx,
                                ab_b_rg_x=d_bx, ab_lambda=d_lam))
    w = {**w, 'ab_norm': w['ab_norm'] + zero}
    dq, dk, dv = _attn_bwd(q.reshape(b, seq, hp), k.reshape(b, seq, hp), v.reshape(b, seq, hp), probs, do.reshape(b, seq, hp), tq)
    dx, d_gn, d_win, d_qn, d_wq, d_kvn, d_wk, d_wv = _ab_in_bwd(
        x, posb, w, dq.reshape(t, hp), dk.reshape(t, hp), dv.reshape(t, hp), dxl, dgate, dh, tm_mid)
    return loss, dx, dict(ab_norm=d_gn, W_in=d_win, ab_q_norm=d_qn, Wq=d_wq, ab_kv_norm=d_kvn, Wk=d_wk, Wv=d_wv)


def _block_diag(wg):
    g, n, _ = wg.shape
    return jnp.einsum('gij,gh->gihj', wg, jnp.eye(g, dtype=wg.dtype)).reshape(g * n, g * n)


def _prepare_out(w_out):
    d = w_out.shape[2]
    mla = HEADS * QK_NOPE
    return {'Wo_a': jnp.pad(w_out[0, :mla].reshape(HEADS, QK_NOPE, d), ((0, 0), (0, HEAD_PAD - QK_NOPE), (0, 0))).reshape(HEADS * HEAD_PAD, d),
            'Wo_b': w_out[0, mla:]}


def _prepare(full):
    d = full['ab_w_in'].shape[1]
    w_in = full['ab_w_in'][0]
    zeros = lambda n: jnp.zeros((d, n), w_in.dtype)
    wq = full['ab_w_q_b'][0].reshape(Q_LORA, HEADS, QK_NOPE + QK_ROPE)
    wkv = full['ab_w_kv_b'][0].reshape(KV_LORA, HEADS, 2 * QK_NOPE)
    pad_head = lambda a: jnp.pad(a, ((0, 0), (0, 0), (0, HEAD_PAD - a.shape[2]))).reshape(a.shape[0], HEADS * HEAD_PAD)
    w = {
        'W_in': jnp.concatenate([w_in[:, :Z_KPE], zeros(QK_NOPE), w_in[:, Z_KPE:Z_KPE + QK_ROPE],
                                 zeros(HEAD_PAD - QK_NOPE - QK_ROPE), w_in[:, Z_KPE + QK_ROPE:]], axis=1),
        'Wq': pad_head(wq), 'Wk': pad_head(wkv[:, :, :QK_NOPE]), 'Wv': pad_head(wkv[:, :, QK_NOPE:]),
        'Wa': _bf(_block_diag(full['ab_w_rg_a'][0])), 'Wx': _bf(_block_diag(full['ab_w_rg_x'][0])),
        'c_w_s': full['c_w_s'][0],
        'bsT': jnp.pad(full['c_b_s'][0].T, ((0, 0), (0, LANES - SGU_GROUPS))),
        'ffn_norm': [full['ffn_norm'][l:l + 1] for l in range(2)], 'ffn_conv_w': [full['ffn_conv_w'][l] for l in range(2)],
        'ffn_conv_b': [full['ffn_conv_b'][l:l + 1] for l in range(2)],
        'ab_conv_w': full['ab_conv_w'][0], 'final_norm': full['final_norm'][None, :],
    }
    for name in ('ab_norm', 'ab_q_norm', 'ab_kv_norm', 'ab_conv_b', 'ab_b_rg_a', 'ab_b_rg_x', 'ab_lambda', 'c_norm', 'c_ln_g', 'c_ln_b'):
        w[name] = full[name]
    return w


def _unprepare(g):
    unpad_head = lambda a, n: a.reshape(a.shape[0], HEADS, HEAD_PAD)[:, :, :n]
    diag = lambda a: jnp.einsum('gigj->gij', a.reshape(HEADS, LRU_W // HEADS, HEADS, LRU_W // HEADS))
    rules = {
        'ab_w_in': (('W_in',), lambda a: jnp.concatenate([a[:, :Z_KPE], a[:, Z_KPE + QK_NOPE:Z_KPE + QK_NOPE + QK_ROPE], a[:, Z_LRU:]], axis=1)[None]),
        'ab_w_q_b': (('Wq',), lambda a: unpad_head(a, QK_NOPE + QK_ROPE).reshape(1, Q_LORA, -1)),
        'ab_w_kv_b': (('Wk', 'Wv'), lambda a, b: jnp.concatenate([unpad_head(a, QK_NOPE), unpad_head(b, QK_NOPE)], axis=2).reshape(1, KV_LORA, -1)),
        'ab_w_out': (('Wo_a', 'Wo_b'), lambda a, b: jnp.concatenate(
            [a.reshape(HEADS, HEAD_PAD, -1)[:, :QK_NOPE].reshape(HEADS * QK_NOPE, -1), b], axis=0)[None]),
        'ab_w_rg_a': (('Wa',), lambda a: diag(a)[None]), 'ab_w_rg_x': (('Wx',), lambda a: diag(a)[None]),
        'c_w_in': (('c_w_in',), lambda a: a[None]), 'c_w_out': (('c_w_out',), lambda a: a[None]), 'c_w_s': (('c_w_s',), lambda a: a[None]),
        'c_b_s': (('bsT',), lambda a: a[:, :SGU_GROUPS].T[None]),
        'ffn_w_gate': (('Wg',), jnp.stack), 'ffn_w_up': (('Wu',), jnp.stack), 'ffn_w_down': (('Wd',), jnp.stack),
        'ffn_norm': (('ffn_norm',), lambda a: jnp.concatenate(a, axis=0)), 'ffn_conv_w': (('ffn_conv_w',), jnp.stack),
        'ffn_conv_b': (('ffn_conv_b',), lambda a: jnp.concatenate(a, axis=0)),
        'ab_conv_w': (('ab_conv_w',), lambda a: a[None]), 'final_norm': (('final_norm',), lambda a: a[0]),
    }
    for name in ('ab_norm', 'ab_q_norm', 'ab_kv_norm', 'ab_conv_b', 'ab_b_rg_a', 'ab_b_rg_x', 'ab_lambda', 'c_norm', 'c_ln_g', 'c_ln_b'):
        rules[name] = ((name,), lambda a: a)
    return {name: fn(*[g[k] for k in keys]) for name, (keys, fn) in rules.items() if all(k in g for k in keys)}


SLAB_ROWS = 16


def _round_up(n, m):
    return -(-n // m) * m


def _to_chunks(full, axis):
    s = full.shape
    return jnp.moveaxis(full.reshape(s[:axis] + (N_DEV, s[axis] // N_DEV) + s[axis + 1:]), axis, 0)


def _from_chunks(chunks, axis):
    local = chunks.shape[1:]
    return jnp.moveaxis(chunks, 0, axis).reshape(local[:axis] + (N_DEV * local[axis],) + local[axis + 1:])


def _merge_columns(landed, name):
    _, _, r, n = landed.shape
    tr = r // 4

    def body(l_ref, o_ref):
        o_ref[0] = jnp.concatenate([l_ref[dev, 0] for dev in range(N_DEV)], axis=1)

    return pl.pallas_call(body, name="merge_" + name, grid=(r // tr,),
                          in_specs=[pl.BlockSpec((N_DEV, 1, tr, n), lambda i: (0, 0, i, 0))],
                          out_specs=pl.BlockSpec((1, tr, N_DEV * n), lambda i: (0, i, 0)),
                          out_shape=jax.ShapeDtypeStruct((1, r, N_DEV * n), landed.dtype), compiler_params=_params())(landed)


def _split_chunks(whole, axis, name):
    _, rows, cols = whole.shape
    if axis == 1:
        r = rows // N_DEV

        def body(x_ref, o_ref):
            o_ref[0] = _bf(x_ref[...])

        grid, out_shape = (N_DEV,), (N_DEV, 1, r, cols)
        spec, out_spec = pl.BlockSpec((1, r, cols), lambda dev: (0, dev, 0)), pl.BlockSpec((1, 1, r, cols), lambda dev: (dev, 0, 0, 0))
    else:
        n, tr = cols // N_DEV, rows // 4

        def body(x_ref, o_ref):
            x = x_ref[0]
            for dev in range(N_DEV):
                o_ref[dev, 0] = _bf(x[:, dev * n:(dev + 1) * n])

        grid, out_shape = (rows // tr,), (N_DEV, 1, rows, n)
        spec, out_spec = pl.BlockSpec((1, tr, cols), lambda i: (0, i, 0)), pl.BlockSpec((N_DEV, 1, tr, n), lambda i: (0, 0, i, 0))
    return pl.pallas_call(body, name="split_" + name, grid=grid, in_specs=[spec], out_specs=out_spec,
                          out_shape=jax.ShapeDtypeStruct(out_shape, BF16), compiler_params=_params())(whole)


def _slab_rows(n):
    return _round_up(-(-n // LANES), SLAB_ROWS)


def _to_slab(a, lead):
    a = a.reshape(lead + (-1,))
    rows = _slab_rows(a.shape[-1])
    a = jnp.pad(a, [(0, 0)] * len(lead) + [(0, rows * LANES - a.shape[-1])])
    return a.reshape(lead + (rows, LANES))


def _pack_slabs(parts, lead):
    return jnp.concatenate([_to_slab(p, lead) for p in parts], axis=len(lead))


def _unpack_slabs(packed, shapes):
    lead = packed.shape[:-2]
    out, row = [], 0
    for shape in shapes:
        size = math.prod(shape)
        rows = _slab_rows(size)
        piece = lax.slice_in_dim(packed, row, row + rows, axis=len(lead))
        out.append(piece.reshape(lead + (rows * LANES,))[..., :size].reshape(lead + tuple(shape)))
        row += rows
    return out


HBM = pl.BlockSpec(memory_space=pl.ANY)


def _other_chips(x, y):
    return [(1 - x, y), (x, 1 - y), (1 - x, 1 - y)]


def _all_gather(blocks):
    n = len(blocks)

    def body(*refs):
        x_refs, out_refs, token = refs[:n], refs[n:2 * n], refs[2 * n]
        send_sems, recv_sems, local_sems = refs[2 * n + 1:]
        token[...] = jnp.zeros_like(token)
        x, y, c = lax.axis_index("x"), lax.axis_index("y"), lax.axis_index("c")
        me, sibling = (x, y, c), (x, y, 1 - c)
        chips = _other_chips(x, y)

        def slab(a, px, py, pc):
            return out_refs[a].at[4 * px + 2 * py + pc]

        def copy(a, k, blk, to, src=None):
            return pltpu.make_async_remote_copy(src_ref=slab(a, *blk) if src is None else src, dst_ref=slab(a, *blk),
                                                send_sem=send_sems.at[7 * a + k], recv_sem=recv_sems.at[7 * a + k],
                                                device_id=to, device_id_type=MESH)

        mine = [pltpu.make_async_copy(x_refs[a], slab(a, *me), local_sems.at[a]) for a in range(n)]
        started = []
        for a in range(n):
            mine[a].start()
            started.append(copy(a, 0, me, sibling, src=x_refs[a]))
            started += [copy(a, 1 + j, me, (*chip, c), src=x_refs[a]) for j, chip in enumerate(chips)]
        for cp in started:
            cp.start()
        for j, chip in enumerate(chips):
            for a in range(n):
                copy(a, 1 + j, (*chip, c), me).wait_recv()
                passed = copy(a, 4 + j, (*chip, c), sibling)
                passed.start()
                started.append(passed)
        for a in range(n):
            copy(a, 0, sibling, me).wait_recv()
        for j, chip in enumerate(chips):
            for a in range(n):
                copy(a, 4 + j, (*chip, 1 - c), me).wait_recv()
        for cp in started:
            cp.wait_send()
        for a in range(n):
            mine[a].wait()

    out = pl.pallas_call(
        body, name="all_gather_weights",
        out_shape=[jax.ShapeDtypeStruct((N_DEV,) + b.shape, b.dtype) for b in blocks] + [jax.ShapeDtypeStruct((8, LANES), F32)],
        in_specs=[HBM] * n, out_specs=[HBM] * n + [pl.BlockSpec(memory_space=pltpu.VMEM)],
        scratch_shapes=[pltpu.SemaphoreType.DMA((7 * n,)), pltpu.SemaphoreType.DMA((7 * n,)), pltpu.SemaphoreType.DMA((n,))],
    )(*blocks)
    return list(out[:n]), out[n][0, 0]


FLIPS = [(0, 0, 1), (1, 0, 0), (1, 0, 1), (0, 1, 0), (0, 1, 1), (1, 1, 0), (1, 1, 1)]


def _peers(x, y, c):
    flip = lambda v, f: 1 - v if f else v
    return [(flip(x, fx), flip(y, fy), flip(c, fc)) for fx, fy, fc in FLIPS]


def _direct_copies(src_refs, land_refs, send_sems, recv_sems, scatter):
    x, y, c = lax.axis_index("x"), lax.axis_index("y"), lax.axis_index("c")
    me = 4 * x + 2 * y + c
    starts, waits = [], []
    for a in range(len(src_refs)):
        for k, (px, py, pc) in enumerate(_peers(x, y, c)):
            peer = 4 * px + 2 * py + pc
            sems = dict(send_sem=send_sems.at[7 * a + k], recv_sem=recv_sems.at[7 * a + k], device_id=(px, py, pc), device_id_type=MESH)
            src = src_refs[a].at[peer] if scatter else src_refs[a]
            starts.append(pltpu.make_async_remote_copy(src_ref=src, dst_ref=land_refs[a].at[me], **sems))
            waits.append(pltpu.make_async_remote_copy(src_ref=src, dst_ref=land_refs[a].at[peer], **sems))
    n = len(src_refs)
    keeps = [] if scatter else [pltpu.make_async_copy(src_refs[a], land_refs[a].at[me], send_sems.at[7 * n + a]) for a in range(n)]
    return starts, waits, keeps


def _landing(src, scatter):
    block = src.shape[1:] if scatter else src.shape
    return jax.ShapeDtypeStruct((N_DEV,) + block, src.dtype)


HBM_SPACE = pl.BlockSpec(memory_space=pltpu.HBM)
SEMAPHORES = pl.BlockSpec(memory_space=pltpu.SEMAPHORE)
SPLIT_EFFECT = pltpu.SideEffectType.DATAFLOW_SIDE_EFFECTING


def _start_exchange(name, srcs, scatter):
    n = len(srcs)
    lands = [lax.empty(s.shape, s.dtype) for s in (_landing(s, scatter) for s in srcs)]

    def body(*refs):
        starts, _, keeps = _direct_copies(refs[:n], refs[n:2 * n], refs[2 * n], refs[2 * n + 1], scatter)
        for cp in starts + keeps:
            cp.start()
        refs[-1][...] = jnp.zeros_like(refs[-1])

    held = [pltpu.with_memory_space_constraint(a, pltpu.HBM) for a in list(srcs) + lands]
    out = pl.pallas_call(
        body, name=name + "_start",
        out_shape=(pltpu.SemaphoreType.DMA(((7 if scatter else 8) * n,)), pltpu.SemaphoreType.DMA((7 * n,)),
                   *[pltpu.HBM(a.shape, a.dtype) for a in held],
                   jax.ShapeDtypeStruct((8, LANES), F32)),
        in_specs=[HBM_SPACE] * (2 * n), out_specs=(SEMAPHORES, SEMAPHORES, *[HBM_SPACE] * (2 * n), pl.BlockSpec(memory_space=pltpu.VMEM)),
        input_output_aliases={i: 2 + i for i in range(2 * n)},
        compiler_params=pltpu.CompilerParams(has_side_effects=SPLIT_EFFECT),
    )(*held)
    return out[0], out[1], list(out[2:2 + n]), list(out[2 + n:2 + 2 * n]), out[-1][0, 0], out[-1]


def _wait_exchange(name, started, after, scatter):
    send_sems, recv_sems, srcs, lands = started[:4]
    n = len(srcs)

    def body(*refs):
        _, waits, keeps = _direct_copies(refs[:n], refs[n:2 * n], refs[2 * n], refs[2 * n + 1], scatter)
        for cp in waits:
            cp.wait_send()
        for cp in waits:
            cp.wait_recv()
        for cp in keeps:
            cp.wait()

    out = pl.pallas_call(
        body, name=name + "_wait", out_shape=tuple(pltpu.HBM(a.shape, a.dtype) for a in srcs + lands),
        in_specs=[HBM_SPACE] * (2 * n) + [SEMAPHORES, SEMAPHORES, HBM], out_specs=tuple([HBM_SPACE] * (2 * n)),
        input_output_aliases={i: i for i in range(2 * n)},
        compiler_params=pltpu.CompilerParams(has_side_effects=SPLIT_EFFECT),
    )(*srcs, *lands, send_sems, recv_sems, after)
    return list(out[:n]), list(out[n:])


def _row_tile(rows):
    return rows // 2 if (rows // 2) % SLAB_ROWS == 0 else rows


def _sum_in_device_order(me_ref, l_ref, own_ref):
    mine = own_ref[0].astype(F32)
    g = jnp.where(me_ref[0] == 0, mine, l_ref[0].astype(F32))
    for dev in range(1, N_DEV):
        g = g + jnp.where(me_ref[0] == dev, mine, l_ref[dev].astype(F32))
    return g


def _adamw(g, w, m, v):
    m_new = ADAM_B1 * m + (1.0 - ADAM_B1) * g
    v_new = ADAM_B2 * v + (1.0 - ADAM_B2) * (g * g)
    m_hat = m_new * (1.0 / (1.0 - ADAM_B1 ** ADAM_STEP))
    v_hat = v_new * (1.0 / (1.0 - ADAM_B2 ** ADAM_STEP))
    return -ADAM_LR * (m_hat / (jnp.sqrt(v_hat) + ADAM_EPS) + ADAM_WD * w), m_new, v_new


def _sum_chunks(me, landed, own, name):
    _, _, r, n = landed.shape

    def body(me_ref, l_ref, own_ref, g_out):
        g_out[...] = _sum_in_device_order(me_ref, l_ref, own_ref)[0]

    return pl.pallas_call(
        body, name="sum_" + name,
        grid_spec=pltpu.PrefetchScalarGridSpec(
            num_scalar_prefetch=1, grid=(1,),
            in_specs=[pl.BlockSpec((N_DEV, 1, r, n), lambda i, me_ref: (0, 0, 0, 0)),
                      pl.BlockSpec((1, 1, r, n), lambda i, me_ref: (me_ref[0], 0, 0, 0))],
            out_specs=pl.BlockSpec((r, n), lambda i, me_ref: (0, 0))),
        out_shape=_sds((r, n)), compiler_params=_params())(me, landed, own)


def _adamw_small(gs, ws, ms, vs):
    n = len(gs)

    def body(*refs):
        ins, outs = refs[:4 * n], refs[4 * n:]
        for i in range(n):
            outs[i][...], outs[n + i][...], outs[2 * n + i][...] = _adamw(*[ins[k * n + i][...] for k in range(4)])

    out = pl.pallas_call(body, name="adamw_small", out_shape=[_sds(w.shape) for w in ws] * 3)(*gs, *ws, *ms, *vs)
    return out[:n], out[n:2 * n], out[2 * n:]


def _sum_and_adamw(me, landed, own, wts, m, v, name, layer=None, into=None):
    layers, r, n = wts.shape
    first = 0 if layer is None else layer
    count = layers if layer is None else 1
    tr = _row_tile(r)
    blk = pl.BlockSpec((1, tr, n), lambda li, ri, me_ref: (first + li, ri, 0))
    held = [] if into is None else list(into)

    def body(me_ref, l_ref, own_ref, w_ref, m_ref, v_ref, *rest):
        g_out, d_out, m_out, v_out = rest[len(held):]
        g = _sum_in_device_order(me_ref, l_ref, own_ref)
        g_out[...] = g
        d_out[...], m_out[...], v_out[...] = _adamw(g, w_ref[...], m_ref[...], v_ref[...])

    return pl.pallas_call(
        body, name="adamw_" + name,
        grid_spec=pltpu.PrefetchScalarGridSpec(
            num_scalar_prefetch=1, grid=(count, r // tr),
            in_specs=[pl.BlockSpec((N_DEV, 1, tr, n), lambda li, ri, me_ref: (0, li, ri, 0)),
                      pl.BlockSpec((1, 1, tr, n), lambda li, ri, me_ref: (me_ref[0], li, ri, 0)), blk, blk, blk] + [HBM] * len(held),
            out_specs=[blk] * 4),
        out_shape=[_sds((layers, r, n))] * 4, input_output_aliases={6 + i: i for i in range(len(held))},
        compiler_params=_params(2))(me, landed, own, wts, m, v, *held)


EARLY = ['ab_w_in']
LATE_STAGES = {
    'out0': [('ab_w_out', None, 'ab_w_out')],
    'ffn0': [('ffn_w_gate', 0, 'Wg'), ('ffn_w_up', 0, 'Wu'), ('ffn_w_down', 0, 'Wd')],
    'mix1': [('c_w_in', None, 'c_w_in'), ('c_w_out', None, 'c_w_out')],
    'ffn1': [('ffn_w_gate', 1, 'Wg'), ('ffn_w_up', 1, 'Wu'), ('ffn_w_down', 1, 'Wd')],
}
TRANSPOSED = ('ffn_w_gate', 'ffn_w_up')


def _stored(name, a):
    return jnp.swapaxes(a, 1, 2) if name in TRANSPOSED else a


def _stored_axis(name):
    return 1 if name in TRANSPOSED else SHARD_AXIS[name]


GRAD_STAGES = {
    'late1': ([('c_w_in', None), ('c_w_out', None), ('ffn_w_gate', 1), ('ffn_w_up', 1), ('ffn_w_down', 1)],
              ['c_norm', 'c_ln_g', 'c_ln_b', 'c_w_s', 'c_b_s', 'final_norm']),
    'late0': ([('ffn_w_gate', 0), ('ffn_w_up', 0), ('ffn_w_down', 0)], ['ffn_norm', 'ffn_conv_w', 'ffn_conv_b']),
    'mid': ([('ab_w_out', None)], ['ab_conv_w', 'ab_conv_b', 'ab_w_rg_a', 'ab_b_rg_a', 'ab_w_rg_x', 'ab_b_rg_x', 'ab_lambda']),
    'last': ([('ab_w_in', None)], ['ab_norm', 'ab_q_norm', 'ab_w_q_b', 'ab_kv_norm', 'ab_w_kv_b']),
}


def _gather_early(local):
    small = [_bf(local[n]) if n in MATRICES else lax.bitcast_convert_type(local[n], BF16) for n in SMALL_SHARDED]
    gathered, zero = _all_gather([_bf(local[n]) for n in EARLY] + [_pack_slabs(small, ())])
    full = {n: local[n] for n in REPLICATED}
    for n, g in zip(EARLY, gathered):
        full[n] = _from_chunks(g, SHARD_AXIS[n])
    for n, p in zip(SMALL_SHARDED, _unpack_slabs(gathered[-1], [s.shape for s in small])):
        full[n] = _from_chunks(p if n in MATRICES else lax.bitcast_convert_type(p, F32), SHARD_AXIS[n])
    return full, zero


def kernel(x, positions, ab_norm, ab_w_in, ab_q_norm, ab_w_q_b, ab_kv_norm, ab_w_kv_b, ab_conv_w, ab_conv_b, ab_w_rg_a, ab_b_rg_a, ab_w_rg_x, ab_b_rg_x, ab_lambda, ab_w_out, c_norm, c_w_in, c_ln_g, c_ln_b, c_w_s, c_b_s, c_w_out, ffn_norm, ffn_w_gate, ffn_w_up, ffn_conv_w, ffn_conv_b, ffn_w_down, final_norm, loss_target, m_ab_norm, m_ab_w_in, m_ab_q_norm, m_ab_w_q_b, m_ab_kv_norm, m_ab_w_kv_b, m_ab_conv_w, m_ab_conv_b, m_ab_w_rg_a, m_ab_b_rg_a, m_ab_w_rg_x, m_ab_b_rg_x, m_ab_lambda, m_ab_w_out, m_c_norm, m_c_w_in, m_c_ln_g, m_c_ln_b, m_c_w_s, m_c_b_s, m_c_w_out, m_ffn_norm, m_ffn_w_gate, m_ffn_w_up, m_ffn_conv_w, m_ffn_conv_b, m_ffn_w_down, m_final_norm, v_ab_norm, v_ab_w_in, v_ab_q_norm, v_ab_w_q_b, v_ab_kv_norm, v_ab_w_kv_b, v_ab_conv_w, v_ab_conv_b, v_ab_w_rg_a, v_ab_b_rg_a, v_ab_w_rg_x, v_ab_b_rg_x, v_ab_lambda, v_ab_w_out, v_c_norm, v_c_w_in, v_c_ln_g, v_c_ln_b, v_c_w_s, v_c_b_s, v_c_w_out, v_ffn_norm, v_ffn_w_gate, v_ffn_w_up, v_ffn_conv_w, v_ffn_conv_b, v_ffn_w_down, v_final_norm):
    given = dict(locals())
    local = {n: given[n] for n in WEIGHTS}
    b, seq, d = x.shape
    t = b * seq

    me = (4 * lax.axis_index("x") + 2 * lax.axis_index("y") + lax.axis_index("c")).astype(jnp.int32)
    me1 = me.reshape(1)

    full, zero = _gather_early(local)
    gathers = {}
    for stage, members in LATE_STAGES.items():
        srcs = [_bf(_stored(n, local[n] if layer is None else local[n][layer:layer + 1]) + zero) for n, layer, _ in members]
        gathers[stage] = _start_exchange('gather_' + stage, srcs, scatter=False)
        zero = gathers[stage][4]
    w = _prepare(full)
    w['ab_norm'] = w['ab_norm'] + zero

    def late_weights(stage, after):
        _, lands = _wait_exchange('gather_' + stage, gathers[stage], after, scatter=False)
        whole = [l.reshape(1, -1, l.shape[-1]) if _stored_axis(n) == 1 else _merge_columns(l, n)
                 for (n, _, _), l in zip(LATE_STAGES[stage], lands)]
        if stage == 'out0':
            return _prepare_out(whole[0])
        return {key: a[0] for (_, _, key), a in zip(LATE_STAGES[stage], whole)}

    scatters = {}

    def start_scatter(stage, g):
        whole = _unprepare(g)
        big, small = GRAD_STAGES[stage]
        slab = [_to_chunks(whole[n], SHARD_AXIS[n]) if n in SHARD_AXIS else jnp.broadcast_to(whole[n][None], (N_DEV,) + whole[n].shape)
                for n in small]
        own = [whole[n].reshape(N_DEV, 1, whole[n].shape[1] // N_DEV, whole[n].shape[2])
               if whole[n].dtype == BF16 and _stored_axis(n) == 1 else
               _split_chunks(whole[n], _stored_axis(n), n + ('' if layer is None else str(layer))) for n, layer in big]
        own.append(_bf(_pack_slabs(slab, (N_DEV,)))[:, None])
        scatters[stage] = _start_exchange('scatter_' + stage, own, scatter=True)
        return scatters[stage][4]

    posb = jnp.broadcast_to(positions.astype(F32).reshape(t, 1), (t, LANES))
    loss, dx, grads = _local_step(x.reshape(t, d), posb, loss_target.reshape(t, d), w, seq, late_weights, start_scatter)
    start_scatter('last', grads)
    after = scatters['last'][5]

    updated, small_grads = {}, {}
    for stage, (big, small) in GRAD_STAGES.items():
        owns, landed = _wait_exchange('scatter_' + stage, scatters[stage], after, scatter=True)
        for (n, layer), own, land in zip(big, owns, landed):
            updated[n] = _sum_and_adamw(me1, land, own, _stored(n, given[n]), _stored(n, given['m_' + n]), _stored(n, given['v_' + n]),
                                        n + ('' if layer is None else str(layer)), layer, updated.get(n))
        summed = _sum_chunks(me1, landed[-1], owns[-1], stage)
        small_grads.update(zip(small, _unpack_slabs(summed, [local[n].shape for n in small])))
        after = sum([updated[n][1][:1, :1, :1] for n, _ in big], summed[:1, :1].reshape(1, 1, 1))
    names = list(small_grads)
    news = _adamw_small([small_grads[n] for n in names], *[[given[p + n] for n in names] for p in ('', 'm_', 'v_')])
    for i, n in enumerate(names):
        updated[n] = [small_grads[n], news[0][i], news[1][i], news[2][i]]
    total = lax.psum(loss[0, 0], ("x", "y", "c"))
    return (total, dx.reshape(b, seq, d), *[_stored(n, updated[n][kind]) for kind in range(4) for n in WEIGHTS])
```

```python
import math

import jax
import jax.numpy as jnp
from jax import lax
from jax.experimental import pallas as pl
from jax.experimental.pallas import tpu as pltpu

F32 = jnp.float32
BF16 = jnp.bfloat16
MESH = pl.DeviceIdType.MESH

N_DEV = 8
LANES = 128
HALO = 8
VMEM_LIMIT = 56 << 20

NORM_EPS = 1e-6
HEADS = 8
HEAD_PAD = 128
QK_NOPE = 64
QK_ROPE = 32
ROPE_HALF = 16
ROPE_BASE = 10000.0
ATTN_SCALE = (QK_NOPE + QK_ROPE) ** -0.5
LRU_C = 8.0
LRU_W = 512
CHUNK = 128
SGU_GROUPS = 8
D_FF = 2816
FF_BLOCKS = 2

ADAM_LR, ADAM_B1, ADAM_B2, ADAM_EPS, ADAM_WD, ADAM_STEP = 0.001, 0.9, 0.999, 1e-08, 0.01, 10

WEIGHTS = ['ab_norm', 'ab_w_in', 'ab_q_norm', 'ab_w_q_b', 'ab_kv_norm', 'ab_w_kv_b', 'ab_conv_w', 'ab_conv_b',
           'ab_w_rg_a', 'ab_b_rg_a', 'ab_w_rg_x', 'ab_b_rg_x', 'ab_lambda', 'ab_w_out', 'c_norm', 'c_w_in', 'c_ln_g',
           'c_ln_b', 'c_w_s', 'c_b_s', 'c_w_out', 'ffn_norm', 'ffn_w_gate', 'ffn_w_up', 'ffn_conv_w', 'ffn_conv_b',
           'ffn_w_down', 'final_norm']
SHARD_AXIS = {'ab_w_in': 2, 'ab_w_q_b': 2, 'ab_w_kv_b': 2, 'ab_conv_w': 2, 'ab_w_out': 1, 'c_norm': 1, 'c_w_in': 2,
              'c_ln_g': 1, 'c_ln_b': 1, 'c_w_out': 1, 'ffn_w_gate': 2, 'ffn_w_up': 2, 'ffn_conv_w': 2, 'ffn_w_down': 1}
MATRICES = ['ab_w_in', 'ab_w_q_b', 'ab_w_kv_b', 'ab_w_out', 'c_w_in', 'c_w_out', 'ffn_w_gate', 'ffn_w_up', 'ffn_w_down']
BIG = ['ab_w_in', 'c_w_in', 'ffn_w_gate', 'ffn_w_up', 'ab_w_out', 'c_w_out', 'ffn_w_down']
REPLICATED = [n for n in WEIGHTS if n not in SHARD_AXIS]
SMALL_SHARDED = [n for n in WEIGHTS if n in SHARD_AXIS and n not in BIG]


def _bf(x):
    return x.astype(BF16)


def _nn(a, b):
    return lax.dot_general(_bf(a), _bf(b), (((1,), (0,)), ((), ())), preferred_element_type=F32)


def _nt(a, b):
    return lax.dot_general(_bf(a), _bf(b), (((1,), (1,)), ((), ())), preferred_element_type=F32)


def _tn(a, b):
    return lax.dot_general(_bf(a), _bf(b), (((0,), (0,)), ((), ())), preferred_element_type=F32)


def _rms(x, g):
    return x * lax.rsqrt(jnp.mean(x * x, axis=-1, keepdims=True) + NORM_EPS) * g


def _layer_norm(x, g, b):
    xc = x - jnp.mean(x, axis=-1, keepdims=True)
    return xc * lax.rsqrt(jnp.mean(xc * xc, axis=-1, keepdims=True) + NORM_EPS) * g + b


def _gelu(x):
    return jax.nn.gelu(x)


STRIP = 16
STRIP_LANES = 384
GELU_C = math.sqrt(2.0 / math.pi)
GELU_A = 0.044715


def _gelu_and_grad(x):
    x2 = x * x
    t = jnp.tanh(x * (GELU_C + (GELU_C * GELU_A) * x2))
    half_x = 0.5 * x
    one_plus_t = 1.0 + t
    return half_x * one_plus_t, 0.5 * one_plus_t + half_x * (1.0 - t * t) * (GELU_C + (3.0 * GELU_C * GELU_A) * x2)


def _colsum(x):
    return jnp.sum(x, axis=0, keepdims=True)


def _softplus(x):
    return jnp.maximum(x, 0.0) + jnp.log1p(jnp.exp(-jnp.abs(x)))


@jax.custom_vjp
def _decay(x):
    a = jnp.exp(x)
    y = 2.0 * x
    series = -y * (1.0 + y * (1 / 2 + y * (1 / 6 + y * (1 / 24 + y * (1 / 120 + y * (1 / 720))))))
    return a, jnp.where(y < -0.3, 1.0 - a * a, series)


def _decay_fwd(x):
    a, gap = _decay(x)
    return (a, gap), a


def _decay_bwd(a, cts):
    return (a * (cts[0] - 2.0 * a * cts[1]),)


_decay.defvjp(_decay_fwd, _decay_bwd)


def _accumulate(ref, val, first):
    @pl.when(first)
    def _():
        ref[...] = val

    @pl.when(jnp.logical_not(first))
    def _():
        ref[...] += val


def _params(n_axes=1):
    return pltpu.CompilerParams(dimension_semantics=("arbitrary",) * n_axes, vmem_limit_bytes=VMEM_LIMIT)


def _row(tm, n):
    return pl.BlockSpec((tm, n), lambda i: (i, 0))


def _const(shape):
    nd = len(shape)
    return pl.BlockSpec(shape, lambda i: (0,) * nd, pipeline_mode=pl.Buffered(1))


def _prev_halo(tm, n):
    return pl.BlockSpec((HALO, n), lambda i: (jnp.maximum(i * (tm // HALO) - 1, 0), 0))


def _next_halo(tm, n, n_tiles):
    last = n_tiles * (tm // HALO) - 1
    return pl.BlockSpec((HALO, n), lambda i: (jnp.minimum((i + 1) * (tm // HALO), last), 0))


def _sds(shape, dtype=F32):
    return jax.ShapeDtypeStruct(shape, dtype)


def _rope_tables(posb):
    lane = lax.broadcasted_iota(jnp.int32, posb.shape, 1)
    in_rope = jnp.logical_and(lane >= QK_NOPE, lane < QK_NOPE + QK_ROPE)
    j = (lane & (ROPE_HALF - 1)).astype(F32)
    inv_freq = jnp.exp((-math.log(ROPE_BASE)) * j / ROPE_HALF)
    ang = posb * inv_freq
    return jnp.where(in_rope, jnp.cos(ang), 1.0), jnp.where(in_rope, jnp.sin(ang), 0.0)


def _rot(q):
    n = q.shape[1]
    lane = lax.broadcasted_iota(jnp.int32, q.shape, 1) & (HEAD_PAD - 1)
    first_half = jnp.where(lane >= QK_NOPE, -pltpu.roll(q, n - ROPE_HALF, 1), 0.0)
    second_half = jnp.where(lane < QK_NOPE + QK_ROPE, pltpu.roll(q, ROPE_HALF, 1), 0.0)
    return jnp.where(lane < QK_NOPE + ROPE_HALF, first_half, second_half)


def _rope(q, cos_t, sin_t):
    return q * cos_t + _rot(q) * sin_t


def _rope_transpose(dq, cos_t, sin_t):
    return dq * cos_t - _rot(dq * sin_t)


def _tile_heads(t):
    return jnp.concatenate([t] * HEADS, axis=1)


Q_LORA, KV_LORA = 256, 128
Z_KPE = Q_LORA + KV_LORA
Z_LRU = Z_KPE + HEAD_PAD
Z_GATE = Z_LRU + LRU_W
Z_WIDTH = Z_GATE + LRU_W


def _ab_in_fwd(x, posb, w, tm):
    t, d = x.shape

    def body(x_ref, pos_ref, gn_ref, win_ref, qn_ref, wq_ref, kvn_ref, wk_ref, wv_ref, q_out, k_out, v_out, xl_out, gate_out):
        hn = _rms(x_ref[...], gn_ref[...])
        z = _nn(hn, win_ref[...])
        cqn = _rms(z[:, :Q_LORA], qn_ref[...])
        kvn = _rms(z[:, Q_LORA:Z_KPE], kvn_ref[...])
        cos_t, sin_t = _rope_tables(pos_ref[...])
        q_out[...] = _bf(_rope(_nn(cqn, wq_ref[...]), _tile_heads(cos_t), _tile_heads(sin_t)))
        kpe = _rope(z[:, Z_KPE:Z_LRU], cos_t, sin_t)
        k_out[...] = _bf(_nn(kvn, wk_ref[...]) + _tile_heads(kpe))
        v_out[...] = _bf(_nn(kvn, wv_ref[...]))
        xl_out[...] = z[:, Z_LRU:Z_GATE]
        gate_out[...] = z[:, Z_GATE:]

    hp = HEADS * HEAD_PAD
    return pl.pallas_call(
        body, name="ab_in_fwd", grid=(t // tm,),
        in_specs=[_row(tm, d), _row(tm, LANES), _const((1, d)), _const((d, Z_WIDTH)), _const((1, Q_LORA)), _const((Q_LORA, hp)),
                  _const((1, KV_LORA)), _const((KV_LORA, hp)), _const((KV_LORA, hp))],
        out_specs=[_row(tm, hp), _row(tm, hp), _row(tm, hp), _row(tm, LRU_W), _row(tm, LRU_W)],
        out_shape=[_sds((t, hp), BF16), _sds((t, hp), BF16), _sds((t, hp), BF16), _sds((t, LRU_W)), _sds((t, LRU_W))],
        compiler_params=_params(),
    )(x, posb, w['ab_norm'], w['W_in'], w['ab_q_norm'], w['Wq'], w['ab_kv_norm'], w['Wk'], w['Wv'])


def _ab_in_bwd(x, posb, w, dq, dk, dv, dxl, dgate, dres, tm):
    t, d = x.shape
    hp = HEADS * HEAD_PAD

    def body(x_ref, pos_ref, gn_ref, win_ref, qn_ref, wq_ref, kvn_ref, wk_ref, wv_ref, dq_ref, dk_ref, dv_ref, dxl_ref, dgate_ref,
             dres_ref, dx_out, dgn_out, dwin_out, dqn_out, dwq_out, dkvn_out, dwk_out, dwv_out):
        first = pl.program_id(0) == 0
        hn, vjp_in = jax.vjp(_rms, x_ref[...], gn_ref[...])
        z = _nn(hn, win_ref[...])
        cqn, vjp_q = jax.vjp(_rms, z[:, :Q_LORA], qn_ref[...])
        kvn, vjp_kv = jax.vjp(_rms, z[:, Q_LORA:Z_KPE], kvn_ref[...])
        cos_t, sin_t = _rope_tables(pos_ref[...])
        dq0 = _rope_transpose(dq_ref[...], _tile_heads(cos_t), _tile_heads(sin_t))
        dk0 = dk_ref[...]
        dv0 = dv_ref[...]
        dkpe = dk0[:, :HEAD_PAD]
        for h in range(1, HEADS):
            dkpe = dkpe + dk0[:, h * HEAD_PAD:(h + 1) * HEAD_PAD]
        dkpe = _rope_transpose(dkpe, cos_t, sin_t)
        _accumulate(dwq_out, _tn(cqn, dq0), first)
        _accumulate(dwk_out, _tn(kvn, dk0), first)
        _accumulate(dwv_out, _tn(kvn, dv0), first)
        dcq, dqn = vjp_q(_nt(dq0, wq_ref[...]))
        dckv, dkvn = vjp_kv(_nt(dk0, wk_ref[...]) + _nt(dv0, wv_ref[...]))
        _accumulate(dqn_out, dqn, first)
        _accumulate(dkvn_out, dkvn, first)
        dz = _bf(jnp.concatenate([_bf(dcq), _bf(dckv), _bf(dkpe), dxl_ref[...], dgate_ref[...]], axis=1))
        _accumulate(dwin_out, _tn(hn, dz), first)
        dx, dgn = vjp_in(_nt(dz, win_ref[...]))
        _accumulate(dgn_out, dgn, first)
        dx_out[...] = dx + dres_ref[...]

    return pl.pallas_call(
        body, name="ab_in_bwd", grid=(t // tm,),
        in_specs=[_row(tm, d), _row(tm, LANES), _const((1, d)), _const((d, Z_WIDTH)), _const((1, Q_LORA)), _const((Q_LORA, hp)),
                  _const((1, KV_LORA)), _const((KV_LORA, hp)), _const((KV_LORA, hp)),
                  _row(tm, hp), _row(tm, hp), _row(tm, hp), _row(tm, LRU_W), _row(tm, LRU_W), _row(tm, d)],
        out_specs=[_row(tm, d), _const((1, d)), _const((d, Z_WIDTH)), _const((1, Q_LORA)), _const((Q_LORA, hp)),
                   _const((1, KV_LORA)), _const((KV_LORA, hp)), _const((KV_LORA, hp))],
        out_shape=[_sds((t, d)), _sds((1, d)), _sds((d, Z_WIDTH)), _sds((1, Q_LORA)), _sds((Q_LORA, hp)),
                   _sds((1, KV_LORA)), _sds((KV_LORA, hp)), _sds((KV_LORA, hp))],
        compiler_params=_params(),
    )(x, posb, w['ab_norm'], w['W_in'], w['ab_q_norm'], w['Wq'], w['ab_kv_norm'], w['Wk'], w['Wv'], dq, dk, dv, dxl, dgate, dres)


def _attn_probs(q_blk, k_ext, tq):
    ext = k_ext.shape[0]
    s = lax.dot_general(q_blk, k_ext, (((1,), (1,)), ((), ())), preferred_element_type=F32) * ATTN_SCALE
    causal = lax.broadcasted_iota(jnp.int32, (tq, tq), 1) <= lax.broadcasted_iota(jnp.int32, (tq, tq), 0)
    diag = jnp.where(causal, s[:, ext - tq:], -1e30)
    s = diag if ext == tq else jnp.concatenate([s[:, :ext - tq], diag], axis=1)
    p = jnp.exp(s - jnp.max(s, axis=1, keepdims=True))
    return p / jnp.sum(p, axis=1, keepdims=True)


def _attn_fwd(q, k, v, tq):
    b, s, hp = q.shape
    blk = pl.BlockSpec((1, s, HEAD_PAD), lambda bi, h: (bi, 0, h))

    def body(q_ref, k_ref, v_ref, o_ref, p_ref):
        kb = _bf(k_ref[0])
        vb = _bf(v_ref[0])
        for i in range(s // tq):
            ext = (i + 1) * tq
            p = _bf(_attn_probs(_bf(q_ref[0, i * tq:ext, :]), kb[:ext], tq))
            p_ref[0, 0, i * tq:ext, :ext] = p
            o_ref[0, i * tq:ext, :] = _bf(lax.dot_general(p, vb[:ext], (((1,), (0,)), ((), ())), preferred_element_type=F32))

    return pl.pallas_call(body, name="attn_fwd", grid=(b, HEADS), in_specs=[blk, blk, blk],
                          out_specs=[blk, pl.BlockSpec((1, 1, s, s), lambda bi, h: (bi, h, 0, 0))],
                          out_shape=[_sds((b, s, hp), BF16), _sds((b, HEADS, s, s), BF16)], compiler_params=_params(2))(q, k, v)


def _attn_bwd(q, k, v, probs, do, tq):
    b, s, hp = q.shape
    blk = pl.BlockSpec((1, s, HEAD_PAD), lambda bi, h: (bi, 0, h))

    def body(q_ref, k_ref, v_ref, p_ref, do_ref, dq_ref, dk_ref, dv_ref):
        kb = _bf(k_ref[0])
        vb = _bf(v_ref[0])
        dk_ref[...] = jnp.zeros_like(dk_ref)
        dv_ref[...] = jnp.zeros_like(dv_ref)
        for i in range(s // tq):
            ext = (i + 1) * tq
            qb = _bf(q_ref[0, i * tq:ext, :])
            dob = _bf(do_ref[0, i * tq:ext, :])
            pb = p_ref[0, 0, i * tq:ext, :ext]
            p = pb.astype(F32)
            dv_ref[0, :ext, :] += lax.dot_general(pb, dob, (((0,), (0,)), ((), ())), preferred_element_type=F32)
            dp = lax.dot_general(dob, vb[:ext], (((1,), (1,)), ((), ())), preferred_element_type=F32)
            ds = _bf(p * (dp - jnp.sum(p * dp, axis=1, keepdims=True)) * ATTN_SCALE)
            dq_ref[0, i * tq:ext, :] = lax.dot_general(ds, kb[:ext], (((1,), (0,)), ((), ())), preferred_element_type=F32)
            dk_ref[0, :ext, :] += lax.dot_general(ds, qb, (((0,), (0,)), ((), ())), preferred_element_type=F32)

    return pl.pallas_call(body, name="attn_bwd", grid=(b, HEADS),
                          in_specs=[blk, blk, blk, pl.BlockSpec((1, 1, s, s), lambda bi, h: (bi, h, 0, 0)), blk], out_specs=[blk, blk, blk],
                          out_shape=[_sds((b, s, hp))] * 3, compiler_params=_params(2))(q, k, v, probs, do)


LRU_CONV = 4


def _lru_point(pre_a, pre_x, xc, lam):
    r = jax.nn.sigmoid(pre_a)
    i = jax.nn.sigmoid(pre_x)
    a, gap = _decay(-LRU_C * r * _softplus(-lam))
    return a, jnp.sqrt(gap) * (i * xc)


def _causal_conv(pad_ref, x, halo, first_in_seq, w, taps):
    tm = x.shape[0]
    pad_ref[:HALO, :] = jnp.where(first_in_seq, 0.0, halo)
    pad_ref[HALO:, :] = x
    y = w[taps - 1:taps, :] * x
    for k in range(taps - 1):
        off = HALO - (taps - 1) + k
        y = y + w[k:k + 1, :] * pad_ref[off:off + tm, :]
    return y


def _conv_taps(pad_ref, r, cols, taps):
    blocks = [pad_ref[r + j * HALO:r + (j + 1) * HALO, cols] for j in range(1 + STRIP // HALO)]
    sub = lax.broadcasted_iota(jnp.int32, blocks[0].shape, 0)
    out = []
    for k in range(taps - 1):
        s = taps - 1 - k
        rolled = [pltpu.roll(b, s, 0) for b in blocks]
        out.append(jnp.concatenate([jnp.where(sub < s, rolled[j], rolled[j + 1]) for j in range(STRIP // HALO)], axis=0))
    out.append(jnp.concatenate(blocks[1:], axis=0))
    return out


def _causal_conv_wgrad(pad_ref, dy, taps):
    tm = dy.shape[0]
    return jnp.concatenate([_colsum(dy * pad_ref[HALO - (taps - 1) + k:HALO - (taps - 1) + k + tm, :]) for k in range(taps)], axis=0)


def _causal_conv_transpose(pad_ref, dy, halo_next, last_in_seq, w, taps):
    tm = dy.shape[0]
    pad_ref[:tm, :] = dy
    pad_ref[tm:, :] = jnp.where(last_in_seq, 0.0, halo_next)
    dx = w[taps - 1:taps, :] * dy
    for k in range(taps - 1):
        off = (taps - 1) - k
        dx = dx + w[k:k + 1, :] * pad_ref[off:off + tm, :]
    return dx


def _lru_fwd(xl, gate, w, ts, seq):
    t, n = xl.shape
    tiles_per_seq = seq // ts

    def body(xl_ref, halo_ref, gate_ref, cw_ref, cb_ref, wa_ref, ba_ref, wx_ref, bx_ref, lam_ref, y_out, h_out, pad_ref, a_ref, b_ref, carry_ref):
        first_in_seq = pl.program_id(0) % tiles_per_seq == 0
        xc = _causal_conv(pad_ref, xl_ref[...], halo_ref[...], first_in_seq, cw_ref[...], LRU_CONV) + cb_ref[...]
        a, bx = _lru_point(_nn(xc, wa_ref[...]) + ba_ref[...], _nn(xc, wx_ref[...]) + bx_ref[...], xc, lam_ref[...])
        a_ref[...] = a
        b_ref[...] = bx

        @pl.when(first_in_seq)
        def _():
            carry_ref[...] = jnp.zeros_like(carry_ref)

        def step(r, h):
            h = a_ref[pl.ds(r, 1), :] * h + b_ref[pl.ds(r, 1), :]
            h_out[pl.ds(r, 1), :] = h
            return h

        carry_ref[...] = lax.fori_loop(0, ts, step, carry_ref[...], unroll=8)
        y_out[...] = _bf(h_out[...] * _gelu(gate_ref[...]))

    return pl.pallas_call(
        body, name="lru_fwd", grid=(t // ts,),
        in_specs=[_row(ts, n), _prev_halo(ts, n), _row(ts, n), _const((LRU_CONV, n)), _const((1, n)), _const((n, n)), _const((1, n)),
                  _const((n, n)), _const((1, n)), _const((1, n))],
        out_specs=[_row(ts, n), _row(ts, n)], out_shape=[_sds((t, n), BF16), _sds((t, n))],
        scratch_shapes=[pltpu.VMEM((HALO + ts, n), F32), pltpu.VMEM((ts, n), F32), pltpu.VMEM((ts, n), F32), pltpu.VMEM((1, n), F32)],
        compiler_params=_params(),
    )(xl, xl, gate, w['ab_conv_w'], w['ab_conv_b'], w['Wa'], w['ab_b_rg_a'], w['Wx'], w['ab_b_rg_x'], w['ab_lambda'])


def _lru_bwd(xl, gate, hs, dy, w, ts, seq):
    t, n = xl.shape
    tiles_per_seq = seq // ts
    n_tiles = t // ts

    def rev(i):
        return n_tiles - 1 - i

    row = pl.BlockSpec((ts, n), lambda i: (rev(i), 0))
    prev = pl.BlockSpec((HALO, n), lambda i: (jnp.maximum(rev(i) * (ts // HALO) - 1, 0), 0))
    acc = lambda shape: pl.BlockSpec(shape, lambda i: (0,) * len(shape))

    def body(xl_ref, xhalo_ref, gate_ref, h_ref, hhalo_ref, dy_ref, cw_ref, cb_ref, wa_ref, ba_ref, wx_ref, bx_ref, lam_ref,
             dxl_out, dgate_out, dcw_out, dcb_out, dwa_out, dba_out, dwx_out, dbx_out, dlam_out,
             pad_ref, padh_ref, padd_ref, a_ref, g_ref, carry_ref, dhalo_ref):
        step_id = pl.program_id(0)
        first = step_id == 0
        tile = rev(step_id)
        first_in_seq = tile % tiles_per_seq == 0
        last_in_seq = tile % tiles_per_seq == tiles_per_seq - 1
        cw = cw_ref[...]
        xc = _causal_conv(pad_ref, xl_ref[...], xhalo_ref[...], first_in_seq, cw, LRU_CONV) + cb_ref[...]
        pre_a = _nn(xc, wa_ref[...]) + ba_ref[...]
        pre_x = _nn(xc, wx_ref[...]) + bx_ref[...]
        (a, _), vjp_point = jax.vjp(_lru_point, pre_a, pre_x, xc, lam_ref[...])
        h = h_ref[...]
        _, vjp_out = jax.vjp(lambda h_, g_: h_ * _gelu(g_), h, gate_ref[...])
        dh, dgate = vjp_out(dy_ref[...])
        dgate_out[...] = _bf(dgate)
        a_ref[...] = a
        g_ref[...] = dh

        @pl.when(last_in_seq)
        def _():
            carry_ref[...] = jnp.zeros_like(carry_ref)

        def step(j, c):
            r = ts - 1 - j
            g = g_ref[pl.ds(r, 1), :] + c
            g_ref[pl.ds(r, 1), :] = g
            return a_ref[pl.ds(r, 1), :] * g

        carry_ref[...] = lax.fori_loop(0, ts, step, carry_ref[...], unroll=8)
        g = g_ref[...]
        padh_ref[:HALO, :] = jnp.where(first_in_seq, 0.0, hhalo_ref[...])
        padh_ref[HALO:, :] = h
        dpre_a, dpre_x, dxc, dlam = vjp_point((g * padh_ref[HALO - 1:HALO - 1 + ts, :], g))
        dxc = dxc + _nt(dpre_a, wa_ref[...]) + _nt(dpre_x, wx_ref[...])
        _accumulate(dwa_out, _tn(xc, dpre_a), first)
        _accumulate(dwx_out, _tn(xc, dpre_x), first)
        _accumulate(dba_out, _colsum(dpre_a), first)
        _accumulate(dbx_out, _colsum(dpre_x), first)
        _accumulate(dlam_out, dlam, first)
        _accumulate(dcb_out, _colsum(dxc), first)
        _accumulate(dcw_out, _causal_conv_wgrad(pad_ref, dxc, LRU_CONV), first)
        dxl_out[...] = _bf(_causal_conv_transpose(padd_ref, dxc, dhalo_ref[...], last_in_seq, cw, LRU_CONV))
        dhalo_ref[...] = dxc[:HALO, :]

    return pl.pallas_call(
        body, name="lru_bwd", grid=(n_tiles,),
        in_specs=[row, prev, row, row, prev, row, _const((LRU_CONV, n)), _const((1, n)), _const((n, n)), _const((1, n)),
                  _const((n, n)), _const((1, n)), _const((1, n))],
        out_specs=[row, row, acc((LRU_CONV, n)), acc((1, n)), acc((n, n)), acc((1, n)), acc((n, n)), acc((1, n)), acc((1, n))],
        out_shape=[_sds((t, n), BF16), _sds((t, n), BF16), _sds((LRU_CONV, n)), _sds((1, n)), _sds((n, n)), _sds((1, n)), _sds((n, n)),
                   _sds((1, n)), _sds((1, n))],
        scratch_shapes=[pltpu.VMEM((HALO + ts, n), F32), pltpu.VMEM((HALO + ts, n), F32), pltpu.VMEM((ts + HALO, n), F32),
                        pltpu.VMEM((ts, n), F32), pltpu.VMEM((ts, n), F32), pltpu.VMEM((1, n), F32), pltpu.VMEM((HALO, n), F32)],
        compiler_params=_params(),
    )(xl, xl, gate, hs, hs, dy, w['ab_conv_w'], w['ab_conv_b'], w['Wa'], w['ab_b_rg_a'], w['Wx'], w['ab_b_rg_x'], w['ab_lambda'])


def _ab_out_fwd(x, o, y, w, tm):
    t, d = x.shape
    hp = o.shape[1]

    def body(x_ref, o_ref, y_ref, wa_ref, wb_ref, h_out):
        h_out[...] = x_ref[...] + _nn(o_ref[...], wa_ref[...]) + _nn(y_ref[...], wb_ref[...])

    return pl.pallas_call(body, name="ab_out_fwd", grid=(t // tm,),
                          in_specs=[_row(tm, d), _row(tm, hp), _row(tm, LRU_W), _const((hp, d)), _const((LRU_W, d))],
                          out_specs=_row(tm, d), out_shape=_sds((t, d)), compiler_params=_params())(x, o, y, w['Wo_a'], w['Wo_b'])


def _ab_out_bwd(o, y, dh, w, tm):
    t, d = dh.shape
    hp = o.shape[1]

    def body(o_ref, y_ref, dh_ref, wa_ref, wb_ref, do_out, dy_out, dwa_out, dwb_out):
        first = pl.program_id(0) == 0
        dh_t = dh_ref[...]
        do_out[...] = _bf(_nt(dh_t, wa_ref[...]))
        dy_out[...] = _nt(dh_t, wb_ref[...])
        _accumulate(dwa_out, _tn(o_ref[...], dh_t), first)
        _accumulate(dwb_out, _tn(y_ref[...], dh_t), first)

    return pl.pallas_call(body, name="ab_out_bwd", grid=(t // tm,),
                          in_specs=[_row(tm, hp), _row(tm, LRU_W), _row(tm, d), _const((hp, d)), _const((LRU_W, d))],
                          out_specs=[_row(tm, hp), _row(tm, LRU_W), _const((hp, d)), _const((LRU_W, d))],
                          out_shape=[_sds((t, hp), BF16), _sds((t, LRU_W)), _sds((hp, d)), _sds((LRU_W, d))],
                          compiler_params=_params())(o, y, dh, w['Wo_a'], w['Wo_b'])


FFN_CONV = 3


def _ffn_a_fwd(h, norm, wg, wu, tm):
    t, d = h.shape
    fb = D_FF // FF_BLOCKS

    def body(h_ref, gn_ref, wg_ref, wu_ref, g_out, u_out, hn_out):
        hn = _bf(_rms(h_ref[...], gn_ref[...]))
        hn_out[0] = hn
        g_out[...] = _nt(hn, wg_ref[...])
        u_out[...] = _nt(hn, wu_ref[...])

    wspec = pl.BlockSpec((fb, d), lambda f, i: (f, 0))
    ospec = pl.BlockSpec((tm, fb), lambda f, i: (i, f))
    return pl.pallas_call(
        body, name="ffn_a_fwd", grid=(FF_BLOCKS, t // tm),
        in_specs=[pl.BlockSpec((tm, d), lambda f, i: (i, 0)), pl.BlockSpec((1, d), lambda f, i: (0, 0)), wspec, wspec],
        out_specs=[ospec, ospec, pl.BlockSpec((1, tm, d), lambda f, i: (f, i, 0))],
        out_shape=[_sds((t, D_FF)), _sds((t, D_FF)), _sds((FF_BLOCKS, t, d), BF16)], compiler_params=_params(2))(h, norm, wg, wu)


def _ffn_b_fwd(g, u, h, cw, cb, wd, tm, seq, final=None):
    t, d = h.shape
    tiles_per_seq = seq // tm

    def body(g_ref, halo_ref, u_ref, h_ref, cw_ref, cb_ref, wd_ref, *rest):
        pad_ref, act_ref = rest[-2:]
        pad_ref[:HALO, :] = jnp.where(pl.program_id(0) % tiles_per_seq == 0, 0.0, halo_ref[...])
        pad_ref[HALO:, :] = g_ref[...]
        cw = cw_ref[...]
        cb = cb_ref[...]
        for c0 in range(0, D_FF, STRIP_LANES):
            cols = slice(c0, min(c0 + STRIP_LANES, D_FF))
            for r in range(0, tm, STRIP):
                taps = _conv_taps(pad_ref, r, cols, FFN_CONV)
                gc = cb[:, cols] + cw[0:1, cols] * taps[0] + cw[1:2, cols] * taps[1] + cw[2:3, cols] * taps[2]
                act_ref[r:r + STRIP, cols] = _bf(_gelu(gc) * u_ref[r:r + STRIP, cols])
        h_new = h_ref[...] + _nn(act_ref[...], wd_ref[...])
        if final is None:
            rest[0][...] = h_new
        else:
            tgt_ref, fn_ref, dh_out, loss_out, dfn_out = rest[:5]
            first = pl.program_id(0) == 0
            loss, dh_out[...], dfn = _loss_and_grad(h_new, tgt_ref[...], fn_ref[...])
            _accumulate(loss_out, loss, first)
            _accumulate(dfn_out, dfn, first)

    in_specs = [_row(tm, D_FF), _prev_halo(tm, D_FF), _row(tm, D_FF), _row(tm, d), _const((FFN_CONV, D_FF)), _const((1, D_FF)), _const((D_FF, d))]
    scratch = [pltpu.VMEM((HALO + tm, D_FF), F32), pltpu.VMEM((tm, D_FF), BF16)]
    if final is None:
        return pl.pallas_call(body, name="ffn_b_fwd", grid=(t // tm,), in_specs=in_specs, out_specs=_row(tm, d), out_shape=_sds((t, d)),
                              scratch_shapes=scratch, compiler_params=_params())(g, g, u, h, cw, cb, wd)
    return pl.pallas_call(body, name="ffn_b_fwd_loss", grid=(t // tm,), in_specs=in_specs + [_row(tm, d), _const((1, d))],
                          out_specs=[_row(tm, d), _const((1, 1)), _const((1, d))],
                          out_shape=[_sds((t, d)), _sds((1, 1)), _sds((1, d))],
                          scratch_shapes=scratch, compiler_params=_params())(g, g, u, h, cw, cb, wd, *final)


def _ffn_b_bwd(g, u, dout, cw, cb, wd, tm, seq):
    t, d = dout.shape
    fb = D_FF // FF_BLOCKS
    tiles_per_seq = seq // tm

    def body(g_ref, halo_ref, u_ref, dout_ref, cw_ref, cb_ref, wd_ref, dgc_out, du_out, dwd_out, dcw_out, dcb_out,
             pad_ref, dact_ref, act_ref, acc_ref, dwd_acc):
        i = pl.program_id(1)
        first = i == 0
        pad_ref[:HALO, :] = jnp.where(i % tiles_per_seq == 0, 0.0, halo_ref[...])
        pad_ref[HALO:, :] = g_ref[...]
        dout_b = _bf(dout_ref[...])
        dact_ref[...] = _nt(dout_b, wd_ref[...])
        cw = cw_ref[...]
        cb = cb_ref[...]
        fold = lambda a: a[:HALO] + a[HALO:]
        for c0 in range(0, fb, STRIP_LANES):
            cols = slice(c0, min(c0 + STRIP_LANES, fb))
            sums = [jnp.zeros((HALO, cols.stop - c0), F32) for _ in range(1 + FFN_CONV)]
            for r in range(0, tm, STRIP):
                rows = slice(r, r + STRIP)
                taps = _conv_taps(pad_ref, r, cols, FFN_CONV)
                gelu, dgelu = _gelu_and_grad(cb[:, cols] + cw[0:1, cols] * taps[0] + cw[1:2, cols] * taps[1] + cw[2:3, cols] * taps[2])
                u = u_ref[rows, cols]
                dact = dact_ref[rows, cols]
                act_ref[rows, cols] = _bf(gelu * u)
                du_out[rows, cols] = _bf(dact * gelu)
                dgc = dact * u * dgelu
                dgc_out[rows, cols] = dgc
                sums = [sums[0] + fold(dgc)] + [sums[1 + k] + fold(dgc * taps[k]) for k in range(FFN_CONV)]
            for k in range(1 + FFN_CONV):
                acc_ref[k, :, cols] = sums[k]
        _accumulate(dwd_acc, _tn(act_ref[...], dout_b), first)

        @pl.when(i == t // tm - 1)
        def _():
            dwd_out[...] = _bf(dwd_acc[...])

        _accumulate(dcb_out, _colsum(acc_ref[0]), first)
        _accumulate(dcw_out, jnp.concatenate([_colsum(acc_ref[1 + k]) for k in range(FFN_CONV)], axis=0), first)

    blk = pl.BlockSpec((tm, fb), lambda f, i: (i, f))
    halo = pl.BlockSpec((HALO, fb), lambda f, i: (jnp.maximum(i * (tm // HALO) - 1, 0), f))
    wd_blk = pl.BlockSpec((fb, d), lambda f, i: (f, 0), pipeline_mode=pl.Buffered(1))
    return pl.pallas_call(
        body, name="ffn_b_bwd", grid=(FF_BLOCKS, t // tm),
        in_specs=[blk, halo, blk, pl.BlockSpec((tm, d), lambda f, i: (i, 0)), pl.BlockSpec((FFN_CONV, fb), lambda f, i: (0, f)),
                  pl.BlockSpec((1, fb), lambda f, i: (0, f)), wd_blk],
        out_specs=[blk, blk, wd_blk, pl.BlockSpec((FFN_CONV, fb), lambda f, i: (0, f)),
                   pl.BlockSpec((1, fb), lambda f, i: (0, f))],
        out_shape=[_sds((t, D_FF)), _sds((t, D_FF), BF16), _sds((D_FF, d), BF16), _sds((FFN_CONV, D_FF)), _sds((1, D_FF))],
        scratch_shapes=[pltpu.VMEM((HALO + tm, fb), F32), pltpu.VMEM((tm, fb), F32), pltpu.VMEM((tm, fb), BF16),
                        pltpu.VMEM((1 + FFN_CONV, HALO, fb), F32), pltpu.VMEM((fb, d), F32)],
        compiler_params=_params(2))(g, g, u, dout, cw, cb, wd)


def _ffn_a_dgrad(h, norm, dgc, du, dres, cw, wg, wu, tm, seq):
    t, d = h.shape
    tiles_per_seq = seq // tm
    n_tiles = t // tm

    def body(h_ref, gn_ref, dgc_ref, halo_ref, du_ref, dres_ref, cw_ref, wg_ref, wu_ref, dh_out, dg_out, dgn_out, pad_ref):
        i = pl.program_id(0)
        last_in_seq = i % tiles_per_seq == tiles_per_seq - 1
        dg = _bf(_causal_conv_transpose(pad_ref, dgc_ref[...], halo_ref[...], last_in_seq, cw_ref[...], FFN_CONV))
        dg_out[...] = dg
        _, vjp_norm = jax.vjp(_rms, h_ref[...], gn_ref[...])
        dh, dgn = vjp_norm(_nn(dg, wg_ref[...]) + _nn(du_ref[...], wu_ref[...]))
        dh_out[...] = dh + dres_ref[...]
        _accumulate(dgn_out, dgn, i == 0)

    return pl.pallas_call(
        body, name="ffn_a_dgrad", grid=(n_tiles,),
        in_specs=[_row(tm, d), _const((1, d)), _row(tm, D_FF), _next_halo(tm, D_FF, n_tiles), _row(tm, D_FF), _row(tm, d),
                  _const((FFN_CONV, D_FF)), _const((D_FF, d)), _const((D_FF, d))],
        out_specs=[_row(tm, d), _row(tm, D_FF), _const((1, d))], out_shape=[_sds((t, d)), _sds((t, D_FF), BF16), _sds((1, d))],
        scratch_shapes=[pltpu.VMEM((tm + HALO, D_FF), F32)], compiler_params=_params())(h, norm, dgc, dgc, du, dres, cw, wg, wu)


def _ffn_a_wgrad(hn, dg, du, tm):
    _, t, d = hn.shape
    fb = D_FF // FF_BLOCKS

    n_tiles = t // tm

    def body(hn_ref, dg_ref, du_ref, dwg_out, dwu_out, acc_g, acc_u):
        i = pl.program_id(1)
        hn_t = hn_ref[0]
        _accumulate(acc_g, _tn(dg_ref[...], hn_t), i == 0)
        _accumulate(acc_u, _tn(du_ref[...], hn_t), i == 0)

        @pl.when(i == n_tiles - 1)
        def _():
            dwg_out[...] = _bf(acc_g[...])
            dwu_out[...] = _bf(acc_u[...])

    blk = pl.BlockSpec((tm, fb), lambda f, i: (i, f))
    wspec = pl.BlockSpec((fb, d), lambda f, i: (f, 0), pipeline_mode=pl.Buffered(1))
    return pl.pallas_call(body, name="ffn_a_wgrad", grid=(FF_BLOCKS, n_tiles),
                          in_specs=[pl.BlockSpec((1, tm, d), lambda f, i: (0, i, 0)), blk, blk],
                          out_specs=[wspec, wspec], out_shape=[_sds((D_FF, d), BF16), _sds((D_FF, d), BF16)],
                          scratch_shapes=[pltpu.VMEM((fb, d), F32), pltpu.VMEM((fb, d), F32)],
                          compiler_params=_params(2))(hn, dg, du)


def _sgu_mix(vn, ws_ref, bst):
    tril = lax.broadcasted_iota(jnp.int32, (CHUNK, CHUNK), 0) >= lax.broadcasted_iota(jnp.int32, (CHUNK, CHUNK), 1)
    wms = [jnp.where(tril, ws_ref[g], 0.0) for g in range(SGU_GROUPS)]
    chunks = []
    for n in range(vn.shape[0] // CHUNK):
        vc = vn[n * CHUNK:(n + 1) * CHUNK, :]
        chunks.append(jnp.concatenate(
            [_nn(wms[g], vc[:, g * CHUNK:(g + 1) * CHUNK]) + bst[:, g:g + 1] for g in range(SGU_GROUPS)], axis=1))
    return jnp.concatenate(chunks, axis=0)


def _sgu_fwd(h, w, tm):
    t, d = h.shape

    def body(h_ref, cn_ref, win_ref, lg_ref, lb_ref, ws_ref, bst_ref, wout_ref, h_out):
        h_t = h_ref[...]
        z = _gelu(_nn(_rms(h_t, cn_ref[...]), win_ref[...]))
        vn = _layer_norm(z[:, d:], lg_ref[...], lb_ref[...])
        s = _sgu_mix(vn, ws_ref, bst_ref[...])
        h_out[...] = h_t + _nn(z[:, :d] * s, wout_ref[...])

    return pl.pallas_call(
        body, name="sgu_fwd", grid=(t // tm,),
        in_specs=[_row(tm, d), _const((1, d)), _const((d, 2 * d)), _const((1, d)), _const((1, d)), _const((SGU_GROUPS, CHUNK, CHUNK)),
                  _const((CHUNK, LANES)), _const((d, d))],
        out_specs=_row(tm, d), out_shape=_sds((t, d)), compiler_params=_params(),
    )(h, w['c_norm'], w['c_w_in'], w['c_ln_g'], w['c_ln_b'], w['c_w_s'], w['bsT'], w['c_w_out'])


def _sgu_bwd(h, dout, w, tm):
    t, d = h.shape

    def body(h_ref, dout_ref, cn_ref, win_ref, lg_ref, lb_ref, ws_ref, bst_ref, wout_ref,
             dh_out, dcn_out, dwin_out, dlg_out, dlb_out, dws_out, dbst_out, dwout_out):
        first = pl.program_id(0) == 0
        hn, vjp_norm = jax.vjp(_rms, h_ref[...], cn_ref[...])
        zpre = _nn(hn, win_ref[...])
        u, vjp_u = jax.vjp(_gelu, zpre[:, :d])
        vn, vjp_v = jax.vjp(lambda zp, lg, lb: _layer_norm(_gelu(zp), lg, lb), zpre[:, d:], lg_ref[...], lb_ref[...])
        s = _sgu_mix(vn, ws_ref, bst_ref[...])
        dout_t = dout_ref[...]
        dus = _nt(dout_t, wout_ref[...])
        _accumulate(dwout_out, _tn(u * s, dout_t), first)
        ds = dus * u
        tril = lax.broadcasted_iota(jnp.int32, (CHUNK, CHUNK), 0) >= lax.broadcasted_iota(jnp.int32, (CHUNK, CHUNK), 1)
        lane = lax.broadcasted_iota(jnp.int32, (CHUNK, LANES), 1)
        dws = [jnp.zeros((CHUNK, CHUNK), F32) for _ in range(SGU_GROUPS)]
        dbst = jnp.zeros((CHUNK, LANES), F32)
        dvn_chunks = []
        for n in range(tm // CHUNK):
            cols = []
            for g in range(SGU_GROUPS):
                ds_ng = ds[n * CHUNK:(n + 1) * CHUNK, g * CHUNK:(g + 1) * CHUNK]
                vc_ng = vn[n * CHUNK:(n + 1) * CHUNK, g * CHUNK:(g + 1) * CHUNK]
                cols.append(_tn(jnp.where(tril, ws_ref[g], 0.0), ds_ng))
                dws[g] = dws[g] + _nt(ds_ng, vc_ng)
                dbst = dbst + jnp.where(lane == g, jnp.sum(ds_ng, axis=1, keepdims=True), 0.0)
            dvn_chunks.append(jnp.concatenate(cols, axis=1))
        dvn = jnp.concatenate(dvn_chunks, axis=0)
        for g in range(SGU_GROUPS):
            val = jnp.where(tril, dws[g], 0.0)

            @pl.when(first)
            def _():
                dws_out[g] = val

            @pl.when(jnp.logical_not(first))
            def _():
                dws_out[g] += val
        _accumulate(dbst_out, dbst, first)
        (dzu,) = vjp_u(dus * s)
        dzv, dlg, dlb = vjp_v(dvn)
        _accumulate(dlg_out, dlg, first)
        _accumulate(dlb_out, dlb, first)
        dzpre = jnp.concatenate([dzu, dzv], axis=1)
        _accumulate(dwin_out, _tn(hn, dzpre), first)
        dh, dcn = vjp_norm(_nt(dzpre, win_ref[...]))
        _accumulate(dcn_out, dcn, first)
        dh_out[...] = dh + dout_t

    return pl.pallas_call(
        body, name="sgu_bwd", grid=(t // tm,),
        in_specs=[_row(tm, d), _row(tm, d), _const((1, d)), _const((d, 2 * d)), _const((1, d)), _const((1, d)),
                  _const((SGU_GROUPS, CHUNK, CHUNK)), _const((CHUNK, LANES)), _const((d, d))],
        out_specs=[_row(tm, d), _const((1, d)), _const((d, 2 * d)), _const((1, d)), _const((1, d)), _const((SGU_GROUPS, CHUNK, CHUNK)),
                   _const((CHUNK, LANES)), _const((d, d))],
        out_shape=[_sds((t, d)), _sds((1, d)), _sds((d, 2 * d)), _sds((1, d)), _sds((1, d)), _sds((SGU_GROUPS, CHUNK, CHUNK)),
                   _sds((CHUNK, LANES)), _sds((d, d))],
        compiler_params=_params(),
    )(h, dout, w['c_norm'], w['c_w_in'], w['c_ln_g'], w['c_ln_b'], w['c_w_s'], w['bsT'], w['c_w_out'])


def _loss_and_grad(h, tgt, g):
    def loss_fn(h_, g_):
        err = _rms(h_, g_) - tgt
        return 0.5 * jnp.sum(jnp.mean(err * err, axis=-1, keepdims=True), axis=0, keepdims=True)

    loss, vjp_loss = jax.vjp(loss_fn, h, g)
    return (loss,) + vjp_loss(jnp.ones((1, 1), F32))


def _tile(t, seq, want):
    tm = min(want, seq)
    assert seq % tm == 0 and t % tm == 0 and tm % CHUNK == 0
    return tm


def _local_step(x, posb, target, w, seq, late_weights, on_grads):
    t, d = x.shape
    b = t // seq
    hp = HEADS * HEAD_PAD
    tm_big, tm_mid = _tile(t, seq, 512), _tile(t, seq, 256)
    tq = _tile(t, seq, 512)

    q, k, v, xl, gate = _ab_in_fwd(x, posb, w, tm_big)
    o, probs = _attn_fwd(q.reshape(b, seq, hp), k.reshape(b, seq, hp), v.reshape(b, seq, hp), tq)
    o = o.reshape(t, hp)
    y, hs = _lru_fwd(xl, gate, w, tm_big, seq)
    w = {**w, **late_weights('out0', y)}
    h1 = _ab_out_fwd(x, o, y, w, tm_big)
    hcur = h1
    saved = []
    for l in range(2):
        if l == 1:
            w = {**w, **late_weights('mix1', hcur)}
            saved_h2 = hcur
            hcur = _sgu_fwd(hcur, w, tm_mid)
        wl = late_weights('ffn%d' % l, hcur)
        g, u, hn = _ffn_a_fwd(hcur, w['ffn_norm'][l], wl['Wg'], wl['Wu'], tm_big)
        saved.append((hcur, g, u, wl, hn))
        ffn_b = (g, u, hcur, w['ffn_conv_w'][l], w['ffn_conv_b'][l], wl['Wd'], tm_mid, seq)
        if l == 0:
            hcur = _ffn_b_fwd(*ffn_b)
    dh, loss, d_final = _ffn_b_fwd(*ffn_b, final=(target, w['final_norm']))

    ffn = {}
    conv_b = list(w['ffn_conv_b'])
    for l in (1, 0):
        hin, g, u, wl, hn = saved[l]
        dgc, du, d_wd, d_cw, d_cb = _ffn_b_bwd(g, u, dh, w['ffn_conv_w'][l], conv_b[l], wl['Wd'], tm_big, seq)
        dh, dg, d_norm = _ffn_a_dgrad(hin, w['ffn_norm'][l], dgc, du, dh, w['ffn_conv_w'][l], wl['Wg'], wl['Wu'], tm_mid, seq)
        d_wg, d_wu = _ffn_a_wgrad(hn, dg, du, _tile(t, seq, 1024))
        ffn[l] = dict(ffn_norm=d_norm, ffn_conv_w=d_cw, ffn_conv_b=d_cb, Wg=d_wg, Wu=d_wu, Wd=d_wd)
        if l == 1:
            dh, d_cn, d_cwin, d_lg, d_lb, d_ws, d_bst, d_cwout = _sgu_bwd(saved_h2, dh, w, tm_mid)
            zero = on_grads('late1', dict(final_norm=d_final, c_norm=d_cn, c_ln_g=d_lg, c_ln_b=d_lb, c_w_s=d_ws, bsT=d_bst, c_w_in=d_cwin,
                                          c_w_out=d_cwout, Wg=[d_wg], Wu=[d_wu], Wd=[d_wd]))
            conv_b[0] = conv_b[0] + zero
    late0 = {name: [ffn[0][name], ffn[1][name]] for name in ('ffn_norm', 'ffn_conv_w', 'ffn_conv_b')}
    zero = on_grads('late0', dict(late0, Wg=[ffn[0]['Wg']], Wu=[ffn[0]['Wu']], Wd=[ffn[0]['Wd']]))
    w = {**w, 'Wo_b': w['Wo_b'] + zero.astype(w['Wo_b'].dtype)}
    do, dy, d_woa, d_wob = _ab_out_bwd(o, y, dh, w, tm_big)
    dxl, dgate, d_cw, d_cb, d_wa, d_ba, d_wx, d_bx, d_lam = _lru_bwd(xl, gate, hs, dy, w, tm_big, seq)
    zero = on_grads('mid', dict(Wo_a=d_woa, Wo_b=d_wob, ab_conv_w=d_cw, ab_conv_b=d_cb, Wa=d_wa, ab_b_rg_a=d_ba, Wx=d_wx,
                                ab_b_rg_x=d_bx, ab_lambda=d_lam))
    w = {**w, 'ab_norm': w['ab_norm'] + zero}
    dq, dk, dv = _attn_bwd(q.reshape(b, seq, hp), k.reshape(b, seq, hp), v.reshape(b, seq, hp), probs, do.reshape(b, seq, hp), tq)
    dx, d_gn, d_win, d_qn, d_wq, d_kvn, d_wk, d_wv = _ab_in_bwd(
        x, posb, w, dq.reshape(t, hp), dk.reshape(t, hp), dv.reshape(t, hp), dxl, dgate, dh, tm_mid)
    return loss, dx, dict(ab_norm=d_gn, W_in=d_win, ab_q_norm=d_qn, Wq=d_wq, ab_kv_norm=d_kvn, Wk=d_wk, Wv=d_wv)


def _block_diag(wg):
    g, n, _ = wg.shape
    return jnp.einsum('gij,gh->gihj', wg, jnp.eye(g, dtype=wg.dtype)).reshape(g * n, g * n)


def _prepare_out(w_out):
    d = w_out.shape[2]
    mla = HEADS * QK_NOPE
    return {'Wo_a': jnp.pad(w_out[0, :mla].reshape(HEADS, QK_NOPE, d), ((0, 0), (0, HEAD_PAD - QK_NOPE), (0, 0))).reshape(HEADS * HEAD_PAD, d),
            'Wo_b': w_out[0, mla:]}


def _prepare(full):
    d = full['ab_w_in'].shape[1]
    w_in = full['ab_w_in'][0]
    zeros = lambda n: jnp.zeros((d, n), w_in.dtype)
    wq = full['ab_w_q_b'][0].reshape(Q_LORA, HEADS, QK_NOPE + QK_ROPE)
    wkv = full['ab_w_kv_b'][0].reshape(KV_LORA, HEADS, 2 * QK_NOPE)
    pad_head = lambda a: jnp.pad(a, ((0, 0), (0, 0), (0, HEAD_PAD - a.shape[2]))).reshape(a.shape[0], HEADS * HEAD_PAD)
    w = {
        'W_in': jnp.concatenate([w_in[:, :Z_KPE], zeros(QK_NOPE), w_in[:, Z_KPE:Z_KPE + QK_ROPE],
                                 zeros(HEAD_PAD - QK_NOPE - QK_ROPE), w_in[:, Z_KPE + QK_ROPE:]], axis=1),
        'Wq': pad_head(wq), 'Wk': pad_head(wkv[:, :, :QK_NOPE]), 'Wv': pad_head(wkv[:, :, QK_NOPE:]),
        'Wa': _bf(_block_diag(full['ab_w_rg_a'][0])), 'Wx': _bf(_block_diag(full['ab_w_rg_x'][0])),
        'c_w_s': full['c_w_s'][0],
        'bsT': jnp.pad(full['c_b_s'][0].T, ((0, 0), (0, LANES - SGU_GROUPS))),
        'ffn_norm': [full['ffn_norm'][l:l + 1] for l in range(2)], 'ffn_conv_w': [full['ffn_conv_w'][l] for l in range(2)],
        'ffn_conv_b': [full['ffn_conv_b'][l:l + 1] for l in range(2)],
        'ab_conv_w': full['ab_conv_w'][0], 'final_norm': full['final_norm'][None, :],
    }
    for name in ('ab_norm', 'ab_q_norm', 'ab_kv_norm', 'ab_conv_b', 'ab_b_rg_a', 'ab_b_rg_x', 'ab_lambda', 'c_norm', 'c_ln_g', 'c_ln_b'):
        w[name] = full[name]
    return w


def _unprepare(g):
    unpad_head = lambda a, n: a.reshape(a.shape[0], HEADS, HEAD_PAD)[:, :, :n]
    diag = lambda a: jnp.einsum('gigj->gij', a.reshape(HEADS, LRU_W // HEADS, HEADS, LRU_W // HEADS))
    rules = {
        'ab_w_in': (('W_in',), lambda a: jnp.concatenate([a[:, :Z_KPE], a[:, Z_KPE + QK_NOPE:Z_KPE + QK_NOPE + QK_ROPE], a[:, Z_LRU:]], axis=1)[None]),
        'ab_w_q_b': (('Wq',), lambda a: unpad_head(a, QK_NOPE + QK_ROPE).reshape(1, Q_LORA, -1)),
        'ab_w_kv_b': (('Wk', 'Wv'), lambda a, b: jnp.concatenate([unpad_head(a, QK_NOPE), unpad_head(b, QK_NOPE)], axis=2).reshape(1, KV_LORA, -1)),
        'ab_w_out': (('Wo_a', 'Wo_b'), lambda a, b: jnp.concatenate(
            [a.reshape(HEADS, HEAD_PAD, -1)[:, :QK_NOPE].reshape(HEADS * QK_NOPE, -1), b], axis=0)[None]),
        'ab_w_rg_a': (('Wa',), lambda a: diag(a)[None]), 'ab_w_rg_x': (('Wx',), lambda a: diag(a)[None]),
        'c_w_in': (('c_w_in',), lambda a: a[None]), 'c_w_out': (('c_w_out',), lambda a: a[None]), 'c_w_s': (('c_w_s',), lambda a: a[None]),
        'c_b_s': (('bsT',), lambda a: a[:, :SGU_GROUPS].T[None]),
        'ffn_w_gate': (('Wg',), jnp.stack), 'ffn_w_up': (('Wu',), jnp.stack), 'ffn_w_down': (('Wd',), jnp.stack),
        'ffn_norm': (('ffn_norm',), lambda a: jnp.concatenate(a, axis=0)), 'ffn_conv_w': (('ffn_conv_w',), jnp.stack),
        'ffn_conv_b': (('ffn_conv_b',), lambda a: jnp.concatenate(a, axis=0)),
        'ab_conv_w': (('ab_conv_w',), lambda a: a[None]), 'final_norm': (('final_norm',), lambda a: a[0]),
    }
    for name in ('ab_norm', 'ab_q_norm', 'ab_kv_norm', 'ab_conv_b', 'ab_b_rg_a', 'ab_b_rg_x', 'ab_lambda', 'c_norm', 'c_ln_g', 'c_ln_b'):
        rules[name] = ((name,), lambda a: a)
    return {name: fn(*[g[k] for k in keys]) for name, (keys, fn) in rules.items() if all(k in g for k in keys)}


SLAB_ROWS = 16


def _round_up(n, m):
    return -(-n // m) * m


def _to_chunks(full, axis):
    s = full.shape
    return jnp.moveaxis(full.reshape(s[:axis] + (N_DEV, s[axis] // N_DEV) + s[axis + 1:]), axis, 0)


def _from_chunks(chunks, axis):
    local = chunks.shape[1:]
    return jnp.moveaxis(chunks, 0, axis).reshape(local[:axis] + (N_DEV * local[axis],) + local[axis + 1:])


def _merge_columns(landed, name):
    _, _, r, n = landed.shape
    tr = r // 4

    def body(l_ref, o_ref):
        o_ref[0] = jnp.concatenate([l_ref[dev, 0] for dev in range(N_DEV)], axis=1)

    return pl.pallas_call(body, name="merge_" + name, grid=(r // tr,),
                          in_specs=[pl.BlockSpec((N_DEV, 1, tr, n), lambda i: (0, 0, i, 0))],
                          out_specs=pl.BlockSpec((1, tr, N_DEV * n), lambda i: (0, i, 0)),
                          out_shape=jax.ShapeDtypeStruct((1, r, N_DEV * n), landed.dtype), compiler_params=_params())(landed)


def _split_chunks(whole, axis, name):
    _, rows, cols = whole.shape
    if axis == 1:
        r = rows // N_DEV

        def body(x_ref, o_ref):
            o_ref[0] = _bf(x_ref[...])

        grid, out_shape = (N_DEV,), (N_DEV, 1, r, cols)
        spec, out_spec = pl.BlockSpec((1, r, cols), lambda dev: (0, dev, 0)), pl.BlockSpec((1, 1, r, cols), lambda dev: (dev, 0, 0, 0))
    else:
        n, tr = cols // N_DEV, rows // 4

        def body(x_ref, o_ref):
            x = x_ref[0]
            for dev in range(N_DEV):
                o_ref[dev, 0] = _bf(x[:, dev * n:(dev + 1) * n])

        grid, out_shape = (rows // tr,), (N_DEV, 1, rows, n)
        spec, out_spec = pl.BlockSpec((1, tr, cols), lambda i: (0, i, 0)), pl.BlockSpec((N_DEV, 1, tr, n), lambda i: (0, 0, i, 0))
    return pl.pallas_call(body, name="split_" + name, grid=grid, in_specs=[spec], out_specs=out_spec,
                          out_shape=jax.ShapeDtypeStruct(out_shape, BF16), compiler_params=_params())(whole)


def _slab_rows(n):
    return _round_up(-(-n // LANES), SLAB_ROWS)


def _to_slab(a, lead):
    a = a.reshape(lead + (-1,))
    rows = _slab_rows(a.shape[-1])
    a = jnp.pad(a, [(0, 0)] * len(lead) + [(0, rows * LANES - a.shape[-1])])
    return a.reshape(lead + (rows, LANES))


def _pack_slabs(parts, lead):
    return jnp.concatenate([_to_slab(p, lead) for p in parts], axis=len(lead))


def _unpack_slabs(packed, shapes):
    lead = packed.shape[:-2]
    out, row = [], 0
    for shape in shapes:
        size = math.prod(shape)
        rows = _slab_rows(size)
        piece = lax.slice_in_dim(packed, row, row + rows, axis=len(lead))
        out.append(piece.reshape(lead + (rows * LANES,))[..., :size].reshape(lead + tuple(shape)))
        row += rows
    return out


HBM = pl.BlockSpec(memory_space=pl.ANY)


def _other_chips(x, y):
    return [(1 - x, y), (x, 1 - y), (1 - x, 1 - y)]


def _all_gather(blocks):
    n = len(blocks)

    def body(*refs):
        x_refs, out_refs, token = refs[:n], refs[n:2 * n], refs[2 * n]
        send_sems, recv_sems, local_sems = refs[2 * n + 1:]
        token[...] = jnp.zeros_like(token)
        x, y, c = lax.axis_index("x"), lax.axis_index("y"), lax.axis_index("c")
        me, sibling = (x, y, c), (x, y, 1 - c)
        chips = _other_chips(x, y)

        def slab(a, px, py, pc):
            return out_refs[a].at[4 * px + 2 * py + pc]

        def copy(a, k, blk, to, src=None):
            return pltpu.make_async_remote_copy(src_ref=slab(a, *blk) if src is None else src, dst_ref=slab(a, *blk),
                                                send_sem=send_sems.at[7 * a + k], recv_sem=recv_sems.at[7 * a + k],
                                                device_id=to, device_id_type=MESH)

        mine = [pltpu.make_async_copy(x_refs[a], slab(a, *me), local_sems.at[a]) for a in range(n)]
        started = []
        for a in range(n):
            mine[a].start()
            started.append(copy(a, 0, me, sibling, src=x_refs[a]))
            started += [copy(a, 1 + j, me, (*chip, c), src=x_refs[a]) for j, chip in enumerate(chips)]
        for cp in started:
            cp.start()
        for j, chip in enumerate(chips):
            for a in range(n):
                copy(a, 1 + j, (*chip, c), me).wait_recv()
                passed = copy(a, 4 + j, (*chip, c), sibling)
                passed.start()
                started.append(passed)
        for a in range(n):
            copy(a, 0, sibling, me).wait_recv()
        for j, chip in enumerate(chips):
            for a in range(n):
                copy(a, 4 + j, (*chip, 1 - c), me).wait_recv()
        for cp in started:
            cp.wait_send()
        for a in range(n):
            mine[a].wait()

    out = pl.pallas_call(
        body, name="all_gather_weights",
        out_shape=[jax.ShapeDtypeStruct((N_DEV,) + b.shape, b.dtype) for b in blocks] + [jax.ShapeDtypeStruct((8, LANES), F32)],
        in_specs=[HBM] * n, out_specs=[HBM] * n + [pl.BlockSpec(memory_space=pltpu.VMEM)],
        scratch_shapes=[pltpu.SemaphoreType.DMA((7 * n,)), pltpu.SemaphoreType.DMA((7 * n,)), pltpu.SemaphoreType.DMA((n,))],
    )(*blocks)
    return list(out[:n]), out[n][0, 0]


FLIPS = [(0, 0, 1), (1, 0, 0), (1, 0, 1), (0, 1, 0), (0, 1, 1), (1, 1, 0), (1, 1, 1)]


def _peers(x, y, c):
    flip = lambda v, f: 1 - v if f else v
    return [(flip(x, fx), flip(y, fy), flip(c, fc)) for fx, fy, fc in FLIPS]


def _direct_copies(src_refs, land_refs, send_sems, recv_sems, scatter):
    x, y, c = lax.axis_index("x"), lax.axis_index("y"), lax.axis_index("c")
    me = 4 * x + 2 * y + c
    starts, waits = [], []
    for a in range(len(src_refs)):
        for k, (px, py, pc) in enumerate(_peers(x, y, c)):
            peer = 4 * px + 2 * py + pc
            sems = dict(send_sem=send_sems.at[7 * a + k], recv_sem=recv_sems.at[7 * a + k], device_id=(px, py, pc), device_id_type=MESH)
            src = src_refs[a].at[peer] if scatter else src_refs[a]
            starts.append(pltpu.make_async_remote_copy(src_ref=src, dst_ref=land_refs[a].at[me], **sems))
            waits.append(pltpu.make_async_remote_copy(src_ref=src, dst_ref=land_refs[a].at[peer], **sems))
    n = len(src_refs)
    keeps = [] if scatter else [pltpu.make_async_copy(src_refs[a], land_refs[a].at[me], send_sems.at[7 * n + a]) for a in range(n)]
    return starts, waits, keeps


def _landing(src, scatter):
    block = src.shape[1:] if scatter else src.shape
    return jax.ShapeDtypeStruct((N_DEV,) + block, src.dtype)


HBM_SPACE = pl.BlockSpec(memory_space=pltpu.HBM)
SEMAPHORES = pl.BlockSpec(memory_space=pltpu.SEMAPHORE)
SPLIT_EFFECT = pltpu.SideEffectType.DATAFLOW_SIDE_EFFECTING


def _start_exchange(name, srcs, scatter):
    n = len(srcs)
    lands = [lax.empty(s.shape, s.dtype) for s in (_landing(s, scatter) for s in srcs)]

    def body(*refs):
        starts, _, keeps = _direct_copies(refs[:n], refs[n:2 * n], refs[2 * n], refs[2 * n + 1], scatter)
        for cp in starts + keeps:
            cp.start()
        refs[-1][...] = jnp.zeros_like(refs[-1])

    held = [pltpu.with_memory_space_constraint(a, pltpu.HBM) for a in list(srcs) + lands]
    out = pl.pallas_call(
        body, name=name + "_start",
        out_shape=(pltpu.SemaphoreType.DMA(((7 if scatter else 8) * n,)), pltpu.SemaphoreType.DMA((7 * n,)),
                   *[pltpu.HBM(a.shape, a.dtype) for a in held],
                   jax.ShapeDtypeStruct((8, LANES), F32)),
        in_specs=[HBM_SPACE] * (2 * n), out_specs=(SEMAPHORES, SEMAPHORES, *[HBM_SPACE] * (2 * n), pl.BlockSpec(memory_space=pltpu.VMEM)),
        input_output_aliases={i: 2 + i for i in range(2 * n)},
        compiler_params=pltpu.CompilerParams(has_side_effects=SPLIT_EFFECT),
    )(*held)
    return out[0], out[1], list(out[2:2 + n]), list(out[2 + n:2 + 2 * n]), out[-1][0, 0], out[-1]


def _wait_exchange(name, started, after, scatter):
    send_sems, recv_sems, srcs, lands = started[:4]
    n = len(srcs)

    def body(*refs):
        _, waits, keeps = _direct_copies(refs[:n], refs[n:2 * n], refs[2 * n], refs[2 * n + 1], scatter)
        for cp in waits:
            cp.wait_send()
        for cp in waits:
            cp.wait_recv()
        for cp in keeps:
            cp.wait()

    out = pl.pallas_call(
        body, name=name + "_wait", out_shape=tuple(pltpu.HBM(a.shape, a.dtype) for a in srcs + lands),
        in_specs=[HBM_SPACE] * (2 * n) + [SEMAPHORES, SEMAPHORES, HBM], out_specs=tuple([HBM_SPACE] * (2 * n)),
        input_output_aliases={i: i for i in range(2 * n)},
        compiler_params=pltpu.CompilerParams(has_side_effects=SPLIT_EFFECT),
    )(*srcs, *lands, send_sems, recv_sems, after)
    return list(out[:n]), list(out[n:])


def _row_tile(rows):
    return rows // 2 if (rows // 2) % SLAB_ROWS == 0 else rows


def _sum_in_device_order(me_ref, l_ref, own_ref):
    mine = own_ref[0].astype(F32)
    g = jnp.where(me_ref[0] == 0, mine, l_ref[0].astype(F32))
    for dev in range(1, N_DEV):
        g = g + jnp.where(me_ref[0] == dev, mine, l_ref[dev].astype(F32))
    return g


def _adamw(g, w, m, v):
    m_new = ADAM_B1 * m + (1.0 - ADAM_B1) * g
    v_new = ADAM_B2 * v + (1.0 - ADAM_B2) * (g * g)
    m_hat = m_new * (1.0 / (1.0 - ADAM_B1 ** ADAM_STEP))
    v_hat = v_new * (1.0 / (1.0 - ADAM_B2 ** ADAM_STEP))
    return -ADAM_LR * (m_hat / (jnp.sqrt(v_hat) + ADAM_EPS) + ADAM_WD * w), m_new, v_new


def _sum_chunks(me, landed, own, name):
    _, _, r, n = landed.shape

    def body(me_ref, l_ref, own_ref, g_out):
        g_out[...] = _sum_in_device_order(me_ref, l_ref, own_ref)[0]

    return pl.pallas_call(
        body, name="sum_" + name,
        grid_spec=pltpu.PrefetchScalarGridSpec(
            num_scalar_prefetch=1, grid=(1,),
            in_specs=[pl.BlockSpec((N_DEV, 1, r, n), lambda i, me_ref: (0, 0, 0, 0)),
                      pl.BlockSpec((1, 1, r, n), lambda i, me_ref: (me_ref[0], 0, 0, 0))],
            out_specs=pl.BlockSpec((r, n), lambda i, me_ref: (0, 0))),
        out_shape=_sds((r, n)), compiler_params=_params())(me, landed, own)


def _adamw_small(gs, ws, ms, vs):
    n = len(gs)

    def body(*refs):
        ins, outs = refs[:4 * n], refs[4 * n:]
        for i in range(n):
            outs[i][...], outs[n + i][...], outs[2 * n + i][...] = _adamw(*[ins[k * n + i][...] for k in range(4)])

    out = pl.pallas_call(body, name="adamw_small", out_shape=[_sds(w.shape) for w in ws] * 3)(*gs, *ws, *ms, *vs)
    return out[:n], out[n:2 * n], out[2 * n:]


def _sum_and_adamw(me, landed, own, wts, m, v, name, layer=None, into=None):
    layers, r, n = wts.shape
    first = 0 if layer is None else layer
    count = layers if layer is None else 1
    tr = _row_tile(r)
    blk = pl.BlockSpec((1, tr, n), lambda li, ri, me_ref: (first + li, ri, 0))
    held = [] if into is None else list(into)

    def body(me_ref, l_ref, own_ref, w_ref, m_ref, v_ref, *rest):
        g_out, d_out, m_out, v_out = rest[len(held):]
        g = _sum_in_device_order(me_ref, l_ref, own_ref)
        g_out[...] = g
        d_out[...], m_out[...], v_out[...] = _adamw(g, w_ref[...], m_ref[...], v_ref[...])

    return pl.pallas_call(
        body, name="adamw_" + name,
        grid_spec=pltpu.PrefetchScalarGridSpec(
            num_scalar_prefetch=1, grid=(count, r // tr),
            in_specs=[pl.BlockSpec((N_DEV, 1, tr, n), lambda li, ri, me_ref: (0, li, ri, 0)),
                      pl.BlockSpec((1, 1, tr, n), lambda li, ri, me_ref: (me_ref[0], li, ri, 0)), blk, blk, blk] + [HBM] * len(held),
            out_specs=[blk] * 4),
        out_shape=[_sds((layers, r, n))] * 4, input_output_aliases={6 + i: i for i in range(len(held))},
        compiler_params=_params(2))(me, landed, own, wts, m, v, *held)


EARLY = ['ab_w_in']
LATE_STAGES = {
    'out0': [('ab_w_out', None, 'ab_w_out')],
    'ffn0': [('ffn_w_gate', 0, 'Wg'), ('ffn_w_up', 0, 'Wu'), ('ffn_w_down', 0, 'Wd')],
    'mix1': [('c_w_in', None, 'c_w_in'), ('c_w_out', None, 'c_w_out')],
    'ffn1': [('ffn_w_gate', 1, 'Wg'), ('ffn_w_up', 1, 'Wu'), ('ffn_w_down', 1, 'Wd')],
}
TRANSPOSED = ('ffn_w_gate', 'ffn_w_up')


def _stored(name, a):
    return jnp.swapaxes(a, 1, 2) if name in TRANSPOSED else a


def _stored_axis(name):
    return 1 if name in TRANSPOSED else SHARD_AXIS[name]


GRAD_STAGES = {
    'late1': ([('c_w_in', None), ('c_w_out', None), ('ffn_w_gate', 1), ('ffn_w_up', 1), ('ffn_w_down', 1)],
              ['c_norm', 'c_ln_g', 'c_ln_b', 'c_w_s', 'c_b_s', 'final_norm']),
    'late0': ([('ffn_w_gate', 0), ('ffn_w_up', 0), ('ffn_w_down', 0)], ['ffn_norm', 'ffn_conv_w', 'ffn_conv_b']),
    'mid': ([('ab_w_out', None)], ['ab_conv_w', 'ab_conv_b', 'ab_w_rg_a', 'ab_b_rg_a', 'ab_w_rg_x', 'ab_b_rg_x', 'ab_lambda']),
    'last': ([('ab_w_in', None)], ['ab_norm', 'ab_q_norm', 'ab_w_q_b', 'ab_kv_norm', 'ab_w_kv_b']),
}


def _gather_early(local):
    small = [_bf(local[n]) if n in MATRICES else lax.bitcast_convert_type(local[n], BF16) for n in SMALL_SHARDED]
    gathered, zero = _all_gather([_bf(local[n]) for n in EARLY] + [_pack_slabs(small, ())])
    full = {n: local[n] for n in REPLICATED}
    for n, g in zip(EARLY, gathered):
        full[n] = _from_chunks(g, SHARD_AXIS[n])
    for n, p in zip(SMALL_SHARDED, _unpack_slabs(gathered[-1], [s.shape for s in small])):
        full[n] = _from_chunks(p if n in MATRICES else lax.bitcast_convert_type(p, F32), SHARD_AXIS[n])
    return full, zero


def kernel(x, positions, ab_norm, ab_w_in, ab_q_norm, ab_w_q_b, ab_kv_norm, ab_w_kv_b, ab_conv_w, ab_conv_b, ab_w_rg_a, ab_b_rg_a, ab_w_rg_x, ab_b_rg_x, ab_lambda, ab_w_out, c_norm, c_w_in, c_ln_g, c_ln_b, c_w_s, c_b_s, c_w_out, ffn_norm, ffn_w_gate, ffn_w_up, ffn_conv_w, ffn_conv_b, ffn_w_down, final_norm, loss_target, m_ab_norm, m_ab_w_in, m_ab_q_norm, m_ab_w_q_b, m_ab_kv_norm, m_ab_w_kv_b, m_ab_conv_w, m_ab_conv_b, m_ab_w_rg_a, m_ab_b_rg_a, m_ab_w_rg_x, m_ab_b_rg_x, m_ab_lambda, m_ab_w_out, m_c_norm, m_c_w_in, m_c_ln_g, m_c_ln_b, m_c_w_s, m_c_b_s, m_c_w_out, m_ffn_norm, m_ffn_w_gate, m_ffn_w_up, m_ffn_conv_w, m_ffn_conv_b, m_ffn_w_down, m_final_norm, v_ab_norm, v_ab_w_in, v_ab_q_norm, v_ab_w_q_b, v_ab_kv_norm, v_ab_w_kv_b, v_ab_conv_w, v_ab_conv_b, v_ab_w_rg_a, v_ab_b_rg_a, v_ab_w_rg_x, v_ab_b_rg_x, v_ab_lambda, v_ab_w_out, v_c_norm, v_c_w_in, v_c_ln_g, v_c_ln_b, v_c_w_s, v_c_b_s, v_c_w_out, v_ffn_norm, v_ffn_w_gate, v_ffn_w_up, v_ffn_conv_w, v_ffn_conv_b, v_ffn_w_down, v_final_norm):
    given = dict(locals())
    local = {n: given[n] for n in WEIGHTS}
    b, seq, d = x.shape
    t = b * seq

    me = (4 * lax.axis_index("x") + 2 * lax.axis_index("y") + lax.axis_index("c")).astype(jnp.int32)
    me1 = me.reshape(1)

    full, zero = _gather_early(local)
    gathers = {}
    for stage, members in LATE_STAGES.items():
        srcs = [_bf(_stored(n, local[n] if layer is None else local[n][layer:layer + 1]) + zero) for n, layer, _ in members]
        gathers[stage] = _start_exchange('gather_' + stage, srcs, scatter=False)
        zero = gathers[stage][4]
    w = _prepare(full)
    w['ab_norm'] = w['ab_norm'] + zero

    def late_weights(stage, after):
        _, lands = _wait_exchange('gather_' + stage, gathers[stage], after, scatter=False)
        whole = [l.reshape(1, -1, l.shape[-1]) if _stored_axis(n) == 1 else _merge_columns(l, n)
                 for (n, _, _), l in zip(LATE_STAGES[stage], lands)]
        if stage == 'out0':
            return _prepare_out(whole[0])
        return {key: a[0] for (_, _, key), a in zip(LATE_STAGES[stage], whole)}

    scatters = {}

    def start_scatter(stage, g):
        whole = _unprepare(g)
        big, small = GRAD_STAGES[stage]
        slab = [_to_chunks(whole[n], SHARD_AXIS[n]) if n in SHARD_AXIS else jnp.broadcast_to(whole[n][None], (N_DEV,) + whole[n].shape)
                for n in small]
        own = [whole[n].reshape(N_DEV, 1, whole[n].shape[1] // N_DEV, whole[n].shape[2])
               if whole[n].dtype == BF16 and _stored_axis(n) == 1 else
               _split_chunks(whole[n], _stored_axis(n), n + ('' if layer is None else str(layer))) for n, layer in big]
        own.append(_bf(_pack_slabs(slab, (N_DEV,)))[:, None])
        scatters[stage] = _start_exchange('scatter_' + stage, own, scatter=True)
        return scatters[stage][4]

    posb = jnp.broadcast_to(positions.astype(F32).reshape(t, 1), (t, LANES))
    loss, dx, grads = _local_step(x.reshape(t, d), posb, loss_target.reshape(t, d), w, seq, late_weights, start_scatter)
    start_scatter('last', grads)
    after = scatters['last'][5]

    updated, small_grads = {}, {}
    for stage, (big, small) in GRAD_STAGES.items():
        owns, landed = _wait_exchange('scatter_' + stage, scatters[stage], after, scatter=True)
        for (n, layer), own, land in zip(big, owns, landed):
            updated[n] = _sum_and_adamw(me1, land, own, _stored(n, given[n]), _stored(n, given['m_' + n]), _stored(n, given['v_' + n]),
                                        n + ('' if layer is None else str(layer)), layer, updated.get(n))
        summed = _sum_chunks(me1, landed[-1], owns[-1], stage)
        small_grads.update(zip(small, _unpack_slabs(summed, [local[n].shape for n in small])))
        after = sum([updated[n][1][:1, :1, :1] for n, _ in big], summed[:1, :1].reshape(1, 1, 1))
    names = list(small_grads)
    news = _adamw_small([small_grads[n] for n in names], *[[given[p + n] for n in names] for p in ('', 'm_', 'v_')])
    for i, n in enumerate(names):
        updated[n] = [small_grads[n], news[0][i], news[1][i], news[2][i]]
    total = lax.psum(loss[0, 0], ("x", "y", "c"))
    return (total, dx.reshape(b, seq, d), *[_stored(n, updated[n][kind]) for kind in range(4) for n in WEIGHTS])
```

```python
import math

import jax
import jax.numpy as jnp
from jax import lax
from jax.experimental import pallas as pl
from jax.experimental.pallas import tpu as pltpu

F32 = jnp.float32
BF16 = jnp.bfloat16
MESH = pl.DeviceIdType.MESH

N_DEV = 8
LANES = 128
HALO = 8
VMEM_LIMIT = 56 << 20

NORM_EPS = 1e-6
HEADS = 8
HEAD_PAD = 128
QK_NOPE = 64
QK_ROPE = 32
ROPE_HALF = 16
ROPE_BASE = 10000.0
ATTN_SCALE = (QK_NOPE + QK_ROPE) ** -0.5
LRU_C = 8.0
LRU_W = 512
CHUNK = 128
SGU_GROUPS = 8
D_FF = 2816
FF_BLOCKS = 2

ADAM_LR, ADAM_B1, ADAM_B2, ADAM_EPS, ADAM_WD, ADAM_STEP = 0.001, 0.9, 0.999, 1e-08, 0.01, 10

WEIGHTS = ['ab_norm', 'ab_w_in', 'ab_q_norm', 'ab_w_q_b', 'ab_kv_norm', 'ab_w_kv_b', 'ab_conv_w', 'ab_conv_b',
           'ab_w_rg_a', 'ab_b_rg_a', 'ab_w_rg_x', 'ab_b_rg_x', 'ab_lambda', 'ab_w_out', 'c_norm', 'c_w_in', 'c_ln_g',
           'c_ln_b', 'c_w_s', 'c_b_s', 'c_w_out', 'ffn_norm', 'ffn_w_gate', 'ffn_w_up', 'ffn_conv_w', 'ffn_conv_b',
           'ffn_w_down', 'final_norm']
SHARD_AXIS = {'ab_w_in': 2, 'ab_w_q_b': 2, 'ab_w_kv_b': 2, 'ab_conv_w': 2, 'ab_w_out': 1, 'c_norm': 1, 'c_w_in': 2,
              'c_ln_g': 1, 'c_ln_b': 1, 'c_w_out': 1, 'ffn_w_gate': 2, 'ffn_w_up': 2, 'ffn_conv_w': 2, 'ffn_w_down': 1}
MATRICES = ['ab_w_in', 'ab_w_q_b', 'ab_w_kv_b', 'ab_w_out', 'c_w_in', 'c_w_out', 'ffn_w_gate', 'ffn_w_up', 'ffn_w_down']
BIG = ['ab_w_in', 'c_w_in', 'ffn_w_gate', 'ffn_w_up', 'ab_w_out', 'c_w_out', 'ffn_w_down']
REPLICATED = [n for n in WEIGHTS if n not in SHARD_AXIS]
SMALL_SHARDED = [n for n in WEIGHTS if n in SHARD_AXIS and n not in BIG]


def _bf(x):
    return x.astype(BF16)


def _nn(a, b):
    return lax.dot_general(_bf(a), _bf(b), (((1,), (0,)), ((), ())), preferred_element_type=F32)


def _nt(a, b):
    return lax.dot_general(_bf(a), _bf(b), (((1,), (1,)), ((), ())), preferred_element_type=F32)


def _tn(a, b):
    return lax.dot_general(_bf(a), _bf(b), (((0,), (0,)), ((), ())), preferred_element_type=F32)


def _rms(x, g):
    return x * lax.rsqrt(jnp.mean(x * x, axis=-1, keepdims=True) + NORM_EPS) * g


def _layer_norm(x, g, b):
    xc = x - jnp.mean(x, axis=-1, keepdims=True)
    return xc * lax.rsqrt(jnp.mean(xc * xc, axis=-1, keepdims=True) + NORM_EPS) * g + b


def _gelu(x):
    return jax.nn.gelu(x)


STRIP = 16
STRIP_LANES = 384
GELU_C = math.sqrt(2.0 / math.pi)
GELU_A = 0.044715


def _gelu_and_grad(x):
    x2 = x * x
    t = jnp.tanh(x * (GELU_C + (GELU_C * GELU_A) * x2))
    half_x = 0.5 * x
    one_plus_t = 1.0 + t
    return half_x * one_plus_t, 0.5 * one_plus_t + half_x * (1.0 - t * t) * (GELU_C + (3.0 * GELU_C * GELU_A) * x2)


def _colsum(x):
    return jnp.sum(x, axis=0, keepdims=True)


def _softplus(x):
    return jnp.maximum(x, 0.0) + jnp.log1p(jnp.exp(-jnp.abs(x)))


@jax.custom_vjp
def _decay(x):
    a = jnp.exp(x)
    y = 2.0 * x
    series = -y * (1.0 + y * (1 / 2 + y * (1 / 6 + y * (1 / 24 + y * (1 / 120 + y * (1 / 720))))))
    return a, jnp.where(y < -0.3, 1.0 - a * a, series)


def _decay_fwd(x):
    a, gap = _decay(x)
    return (a, gap), a


def _decay_bwd(a, cts):
    return (a * (cts[0] - 2.0 * a * cts[1]),)


_decay.defvjp(_decay_fwd, _decay_bwd)


def _accumulate(ref, val, first):
    @pl.when(first)
    def _():
        ref[...] = val

    @pl.when(jnp.logical_not(first))
    def _():
        ref[...] += val


def _params(n_axes=1):
    return pltpu.CompilerParams(dimension_semantics=("arbitrary",) * n_axes, vmem_limit_bytes=VMEM_LIMIT)


def _row(tm, n):
    return pl.BlockSpec((tm, n), lambda i: (i, 0))


def _const(shape):
    nd = len(shape)
    return pl.BlockSpec(shape, lambda i: (0,) * nd, pipeline_mode=pl.Buffered(1))


def _prev_halo(tm, n):
    return pl.BlockSpec((HALO, n), lambda i: (jnp.maximum(i * (tm // HALO) - 1, 0), 0))


def _next_halo(tm, n, n_tiles):
    last = n_tiles * (tm // HALO) - 1
    return pl.BlockSpec((HALO, n), lambda i: (jnp.minimum((i + 1) * (tm // HALO), last), 0))


def _sds(shape, dtype=F32):
    return jax.ShapeDtypeStruct(shape, dtype)


def _rope_tables(posb):
    lane = lax.broadcasted_iota(jnp.int32, posb.shape, 1)
    in_rope = jnp.logical_and(lane >= QK_NOPE, lane < QK_NOPE + QK_ROPE)
    j = (lane & (ROPE_HALF - 1)).astype(F32)
    inv_freq = jnp.exp((-math.log(ROPE_BASE)) * j / ROPE_HALF)
    ang = posb * inv_freq
    return jnp.where(in_rope, jnp.cos(ang), 1.0), jnp.where(in_rope, jnp.sin(ang), 0.0)


def _rot(q):
    n = q.shape[1]
    lane = lax.broadcasted_iota(jnp.int32, q.shape, 1) & (HEAD_PAD - 1)
    first_half = jnp.where(lane >= QK_NOPE, -pltpu.roll(q, n - ROPE_HALF, 1), 0.0)
    second_half = jnp.where(lane < QK_NOPE + QK_ROPE, pltpu.roll(q, ROPE_HALF, 1), 0.0)
    return jnp.where(lane < QK_NOPE + ROPE_HALF, first_half, second_half)


def _rope(q, cos_t, sin_t):
    return q * cos_t + _rot(q) * sin_t


def _rope_transpose(dq, cos_t, sin_t):
    return dq * cos_t - _rot(dq * sin_t)


def _tile_heads(t):
    return jnp.concatenate([t] * HEADS, axis=1)


Q_LORA, KV_LORA = 256, 128
Z_KPE = Q_LORA + KV_LORA
Z_LRU = Z_KPE + HEAD_PAD
Z_GATE = Z_LRU + LRU_W
Z_WIDTH = Z_GATE + LRU_W


def _ab_in_fwd(x, posb, w, tm):
    t, d = x.shape

    def body(x_ref, pos_ref, gn_ref, win_ref, qn_ref, wq_ref, kvn_ref, wk_ref, wv_ref, q_out, k_out, v_out, xl_out, gate_out):
        hn = _rms(x_ref[...], gn_ref[...])
        z = _nn(hn, win_ref[...])
        cqn = _rms(z[:, :Q_LORA], qn_ref[...])
        kvn = _rms(z[:, Q_LORA:Z_KPE], kvn_ref[...])
        cos_t, sin_t = _rope_tables(pos_ref[...])
        q_out[...] = _bf(_rope(_nn(cqn, wq_ref[...]), _tile_heads(cos_t), _tile_heads(sin_t)))
        kpe = _rope(z[:, Z_KPE:Z_LRU], cos_t, sin_t)
        k_out[...] = _bf(_nn(kvn, wk_ref[...]) + _tile_heads(kpe))
        v_out[...] = _bf(_nn(kvn, wv_ref[...]))
        xl_out[...] = z[:, Z_LRU:Z_GATE]
        gate_out[...] = z[:, Z_GATE:]

    hp = HEADS * HEAD_PAD
    return pl.pallas_call(
        body, name="ab_in_fwd", grid=(t // tm,),
        in_specs=[_row(tm, d), _row(tm, LANES), _const((1, d)), _const((d, Z_WIDTH)), _const((1, Q_LORA)), _const((Q_LORA, hp)),
                  _const((1, KV_LORA)), _const((KV_LORA, hp)), _const((KV_LORA, hp))],
        out_specs=[_row(tm, hp), _row(tm, hp), _row(tm, hp), _row(tm, LRU_W), _row(tm, LRU_W)],
        out_shape=[_sds((t, hp), BF16), _sds((t, hp), BF16), _sds((t, hp), BF16), _sds((t, LRU_W)), _sds((t, LRU_W))],
        compiler_params=_params(),
    )(x, posb, w['ab_norm'], w['W_in'], w['ab_q_norm'], w['Wq'], w['ab_kv_norm'], w['Wk'], w['Wv'])


def _ab_in_bwd(x, posb, w, dq, dk, dv, dxl, dgate, dres, tm):
    t, d = x.shape
    hp = HEADS * HEAD_PAD

    def body(x_ref, pos_ref, gn_ref, win_ref, qn_ref, wq_ref, kvn_ref, wk_ref, wv_ref, dq_ref, dk_ref, dv_ref, dxl_ref, dgate_ref,
             dres_ref, dx_out, dgn_out, dwin_out, dqn_out, dwq_out, dkvn_out, dwk_out, dwv_out):
        first = pl.program_id(0) == 0
        hn, vjp_in = jax.vjp(_rms, x_ref[...], gn_ref[...])
        z = _nn(hn, win_ref[...])
        cqn, vjp_q = jax.vjp(_rms, z[:, :Q_LORA], qn_ref[...])
        kvn, vjp_kv = jax.vjp(_rms, z[:, Q_LORA:Z_KPE], kvn_ref[...])
        cos_t, sin_t = _rope_tables(pos_ref[...])
        dq0 = _rope_transpose(dq_ref[...], _tile_heads(cos_t), _tile_heads(sin_t))
        dk0 = dk_ref[...]
        dv0 = dv_ref[...]
        dkpe = dk0[:, :HEAD_PAD]
        for h in range(1, HEADS):
            dkpe = dkpe + dk0[:, h * HEAD_PAD:(h + 1) * HEAD_PAD]
        dkpe = _rope_transpose(dkpe, cos_t, sin_t)
        _accumulate(dwq_out, _tn(cqn, dq0), first)
        _accumulate(dwk_out, _tn(kvn, dk0), first)
        _accumulate(dwv_out, _tn(kvn, dv0), first)
        dcq, dqn = vjp_q(_nt(dq0, wq_ref[...]))
        dckv, dkvn = vjp_kv(_nt(dk0, wk_ref[...]) + _nt(dv0, wv_ref[...]))
        _accumulate(dqn_out, dqn, first)
        _accumulate(dkvn_out, dkvn, first)
        dz = _bf(jnp.concatenate([_bf(dcq), _bf(dckv), _bf(dkpe), dxl_ref[...], dgate_ref[...]], axis=1))
        _accumulate(dwin_out, _tn(hn, dz), first)
        dx, dgn = vjp_in(_nt(dz, win_ref[...]))
        _accumulate(dgn_out, dgn, first)
        dx_out[...] = dx + dres_ref[...]

    return pl.pallas_call(
        body, name="ab_in_bwd", grid=(t // tm,),
        in_specs=[_row(tm, d), _row(tm, LANES), _const((1, d)), _const((d, Z_WIDTH)), _const((1, Q_LORA)), _const((Q_LORA, hp)),
                  _const((1, KV_LORA)), _const((KV_LORA, hp)), _const((KV_LORA, hp)),
                  _row(tm, hp), _row(tm, hp), _row(tm, hp), _row(tm, LRU_W), _row(tm, LRU_W), _row(tm, d)],
        out_specs=[_row(tm, d), _const((1, d)), _const((d, Z_WIDTH)), _const((1, Q_LORA)), _const((Q_LORA, hp)),
                   _const((1, KV_LORA)), _const((KV_LORA, hp)), _const((KV_LORA, hp))],
        out_shape=[_sds((t, d)), _sds((1, d)), _sds((d, Z_WIDTH)), _sds((1, Q_LORA)), _sds((Q_LORA, hp)),
                   _sds((1, KV_LORA)), _sds((KV_LORA, hp)), _sds((KV_LORA, hp))],
        compiler_params=_params(),
    )(x, posb, w['ab_norm'], w['W_in'], w['ab_q_norm'], w['Wq'], w['ab_kv_norm'], w['Wk'], w['Wv'], dq, dk, dv, dxl, dgate, dres)


def _attn_probs(q_blk, k_ext, tq):
    ext = k_ext.shape[0]
    s = lax.dot_general(q_blk, k_ext, (((1,), (1,)), ((), ())), preferred_element_type=F32) * ATTN_SCALE
    causal = lax.broadcasted_iota(jnp.int32, (tq, tq), 1) <= lax.broadcasted_iota(jnp.int32, (tq, tq), 0)
    diag = jnp.where(causal, s[:, ext - tq:], -1e30)
    s = diag if ext == tq else jnp.concatenate([s[:, :ext - tq], diag], axis=1)
    p = jnp.exp(s - jnp.max(s, axis=1, keepdims=True))
    return p * (1.0 / jnp.sum(p, axis=1, keepdims=True))


def _attn_fwd(q, k, v, tq):
    b, s, hp = q.shape
    blk = pl.BlockSpec((1, s, HEAD_PAD), lambda bi, h: (bi, 0, h))

    def body(q_ref, k_ref, v_ref, o_ref, p_ref):
        kb = _bf(k_ref[0])
        vb = _bf(v_ref[0])
        for i in range(s // tq):
            ext = (i + 1) * tq
            p = _bf(_attn_probs(_bf(q_ref[0, i * tq:ext, :]), kb[:ext], tq))
            p_ref[0, 0, i * tq:ext, :ext] = p
            o_ref[0, i * tq:ext, :] = _bf(lax.dot_general(p, vb[:ext], (((1,), (0,)), ((), ())), preferred_element_type=F32))

    return pl.pallas_call(body, name="attn_fwd", grid=(b, HEADS), in_specs=[blk, blk, blk],
                          out_specs=[blk, pl.BlockSpec((1, 1, s, s), lambda bi, h: (bi, h, 0, 0))],
                          out_shape=[_sds((b, s, hp), BF16), _sds((b, HEADS, s, s), BF16)], compiler_params=_params(2))(q, k, v)


def _attn_bwd(q, k, v, probs, do, tq):
    b, s, hp = q.shape
    blk = pl.BlockSpec((1, s, HEAD_PAD), lambda bi, h: (bi, 0, h))

    def body(q_ref, k_ref, v_ref, p_ref, do_ref, dq_ref, dk_ref, dv_ref):
        kb = _bf(k_ref[0])
        vb = _bf(v_ref[0])
        dk_ref[...] = jnp.zeros_like(dk_ref)
        dv_ref[...] = jnp.zeros_like(dv_ref)
        for i in range(s // tq):
            ext = (i + 1) * tq
            qb = _bf(q_ref[0, i * tq:ext, :])
            dob = _bf(do_ref[0, i * tq:ext, :])
            pb = p_ref[0, 0, i * tq:ext, :ext]
            p = pb.astype(F32)
            dv_ref[0, :ext, :] += lax.dot_general(pb, dob, (((0,), (0,)), ((), ())), preferred_element_type=F32)
            dp = lax.dot_general(dob, vb[:ext], (((1,), (1,)), ((), ())), preferred_element_type=F32)
            ds = _bf(p * (dp - jnp.sum(p * dp, axis=1, keepdims=True)) * ATTN_SCALE)
            dq_ref[0, i * tq:ext, :] = lax.dot_general(ds, kb[:ext], (((1,), (0,)), ((), ())), preferred_element_type=F32)
            dk_ref[0, :ext, :] += lax.dot_general(ds, qb, (((0,), (0,)), ((), ())), preferred_element_type=F32)

    return pl.pallas_call(body, name="attn_bwd", grid=(b, HEADS),
                          in_specs=[blk, blk, blk, pl.BlockSpec((1, 1, s, s), lambda bi, h: (bi, h, 0, 0)), blk], out_specs=[blk, blk, blk],
                          out_shape=[_sds((b, s, hp))] * 3, compiler_params=_params(2))(q, k, v, probs, do)


LRU_CONV = 4


def _lru_point(pre_a, pre_x, xc, lam):
    r = jax.nn.sigmoid(pre_a)
    i = jax.nn.sigmoid(pre_x)
    a, gap = _decay(-LRU_C * r * _softplus(-lam))
    return a, jnp.sqrt(gap) * (i * xc)


def _causal_conv(pad_ref, x, halo, first_in_seq, w, taps):
    tm = x.shape[0]
    pad_ref[:HALO, :] = jnp.where(first_in_seq, 0.0, halo)
    pad_ref[HALO:, :] = x
    y = w[taps - 1:taps, :] * x
    for k in range(taps - 1):
        off = HALO - (taps - 1) + k
        y = y + w[k:k + 1, :] * pad_ref[off:off + tm, :]
    return y


def _conv_taps(pad_ref, r, cols, taps):
    blocks = [pad_ref[r + j * HALO:r + (j + 1) * HALO, cols] for j in range(1 + STRIP // HALO)]
    sub = lax.broadcasted_iota(jnp.int32, blocks[0].shape, 0)
    out = []
    for k in range(taps - 1):
        s = taps - 1 - k
        rolled = [pltpu.roll(b, s, 0) for b in blocks]
        out.append(jnp.concatenate([jnp.where(sub < s, rolled[j], rolled[j + 1]) for j in range(STRIP // HALO)], axis=0))
    out.append(jnp.concatenate(blocks[1:], axis=0))
    return out


def _causal_conv_wgrad(pad_ref, dy, taps):
    tm = dy.shape[0]
    return jnp.concatenate([_colsum(dy * pad_ref[HALO - (taps - 1) + k:HALO - (taps - 1) + k + tm, :]) for k in range(taps)], axis=0)


def _causal_conv_transpose(pad_ref, dy, halo_next, last_in_seq, w, taps):
    tm = dy.shape[0]
    pad_ref[:tm, :] = dy
    pad_ref[tm:, :] = jnp.where(last_in_seq, 0.0, halo_next)
    dx = w[taps - 1:taps, :] * dy
    for k in range(taps - 1):
        off = (taps - 1) - k
        dx = dx + w[k:k + 1, :] * pad_ref[off:off + tm, :]
    return dx


def _lru_fwd(xl, gate, w, ts, seq):
    t, n = xl.shape
    tiles_per_seq = seq // ts

    def body(xl_ref, halo_ref, gate_ref, cw_ref, cb_ref, wa_ref, ba_ref, wx_ref, bx_ref, lam_ref, y_out, h_out, pad_ref, a_ref, b_ref, carry_ref):
        first_in_seq = pl.program_id(0) % tiles_per_seq == 0
        xc = _causal_conv(pad_ref, xl_ref[...], halo_ref[...], first_in_seq, cw_ref[...], LRU_CONV) + cb_ref[...]
        a, bx = _lru_point(_nn(xc, wa_ref[...]) + ba_ref[...], _nn(xc, wx_ref[...]) + bx_ref[...], xc, lam_ref[...])
        a_ref[...] = a
        b_ref[...] = bx

        @pl.when(first_in_seq)
        def _():
            carry_ref[...] = jnp.zeros_like(carry_ref)

        def step(r, h):
            h = a_ref[pl.ds(r, 1), :] * h + b_ref[pl.ds(r, 1), :]
            h_out[pl.ds(r, 1), :] = h
            return h

        carry_ref[...] = lax.fori_loop(0, ts, step, carry_ref[...], unroll=8)
        y_out[...] = _bf(h_out[...] * _gelu(gate_ref[...]))

    return pl.pallas_call(
        body, name="lru_fwd", grid=(t // ts,),
        in_specs=[_row(ts, n), _prev_halo(ts, n), _row(ts, n), _const((LRU_CONV, n)), _const((1, n)), _const((n, n)), _const((1, n)),
                  _const((n, n)), _const((1, n)), _const((1, n))],
        out_specs=[_row(ts, n), _row(ts, n)], out_shape=[_sds((t, n), BF16), _sds((t, n))],
        scratch_shapes=[pltpu.VMEM((HALO + ts, n), F32), pltpu.VMEM((ts, n), F32), pltpu.VMEM((ts, n), F32), pltpu.VMEM((1, n), F32)],
        compiler_params=_params(),
    )(xl, xl, gate, w['ab_conv_w'], w['ab_conv_b'], w['Wa'], w['ab_b_rg_a'], w['Wx'], w['ab_b_rg_x'], w['ab_lambda'])


def _lru_bwd(xl, gate, hs, dy, w, ts, seq):
    t, n = xl.shape
    tiles_per_seq = seq // ts
    n_tiles = t // ts

    def rev(i):
        return n_tiles - 1 - i

    row = pl.BlockSpec((ts, n), lambda i: (rev(i), 0))
    prev = pl.BlockSpec((HALO, n), lambda i: (jnp.maximum(rev(i) * (ts // HALO) - 1, 0), 0))
    acc = lambda shape: pl.BlockSpec(shape, lambda i: (0,) * len(shape))

    def body(xl_ref, xhalo_ref, gate_ref, h_ref, hhalo_ref, dy_ref, cw_ref, cb_ref, wa_ref, ba_ref, wx_ref, bx_ref, lam_ref,
             dxl_out, dgate_out, dcw_out, dcb_out, dwa_out, dba_out, dwx_out, dbx_out, dlam_out,
             pad_ref, padh_ref, padd_ref, a_ref, g_ref, carry_ref, dhalo_ref):
        step_id = pl.program_id(0)
        first = step_id == 0
        tile = rev(step_id)
        first_in_seq = tile % tiles_per_seq == 0
        last_in_seq = tile % tiles_per_seq == tiles_per_seq - 1
        cw = cw_ref[...]
        xc = _causal_conv(pad_ref, xl_ref[...], xhalo_ref[...], first_in_seq, cw, LRU_CONV) + cb_ref[...]
        pre_a = _nn(xc, wa_ref[...]) + ba_ref[...]
        pre_x = _nn(xc, wx_ref[...]) + bx_ref[...]
        (a, _), vjp_point = jax.vjp(_lru_point, pre_a, pre_x, xc, lam_ref[...])
        h = h_ref[...]
        _, vjp_out = jax.vjp(lambda h_, g_: h_ * _gelu(g_), h, gate_ref[...])
        dh, dgate = vjp_out(dy_ref[...])
        dgate_out[...] = _bf(dgate)
        a_ref[...] = a
        g_ref[...] = dh

        @pl.when(last_in_seq)
        def _():
            carry_ref[...] = jnp.zeros_like(carry_ref)

        def step(j, c):
            r = ts - 1 - j
            g = g_ref[pl.ds(r, 1), :] + c
            g_ref[pl.ds(r, 1), :] = g
            return a_ref[pl.ds(r, 1), :] * g

        carry_ref[...] = lax.fori_loop(0, ts, step, carry_ref[...], unroll=8)
        g = g_ref[...]
        padh_ref[:HALO, :] = jnp.where(first_in_seq, 0.0, hhalo_ref[...])
        padh_ref[HALO:, :] = h
        dpre_a, dpre_x, dxc, dlam = vjp_point((g * padh_ref[HALO - 1:HALO - 1 + ts, :], g))
        dxc = dxc + _nt(dpre_a, wa_ref[...]) + _nt(dpre_x, wx_ref[...])
        _accumulate(dwa_out, _tn(xc, dpre_a), first)
        _accumulate(dwx_out, _tn(xc, dpre_x), first)
        _accumulate(dba_out, _colsum(dpre_a), first)
        _accumulate(dbx_out, _colsum(dpre_x), first)
        _accumulate(dlam_out, dlam, first)
        _accumulate(dcb_out, _colsum(dxc), first)
        _accumulate(dcw_out, _causal_conv_wgrad(pad_ref, dxc, LRU_CONV), first)
        dxl_out[...] = _bf(_causal_conv_transpose(padd_ref, dxc, dhalo_ref[...], last_in_seq, cw, LRU_CONV))
        dhalo_ref[...] = dxc[:HALO, :]

    return pl.pallas_call(
        body, name="lru_bwd", grid=(n_tiles,),
        in_specs=[row, prev, row, row, prev, row, _const((LRU_CONV, n)), _const((1, n)), _const((n, n)), _const((1, n)),
                  _const((n, n)), _const((1, n)), _const((1, n))],
        out_specs=[row, row, acc((LRU_CONV, n)), acc((1, n)), acc((n, n)), acc((1, n)), acc((n, n)), acc((1, n)), acc((1, n))],
        out_shape=[_sds((t, n), BF16), _sds((t, n), BF16), _sds((LRU_CONV, n)), _sds((1, n)), _sds((n, n)), _sds((1, n)), _sds((n, n)),
                   _sds((1, n)), _sds((1, n))],
        scratch_shapes=[pltpu.VMEM((HALO + ts, n), F32), pltpu.VMEM((HALO + ts, n), F32), pltpu.VMEM((ts + HALO, n), F32),
                        pltpu.VMEM((ts, n), F32), pltpu.VMEM((ts, n), F32), pltpu.VMEM((1, n), F32), pltpu.VMEM((HALO, n), F32)],
        compiler_params=_params(),
    )(xl, xl, gate, hs, hs, dy, w['ab_conv_w'], w['ab_conv_b'], w['Wa'], w['ab_b_rg_a'], w['Wx'], w['ab_b_rg_x'], w['ab_lambda'])


def _ab_out_fwd(x, o, y, w, tm):
    t, d = x.shape
    hp = o.shape[1]

    def body(x_ref, o_ref, y_ref, wa_ref, wb_ref, h_out):
        h_out[...] = x_ref[...] + _nn(o_ref[...], wa_ref[...]) + _nn(y_ref[...], wb_ref[...])

    return pl.pallas_call(body, name="ab_out_fwd", grid=(t // tm,),
                          in_specs=[_row(tm, d), _row(tm, hp), _row(tm, LRU_W), _const((hp, d)), _const((LRU_W, d))],
                          out_specs=_row(tm, d), out_shape=_sds((t, d)), compiler_params=_params())(x, o, y, w['Wo_a'], w['Wo_b'])


def _ab_out_bwd(o, y, dh, w, tm):
    t, d = dh.shape
    hp = o.shape[1]

    def body(o_ref, y_ref, dh_ref, wa_ref, wb_ref, do_out, dy_out, dwa_out, dwb_out):
        first = pl.program_id(0) == 0
        dh_t = dh_ref[...]
        do_out[...] = _bf(_nt(dh_t, wa_ref[...]))
        dy_out[...] = _nt(dh_t, wb_ref[...])
        _accumulate(dwa_out, _tn(o_ref[...], dh_t), first)
        _accumulate(dwb_out, _tn(y_ref[...], dh_t), first)

    return pl.pallas_call(body, name="ab_out_bwd", grid=(t // tm,),
                          in_specs=[_row(tm, hp), _row(tm, LRU_W), _row(tm, d), _const((hp, d)), _const((LRU_W, d))],
                          out_specs=[_row(tm, hp), _row(tm, LRU_W), _const((hp, d)), _const((LRU_W, d))],
                          out_shape=[_sds((t, hp), BF16), _sds((t, LRU_W)), _sds((hp, d)), _sds((LRU_W, d))],
                          compiler_params=_params())(o, y, dh, w['Wo_a'], w['Wo_b'])


FFN_CONV = 3


def _ffn_a_fwd(h, norm, wg, wu, tm):
    t, d = h.shape
    fb = D_FF // FF_BLOCKS

    def body(h_ref, gn_ref, wg_ref, wu_ref, g_out, u_out, hn_out):
        hn = _bf(_rms(h_ref[...], gn_ref[...]))
        hn_out[0] = hn
        g_out[...] = _nt(hn, wg_ref[...])
        u_out[...] = _nt(hn, wu_ref[...])

    wspec = pl.BlockSpec((fb, d), lambda f, i: (f, 0))
    ospec = pl.BlockSpec((tm, fb), lambda f, i: (i, f))
    return pl.pallas_call(
        body, name="ffn_a_fwd", grid=(FF_BLOCKS, t // tm),
        in_specs=[pl.BlockSpec((tm, d), lambda f, i: (i, 0)), pl.BlockSpec((1, d), lambda f, i: (0, 0)), wspec, wspec],
        out_specs=[ospec, ospec, pl.BlockSpec((1, tm, d), lambda f, i: (f, i, 0))],
        out_shape=[_sds((t, D_FF)), _sds((t, D_FF)), _sds((FF_BLOCKS, t, d), BF16)], compiler_params=_params(2))(h, norm, wg, wu)


def _ffn_b_fwd(g, u, h, cw, cb, wd, tm, seq, final=None):
    t, d = h.shape
    tiles_per_seq = seq // tm

    def body(g_ref, halo_ref, u_ref, h_ref, cw_ref, cb_ref, wd_ref, *rest):
        pad_ref, act_ref = rest[-2:]
        pad_ref[:HALO, :] = jnp.where(pl.program_id(0) % tiles_per_seq == 0, 0.0, halo_ref[...])
        pad_ref[HALO:, :] = g_ref[...]
        cw = cw_ref[...]
        cb = cb_ref[...]
        for c0 in range(0, D_FF, STRIP_LANES):
            cols = slice(c0, min(c0 + STRIP_LANES, D_FF))
            for r in range(0, tm, STRIP):
                taps = _conv_taps(pad_ref, r, cols, FFN_CONV)
                gc = cb[:, cols] + cw[0:1, cols] * taps[0] + cw[1:2, cols] * taps[1] + cw[2:3, cols] * taps[2]
                act_ref[r:r + STRIP, cols] = _bf(_gelu(gc) * u_ref[r:r + STRIP, cols])
        h_new = h_ref[...] + _nn(act_ref[...], wd_ref[...])
        if final is None:
            rest[0][...] = h_new
        else:
            tgt_ref, fn_ref, dh_out, loss_out, dfn_out = rest[:5]
            first = pl.program_id(0) == 0
            loss, dh_out[...], dfn = _loss_and_grad(h_new, tgt_ref[...], fn_ref[...])
            _accumulate(loss_out, loss, first)
            _accumulate(dfn_out, dfn, first)

    in_specs = [_row(tm, D_FF), _prev_halo(tm, D_FF), _row(tm, D_FF), _row(tm, d), _const((FFN_CONV, D_FF)), _const((1, D_FF)), _const((D_FF, d))]
    scratch = [pltpu.VMEM((HALO + tm, D_FF), F32), pltpu.VMEM((tm, D_FF), BF16)]
    if final is None:
        return pl.pallas_call(body, name="ffn_b_fwd", grid=(t // tm,), in_specs=in_specs, out_specs=_row(tm, d), out_shape=_sds((t, d)),
                              scratch_shapes=scratch, compiler_params=_params())(g, g, u, h, cw, cb, wd)
    return pl.pallas_call(body, name="ffn_b_fwd_loss", grid=(t // tm,), in_specs=in_specs + [_row(tm, d), _const((1, d))],
                          out_specs=[_row(tm, d), _const((1, 1)), _const((1, d))],
                          out_shape=[_sds((t, d)), _sds((1, 1)), _sds((1, d))],
                          scratch_shapes=scratch, compiler_params=_params())(g, g, u, h, cw, cb, wd, *final)


def _ffn_b_bwd(g, u, dout, cw, cb, wd, tm, seq):
    t, d = dout.shape
    fb = D_FF // FF_BLOCKS
    tiles_per_seq = seq // tm

    def body(g_ref, halo_ref, u_ref, dout_ref, cw_ref, cb_ref, wd_ref, dgc_out, du_out, dwd_out, dcw_out, dcb_out,
             pad_ref, dact_ref, act_ref, acc_ref, dwd_acc):
        i = pl.program_id(1)
        first = i == 0
        pad_ref[:HALO, :] = jnp.where(i % tiles_per_seq == 0, 0.0, halo_ref[...])
        pad_ref[HALO:, :] = g_ref[...]
        dout_b = _bf(dout_ref[...])
        dact_ref[...] = _nt(dout_b, wd_ref[...])
        cw = cw_ref[...]
        cb = cb_ref[...]
        fold = lambda a: a[:HALO] + a[HALO:]
        for c0 in range(0, fb, STRIP_LANES):
            cols = slice(c0, min(c0 + STRIP_LANES, fb))
            sums = [jnp.zeros((HALO, cols.stop - c0), F32) for _ in range(1 + FFN_CONV)]
            for r in range(0, tm, STRIP):
                rows = slice(r, r + STRIP)
                taps = _conv_taps(pad_ref, r, cols, FFN_CONV)
                gelu, dgelu = _gelu_and_grad(cb[:, cols] + cw[0:1, cols] * taps[0] + cw[1:2, cols] * taps[1] + cw[2:3, cols] * taps[2])
                u = u_ref[rows, cols]
                dact = dact_ref[rows, cols]
                act_ref[rows, cols] = _bf(gelu * u)
                du_out[rows, cols] = _bf(dact * gelu)
                dgc = dact * u * dgelu
                dgc_out[rows, cols] = dgc
                sums = [sums[0] + fold(dgc)] + [sums[1 + k] + fold(dgc * taps[k]) for k in range(FFN_CONV)]
            for k in range(1 + FFN_CONV):
                acc_ref[k, :, cols] = sums[k]
        _accumulate(dwd_acc, _tn(act_ref[...], dout_b), first)

        @pl.when(i == t // tm - 1)
        def _():
            dwd_out[...] = _bf(dwd_acc[...])

        _accumulate(dcb_out, _colsum(acc_ref[0]), first)
        _accumulate(dcw_out, jnp.concatenate([_colsum(acc_ref[1 + k]) for k in range(FFN_CONV)], axis=0), first)

    blk = pl.BlockSpec((tm, fb), lambda f, i: (i, f))
    halo = pl.BlockSpec((HALO, fb), lambda f, i: (jnp.maximum(i * (tm // HALO) - 1, 0), f))
    wd_blk = pl.BlockSpec((fb, d), lambda f, i: (f, 0), pipeline_mode=pl.Buffered(1))
    return pl.pallas_call(
        body, name="ffn_b_bwd", grid=(FF_BLOCKS, t // tm),
        in_specs=[blk, halo, blk, pl.BlockSpec((tm, d), lambda f, i: (i, 0)), pl.BlockSpec((FFN_CONV, fb), lambda f, i: (0, f)),
                  pl.BlockSpec((1, fb), lambda f, i: (0, f)), wd_blk],
        out_specs=[blk, blk, wd_blk, pl.BlockSpec((FFN_CONV, fb), lambda f, i: (0, f)),
                   pl.BlockSpec((1, fb), lambda f, i: (0, f))],
        out_shape=[_sds((t, D_FF)), _sds((t, D_FF), BF16), _sds((D_FF, d), BF16), _sds((FFN_CONV, D_FF)), _sds((1, D_FF))],
        scratch_shapes=[pltpu.VMEM((HALO + tm, fb), F32), pltpu.VMEM((tm, fb), F32), pltpu.VMEM((tm, fb), BF16),
                        pltpu.VMEM((1 + FFN_CONV, HALO, fb), F32), pltpu.VMEM((fb, d), F32)],
        compiler_params=_params(2))(g, g, u, dout, cw, cb, wd)


def _ffn_a_dgrad(h, norm, dgc, du, dres, cw, wg, wu, tm, seq):
    t, d = h.shape
    tiles_per_seq = seq // tm
    n_tiles = t // tm

    def body(h_ref, gn_ref, dgc_ref, halo_ref, du_ref, dres_ref, cw_ref, wg_ref, wu_ref, dh_out, dg_out, dgn_out, pad_ref):
        i = pl.program_id(0)
        last_in_seq = i % tiles_per_seq == tiles_per_seq - 1
        dg = _bf(_causal_conv_transpose(pad_ref, dgc_ref[...], halo_ref[...], last_in_seq, cw_ref[...], FFN_CONV))
        dg_out[...] = dg
        _, vjp_norm = jax.vjp(_rms, h_ref[...], gn_ref[...])
        dh, dgn = vjp_norm(_nn(dg, wg_ref[...]) + _nn(du_ref[...], wu_ref[...]))
        dh_out[...] = dh + dres_ref[...]
        _accumulate(dgn_out, dgn, i == 0)

    return pl.pallas_call(
        body, name="ffn_a_dgrad", grid=(n_tiles,),
        in_specs=[_row(tm, d), _const((1, d)), _row(tm, D_FF), _next_halo(tm, D_FF, n_tiles), _row(tm, D_FF), _row(tm, d),
                  _const((FFN_CONV, D_FF)), _const((D_FF, d)), _const((D_FF, d))],
        out_specs=[_row(tm, d), _row(tm, D_FF), _const((1, d))], out_shape=[_sds((t, d)), _sds((t, D_FF), BF16), _sds((1, d))],
        scratch_shapes=[pltpu.VMEM((tm + HALO, D_FF), F32)], compiler_params=_params())(h, norm, dgc, dgc, du, dres, cw, wg, wu)


def _ffn_a_wgrad(hn, dg, du, tm):
    _, t, d = hn.shape
    fb = D_FF // FF_BLOCKS

    n_tiles = t // tm

    def body(hn_ref, dg_ref, du_ref, dwg_out, dwu_out, acc_g, acc_u):
        i = pl.program_id(1)
        hn_t = hn_ref[0]
        _accumulate(acc_g, _tn(dg_ref[...], hn_t), i == 0)
        _accumulate(acc_u, _tn(du_ref[...], hn_t), i == 0)

        @pl.when(i == n_tiles - 1)
        def _():
            dwg_out[...] = _bf(acc_g[...])
            dwu_out[...] = _bf(acc_u[...])

    blk = pl.BlockSpec((tm, fb), lambda f, i: (i, f))
    wspec = pl.BlockSpec((fb, d), lambda f, i: (f, 0), pipeline_mode=pl.Buffered(1))
    return pl.pallas_call(body, name="ffn_a_wgrad", grid=(FF_BLOCKS, n_tiles),
                          in_specs=[pl.BlockSpec((1, tm, d), lambda f, i: (0, i, 0)), blk, blk],
                          out_specs=[wspec, wspec], out_shape=[_sds((D_FF, d), BF16), _sds((D_FF, d), BF16)],
                          scratch_shapes=[pltpu.VMEM((fb, d), F32), pltpu.VMEM((fb, d), F32)],
                          compiler_params=_params(2))(hn, dg, du)


def _sgu_mix(vn, ws_ref, bst):
    tril = lax.broadcasted_iota(jnp.int32, (CHUNK, CHUNK), 0) >= lax.broadcasted_iota(jnp.int32, (CHUNK, CHUNK), 1)
    wms = [jnp.where(tril, ws_ref[g], 0.0) for g in range(SGU_GROUPS)]
    chunks = []
    for n in range(vn.shape[0] // CHUNK):
        vc = vn[n * CHUNK:(n + 1) * CHUNK, :]
        chunks.append(jnp.concatenate(
            [_nn(wms[g], vc[:, g * CHUNK:(g + 1) * CHUNK]) + bst[:, g:g + 1] for g in range(SGU_GROUPS)], axis=1))
    return jnp.concatenate(chunks, axis=0)


def _sgu_fwd(h, w, tm):
    t, d = h.shape

    def body(h_ref, cn_ref, win_ref, lg_ref, lb_ref, ws_ref, bst_ref, wout_ref, h_out):
        h_t = h_ref[...]
        z = _gelu(_nn(_rms(h_t, cn_ref[...]), win_ref[...]))
        vn = _layer_norm(z[:, d:], lg_ref[...], lb_ref[...])
        s = _sgu_mix(vn, ws_ref, bst_ref[...])
        h_out[...] = h_t + _nn(z[:, :d] * s, wout_ref[...])

    return pl.pallas_call(
        body, name="sgu_fwd", grid=(t // tm,),
        in_specs=[_row(tm, d), _const((1, d)), _const((d, 2 * d)), _const((1, d)), _const((1, d)), _const((SGU_GROUPS, CHUNK, CHUNK)),
                  _const((CHUNK, LANES)), _const((d, d))],
        out_specs=_row(tm, d), out_shape=_sds((t, d)), compiler_params=_params(),
    )(h, w['c_norm'], w['c_w_in'], w['c_ln_g'], w['c_ln_b'], w['c_w_s'], w['bsT'], w['c_w_out'])


def _sgu_bwd(h, dout, w, tm):
    t, d = h.shape

    def body(h_ref, dout_ref, cn_ref, win_ref, lg_ref, lb_ref, ws_ref, bst_ref, wout_ref,
             dh_out, dcn_out, dwin_out, dlg_out, dlb_out, dws_out, dbst_out, dwout_out):
        first = pl.program_id(0) == 0
        hn, vjp_norm = jax.vjp(_rms, h_ref[...], cn_ref[...])
        zpre = _nn(hn, win_ref[...])
        u, vjp_u = jax.vjp(_gelu, zpre[:, :d])
        vn, vjp_v = jax.vjp(lambda zp, lg, lb: _layer_norm(_gelu(zp), lg, lb), zpre[:, d:], lg_ref[...], lb_ref[...])
        s = _sgu_mix(vn, ws_ref, bst_ref[...])
        dout_t = dout_ref[...]
        dus = _nt(dout_t, wout_ref[...])
        _accumulate(dwout_out, _tn(u * s, dout_t), first)
        ds = dus * u
        tril = lax.broadcasted_iota(jnp.int32, (CHUNK, CHUNK), 0) >= lax.broadcasted_iota(jnp.int32, (CHUNK, CHUNK), 1)
        lane = lax.broadcasted_iota(jnp.int32, (CHUNK, LANES), 1)
        dws = [jnp.zeros((CHUNK, CHUNK), F32) for _ in range(SGU_GROUPS)]
        dbst = jnp.zeros((CHUNK, LANES), F32)
        dvn_chunks = []
        for n in range(tm // CHUNK):
            cols = []
            for g in range(SGU_GROUPS):
                ds_ng = ds[n * CHUNK:(n + 1) * CHUNK, g * CHUNK:(g + 1) * CHUNK]
                vc_ng = vn[n * CHUNK:(n + 1) * CHUNK, g * CHUNK:(g + 1) * CHUNK]
                cols.append(_tn(jnp.where(tril, ws_ref[g], 0.0), ds_ng))
                dws[g] = dws[g] + _nt(ds_ng, vc_ng)
                dbst = dbst + jnp.where(lane == g, jnp.sum(ds_ng, axis=1, keepdims=True), 0.0)
            dvn_chunks.append(jnp.concatenate(cols, axis=1))
        dvn = jnp.concatenate(dvn_chunks, axis=0)
        for g in range(SGU_GROUPS):
            val = jnp.where(tril, dws[g], 0.0)

            @pl.when(first)
            def _():
                dws_out[g] = val

            @pl.when(jnp.logical_not(first))
            def _():
                dws_out[g] += val
        _accumulate(dbst_out, dbst, first)
        (dzu,) = vjp_u(dus * s)
        dzv, dlg, dlb = vjp_v(dvn)
        _accumulate(dlg_out, dlg, first)
        _accumulate(dlb_out, dlb, first)
        dzpre = jnp.concatenate([dzu, dzv], axis=1)
        _accumulate(dwin_out, _tn(hn, dzpre), first)
        dh, dcn = vjp_norm(_nt(dzpre, win_ref[...]))
        _accumulate(dcn_out, dcn, first)
        dh_out[...] = dh + dout_t

    return pl.pallas_call(
        body, name="sgu_bwd", grid=(t // tm,),
        in_specs=[_row(tm, d), _row(tm, d), _const((1, d)), _const((d, 2 * d)), _const((1, d)), _const((1, d)),
                  _const((SGU_GROUPS, CHUNK, CHUNK)), _const((CHUNK, LANES)), _const((d, d))],
        out_specs=[_row(tm, d), _const((1, d)), _const((d, 2 * d)), _const((1, d)), _const((1, d)), _const((SGU_GROUPS, CHUNK, CHUNK)),
                   _const((CHUNK, LANES)), _const((d, d))],
        out_shape=[_sds((t, d)), _sds((1, d)), _sds((d, 2 * d)), _sds((1, d)), _sds((1, d)), _sds((SGU_GROUPS, CHUNK, CHUNK)),
                   _sds((CHUNK, LANES)), _sds((d, d))],
        compiler_params=_params(),
    )(h, dout, w['c_norm'], w['c_w_in'], w['c_ln_g'], w['c_ln_b'], w['c_w_s'], w['bsT'], w['c_w_out'])


def _loss_and_grad(h, tgt, g):
    def loss_fn(h_, g_):
        err = _rms(h_, g_) - tgt
        return 0.5 * jnp.sum(jnp.mean(err * err, axis=-1, keepdims=True), axis=0, keepdims=True)

    loss, vjp_loss = jax.vjp(loss_fn, h, g)
    return (loss,) + vjp_loss(jnp.ones((1, 1), F32))


def _tile(t, seq, want):
    tm = min(want, seq)
    assert seq % tm == 0 and t % tm == 0 and tm % CHUNK == 0
    return tm


def _local_step(x, posb, target, w, seq, late_weights, on_grads):
    t, d = x.shape
    b = t // seq
    hp = HEADS * HEAD_PAD
    tm_big, tm_mid = _tile(t, seq, 512), _tile(t, seq, 256)
    tq = _tile(t, seq, 512)

    q, k, v, xl, gate = _ab_in_fwd(x, posb, w, tm_big)
    o, probs = _attn_fwd(q.reshape(b, seq, hp), k.reshape(b, seq, hp), v.reshape(b, seq, hp), tq)
    o = o.reshape(t, hp)
    y, hs = _lru_fwd(xl, gate, w, tm_big, seq)
    w = {**w, **late_weights('out0', y)}
    h1 = _ab_out_fwd(x, o, y, w, tm_big)
    hcur = h1
    saved = []
    for l in range(2):
        if l == 1:
            w = {**w, **late_weights('mix1', hcur)}
            saved_h2 = hcur
            hcur = _sgu_fwd(hcur, w, tm_big)
        wl = late_weights('ffn%d' % l, hcur)
        g, u, hn = _ffn_a_fwd(hcur, w['ffn_norm'][l], wl['Wg'], wl['Wu'], tm_big)
        saved.append((hcur, g, u, wl, hn))
        ffn_b = (g, u, hcur, w['ffn_conv_w'][l], w['ffn_conv_b'][l], wl['Wd'], tm_big, seq)
        if l == 0:
            hcur = _ffn_b_fwd(*ffn_b)
    dh, loss, d_final = _ffn_b_fwd(*ffn_b, final=(target, w['final_norm']))

    ffn = {}
    conv_b = list(w['ffn_conv_b'])
    for l in (1, 0):
        hin, g, u, wl, hn = saved[l]
        dgc, du, d_wd, d_cw, d_cb = _ffn_b_bwd(g, u, dh, w['ffn_conv_w'][l], conv_b[l], wl['Wd'], tm_big, seq)
        dh, dg, d_norm = _ffn_a_dgrad(hin, w['ffn_norm'][l], dgc, du, dh, w['ffn_conv_w'][l], wl['Wg'], wl['Wu'], tm_big, seq)
        d_wg, d_wu = _ffn_a_wgrad(hn, dg, du, _tile(t, seq, 1024))
        ffn[l] = dict(ffn_norm=d_norm, ffn_conv_w=d_cw, ffn_conv_b=d_cb, Wg=d_wg, Wu=d_wu, Wd=d_wd)
        if l == 1:
            dh, d_cn, d_cwin, d_lg, d_lb, d_ws, d_bst, d_cwout = _sgu_bwd(saved_h2, dh, w, tm_mid)
            zero = on_grads('late1', dict(final_norm=d_final, c_norm=d_cn, c_ln_g=d_lg, c_ln_b=d_lb, c_w_s=d_ws, bsT=d_bst, c_w_in=d_cwin,
                                          c_w_out=d_cwout, Wg=[d_wg], Wu=[d_wu], Wd=[d_wd]))
            conv_b[0] = conv_b[0] + zero
    late0 = {name: [ffn[0][name], ffn[1][name]] for name in ('ffn_norm', 'ffn_conv_w', 'ffn_conv_b')}
    zero = on_grads('late0', dict(late0, Wg=[ffn[0]['Wg']], Wu=[ffn[0]['Wu']], Wd=[ffn[0]['Wd']]))
    w = {**w, 'Wo_b': w['Wo_b'] + zero.astype(w['Wo_b'].dtype)}
    do, dy, d_woa, d_wob = _ab_out_bwd(o, y, dh, w, tm_big)
    dxl, dgate, d_cw, d_cb, d_wa, d_ba, d_wx, d_bx, d_lam = _lru_bwd(xl, gate, hs, dy, w, tm_big, seq)
    zero = on_grads('mid', dict(Wo_a=d_woa, Wo_b=d_wob, ab_conv_w=d_cw, ab_conv_b=d_cb, Wa=d_wa, ab_b_rg_a=d_ba, Wx=d_wx,
                                ab_b_rg_x=d_bx, ab_lambda=d_lam))
    w = {**w, 'ab_norm': w['ab_norm'] + zero}
    dq, dk, dv = _attn_bwd(q.reshape(b, seq, hp), k.reshape(b, seq, hp), v.reshape(b, seq, hp), probs, do.reshape(b, seq, hp), tq)
    dx, d_gn, d_win, d_qn, d_wq, d_kvn, d_wk, d_wv = _ab_in_bwd(
        x, posb, w, dq.reshape(t, hp), dk.reshape(t, hp), dv.reshape(t, hp), dxl, dgate, dh, tm_big)
    return loss, dx, dict(ab_norm=d_gn, W_in=d_win, ab_q_norm=d_qn, Wq=d_wq, ab_kv_norm=d_kvn, Wk=d_wk, Wv=d_wv)


def _block_diag(wg):
    g, n, _ = wg.shape
    return jnp.einsum('gij,gh->gihj', wg, jnp.eye(g, dtype=wg.dtype)).reshape(g * n, g * n)


def _prepare_out(w_out):
    d = w_out.shape[2]
    mla = HEADS * QK_NOPE
    return {'Wo_a': jnp.pad(w_out[0, :mla].reshape(HEADS, QK_NOPE, d), ((0, 0), (0, HEAD_PAD - QK_NOPE), (0, 0))).reshape(HEADS * HEAD_PAD, d),
            'Wo_b': w_out[0, mla:]}


def _prepare(full):
    d = full['ab_w_in'].shape[1]
    w_in = full['ab_w_in'][0]
    zeros = lambda n: jnp.zeros((d, n), w_in.dtype)
    wq = full['ab_w_q_b'][0].reshape(Q_LORA, HEADS, QK_NOPE + QK_ROPE)
    wkv = full['ab_w_kv_b'][0].reshape(KV_LORA, HEADS, 2 * QK_NOPE)
    pad_head = lambda a: jnp.pad(a, ((0, 0), (0, 0), (0, HEAD_PAD - a.shape[2]))).reshape(a.shape[0], HEADS * HEAD_PAD)
    w = {
        'W_in': jnp.concatenate([w_in[:, :Z_KPE], zeros(QK_NOPE), w_in[:, Z_KPE:Z_KPE + QK_ROPE],
                                 zeros(HEAD_PAD - QK_NOPE - QK_ROPE), w_in[:, Z_KPE + QK_ROPE:]], axis=1),
        'Wq': pad_head(wq), 'Wk': pad_head(wkv[:, :, :QK_NOPE]), 'Wv': pad_head(wkv[:, :, QK_NOPE:]),
        'Wa': _bf(_block_diag(full['ab_w_rg_a'][0])), 'Wx': _bf(_block_diag(full['ab_w_rg_x'][0])),
        'c_w_s': full['c_w_s'][0],
        'bsT': jnp.pad(full['c_b_s'][0].T, ((0, 0), (0, LANES - SGU_GROUPS))),
        'ffn_norm': [full['ffn_norm'][l:l + 1] for l in range(2)], 'ffn_conv_w': [full['ffn_conv_w'][l] for l in range(2)],
        'ffn_conv_b': [full['ffn_conv_b'][l:l + 1] for l in range(2)],
        'ab_conv_w': full['ab_conv_w'][0], 'final_norm': full['final_norm'][None, :],
    }
    for name in ('ab_norm', 'ab_q_norm', 'ab_kv_norm', 'ab_conv_b', 'ab_b_rg_a', 'ab_b_rg_x', 'ab_lambda', 'c_norm', 'c_ln_g', 'c_ln_b'):
        w[name] = full[name]
    return w


def _unprepare(g):
    unpad_head = lambda a, n: a.reshape(a.shape[0], HEADS, HEAD_PAD)[:, :, :n]
    diag = lambda a: jnp.einsum('gigj->gij', a.reshape(HEADS, LRU_W // HEADS, HEADS, LRU_W // HEADS))
    rules = {
        'ab_w_in': (('W_in',), lambda a: jnp.concatenate([a[:, :Z_KPE], a[:, Z_KPE + QK_NOPE:Z_KPE + QK_NOPE + QK_ROPE], a[:, Z_LRU:]], axis=1)[None]),
        'ab_w_q_b': (('Wq',), lambda a: unpad_head(a, QK_NOPE + QK_ROPE).reshape(1, Q_LORA, -1)),
        'ab_w_kv_b': (('Wk', 'Wv'), lambda a, b: jnp.concatenate([unpad_head(a, QK_NOPE), unpad_head(b, QK_NOPE)], axis=2).reshape(1, KV_LORA, -1)),
        'ab_w_out': (('Wo_a', 'Wo_b'), lambda a, b: jnp.concatenate(
            [a.reshape(HEADS, HEAD_PAD, -1)[:, :QK_NOPE].reshape(HEADS * QK_NOPE, -1), b], axis=0)[None]),
        'ab_w_rg_a': (('Wa',), lambda a: diag(a)[None]), 'ab_w_rg_x': (('Wx',), lambda a: diag(a)[None]),
        'c_w_in': (('c_w_in',), lambda a: a[None]), 'c_w_out': (('c_w_out',), lambda a: a[None]), 'c_w_s': (('c_w_s',), lambda a: a[None]),
        'c_b_s': (('bsT',), lambda a: a[:, :SGU_GROUPS].T[None]),
        'ffn_w_gate': (('Wg',), jnp.stack), 'ffn_w_up': (('Wu',), jnp.stack), 'ffn_w_down': (('Wd',), jnp.stack),
        'ffn_norm': (('ffn_norm',), lambda a: jnp.concatenate(a, axis=0)), 'ffn_conv_w': (('ffn_conv_w',), jnp.stack),
        'ffn_conv_b': (('ffn_conv_b',), lambda a: jnp.concatenate(a, axis=0)),
        'ab_conv_w': (('ab_conv_w',), lambda a: a[None]), 'final_norm': (('final_norm',), lambda a: a[0]),
    }
    for name in ('ab_norm', 'ab_q_norm', 'ab_kv_norm', 'ab_conv_b', 'ab_b_rg_a', 'ab_b_rg_x', 'ab_lambda', 'c_norm', 'c_ln_g', 'c_ln_b'):
        rules[name] = ((name,), lambda a: a)
    return {name: fn(*[g[k] for k in keys]) for name, (keys, fn) in rules.items() if all(k in g for k in keys)}


SLAB_ROWS = 16


def _round_up(n, m):
    return -(-n // m) * m


def _to_chunks(full, axis):
    s = full.shape
    return jnp.moveaxis(full.reshape(s[:axis] + (N_DEV, s[axis] // N_DEV) + s[axis + 1:]), axis, 0)


def _from_chunks(chunks, axis):
    local = chunks.shape[1:]
    return jnp.moveaxis(chunks, 0, axis).reshape(local[:axis] + (N_DEV * local[axis],) + local[axis + 1:])


def _merge_columns(landed, name):
    _, _, r, n = landed.shape
    tr = r // 4

    def body(l_ref, o_ref):
        o_ref[0] = jnp.concatenate([l_ref[dev, 0] for dev in range(N_DEV)], axis=1)

    return pl.pallas_call(body, name="merge_" + name, grid=(r // tr,),
                          in_specs=[pl.BlockSpec((N_DEV, 1, tr, n), lambda i: (0, 0, i, 0))],
                          out_specs=pl.BlockSpec((1, tr, N_DEV * n), lambda i: (0, i, 0)),
                          out_shape=jax.ShapeDtypeStruct((1, r, N_DEV * n), landed.dtype), compiler_params=_params())(landed)


def _split_chunks(whole, axis, name):
    _, rows, cols = whole.shape
    if axis == 1:
        r = rows // N_DEV

        def body(x_ref, o_ref):
            o_ref[0] = _bf(x_ref[...])

        grid, out_shape = (N_DEV,), (N_DEV, 1, r, cols)
        spec, out_spec = pl.BlockSpec((1, r, cols), lambda dev: (0, dev, 0)), pl.BlockSpec((1, 1, r, cols), lambda dev: (dev, 0, 0, 0))
    else:
        n, tr = cols // N_DEV, rows // 4

        def body(x_ref, o_ref):
            x = x_ref[0]
            for dev in range(N_DEV):
                o_ref[dev, 0] = _bf(x[:, dev * n:(dev + 1) * n])

        grid, out_shape = (rows // tr,), (N_DEV, 1, rows, n)
        spec, out_spec = pl.BlockSpec((1, tr, cols), lambda i: (0, i, 0)), pl.BlockSpec((N_DEV, 1, tr, n), lambda i: (0, 0, i, 0))
    return pl.pallas_call(body, name="split_" + name, grid=grid, in_specs=[spec], out_specs=out_spec,
                          out_shape=jax.ShapeDtypeStruct(out_shape, BF16), compiler_params=_params())(whole)


def _slab_rows(n):
    return _round_up(-(-n // LANES), SLAB_ROWS)


def _to_slab(a, lead):
    a = a.reshape(lead + (-1,))
    rows = _slab_rows(a.shape[-1])
    a = jnp.pad(a, [(0, 0)] * len(lead) + [(0, rows * LANES - a.shape[-1])])
    return a.reshape(lead + (rows, LANES))


def _pack_slabs(parts, lead):
    return jnp.concatenate([_to_slab(p, lead) for p in parts], axis=len(lead))


def _unpack_slabs(packed, shapes):
    lead = packed.shape[:-2]
    out, row = [], 0
    for shape in shapes:
        size = math.prod(shape)
        rows = _slab_rows(size)
        piece = lax.slice_in_dim(packed, row, row + rows, axis=len(lead))
        out.append(piece.reshape(lead + (rows * LANES,))[..., :size].reshape(lead + tuple(shape)))
        row += rows
    return out


HBM = pl.BlockSpec(memory_space=pl.ANY)


def _other_chips(x, y):
    return [(1 - x, y), (x, 1 - y), (1 - x, 1 - y)]


def _all_gather(blocks):
    n = len(blocks)

    def body(*refs):
        x_refs, out_refs, token = refs[:n], refs[n:2 * n], refs[2 * n]
        send_sems, recv_sems, local_sems = refs[2 * n + 1:]
        token[...] = jnp.zeros_like(token)
        x, y, c = lax.axis_index("x"), lax.axis_index("y"), lax.axis_index("c")
        me, sibling = (x, y, c), (x, y, 1 - c)
        chips = _other_chips(x, y)

        def slab(a, px, py, pc):
            return out_refs[a].at[4 * px + 2 * py + pc]

        def copy(a, k, blk, to, src=None):
            return pltpu.make_async_remote_copy(src_ref=slab(a, *blk) if src is None else src, dst_ref=slab(a, *blk),
                                                send_sem=send_sems.at[7 * a + k], recv_sem=recv_sems.at[7 * a + k],
                                                device_id=to, device_id_type=MESH)

        mine = [pltpu.make_async_copy(x_refs[a], slab(a, *me), local_sems.at[a]) for a in range(n)]
        started = []
        for a in range(n):
            mine[a].start()
            started.append(copy(a, 0, me, sibling, src=x_refs[a]))
            started += [copy(a, 1 + j, me, (*chip, c), src=x_refs[a]) for j, chip in enumerate(chips)]
        for cp in started:
            cp.start()
        for j, chip in enumerate(chips):
            for a in range(n):
                copy(a, 1 + j, (*chip, c), me).wait_recv()
                passed = copy(a, 4 + j, (*chip, c), sibling)
                passed.start()
                started.append(passed)
        for a in range(n):
            copy(a, 0, sibling, me).wait_recv()
        for j, chip in enumerate(chips):
            for a in range(n):
                copy(a, 4 + j, (*chip, 1 - c), me).wait_recv()
        for cp in started:
            cp.wait_send()
        for a in range(n):
            mine[a].wait()

    out = pl.pallas_call(
        body, name="all_gather_weights",
        out_shape=[jax.ShapeDtypeStruct((N_DEV,) + b.shape, b.dtype) for b in blocks] + [jax.ShapeDtypeStruct((8, LANES), F32)],
        in_specs=[HBM] * n, out_specs=[HBM] * n + [pl.BlockSpec(memory_space=pltpu.VMEM)],
        scratch_shapes=[pltpu.SemaphoreType.DMA((7 * n,)), pltpu.SemaphoreType.DMA((7 * n,)), pltpu.SemaphoreType.DMA((n,))],
    )(*blocks)
    return list(out[:n]), out[n][0, 0]


FLIPS = [(0, 0, 1), (1, 0, 0), (1, 0, 1), (0, 1, 0), (0, 1, 1), (1, 1, 0), (1, 1, 1)]


def _peers(x, y, c):
    flip = lambda v, f: 1 - v if f else v
    return [(flip(x, fx), flip(y, fy), flip(c, fc)) for fx, fy, fc in FLIPS]


def _direct_copies(src_refs, land_refs, send_sems, recv_sems, scatter):
    x, y, c = lax.axis_index("x"), lax.axis_index("y"), lax.axis_index("c")
    me = 4 * x + 2 * y + c
    starts, waits = [], []
    for a in range(len(src_refs)):
        for k, (px, py, pc) in enumerate(_peers(x, y, c)):
            peer = 4 * px + 2 * py + pc
            sems = dict(send_sem=send_sems.at[7 * a + k], recv_sem=recv_sems.at[7 * a + k], device_id=(px, py, pc), device_id_type=MESH)
            src = src_refs[a].at[peer] if scatter else src_refs[a]
            starts.append(pltpu.make_async_remote_copy(src_ref=src, dst_ref=land_refs[a].at[me], **sems))
            waits.append(pltpu.make_async_remote_copy(src_ref=src, dst_ref=land_refs[a].at[peer], **sems))
    n = len(src_refs)
    keeps = [] if scatter else [pltpu.make_async_copy(src_refs[a], land_refs[a].at[me], send_sems.at[7 * n + a]) for a in range(n)]
    return starts, waits, keeps


def _landing(src, scatter):
    block = src.shape[1:] if scatter else src.shape
    return jax.ShapeDtypeStruct((N_DEV,) + block, src.dtype)


HBM_SPACE = pl.BlockSpec(memory_space=pltpu.HBM)
SEMAPHORES = pl.BlockSpec(memory_space=pltpu.SEMAPHORE)
SPLIT_EFFECT = pltpu.SideEffectType.DATAFLOW_SIDE_EFFECTING


def _start_exchange(name, srcs, scatter):
    n = len(srcs)
    lands = [lax.empty(s.shape, s.dtype) for s in (_landing(s, scatter) for s in srcs)]

    def body(*refs):
        starts, _, keeps = _direct_copies(refs[:n], refs[n:2 * n], refs[2 * n], refs[2 * n + 1], scatter)
        for cp in starts + keeps:
            cp.start()
        refs[-1][...] = jnp.zeros_like(refs[-1])

    held = [pltpu.with_memory_space_constraint(a, pltpu.HBM) for a in list(srcs) + lands]
    out = pl.pallas_call(
        body, name=name + "_start",
        out_shape=(pltpu.SemaphoreType.DMA(((7 if scatter else 8) * n,)), pltpu.SemaphoreType.DMA((7 * n,)),
                   *[pltpu.HBM(a.shape, a.dtype) for a in held],
                   jax.ShapeDtypeStruct((8, LANES), F32)),
        in_specs=[HBM_SPACE] * (2 * n), out_specs=(SEMAPHORES, SEMAPHORES, *[HBM_SPACE] * (2 * n), pl.BlockSpec(memory_space=pltpu.VMEM)),
        input_output_aliases={i: 2 + i for i in range(2 * n)},
        compiler_params=pltpu.CompilerParams(has_side_effects=SPLIT_EFFECT),
    )(*held)
    return out[0], out[1], list(out[2:2 + n]), list(out[2 + n:2 + 2 * n]), out[-1][0, 0], out[-1]


def _wait_exchange(name, started, after, scatter):
    send_sems, recv_sems, srcs, lands = started[:4]
    n = len(srcs)

    def body(*refs):
        _, waits, keeps = _direct_copies(refs[:n], refs[n:2 * n], refs[2 * n], refs[2 * n + 1], scatter)
        for cp in waits:
            cp.wait_send()
        for cp in waits:
            cp.wait_recv()
        for cp in keeps:
            cp.wait()

    out = pl.pallas_call(
        body, name=name + "_wait", out_shape=tuple(pltpu.HBM(a.shape, a.dtype) for a in srcs + lands),
        in_specs=[HBM_SPACE] * (2 * n) + [SEMAPHORES, SEMAPHORES, HBM], out_specs=tuple([HBM_SPACE] * (2 * n)),
        input_output_aliases={i: i for i in range(2 * n)},
        compiler_params=pltpu.CompilerParams(has_side_effects=SPLIT_EFFECT),
    )(*srcs, *lands, send_sems, recv_sems, after)
    return list(out[:n]), list(out[n:])


def _row_tile(rows):
    return rows // 2 if (rows // 2) % SLAB_ROWS == 0 else rows


def _sum_in_device_order(me_ref, l_ref, own_ref):
    mine = own_ref[0].astype(F32)
    g = jnp.where(me_ref[0] == 0, mine, l_ref[0].astype(F32))
    for dev in range(1, N_DEV):
        g = g + jnp.where(me_ref[0] == dev, mine, l_ref[dev].astype(F32))
    return g


def _adamw(g, w, m, v):
    m_new = ADAM_B1 * m + (1.0 - ADAM_B1) * g
    v_new = ADAM_B2 * v + (1.0 - ADAM_B2) * (g * g)
    m_hat = m_new * (1.0 / (1.0 - ADAM_B1 ** ADAM_STEP))
    v_hat = v_new * (1.0 / (1.0 - ADAM_B2 ** ADAM_STEP))
    return -ADAM_LR * (m_hat / (jnp.sqrt(v_hat) + ADAM_EPS) + ADAM_WD * w), m_new, v_new


def _sum_chunks(me, landed, own, name):
    _, _, r, n = landed.shape

    def body(me_ref, l_ref, own_ref, g_out):
        g_out[...] = _sum_in_device_order(me_ref, l_ref, own_ref)[0]

    return pl.pallas_call(
        body, name="sum_" + name,
        grid_spec=pltpu.PrefetchScalarGridSpec(
            num_scalar_prefetch=1, grid=(1,),
            in_specs=[pl.BlockSpec((N_DEV, 1, r, n), lambda i, me_ref: (0, 0, 0, 0)),
                      pl.BlockSpec((1, 1, r, n), lambda i, me_ref: (me_ref[0], 0, 0, 0))],
            out_specs=pl.BlockSpec((r, n), lambda i, me_ref: (0, 0))),
        out_shape=_sds((r, n)), compiler_params=_params())(me, landed, own)


def _adamw_small(gs, ws, ms, vs):
    n = len(gs)

    def body(*refs):
        ins, outs = refs[:4 * n], refs[4 * n:]
        for i in range(n):
            outs[i][...], outs[n + i][...], outs[2 * n + i][...] = _adamw(*[ins[k * n + i][...] for k in range(4)])

    out = pl.pallas_call(body, name="adamw_small", out_shape=[_sds(w.shape) for w in ws] * 3)(*gs, *ws, *ms, *vs)
    return out[:n], out[n:2 * n], out[2 * n:]


def _sum_and_adamw(me, landed, own, wts, m, v, name, layer=None, into=None):
    layers, r, n = wts.shape
    first = 0 if layer is None else layer
    count = layers if layer is None else 1
    tr = _row_tile(r)
    blk = pl.BlockSpec((1, tr, n), lambda li, ri, me_ref: (first + li, ri, 0))
    held = [] if into is None else list(into)

    def body(me_ref, l_ref, own_ref, w_ref, m_ref, v_ref, *rest):
        g_out, d_out, m_out, v_out = rest[len(held):]
        g = _sum_in_device_order(me_ref, l_ref, own_ref)
        g_out[...] = g
        d_out[...], m_out[...], v_out[...] = _adamw(g, w_ref[...], m_ref[...], v_ref[...])

    return pl.pallas_call(
        body, name="adamw_" + name,
        grid_spec=pltpu.PrefetchScalarGridSpec(
            num_scalar_prefetch=1, grid=(count, r // tr),
            in_specs=[pl.BlockSpec((N_DEV, 1, tr, n), lambda li, ri, me_ref: (0, li, ri, 0)),
                      pl.BlockSpec((1, 1, tr, n), lambda li, ri, me_ref: (me_ref[0], li, ri, 0)), blk, blk, blk] + [HBM] * len(held),
            out_specs=[blk] * 4),
        out_shape=[_sds((layers, r, n))] * 4, input_output_aliases={6 + i: i for i in range(len(held))},
        compiler_params=_params(2))(me, landed, own, wts, m, v, *held)


EARLY = ['ab_w_in']
LATE_STAGES = {
    'out0': [('ab_w_out', None, 'ab_w_out')],
    'ffn0': [('ffn_w_gate', 0, 'Wg'), ('ffn_w_up', 0, 'Wu'), ('ffn_w_down', 0, 'Wd')],
    'mix1': [('c_w_in', None, 'c_w_in'), ('c_w_out', None, 'c_w_out')],
    'ffn1': [('ffn_w_gate', 1, 'Wg'), ('ffn_w_up', 1, 'Wu'), ('ffn_w_down', 1, 'Wd')],
}
TRANSPOSED = ('ffn_w_gate', 'ffn_w_up')


def _stored(name, a):
    return jnp.swapaxes(a, 1, 2) if name in TRANSPOSED else a


def _stored_axis(name):
    return 1 if name in TRANSPOSED else SHARD_AXIS[name]


GRAD_STAGES = {
    'late1': ([('c_w_in', None), ('c_w_out', None), ('ffn_w_gate', 1), ('ffn_w_up', 1), ('ffn_w_down', 1)],
              ['c_norm', 'c_ln_g', 'c_ln_b', 'c_w_s', 'c_b_s', 'final_norm']),
    'late0': ([('ffn_w_gate', 0), ('ffn_w_up', 0), ('ffn_w_down', 0)], ['ffn_norm', 'ffn_conv_w', 'ffn_conv_b']),
    'mid': ([('ab_w_out', None)], ['ab_conv_w', 'ab_conv_b', 'ab_w_rg_a', 'ab_b_rg_a', 'ab_w_rg_x', 'ab_b_rg_x', 'ab_lambda']),
    'last': ([('ab_w_in', None)], ['ab_norm', 'ab_q_norm', 'ab_w_q_b', 'ab_kv_norm', 'ab_w_kv_b']),
}


def _gather_early(local):
    small = [_bf(local[n]) if n in MATRICES else lax.bitcast_convert_type(local[n], BF16) for n in SMALL_SHARDED]
    gathered, zero = _all_gather([_bf(local[n]) for n in EARLY] + [_pack_slabs(small, ())])
    full = {n: local[n] for n in REPLICATED}
    for n, g in zip(EARLY, gathered):
        full[n] = _from_chunks(g, SHARD_AXIS[n])
    for n, p in zip(SMALL_SHARDED, _unpack_slabs(gathered[-1], [s.shape for s in small])):
        full[n] = _from_chunks(p if n in MATRICES else lax.bitcast_convert_type(p, F32), SHARD_AXIS[n])
    return full, zero


def kernel(x, positions, ab_norm, ab_w_in, ab_q_norm, ab_w_q_b, ab_kv_norm, ab_w_kv_b, ab_conv_w, ab_conv_b, ab_w_rg_a, ab_b_rg_a, ab_w_rg_x, ab_b_rg_x, ab_lambda, ab_w_out, c_norm, c_w_in, c_ln_g, c_ln_b, c_w_s, c_b_s, c_w_out, ffn_norm, ffn_w_gate, ffn_w_up, ffn_conv_w, ffn_conv_b, ffn_w_down, final_norm, loss_target, m_ab_norm, m_ab_w_in, m_ab_q_norm, m_ab_w_q_b, m_ab_kv_norm, m_ab_w_kv_b, m_ab_conv_w, m_ab_conv_b, m_ab_w_rg_a, m_ab_b_rg_a, m_ab_w_rg_x, m_ab_b_rg_x, m_ab_lambda, m_ab_w_out, m_c_norm, m_c_w_in, m_c_ln_g, m_c_ln_b, m_c_w_s, m_c_b_s, m_c_w_out, m_ffn_norm, m_ffn_w_gate, m_ffn_w_up, m_ffn_conv_w, m_ffn_conv_b, m_ffn_w_down, m_final_norm, v_ab_norm, v_ab_w_in, v_ab_q_norm, v_ab_w_q_b, v_ab_kv_norm, v_ab_w_kv_b, v_ab_conv_w, v_ab_conv_b, v_ab_w_rg_a, v_ab_b_rg_a, v_ab_w_rg_x, v_ab_b_rg_x, v_ab_lambda, v_ab_w_out, v_c_norm, v_c_w_in, v_c_ln_g, v_c_ln_b, v_c_w_s, v_c_b_s, v_c_w_out, v_ffn_norm, v_ffn_w_gate, v_ffn_w_up, v_ffn_conv_w, v_ffn_conv_b, v_ffn_w_down, v_final_norm):
    given = dict(locals())
    local = {n: given[n] for n in WEIGHTS}
    b, seq, d = x.shape
    t = b * seq

    me = (4 * lax.axis_index("x") + 2 * lax.axis_index("y") + lax.axis_index("c")).astype(jnp.int32)
    me1 = me.reshape(1)

    full, zero = _gather_early(local)
    gathers = {}
    for stage, members in LATE_STAGES.items():
        srcs = [_bf(_stored(n, local[n] if layer is None else local[n][layer:layer + 1]) + zero) for n, layer, _ in members]
        gathers[stage] = _start_exchange('gather_' + stage, srcs, scatter=False)
        zero = gathers[stage][4]
    w = _prepare(full)
    w['ab_norm'] = w['ab_norm'] + zero

    def late_weights(stage, after):
        _, lands = _wait_exchange('gather_' + stage, gathers[stage], after, scatter=False)
        whole = [l.reshape(1, -1, l.shape[-1]) if _stored_axis(n) == 1 else _merge_columns(l, n)
                 for (n, _, _), l in zip(LATE_STAGES[stage], lands)]
        if stage == 'out0':
            return _prepare_out(whole[0])
        return {key: a[0] for (_, _, key), a in zip(LATE_STAGES[stage], whole)}

    scatters = {}

    def start_scatter(stage, g):
        whole = _unprepare(g)
        big, small = GRAD_STAGES[stage]
        slab = [_to_chunks(whole[n], SHARD_AXIS[n]) if n in SHARD_AXIS else jnp.broadcast_to(whole[n][None], (N_DEV,) + whole[n].shape)
                for n in small]
        own = [whole[n].reshape(N_DEV, 1, whole[n].shape[1] // N_DEV, whole[n].shape[2])
               if whole[n].dtype == BF16 and _stored_axis(n) == 1 else
               _split_chunks(whole[n], _stored_axis(n), n + ('' if layer is None else str(layer))) for n, layer in big]
        own.append(_bf(_pack_slabs(slab, (N_DEV,)))[:, None])
        scatters[stage] = _start_exchange('scatter_' + stage, own, scatter=True)
        return scatters[stage][4]

    posb = jnp.broadcast_to(positions.astype(F32).reshape(t, 1), (t, LANES))
    loss, dx, grads = _local_step(x.reshape(t, d), posb, loss_target.reshape(t, d), w, seq, late_weights, start_scatter)
    start_scatter('last', grads)
    after = scatters['last'][5]

    updated, small_grads = {}, {}
    for stage, (big, small) in GRAD_STAGES.items():
        owns, landed = _wait_exchange('scatter_' + stage, scatters[stage], after, scatter=True)
        for (n, layer), own, land in zip(big, owns, landed):
            updated[n] = _sum_and_adamw(me1, land, own, _stored(n, given[n]), _stored(n, given['m_' + n]), _stored(n, given['v_' + n]),
                                        n + ('' if layer is None else str(layer)), layer, updated.get(n))
        summed = _sum_chunks(me1, landed[-1], owns[-1], stage)
        small_grads.update(zip(small, _unpack_slabs(summed, [local[n].shape for n in small])))
        after = sum([updated[n][1][:1, :1, :1] for n, _ in big], summed[:1, :1].reshape(1, 1, 1))
    names = list(small_grads)
    news = _adamw_small([small_grads[n] for n in names], *[[given[p + n] for n in names] for p in ('', 'm_', 'v_')])
    for i, n in enumerate(names):
        updated[n] = [small_grads[n], news[0][i], news[1][i], news[2][i]]
    total = lax.psum(loss[0, 0], ("x", "y", "c"))
    return (total, dx.reshape(b, seq, d), *[_stored(n, updated[n][kind]) for kind in range(4) for n in WEIGHTS])
```

```python
import math

import jax
import jax.numpy as jnp
from jax import lax
from jax.experimental import pallas as pl
from jax.experimental.pallas import tpu as pltpu

F32 = jnp.float32
BF16 = jnp.bfloat16
MESH = pl.DeviceIdType.MESH

N_DEV = 8
LANES = 128
HALO = 8
VMEM_LIMIT = 56 << 20

NORM_EPS = 1e-6
HEADS = 8
HEAD_PAD = 128
QK_NOPE = 64
QK_ROPE = 32
ROPE_HALF = 16
ROPE_BASE = 10000.0
ATTN_SCALE = (QK_NOPE + QK_ROPE) ** -0.5
LRU_C = 8.0
LRU_W = 512
CHUNK = 128
SGU_GROUPS = 8
D_FF = 2816
FF_BLOCKS = 2

ADAM_LR, ADAM_B1, ADAM_B2, ADAM_EPS, ADAM_WD, ADAM_STEP = 0.001, 0.9, 0.999, 1e-08, 0.01, 10

WEIGHTS = ['ab_norm', 'ab_w_in', 'ab_q_norm', 'ab_w_q_b', 'ab_kv_norm', 'ab_w_kv_b', 'ab_conv_w', 'ab_conv_b',
           'ab_w_rg_a', 'ab_b_rg_a', 'ab_w_rg_x', 'ab_b_rg_x', 'ab_lambda', 'ab_w_out', 'c_norm', 'c_w_in', 'c_ln_g',
           'c_ln_b', 'c_w_s', 'c_b_s', 'c_w_out', 'ffn_norm', 'ffn_w_gate', 'ffn_w_up', 'ffn_conv_w', 'ffn_conv_b',
           'ffn_w_down', 'final_norm']
SHARD_AXIS = {'ab_w_in': 2, 'ab_w_q_b': 2, 'ab_w_kv_b': 2, 'ab_conv_w': 2, 'ab_w_out': 1, 'c_norm': 1, 'c_w_in': 2,
              'c_ln_g': 1, 'c_ln_b': 1, 'c_w_out': 1, 'ffn_w_gate': 2, 'ffn_w_up': 2, 'ffn_conv_w': 2, 'ffn_w_down': 1}
MATRICES = ['ab_w_in', 'ab_w_q_b', 'ab_w_kv_b', 'ab_w_out', 'c_w_in', 'c_w_out', 'ffn_w_gate', 'ffn_w_up', 'ffn_w_down']
BIG = ['ab_w_in', 'c_w_in', 'ffn_w_gate', 'ffn_w_up', 'ab_w_out', 'c_w_out', 'ffn_w_down']
REPLICATED = [n for n in WEIGHTS if n not in SHARD_AXIS]
SMALL_SHARDED = [n for n in WEIGHTS if n in SHARD_AXIS and n not in BIG]


def _bf(x):
    return x.astype(BF16)


def _nn(a, b):
    return lax.dot_general(_bf(a), _bf(b), (((1,), (0,)), ((), ())), preferred_element_type=F32)


def _nt(a, b):
    return lax.dot_general(_bf(a), _bf(b), (((1,), (1,)), ((), ())), preferred_element_type=F32)


def _tn(a, b):
    return lax.dot_general(_bf(a), _bf(b), (((0,), (0,)), ((), ())), preferred_element_type=F32)


def _rms(x, g):
    return x * lax.rsqrt(jnp.mean(x * x, axis=-1, keepdims=True) + NORM_EPS) * g


def _layer_norm(x, g, b):
    xc = x - jnp.mean(x, axis=-1, keepdims=True)
    return xc * lax.rsqrt(jnp.mean(xc * xc, axis=-1, keepdims=True) + NORM_EPS) * g + b


def _gelu(x):
    return jax.nn.gelu(x)


STRIP = 16
STRIP_LANES = 384
GELU_C = math.sqrt(2.0 / math.pi)
GELU_A = 0.044715


def _gelu_and_grad(x):
    x2 = x * x
    t = jnp.tanh(x * (GELU_C + (GELU_C * GELU_A) * x2))
    half_x = 0.5 * x
    one_plus_t = 1.0 + t
    return half_x * one_plus_t, 0.5 * one_plus_t + half_x * (1.0 - t * t) * (GELU_C + (3.0 * GELU_C * GELU_A) * x2)


def _colsum(x):
    return jnp.sum(x, axis=0, keepdims=True)


def _softplus(x):
    return jnp.maximum(x, 0.0) + jnp.log1p(jnp.exp(-jnp.abs(x)))


@jax.custom_vjp
def _decay(x):
    a = jnp.exp(x)
    y = 2.0 * x
    series = -y * (1.0 + y * (1 / 2 + y * (1 / 6 + y * (1 / 24 + y * (1 / 120 + y * (1 / 720))))))
    return a, jnp.where(y < -0.3, 1.0 - a * a, series)


def _decay_fwd(x):
    a, gap = _decay(x)
    return (a, gap), a


def _decay_bwd(a, cts):
    return (a * (cts[0] - 2.0 * a * cts[1]),)


_decay.defvjp(_decay_fwd, _decay_bwd)


def _accumulate(ref, val, first):
    @pl.when(first)
    def _():
        ref[...] = val

    @pl.when(jnp.logical_not(first))
    def _():
        ref[...] += val


def _params(n_axes=1):
    return pltpu.CompilerParams(dimension_semantics=("arbitrary",) * n_axes, vmem_limit_bytes=VMEM_LIMIT)


def _row(tm, n):
    return pl.BlockSpec((tm, n), lambda i: (i, 0))


def _const(shape):
    nd = len(shape)
    return pl.BlockSpec(shape, lambda i: (0,) * nd, pipeline_mode=pl.Buffered(1))


def _prev_halo(tm, n):
    return pl.BlockSpec((HALO, n), lambda i: (jnp.maximum(i * (tm // HALO) - 1, 0), 0))


def _next_halo(tm, n, n_tiles):
    last = n_tiles * (tm // HALO) - 1
    return pl.BlockSpec((HALO, n), lambda i: (jnp.minimum((i + 1) * (tm // HALO), last), 0))


def _sds(shape, dtype=F32):
    return jax.ShapeDtypeStruct(shape, dtype)


def _rope_tables(posb):
    lane = lax.broadcasted_iota(jnp.int32, posb.shape, 1)
    in_rope = jnp.logical_and(lane >= QK_NOPE, lane < QK_NOPE + QK_ROPE)
    j = (lane & (ROPE_HALF - 1)).astype(F32)
    inv_freq = jnp.exp((-math.log(ROPE_BASE)) * j / ROPE_HALF)
    ang = posb * inv_freq
    return jnp.where(in_rope, jnp.cos(ang), 1.0), jnp.where(in_rope, jnp.sin(ang), 0.0)


def _rot(q):
    n = q.shape[1]
    lane = lax.broadcasted_iota(jnp.int32, q.shape, 1) & (HEAD_PAD - 1)
    first_half = jnp.where(lane >= QK_NOPE, -pltpu.roll(q, n - ROPE_HALF, 1), 0.0)
    second_half = jnp.where(lane < QK_NOPE + QK_ROPE, pltpu.roll(q, ROPE_HALF, 1), 0.0)
    return jnp.where(lane < QK_NOPE + ROPE_HALF, first_half, second_half)


def _rope(q, cos_t, sin_t):
    return q * cos_t + _rot(q) * sin_t


def _rope_transpose(dq, cos_t, sin_t):
    return dq * cos_t - _rot(dq * sin_t)


def _tile_heads(t):
    return jnp.concatenate([t] * HEADS, axis=1)


Q_LORA, KV_LORA = 256, 128
Z_KPE = Q_LORA + KV_LORA
Z_LRU = Z_KPE + HEAD_PAD
Z_GATE = Z_LRU + LRU_W
Z_WIDTH = Z_GATE + LRU_W


def _ab_in_fwd(x, posb, w, tm):
    t, d = x.shape

    def body(x_ref, pos_ref, gn_ref, win_ref, qn_ref, wq_ref, kvn_ref, wk_ref, wv_ref, q_out, k_out, v_out, xl_out, gate_out):
        hn = _rms(x_ref[...], gn_ref[...])
        z = _nn(hn, win_ref[...])
        cqn = _rms(z[:, :Q_LORA], qn_ref[...])
        kvn = _rms(z[:, Q_LORA:Z_KPE], kvn_ref[...])
        cos_t, sin_t = _rope_tables(pos_ref[...])
        q_out[...] = _bf(_rope(_nn(cqn, wq_ref[...]), _tile_heads(cos_t), _tile_heads(sin_t)))
        kpe = _rope(z[:, Z_KPE:Z_LRU], cos_t, sin_t)
        k_out[...] = _bf(_nn(kvn, wk_ref[...]) + _tile_heads(kpe))
        v_out[...] = _bf(_nn(kvn, wv_ref[...]))
        xl_out[...] = z[:, Z_LRU:Z_GATE]
        gate_out[...] = z[:, Z_GATE:]

    hp = HEADS * HEAD_PAD
    return pl.pallas_call(
        body, name="ab_in_fwd", grid=(t // tm,),
        in_specs=[_row(tm, d), _row(tm, LANES), _const((1, d)), _const((d, Z_WIDTH)), _const((1, Q_LORA)), _const((Q_LORA, hp)),
                  _const((1, KV_LORA)), _const((KV_LORA, hp)), _const((KV_LORA, hp))],
        out_specs=[_row(tm, hp), _row(tm, hp), _row(tm, hp), _row(tm, LRU_W), _row(tm, LRU_W)],
        out_shape=[_sds((t, hp), BF16), _sds((t, hp), BF16), _sds((t, hp), BF16), _sds((t, LRU_W)), _sds((t, LRU_W))],
        compiler_params=_params(),
    )(x, posb, w['ab_norm'], w['W_in'], w['ab_q_norm'], w['Wq'], w['ab_kv_norm'], w['Wk'], w['Wv'])


def _ab_in_bwd(x, posb, w, dq, dk, dv, dxl, dgate, dres, tm):
    t, d = x.shape
    hp = HEADS * HEAD_PAD

    def body(x_ref, pos_ref, gn_ref, win_ref, qn_ref, wq_ref, kvn_ref, wk_ref, wv_ref, dq_ref, dk_ref, dv_ref, dxl_ref, dgate_ref,
             dres_ref, dx_out, dgn_out, dwin_out, dqn_out, dwq_out, dkvn_out, dwk_out, dwv_out):
        first = pl.program_id(0) == 0
        hn, vjp_in = jax.vjp(_rms, x_ref[...], gn_ref[...])
        z = _nn(hn, win_ref[...])
        cqn, vjp_q = jax.vjp(_rms, z[:, :Q_LORA], qn_ref[...])
        kvn, vjp_kv = jax.vjp(_rms, z[:, Q_LORA:Z_KPE], kvn_ref[...])
        cos_t, sin_t = _rope_tables(pos_ref[...])
        dq0 = _rope_transpose(dq_ref[...], _tile_heads(cos_t), _tile_heads(sin_t))
        dk0 = dk_ref[...]
        dv0 = dv_ref[...]
        dkpe = dk0[:, :HEAD_PAD]
        for h in range(1, HEADS):
            dkpe = dkpe + dk0[:, h * HEAD_PAD:(h + 1) * HEAD_PAD]
        dkpe = _rope_transpose(dkpe, cos_t, sin_t)
        _accumulate(dwq_out, _tn(cqn, dq0), first)
        _accumulate(dwk_out, _tn(kvn, dk0), first)
        _accumulate(dwv_out, _tn(kvn, dv0), first)
        dcq, dqn = vjp_q(_nt(dq0, wq_ref[...]))
        dckv, dkvn = vjp_kv(_nt(dk0, wk_ref[...]) + _nt(dv0, wv_ref[...]))
        _accumulate(dqn_out, dqn, first)
        _accumulate(dkvn_out, dkvn, first)
        dz = _bf(jnp.concatenate([_bf(dcq), _bf(dckv), _bf(dkpe), dxl_ref[...], dgate_ref[...]], axis=1))
        _accumulate(dwin_out, _tn(hn, dz), first)
        dx, dgn = vjp_in(_nt(dz, win_ref[...]))
        _accumulate(dgn_out, dgn, first)
        dx_out[...] = dx + dres_ref[...]

    return pl.pallas_call(
        body, name="ab_in_bwd", grid=(t // tm,),
        in_specs=[_row(tm, d), _row(tm, LANES), _const((1, d)), _const((d, Z_WIDTH)), _const((1, Q_LORA)), _const((Q_LORA, hp)),
                  _const((1, KV_LORA)), _const((KV_LORA, hp)), _const((KV_LORA, hp)),
                  _row(tm, hp), _row(tm, hp), _row(tm, hp), _row(tm, LRU_W), _row(tm, LRU_W), _row(tm, d)],
        out_specs=[_row(tm, d), _const((1, d)), _const((d, Z_WIDTH)), _const((1, Q_LORA)), _const((Q_LORA, hp)),
                   _const((1, KV_LORA)), _const((KV_LORA, hp)), _const((KV_LORA, hp))],
        out_shape=[_sds((t, d)), _sds((1, d)), _sds((d, Z_WIDTH)), _sds((1, Q_LORA)), _sds((Q_LORA, hp)),
                   _sds((1, KV_LORA)), _sds((KV_LORA, hp)), _sds((KV_LORA, hp))],
        compiler_params=_params(),
    )(x, posb, w['ab_norm'], w['W_in'], w['ab_q_norm'], w['Wq'], w['ab_kv_norm'], w['Wk'], w['Wv'], dq, dk, dv, dxl, dgate, dres)


def _attn_probs(q_blk, k_ext, tq):
    ext = k_ext.shape[0]
    s = lax.dot_general(q_blk, k_ext, (((1,), (1,)), ((), ())), preferred_element_type=F32) * ATTN_SCALE
    causal = lax.broadcasted_iota(jnp.int32, (tq, tq), 1) <= lax.broadcasted_iota(jnp.int32, (tq, tq), 0)
    diag = jnp.where(causal, s[:, ext - tq:], -1e30)
    s = diag if ext == tq else jnp.concatenate([s[:, :ext - tq], diag], axis=1)
    p = jnp.exp(s - jnp.max(s, axis=1, keepdims=True))
    return p * (1.0 / jnp.sum(p, axis=1, keepdims=True))


def _attn_fwd(q, k, v, tq):
    b, s, hp = q.shape
    blk = pl.BlockSpec((1, s, HEAD_PAD), lambda bi, h: (bi, 0, h))

    def body(q_ref, k_ref, v_ref, o_ref, p_ref):
        kb = _bf(k_ref[0])
        vb = _bf(v_ref[0])
        for i in range(s // tq):
            ext = (i + 1) * tq
            p = _bf(_attn_probs(_bf(q_ref[0, i * tq:ext, :]), kb[:ext], tq))
            p_ref[0, 0, i * tq:ext, :ext] = p
            o_ref[0, i * tq:ext, :] = _bf(lax.dot_general(p, vb[:ext], (((1,), (0,)), ((), ())), preferred_element_type=F32))

    return pl.pallas_call(body, name="attn_fwd", grid=(b, HEADS), in_specs=[blk, blk, blk],
                          out_specs=[blk, pl.BlockSpec((1, 1, s, s), lambda bi, h: (bi, h, 0, 0))],
                          out_shape=[_sds((b, s, hp), BF16), _sds((b, HEADS, s, s), BF16)], compiler_params=_params(2))(q, k, v)


def _attn_bwd(q, k, v, probs, do, tq):
    b, s, hp = q.shape
    blk = pl.BlockSpec((1, s, HEAD_PAD), lambda bi, h: (bi, 0, h))

    def body(q_ref, k_ref, v_ref, p_ref, do_ref, dq_ref, dk_ref, dv_ref):
        kb = _bf(k_ref[0])
        vb = _bf(v_ref[0])
        dk_ref[...] = jnp.zeros_like(dk_ref)
        dv_ref[...] = jnp.zeros_like(dv_ref)
        for i in range(s // tq):
            ext = (i + 1) * tq
            qb = _bf(q_ref[0, i * tq:ext, :])
            dob = _bf(do_ref[0, i * tq:ext, :])
            pb = p_ref[0, 0, i * tq:ext, :ext]
            p = pb.astype(F32)
            dv_ref[0, :ext, :] += lax.dot_general(pb, dob, (((0,), (0,)), ((), ())), preferred_element_type=F32)
            dp = lax.dot_general(dob, vb[:ext], (((1,), (1,)), ((), ())), preferred_element_type=F32)
            ds = _bf(p * (dp - jnp.sum(p * dp, axis=1, keepdims=True)) * ATTN_SCALE)
            dq_ref[0, i * tq:ext, :] = lax.dot_general(ds, kb[:ext], (((1,), (0,)), ((), ())), preferred_element_type=F32)
            dk_ref[0, :ext, :] += lax.dot_general(ds, qb, (((0,), (0,)), ((), ())), preferred_element_type=F32)

    return pl.pallas_call(body, name="attn_bwd", grid=(b, HEADS),
                          in_specs=[blk, blk, blk, pl.BlockSpec((1, 1, s, s), lambda bi, h: (bi, h, 0, 0)), blk], out_specs=[blk, blk, blk],
                          out_shape=[_sds((b, s, hp))] * 3, compiler_params=_params(2))(q, k, v, probs, do)


LRU_CONV = 4


def _lru_point(pre_a, pre_x, xc, lam):
    r = jax.nn.sigmoid(pre_a)
    i = jax.nn.sigmoid(pre_x)
    a, gap = _decay(-LRU_C * r * _softplus(-lam))
    return a, jnp.sqrt(gap) * (i * xc)


def _causal_conv(pad_ref, x, halo, first_in_seq, w, taps):
    tm = x.shape[0]
    pad_ref[:HALO, :] = jnp.where(first_in_seq, 0.0, halo)
    pad_ref[HALO:, :] = x
    y = w[taps - 1:taps, :] * x
    for k in range(taps - 1):
        off = HALO - (taps - 1) + k
        y = y + w[k:k + 1, :] * pad_ref[off:off + tm, :]
    return y


def _conv_taps(pad_ref, r, cols, taps):
    blocks = [pad_ref[r + j * HALO:r + (j + 1) * HALO, cols] for j in range(1 + STRIP // HALO)]
    sub = lax.broadcasted_iota(jnp.int32, blocks[0].shape, 0)
    out = []
    for k in range(taps - 1):
        s = taps - 1 - k
        rolled = [pltpu.roll(b, s, 0) for b in blocks]
        out.append(jnp.concatenate([jnp.where(sub < s, rolled[j], rolled[j + 1]) for j in range(STRIP // HALO)], axis=0))
    out.append(jnp.concatenate(blocks[1:], axis=0))
    return out


def _causal_conv_wgrad(pad_ref, dy, taps):
    tm = dy.shape[0]
    return jnp.concatenate([_colsum(dy * pad_ref[HALO - (taps - 1) + k:HALO - (taps - 1) + k + tm, :]) for k in range(taps)], axis=0)


def _causal_conv_transpose(pad_ref, dy, halo_next, last_in_seq, w, taps):
    tm = dy.shape[0]
    pad_ref[:tm, :] = dy
    pad_ref[tm:, :] = jnp.where(last_in_seq, 0.0, halo_next)
    dx = w[taps - 1:taps, :] * dy
    for k in range(taps - 1):
        off = (taps - 1) - k
        dx = dx + w[k:k + 1, :] * pad_ref[off:off + tm, :]
    return dx


def _lru_fwd(xl, gate, w, ts, seq):
    t, n = xl.shape
    tiles_per_seq = seq // ts

    def body(xl_ref, halo_ref, gate_ref, cw_ref, cb_ref, wa_ref, ba_ref, wx_ref, bx_ref, lam_ref, y_out, h_out, pad_ref, a_ref, b_ref, carry_ref):
        first_in_seq = pl.program_id(0) % tiles_per_seq == 0
        xc = _causal_conv(pad_ref, xl_ref[...], halo_ref[...], first_in_seq, cw_ref[...], LRU_CONV) + cb_ref[...]
        a, bx = _lru_point(_nn(xc, wa_ref[...]) + ba_ref[...], _nn(xc, wx_ref[...]) + bx_ref[...], xc, lam_ref[...])
        a_ref[...] = a
        b_ref[...] = bx

        @pl.when(first_in_seq)
        def _():
            carry_ref[...] = jnp.zeros_like(carry_ref)

        def step(r, h):
            h = a_ref[pl.ds(r, 1), :] * h + b_ref[pl.ds(r, 1), :]
            h_out[pl.ds(r, 1), :] = h
            return h

        carry_ref[...] = lax.fori_loop(0, ts, step, carry_ref[...], unroll=8)
        y_out[...] = _bf(h_out[...] * _gelu(gate_ref[...]))

    return pl.pallas_call(
        body, name="lru_fwd", grid=(t // ts,),
        in_specs=[_row(ts, n), _prev_halo(ts, n), _row(ts, n), _const((LRU_CONV, n)), _const((1, n)), _const((n, n)), _const((1, n)),
                  _const((n, n)), _const((1, n)), _const((1, n))],
        out_specs=[_row(ts, n), _row(ts, n)], out_shape=[_sds((t, n), BF16), _sds((t, n))],
        scratch_shapes=[pltpu.VMEM((HALO + ts, n), F32), pltpu.VMEM((ts, n), F32), pltpu.VMEM((ts, n), F32), pltpu.VMEM((1, n), F32)],
        compiler_params=_params(),
    )(xl, xl, gate, w['ab_conv_w'], w['ab_conv_b'], w['Wa'], w['ab_b_rg_a'], w['Wx'], w['ab_b_rg_x'], w['ab_lambda'])


def _lru_bwd(xl, gate, hs, dy, w, ts, seq):
    t, n = xl.shape
    tiles_per_seq = seq // ts
    n_tiles = t // ts

    def rev(i):
        return n_tiles - 1 - i

    row = pl.BlockSpec((ts, n), lambda i: (rev(i), 0))
    prev = pl.BlockSpec((HALO, n), lambda i: (jnp.maximum(rev(i) * (ts // HALO) - 1, 0), 0))
    acc = lambda shape: pl.BlockSpec(shape, lambda i: (0,) * len(shape))

    def body(xl_ref, xhalo_ref, gate_ref, h_ref, hhalo_ref, dy_ref, cw_ref, cb_ref, wa_ref, ba_ref, wx_ref, bx_ref, lam_ref,
             dxl_out, dgate_out, dcw_out, dcb_out, dwa_out, dba_out, dwx_out, dbx_out, dlam_out,
             pad_ref, padh_ref, padd_ref, a_ref, g_ref, carry_ref, dhalo_ref):
        step_id = pl.program_id(0)
        first = step_id == 0
        tile = rev(step_id)
        first_in_seq = tile % tiles_per_seq == 0
        last_in_seq = tile % tiles_per_seq == tiles_per_seq - 1
        cw = cw_ref[...]
        xc = _causal_conv(pad_ref, xl_ref[...], xhalo_ref[...], first_in_seq, cw, LRU_CONV) + cb_ref[...]
        pre_a = _nn(xc, wa_ref[...]) + ba_ref[...]
        pre_x = _nn(xc, wx_ref[...]) + bx_ref[...]
        (a, _), vjp_point = jax.vjp(_lru_point, pre_a, pre_x, xc, lam_ref[...])
        h = h_ref[...]
        _, vjp_out = jax.vjp(lambda h_, g_: h_ * _gelu(g_), h, gate_ref[...])
        dh, dgate = vjp_out(dy_ref[...])
        dgate_out[...] = _bf(dgate)
        a_ref[...] = a
        g_ref[...] = dh

        @pl.when(last_in_seq)
        def _():
            carry_ref[...] = jnp.zeros_like(carry_ref)

        def step(j, c):
            r = ts - 1 - j
            g = g_ref[pl.ds(r, 1), :] + c
            g_ref[pl.ds(r, 1), :] = g
            return a_ref[pl.ds(r, 1), :] * g

        carry_ref[...] = lax.fori_loop(0, ts, step, carry_ref[...], unroll=8)
        g = g_ref[...]
        padh_ref[:HALO, :] = jnp.where(first_in_seq, 0.0, hhalo_ref[...])
        padh_ref[HALO:, :] = h
        dpre_a, dpre_x, dxc, dlam = vjp_point((g * padh_ref[HALO - 1:HALO - 1 + ts, :], g))
        dxc = dxc + _nt(dpre_a, wa_ref[...]) + _nt(dpre_x, wx_ref[...])
        _accumulate(dwa_out, _tn(xc, dpre_a), first)
        _accumulate(dwx_out, _tn(xc, dpre_x), first)
        _accumulate(dba_out, _colsum(dpre_a), first)
        _accumulate(dbx_out, _colsum(dpre_x), first)
        _accumulate(dlam_out, dlam, first)
        _accumulate(dcb_out, _colsum(dxc), first)
        _accumulate(dcw_out, _causal_conv_wgrad(pad_ref, dxc, LRU_CONV), first)
        dxl_out[...] = _bf(_causal_conv_transpose(padd_ref, dxc, dhalo_ref[...], last_in_seq, cw, LRU_CONV))
        dhalo_ref[...] = dxc[:HALO, :]

    return pl.pallas_call(
        body, name="lru_bwd", grid=(n_tiles,),
        in_specs=[row, prev, row, row, prev, row, _const((LRU_CONV, n)), _const((1, n)), _const((n, n)), _const((1, n)),
                  _const((n, n)), _const((1, n)), _const((1, n))],
        out_specs=[row, row, acc((LRU_CONV, n)), acc((1, n)), acc((n, n)), acc((1, n)), acc((n, n)), acc((1, n)), acc((1, n))],
        out_shape=[_sds((t, n), BF16), _sds((t, n), BF16), _sds((LRU_CONV, n)), _sds((1, n)), _sds((n, n)), _sds((1, n)), _sds((n, n)),
                   _sds((1, n)), _sds((1, n))],
        scratch_shapes=[pltpu.VMEM((HALO + ts, n), F32), pltpu.VMEM((HALO + ts, n), F32), pltpu.VMEM((ts + HALO, n), F32),
                        pltpu.VMEM((ts, n), F32), pltpu.VMEM((ts, n), F32), pltpu.VMEM((1, n), F32), pltpu.VMEM((HALO, n), F32)],
        compiler_params=_params(),
    )(xl, xl, gate, hs, hs, dy, w['ab_conv_w'], w['ab_conv_b'], w['Wa'], w['ab_b_rg_a'], w['Wx'], w['ab_b_rg_x'], w['ab_lambda'])


def _ab_out_fwd(x, o, y, w, tm):
    t, d = x.shape
    hp = o.shape[1]

    def body(x_ref, o_ref, y_ref, wa_ref, wb_ref, h_out):
        h_out[...] = x_ref[...] + _nn(o_ref[...], wa_ref[...]) + _nn(y_ref[...], wb_ref[...])

    return pl.pallas_call(body, name="ab_out_fwd", grid=(t // tm,),
                          in_specs=[_row(tm, d), _row(tm, hp), _row(tm, LRU_W), _const((hp, d)), _const((LRU_W, d))],
                          out_specs=_row(tm, d), out_shape=_sds((t, d)), compiler_params=_params())(x, o, y, w['Wo_a'], w['Wo_b'])


def _ab_out_bwd(o, y, dh, w, tm):
    t, d = dh.shape
    hp = o.shape[1]

    def body(o_ref, y_ref, dh_ref, wa_ref, wb_ref, do_out, dy_out, dwa_out, dwb_out):
        first = pl.program_id(0) == 0
        dh_t = dh_ref[...]
        do_out[...] = _bf(_nt(dh_t, wa_ref[...]))
        dy_out[...] = _nt(dh_t, wb_ref[...])
        _accumulate(dwa_out, _tn(o_ref[...], dh_t), first)
        _accumulate(dwb_out, _tn(y_ref[...], dh_t), first)

    return pl.pallas_call(body, name="ab_out_bwd", grid=(t // tm,),
                          in_specs=[_row(tm, hp), _row(tm, LRU_W), _row(tm, d), _const((hp, d)), _const((LRU_W, d))],
                          out_specs=[_row(tm, hp), _row(tm, LRU_W), _const((hp, d)), _const((LRU_W, d))],
                          out_shape=[_sds((t, hp), BF16), _sds((t, LRU_W)), _sds((hp, d)), _sds((LRU_W, d))],
                          compiler_params=_params())(o, y, dh, w['Wo_a'], w['Wo_b'])


FFN_CONV = 3


def _ffn_a_fwd(h, norm, wg, wu, tm):
    t, d = h.shape
    fb = D_FF // FF_BLOCKS

    def body(h_ref, gn_ref, wg_ref, wu_ref, g_out, u_out, hn_out):
        hn = _bf(_rms(h_ref[...], gn_ref[...]))
        hn_out[0] = hn
        g_out[...] = _nt(hn, wg_ref[...])
        u_out[...] = _nt(hn, wu_ref[...])

    wspec = pl.BlockSpec((fb, d), lambda f, i: (f, 0))
    ospec = pl.BlockSpec((tm, fb), lambda f, i: (i, f))
    return pl.pallas_call(
        body, name="ffn_a_fwd", grid=(FF_BLOCKS, t // tm),
        in_specs=[pl.BlockSpec((tm, d), lambda f, i: (i, 0)), pl.BlockSpec((1, d), lambda f, i: (0, 0)), wspec, wspec],
        out_specs=[ospec, ospec, pl.BlockSpec((1, tm, d), lambda f, i: (f, i, 0))],
        out_shape=[_sds((t, D_FF)), _sds((t, D_FF)), _sds((FF_BLOCKS, t, d), BF16)], compiler_params=_params(2))(h, norm, wg, wu)


def _ffn_b_fwd(g, u, h, cw, cb, wd, tm, seq, final=None):
    t, d = h.shape
    tiles_per_seq = seq // tm

    def body(g_ref, halo_ref, u_ref, h_ref, cw_ref, cb_ref, wd_ref, *rest):
        pad_ref, act_ref = rest[-2:]
        pad_ref[:HALO, :] = jnp.where(pl.program_id(0) % tiles_per_seq == 0, 0.0, halo_ref[...])
        pad_ref[HALO:, :] = g_ref[...]
        cw = cw_ref[...]
        cb = cb_ref[...]
        for c0 in range(0, D_FF, STRIP_LANES):
            cols = slice(c0, min(c0 + STRIP_LANES, D_FF))
            for r in range(0, tm, STRIP):
                taps = _conv_taps(pad_ref, r, cols, FFN_CONV)
                gc = cb[:, cols] + cw[0:1, cols] * taps[0] + cw[1:2, cols] * taps[1] + cw[2:3, cols] * taps[2]
                act_ref[r:r + STRIP, cols] = _bf(_gelu(gc) * u_ref[r:r + STRIP, cols])
        h_new = h_ref[...] + _nn(act_ref[...], wd_ref[...])
        if final is None:
            rest[0][...] = h_new
        else:
            tgt_ref, fn_ref, dh_out, loss_out, dfn_out = rest[:5]
            first = pl.program_id(0) == 0
            loss, dh_out[...], dfn = _loss_and_grad(h_new, tgt_ref[...], fn_ref[...])
            _accumulate(loss_out, loss, first)
            _accumulate(dfn_out, dfn, first)

    in_specs = [_row(tm, D_FF), _prev_halo(tm, D_FF), _row(tm, D_FF), _row(tm, d), _const((FFN_CONV, D_FF)), _const((1, D_FF)), _const((D_FF, d))]
    scratch = [pltpu.VMEM((HALO + tm, D_FF), F32), pltpu.VMEM((tm, D_FF), BF16)]
    if final is None:
        return pl.pallas_call(body, name="ffn_b_fwd", grid=(t // tm,), in_specs=in_specs, out_specs=_row(tm, d), out_shape=_sds((t, d)),
                              scratch_shapes=scratch, compiler_params=_params())(g, g, u, h, cw, cb, wd)
    return pl.pallas_call(body, name="ffn_b_fwd_loss", grid=(t // tm,), in_specs=in_specs + [_row(tm, d), _const((1, d))],
                          out_specs=[_row(tm, d), _const((1, 1)), _const((1, d))],
                          out_shape=[_sds((t, d)), _sds((1, 1)), _sds((1, d))],
                          scratch_shapes=scratch, compiler_params=_params())(g, g, u, h, cw, cb, wd, *final)


def _ffn_b_bwd(g, u, dout, cw, cb, wd, tm, seq):
    t, d = dout.shape
    fb = D_FF // FF_BLOCKS
    tiles_per_seq = seq // tm

    def body(g_ref, halo_ref, u_ref, dout_ref, cw_ref, cb_ref, wd_ref, dgc_out, du_out, dwd_out, dcw_out, dcb_out,
             pad_ref, dact_ref, act_ref, acc_ref, dwd_acc):
        i = pl.program_id(1)
        first = i == 0
        pad_ref[:HALO, :] = jnp.where(i % tiles_per_seq == 0, 0.0, halo_ref[...])
        pad_ref[HALO:, :] = g_ref[...]
        dout_b = _bf(dout_ref[...])
        dact_ref[...] = _nt(dout_b, wd_ref[...])
        cw = cw_ref[...]
        cb = cb_ref[...]
        fold = lambda a: a[:HALO] + a[HALO:]
        for c0 in range(0, fb, STRIP_LANES):
            cols = slice(c0, min(c0 + STRIP_LANES, fb))
            sums = [jnp.zeros((HALO, cols.stop - c0), F32) for _ in range(1 + FFN_CONV)]
            for r in range(0, tm, STRIP):
                rows = slice(r, r + STRIP)
                taps = _conv_taps(pad_ref, r, cols, FFN_CONV)
                gelu, dgelu = _gelu_and_grad(cb[:, cols] + cw[0:1, cols] * taps[0] + cw[1:2, cols] * taps[1] + cw[2:3, cols] * taps[2])
                u = u_ref[rows, cols]
                dact = dact_ref[rows, cols]
                act_ref[rows, cols] = _bf(gelu * u)
                du_out[rows, cols] = _bf(dact * gelu)
                dgc = dact * u * dgelu
                dgc_out[rows, cols] = dgc
                sums = [sums[0] + fold(dgc)] + [sums[1 + k] + fold(dgc * taps[k]) for k in range(FFN_CONV)]
            for k in range(1 + FFN_CONV):
                acc_ref[k, :, cols] = sums[k]
        _accumulate(dwd_acc, _tn(act_ref[...], dout_b), first)

        @pl.when(i == t // tm - 1)
        def _():
            dwd_out[...] = _bf(dwd_acc[...])

        _accumulate(dcb_out, _colsum(acc_ref[0]), first)
        _accumulate(dcw_out, jnp.concatenate([_colsum(acc_ref[1 + k]) for k in range(FFN_CONV)], axis=0), first)

    blk = pl.BlockSpec((tm, fb), lambda f, i: (i, f))
    halo = pl.BlockSpec((HALO, fb), lambda f, i: (jnp.maximum(i * (tm // HALO) - 1, 0), f))
    wd_blk = pl.BlockSpec((fb, d), lambda f, i: (f, 0), pipeline_mode=pl.Buffered(1))
    return pl.pallas_call(
        body, name="ffn_b_bwd", grid=(FF_BLOCKS, t // tm),
        in_specs=[blk, halo, blk, pl.BlockSpec((tm, d), lambda f, i: (i, 0)), pl.BlockSpec((FFN_CONV, fb), lambda f, i: (0, f)),
                  pl.BlockSpec((1, fb), lambda f, i: (0, f)), wd_blk],
        out_specs=[blk, blk, wd_blk, pl.BlockSpec((FFN_CONV, fb), lambda f, i: (0, f)),
                   pl.BlockSpec((1, fb), lambda f, i: (0, f))],
        out_shape=[_sds((t, D_FF)), _sds((t, D_FF), BF16), _sds((D_FF, d), BF16), _sds((FFN_CONV, D_FF)), _sds((1, D_FF))],
        scratch_shapes=[pltpu.VMEM((HALO + tm, fb), F32), pltpu.VMEM((tm, fb), F32), pltpu.VMEM((tm, fb), BF16),
                        pltpu.VMEM((1 + FFN_CONV, HALO, fb), F32), pltpu.VMEM((fb, d), F32)],
        compiler_params=_params(2))(g, g, u, dout, cw, cb, wd)


def _ffn_a_dgrad(h, norm, dgc, du, dres, cw, wg, wu, tm, seq):
    t, d = h.shape
    tiles_per_seq = seq // tm
    n_tiles = t // tm

    def body(h_ref, gn_ref, dgc_ref, halo_ref, du_ref, dres_ref, cw_ref, wg_ref, wu_ref, dh_out, dg_out, dgn_out, pad_ref):
        i = pl.program_id(0)
        last_in_seq = i % tiles_per_seq == tiles_per_seq - 1
        dg = _bf(_causal_conv_transpose(pad_ref, dgc_ref[...], halo_ref[...], last_in_seq, cw_ref[...], FFN_CONV))
        dg_out[...] = dg
        _, vjp_norm = jax.vjp(_rms, h_ref[...], gn_ref[...])
        dh, dgn = vjp_norm(_nn(dg, wg_ref[...]) + _nn(du_ref[...], wu_ref[...]))
        dh_out[...] = dh + dres_ref[...]
        _accumulate(dgn_out, dgn, i == 0)

    return pl.pallas_call(
        body, name="ffn_a_dgrad", grid=(n_tiles,),
        in_specs=[_row(tm, d), _const((1, d)), _row(tm, D_FF), _next_halo(tm, D_FF, n_tiles), _row(tm, D_FF), _row(tm, d),
                  _const((FFN_CONV, D_FF)), _const((D_FF, d)), _const((D_FF, d))],
        out_specs=[_row(tm, d), _row(tm, D_FF), _const((1, d))], out_shape=[_sds((t, d)), _sds((t, D_FF), BF16), _sds((1, d))],
        scratch_shapes=[pltpu.VMEM((tm + HALO, D_FF), F32)], compiler_params=_params())(h, norm, dgc, dgc, du, dres, cw, wg, wu)


def _ffn_a_wgrad(hn, dg, du, tm):
    _, t, d = hn.shape
    fb = D_FF // FF_BLOCKS

    n_tiles = t // tm

    def body(hn_ref, dg_ref, du_ref, dwg_out, dwu_out, acc_g, acc_u):
        i = pl.program_id(1)
        hn_t = hn_ref[0]
        _accumulate(acc_g, _tn(dg_ref[...], hn_t), i == 0)
        _accumulate(acc_u, _tn(du_ref[...], hn_t), i == 0)

        @pl.when(i == n_tiles - 1)
        def _():
            dwg_out[...] = _bf(acc_g[...])
            dwu_out[...] = _bf(acc_u[...])

    blk = pl.BlockSpec((tm, fb), lambda f, i: (i, f))
    wspec = pl.BlockSpec((fb, d), lambda f, i: (f, 0), pipeline_mode=pl.Buffered(1))
    return pl.pallas_call(body, name="ffn_a_wgrad", grid=(FF_BLOCKS, n_tiles),
                          in_specs=[pl.BlockSpec((1, tm, d), lambda f, i: (0, i, 0)), blk, blk],
                          out_specs=[wspec, wspec], out_shape=[_sds((D_FF, d), BF16), _sds((D_FF, d), BF16)],
                          scratch_shapes=[pltpu.VMEM((fb, d), F32), pltpu.VMEM((fb, d), F32)],
                          compiler_params=_params(2))(hn, dg, du)


def _sgu_mix(vn, ws_ref, bst):
    tril = lax.broadcasted_iota(jnp.int32, (CHUNK, CHUNK), 0) >= lax.broadcasted_iota(jnp.int32, (CHUNK, CHUNK), 1)
    wms = [jnp.where(tril, ws_ref[g], 0.0) for g in range(SGU_GROUPS)]
    chunks = []
    for n in range(vn.shape[0] // CHUNK):
        vc = vn[n * CHUNK:(n + 1) * CHUNK, :]
        chunks.append(jnp.concatenate(
            [_nn(wms[g], vc[:, g * CHUNK:(g + 1) * CHUNK]) + bst[:, g:g + 1] for g in range(SGU_GROUPS)], axis=1))
    return jnp.concatenate(chunks, axis=0)


def _sgu_fwd(h, w, tm):
    t, d = h.shape

    def body(h_ref, cn_ref, win_ref, lg_ref, lb_ref, ws_ref, bst_ref, wout_ref, h_out):
        h_t = h_ref[...]
        z = _gelu(_nn(_rms(h_t, cn_ref[...]), win_ref[...]))
        vn = _layer_norm(z[:, d:], lg_ref[...], lb_ref[...])
        s = _sgu_mix(vn, ws_ref, bst_ref[...])
        h_out[...] = h_t + _nn(z[:, :d] * s, wout_ref[...])

    return pl.pallas_call(
        body, name="sgu_fwd", grid=(t // tm,),
        in_specs=[_row(tm, d), _const((1, d)), _const((d, 2 * d)), _const((1, d)), _const((1, d)), _const((SGU_GROUPS, CHUNK, CHUNK)),
                  _const((CHUNK, LANES)), _const((d, d))],
        out_specs=_row(tm, d), out_shape=_sds((t, d)), compiler_params=_params(),
    )(h, w['c_norm'], w['c_w_in'], w['c_ln_g'], w['c_ln_b'], w['c_w_s'], w['bsT'], w['c_w_out'])


def _sgu_bwd(h, dout, w, tm, sub):
    t, d = h.shape

    def body(h_ref, dout_ref, cn_ref, win_ref, lg_ref, lb_ref, ws_ref, bst_ref, wout_ref,
             dh_out, dcn_out, dwin_out, dlg_out, dlb_out, dws_out, dbst_out, dwout_out, hn_ref, us_ref, dz_ref):
        first = pl.program_id(0) == 0
        tril = lax.broadcasted_iota(jnp.int32, (CHUNK, CHUNK), 0) >= lax.broadcasted_iota(jnp.int32, (CHUNK, CHUNK), 1)
        lane = lax.broadcasted_iota(jnp.int32, (CHUNK, LANES), 1)
        dws = [jnp.zeros((CHUNK, CHUNK), F32) for _ in range(SGU_GROUPS)]
        dbst = jnp.zeros((CHUNK, LANES), F32)
        dlg = dlb = dcn = 0.0
        for r in range(0, tm, sub):
            rows = slice(r, r + sub)
            hn, vjp_norm = jax.vjp(_rms, h_ref[rows, :], cn_ref[...])
            zpre = _nn(hn, win_ref[...])
            u, vjp_u = jax.vjp(_gelu, zpre[:, :d])
            vn, vjp_v = jax.vjp(lambda zp, lg, lb: _layer_norm(_gelu(zp), lg, lb), zpre[:, d:], lg_ref[...], lb_ref[...])
            s = _sgu_mix(vn, ws_ref, bst_ref[...])
            dout_t = dout_ref[rows, :]
            dus = _nt(dout_t, wout_ref[...])
            hn_ref[rows, :] = _bf(hn)
            us_ref[rows, :] = _bf(u * s)
            ds = dus * u
            dvn_chunks = []
            for n in range(sub // CHUNK):
                cols = []
                for g in range(SGU_GROUPS):
                    ds_ng = ds[n * CHUNK:(n + 1) * CHUNK, g * CHUNK:(g + 1) * CHUNK]
                    vc_ng = vn[n * CHUNK:(n + 1) * CHUNK, g * CHUNK:(g + 1) * CHUNK]
                    cols.append(_tn(jnp.where(tril, ws_ref[g], 0.0), ds_ng))
                    dws[g] = dws[g] + _nt(ds_ng, vc_ng)
                    dbst = dbst + jnp.where(lane == g, jnp.sum(ds_ng, axis=1, keepdims=True), 0.0)
                dvn_chunks.append(jnp.concatenate(cols, axis=1))
            dvn = jnp.concatenate(dvn_chunks, axis=0)
            (dzu,) = vjp_u(dus * s)
            dzv, dlg_r, dlb_r = vjp_v(dvn)
            dzpre = jnp.concatenate([dzu, dzv], axis=1)
            dz_ref[rows, :] = _bf(dzpre)
            dh, dcn_r = vjp_norm(_nt(dzpre, win_ref[...]))
            dh_out[rows, :] = dh + dout_t
            dlg, dlb, dcn = dlg + dlg_r, dlb + dlb_r, dcn + dcn_r
        _accumulate(dwout_out, _tn(us_ref[...], dout_ref[...]), first)
        _accumulate(dwin_out, _tn(hn_ref[...], dz_ref[...]), first)
        for g in range(SGU_GROUPS):
            val = jnp.where(tril, dws[g], 0.0)

            @pl.when(first)
            def _():
                dws_out[g] = val

            @pl.when(jnp.logical_not(first))
            def _():
                dws_out[g] += val
        _accumulate(dbst_out, dbst, first)
        _accumulate(dlg_out, dlg, first)
        _accumulate(dlb_out, dlb, first)
        _accumulate(dcn_out, dcn, first)

    return pl.pallas_call(
        body, name="sgu_bwd", grid=(t // tm,),
        in_specs=[_row(tm, d), _row(tm, d), _const((1, d)), _const((d, 2 * d)), _const((1, d)), _const((1, d)),
                  _const((SGU_GROUPS, CHUNK, CHUNK)), _const((CHUNK, LANES)), _const((d, d))],
        out_specs=[_row(tm, d), _const((1, d)), _const((d, 2 * d)), _const((1, d)), _const((1, d)), _const((SGU_GROUPS, CHUNK, CHUNK)),
                   _const((CHUNK, LANES)), _const((d, d))],
        out_shape=[_sds((t, d)), _sds((1, d)), _sds((d, 2 * d)), _sds((1, d)), _sds((1, d)), _sds((SGU_GROUPS, CHUNK, CHUNK)),
                   _sds((CHUNK, LANES)), _sds((d, d))],
        scratch_shapes=[pltpu.VMEM((tm, d), BF16), pltpu.VMEM((tm, d), BF16), pltpu.VMEM((tm, 2 * d), BF16)],
        compiler_params=_params(),
    )(h, dout, w['c_norm'], w['c_w_in'], w['c_ln_g'], w['c_ln_b'], w['c_w_s'], w['bsT'], w['c_w_out'])


def _loss_and_grad(h, tgt, g):
    def loss_fn(h_, g_):
        err = _rms(h_, g_) - tgt
        return 0.5 * jnp.sum(jnp.mean(err * err, axis=-1, keepdims=True), axis=0, keepdims=True)

    loss, vjp_loss = jax.vjp(loss_fn, h, g)
    return (loss,) + vjp_loss(jnp.ones((1, 1), F32))


def _tile(t, seq, want):
    tm = min(want, seq)
    assert seq % tm == 0 and t % tm == 0 and tm % CHUNK == 0
    return tm


def _local_step(x, posb, target, w, seq, late_weights, on_grads):
    t, d = x.shape
    b = t // seq
    hp = HEADS * HEAD_PAD
    tm_big, tm_mid = _tile(t, seq, 512), _tile(t, seq, 256)
    tq = _tile(t, seq, 512)

    q, k, v, xl, gate = _ab_in_fwd(x, posb, w, tm_big)
    o, probs = _attn_fwd(q.reshape(b, seq, hp), k.reshape(b, seq, hp), v.reshape(b, seq, hp), tq)
    o = o.reshape(t, hp)
    y, hs = _lru_fwd(xl, gate, w, tm_big, seq)
    w = {**w, **late_weights('out0', y)}
    h1 = _ab_out_fwd(x, o, y, w, tm_big)
    hcur = h1
    saved = []
    for l in range(2):
        if l == 1:
            w = {**w, **late_weights('mix1', hcur)}
            saved_h2 = hcur
            hcur = _sgu_fwd(hcur, w, tm_mid)
        wl = late_weights('ffn%d' % l, hcur)
        g, u, hn = _ffn_a_fwd(hcur, w['ffn_norm'][l], wl['Wg'], wl['Wu'], tm_big)
        saved.append((hcur, g, u, wl, hn))
        ffn_b = (g, u, hcur, w['ffn_conv_w'][l], w['ffn_conv_b'][l], wl['Wd'], tm_big, seq)
        if l == 0:
            hcur = _ffn_b_fwd(*ffn_b)
    dh, loss, d_final = _ffn_b_fwd(*ffn_b, final=(target, w['final_norm']))

    ffn = {}
    conv_b = list(w['ffn_conv_b'])
    for l in (1, 0):
        hin, g, u, wl, hn = saved[l]
        dgc, du, d_wd, d_cw, d_cb = _ffn_b_bwd(g, u, dh, w['ffn_conv_w'][l], conv_b[l], wl['Wd'], tm_big, seq)
        dh, dg, d_norm = _ffn_a_dgrad(hin, w['ffn_norm'][l], dgc, du, dh, w['ffn_conv_w'][l], wl['Wg'], wl['Wu'], tm_mid, seq)
        d_wg, d_wu = _ffn_a_wgrad(hn, dg, du, _tile(t, seq, 1024))
        ffn[l] = dict(ffn_norm=d_norm, ffn_conv_w=d_cw, ffn_conv_b=d_cb, Wg=d_wg, Wu=d_wu, Wd=d_wd)
        if l == 1:
            dh, d_cn, d_cwin, d_lg, d_lb, d_ws, d_bst, d_cwout = _sgu_bwd(saved_h2, dh, w, tm_big, tm_mid)
            zero = on_grads('late1', dict(final_norm=d_final, c_norm=d_cn, c_ln_g=d_lg, c_ln_b=d_lb, c_w_s=d_ws, bsT=d_bst, c_w_in=d_cwin,
                                          c_w_out=d_cwout, Wg=[d_wg], Wu=[d_wu], Wd=[d_wd]))
            conv_b[0] = conv_b[0] + zero
    late0 = {name: [ffn[0][name], ffn[1][name]] for name in ('ffn_norm', 'ffn_conv_w', 'ffn_conv_b')}
    zero = on_grads('late0', dict(late0, Wg=[ffn[0]['Wg']], Wu=[ffn[0]['Wu']], Wd=[ffn[0]['Wd']]))
    w = {**w, 'Wo_b': w['Wo_b'] + zero.astype(w['Wo_b'].dtype)}
    do, dy, d_woa, d_wob = _ab_out_bwd(o, y, dh, w, _tile(t, seq, 1024))
    dxl, dgate, d_cw, d_cb, d_wa, d_ba, d_wx, d_bx, d_lam = _lru_bwd(xl, gate, hs, dy, w, tm_big, seq)
    zero = on_grads('mid', dict(Wo_a=d_woa, Wo_b=d_wob, ab_conv_w=d_cw, ab_conv_b=d_cb, Wa=d_wa, ab_b_rg_a=d_ba, Wx=d_wx,
                                ab_b_rg_x=d_bx, ab_lambda=d_lam))
    w = {**w, 'ab_norm': w['ab_norm'] + zero}
    dq, dk, dv = _attn_bwd(q.reshape(b, seq, hp), k.reshape(b, seq, hp), v.reshape(b, seq, hp), probs, do.reshape(b, seq, hp), tq)
    dx, d_gn, d_win, d_qn, d_wq, d_kvn, d_wk, d_wv = _ab_in_bwd(
        x, posb, w, dq.reshape(t, hp), dk.reshape(t, hp), dv.reshape(t, hp), dxl, dgate, dh, tm_big)
    return loss, dx, dict(ab_norm=d_gn, W_in=d_win, ab_q_norm=d_qn, Wq=d_wq, ab_kv_norm=d_kvn, Wk=d_wk, Wv=d_wv)


def _block_diag(wg):
    g, n, _ = wg.shape
    return jnp.einsum('gij,gh->gihj', wg, jnp.eye(g, dtype=wg.dtype)).reshape(g * n, g * n)


def _prepare_out(w_out):
    d = w_out.shape[2]
    mla = HEADS * QK_NOPE
    return {'Wo_a': jnp.pad(w_out[0, :mla].reshape(HEADS, QK_NOPE, d), ((0, 0), (0, HEAD_PAD - QK_NOPE), (0, 0))).reshape(HEADS * HEAD_PAD, d),
            'Wo_b': w_out[0, mla:]}


def _prepare(full):
    d = full['ab_w_in'].shape[1]
    w_in = full['ab_w_in'][0]
    zeros = lambda n: jnp.zeros((d, n), w_in.dtype)
    wq = full['ab_w_q_b'][0].reshape(Q_LORA, HEADS, QK_NOPE + QK_ROPE)
    wkv = full['ab_w_kv_b'][0].reshape(KV_LORA, HEADS, 2 * QK_NOPE)
    pad_head = lambda a: jnp.pad(a, ((0, 0), (0, 0), (0, HEAD_PAD - a.shape[2]))).reshape(a.shape[0], HEADS * HEAD_PAD)
    w = {
        'W_in': jnp.concatenate([w_in[:, :Z_KPE], zeros(QK_NOPE), w_in[:, Z_KPE:Z_KPE + QK_ROPE],
                                 zeros(HEAD_PAD - QK_NOPE - QK_ROPE), w_in[:, Z_KPE + QK_ROPE:]], axis=1),
        'Wq': pad_head(wq), 'Wk': pad_head(wkv[:, :, :QK_NOPE]), 'Wv': pad_head(wkv[:, :, QK_NOPE:]),
        'Wa': _bf(_block_diag(full['ab_w_rg_a'][0])), 'Wx': _bf(_block_diag(full['ab_w_rg_x'][0])),
        'c_w_s': full['c_w_s'][0],
        'bsT': jnp.pad(full['c_b_s'][0].T, ((0, 0), (0, LANES - SGU_GROUPS))),
        'ffn_norm': [full['ffn_norm'][l:l + 1] for l in range(2)], 'ffn_conv_w': [full['ffn_conv_w'][l] for l in range(2)],
        'ffn_conv_b': [full['ffn_conv_b'][l:l + 1] for l in range(2)],
        'ab_conv_w': full['ab_conv_w'][0], 'final_norm': full['final_norm'][None, :],
    }
    for name in ('ab_norm', 'ab_q_norm', 'ab_kv_norm', 'ab_conv_b', 'ab_b_rg_a', 'ab_b_rg_x', 'ab_lambda', 'c_norm', 'c_ln_g', 'c_ln_b'):
        w[name] = full[name]
    return w


def _unprepare(g):
    unpad_head = lambda a, n: a.reshape(a.shape[0], HEADS, HEAD_PAD)[:, :, :n]
    diag = lambda a: jnp.einsum('gigj->gij', a.reshape(HEADS, LRU_W // HEADS, HEADS, LRU_W // HEADS))
    rules = {
        'ab_w_in': (('W_in',), lambda a: jnp.concatenate([a[:, :Z_KPE], a[:, Z_KPE + QK_NOPE:Z_KPE + QK_NOPE + QK_ROPE], a[:, Z_LRU:]], axis=1)[None]),
        'ab_w_q_b': (('Wq',), lambda a: unpad_head(a, QK_NOPE + QK_ROPE).reshape(1, Q_LORA, -1)),
        'ab_w_kv_b': (('Wk', 'Wv'), lambda a, b: jnp.concatenate([unpad_head(a, QK_NOPE), unpad_head(b, QK_NOPE)], axis=2).reshape(1, KV_LORA, -1)),
        'ab_w_out': (('Wo_a', 'Wo_b'), lambda a, b: jnp.concatenate(
            [a.reshape(HEADS, HEAD_PAD, -1)[:, :QK_NOPE].reshape(HEADS * QK_NOPE, -1), b], axis=0)[None]),
        'ab_w_rg_a': (('Wa',), lambda a: diag(a)[None]), 'ab_w_rg_x': (('Wx',), lambda a: diag(a)[None]),
        'c_w_in': (('c_w_in',), lambda a: a[None]), 'c_w_out': (('c_w_out',), lambda a: a[None]), 'c_w_s': (('c_w_s',), lambda a: a[None]),
        'c_b_s': (('bsT',), lambda a: a[:, :SGU_GROUPS].T[None]),
        'ffn_w_gate': (('Wg',), jnp.stack), 'ffn_w_up': (('Wu',), jnp.stack), 'ffn_w_down': (('Wd',), jnp.stack),
        'ffn_norm': (('ffn_norm',), lambda a: jnp.concatenate(a, axis=0)), 'ffn_conv_w': (('ffn_conv_w',), jnp.stack),
        'ffn_conv_b': (('ffn_conv_b',), lambda a: jnp.concatenate(a, axis=0)),
        'ab_conv_w': (('ab_conv_w',), lambda a: a[None]), 'final_norm': (('final_norm',), lambda a: a[0]),
    }
    for name in ('ab_norm', 'ab_q_norm', 'ab_kv_norm', 'ab_conv_b', 'ab_b_rg_a', 'ab_b_rg_x', 'ab_lambda', 'c_norm', 'c_ln_g', 'c_ln_b'):
        rules[name] = ((name,), lambda a: a)
    return {name: fn(*[g[k] for k in keys]) for name, (keys, fn) in rules.items() if all(k in g for k in keys)}


SLAB_ROWS = 16


def _round_up(n, m):
    return -(-n // m) * m


def _to_chunks(full, axis):
    s = full.shape
    return jnp.moveaxis(full.reshape(s[:axis] + (N_DEV, s[axis] // N_DEV) + s[axis + 1:]), axis, 0)


def _from_chunks(chunks, axis):
    local = chunks.shape[1:]
    return jnp.moveaxis(chunks, 0, axis).reshape(local[:axis] + (N_DEV * local[axis],) + local[axis + 1:])


def _merge_columns(landed, name):
    _, _, r, n = landed.shape
    tr = r // 4

    def body(l_ref, o_ref):
        o_ref[0] = jnp.concatenate([l_ref[dev, 0] for dev in range(N_DEV)], axis=1)

    return pl.pallas_call(body, name="merge_" + name, grid=(r // tr,),
                          in_specs=[pl.BlockSpec((N_DEV, 1, tr, n), lambda i: (0, 0, i, 0))],
                          out_specs=pl.BlockSpec((1, tr, N_DEV * n), lambda i: (0, i, 0)),
                          out_shape=jax.ShapeDtypeStruct((1, r, N_DEV * n), landed.dtype), compiler_params=_params())(landed)


def _split_chunks(whole, axis, name):
    _, rows, cols = whole.shape
    if axis == 1:
        r = rows // N_DEV

        def body(x_ref, o_ref):
            o_ref[0] = _bf(x_ref[...])

        grid, out_shape = (N_DEV,), (N_DEV, 1, r, cols)
        spec, out_spec = pl.BlockSpec((1, r, cols), lambda dev: (0, dev, 0)), pl.BlockSpec((1, 1, r, cols), lambda dev: (dev, 0, 0, 0))
    else:
        n, tr = cols // N_DEV, rows // 4

        def body(x_ref, o_ref):
            x = x_ref[0]
            for dev in range(N_DEV):
                o_ref[dev, 0] = _bf(x[:, dev * n:(dev + 1) * n])

        grid, out_shape = (rows // tr,), (N_DEV, 1, rows, n)
        spec, out_spec = pl.BlockSpec((1, tr, cols), lambda i: (0, i, 0)), pl.BlockSpec((N_DEV, 1, tr, n), lambda i: (0, 0, i, 0))
    return pl.pallas_call(body, name="split_" + name, grid=grid, in_specs=[spec], out_specs=out_spec,
                          out_shape=jax.ShapeDtypeStruct(out_shape, BF16), compiler_params=_params())(whole)


def _slab_rows(n):
    return _round_up(-(-n // LANES), SLAB_ROWS)


def _to_slab(a, lead):
    a = a.reshape(lead + (-1,))
    rows = _slab_rows(a.shape[-1])
    a = jnp.pad(a, [(0, 0)] * len(lead) + [(0, rows * LANES - a.shape[-1])])
    return a.reshape(lead + (rows, LANES))


def _pack_slabs(parts, lead):
    return jnp.concatenate([_to_slab(p, lead) for p in parts], axis=len(lead))


def _unpack_slabs(packed, shapes):
    lead = packed.shape[:-2]
    out, row = [], 0
    for shape in shapes:
        size = math.prod(shape)
        rows = _slab_rows(size)
        piece = lax.slice_in_dim(packed, row, row + rows, axis=len(lead))
        out.append(piece.reshape(lead + (rows * LANES,))[..., :size].reshape(lead + tuple(shape)))
        row += rows
    return out


HBM = pl.BlockSpec(memory_space=pl.ANY)


def _other_chips(x, y):
    return [(1 - x, y), (x, 1 - y), (1 - x, 1 - y)]


def _all_gather(blocks):
    n = len(blocks)

    def body(*refs):
        x_refs, out_refs, token = refs[:n], refs[n:2 * n], refs[2 * n]
        send_sems, recv_sems, local_sems = refs[2 * n + 1:]
        token[...] = jnp.zeros_like(token)
        x, y, c = lax.axis_index("x"), lax.axis_index("y"), lax.axis_index("c")
        me, sibling = (x, y, c), (x, y, 1 - c)
        chips = _other_chips(x, y)

        def slab(a, px, py, pc):
            return out_refs[a].at[4 * px + 2 * py + pc]

        def copy(a, k, blk, to, src=None):
            return pltpu.make_async_remote_copy(src_ref=slab(a, *blk) if src is None else src, dst_ref=slab(a, *blk),
                                                send_sem=send_sems.at[7 * a + k], recv_sem=recv_sems.at[7 * a + k],
                                                device_id=to, device_id_type=MESH)

        mine = [pltpu.make_async_copy(x_refs[a], slab(a, *me), local_sems.at[a]) for a in range(n)]
        started = []
        for a in range(n):
            mine[a].start()
            started.append(copy(a, 0, me, sibling, src=x_refs[a]))
            started += [copy(a, 1 + j, me, (*chip, c), src=x_refs[a]) for j, chip in enumerate(chips)]
        for cp in started:
            cp.start()
        for j, chip in enumerate(chips):
            for a in range(n):
                copy(a, 1 + j, (*chip, c), me).wait_recv()
                passed = copy(a, 4 + j, (*chip, c), sibling)
                passed.start()
                started.append(passed)
        for a in range(n):
            copy(a, 0, sibling, me).wait_recv()
        for j, chip in enumerate(chips):
            for a in range(n):
                copy(a, 4 + j, (*chip, 1 - c), me).wait_recv()
        for cp in started:
            cp.wait_send()
        for a in range(n):
            mine[a].wait()

    out = pl.pallas_call(
        body, name="all_gather_weights",
        out_shape=[jax.ShapeDtypeStruct((N_DEV,) + b.shape, b.dtype) for b in blocks] + [jax.ShapeDtypeStruct((8, LANES), F32)],
        in_specs=[HBM] * n, out_specs=[HBM] * n + [pl.BlockSpec(memory_space=pltpu.VMEM)],
        scratch_shapes=[pltpu.SemaphoreType.DMA((7 * n,)), pltpu.SemaphoreType.DMA((7 * n,)), pltpu.SemaphoreType.DMA((n,))],
    )(*blocks)
    return list(out[:n]), out[n][0, 0]


FLIPS = [(0, 0, 1), (1, 0, 0), (1, 0, 1), (0, 1, 0), (0, 1, 1), (1, 1, 0), (1, 1, 1)]


def _peers(x, y, c):
    flip = lambda v, f: 1 - v if f else v
    return [(flip(x, fx), flip(y, fy), flip(c, fc)) for fx, fy, fc in FLIPS]


def _direct_copies(src_refs, land_refs, send_sems, recv_sems, scatter):
    x, y, c = lax.axis_index("x"), lax.axis_index("y"), lax.axis_index("c")
    me = 4 * x + 2 * y + c
    starts, waits = [], []
    for a in range(len(src_refs)):
        for k, (px, py, pc) in enumerate(_peers(x, y, c)):
            peer = 4 * px + 2 * py + pc
            sems = dict(send_sem=send_sems.at[7 * a + k], recv_sem=recv_sems.at[7 * a + k], device_id=(px, py, pc), device_id_type=MESH)
            src = src_refs[a].at[peer] if scatter else src_refs[a]
            starts.append(pltpu.make_async_remote_copy(src_ref=src, dst_ref=land_refs[a].at[me], **sems))
            waits.append(pltpu.make_async_remote_copy(src_ref=src, dst_ref=land_refs[a].at[peer], **sems))
    n = len(src_refs)
    keeps = [] if scatter else [pltpu.make_async_copy(src_refs[a], land_refs[a].at[me], send_sems.at[7 * n + a]) for a in range(n)]
    return starts, waits, keeps


def _landing(src, scatter):
    block = src.shape[1:] if scatter else src.shape
    return jax.ShapeDtypeStruct((N_DEV,) + block, src.dtype)


HBM_SPACE = pl.BlockSpec(memory_space=pltpu.HBM)
SEMAPHORES = pl.BlockSpec(memory_space=pltpu.SEMAPHORE)
SPLIT_EFFECT = pltpu.SideEffectType.DATAFLOW_SIDE_EFFECTING


def _start_exchange(name, srcs, scatter):
    n = len(srcs)
    lands = [lax.empty(s.shape, s.dtype) for s in (_landing(s, scatter) for s in srcs)]

    def body(*refs):
        starts, _, keeps = _direct_copies(refs[:n], refs[n:2 * n], refs[2 * n], refs[2 * n + 1], scatter)
        for cp in starts + keeps:
            cp.start()
        refs[-1][...] = jnp.zeros_like(refs[-1])

    held = [pltpu.with_memory_space_constraint(a, pltpu.HBM) for a in list(srcs) + lands]
    out = pl.pallas_call(
        body, name=name + "_start",
        out_shape=(pltpu.SemaphoreType.DMA(((7 if scatter else 8) * n,)), pltpu.SemaphoreType.DMA((7 * n,)),
                   *[pltpu.HBM(a.shape, a.dtype) for a in held],
                   jax.ShapeDtypeStruct((8, LANES), F32)),
        in_specs=[HBM_SPACE] * (2 * n), out_specs=(SEMAPHORES, SEMAPHORES, *[HBM_SPACE] * (2 * n), pl.BlockSpec(memory_space=pltpu.VMEM)),
        input_output_aliases={i: 2 + i for i in range(2 * n)},
        compiler_params=pltpu.CompilerParams(has_side_effects=SPLIT_EFFECT),
    )(*held)
    return out[0], out[1], list(out[2:2 + n]), list(out[2 + n:2 + 2 * n]), out[-1][0, 0], out[-1]


def _wait_exchange(name, started, after, scatter):
    send_sems, recv_sems, srcs, lands = started[:4]
    n = len(srcs)

    def body(*refs):
        _, waits, keeps = _direct_copies(refs[:n], refs[n:2 * n], refs[2 * n], refs[2 * n + 1], scatter)
        for cp in waits:
            cp.wait_send()
        for cp in waits:
            cp.wait_recv()
        for cp in keeps:
            cp.wait()

    out = pl.pallas_call(
        body, name=name + "_wait", out_shape=tuple(pltpu.HBM(a.shape, a.dtype) for a in srcs + lands),
        in_specs=[HBM_SPACE] * (2 * n) + [SEMAPHORES, SEMAPHORES, HBM], out_specs=tuple([HBM_SPACE] * (2 * n)),
        input_output_aliases={i: i for i in range(2 * n)},
        compiler_params=pltpu.CompilerParams(has_side_effects=SPLIT_EFFECT),
    )(*srcs, *lands, send_sems, recv_sems, after)
    return list(out[:n]), list(out[n:])


def _row_tile(rows):
    return rows // 2 if (rows // 2) % SLAB_ROWS == 0 else rows


def _sum_in_device_order(me_ref, l_ref, own_ref):
    mine = own_ref[0].astype(F32)
    g = jnp.where(me_ref[0] == 0, mine, l_ref[0].astype(F32))
    for dev in range(1, N_DEV):
        g = g + jnp.where(me_ref[0] == dev, mine, l_ref[dev].astype(F32))
    return g


def _adamw(g, w, m, v):
    m_new = ADAM_B1 * m + (1.0 - ADAM_B1) * g
    v_new = ADAM_B2 * v + (1.0 - ADAM_B2) * (g * g)
    m_hat = m_new * (1.0 / (1.0 - ADAM_B1 ** ADAM_STEP))
    v_hat = v_new * (1.0 / (1.0 - ADAM_B2 ** ADAM_STEP))
    return -ADAM_LR * (m_hat / (jnp.sqrt(v_hat) + ADAM_EPS) + ADAM_WD * w), m_new, v_new


def _sum_chunks(me, landed, own, name):
    _, _, r, n = landed.shape

    def body(me_ref, l_ref, own_ref, g_out):
        g_out[...] = _sum_in_device_order(me_ref, l_ref, own_ref)[0]

    return pl.pallas_call(
        body, name="sum_" + name,
        grid_spec=pltpu.PrefetchScalarGridSpec(
            num_scalar_prefetch=1, grid=(1,),
            in_specs=[pl.BlockSpec((N_DEV, 1, r, n), lambda i, me_ref: (0, 0, 0, 0)),
                      pl.BlockSpec((1, 1, r, n), lambda i, me_ref: (me_ref[0], 0, 0, 0))],
            out_specs=pl.BlockSpec((r, n), lambda i, me_ref: (0, 0))),
        out_shape=_sds((r, n)), compiler_params=_params())(me, landed, own)


def _adamw_small(gs, ws, ms, vs):
    n = len(gs)

    def body(*refs):
        ins, outs = refs[:4 * n], refs[4 * n:]
        for i in range(n):
            outs[i][...], outs[n + i][...], outs[2 * n + i][...] = _adamw(*[ins[k * n + i][...] for k in range(4)])

    out = pl.pallas_call(body, name="adamw_small", out_shape=[_sds(w.shape) for w in ws] * 3)(*gs, *ws, *ms, *vs)
    return out[:n], out[n:2 * n], out[2 * n:]


def _sum_and_adamw(me, landed, own, wts, m, v, name, layer=None, into=None):
    layers, r, n = wts.shape
    first = 0 if layer is None else layer
    count = layers if layer is None else 1
    tr = _row_tile(r)
    blk = pl.BlockSpec((1, tr, n), lambda li, ri, me_ref: (first + li, ri, 0))
    held = [] if into is None else list(into)

    def body(me_ref, l_ref, own_ref, w_ref, m_ref, v_ref, *rest):
        g_out, d_out, m_out, v_out = rest[len(held):]
        g = _sum_in_device_order(me_ref, l_ref, own_ref)
        g_out[...] = g
        d_out[...], m_out[...], v_out[...] = _adamw(g, w_ref[...], m_ref[...], v_ref[...])

    return pl.pallas_call(
        body, name="adamw_" + name,
        grid_spec=pltpu.PrefetchScalarGridSpec(
            num_scalar_prefetch=1, grid=(count, r // tr),
            in_specs=[pl.BlockSpec((N_DEV, 1, tr, n), lambda li, ri, me_ref: (0, li, ri, 0)),
                      pl.BlockSpec((1, 1, tr, n), lambda li, ri, me_ref: (me_ref[0], li, ri, 0)), blk, blk, blk] + [HBM] * len(held),
            out_specs=[blk] * 4),
        out_shape=[_sds((layers, r, n))] * 4, input_output_aliases={6 + i: i for i in range(len(held))},
        compiler_params=_params(2))(me, landed, own, wts, m, v, *held)


EARLY = ['ab_w_in']
LATE_STAGES = {
    'out0': [('ab_w_out', None, 'ab_w_out')],
    'ffn0': [('ffn_w_gate', 0, 'Wg'), ('ffn_w_up', 0, 'Wu'), ('ffn_w_down', 0, 'Wd')],
    'mix1': [('c_w_in', None, 'c_w_in'), ('c_w_out', None, 'c_w_out')],
    'ffn1': [('ffn_w_gate', 1, 'Wg'), ('ffn_w_up', 1, 'Wu'), ('ffn_w_down', 1, 'Wd')],
}
TRANSPOSED = ('ffn_w_gate', 'ffn_w_up')


def _stored(name, a):
    return jnp.swapaxes(a, 1, 2) if name in TRANSPOSED else a


def _stored_axis(name):
    return 1 if name in TRANSPOSED else SHARD_AXIS[name]


GRAD_STAGES = {
    'late1': ([('c_w_in', None), ('c_w_out', None), ('ffn_w_gate', 1), ('ffn_w_up', 1), ('ffn_w_down', 1)],
              ['c_norm', 'c_ln_g', 'c_ln_b', 'c_w_s', 'c_b_s', 'final_norm']),
    'late0': ([('ffn_w_gate', 0), ('ffn_w_up', 0), ('ffn_w_down', 0)], ['ffn_norm', 'ffn_conv_w', 'ffn_conv_b']),
    'mid': ([('ab_w_out', None)], ['ab_conv_w', 'ab_conv_b', 'ab_w_rg_a', 'ab_b_rg_a', 'ab_w_rg_x', 'ab_b_rg_x', 'ab_lambda']),
    'last': ([('ab_w_in', None)], ['ab_norm', 'ab_q_norm', 'ab_w_q_b', 'ab_kv_norm', 'ab_w_kv_b']),
}


def _gather_early(local):
    small = [_bf(local[n]) if n in MATRICES else lax.bitcast_convert_type(local[n], BF16) for n in SMALL_SHARDED]
    gathered, zero = _all_gather([_bf(local[n]) for n in EARLY] + [_pack_slabs(small, ())])
    full = {n: local[n] for n in REPLICATED}
    for n, g in zip(EARLY, gathered):
        full[n] = _from_chunks(g, SHARD_AXIS[n])
    for n, p in zip(SMALL_SHARDED, _unpack_slabs(gathered[-1], [s.shape for s in small])):
        full[n] = _from_chunks(p if n in MATRICES else lax.bitcast_convert_type(p, F32), SHARD_AXIS[n])
    return full, zero


def kernel(x, positions, ab_norm, ab_w_in, ab_q_norm, ab_w_q_b, ab_kv_norm, ab_w_kv_b, ab_conv_w, ab_conv_b, ab_w_rg_a, ab_b_rg_a, ab_w_rg_x, ab_b_rg_x, ab_lambda, ab_w_out, c_norm, c_w_in, c_ln_g, c_ln_b, c_w_s, c_b_s, c_w_out, ffn_norm, ffn_w_gate, ffn_w_up, ffn_conv_w, ffn_conv_b, ffn_w_down, final_norm, loss_target, m_ab_norm, m_ab_w_in, m_ab_q_norm, m_ab_w_q_b, m_ab_kv_norm, m_ab_w_kv_b, m_ab_conv_w, m_ab_conv_b, m_ab_w_rg_a, m_ab_b_rg_a, m_ab_w_rg_x, m_ab_b_rg_x, m_ab_lambda, m_ab_w_out, m_c_norm, m_c_w_in, m_c_ln_g, m_c_ln_b, m_c_w_s, m_c_b_s, m_c_w_out, m_ffn_norm, m_ffn_w_gate, m_ffn_w_up, m_ffn_conv_w, m_ffn_conv_b, m_ffn_w_down, m_final_norm, v_ab_norm, v_ab_w_in, v_ab_q_norm, v_ab_w_q_b, v_ab_kv_norm, v_ab_w_kv_b, v_ab_conv_w, v_ab_conv_b, v_ab_w_rg_a, v_ab_b_rg_a, v_ab_w_rg_x, v_ab_b_rg_x, v_ab_lambda, v_ab_w_out, v_c_norm, v_c_w_in, v_c_ln_g, v_c_ln_b, v_c_w_s, v_c_b_s, v_c_w_out, v_ffn_norm, v_ffn_w_gate, v_ffn_w_up, v_ffn_conv_w, v_ffn_conv_b, v_ffn_w_down, v_final_norm):
    given = dict(locals())
    local = {n: given[n] for n in WEIGHTS}
    b, seq, d = x.shape
    t = b * seq

    me = (4 * lax.axis_index("x") + 2 * lax.axis_index("y") + lax.axis_index("c")).astype(jnp.int32)
    me1 = me.reshape(1)

    full, zero = _gather_early(local)
    gathers = {}
    for stage, members in LATE_STAGES.items():
        srcs = [_bf(_stored(n, local[n] if layer is None else local[n][layer:layer + 1]) + zero) for n, layer, _ in members]
        gathers[stage] = _start_exchange('gather_' + stage, srcs, scatter=False)
        zero = gathers[stage][4]
    w = _prepare(full)
    w['ab_norm'] = w['ab_norm'] + zero

    def late_weights(stage, after):
        _, lands = _wait_exchange('gather_' + stage, gathers[stage], after, scatter=False)
        whole = [l.reshape(1, -1, l.shape[-1]) if _stored_axis(n) == 1 else _merge_columns(l, n)
                 for (n, _, _), l in zip(LATE_STAGES[stage], lands)]
        if stage == 'out0':
            return _prepare_out(whole[0])
        return {key: a[0] for (_, _, key), a in zip(LATE_STAGES[stage], whole)}

    scatters = {}

    def start_scatter(stage, g):
        whole = _unprepare(g)
        big, small = GRAD_STAGES[stage]
        slab = [_to_chunks(whole[n], SHARD_AXIS[n]) if n in SHARD_AXIS else jnp.broadcast_to(whole[n][None], (N_DEV,) + whole[n].shape)
                for n in small]
        own = [whole[n].reshape(N_DEV, 1, whole[n].shape[1] // N_DEV, whole[n].shape[2])
               if whole[n].dtype == BF16 and _stored_axis(n) == 1 else
               _split_chunks(whole[n], _stored_axis(n), n + ('' if layer is None else str(layer))) for n, layer in big]
        own.append(_bf(_pack_slabs(slab, (N_DEV,)))[:, None])
        scatters[stage] = _start_exchange('scatter_' + stage, own, scatter=True)
        return scatters[stage][4]

    posb = jnp.broadcast_to(positions.astype(F32).reshape(t, 1), (t, LANES))
    loss, dx, grads = _local_step(x.reshape(t, d), posb, loss_target.reshape(t, d), w, seq, late_weights, start_scatter)
    start_scatter('last', grads)
    after = scatters['last'][5]

    updated, small_grads = {}, {}
    for stage, (big, small) in GRAD_STAGES.items():
        owns, landed = _wait_exchange('scatter_' + stage, scatters[stage], after, scatter=True)
        for (n, layer), own, land in zip(big, owns, landed):
            updated[n] = _sum_and_adamw(me1, land, own, _stored(n, given[n]), _stored(n, given['m_' + n]), _stored(n, given['v_' + n]),
                                        n + ('' if layer is None else str(layer)), layer, updated.get(n))
        summed = _sum_chunks(me1, landed[-1], owns[-1], stage)
        small_grads.update(zip(small, _unpack_slabs(summed, [local[n].shape for n in small])))
        after = sum([updated[n][1][:1, :1, :1] for n, _ in big], summed[:1, :1].reshape(1, 1, 1))
    names = list(small_grads)
    news = _adamw_small([small_grads[n] for n in names], *[[given[p + n] for n in names] for p in ('', 'm_', 'v_')])
    for i, n in enumerate(names):
        updated[n] = [small_grads[n], news[0][i], news[1][i], news[2][i]]
    total = lax.psum(loss[0, 0], ("x", "y", "c"))
    return (total, dx.reshape(b, seq, d), *[_stored(n, updated[n][kind]) for kind in range(4) for n in WEIGHTS])
```

```python
import math

import jax
import jax.numpy as jnp
from jax import lax
from jax.experimental import pallas as pl
from jax.experimental.pallas import tpu as pltpu

F32 = jnp.float32
BF16 = jnp.bfloat16
MESH = pl.DeviceIdType.MESH

N_DEV = 8
LANES = 128
HALO = 8
VMEM_LIMIT = 56 << 20

NORM_EPS = 1e-6
HEADS = 8
HEAD_PAD = 128
QK_NOPE = 64
QK_ROPE = 32
ROPE_HALF = 16
ROPE_BASE = 10000.0
ATTN_SCALE = (QK_NOPE + QK_ROPE) ** -0.5
LRU_C = 8.0
LRU_W = 512
CHUNK = 128
SGU_GROUPS = 8
D_FF = 2816
FF_BLOCKS = 2

ADAM_LR, ADAM_B1, ADAM_B2, ADAM_EPS, ADAM_WD, ADAM_STEP = 0.001, 0.9, 0.999, 1e-08, 0.01, 10

WEIGHTS = ['ab_norm', 'ab_w_in', 'ab_q_norm', 'ab_w_q_b', 'ab_kv_norm', 'ab_w_kv_b', 'ab_conv_w', 'ab_conv_b',
           'ab_w_rg_a', 'ab_b_rg_a', 'ab_w_rg_x', 'ab_b_rg_x', 'ab_lambda', 'ab_w_out', 'c_norm', 'c_w_in', 'c_ln_g',
           'c_ln_b', 'c_w_s', 'c_b_s', 'c_w_out', 'ffn_norm', 'ffn_w_gate', 'ffn_w_up', 'ffn_conv_w', 'ffn_conv_b',
           'ffn_w_down', 'final_norm']
SHARD_AXIS = {'ab_w_in': 2, 'ab_w_q_b': 2, 'ab_w_kv_b': 2, 'ab_conv_w': 2, 'ab_w_out': 1, 'c_norm': 1, 'c_w_in': 2,
              'c_ln_g': 1, 'c_ln_b': 1, 'c_w_out': 1, 'ffn_w_gate': 2, 'ffn_w_up': 2, 'ffn_conv_w': 2, 'ffn_w_down': 1}
MATRICES = ['ab_w_in', 'ab_w_q_b', 'ab_w_kv_b', 'ab_w_out', 'c_w_in', 'c_w_out', 'ffn_w_gate', 'ffn_w_up', 'ffn_w_down']
BIG = ['ab_w_in', 'c_w_in', 'ffn_w_gate', 'ffn_w_up', 'ab_w_out', 'c_w_out', 'ffn_w_down']
REPLICATED = [n for n in WEIGHTS if n not in SHARD_AXIS]
SMALL_SHARDED = [n for n in WEIGHTS if n in SHARD_AXIS and n not in BIG]


def _bf(x):
    return x.astype(BF16)


def _nn(a, b):
    return lax.dot_general(_bf(a), _bf(b), (((1,), (0,)), ((), ())), preferred_element_type=F32)


def _nt(a, b):
    return lax.dot_general(_bf(a), _bf(b), (((1,), (1,)), ((), ())), preferred_element_type=F32)


def _tn(a, b):
    return lax.dot_general(_bf(a), _bf(b), (((0,), (0,)), ((), ())), preferred_element_type=F32)


def _rms(x, g):
    return x * lax.rsqrt(jnp.mean(x * x, axis=-1, keepdims=True) + NORM_EPS) * g


def _layer_norm(x, g, b):
    xc = x - jnp.mean(x, axis=-1, keepdims=True)
    return xc * lax.rsqrt(jnp.mean(xc * xc, axis=-1, keepdims=True) + NORM_EPS) * g + b


def _gelu(x):
    return jax.nn.gelu(x)


STRIP = 16
STRIP_LANES = 384
GELU_C = math.sqrt(2.0 / math.pi)
GELU_A = 0.044715


def _gelu_and_grad(x):
    x2 = x * x
    t = jnp.tanh(x * (GELU_C + (GELU_C * GELU_A) * x2))
    half_x = 0.5 * x
    one_plus_t = 1.0 + t
    return half_x * one_plus_t, 0.5 * one_plus_t + half_x * (1.0 - t * t) * (GELU_C + (3.0 * GELU_C * GELU_A) * x2)


def _colsum(x):
    return jnp.sum(x, axis=0, keepdims=True)


def _softplus(x):
    return jnp.maximum(x, 0.0) + jnp.log1p(jnp.exp(-jnp.abs(x)))


@jax.custom_vjp
def _decay(x):
    a = jnp.exp(x)
    y = 2.0 * x
    series = -y * (1.0 + y * (1 / 2 + y * (1 / 6 + y * (1 / 24 + y * (1 / 120 + y * (1 / 720))))))
    return a, jnp.where(y < -0.3, 1.0 - a * a, series)


def _decay_fwd(x):
    a, gap = _decay(x)
    return (a, gap), a


def _decay_bwd(a, cts):
    return (a * (cts[0] - 2.0 * a * cts[1]),)


_decay.defvjp(_decay_fwd, _decay_bwd)


def _accumulate(ref, val, first):
    @pl.when(first)
    def _():
        ref[...] = val

    @pl.when(jnp.logical_not(first))
    def _():
        ref[...] += val


def _params(n_axes=1):
    return pltpu.CompilerParams(dimension_semantics=("arbitrary",) * n_axes, vmem_limit_bytes=VMEM_LIMIT)


def _row(tm, n):
    return pl.BlockSpec((tm, n), lambda i: (i, 0))


def _const(shape):
    nd = len(shape)
    return pl.BlockSpec(shape, lambda i: (0,) * nd, pipeline_mode=pl.Buffered(1))


def _prev_halo(tm, n):
    return pl.BlockSpec((HALO, n), lambda i: (jnp.maximum(i * (tm // HALO) - 1, 0), 0))


def _next_halo(tm, n, n_tiles):
    last = n_tiles * (tm // HALO) - 1
    return pl.BlockSpec((HALO, n), lambda i: (jnp.minimum((i + 1) * (tm // HALO), last), 0))


def _sds(shape, dtype=F32):
    return jax.ShapeDtypeStruct(shape, dtype)


def _rope_tables(posb):
    lane = lax.broadcasted_iota(jnp.int32, posb.shape, 1)
    in_rope = jnp.logical_and(lane >= QK_NOPE, lane < QK_NOPE + QK_ROPE)
    j = (lane & (ROPE_HALF - 1)).astype(F32)
    inv_freq = jnp.exp((-math.log(ROPE_BASE)) * j / ROPE_HALF)
    ang = posb * inv_freq
    return jnp.where(in_rope, jnp.cos(ang), 1.0), jnp.where(in_rope, jnp.sin(ang), 0.0)


def _rot(q):
    n = q.shape[1]
    lane = lax.broadcasted_iota(jnp.int32, q.shape, 1) & (HEAD_PAD - 1)
    first_half = jnp.where(lane >= QK_NOPE, -pltpu.roll(q, n - ROPE_HALF, 1), 0.0)
    second_half = jnp.where(lane < QK_NOPE + QK_ROPE, pltpu.roll(q, ROPE_HALF, 1), 0.0)
    return jnp.where(lane < QK_NOPE + ROPE_HALF, first_half, second_half)


def _rope(q, cos_t, sin_t):
    return q * cos_t + _rot(q) * sin_t


def _rope_transpose(dq, cos_t, sin_t):
    return dq * cos_t - _rot(dq * sin_t)


def _tile_heads(t):
    return jnp.concatenate([t] * HEADS, axis=1)


Q_LORA, KV_LORA = 256, 128
Z_KPE = Q_LORA + KV_LORA
Z_LRU = Z_KPE + HEAD_PAD
Z_GATE = Z_LRU + LRU_W
Z_WIDTH = Z_GATE + LRU_W


def _ab_in_fwd(x, posb, w, tm):
    t, d = x.shape

    def body(x_ref, pos_ref, gn_ref, win_ref, qn_ref, wq_ref, kvn_ref, wk_ref, wv_ref, q_out, k_out, v_out, xl_out, gate_out):
        hn = _rms(x_ref[...], gn_ref[...])
        z = _nn(hn, win_ref[...])
        cqn = _rms(z[:, :Q_LORA], qn_ref[...])
        kvn = _rms(z[:, Q_LORA:Z_KPE], kvn_ref[...])
        cos_t, sin_t = _rope_tables(pos_ref[...])
        q_out[...] = _bf(_rope(_nn(cqn, wq_ref[...]), _tile_heads(cos_t), _tile_heads(sin_t)))
        kpe = _rope(z[:, Z_KPE:Z_LRU], cos_t, sin_t)
        k_out[...] = _bf(_nn(kvn, wk_ref[...]) + _tile_heads(kpe))
        v_out[...] = _bf(_nn(kvn, wv_ref[...]))
        xl_out[...] = z[:, Z_LRU:Z_GATE]
        gate_out[...] = z[:, Z_GATE:]

    hp = HEADS * HEAD_PAD
    return pl.pallas_call(
        body, name="ab_in_fwd", grid=(t // tm,),
        in_specs=[_row(tm, d), _row(tm, LANES), _const((1, d)), _const((d, Z_WIDTH)), _const((1, Q_LORA)), _const((Q_LORA, hp)),
                  _const((1, KV_LORA)), _const((KV_LORA, hp)), _const((KV_LORA, hp))],
        out_specs=[_row(tm, hp), _row(tm, hp), _row(tm, hp), _row(tm, LRU_W), _row(tm, LRU_W)],
        out_shape=[_sds((t, hp), BF16), _sds((t, hp), BF16), _sds((t, hp), BF16), _sds((t, LRU_W)), _sds((t, LRU_W))],
        compiler_params=_params(),
    )(x, posb, w['ab_norm'], w['W_in'], w['ab_q_norm'], w['Wq'], w['ab_kv_norm'], w['Wk'], w['Wv'])


def _ab_in_bwd(x, posb, w, dq, dk, dv, dxl, dgate, dres, tm):
    t, d = x.shape
    hp = HEADS * HEAD_PAD

    def body(x_ref, pos_ref, gn_ref, win_ref, qn_ref, wq_ref, kvn_ref, wk_ref, wv_ref, dq_ref, dk_ref, dv_ref, dxl_ref, dgate_ref,
             dres_ref, dx_out, dgn_out, dwin_out, dqn_out, dwq_out, dkvn_out, dwk_out, dwv_out):
        first = pl.program_id(0) == 0
        hn, vjp_in = jax.vjp(_rms, x_ref[...], gn_ref[...])
        z = _nn(hn, win_ref[...])
        cqn, vjp_q = jax.vjp(_rms, z[:, :Q_LORA], qn_ref[...])
        kvn, vjp_kv = jax.vjp(_rms, z[:, Q_LORA:Z_KPE], kvn_ref[...])
        cos_t, sin_t = _rope_tables(pos_ref[...])
        dq0 = _rope_transpose(dq_ref[...], _tile_heads(cos_t), _tile_heads(sin_t))
        dk0 = dk_ref[...]
        dv0 = dv_ref[...]
        dkpe = dk0[:, :HEAD_PAD]
        for h in range(1, HEADS):
            dkpe = dkpe + dk0[:, h * HEAD_PAD:(h + 1) * HEAD_PAD]
        dkpe = _rope_transpose(dkpe, cos_t, sin_t)
        _accumulate(dwq_out, _tn(cqn, dq0), first)
        _accumulate(dwk_out, _tn(kvn, dk0), first)
        _accumulate(dwv_out, _tn(kvn, dv0), first)
        dcq, dqn = vjp_q(_nt(dq0, wq_ref[...]))
        dckv, dkvn = vjp_kv(_nt(dk0, wk_ref[...]) + _nt(dv0, wv_ref[...]))
        _accumulate(dqn_out, dqn, first)
        _accumulate(dkvn_out, dkvn, first)
        dz = _bf(jnp.concatenate([_bf(dcq), _bf(dckv), _bf(dkpe), dxl_ref[...], dgate_ref[...]], axis=1))
        _accumulate(dwin_out, _tn(hn, dz), first)
        dx, dgn = vjp_in(_nt(dz, win_ref[...]))
        _accumulate(dgn_out, dgn, first)
        dx_out[...] = dx + dres_ref[...]

    return pl.pallas_call(
        body, name="ab_in_bwd", grid=(t // tm,),
        in_specs=[_row(tm, d), _row(tm, LANES), _const((1, d)), _const((d, Z_WIDTH)), _const((1, Q_LORA)), _const((Q_LORA, hp)),
                  _const((1, KV_LORA)), _const((KV_LORA, hp)), _const((KV_LORA, hp)),
                  _row(tm, hp), _row(tm, hp), _row(tm, hp), _row(tm, LRU_W), _row(tm, LRU_W), _row(tm, d)],
        out_specs=[_row(tm, d), _const((1, d)), _const((d, Z_WIDTH)), _const((1, Q_LORA)), _const((Q_LORA, hp)),
                   _const((1, KV_LORA)), _const((KV_LORA, hp)), _const((KV_LORA, hp))],
        out_shape=[_sds((t, d)), _sds((1, d)), _sds((d, Z_WIDTH)), _sds((1, Q_LORA)), _sds((Q_LORA, hp)),
                   _sds((1, KV_LORA)), _sds((KV_LORA, hp)), _sds((KV_LORA, hp))],
        compiler_params=_params(),
    )(x, posb, w['ab_norm'], w['W_in'], w['ab_q_norm'], w['Wq'], w['ab_kv_norm'], w['Wk'], w['Wv'], dq, dk, dv, dxl, dgate, dres)


def _attn_probs(q_blk, k_ext, tq):
    ext = k_ext.shape[0]
    s = lax.dot_general(q_blk, k_ext, (((1,), (1,)), ((), ())), preferred_element_type=F32) * ATTN_SCALE
    causal = lax.broadcasted_iota(jnp.int32, (tq, tq), 1) <= lax.broadcasted_iota(jnp.int32, (tq, tq), 0)
    diag = jnp.where(causal, s[:, ext - tq:], -1e30)
    s = diag if ext == tq else jnp.concatenate([s[:, :ext - tq], diag], axis=1)
    p = jnp.exp(s - jnp.max(s, axis=1, keepdims=True))
    return p * (1.0 / jnp.sum(p, axis=1, keepdims=True))


def _attn_fwd(q, k, v, tq):
    b, s, hp = q.shape
    blk = pl.BlockSpec((1, s, HEAD_PAD), lambda bi, h: (bi, 0, h))

    def body(q_ref, k_ref, v_ref, o_ref, p_ref):
        kb = _bf(k_ref[0])
        vb = _bf(v_ref[0])
        for i in range(s // tq):
            ext = (i + 1) * tq
            p = _bf(_attn_probs(_bf(q_ref[0, i * tq:ext, :]), kb[:ext], tq))
            p_ref[0, 0, i * tq:ext, :ext] = p
            o_ref[0, i * tq:ext, :] = _bf(lax.dot_general(p, vb[:ext], (((1,), (0,)), ((), ())), preferred_element_type=F32))

    return pl.pallas_call(body, name="attn_fwd", grid=(b, HEADS), in_specs=[blk, blk, blk],
                          out_specs=[blk, pl.BlockSpec((1, 1, s, s), lambda bi, h: (bi, h, 0, 0))],
                          out_shape=[_sds((b, s, hp), BF16), _sds((b, HEADS, s, s), BF16)], compiler_params=_params(2))(q, k, v)


def _attn_bwd(q, k, v, probs, do, tq):
    b, s, hp = q.shape
    blk = pl.BlockSpec((1, s, HEAD_PAD), lambda bi, h: (bi, 0, h))

    def body(q_ref, k_ref, v_ref, p_ref, do_ref, dq_ref, dk_ref, dv_ref):
        kb = _bf(k_ref[0])
        vb = _bf(v_ref[0])
        dk_ref[...] = jnp.zeros_like(dk_ref)
        dv_ref[...] = jnp.zeros_like(dv_ref)
        for i in range(s // tq):
            ext = (i + 1) * tq
            qb = _bf(q_ref[0, i * tq:ext, :])
            dob = _bf(do_ref[0, i * tq:ext, :])
            pb = p_ref[0, 0, i * tq:ext, :ext]
            p = pb.astype(F32)
            dv_ref[0, :ext, :] += lax.dot_general(pb, dob, (((0,), (0,)), ((), ())), preferred_element_type=F32)
            dp = lax.dot_general(dob, vb[:ext], (((1,), (1,)), ((), ())), preferred_element_type=F32)
            ds = _bf(p * (dp - jnp.sum(p * dp, axis=1, keepdims=True)) * ATTN_SCALE)
            dq_ref[0, i * tq:ext, :] = lax.dot_general(ds, kb[:ext], (((1,), (0,)), ((), ())), preferred_element_type=F32)
            dk_ref[0, :ext, :] += lax.dot_general(ds, qb, (((0,), (0,)), ((), ())), preferred_element_type=F32)

    return pl.pallas_call(body, name="attn_bwd", grid=(b, HEADS),
                          in_specs=[blk, blk, blk, pl.BlockSpec((1, 1, s, s), lambda bi, h: (bi, h, 0, 0)), blk], out_specs=[blk, blk, blk],
                          out_shape=[_sds((b, s, hp))] * 3, compiler_params=_params(2))(q, k, v, probs, do)


LRU_CONV = 4


def _lru_point(pre_a, pre_x, xc, lam):
    r = jax.nn.sigmoid(pre_a)
    i = jax.nn.sigmoid(pre_x)
    a, gap = _decay(-LRU_C * r * _softplus(-lam))
    return a, jnp.sqrt(gap) * (i * xc)


def _causal_conv(pad_ref, x, halo, first_in_seq, w, taps):
    tm = x.shape[0]
    pad_ref[:HALO, :] = jnp.where(first_in_seq, 0.0, halo)
    pad_ref[HALO:, :] = x
    y = w[taps - 1:taps, :] * x
    for k in range(taps - 1):
        off = HALO - (taps - 1) + k
        y = y + w[k:k + 1, :] * pad_ref[off:off + tm, :]
    return y


def _conv_taps(pad_ref, r, cols, taps):
    blocks = [pad_ref[r + j * HALO:r + (j + 1) * HALO, cols] for j in range(1 + STRIP // HALO)]
    sub = lax.broadcasted_iota(jnp.int32, blocks[0].shape, 0)
    out = []
    for k in range(taps - 1):
        s = taps - 1 - k
        rolled = [pltpu.roll(b, s, 0) for b in blocks]
        out.append(jnp.concatenate([jnp.where(sub < s, rolled[j], rolled[j + 1]) for j in range(STRIP // HALO)], axis=0))
    out.append(jnp.concatenate(blocks[1:], axis=0))
    return out


def _causal_conv_wgrad(pad_ref, dy, taps):
    tm = dy.shape[0]
    return jnp.concatenate([_colsum(dy * pad_ref[HALO - (taps - 1) + k:HALO - (taps - 1) + k + tm, :]) for k in range(taps)], axis=0)


def _causal_conv_transpose(pad_ref, dy, halo_next, last_in_seq, w, taps):
    tm = dy.shape[0]
    pad_ref[:tm, :] = dy
    pad_ref[tm:, :] = jnp.where(last_in_seq, 0.0, halo_next)
    dx = w[taps - 1:taps, :] * dy
    for k in range(taps - 1):
        off = (taps - 1) - k
        dx = dx + w[k:k + 1, :] * pad_ref[off:off + tm, :]
    return dx


def _lru_fwd(xl, gate, w, ts, seq):
    t, n = xl.shape
    tiles_per_seq = seq // ts

    def body(xl_ref, halo_ref, gate_ref, cw_ref, cb_ref, wa_ref, ba_ref, wx_ref, bx_ref, lam_ref, y_out, h_out, pad_ref, a_ref, b_ref, carry_ref):
        first_in_seq = pl.program_id(0) % tiles_per_seq == 0
        xc = _causal_conv(pad_ref, xl_ref[...], halo_ref[...], first_in_seq, cw_ref[...], LRU_CONV) + cb_ref[...]
        a, bx = _lru_point(_nn(xc, wa_ref[...]) + ba_ref[...], _nn(xc, wx_ref[...]) + bx_ref[...], xc, lam_ref[...])
        a_ref[...] = a
        b_ref[...] = bx

        @pl.when(first_in_seq)
        def _():
            carry_ref[...] = jnp.zeros_like(carry_ref)

        def step(r, h):
            h = a_ref[pl.ds(r, 1), :] * h + b_ref[pl.ds(r, 1), :]
            h_out[pl.ds(r, 1), :] = h
            return h

        carry_ref[...] = lax.fori_loop(0, ts, step, carry_ref[...], unroll=8)
        y_out[...] = _bf(h_out[...] * _gelu(gate_ref[...]))

    return pl.pallas_call(
        body, name="lru_fwd", grid=(t // ts,),
        in_specs=[_row(ts, n), _prev_halo(ts, n), _row(ts, n), _const((LRU_CONV, n)), _const((1, n)), _const((n, n)), _const((1, n)),
                  _const((n, n)), _const((1, n)), _const((1, n))],
        out_specs=[_row(ts, n), _row(ts, n)], out_shape=[_sds((t, n), BF16), _sds((t, n))],
        scratch_shapes=[pltpu.VMEM((HALO + ts, n), F32), pltpu.VMEM((ts, n), F32), pltpu.VMEM((ts, n), F32), pltpu.VMEM((1, n), F32)],
        compiler_params=_params(),
    )(xl, xl, gate, w['ab_conv_w'], w['ab_conv_b'], w['Wa'], w['ab_b_rg_a'], w['Wx'], w['ab_b_rg_x'], w['ab_lambda'])


def _lru_bwd(xl, gate, hs, dy, w, ts, seq):
    t, n = xl.shape
    tiles_per_seq = seq // ts
    n_tiles = t // ts

    def rev(i):
        return n_tiles - 1 - i

    row = pl.BlockSpec((ts, n), lambda i: (rev(i), 0))
    prev = pl.BlockSpec((HALO, n), lambda i: (jnp.maximum(rev(i) * (ts // HALO) - 1, 0), 0))
    acc = lambda shape: pl.BlockSpec(shape, lambda i: (0,) * len(shape))

    def body(xl_ref, xhalo_ref, gate_ref, h_ref, hhalo_ref, dy_ref, cw_ref, cb_ref, wa_ref, ba_ref, wx_ref, bx_ref, lam_ref,
             dxl_out, dgate_out, dcw_out, dcb_out, dwa_out, dba_out, dwx_out, dbx_out, dlam_out,
             pad_ref, padh_ref, padd_ref, a_ref, g_ref, carry_ref, dhalo_ref):
        step_id = pl.program_id(0)
        first = step_id == 0
        tile = rev(step_id)
        first_in_seq = tile % tiles_per_seq == 0
        last_in_seq = tile % tiles_per_seq == tiles_per_seq - 1
        cw = cw_ref[...]
        xc = _causal_conv(pad_ref, xl_ref[...], xhalo_ref[...], first_in_seq, cw, LRU_CONV) + cb_ref[...]
        pre_a = _nn(xc, wa_ref[...]) + ba_ref[...]
        pre_x = _nn(xc, wx_ref[...]) + bx_ref[...]
        (a, _), vjp_point = jax.vjp(_lru_point, pre_a, pre_x, xc, lam_ref[...])
        h = h_ref[...]
        _, vjp_out = jax.vjp(lambda h_, g_: h_ * _gelu(g_), h, gate_ref[...])
        dh, dgate = vjp_out(dy_ref[...])
        dgate_out[...] = _bf(dgate)
        a_ref[...] = a
        g_ref[...] = dh

        @pl.when(last_in_seq)
        def _():
            carry_ref[...] = jnp.zeros_like(carry_ref)

        def step(j, c):
            r = ts - 1 - j
            g = g_ref[pl.ds(r, 1), :] + c
            g_ref[pl.ds(r, 1), :] = g
            return a_ref[pl.ds(r, 1), :] * g

        carry_ref[...] = lax.fori_loop(0, ts, step, carry_ref[...], unroll=8)
        g = g_ref[...]
        padh_ref[:HALO, :] = jnp.where(first_in_seq, 0.0, hhalo_ref[...])
        padh_ref[HALO:, :] = h
        dpre_a, dpre_x, dxc, dlam = vjp_point((g * padh_ref[HALO - 1:HALO - 1 + ts, :], g))
        dxc = dxc + _nt(dpre_a, wa_ref[...]) + _nt(dpre_x, wx_ref[...])
        _accumulate(dwa_out, _tn(xc, dpre_a), first)
        _accumulate(dwx_out, _tn(xc, dpre_x), first)
        _accumulate(dba_out, _colsum(dpre_a), first)
        _accumulate(dbx_out, _colsum(dpre_x), first)
        _accumulate(dlam_out, dlam, first)
        _accumulate(dcb_out, _colsum(dxc), first)
        _accumulate(dcw_out, _causal_conv_wgrad(pad_ref, dxc, LRU_CONV), first)
        dxl_out[...] = _bf(_causal_conv_transpose(padd_ref, dxc, dhalo_ref[...], last_in_seq, cw, LRU_CONV))
        dhalo_ref[...] = dxc[:HALO, :]

    return pl.pallas_call(
        body, name="lru_bwd", grid=(n_tiles,),
        in_specs=[row, prev, row, row, prev, row, _const((LRU_CONV, n)), _const((1, n)), _const((n, n)), _const((1, n)),
                  _const((n, n)), _const((1, n)), _const((1, n))],
        out_specs=[row, row, acc((LRU_CONV, n)), acc((1, n)), acc((n, n)), acc((1, n)), acc((n, n)), acc((1, n)), acc((1, n))],
        out_shape=[_sds((t, n), BF16), _sds((t, n), BF16), _sds((LRU_CONV, n)), _sds((1, n)), _sds((n, n)), _sds((1, n)), _sds((n, n)),
                   _sds((1, n)), _sds((1, n))],
        scratch_shapes=[pltpu.VMEM((HALO + ts, n), F32), pltpu.VMEM((HALO + ts, n), F32), pltpu.VMEM((ts + HALO, n), F32),
                        pltpu.VMEM((ts, n), F32), pltpu.VMEM((ts, n), F32), pltpu.VMEM((1, n), F32), pltpu.VMEM((HALO, n), F32)],
        compiler_params=_params(),
    )(xl, xl, gate, hs, hs, dy, w['ab_conv_w'], w['ab_conv_b'], w['Wa'], w['ab_b_rg_a'], w['Wx'], w['ab_b_rg_x'], w['ab_lambda'])


def _ab_out_fwd(x, o, y, w, tm):
    t, d = x.shape
    hp = o.shape[1]

    def body(x_ref, o_ref, y_ref, wa_ref, wb_ref, h_out):
        h_out[...] = x_ref[...] + _nn(o_ref[...], wa_ref[...]) + _nn(y_ref[...], wb_ref[...])

    return pl.pallas_call(body, name="ab_out_fwd", grid=(t // tm,),
                          in_specs=[_row(tm, d), _row(tm, hp), _row(tm, LRU_W), _const((hp, d)), _const((LRU_W, d))],
                          out_specs=_row(tm, d), out_shape=_sds((t, d)), compiler_params=_params())(x, o, y, w['Wo_a'], w['Wo_b'])


def _ab_out_bwd(o, y, dh, w, tm):
    t, d = dh.shape
    hp = o.shape[1]

    def body(o_ref, y_ref, dh_ref, wa_ref, wb_ref, do_out, dy_out, dwa_out, dwb_out):
        first = pl.program_id(0) == 0
        dh_t = dh_ref[...]
        do_out[...] = _bf(_nt(dh_t, wa_ref[...]))
        dy_out[...] = _nt(dh_t, wb_ref[...])
        _accumulate(dwa_out, _tn(o_ref[...], dh_t), first)
        _accumulate(dwb_out, _tn(y_ref[...], dh_t), first)

    return pl.pallas_call(body, name="ab_out_bwd", grid=(t // tm,),
                          in_specs=[_row(tm, hp), _row(tm, LRU_W), _row(tm, d), _const((hp, d)), _const((LRU_W, d))],
                          out_specs=[_row(tm, hp), _row(tm, LRU_W), _const((hp, d)), _const((LRU_W, d))],
                          out_shape=[_sds((t, hp), BF16), _sds((t, LRU_W)), _sds((hp, d)), _sds((LRU_W, d))],
                          compiler_params=_params())(o, y, dh, w['Wo_a'], w['Wo_b'])


FFN_CONV = 3


def _ffn_a_fwd(h, norm, wg, wu, tm):
    t, d = h.shape
    fb = D_FF // FF_BLOCKS

    def body(h_ref, gn_ref, wg_ref, wu_ref, g_out, u_out, hn_out):
        hn = _bf(_rms(h_ref[...], gn_ref[...]))
        hn_out[0] = hn
        g_out[...] = _nt(hn, wg_ref[...])
        u_out[...] = _nt(hn, wu_ref[...])

    wspec = pl.BlockSpec((fb, d), lambda f, i: (f, 0))
    ospec = pl.BlockSpec((tm, fb), lambda f, i: (i, f))
    return pl.pallas_call(
        body, name="ffn_a_fwd", grid=(FF_BLOCKS, t // tm),
        in_specs=[pl.BlockSpec((tm, d), lambda f, i: (i, 0)), pl.BlockSpec((1, d), lambda f, i: (0, 0)), wspec, wspec],
        out_specs=[ospec, ospec, pl.BlockSpec((1, tm, d), lambda f, i: (f, i, 0))],
        out_shape=[_sds((t, D_FF)), _sds((t, D_FF)), _sds((FF_BLOCKS, t, d), BF16)], compiler_params=_params(2))(h, norm, wg, wu)


def _ffn_b_fwd(g, u, h, cw, cb, wd, tm, seq, final=None):
    t, d = h.shape
    tiles_per_seq = seq // tm

    def body(g_ref, halo_ref, u_ref, h_ref, cw_ref, cb_ref, wd_ref, *rest):
        pad_ref, act_ref = rest[-2:]
        pad_ref[:HALO, :] = jnp.where(pl.program_id(0) % tiles_per_seq == 0, 0.0, halo_ref[...])
        pad_ref[HALO:, :] = g_ref[...]
        cw = cw_ref[...]
        cb = cb_ref[...]
        for c0 in range(0, D_FF, STRIP_LANES):
            cols = slice(c0, min(c0 + STRIP_LANES, D_FF))
            for r in range(0, tm, STRIP):
                taps = _conv_taps(pad_ref, r, cols, FFN_CONV)
                gc = cb[:, cols] + cw[0:1, cols] * taps[0] + cw[1:2, cols] * taps[1] + cw[2:3, cols] * taps[2]
                act_ref[r:r + STRIP, cols] = _bf(_gelu(gc) * u_ref[r:r + STRIP, cols])
        h_new = h_ref[...] + _nn(act_ref[...], wd_ref[...])
        if final is None:
            rest[0][...] = h_new
        else:
            tgt_ref, fn_ref, dh_out, loss_out, dfn_out = rest[:5]
            first = pl.program_id(0) == 0
            loss, dh_out[...], dfn = _loss_and_grad(h_new, tgt_ref[...], fn_ref[...])
            _accumulate(loss_out, loss, first)
            _accumulate(dfn_out, dfn, first)

    in_specs = [_row(tm, D_FF), _prev_halo(tm, D_FF), _row(tm, D_FF), _row(tm, d), _const((FFN_CONV, D_FF)), _const((1, D_FF)), _const((D_FF, d))]
    scratch = [pltpu.VMEM((HALO + tm, D_FF), F32), pltpu.VMEM((tm, D_FF), BF16)]
    if final is None:
        return pl.pallas_call(body, name="ffn_b_fwd", grid=(t // tm,), in_specs=in_specs, out_specs=_row(tm, d), out_shape=_sds((t, d)),
                              scratch_shapes=scratch, compiler_params=_params())(g, g, u, h, cw, cb, wd)
    return pl.pallas_call(body, name="ffn_b_fwd_loss", grid=(t // tm,), in_specs=in_specs + [_row(tm, d), _const((1, d))],
                          out_specs=[_row(tm, d), _const((1, 1)), _const((1, d))],
                          out_shape=[_sds((t, d)), _sds((1, 1)), _sds((1, d))],
                          scratch_shapes=scratch, compiler_params=_params())(g, g, u, h, cw, cb, wd, *final)


def _ffn_b_bwd(g, u, dout, cw, cb, wd, tm, seq):
    t, d = dout.shape
    fb = D_FF // FF_BLOCKS
    tiles_per_seq = seq // tm

    def body(g_ref, halo_ref, u_ref, dout_ref, cw_ref, cb_ref, wd_ref, dgc_out, du_out, dwd_out, dcw_out, dcb_out,
             pad_ref, dact_ref, act_ref, acc_ref, dwd_acc):
        i = pl.program_id(1)
        first = i == 0
        pad_ref[:HALO, :] = jnp.where(i % tiles_per_seq == 0, 0.0, halo_ref[...])
        pad_ref[HALO:, :] = g_ref[...]
        dout_b = _bf(dout_ref[...])
        dact_ref[...] = _nt(dout_b, wd_ref[...])
        cw = cw_ref[...]
        cb = cb_ref[...]
        fold = lambda a: a[:HALO] + a[HALO:]
        for c0 in range(0, fb, STRIP_LANES):
            cols = slice(c0, min(c0 + STRIP_LANES, fb))
            sums = [jnp.zeros((HALO, cols.stop - c0), F32) for _ in range(1 + FFN_CONV)]
            for r in range(0, tm, STRIP):
                rows = slice(r, r + STRIP)
                taps = _conv_taps(pad_ref, r, cols, FFN_CONV)
                gelu, dgelu = _gelu_and_grad(cb[:, cols] + cw[0:1, cols] * taps[0] + cw[1:2, cols] * taps[1] + cw[2:3, cols] * taps[2])
                u = u_ref[rows, cols]
                dact = dact_ref[rows, cols]
                act_ref[rows, cols] = _bf(gelu * u)
                du_out[rows, cols] = _bf(dact * gelu)
                dgc = dact * u * dgelu
                dgc_out[rows, cols] = dgc
                sums = [sums[0] + fold(dgc)] + [sums[1 + k] + fold(dgc * taps[k]) for k in range(FFN_CONV)]
            for k in range(1 + FFN_CONV):
                acc_ref[k, :, cols] = sums[k]
        _accumulate(dwd_acc, _tn(act_ref[...], dout_b), first)

        @pl.when(i == t // tm - 1)
        def _():
            dwd_out[...] = _bf(dwd_acc[...])

        _accumulate(dcb_out, _colsum(acc_ref[0]), first)
        _accumulate(dcw_out, jnp.concatenate([_colsum(acc_ref[1 + k]) for k in range(FFN_CONV)], axis=0), first)

    blk = pl.BlockSpec((tm, fb), lambda f, i: (i, f))
    halo = pl.BlockSpec((HALO, fb), lambda f, i: (jnp.maximum(i * (tm // HALO) - 1, 0), f))
    wd_blk = pl.BlockSpec((fb, d), lambda f, i: (f, 0), pipeline_mode=pl.Buffered(1))
    return pl.pallas_call(
        body, name="ffn_b_bwd", grid=(FF_BLOCKS, t // tm),
        in_specs=[blk, halo, blk, pl.BlockSpec((tm, d), lambda f, i: (i, 0)), pl.BlockSpec((FFN_CONV, fb), lambda f, i: (0, f)),
                  pl.BlockSpec((1, fb), lambda f, i: (0, f)), wd_blk],
        out_specs=[blk, blk, wd_blk, pl.BlockSpec((FFN_CONV, fb), lambda f, i: (0, f)),
                   pl.BlockSpec((1, fb), lambda f, i: (0, f))],
        out_shape=[_sds((t, D_FF)), _sds((t, D_FF), BF16), _sds((D_FF, d), BF16), _sds((FFN_CONV, D_FF)), _sds((1, D_FF))],
        scratch_shapes=[pltpu.VMEM((HALO + tm, fb), F32), pltpu.VMEM((tm, fb), F32), pltpu.VMEM((tm, fb), BF16),
                        pltpu.VMEM((1 + FFN_CONV, HALO, fb), F32), pltpu.VMEM((fb, d), F32)],
        compiler_params=_params(2))(g, g, u, dout, cw, cb, wd)


def _ffn_a_dgrad(h, norm, dgc, du, dres, cw, wg, wu, tm, seq):
    t, d = h.shape
    tiles_per_seq = seq // tm
    n_tiles = t // tm

    def body(h_ref, gn_ref, dgc_ref, halo_ref, du_ref, dres_ref, cw_ref, wg_ref, wu_ref, dh_out, dg_out, dgn_out, pad_ref):
        i = pl.program_id(0)
        last_in_seq = i % tiles_per_seq == tiles_per_seq - 1
        dg = _bf(_causal_conv_transpose(pad_ref, dgc_ref[...], halo_ref[...], last_in_seq, cw_ref[...], FFN_CONV))
        dg_out[...] = dg
        _, vjp_norm = jax.vjp(_rms, h_ref[...], gn_ref[...])
        dh, dgn = vjp_norm(_nn(dg, wg_ref[...]) + _nn(du_ref[...], wu_ref[...]))
        dh_out[...] = dh + dres_ref[...]
        _accumulate(dgn_out, dgn, i == 0)

    return pl.pallas_call(
        body, name="ffn_a_dgrad", grid=(n_tiles,),
        in_specs=[_row(tm, d), _const((1, d)), _row(tm, D_FF), _next_halo(tm, D_FF, n_tiles), _row(tm, D_FF), _row(tm, d),
                  _const((FFN_CONV, D_FF)), _const((D_FF, d)), _const((D_FF, d))],
        out_specs=[_row(tm, d), _row(tm, D_FF), _const((1, d))], out_shape=[_sds((t, d)), _sds((t, D_FF), BF16), _sds((1, d))],
        scratch_shapes=[pltpu.VMEM((tm + HALO, D_FF), F32)], compiler_params=_params())(h, norm, dgc, dgc, du, dres, cw, wg, wu)


def _ffn_a_wgrad(hn, dg, du, tm):
    _, t, d = hn.shape
    fb = D_FF // FF_BLOCKS

    n_tiles = t // tm

    def body(hn_ref, dg_ref, du_ref, dwg_out, dwu_out, acc_g, acc_u):
        i = pl.program_id(1)
        hn_t = hn_ref[0]
        _accumulate(acc_g, _tn(dg_ref[...], hn_t), i == 0)
        _accumulate(acc_u, _tn(du_ref[...], hn_t), i == 0)

        @pl.when(i == n_tiles - 1)
        def _():
            dwg_out[...] = _bf(acc_g[...])
            dwu_out[...] = _bf(acc_u[...])

    blk = pl.BlockSpec((tm, fb), lambda f, i: (i, f))
    wspec = pl.BlockSpec((fb, d), lambda f, i: (f, 0), pipeline_mode=pl.Buffered(1))
    return pl.pallas_call(body, name="ffn_a_wgrad", grid=(FF_BLOCKS, n_tiles),
                          in_specs=[pl.BlockSpec((1, tm, d), lambda f, i: (0, i, 0)), blk, blk],
                          out_specs=[wspec, wspec], out_shape=[_sds((D_FF, d), BF16), _sds((D_FF, d), BF16)],
                          scratch_shapes=[pltpu.VMEM((fb, d), F32), pltpu.VMEM((fb, d), F32)],
                          compiler_params=_params(2))(hn, dg, du)


def _sgu_mix(vn, ws_ref, bst):
    tril = lax.broadcasted_iota(jnp.int32, (CHUNK, CHUNK), 0) >= lax.broadcasted_iota(jnp.int32, (CHUNK, CHUNK), 1)
    wms = [jnp.where(tril, ws_ref[g], 0.0) for g in range(SGU_GROUPS)]
    chunks = []
    for n in range(vn.shape[0] // CHUNK):
        vc = vn[n * CHUNK:(n + 1) * CHUNK, :]
        chunks.append(jnp.concatenate(
            [_nn(wms[g], vc[:, g * CHUNK:(g + 1) * CHUNK]) + bst[:, g:g + 1] for g in range(SGU_GROUPS)], axis=1))
    return jnp.concatenate(chunks, axis=0)


def _sgu_fwd(h, w, tm):
    t, d = h.shape

    def body(h_ref, cn_ref, win_ref, lg_ref, lb_ref, ws_ref, bst_ref, wout_ref, h_out):
        h_t = h_ref[...]
        z = _gelu(_nn(_rms(h_t, cn_ref[...]), win_ref[...]))
        vn = _layer_norm(z[:, d:], lg_ref[...], lb_ref[...])
        s = _sgu_mix(vn, ws_ref, bst_ref[...])
        h_out[...] = h_t + _nn(z[:, :d] * s, wout_ref[...])

    return pl.pallas_call(
        body, name="sgu_fwd", grid=(t // tm,),
        in_specs=[_row(tm, d), _const((1, d)), _const((d, 2 * d)), _const((1, d)), _const((1, d)), _const((SGU_GROUPS, CHUNK, CHUNK)),
                  _const((CHUNK, LANES)), _const((d, d))],
        out_specs=_row(tm, d), out_shape=_sds((t, d)), compiler_params=_params(),
    )(h, w['c_norm'], w['c_w_in'], w['c_ln_g'], w['c_ln_b'], w['c_w_s'], w['bsT'], w['c_w_out'])


def _sgu_bwd(h, dout, w, tm, sub):
    t, d = h.shape

    def body(h_ref, dout_ref, cn_ref, win_ref, lg_ref, lb_ref, ws_ref, bst_ref, wout_ref,
             dh_out, dcn_out, dwin_out, dlg_out, dlb_out, dws_out, dbst_out, dwout_out, hn_ref, us_ref, dz_ref):
        first = pl.program_id(0) == 0
        tril = lax.broadcasted_iota(jnp.int32, (CHUNK, CHUNK), 0) >= lax.broadcasted_iota(jnp.int32, (CHUNK, CHUNK), 1)
        lane = lax.broadcasted_iota(jnp.int32, (CHUNK, LANES), 1)
        dws = [jnp.zeros((CHUNK, CHUNK), F32) for _ in range(SGU_GROUPS)]
        dbst = jnp.zeros((CHUNK, LANES), F32)
        dlg = dlb = dcn = 0.0
        for r in range(0, tm, sub):
            rows = slice(r, r + sub)
            hn, vjp_norm = jax.vjp(_rms, h_ref[rows, :], cn_ref[...])
            zpre = _nn(hn, win_ref[...])
            u, vjp_u = jax.vjp(_gelu, zpre[:, :d])
            vn, vjp_v = jax.vjp(lambda zp, lg, lb: _layer_norm(_gelu(zp), lg, lb), zpre[:, d:], lg_ref[...], lb_ref[...])
            s = _sgu_mix(vn, ws_ref, bst_ref[...])
            dout_t = dout_ref[rows, :]
            dus = _nt(dout_t, wout_ref[...])
            hn_ref[rows, :] = _bf(hn)
            us_ref[rows, :] = _bf(u * s)
            ds = dus * u
            dvn_chunks = []
            for n in range(sub // CHUNK):
                cols = []
                for g in range(SGU_GROUPS):
                    ds_ng = ds[n * CHUNK:(n + 1) * CHUNK, g * CHUNK:(g + 1) * CHUNK]
                    vc_ng = vn[n * CHUNK:(n + 1) * CHUNK, g * CHUNK:(g + 1) * CHUNK]
                    cols.append(_tn(jnp.where(tril, ws_ref[g], 0.0), ds_ng))
                    dws[g] = dws[g] + _nt(ds_ng, vc_ng)
                    dbst = dbst + jnp.where(lane == g, jnp.sum(ds_ng, axis=1, keepdims=True), 0.0)
                dvn_chunks.append(jnp.concatenate(cols, axis=1))
            dvn = jnp.concatenate(dvn_chunks, axis=0)
            (dzu,) = vjp_u(dus * s)
            dzv, dlg_r, dlb_r = vjp_v(dvn)
            dzpre = jnp.concatenate([dzu, dzv], axis=1)
            dz_ref[rows, :] = _bf(dzpre)
            dh, dcn_r = vjp_norm(_nt(dzpre, win_ref[...]))
            dh_out[rows, :] = dh + dout_t
            dlg, dlb, dcn = dlg + dlg_r, dlb + dlb_r, dcn + dcn_r
        _accumulate(dwout_out, _tn(us_ref[...], dout_ref[...]), first)
        _accumulate(dwin_out, _tn(hn_ref[...], dz_ref[...]), first)
        for g in range(SGU_GROUPS):
            val = jnp.where(tril, dws[g], 0.0)

            @pl.when(first)
            def _():
                dws_out[g] = val

            @pl.when(jnp.logical_not(first))
            def _():
                dws_out[g] += val
        _accumulate(dbst_out, dbst, first)
        _accumulate(dlg_out, dlg, first)
        _accumulate(dlb_out, dlb, first)
        _accumulate(dcn_out, dcn, first)

    return pl.pallas_call(
        body, name="sgu_bwd", grid=(t // tm,),
        in_specs=[_row(tm, d), _row(tm, d), _const((1, d)), _const((d, 2 * d)), _const((1, d)), _const((1, d)),
                  _const((SGU_GROUPS, CHUNK, CHUNK)), _const((CHUNK, LANES)), _const((d, d))],
        out_specs=[_row(tm, d), _const((1, d)), _const((d, 2 * d)), _const((1, d)), _const((1, d)), _const((SGU_GROUPS, CHUNK, CHUNK)),
                   _const((CHUNK, LANES)), _const((d, d))],
        out_shape=[_sds((t, d)), _sds((1, d)), _sds((d, 2 * d)), _sds((1, d)), _sds((1, d)), _sds((SGU_GROUPS, CHUNK, CHUNK)),
                   _sds((CHUNK, LANES)), _sds((d, d))],
        scratch_shapes=[pltpu.VMEM((tm, d), BF16), pltpu.VMEM((tm, d), BF16), pltpu.VMEM((tm, 2 * d), BF16)],
        compiler_params=_params(),
    )(h, dout, w['c_norm'], w['c_w_in'], w['c_ln_g'], w['c_ln_b'], w['c_w_s'], w['bsT'], w['c_w_out'])


def _loss_and_grad(h, tgt, g):
    def loss_fn(h_, g_):
        err = _rms(h_, g_) - tgt
        return 0.5 * jnp.sum(jnp.mean(err * err, axis=-1, keepdims=True), axis=0, keepdims=True)

    loss, vjp_loss = jax.vjp(loss_fn, h, g)
    return (loss,) + vjp_loss(jnp.ones((1, 1), F32))


def _tile(t, seq, want):
    tm = min(want, seq)
    assert seq % tm == 0 and t % tm == 0 and tm % CHUNK == 0
    return tm


def _local_step(x, posb, target, w, seq, late_weights, on_grads):
    t, d = x.shape
    b = t // seq
    hp = HEADS * HEAD_PAD
    tm_wide, tm_big, tm_mid = _tile(t, seq, 1024), _tile(t, seq, 512), _tile(t, seq, 256)
    tq = _tile(t, seq, 512)

    q, k, v, xl, gate = _ab_in_fwd(x, posb, w, tm_big)
    o, probs = _attn_fwd(q.reshape(b, seq, hp), k.reshape(b, seq, hp), v.reshape(b, seq, hp), tq)
    o = o.reshape(t, hp)
    y, hs = _lru_fwd(xl, gate, w, tm_big, seq)
    w = {**w, **late_weights('out0', y)}
    h1 = _ab_out_fwd(x, o, y, w, tm_wide)
    hcur = h1
    saved = []
    for l in range(2):
        if l == 1:
            w = {**w, **late_weights('mix1', hcur)}
            saved_h2 = hcur
            hcur = _sgu_fwd(hcur, w, tm_mid)
        wl = late_weights('ffn%d' % l, hcur)
        g, u, hn = _ffn_a_fwd(hcur, w['ffn_norm'][l], wl['Wg'], wl['Wu'], tm_wide)
        saved.append((hcur, g, u, wl, hn))
        ffn_b = (g, u, hcur, w['ffn_conv_w'][l], w['ffn_conv_b'][l], wl['Wd'], tm_big, seq)
        if l == 0:
            hcur = _ffn_b_fwd(*ffn_b)
    dh, loss, d_final = _ffn_b_fwd(*ffn_b, final=(target, w['final_norm']))

    ffn = {}
    conv_b = list(w['ffn_conv_b'])
    for l in (1, 0):
        hin, g, u, wl, hn = saved[l]
        dgc, du, d_wd, d_cw, d_cb = _ffn_b_bwd(g, u, dh, w['ffn_conv_w'][l], conv_b[l], wl['Wd'], tm_big, seq)
        dh, dg, d_norm = _ffn_a_dgrad(hin, w['ffn_norm'][l], dgc, du, dh, w['ffn_conv_w'][l], wl['Wg'], wl['Wu'], tm_mid, seq)
        d_wg, d_wu = _ffn_a_wgrad(hn, dg, du, tm_wide)
        ffn[l] = dict(ffn_norm=d_norm, ffn_conv_w=d_cw, ffn_conv_b=d_cb, Wg=d_wg, Wu=d_wu, Wd=d_wd)
        if l == 1:
            dh, d_cn, d_cwin, d_lg, d_lb, d_ws, d_bst, d_cwout = _sgu_bwd(saved_h2, dh, w, tm_big, tm_mid)
            zero = on_grads('late1', dict(final_norm=d_final, c_norm=d_cn, c_ln_g=d_lg, c_ln_b=d_lb, c_w_s=d_ws, bsT=d_bst, c_w_in=d_cwin,
                                          c_w_out=d_cwout, Wg=[d_wg], Wu=[d_wu], Wd=[d_wd]))
            conv_b[0] = conv_b[0] + zero
    late0 = {name: [ffn[0][name], ffn[1][name]] for name in ('ffn_norm', 'ffn_conv_w', 'ffn_conv_b')}
    zero = on_grads('late0', dict(late0, Wg=[ffn[0]['Wg']], Wu=[ffn[0]['Wu']], Wd=[ffn[0]['Wd']]))
    w = {**w, 'Wo_b': w['Wo_b'] + zero.astype(w['Wo_b'].dtype)}
    do, dy, d_woa, d_wob = _ab_out_bwd(o, y, dh, w, tm_wide)
    dxl, dgate, d_cw, d_cb, d_wa, d_ba, d_wx, d_bx, d_lam = _lru_bwd(xl, gate, hs, dy, w, tm_big, seq)
    zero = on_grads('mid', dict(Wo_a=d_woa, Wo_b=d_wob, ab_conv_w=d_cw, ab_conv_b=d_cb, Wa=d_wa, ab_b_rg_a=d_ba, Wx=d_wx,
                                ab_b_rg_x=d_bx, ab_lambda=d_lam))
    w = {**w, 'ab_norm': w['ab_norm'] + zero}
    dq, dk, dv = _attn_bwd(q.reshape(b, seq, hp), k.reshape(b, seq, hp), v.reshape(b, seq, hp), probs, do.reshape(b, seq, hp), tq)
    dx, d_gn, d_win, d_qn, d_wq, d_kvn, d_wk, d_wv = _ab_in_bwd(
        x, posb, w, dq.reshape(t, hp), dk.reshape(t, hp), dv.reshape(t, hp), dxl, dgate, dh, tm_big)
    return loss, dx, dict(ab_norm=d_gn, W_in=d_win, ab_q_norm=d_qn, Wq=d_wq, ab_kv_norm=d_kvn, Wk=d_wk, Wv=d_wv)


def _block_diag(wg):
    g, n, _ = wg.shape
    return jnp.einsum('gij,gh->gihj', wg, jnp.eye(g, dtype=wg.dtype)).reshape(g * n, g * n)


def _prepare_out(w_out):
    d = w_out.shape[2]
    mla = HEADS * QK_NOPE
    return {'Wo_a': jnp.pad(w_out[0, :mla].reshape(HEADS, QK_NOPE, d), ((0, 0), (0, HEAD_PAD - QK_NOPE), (0, 0))).reshape(HEADS * HEAD_PAD, d),
            'Wo_b': w_out[0, mla:]}


def _prepare(full):
    d = full['ab_w_in'].shape[1]
    w_in = full['ab_w_in'][0]
    zeros = lambda n: jnp.zeros((d, n), w_in.dtype)
    wq = full['ab_w_q_b'][0].reshape(Q_LORA, HEADS, QK_NOPE + QK_ROPE)
    wkv = full['ab_w_kv_b'][0].reshape(KV_LORA, HEADS, 2 * QK_NOPE)
    pad_head = lambda a: jnp.pad(a, ((0, 0), (0, 0), (0, HEAD_PAD - a.shape[2]))).reshape(a.shape[0], HEADS * HEAD_PAD)
    w = {
        'W_in': jnp.concatenate([w_in[:, :Z_KPE], zeros(QK_NOPE), w_in[:, Z_KPE:Z_KPE + QK_ROPE],
                                 zeros(HEAD_PAD - QK_NOPE - QK_ROPE), w_in[:, Z_KPE + QK_ROPE:]], axis=1),
        'Wq': pad_head(wq), 'Wk': pad_head(wkv[:, :, :QK_NOPE]), 'Wv': pad_head(wkv[:, :, QK_NOPE:]),
        'Wa': _bf(_block_diag(full['ab_w_rg_a'][0])), 'Wx': _bf(_block_diag(full['ab_w_rg_x'][0])),
        'c_w_s': full['c_w_s'][0],
        'bsT': jnp.pad(full['c_b_s'][0].T, ((0, 0), (0, LANES - SGU_GROUPS))),
        'ffn_norm': [full['ffn_norm'][l:l + 1] for l in range(2)], 'ffn_conv_w': [full['ffn_conv_w'][l] for l in range(2)],
        'ffn_conv_b': [full['ffn_conv_b'][l:l + 1] for l in range(2)],
        'ab_conv_w': full['ab_conv_w'][0], 'final_norm': full['final_norm'][None, :],
    }
    for name in ('ab_norm', 'ab_q_norm', 'ab_kv_norm', 'ab_conv_b', 'ab_b_rg_a', 'ab_b_rg_x', 'ab_lambda', 'c_norm', 'c_ln_g', 'c_ln_b'):
        w[name] = full[name]
    return w


def _unprepare(g):
    unpad_head = lambda a, n: a.reshape(a.shape[0], HEADS, HEAD_PAD)[:, :, :n]
    diag = lambda a: jnp.einsum('gigj->gij', a.reshape(HEADS, LRU_W // HEADS, HEADS, LRU_W // HEADS))
    rules = {
        'ab_w_in': (('W_in',), lambda a: jnp.concatenate([a[:, :Z_KPE], a[:, Z_KPE + QK_NOPE:Z_KPE + QK_NOPE + QK_ROPE], a[:, Z_LRU:]], axis=1)[None]),
        'ab_w_q_b': (('Wq',), lambda a: unpad_head(a, QK_NOPE + QK_ROPE).reshape(1, Q_LORA, -1)),
        'ab_w_kv_b': (('Wk', 'Wv'), lambda a, b: jnp.concatenate([unpad_head(a, QK_NOPE), unpad_head(b, QK_NOPE)], axis=2).reshape(1, KV_LORA, -1)),
        'ab_w_out': (('Wo_a', 'Wo_b'), lambda a, b: jnp.concatenate(
            [a.reshape(HEADS, HEAD_PAD, -1)[:, :QK_NOPE].reshape(HEADS * QK_NOPE, -1), b], axis=0)[None]),
        'ab_w_rg_a': (('Wa',), lambda a: diag(a)[None]), 'ab_w_rg_x': (('Wx',), lambda a: diag(a)[None]),
        'c_w_in': (('c_w_in',), lambda a: a[None]), 'c_w_out': (('c_w_out',), lambda a: a[None]), 'c_w_s': (('c_w_s',), lambda a: a[None]),
        'c_b_s': (('bsT',), lambda a: a[:, :SGU_GROUPS].T[None]),
        'ffn_w_gate': (('Wg',), jnp.stack), 'ffn_w_up': (('Wu',), jnp.stack), 'ffn_w_down': (('Wd',), jnp.stack),
        'ffn_norm': (('ffn_norm',), lambda a: jnp.concatenate(a, axis=0)), 'ffn_conv_w': (('ffn_conv_w',), jnp.stack),
        'ffn_conv_b': (('ffn_conv_b',), lambda a: jnp.concatenate(a, axis=0)),
        'ab_conv_w': (('ab_conv_w',), lambda a: a[None]), 'final_norm': (('final_norm',), lambda a: a[0]),
    }
    for name in ('ab_norm', 'ab_q_norm', 'ab_kv_norm', 'ab_conv_b', 'ab_b_rg_a', 'ab_b_rg_x', 'ab_lambda', 'c_norm', 'c_ln_g', 'c_ln_b'):
        rules[name] = ((name,), lambda a: a)
    return {name: fn(*[g[k] for k in keys]) for name, (keys, fn) in rules.items() if all(k in g for k in keys)}


SLAB_ROWS = 16


def _round_up(n, m):
    return -(-n // m) * m


def _to_chunks(full, axis):
    s = full.shape
    return jnp.moveaxis(full.reshape(s[:axis] + (N_DEV, s[axis] // N_DEV) + s[axis + 1:]), axis, 0)


def _from_chunks(chunks, axis):
    local = chunks.shape[1:]
    return jnp.moveaxis(chunks, 0, axis).reshape(local[:axis] + (N_DEV * local[axis],) + local[axis + 1:])


def _merge_columns(landed, name):
    _, _, r, n = landed.shape
    tr = r // 4

    def body(l_ref, o_ref):
        o_ref[0] = jnp.concatenate([l_ref[dev, 0] for dev in range(N_DEV)], axis=1)

    return pl.pallas_call(body, name="merge_" + name, grid=(r // tr,),
                          in_specs=[pl.BlockSpec((N_DEV, 1, tr, n), lambda i: (0, 0, i, 0))],
                          out_specs=pl.BlockSpec((1, tr, N_DEV * n), lambda i: (0, i, 0)),
                          out_shape=jax.ShapeDtypeStruct((1, r, N_DEV * n), landed.dtype), compiler_params=_params())(landed)


def _split_chunks(whole, axis, name):
    _, rows, cols = whole.shape
    if axis == 1:
        r = rows // N_DEV

        def body(x_ref, o_ref):
            o_ref[0] = _bf(x_ref[...])

        grid, out_shape = (N_DEV,), (N_DEV, 1, r, cols)
        spec, out_spec = pl.BlockSpec((1, r, cols), lambda dev: (0, dev, 0)), pl.BlockSpec((1, 1, r, cols), lambda dev: (dev, 0, 0, 0))
    else:
        n, tr = cols // N_DEV, rows // 4

        def body(x_ref, o_ref):
            x = x_ref[0]
            for dev in range(N_DEV):
                o_ref[dev, 0] = _bf(x[:, dev * n:(dev + 1) * n])

        grid, out_shape = (rows // tr,), (N_DEV, 1, rows, n)
        spec, out_spec = pl.BlockSpec((1, tr, cols), lambda i: (0, i, 0)), pl.BlockSpec((N_DEV, 1, tr, n), lambda i: (0, 0, i, 0))
    return pl.pallas_call(body, name="split_" + name, grid=grid, in_specs=[spec], out_specs=out_spec,
                          out_shape=jax.ShapeDtypeStruct(out_shape, BF16), compiler_params=_params())(whole)


def _slab_rows(n):
    return _round_up(-(-n // LANES), SLAB_ROWS)


def _to_slab(a, lead):
    a = a.reshape(lead + (-1,))
    rows = _slab_rows(a.shape[-1])
    a = jnp.pad(a, [(0, 0)] * len(lead) + [(0, rows * LANES - a.shape[-1])])
    return a.reshape(lead + (rows, LANES))


def _pack_slabs(parts, lead):
    return jnp.concatenate([_to_slab(p, lead) for p in parts], axis=len(lead))


def _unpack_slabs(packed, shapes):
    lead = packed.shape[:-2]
    out, row = [], 0
    for shape in shapes:
        size = math.prod(shape)
        rows = _slab_rows(size)
        piece = lax.slice_in_dim(packed, row, row + rows, axis=len(lead))
        out.append(piece.reshape(lead + (rows * LANES,))[..., :size].reshape(lead + tuple(shape)))
        row += rows
    return out


HBM = pl.BlockSpec(memory_space=pl.ANY)


def _other_chips(x, y):
    return [(1 - x, y), (x, 1 - y), (1 - x, 1 - y)]


def _all_gather(blocks):
    n = len(blocks)

    def body(*refs):
        x_refs, out_refs, token = refs[:n], refs[n:2 * n], refs[2 * n]
        send_sems, recv_sems, local_sems = refs[2 * n + 1:]
        token[...] = jnp.zeros_like(token)
        x, y, c = lax.axis_index("x"), lax.axis_index("y"), lax.axis_index("c")
        me, sibling = (x, y, c), (x, y, 1 - c)
        chips = _other_chips(x, y)

        def slab(a, px, py, pc):
            return out_refs[a].at[4 * px + 2 * py + pc]

        def copy(a, k, blk, to, src=None):
            return pltpu.make_async_remote_copy(src_ref=slab(a, *blk) if src is None else src, dst_ref=slab(a, *blk),
                                                send_sem=send_sems.at[7 * a + k], recv_sem=recv_sems.at[7 * a + k],
                                                device_id=to, device_id_type=MESH)

        mine = [pltpu.make_async_copy(x_refs[a], slab(a, *me), local_sems.at[a]) for a in range(n)]
        started = []
        for a in range(n):
            mine[a].start()
            started.append(copy(a, 0, me, sibling, src=x_refs[a]))
            started += [copy(a, 1 + j, me, (*chip, c), src=x_refs[a]) for j, chip in enumerate(chips)]
        for cp in started:
            cp.start()
        for j, chip in enumerate(chips):
            for a in range(n):
                copy(a, 1 + j, (*chip, c), me).wait_recv()
                passed = copy(a, 4 + j, (*chip, c), sibling)
                passed.start()
                started.append(passed)
        for a in range(n):
            copy(a, 0, sibling, me).wait_recv()
        for j, chip in enumerate(chips):
            for a in range(n):
                copy(a, 4 + j, (*chip, 1 - c), me).wait_recv()
        for cp in started:
            cp.wait_send()
        for a in range(n):
            mine[a].wait()

    out = pl.pallas_call(
        body, name="all_gather_weights",
        out_shape=[jax.ShapeDtypeStruct((N_DEV,) + b.shape, b.dtype) for b in blocks] + [jax.ShapeDtypeStruct((8, LANES), F32)],
        in_specs=[HBM] * n, out_specs=[HBM] * n + [pl.BlockSpec(memory_space=pltpu.VMEM)],
        scratch_shapes=[pltpu.SemaphoreType.DMA((7 * n,)), pltpu.SemaphoreType.DMA((7 * n,)), pltpu.SemaphoreType.DMA((n,))],
    )(*blocks)
    return list(out[:n]), out[n][0, 0]


FLIPS = [(0, 0, 1), (1, 0, 0), (1, 0, 1), (0, 1, 0), (0, 1, 1), (1, 1, 0), (1, 1, 1)]


def _peers(x, y, c):
    flip = lambda v, f: 1 - v if f else v
    return [(flip(x, fx), flip(y, fy), flip(c, fc)) for fx, fy, fc in FLIPS]


def _direct_copies(src_refs, land_refs, send_sems, recv_sems, scatter):
    x, y, c = lax.axis_index("x"), lax.axis_index("y"), lax.axis_index("c")
    me = 4 * x + 2 * y + c
    starts, waits = [], []
    for a in range(len(src_refs)):
        for k, (px, py, pc) in enumerate(_peers(x, y, c)):
            peer = 4 * px + 2 * py + pc
            sems = dict(send_sem=send_sems.at[7 * a + k], recv_sem=recv_sems.at[7 * a + k], device_id=(px, py, pc), device_id_type=MESH)
            src = src_refs[a].at[peer] if scatter else src_refs[a]
            starts.append(pltpu.make_async_remote_copy(src_ref=src, dst_ref=land_refs[a].at[me], **sems))
            waits.append(pltpu.make_async_remote_copy(src_ref=src, dst_ref=land_refs[a].at[peer], **sems))
    n = len(src_refs)
    keeps = [] if scatter else [pltpu.make_async_copy(src_refs[a], land_refs[a].at[me], send_sems.at[7 * n + a]) for a in range(n)]
    return starts, waits, keeps


def _landing(src, scatter):
    block = src.shape[1:] if scatter else src.shape
    return jax.ShapeDtypeStruct((N_DEV,) + block, src.dtype)


HBM_SPACE = pl.BlockSpec(memory_space=pltpu.HBM)
SEMAPHORES = pl.BlockSpec(memory_space=pltpu.SEMAPHORE)
SPLIT_EFFECT = pltpu.SideEffectType.DATAFLOW_SIDE_EFFECTING


def _start_exchange(name, srcs, scatter):
    n = len(srcs)
    lands = [lax.empty(s.shape, s.dtype) for s in (_landing(s, scatter) for s in srcs)]

    def body(*refs):
        starts, _, keeps = _direct_copies(refs[:n], refs[n:2 * n], refs[2 * n], refs[2 * n + 1], scatter)
        for cp in starts + keeps:
            cp.start()
        refs[-1][...] = jnp.zeros_like(refs[-1])

    held = [pltpu.with_memory_space_constraint(a, pltpu.HBM) for a in list(srcs) + lands]
    out = pl.pallas_call(
        body, name=name + "_start",
        out_shape=(pltpu.SemaphoreType.DMA(((7 if scatter else 8) * n,)), pltpu.SemaphoreType.DMA((7 * n,)),
                   *[pltpu.HBM(a.shape, a.dtype) for a in held],
                   jax.ShapeDtypeStruct((8, LANES), F32)),
        in_specs=[HBM_SPACE] * (2 * n), out_specs=(SEMAPHORES, SEMAPHORES, *[HBM_SPACE] * (2 * n), pl.BlockSpec(memory_space=pltpu.VMEM)),
        input_output_aliases={i: 2 + i for i in range(2 * n)},
        compiler_params=pltpu.CompilerParams(has_side_effects=SPLIT_EFFECT),
    )(*held)
    return out[0], out[1], list(out[2:2 + n]), list(out[2 + n:2 + 2 * n]), out[-1][0, 0], out[-1]


def _wait_exchange(name, started, after, scatter):
    send_sems, recv_sems, srcs, lands = started[:4]
    n = len(srcs)

    def body(*refs):
        _, waits, keeps = _direct_copies(refs[:n], refs[n:2 * n], refs[2 * n], refs[2 * n + 1], scatter)
        for cp in waits:
            cp.wait_send()
        for cp in waits:
            cp.wait_recv()
        for cp in keeps:
            cp.wait()

    out = pl.pallas_call(
        body, name=name + "_wait", out_shape=tuple(pltpu.HBM(a.shape, a.dtype) for a in srcs + lands),
        in_specs=[HBM_SPACE] * (2 * n) + [SEMAPHORES, SEMAPHORES, HBM], out_specs=tuple([HBM_SPACE] * (2 * n)),
        input_output_aliases={i: i for i in range(2 * n)},
        compiler_params=pltpu.CompilerParams(has_side_effects=SPLIT_EFFECT),
    )(*srcs, *lands, send_sems, recv_sems, after)
    return list(out[:n]), list(out[n:])


def _row_tile(rows):
    return rows // 2 if (rows // 2) % SLAB_ROWS == 0 else rows


def _sum_in_device_order(me_ref, l_ref, own_ref):
    mine = own_ref[0].astype(F32)
    g = jnp.where(me_ref[0] == 0, mine, l_ref[0].astype(F32))
    for dev in range(1, N_DEV):
        g = g + jnp.where(me_ref[0] == dev, mine, l_ref[dev].astype(F32))
    return g


def _adamw(g, w, m, v):
    m_new = ADAM_B1 * m + (1.0 - ADAM_B1) * g
    v_new = ADAM_B2 * v + (1.0 - ADAM_B2) * (g * g)
    m_hat = m_new * (1.0 / (1.0 - ADAM_B1 ** ADAM_STEP))
    v_hat = v_new * (1.0 / (1.0 - ADAM_B2 ** ADAM_STEP))
    return -ADAM_LR * (m_hat / (jnp.sqrt(v_hat) + ADAM_EPS) + ADAM_WD * w), m_new, v_new


def _sum_chunks(me, landed, own, name):
    _, _, r, n = landed.shape

    def body(me_ref, l_ref, own_ref, g_out):
        g_out[...] = _sum_in_device_order(me_ref, l_ref, own_ref)[0]

    return pl.pallas_call(
        body, name="sum_" + name,
        grid_spec=pltpu.PrefetchScalarGridSpec(
            num_scalar_prefetch=1, grid=(1,),
            in_specs=[pl.BlockSpec((N_DEV, 1, r, n), lambda i, me_ref: (0, 0, 0, 0)),
                      pl.BlockSpec((1, 1, r, n), lambda i, me_ref: (me_ref[0], 0, 0, 0))],
            out_specs=pl.BlockSpec((r, n), lambda i, me_ref: (0, 0))),
        out_shape=_sds((r, n)), compiler_params=_params())(me, landed, own)


def _adamw_small(gs, ws, ms, vs):
    n = len(gs)

    def body(*refs):
        ins, outs = refs[:4 * n], refs[4 * n:]
        for i in range(n):
            outs[i][...], outs[n + i][...], outs[2 * n + i][...] = _adamw(*[ins[k * n + i][...] for k in range(4)])

    out = pl.pallas_call(body, name="adamw_small", out_shape=[_sds(w.shape) for w in ws] * 3)(*gs, *ws, *ms, *vs)
    return out[:n], out[n:2 * n], out[2 * n:]


def _sum_and_adamw(me, landed, own, wts, m, v, name, layer=None, into=None):
    layers, r, n = wts.shape
    first = 0 if layer is None else layer
    count = layers if layer is None else 1
    tr = _row_tile(r)
    blk = pl.BlockSpec((1, tr, n), lambda li, ri, me_ref: (first + li, ri, 0))
    held = [] if into is None else list(into)

    def body(me_ref, l_ref, own_ref, w_ref, m_ref, v_ref, *rest):
        g_out, d_out, m_out, v_out = rest[len(held):]
        g = _sum_in_device_order(me_ref, l_ref, own_ref)
        g_out[...] = g
        d_out[...], m_out[...], v_out[...] = _adamw(g, w_ref[...], m_ref[...], v_ref[...])

    return pl.pallas_call(
        body, name="adamw_" + name,
        grid_spec=pltpu.PrefetchScalarGridSpec(
            num_scalar_prefetch=1, grid=(count, r // tr),
            in_specs=[pl.BlockSpec((N_DEV, 1, tr, n), lambda li, ri, me_ref: (0, li, ri, 0)),
                      pl.BlockSpec((1, 1, tr, n), lambda li, ri, me_ref: (me_ref[0], li, ri, 0)), blk, blk, blk] + [HBM] * len(held),
            out_specs=[blk] * 4),
        out_shape=[_sds((layers, r, n))] * 4, input_output_aliases={6 + i: i for i in range(len(held))},
        compiler_params=_params(2))(me, landed, own, wts, m, v, *held)


EARLY = ['ab_w_in']
LATE_STAGES = {
    'out0': [('ab_w_out', None, 'ab_w_out')],
    'ffn0': [('ffn_w_gate', 0, 'Wg'), ('ffn_w_up', 0, 'Wu'), ('ffn_w_down', 0, 'Wd')],
    'mix1': [('c_w_in', None, 'c_w_in'), ('c_w_out', None, 'c_w_out')],
    'ffn1': [('ffn_w_gate', 1, 'Wg'), ('ffn_w_up', 1, 'Wu'), ('ffn_w_down', 1, 'Wd')],
}
TRANSPOSED = ('ffn_w_gate', 'ffn_w_up')


def _stored(name, a):
    return jnp.swapaxes(a, 1, 2) if name in TRANSPOSED else a


def _stored_axis(name):
    return 1 if name in TRANSPOSED else SHARD_AXIS[name]


GRAD_STAGES = {
    'late1': ([('c_w_in', None), ('c_w_out', None), ('ffn_w_gate', 1), ('ffn_w_up', 1), ('ffn_w_down', 1)],
              ['c_norm', 'c_ln_g', 'c_ln_b', 'c_w_s', 'c_b_s', 'final_norm']),
    'late0': ([('ffn_w_gate', 0), ('ffn_w_up', 0), ('ffn_w_down', 0)], ['ffn_norm', 'ffn_conv_w', 'ffn_conv_b']),
    'mid': ([('ab_w_out', None)], ['ab_conv_w', 'ab_conv_b', 'ab_w_rg_a', 'ab_b_rg_a', 'ab_w_rg_x', 'ab_b_rg_x', 'ab_lambda']),
    'last': ([('ab_w_in', None)], ['ab_norm', 'ab_q_norm', 'ab_w_q_b', 'ab_kv_norm', 'ab_w_kv_b']),
}


def _gather_early(local):
    small = [_bf(local[n]) if n in MATRICES else lax.bitcast_convert_type(local[n], BF16) for n in SMALL_SHARDED]
    gathered, zero = _all_gather([_bf(local[n]) for n in EARLY] + [_pack_slabs(small, ())])
    full = {n: local[n] for n in REPLICATED}
    for n, g in zip(EARLY, gathered):
        full[n] = _from_chunks(g, SHARD_AXIS[n])
    for n, p in zip(SMALL_SHARDED, _unpack_slabs(gathered[-1], [s.shape for s in small])):
        full[n] = _from_chunks(p if n in MATRICES else lax.bitcast_convert_type(p, F32), SHARD_AXIS[n])
    return full, zero


def kernel(x, positions, ab_norm, ab_w_in, ab_q_norm, ab_w_q_b, ab_kv_norm, ab_w_kv_b, ab_conv_w, ab_conv_b, ab_w_rg_a, ab_b_rg_a, ab_w_rg_x, ab_b_rg_x, ab_lambda, ab_w_out, c_norm, c_w_in, c_ln_g, c_ln_b, c_w_s, c_b_s, c_w_out, ffn_norm, ffn_w_gate, ffn_w_up, ffn_conv_w, ffn_conv_b, ffn_w_down, final_norm, loss_target, m_ab_norm, m_ab_w_in, m_ab_q_norm, m_ab_w_q_b, m_ab_kv_norm, m_ab_w_kv_b, m_ab_conv_w, m_ab_conv_b, m_ab_w_rg_a, m_ab_b_rg_a, m_ab_w_rg_x, m_ab_b_rg_x, m_ab_lambda, m_ab_w_out, m_c_norm, m_c_w_in, m_c_ln_g, m_c_ln_b, m_c_w_s, m_c_b_s, m_c_w_out, m_ffn_norm, m_ffn_w_gate, m_ffn_w_up, m_ffn_conv_w, m_ffn_conv_b, m_ffn_w_down, m_final_norm, v_ab_norm, v_ab_w_in, v_ab_q_norm, v_ab_w_q_b, v_ab_kv_norm, v_ab_w_kv_b, v_ab_conv_w, v_ab_conv_b, v_ab_w_rg_a, v_ab_b_rg_a, v_ab_w_rg_x, v_ab_b_rg_x, v_ab_lambda, v_ab_w_out, v_c_norm, v_c_w_in, v_c_ln_g, v_c_ln_b, v_c_w_s, v_c_b_s, v_c_w_out, v_ffn_norm, v_ffn_w_gate, v_ffn_w_up, v_ffn_conv_w, v_ffn_conv_b, v_ffn_w_down, v_final_norm):
    given = dict(locals())
    local = {n: given[n] for n in WEIGHTS}
    b, seq, d = x.shape
    t = b * seq

    me = (4 * lax.axis_index("x") + 2 * lax.axis_index("y") + lax.axis_index("c")).astype(jnp.int32)
    me1 = me.reshape(1)

    full, zero = _gather_early(local)
    gathers = {}
    for stage, members in LATE_STAGES.items():
        srcs = [_bf(_stored(n, local[n] if layer is None else local[n][layer:layer + 1]) + zero) for n, layer, _ in members]
        gathers[stage] = _start_exchange('gather_' + stage, srcs, scatter=False)
        zero = gathers[stage][4]
    w = _prepare(full)
    w['ab_norm'] = w['ab_norm'] + zero

    def late_weights(stage, after):
        _, lands = _wait_exchange('gather_' + stage, gathers[stage], after, scatter=False)
        whole = [l.reshape(1, -1, l.shape[-1]) if _stored_axis(n) == 1 else _merge_columns(l, n)
                 for (n, _, _), l in zip(LATE_STAGES[stage], lands)]
        if stage == 'out0':
            return _prepare_out(whole[0])
        return {key: a[0] for (_, _, key), a in zip(LATE_STAGES[stage], whole)}

    scatters = {}

    def start_scatter(stage, g):
        whole = _unprepare(g)
        big, small = GRAD_STAGES[stage]
        slab = [_to_chunks(whole[n], SHARD_AXIS[n]) if n in SHARD_AXIS else jnp.broadcast_to(whole[n][None], (N_DEV,) + whole[n].shape)
                for n in small]
        own = [whole[n].reshape(N_DEV, 1, whole[n].shape[1] // N_DEV, whole[n].shape[2])
               if whole[n].dtype == BF16 and _stored_axis(n) == 1 else
               _split_chunks(whole[n], _stored_axis(n), n + ('' if layer is None else str(layer))) for n, layer in big]
        own.append(_bf(_pack_slabs(slab, (N_DEV,)))[:, None])
        scatters[stage] = _start_exchange('scatter_' + stage, own, scatter=True)
        return scatters[stage][4]

    posb = jnp.broadcast_to(positions.astype(F32).reshape(t, 1), (t, LANES))
    loss, dx, grads = _local_step(x.reshape(t, d), posb, loss_target.reshape(t, d), w, seq, late_weights, start_scatter)
    start_scatter('last', grads)
    after = scatters['last'][5]

    updated, small_grads = {}, {}
    for stage, (big, small) in GRAD_STAGES.items():
        owns, landed = _wait_exchange('scatter_' + stage, scatters[stage], after, scatter=True)
        for (n, layer), own, land in zip(big, owns, landed):
            updated[n] = _sum_and_adamw(me1, land, own, _stored(n, given[n]), _stored(n, given['m_' + n]), _stored(n, given['v_' + n]),
                                        n + ('' if layer is None else str(layer)), layer, updated.get(n))
        summed = _sum_chunks(me1, landed[-1], owns[-1], stage)
        small_grads.update(zip(small, _unpack_slabs(summed, [local[n].shape for n in small])))
        after = sum([updated[n][1][:1, :1, :1] for n, _ in big], summed[:1, :1].reshape(1, 1, 1))
    names = list(small_grads)
    news = _adamw_small([small_grads[n] for n in names], *[[given[p + n] for n in names] for p in ('', 'm_', 'v_')])
    for i, n in enumerate(names):
        updated[n] = [small_grads[n], news[0][i], news[1][i], news[2][i]]
    total = lax.psum(loss[0, 0], ("x", "y", "c"))
    return (total, dx.reshape(b, seq, d), *[_stored(n, updated[n][kind]) for kind in range(4) for n in WEIGHTS])
```

```python
import math

import jax
import jax.numpy as jnp
from jax import lax
from jax.experimental import pallas as pl
from jax.experimental.pallas import tpu as pltpu

F32 = jnp.float32
BF16 = jnp.bfloat16
MESH = pl.DeviceIdType.MESH

N_DEV = 8
LANES = 128
HALO = 8
VMEM_LIMIT = 56 << 20

NORM_EPS = 1e-6
HEADS = 8
HEAD_PAD = 128
QK_NOPE = 64
QK_ROPE = 32
ROPE_HALF = 16
ROPE_BASE = 10000.0
ATTN_SCALE = (QK_NOPE + QK_ROPE) ** -0.5
LRU_C = 8.0
LRU_W = 512
CHUNK = 128
SGU_GROUPS = 8
D_FF = 2816
FF_BLOCKS = 2

ADAM_LR, ADAM_B1, ADAM_B2, ADAM_EPS, ADAM_WD, ADAM_STEP = 0.001, 0.9, 0.999, 1e-08, 0.01, 10

WEIGHTS = ['ab_norm', 'ab_w_in', 'ab_q_norm', 'ab_w_q_b', 'ab_kv_norm', 'ab_w_kv_b', 'ab_conv_w', 'ab_conv_b',
           'ab_w_rg_a', 'ab_b_rg_a', 'ab_w_rg_x', 'ab_b_rg_x', 'ab_lambda', 'ab_w_out', 'c_norm', 'c_w_in', 'c_ln_g',
           'c_ln_b', 'c_w_s', 'c_b_s', 'c_w_out', 'ffn_norm', 'ffn_w_gate', 'ffn_w_up', 'ffn_conv_w', 'ffn_conv_b',
           'ffn_w_down', 'final_norm']
SHARD_AXIS = {'ab_w_in': 2, 'ab_w_q_b': 2, 'ab_w_kv_b': 2, 'ab_conv_w': 2, 'ab_w_out': 1, 'c_norm': 1, 'c_w_in': 2,
              'c_ln_g': 1, 'c_ln_b': 1, 'c_w_out': 1, 'ffn_w_gate': 2, 'ffn_w_up': 2, 'ffn_conv_w': 2, 'ffn_w_down': 1}
MATRICES = ['ab_w_in', 'ab_w_q_b', 'ab_w_kv_b', 'ab_w_out', 'c_w_in', 'c_w_out', 'ffn_w_gate', 'ffn_w_up', 'ffn_w_down']
BIG = ['ab_w_in', 'c_w_in', 'ffn_w_gate', 'ffn_w_up', 'ab_w_out', 'c_w_out', 'ffn_w_down']
REPLICATED = [n for n in WEIGHTS if n not in SHARD_AXIS]
SMALL_SHARDED = [n for n in WEIGHTS if n in SHARD_AXIS and n not in BIG]


def _bf(x):
    return x.astype(BF16)


def _nn(a, b):
    return lax.dot_general(_bf(a), _bf(b), (((1,), (0,)), ((), ())), preferred_element_type=F32)


def _nt(a, b):
    return lax.dot_general(_bf(a), _bf(b), (((1,), (1,)), ((), ())), preferred_element_type=F32)


def _tn(a, b):
    return lax.dot_general(_bf(a), _bf(b), (((0,), (0,)), ((), ())), preferred_element_type=F32)


def _rms(x, g):
    return x * lax.rsqrt(jnp.mean(x * x, axis=-1, keepdims=True) + NORM_EPS) * g


def _layer_norm(x, g, b):
    xc = x - jnp.mean(x, axis=-1, keepdims=True)
    return xc * lax.rsqrt(jnp.mean(xc * xc, axis=-1, keepdims=True) + NORM_EPS) * g + b


def _gelu(x):
    return jax.nn.gelu(x)


STRIP = 16
STRIP_LANES = 384
GELU_C = math.sqrt(2.0 / math.pi)
GELU_A = 0.044715


def _gelu_and_grad(x):
    x2 = x * x
    t = jnp.tanh(x * (GELU_C + (GELU_C * GELU_A) * x2))
    half_x = 0.5 * x
    one_plus_t = 1.0 + t
    return half_x * one_plus_t, 0.5 * one_plus_t + half_x * (1.0 - t * t) * (GELU_C + (3.0 * GELU_C * GELU_A) * x2)


def _colsum(x):
    return jnp.sum(x, axis=0, keepdims=True)


def _softplus(x):
    return jnp.maximum(x, 0.0) + jnp.log1p(jnp.exp(-jnp.abs(x)))


@jax.custom_vjp
def _decay(x):
    a = jnp.exp(x)
    y = 2.0 * x
    series = -y * (1.0 + y * (1 / 2 + y * (1 / 6 + y * (1 / 24 + y * (1 / 120 + y * (1 / 720))))))
    return a, jnp.where(y < -0.3, 1.0 - a * a, series)


def _decay_fwd(x):
    a, gap = _decay(x)
    return (a, gap), a


def _decay_bwd(a, cts):
    return (a * (cts[0] - 2.0 * a * cts[1]),)


_decay.defvjp(_decay_fwd, _decay_bwd)


def _accumulate(ref, val, first):
    @pl.when(first)
    def _():
        ref[...] = val

    @pl.when(jnp.logical_not(first))
    def _():
        ref[...] += val


def _params(n_axes=1):
    return pltpu.CompilerParams(dimension_semantics=("arbitrary",) * n_axes, vmem_limit_bytes=VMEM_LIMIT)


def _row(tm, n):
    return pl.BlockSpec((tm, n), lambda i: (i, 0))


def _const(shape):
    nd = len(shape)
    return pl.BlockSpec(shape, lambda i: (0,) * nd, pipeline_mode=pl.Buffered(1))


def _prev_halo(tm, n):
    return pl.BlockSpec((HALO, n), lambda i: (jnp.maximum(i * (tm // HALO) - 1, 0), 0))


def _next_halo(tm, n, n_tiles):
    last = n_tiles * (tm // HALO) - 1
    return pl.BlockSpec((HALO, n), lambda i: (jnp.minimum((i + 1) * (tm // HALO), last), 0))


def _sds(shape, dtype=F32):
    return jax.ShapeDtypeStruct(shape, dtype)


def _rope_tables(posb):
    lane = lax.broadcasted_iota(jnp.int32, posb.shape, 1)
    in_rope = jnp.logical_and(lane >= QK_NOPE, lane < QK_NOPE + QK_ROPE)
    j = (lane & (ROPE_HALF - 1)).astype(F32)
    inv_freq = jnp.exp((-math.log(ROPE_BASE)) * j / ROPE_HALF)
    ang = posb * inv_freq
    return jnp.where(in_rope, jnp.cos(ang), 1.0), jnp.where(in_rope, jnp.sin(ang), 0.0)


def _rot(q):
    n = q.shape[1]
    lane = lax.broadcasted_iota(jnp.int32, q.shape, 1) & (HEAD_PAD - 1)
    first_half = jnp.where(lane >= QK_NOPE, -pltpu.roll(q, n - ROPE_HALF, 1), 0.0)
    second_half = jnp.where(lane < QK_NOPE + QK_ROPE, pltpu.roll(q, ROPE_HALF, 1), 0.0)
    return jnp.where(lane < QK_NOPE + ROPE_HALF, first_half, second_half)


def _rope(q, cos_t, sin_t):
    return q * cos_t + _rot(q) * sin_t


def _rope_transpose(dq, cos_t, sin_t):
    return dq * cos_t - _rot(dq * sin_t)


def _tile_heads(t):
    return jnp.concatenate([t] * HEADS, axis=1)


Q_LORA, KV_LORA = 256, 128
Z_KPE = Q_LORA + KV_LORA
Z_LRU = Z_KPE + HEAD_PAD
Z_GATE = Z_LRU + LRU_W
Z_WIDTH = Z_GATE + LRU_W


def _ab_in_fwd(x, posb, w, tm):
    t, d = x.shape

    def body(x_ref, pos_ref, gn_ref, win_ref, qn_ref, wq_ref, kvn_ref, wk_ref, wv_ref, q_out, k_out, v_out, xl_out, gate_out):
        hn = _rms(x_ref[...], gn_ref[...])
        z = _nn(hn, win_ref[...])
        cqn = _rms(z[:, :Q_LORA], qn_ref[...])
        kvn = _rms(z[:, Q_LORA:Z_KPE], kvn_ref[...])
        cos_t, sin_t = _rope_tables(pos_ref[...])
        q_out[...] = _bf(_rope(_nn(cqn, wq_ref[...]), _tile_heads(cos_t), _tile_heads(sin_t)))
        kpe = _rope(z[:, Z_KPE:Z_LRU], cos_t, sin_t)
        k_out[...] = _bf(_nn(kvn, wk_ref[...]) + _tile_heads(kpe))
        v_out[...] = _bf(_nn(kvn, wv_ref[...]))
        xl_out[...] = z[:, Z_LRU:Z_GATE]
        gate_out[...] = z[:, Z_GATE:]

    hp = HEADS * HEAD_PAD
    return pl.pallas_call(
        body, name="ab_in_fwd", grid=(t // tm,),
        in_specs=[_row(tm, d), _row(tm, LANES), _const((1, d)), _const((d, Z_WIDTH)), _const((1, Q_LORA)), _const((Q_LORA, hp)),
                  _const((1, KV_LORA)), _const((KV_LORA, hp)), _const((KV_LORA, hp))],
        out_specs=[_row(tm, hp), _row(tm, hp), _row(tm, hp), _row(tm, LRU_W), _row(tm, LRU_W)],
        out_shape=[_sds((t, hp), BF16), _sds((t, hp), BF16), _sds((t, hp), BF16), _sds((t, LRU_W)), _sds((t, LRU_W))],
        compiler_params=_params(),
    )(x, posb, w['ab_norm'], w['W_in'], w['ab_q_norm'], w['Wq'], w['ab_kv_norm'], w['Wk'], w['Wv'])


def _ab_in_bwd(x, posb, w, dq, dk, dv, dxl, dgate, dres, tm):
    t, d = x.shape
    hp = HEADS * HEAD_PAD

    def body(x_ref, pos_ref, gn_ref, win_ref, qn_ref, wq_ref, kvn_ref, wk_ref, wv_ref, dq_ref, dk_ref, dv_ref, dxl_ref, dgate_ref,
             dres_ref, dx_out, dgn_out, dwin_out, dqn_out, dwq_out, dkvn_out, dwk_out, dwv_out):
        first = pl.program_id(0) == 0
        hn, vjp_in = jax.vjp(_rms, x_ref[...], gn_ref[...])
        z = _nn(hn, win_ref[...])
        cqn, vjp_q = jax.vjp(_rms, z[:, :Q_LORA], qn_ref[...])
        kvn, vjp_kv = jax.vjp(_rms, z[:, Q_LORA:Z_KPE], kvn_ref[...])
        cos_t, sin_t = _rope_tables(pos_ref[...])
        dq0 = _rope_transpose(dq_ref[...], _tile_heads(cos_t), _tile_heads(sin_t))
        dk0 = dk_ref[...]
        dv0 = dv_ref[...]
        dkpe = dk0[:, :HEAD_PAD]
        for h in range(1, HEADS):
            dkpe = dkpe + dk0[:, h * HEAD_PAD:(h + 1) * HEAD_PAD]
        dkpe = _rope_transpose(dkpe, cos_t, sin_t)
        _accumulate(dwq_out, _tn(cqn, dq0), first)
        _accumulate(dwk_out, _tn(kvn, dk0), first)
        _accumulate(dwv_out, _tn(kvn, dv0), first)
        dcq, dqn = vjp_q(_nt(dq0, wq_ref[...]))
        dckv, dkvn = vjp_kv(_nt(dk0, wk_ref[...]) + _nt(dv0, wv_ref[...]))
        _accumulate(dqn_out, dqn, first)
        _accumulate(dkvn_out, dkvn, first)
        dz = _bf(jnp.concatenate([_bf(dcq), _bf(dckv), _bf(dkpe), dxl_ref[...], dgate_ref[...]], axis=1))
        _accumulate(dwin_out, _tn(hn, dz), first)
        dx, dgn = vjp_in(_nt(dz, win_ref[...]))
        _accumulate(dgn_out, dgn, first)
        dx_out[...] = dx + dres_ref[...]

    return pl.pallas_call(
        body, name="ab_in_bwd", grid=(t // tm,),
        in_specs=[_row(tm, d), _row(tm, LANES), _const((1, d)), _const((d, Z_WIDTH)), _const((1, Q_LORA)), _const((Q_LORA, hp)),
                  _const((1, KV_LORA)), _const((KV_LORA, hp)), _const((KV_LORA, hp)),
                  _row(tm, hp), _row(tm, hp), _row(tm, hp), _row(tm, LRU_W), _row(tm, LRU_W), _row(tm, d)],
        out_specs=[_row(tm, d), _const((1, d)), _const((d, Z_WIDTH)), _const((1, Q_LORA)), _const((Q_LORA, hp)),
                   _const((1, KV_LORA)), _const((KV_LORA, hp)), _const((KV_LORA, hp))],
        out_shape=[_sds((t, d)), _sds((1, d)), _sds((d, Z_WIDTH)), _sds((1, Q_LORA)), _sds((Q_LORA, hp)),
                   _sds((1, KV_LORA)), _sds((KV_LORA, hp)), _sds((KV_LORA, hp))],
        compiler_params=_params(),
    )(x, posb, w['ab_norm'], w['W_in'], w['ab_q_norm'], w['Wq'], w['ab_kv_norm'], w['Wk'], w['Wv'], dq, dk, dv, dxl, dgate, dres)


def _attn_probs(q_blk, k_ext, tq):
    ext = k_ext.shape[0]
    s = lax.dot_general(q_blk, k_ext, (((1,), (1,)), ((), ())), preferred_element_type=F32) * ATTN_SCALE
    causal = lax.broadcasted_iota(jnp.int32, (tq, tq), 1) <= lax.broadcasted_iota(jnp.int32, (tq, tq), 0)
    diag = jnp.where(causal, s[:, ext - tq:], -1e30)
    s = diag if ext == tq else jnp.concatenate([s[:, :ext - tq], diag], axis=1)
    p = jnp.exp(s - jnp.max(s, axis=1, keepdims=True))
    return p * (1.0 / jnp.sum(p, axis=1, keepdims=True))


def _attn_fwd(q, k, v, tq):
    b, s, hp = q.shape
    blk = pl.BlockSpec((1, s, HEAD_PAD), lambda bi, h: (bi, 0, h))

    def body(q_ref, k_ref, v_ref, o_ref, p_ref):
        for i in range(s // tq):
            ext = (i + 1) * tq
            p = _bf(_attn_probs(_bf(q_ref[0, i * tq:ext, :]), _bf(k_ref[0, :ext, :]), tq))
            p_ref[0, 0, i * tq:ext, :ext] = p
            o_ref[0, i * tq:ext, :] = _bf(lax.dot_general(p, _bf(v_ref[0, :ext, :]), (((1,), (0,)), ((), ())),
                                                          preferred_element_type=F32))

    return pl.pallas_call(body, name="attn_fwd", grid=(b, HEADS), in_specs=[blk, blk, blk],
                          out_specs=[blk, pl.BlockSpec((1, 1, s, s), lambda bi, h: (bi, h, 0, 0))],
                          out_shape=[_sds((b, s, hp), BF16), _sds((b, HEADS, s, s), BF16)], compiler_params=_params(2))(q, k, v)


def _attn_bwd(q, k, v, probs, do, tq):
    b, s, hp = q.shape
    blk = pl.BlockSpec((1, s, HEAD_PAD), lambda bi, h: (bi, 0, h))

    def body(q_ref, k_ref, v_ref, p_ref, do_ref, dq_ref, dk_ref, dv_ref):
        dk_ref[...] = jnp.zeros_like(dk_ref)
        dv_ref[...] = jnp.zeros_like(dv_ref)
        for i in range(s // tq):
            ext = (i + 1) * tq
            qb = _bf(q_ref[0, i * tq:ext, :])
            dob = _bf(do_ref[0, i * tq:ext, :])
            pb = p_ref[0, 0, i * tq:ext, :ext]
            p = pb.astype(F32)
            dv_ref[0, :ext, :] += lax.dot_general(pb, dob, (((0,), (0,)), ((), ())), preferred_element_type=F32)
            dp = lax.dot_general(dob, _bf(v_ref[0, :ext, :]), (((1,), (1,)), ((), ())), preferred_element_type=F32)
            ds = _bf(p * (dp - jnp.sum(p * dp, axis=1, keepdims=True)) * ATTN_SCALE)
            dq_ref[0, i * tq:ext, :] = lax.dot_general(ds, _bf(k_ref[0, :ext, :]), (((1,), (0,)), ((), ())), preferred_element_type=F32)
            dk_ref[0, :ext, :] += lax.dot_general(ds, qb, (((0,), (0,)), ((), ())), preferred_element_type=F32)

    return pl.pallas_call(body, name="attn_bwd", grid=(b, HEADS),
                          in_specs=[blk, blk, blk, pl.BlockSpec((1, 1, s, s), lambda bi, h: (bi, h, 0, 0)), blk], out_specs=[blk, blk, blk],
                          out_shape=[_sds((b, s, hp))] * 3, compiler_params=_params(2))(q, k, v, probs, do)


LRU_CONV = 4


def _lru_point(pre_a, pre_x, xc, lam):
    r = jax.nn.sigmoid(pre_a)
    i = jax.nn.sigmoid(pre_x)
    a, gap = _decay(-LRU_C * r * _softplus(-lam))
    return a, jnp.sqrt(gap) * (i * xc)


def _causal_conv(pad_ref, x, halo, first_in_seq, w, taps):
    tm = x.shape[0]
    pad_ref[:HALO, :] = jnp.where(first_in_seq, 0.0, halo)
    pad_ref[HALO:, :] = x
    y = w[taps - 1:taps, :] * x
    for k in range(taps - 1):
        off = HALO - (taps - 1) + k
        y = y + w[k:k + 1, :] * pad_ref[off:off + tm, :]
    return y


def _conv_taps(pad_ref, r, cols, taps):
    blocks = [pad_ref[r + j * HALO:r + (j + 1) * HALO, cols] for j in range(1 + STRIP // HALO)]
    sub = lax.broadcasted_iota(jnp.int32, blocks[0].shape, 0)
    out = []
    for k in range(taps - 1):
        s = taps - 1 - k
        rolled = [pltpu.roll(b, s, 0) for b in blocks]
        out.append(jnp.concatenate([jnp.where(sub < s, rolled[j], rolled[j + 1]) for j in range(STRIP // HALO)], axis=0))
    out.append(jnp.concatenate(blocks[1:], axis=0))
    return out


def _causal_conv_wgrad(pad_ref, dy, taps):
    tm = dy.shape[0]
    return jnp.concatenate([_colsum(dy * pad_ref[HALO - (taps - 1) + k:HALO - (taps - 1) + k + tm, :]) for k in range(taps)], axis=0)


def _causal_conv_transpose(pad_ref, dy, halo_next, last_in_seq, w, taps):
    tm = dy.shape[0]
    pad_ref[:tm, :] = dy
    pad_ref[tm:, :] = jnp.where(last_in_seq, 0.0, halo_next)
    dx = w[taps - 1:taps, :] * dy
    for k in range(taps - 1):
        off = (taps - 1) - k
        dx = dx + w[k:k + 1, :] * pad_ref[off:off + tm, :]
    return dx


def _lru_fwd(xl, gate, w, ts, seq):
    t, n = xl.shape
    tiles_per_seq = seq // ts

    def body(xl_ref, halo_ref, gate_ref, cw_ref, cb_ref, wa_ref, ba_ref, wx_ref, bx_ref, lam_ref, y_out, h_out, pad_ref, a_ref, b_ref, carry_ref):
        first_in_seq = pl.program_id(0) % tiles_per_seq == 0
        xc = _causal_conv(pad_ref, xl_ref[...], halo_ref[...], first_in_seq, cw_ref[...], LRU_CONV) + cb_ref[...]
        a, bx = _lru_point(_nn(xc, wa_ref[...]) + ba_ref[...], _nn(xc, wx_ref[...]) + bx_ref[...], xc, lam_ref[...])
        a_ref[...] = a
        b_ref[...] = bx

        @pl.when(first_in_seq)
        def _():
            carry_ref[...] = jnp.zeros_like(carry_ref)

        def step(r, h):
            h = a_ref[pl.ds(r, 1), :] * h + b_ref[pl.ds(r, 1), :]
            h_out[pl.ds(r, 1), :] = h
            return h

        carry_ref[...] = lax.fori_loop(0, ts, step, carry_ref[...], unroll=8)
        y_out[...] = _bf(h_out[...] * _gelu(gate_ref[...]))

    return pl.pallas_call(
        body, name="lru_fwd", grid=(t // ts,),
        in_specs=[_row(ts, n), _prev_halo(ts, n), _row(ts, n), _const((LRU_CONV, n)), _const((1, n)), _const((n, n)), _const((1, n)),
                  _const((n, n)), _const((1, n)), _const((1, n))],
        out_specs=[_row(ts, n), _row(ts, n)], out_shape=[_sds((t, n), BF16), _sds((t, n))],
        scratch_shapes=[pltpu.VMEM((HALO + ts, n), F32), pltpu.VMEM((ts, n), F32), pltpu.VMEM((ts, n), F32), pltpu.VMEM((1, n), F32)],
        compiler_params=_params(),
    )(xl, xl, gate, w['ab_conv_w'], w['ab_conv_b'], w['Wa'], w['ab_b_rg_a'], w['Wx'], w['ab_b_rg_x'], w['ab_lambda'])


def _lru_bwd(xl, gate, hs, dy, w, ts, seq):
    t, n = xl.shape
    tiles_per_seq = seq // ts
    n_tiles = t // ts

    def rev(i):
        return n_tiles - 1 - i

    row = pl.BlockSpec((ts, n), lambda i: (rev(i), 0))
    prev = pl.BlockSpec((HALO, n), lambda i: (jnp.maximum(rev(i) * (ts // HALO) - 1, 0), 0))
    acc = lambda shape: pl.BlockSpec(shape, lambda i: (0,) * len(shape))

    def body(xl_ref, xhalo_ref, gate_ref, h_ref, hhalo_ref, dy_ref, cw_ref, cb_ref, wa_ref, ba_ref, wx_ref, bx_ref, lam_ref,
             dxl_out, dgate_out, dcw_out, dcb_out, dwa_out, dba_out, dwx_out, dbx_out, dlam_out,
             pad_ref, padh_ref, padd_ref, a_ref, g_ref, carry_ref, dhalo_ref):
        step_id = pl.program_id(0)
        first = step_id == 0
        tile = rev(step_id)
        first_in_seq = tile % tiles_per_seq == 0
        last_in_seq = tile % tiles_per_seq == tiles_per_seq - 1
        cw = cw_ref[...]
        xc = _causal_conv(pad_ref, xl_ref[...], xhalo_ref[...], first_in_seq, cw, LRU_CONV) + cb_ref[...]
        pre_a = _nn(xc, wa_ref[...]) + ba_ref[...]
        pre_x = _nn(xc, wx_ref[...]) + bx_ref[...]
        (a, _), vjp_point = jax.vjp(_lru_point, pre_a, pre_x, xc, lam_ref[...])
        h = h_ref[...]
        _, vjp_out = jax.vjp(lambda h_, g_: h_ * _gelu(g_), h, gate_ref[...])
        dh, dgate = vjp_out(dy_ref[...])
        dgate_out[...] = _bf(dgate)
        a_ref[...] = a
        g_ref[...] = dh

        @pl.when(last_in_seq)
        def _():
            carry_ref[...] = jnp.zeros_like(carry_ref)

        def step(j, c):
            r = ts - 1 - j
            g = g_ref[pl.ds(r, 1), :] + c
            g_ref[pl.ds(r, 1), :] = g
            return a_ref[pl.ds(r, 1), :] * g

        carry_ref[...] = lax.fori_loop(0, ts, step, carry_ref[...], unroll=8)
        g = g_ref[...]
        padh_ref[:HALO, :] = jnp.where(first_in_seq, 0.0, hhalo_ref[...])
        padh_ref[HALO:, :] = h
        dpre_a, dpre_x, dxc, dlam = vjp_point((g * padh_ref[HALO - 1:HALO - 1 + ts, :], g))
        dxc = dxc + _nt(dpre_a, wa_ref[...]) + _nt(dpre_x, wx_ref[...])
        _accumulate(dwa_out, _tn(xc, dpre_a), first)
        _accumulate(dwx_out, _tn(xc, dpre_x), first)
        _accumulate(dba_out, _colsum(dpre_a), first)
        _accumulate(dbx_out, _colsum(dpre_x), first)
        _accumulate(dlam_out, dlam, first)
        _accumulate(dcb_out, _colsum(dxc), first)
        _accumulate(dcw_out, _causal_conv_wgrad(pad_ref, dxc, LRU_CONV), first)
        dxl_out[...] = _bf(_causal_conv_transpose(padd_ref, dxc, dhalo_ref[...], last_in_seq, cw, LRU_CONV))
        dhalo_ref[...] = dxc[:HALO, :]

    return pl.pallas_call(
        body, name="lru_bwd", grid=(n_tiles,),
        in_specs=[row, prev, row, row, prev, row, _const((LRU_CONV, n)), _const((1, n)), _const((n, n)), _const((1, n)),
                  _const((n, n)), _const((1, n)), _const((1, n))],
        out_specs=[row, row, acc((LRU_CONV, n)), acc((1, n)), acc((n, n)), acc((1, n)), acc((n, n)), acc((1, n)), acc((1, n))],
        out_shape=[_sds((t, n), BF16), _sds((t, n), BF16), _sds((LRU_CONV, n)), _sds((1, n)), _sds((n, n)), _sds((1, n)), _sds((n, n)),
                   _sds((1, n)), _sds((1, n))],
        scratch_shapes=[pltpu.VMEM((HALO + ts, n), F32), pltpu.VMEM((HALO + ts, n), F32), pltpu.VMEM((ts + HALO, n), F32),
                        pltpu.VMEM((ts, n), F32), pltpu.VMEM((ts, n), F32), pltpu.VMEM((1, n), F32), pltpu.VMEM((HALO, n), F32)],
        compiler_params=_params(),
    )(xl, xl, gate, hs, hs, dy, w['ab_conv_w'], w['ab_conv_b'], w['Wa'], w['ab_b_rg_a'], w['Wx'], w['ab_b_rg_x'], w['ab_lambda'])


def _ab_out_fwd(x, o, y, w, tm):
    t, d = x.shape
    hp = o.shape[1]

    def body(x_ref, o_ref, y_ref, wa_ref, wb_ref, h_out):
        h_out[...] = x_ref[...] + _nn(o_ref[...], wa_ref[...]) + _nn(y_ref[...], wb_ref[...])

    return pl.pallas_call(body, name="ab_out_fwd", grid=(t // tm,),
                          in_specs=[_row(tm, d), _row(tm, hp), _row(tm, LRU_W), _const((hp, d)), _const((LRU_W, d))],
                          out_specs=_row(tm, d), out_shape=_sds((t, d)), compiler_params=_params())(x, o, y, w['Wo_a'], w['Wo_b'])


def _ab_out_bwd(o, y, dh, w, tm):
    t, d = dh.shape
    hp = o.shape[1]

    def body(o_ref, y_ref, dh_ref, wa_ref, wb_ref, do_out, dy_out, dwa_out, dwb_out):
        first = pl.program_id(0) == 0
        dh_t = dh_ref[...]
        do_out[...] = _bf(_nt(dh_t, wa_ref[...]))
        dy_out[...] = _nt(dh_t, wb_ref[...])
        _accumulate(dwa_out, _tn(o_ref[...], dh_t), first)
        _accumulate(dwb_out, _tn(y_ref[...], dh_t), first)

    return pl.pallas_call(body, name="ab_out_bwd", grid=(t // tm,),
                          in_specs=[_row(tm, hp), _row(tm, LRU_W), _row(tm, d), _const((hp, d)), _const((LRU_W, d))],
                          out_specs=[_row(tm, hp), _row(tm, LRU_W), _const((hp, d)), _const((LRU_W, d))],
                          out_shape=[_sds((t, hp), BF16), _sds((t, LRU_W)), _sds((hp, d)), _sds((LRU_W, d))],
                          compiler_params=_params())(o, y, dh, w['Wo_a'], w['Wo_b'])


FFN_CONV = 3


def _ffn_a_fwd(h, norm, wg, wu, tm):
    t, d = h.shape
    fb = D_FF // FF_BLOCKS

    def body(h_ref, gn_ref, wg_ref, wu_ref, g_out, u_out, hn_out):
        hn = _bf(_rms(h_ref[...], gn_ref[...]))
        hn_out[0] = hn
        g_out[...] = _nt(hn, wg_ref[...])
        u_out[...] = _nt(hn, wu_ref[...])

    wspec = pl.BlockSpec((fb, d), lambda f, i: (f, 0))
    ospec = pl.BlockSpec((tm, fb), lambda f, i: (i, f))
    return pl.pallas_call(
        body, name="ffn_a_fwd", grid=(FF_BLOCKS, t // tm),
        in_specs=[pl.BlockSpec((tm, d), lambda f, i: (i, 0)), pl.BlockSpec((1, d), lambda f, i: (0, 0)), wspec, wspec],
        out_specs=[ospec, ospec, pl.BlockSpec((1, tm, d), lambda f, i: (f, i, 0))],
        out_shape=[_sds((t, D_FF)), _sds((t, D_FF)), _sds((FF_BLOCKS, t, d), BF16)], compiler_params=_params(2))(h, norm, wg, wu)


def _ffn_b_fwd(g, u, h, cw, cb, wd, tm, seq, final=None):
    t, d = h.shape
    tiles_per_seq = seq // tm

    def body(g_ref, halo_ref, u_ref, h_ref, cw_ref, cb_ref, wd_ref, *rest):
        pad_ref, act_ref = rest[-2:]
        pad_ref[:HALO, :] = jnp.where(pl.program_id(0) % tiles_per_seq == 0, 0.0, halo_ref[...])
        pad_ref[HALO:, :] = g_ref[...]
        cw = cw_ref[...]
        cb = cb_ref[...]
        for c0 in range(0, D_FF, STRIP_LANES):
            cols = slice(c0, min(c0 + STRIP_LANES, D_FF))
            for r in range(0, tm, STRIP):
                taps = _conv_taps(pad_ref, r, cols, FFN_CONV)
                gc = cb[:, cols] + cw[0:1, cols] * taps[0] + cw[1:2, cols] * taps[1] + cw[2:3, cols] * taps[2]
                act_ref[r:r + STRIP, cols] = _bf(_gelu(gc) * u_ref[r:r + STRIP, cols])
        h_new = h_ref[...] + _nn(act_ref[...], wd_ref[...])
        if final is None:
            rest[0][...] = h_new
        else:
            tgt_ref, fn_ref, dh_out, loss_out, dfn_out = rest[:5]
            first = pl.program_id(0) == 0
            loss, dh_out[...], dfn = _loss_and_grad(h_new, tgt_ref[...], fn_ref[...])
            _accumulate(loss_out, loss, first)
            _accumulate(dfn_out, dfn, first)

    in_specs = [_row(tm, D_FF), _prev_halo(tm, D_FF), _row(tm, D_FF), _row(tm, d), _const((FFN_CONV, D_FF)), _const((1, D_FF)), _const((D_FF, d))]
    scratch = [pltpu.VMEM((HALO + tm, D_FF), F32), pltpu.VMEM((tm, D_FF), BF16)]
    if final is None:
        return pl.pallas_call(body, name="ffn_b_fwd", grid=(t // tm,), in_specs=in_specs, out_specs=_row(tm, d), out_shape=_sds((t, d)),
                              scratch_shapes=scratch, compiler_params=_params())(g, g, u, h, cw, cb, wd)
    return pl.pallas_call(body, name="ffn_b_fwd_loss", grid=(t // tm,), in_specs=in_specs + [_row(tm, d), _const((1, d))],
                          out_specs=[_row(tm, d), _const((1, 1)), _const((1, d))],
                          out_shape=[_sds((t, d)), _sds((1, 1)), _sds((1, d))],
                          scratch_shapes=scratch, compiler_params=_params())(g, g, u, h, cw, cb, wd, *final)


def _ffn_b_bwd(g, u, dout, cw, cb, wd, tm, seq):
    t, d = dout.shape
    fb = D_FF // FF_BLOCKS
    tiles_per_seq = seq // tm

    def body(g_ref, halo_ref, u_ref, dout_ref, cw_ref, cb_ref, wd_ref, dgc_out, du_out, dwd_out, dcw_out, dcb_out,
             pad_ref, dact_ref, act_ref, acc_ref, dwd_acc):
        i = pl.program_id(1)
        first = i == 0
        pad_ref[:HALO, :] = jnp.where(i % tiles_per_seq == 0, 0.0, halo_ref[...])
        pad_ref[HALO:, :] = g_ref[...]
        dout_b = _bf(dout_ref[...])
        dact_ref[...] = _nt(dout_b, wd_ref[...])
        cw = cw_ref[...]
        cb = cb_ref[...]
        fold = lambda a: a[:HALO] + a[HALO:]
        for c0 in range(0, fb, STRIP_LANES):
            cols = slice(c0, min(c0 + STRIP_LANES, fb))
            sums = [jnp.zeros((HALO, cols.stop - c0), F32) for _ in range(1 + FFN_CONV)]
            for r in range(0, tm, STRIP):
                rows = slice(r, r + STRIP)
                taps = _conv_taps(pad_ref, r, cols, FFN_CONV)
                gelu, dgelu = _gelu_and_grad(cb[:, cols] + cw[0:1, cols] * taps[0] + cw[1:2, cols] * taps[1] + cw[2:3, cols] * taps[2])
                u = u_ref[rows, cols]
                dact = dact_ref[rows, cols]
                act_ref[rows, cols] = _bf(gelu * u)
                du_out[rows, cols] = _bf(dact * gelu)
                dgc = dact * u * dgelu
                dgc_out[rows, cols] = dgc
                sums = [sums[0] + fold(dgc)] + [sums[1 + k] + fold(dgc * taps[k]) for k in range(FFN_CONV)]
            for k in range(1 + FFN_CONV):
                acc_ref[k, :, cols] = sums[k]
        _accumulate(dwd_acc, _tn(act_ref[...], dout_b), first)

        @pl.when(i == t // tm - 1)
        def _():
            dwd_out[...] = _bf(dwd_acc[...])

        _accumulate(dcb_out, _colsum(acc_ref[0]), first)
        _accumulate(dcw_out, jnp.concatenate([_colsum(acc_ref[1 + k]) for k in range(FFN_CONV)], axis=0), first)

    blk = pl.BlockSpec((tm, fb), lambda f, i: (i, f))
    halo = pl.BlockSpec((HALO, fb), lambda f, i: (jnp.maximum(i * (tm // HALO) - 1, 0), f))
    wd_blk = pl.BlockSpec((fb, d), lambda f, i: (f, 0), pipeline_mode=pl.Buffered(1))
    return pl.pallas_call(
        body, name="ffn_b_bwd", grid=(FF_BLOCKS, t // tm),
        in_specs=[blk, halo, blk, pl.BlockSpec((tm, d), lambda f, i: (i, 0)), pl.BlockSpec((FFN_CONV, fb), lambda f, i: (0, f)),
                  pl.BlockSpec((1, fb), lambda f, i: (0, f)), wd_blk],
        out_specs=[blk, blk, wd_blk, pl.BlockSpec((FFN_CONV, fb), lambda f, i: (0, f)),
                   pl.BlockSpec((1, fb), lambda f, i: (0, f))],
        out_shape=[_sds((t, D_FF)), _sds((t, D_FF), BF16), _sds((D_FF, d), BF16), _sds((FFN_CONV, D_FF)), _sds((1, D_FF))],
        scratch_shapes=[pltpu.VMEM((HALO + tm, fb), F32), pltpu.VMEM((tm, fb), F32), pltpu.VMEM((tm, fb), BF16),
                        pltpu.VMEM((1 + FFN_CONV, HALO, fb), F32), pltpu.VMEM((fb, d), F32)],
        compiler_params=_params(2))(g, g, u, dout, cw, cb, wd)


def _ffn_a_dgrad(h, norm, dgc, du, dres, cw, wg, wu, tm, seq):
    t, d = h.shape
    tiles_per_seq = seq // tm
    n_tiles = t // tm

    def body(h_ref, gn_ref, dgc_ref, halo_ref, du_ref, dres_ref, cw_ref, wg_ref, wu_ref, dh_out, dg_out, dgn_out, pad_ref):
        i = pl.program_id(0)
        last_in_seq = i % tiles_per_seq == tiles_per_seq - 1
        dg = _bf(_causal_conv_transpose(pad_ref, dgc_ref[...], halo_ref[...], last_in_seq, cw_ref[...], FFN_CONV))
        dg_out[...] = dg
        _, vjp_norm = jax.vjp(_rms, h_ref[...], gn_ref[...])
        dh, dgn = vjp_norm(_nn(dg, wg_ref[...]) + _nn(du_ref[...], wu_ref[...]))
        dh_out[...] = dh + dres_ref[...]
        _accumulate(dgn_out, dgn, i == 0)

    return pl.pallas_call(
        body, name="ffn_a_dgrad", grid=(n_tiles,),
        in_specs=[_row(tm, d), _const((1, d)), _row(tm, D_FF), _next_halo(tm, D_FF, n_tiles), _row(tm, D_FF), _row(tm, d),
                  _const((FFN_CONV, D_FF)), _const((D_FF, d)), _const((D_FF, d))],
        out_specs=[_row(tm, d), _row(tm, D_FF), _const((1, d))], out_shape=[_sds((t, d)), _sds((t, D_FF), BF16), _sds((1, d))],
        scratch_shapes=[pltpu.VMEM((tm + HALO, D_FF), F32)], compiler_params=_params())(h, norm, dgc, dgc, du, dres, cw, wg, wu)


def _ffn_a_wgrad(hn, dg, du, tm):
    _, t, d = hn.shape
    fb = D_FF // FF_BLOCKS

    n_tiles = t // tm

    def body(hn_ref, dg_ref, du_ref, dwg_out, dwu_out, acc_g, acc_u):
        i = pl.program_id(1)
        hn_t = hn_ref[0]
        _accumulate(acc_g, _tn(dg_ref[...], hn_t), i == 0)
        _accumulate(acc_u, _tn(du_ref[...], hn_t), i == 0)

        @pl.when(i == n_tiles - 1)
        def _():
            dwg_out[...] = _bf(acc_g[...])
            dwu_out[...] = _bf(acc_u[...])

    blk = pl.BlockSpec((tm, fb), lambda f, i: (i, f))
    wspec = pl.BlockSpec((fb, d), lambda f, i: (f, 0), pipeline_mode=pl.Buffered(1))
    return pl.pallas_call(body, name="ffn_a_wgrad", grid=(FF_BLOCKS, n_tiles),
                          in_specs=[pl.BlockSpec((1, tm, d), lambda f, i: (0, i, 0)), blk, blk],
                          out_specs=[wspec, wspec], out_shape=[_sds((D_FF, d), BF16), _sds((D_FF, d), BF16)],
                          scratch_shapes=[pltpu.VMEM((fb, d), F32), pltpu.VMEM((fb, d), F32)],
                          compiler_params=_params(2))(hn, dg, du)


def _sgu_mix(vn, ws_ref, bst):
    tril = lax.broadcasted_iota(jnp.int32, (CHUNK, CHUNK), 0) >= lax.broadcasted_iota(jnp.int32, (CHUNK, CHUNK), 1)
    wms = [jnp.where(tril, ws_ref[g], 0.0) for g in range(SGU_GROUPS)]
    chunks = []
    for n in range(vn.shape[0] // CHUNK):
        vc = vn[n * CHUNK:(n + 1) * CHUNK, :]
        chunks.append(jnp.concatenate(
            [_nn(wms[g], vc[:, g * CHUNK:(g + 1) * CHUNK]) + bst[:, g:g + 1] for g in range(SGU_GROUPS)], axis=1))
    return jnp.concatenate(chunks, axis=0)


def _sgu_fwd(h, w, tm):
    t, d = h.shape

    def body(h_ref, cn_ref, win_ref, lg_ref, lb_ref, ws_ref, bst_ref, wout_ref, h_out):
        h_t = h_ref[...]
        z = _gelu(_nn(_rms(h_t, cn_ref[...]), win_ref[...]))
        vn = _layer_norm(z[:, d:], lg_ref[...], lb_ref[...])
        s = _sgu_mix(vn, ws_ref, bst_ref[...])
        h_out[...] = h_t + _nn(z[:, :d] * s, wout_ref[...])

    return pl.pallas_call(
        body, name="sgu_fwd", grid=(t // tm,),
        in_specs=[_row(tm, d), _const((1, d)), _const((d, 2 * d)), _const((1, d)), _const((1, d)), _const((SGU_GROUPS, CHUNK, CHUNK)),
                  _const((CHUNK, LANES)), _const((d, d))],
        out_specs=_row(tm, d), out_shape=_sds((t, d)), compiler_params=_params(),
    )(h, w['c_norm'], w['c_w_in'], w['c_ln_g'], w['c_ln_b'], w['c_w_s'], w['bsT'], w['c_w_out'])


def _sgu_bwd(h, dout, w, tm, sub):
    t, d = h.shape

    def body(h_ref, dout_ref, cn_ref, win_ref, lg_ref, lb_ref, ws_ref, bst_ref, wout_ref,
             dh_out, dcn_out, dwin_out, dlg_out, dlb_out, dws_out, dbst_out, dwout_out, hn_ref, us_ref, dz_ref):
        first = pl.program_id(0) == 0
        tril = lax.broadcasted_iota(jnp.int32, (CHUNK, CHUNK), 0) >= lax.broadcasted_iota(jnp.int32, (CHUNK, CHUNK), 1)
        lane = lax.broadcasted_iota(jnp.int32, (CHUNK, LANES), 1)
        dws = [jnp.zeros((CHUNK, CHUNK), F32) for _ in range(SGU_GROUPS)]
        dbst = jnp.zeros((CHUNK, LANES), F32)
        dlg = dlb = dcn = 0.0
        for r in range(0, tm, sub):
            rows = slice(r, r + sub)
            hn, vjp_norm = jax.vjp(_rms, h_ref[rows, :], cn_ref[...])
            zpre = _nn(hn, win_ref[...])
            u, vjp_u = jax.vjp(_gelu, zpre[:, :d])
            vn, vjp_v = jax.vjp(lambda zp, lg, lb: _layer_norm(_gelu(zp), lg, lb), zpre[:, d:], lg_ref[...], lb_ref[...])
            s = _sgu_mix(vn, ws_ref, bst_ref[...])
            dout_t = dout_ref[rows, :]
            dus = _nt(dout_t, wout_ref[...])
            hn_ref[rows, :] = _bf(hn)
            us_ref[rows, :] = _bf(u * s)
            ds = dus * u
            dvn_chunks = []
            for n in range(sub // CHUNK):
                cols = []
                for g in range(SGU_GROUPS):
                    ds_ng = ds[n * CHUNK:(n + 1) * CHUNK, g * CHUNK:(g + 1) * CHUNK]
                    vc_ng = vn[n * CHUNK:(n + 1) * CHUNK, g * CHUNK:(g + 1) * CHUNK]
                    cols.append(_tn(jnp.where(tril, ws_ref[g], 0.0), ds_ng))
                    dws[g] = dws[g] + _nt(ds_ng, vc_ng)
                    dbst = dbst + jnp.where(lane == g, jnp.sum(ds_ng, axis=1, keepdims=True), 0.0)
                dvn_chunks.append(jnp.concatenate(cols, axis=1))
            dvn = jnp.concatenate(dvn_chunks, axis=0)
            (dzu,) = vjp_u(dus * s)
            dzv, dlg_r, dlb_r = vjp_v(dvn)
            dzpre = jnp.concatenate([dzu, dzv], axis=1)
            dz_ref[rows, :] = _bf(dzpre)
            dh, dcn_r = vjp_norm(_nt(dzpre, win_ref[...]))
            dh_out[rows, :] = dh + dout_t
            dlg, dlb, dcn = dlg + dlg_r, dlb + dlb_r, dcn + dcn_r
        _accumulate(dwout_out, _tn(us_ref[...], dout_ref[...]), first)
        _accumulate(dwin_out, _tn(hn_ref[...], dz_ref[...]), first)
        for g in range(SGU_GROUPS):
            val = jnp.where(tril, dws[g], 0.0)

            @pl.when(first)
            def _():
                dws_out[g] = val

            @pl.when(jnp.logical_not(first))
            def _():
                dws_out[g] += val
        _accumulate(dbst_out, dbst, first)
        _accumulate(dlg_out, dlg, first)
        _accumulate(dlb_out, dlb, first)
        _accumulate(dcn_out, dcn, first)

    return pl.pallas_call(
        body, name="sgu_bwd", grid=(t // tm,),
        in_specs=[_row(tm, d), _row(tm, d), _const((1, d)), _const((d, 2 * d)), _const((1, d)), _const((1, d)),
                  _const((SGU_GROUPS, CHUNK, CHUNK)), _const((CHUNK, LANES)), _const((d, d))],
        out_specs=[_row(tm, d), _const((1, d)), _const((d, 2 * d)), _const((1, d)), _const((1, d)), _const((SGU_GROUPS, CHUNK, CHUNK)),
                   _const((CHUNK, LANES)), _const((d, d))],
        out_shape=[_sds((t, d)), _sds((1, d)), _sds((d, 2 * d)), _sds((1, d)), _sds((1, d)), _sds((SGU_GROUPS, CHUNK, CHUNK)),
                   _sds((CHUNK, LANES)), _sds((d, d))],
        scratch_shapes=[pltpu.VMEM((tm, d), BF16), pltpu.VMEM((tm, d), BF16), pltpu.VMEM((tm, 2 * d), BF16)],
        compiler_params=_params(),
    )(h, dout, w['c_norm'], w['c_w_in'], w['c_ln_g'], w['c_ln_b'], w['c_w_s'], w['bsT'], w['c_w_out'])


def _loss_and_grad(h, tgt, g):
    def loss_fn(h_, g_):
        err = _rms(h_, g_) - tgt
        return 0.5 * jnp.sum(jnp.mean(err * err, axis=-1, keepdims=True), axis=0, keepdims=True)

    loss, vjp_loss = jax.vjp(loss_fn, h, g)
    return (loss,) + vjp_loss(jnp.ones((1, 1), F32))


def _tile(t, seq, want):
    tm = min(want, seq)
    assert seq % tm == 0 and t % tm == 0 and tm % CHUNK == 0
    return tm


def _local_step(x, posb, target, w, seq, late_weights, on_grads):
    t, d = x.shape
    b = t // seq
    hp = HEADS * HEAD_PAD
    tm_wide, tm_big, tm_mid = _tile(t, seq, 1024), _tile(t, seq, 512), _tile(t, seq, 256)
    tq = _tile(t, seq, 512)

    q, k, v, xl, gate = _ab_in_fwd(x, posb, w, tm_big)
    o, probs = _attn_fwd(q.reshape(b, seq, hp), k.reshape(b, seq, hp), v.reshape(b, seq, hp), tq)
    o = o.reshape(t, hp)
    y, hs = _lru_fwd(xl, gate, w, tm_big, seq)
    w = {**w, **late_weights('out0', y)}
    h1 = _ab_out_fwd(x, o, y, w, tm_wide)
    hcur = h1
    saved = []
    for l in range(2):
        if l == 1:
            w = {**w, **late_weights('mix1', hcur)}
            saved_h2 = hcur
            hcur = _sgu_fwd(hcur, w, tm_mid)
        wl = late_weights('ffn%d' % l, hcur)
        g, u, hn = _ffn_a_fwd(hcur, w['ffn_norm'][l], wl['Wg'], wl['Wu'], tm_wide)
        saved.append((hcur, g, u, wl, hn))
        ffn_b = (g, u, hcur, w['ffn_conv_w'][l], w['ffn_conv_b'][l], wl['Wd'], tm_big, seq)
        if l == 0:
            hcur = _ffn_b_fwd(*ffn_b)
    dh, loss, d_final = _ffn_b_fwd(*ffn_b, final=(target, w['final_norm']))

    ffn = {}
    conv_b = list(w['ffn_conv_b'])
    for l in (1, 0):
        hin, g, u, wl, hn = saved[l]
        dgc, du, d_wd, d_cw, d_cb = _ffn_b_bwd(g, u, dh, w['ffn_conv_w'][l], conv_b[l], wl['Wd'], tm_big, seq)
        dh, dg, d_norm = _ffn_a_dgrad(hin, w['ffn_norm'][l], dgc, du, dh, w['ffn_conv_w'][l], wl['Wg'], wl['Wu'], tm_mid, seq)
        d_wg, d_wu = _ffn_a_wgrad(hn, dg, du, tm_wide)
        ffn[l] = dict(ffn_norm=d_norm, ffn_conv_w=d_cw, ffn_conv_b=d_cb, Wg=d_wg, Wu=d_wu, Wd=d_wd)
        if l == 1:
            dh, d_cn, d_cwin, d_lg, d_lb, d_ws, d_bst, d_cwout = _sgu_bwd(saved_h2, dh, w, tm_big, tm_mid)
            zero = on_grads('late1', dict(final_norm=d_final, c_norm=d_cn, c_ln_g=d_lg, c_ln_b=d_lb, c_w_s=d_ws, bsT=d_bst, c_w_in=d_cwin,
                                          c_w_out=d_cwout, Wg=[d_wg], Wu=[d_wu], Wd=[d_wd]))
            conv_b[0] = conv_b[0] + zero
    late0 = {name: [ffn[0][name], ffn[1][name]] for name in ('ffn_norm', 'ffn_conv_w', 'ffn_conv_b')}
    zero = on_grads('late0', dict(late0, Wg=[ffn[0]['Wg']], Wu=[ffn[0]['Wu']], Wd=[ffn[0]['Wd']]))
    w = {**w, 'Wo_b': w['Wo_b'] + zero.astype(w['Wo_b'].dtype)}
    do, dy, d_woa, d_wob = _ab_out_bwd(o, y, dh, w, tm_wide)
    dxl, dgate, d_cw, d_cb, d_wa, d_ba, d_wx, d_bx, d_lam = _lru_bwd(xl, gate, hs, dy, w, tm_big, seq)
    zero = on_grads('mid', dict(Wo_a=d_woa, Wo_b=d_wob, ab_conv_w=d_cw, ab_conv_b=d_cb, Wa=d_wa, ab_b_rg_a=d_ba, Wx=d_wx,
                                ab_b_rg_x=d_bx, ab_lambda=d_lam))
    w = {**w, 'ab_norm': w['ab_norm'] + zero}
    dq, dk, dv = _attn_bwd(q.reshape(b, seq, hp), k.reshape(b, seq, hp), v.reshape(b, seq, hp), probs, do.reshape(b, seq, hp), tq)
    dx, d_gn, d_win, d_qn, d_wq, d_kvn, d_wk, d_wv = _ab_in_bwd(
        x, posb, w, dq.reshape(t, hp), dk.reshape(t, hp), dv.reshape(t, hp), dxl, dgate, dh, tm_big)
    return loss, dx, dict(ab_norm=d_gn, W_in=d_win, ab_q_norm=d_qn, Wq=d_wq, ab_kv_norm=d_kvn, Wk=d_wk, Wv=d_wv)


def _block_diag(wg):
    g, n, _ = wg.shape
    return jnp.einsum('gij,gh->gihj', wg, jnp.eye(g, dtype=wg.dtype)).reshape(g * n, g * n)


def _prepare_out(w_out):
    d = w_out.shape[2]
    mla = HEADS * QK_NOPE
    return {'Wo_a': jnp.pad(w_out[0, :mla].reshape(HEADS, QK_NOPE, d), ((0, 0), (0, HEAD_PAD - QK_NOPE), (0, 0))).reshape(HEADS * HEAD_PAD, d),
            'Wo_b': w_out[0, mla:]}


def _prepare(full):
    d = full['ab_w_in'].shape[1]
    w_in = full['ab_w_in'][0]
    zeros = lambda n: jnp.zeros((d, n), w_in.dtype)
    wq = full['ab_w_q_b'][0].reshape(Q_LORA, HEADS, QK_NOPE + QK_ROPE)
    wkv = full['ab_w_kv_b'][0].reshape(KV_LORA, HEADS, 2 * QK_NOPE)
    pad_head = lambda a: jnp.pad(a, ((0, 0), (0, 0), (0, HEAD_PAD - a.shape[2]))).reshape(a.shape[0], HEADS * HEAD_PAD)
    w = {
        'W_in': jnp.concatenate([w_in[:, :Z_KPE], zeros(QK_NOPE), w_in[:, Z_KPE:Z_KPE + QK_ROPE],
                                 zeros(HEAD_PAD - QK_NOPE - QK_ROPE), w_in[:, Z_KPE + QK_ROPE:]], axis=1),
        'Wq': pad_head(wq), 'Wk': pad_head(wkv[:, :, :QK_NOPE]), 'Wv': pad_head(wkv[:, :, QK_NOPE:]),
        'Wa': _bf(_block_diag(full['ab_w_rg_a'][0])), 'Wx': _bf(_block_diag(full['ab_w_rg_x'][0])),
        'c_w_s': full['c_w_s'][0],
        'bsT': jnp.pad(full['c_b_s'][0].T, ((0, 0), (0, LANES - SGU_GROUPS))),
        'ffn_norm': [full['ffn_norm'][l:l + 1] for l in range(2)], 'ffn_conv_w': [full['ffn_conv_w'][l] for l in range(2)],
        'ffn_conv_b': [full['ffn_conv_b'][l:l + 1] for l in range(2)],
        'ab_conv_w': full['ab_conv_w'][0], 'final_norm': full['final_norm'][None, :],
    }
    for name in ('ab_norm', 'ab_q_norm', 'ab_kv_norm', 'ab_conv_b', 'ab_b_rg_a', 'ab_b_rg_x', 'ab_lambda', 'c_norm', 'c_ln_g', 'c_ln_b'):
        w[name] = full[name]
    return w


def _unprepare(g):
    unpad_head = lambda a, n: a.reshape(a.shape[0], HEADS, HEAD_PAD)[:, :, :n]
    diag = lambda a: jnp.einsum('gigj->gij', a.reshape(HEADS, LRU_W // HEADS, HEADS, LRU_W // HEADS))
    rules = {
        'ab_w_in': (('W_in',), lambda a: jnp.concatenate([a[:, :Z_KPE], a[:, Z_KPE + QK_NOPE:Z_KPE + QK_NOPE + QK_ROPE], a[:, Z_LRU:]], axis=1)[None]),
        'ab_w_q_b': (('Wq',), lambda a: unpad_head(a, QK_NOPE + QK_ROPE).reshape(1, Q_LORA, -1)),
        'ab_w_kv_b': (('Wk', 'Wv'), lambda a, b: jnp.concatenate([unpad_head(a, QK_NOPE), unpad_head(b, QK_NOPE)], axis=2).reshape(1, KV_LORA, -1)),
        'ab_w_out': (('Wo_a', 'Wo_b'), lambda a, b: jnp.concatenate(
            [a.reshape(HEADS, HEAD_PAD, -1)[:, :QK_NOPE].reshape(HEADS * QK_NOPE, -1), b], axis=0)[None]),
        'ab_w_rg_a': (('Wa',), lambda a: diag(a)[None]), 'ab_w_rg_x': (('Wx',), lambda a: diag(a)[None]),
        'c_w_in': (('c_w_in',), lambda a: a[None]), 'c_w_out': (('c_w_out',), lambda a: a[None]), 'c_w_s': (('c_w_s',), lambda a: a[None]),
        'c_b_s': (('bsT',), lambda a: a[:, :SGU_GROUPS].T[None]),
        'ffn_w_gate': (('Wg',), jnp.stack), 'ffn_w_up': (('Wu',), jnp.stack), 'ffn_w_down': (('Wd',), jnp.stack),
        'ffn_norm': (('ffn_norm',), lambda a: jnp.concatenate(a, axis=0)), 'ffn_conv_w': (('ffn_conv_w',), jnp.stack),
        'ffn_conv_b': (('ffn_conv_b',), lambda a: jnp.concatenate(a, axis=0)),
        'ab_conv_w': (('ab_conv_w',), lambda a: a[None]), 'final_norm': (('final_norm',), lambda a: a[0]),
    }
    for name in ('ab_norm', 'ab_q_norm', 'ab_kv_norm', 'ab_conv_b', 'ab_b_rg_a', 'ab_b_rg_x', 'ab_lambda', 'c_norm', 'c_ln_g', 'c_ln_b'):
        rules[name] = ((name,), lambda a: a)
    return {name: fn(*[g[k] for k in keys]) for name, (keys, fn) in rules.items() if all(k in g for k in keys)}


SLAB_ROWS = 16


def _round_up(n, m):
    return -(-n // m) * m


def _to_chunks(full, axis):
    s = full.shape
    return jnp.moveaxis(full.reshape(s[:axis] + (N_DEV, s[axis] // N_DEV) + s[axis + 1:]), axis, 0)


def _from_chunks(chunks, axis):
    local = chunks.shape[1:]
    return jnp.moveaxis(chunks, 0, axis).reshape(local[:axis] + (N_DEV * local[axis],) + local[axis + 1:])


def _merge_columns(landed, name):
    _, _, r, n = landed.shape
    tr = r // 4

    def body(l_ref, o_ref):
        o_ref[0] = jnp.concatenate([l_ref[dev, 0] for dev in range(N_DEV)], axis=1)

    return pl.pallas_call(body, name="merge_" + name, grid=(r // tr,),
                          in_specs=[pl.BlockSpec((N_DEV, 1, tr, n), lambda i: (0, 0, i, 0))],
                          out_specs=pl.BlockSpec((1, tr, N_DEV * n), lambda i: (0, i, 0)),
                          out_shape=jax.ShapeDtypeStruct((1, r, N_DEV * n), landed.dtype), compiler_params=_params())(landed)


def _split_chunks(whole, axis, name):
    _, rows, cols = whole.shape
    if axis == 1:
        r = rows // N_DEV

        def body(x_ref, o_ref):
            o_ref[0] = _bf(x_ref[...])

        grid, out_shape = (N_DEV,), (N_DEV, 1, r, cols)
        spec, out_spec = pl.BlockSpec((1, r, cols), lambda dev: (0, dev, 0)), pl.BlockSpec((1, 1, r, cols), lambda dev: (dev, 0, 0, 0))
    else:
        n, tr = cols // N_DEV, rows // 4

        def body(x_ref, o_ref):
            x = x_ref[0]
            for dev in range(N_DEV):
                o_ref[dev, 0] = _bf(x[:, dev * n:(dev + 1) * n])

        grid, out_shape = (rows // tr,), (N_DEV, 1, rows, n)
        spec, out_spec = pl.BlockSpec((1, tr, cols), lambda i: (0, i, 0)), pl.BlockSpec((N_DEV, 1, tr, n), lambda i: (0, 0, i, 0))
    return pl.pallas_call(body, name="split_" + name, grid=grid, in_specs=[spec], out_specs=out_spec,
                          out_shape=jax.ShapeDtypeStruct(out_shape, BF16), compiler_params=_params())(whole)


def _slab_rows(n):
    return _round_up(-(-n // LANES), SLAB_ROWS)


def _to_slab(a, lead):
    a = a.reshape(lead + (-1,))
    rows = _slab_rows(a.shape[-1])
    a = jnp.pad(a, [(0, 0)] * len(lead) + [(0, rows * LANES - a.shape[-1])])
    return a.reshape(lead + (rows, LANES))


def _pack_slabs(parts, lead):
    return jnp.concatenate([_to_slab(p, lead) for p in parts], axis=len(lead))


def _unpack_slabs(packed, shapes):
    lead = packed.shape[:-2]
    out, row = [], 0
    for shape in shapes:
        size = math.prod(shape)
        rows = _slab_rows(size)
        piece = lax.slice_in_dim(packed, row, row + rows, axis=len(lead))
        out.append(piece.reshape(lead + (rows * LANES,))[..., :size].reshape(lead + tuple(shape)))
        row += rows
    return out


HBM = pl.BlockSpec(memory_space=pl.ANY)


def _other_chips(x, y):
    return [(1 - x, y), (x, 1 - y), (1 - x, 1 - y)]


def _all_gather(blocks):
    n = len(blocks)

    def body(*refs):
        x_refs, out_refs, token = refs[:n], refs[n:2 * n], refs[2 * n]
        send_sems, recv_sems, local_sems = refs[2 * n + 1:]
        token[...] = jnp.zeros_like(token)
        x, y, c = lax.axis_index("x"), lax.axis_index("y"), lax.axis_index("c")
        me, sibling = (x, y, c), (x, y, 1 - c)
        chips = _other_chips(x, y)

        def slab(a, px, py, pc):
            return out_refs[a].at[4 * px + 2 * py + pc]

        def copy(a, k, blk, to, src=None):
            return pltpu.make_async_remote_copy(src_ref=slab(a, *blk) if src is None else src, dst_ref=slab(a, *blk),
                                                send_sem=send_sems.at[7 * a + k], recv_sem=recv_sems.at[7 * a + k],
                                                device_id=to, device_id_type=MESH)

        mine = [pltpu.make_async_copy(x_refs[a], slab(a, *me), local_sems.at[a]) for a in range(n)]
        started = []
        for a in range(n):
            mine[a].start()
            started.append(copy(a, 0, me, sibling, src=x_refs[a]))
            started += [copy(a, 1 + j, me, (*chip, c), src=x_refs[a]) for j, chip in enumerate(chips)]
        for cp in started:
            cp.start()
        for j, chip in enumerate(chips):
            for a in range(n):
                copy(a, 1 + j, (*chip, c), me).wait_recv()
                passed = copy(a, 4 + j, (*chip, c), sibling)
                passed.start()
                started.append(passed)
        for a in range(n):
            copy(a, 0, sibling, me).wait_recv()
        for j, chip in enumerate(chips):
            for a in range(n):
                copy(a, 4 + j, (*chip, 1 - c), me).wait_recv()
        for cp in started:
            cp.wait_send()
        for a in range(n):
            mine[a].wait()

    out = pl.pallas_call(
        body, name="all_gather_weights",
        out_shape=[jax.ShapeDtypeStruct((N_DEV,) + b.shape, b.dtype) for b in blocks] + [jax.ShapeDtypeStruct((8, LANES), F32)],
        in_specs=[HBM] * n, out_specs=[HBM] * n + [pl.BlockSpec(memory_space=pltpu.VMEM)],
        scratch_shapes=[pltpu.SemaphoreType.DMA((7 * n,)), pltpu.SemaphoreType.DMA((7 * n,)), pltpu.SemaphoreType.DMA((n,))],
    )(*blocks)
    return list(out[:n]), out[n][0, 0]


FLIPS = [(0, 0, 1), (1, 0, 0), (1, 0, 1), (0, 1, 0), (0, 1, 1), (1, 1, 0), (1, 1, 1)]


def _peers(x, y, c):
    flip = lambda v, f: 1 - v if f else v
    return [(flip(x, fx), flip(y, fy), flip(c, fc)) for fx, fy, fc in FLIPS]


def _direct_copies(src_refs, land_refs, send_sems, recv_sems, scatter):
    x, y, c = lax.axis_index("x"), lax.axis_index("y"), lax.axis_index("c")
    me = 4 * x + 2 * y + c
    starts, waits = [], []
    for a in range(len(src_refs)):
        for k, (px, py, pc) in enumerate(_peers(x, y, c)):
            peer = 4 * px + 2 * py + pc
            sems = dict(send_sem=send_sems.at[7 * a + k], recv_sem=recv_sems.at[7 * a + k], device_id=(px, py, pc), device_id_type=MESH)
            src = src_refs[a].at[peer] if scatter else src_refs[a]
            starts.append(pltpu.make_async_remote_copy(src_ref=src, dst_ref=land_refs[a].at[me], **sems))
            waits.append(pltpu.make_async_remote_copy(src_ref=src, dst_ref=land_refs[a].at[peer], **sems))
    n = len(src_refs)
    keeps = [] if scatter else [pltpu.make_async_copy(src_refs[a], land_refs[a].at[me], send_sems.at[7 * n + a]) for a in range(n)]
    return starts, waits, keeps


def _landing(src, scatter):
    block = src.shape[1:] if scatter else src.shape
    return jax.ShapeDtypeStruct((N_DEV,) + block, src.dtype)


HBM_SPACE = pl.BlockSpec(memory_space=pltpu.HBM)
SEMAPHORES = pl.BlockSpec(memory_space=pltpu.SEMAPHORE)
SPLIT_EFFECT = pltpu.SideEffectType.DATAFLOW_SIDE_EFFECTING


def _start_exchange(name, srcs, scatter):
    n = len(srcs)
    lands = [lax.empty(s.shape, s.dtype) for s in (_landing(s, scatter) for s in srcs)]

    def body(*refs):
        starts, _, keeps = _direct_copies(refs[:n], refs[n:2 * n], refs[2 * n], refs[2 * n + 1], scatter)
        for cp in starts + keeps:
            cp.start()
        refs[-1][...] = jnp.zeros_like(refs[-1])

    held = [pltpu.with_memory_space_constraint(a, pltpu.HBM) for a in list(srcs) + lands]
    out = pl.pallas_call(
        body, name=name + "_start",
        out_shape=(pltpu.SemaphoreType.DMA(((7 if scatter else 8) * n,)), pltpu.SemaphoreType.DMA((7 * n,)),
                   *[pltpu.HBM(a.shape, a.dtype) for a in held],
                   jax.ShapeDtypeStruct((8, LANES), F32)),
        in_specs=[HBM_SPACE] * (2 * n), out_specs=(SEMAPHORES, SEMAPHORES, *[HBM_SPACE] * (2 * n), pl.BlockSpec(memory_space=pltpu.VMEM)),
        input_output_aliases={i: 2 + i for i in range(2 * n)},
        compiler_params=pltpu.CompilerParams(has_side_effects=SPLIT_EFFECT),
    )(*held)
    return out[0], out[1], list(out[2:2 + n]), list(out[2 + n:2 + 2 * n]), out[-1][0, 0], out[-1]


def _wait_exchange(name, started, after, scatter):
    send_sems, recv_sems, srcs, lands = started[:4]
    n = len(srcs)

    def body(*refs):
        _, waits, keeps = _direct_copies(refs[:n], refs[n:2 * n], refs[2 * n], refs[2 * n + 1], scatter)
        for cp in waits:
            cp.wait_send()
        for cp in waits:
            cp.wait_recv()
        for cp in keeps:
            cp.wait()

    out = pl.pallas_call(
        body, name=name + "_wait", out_shape=tuple(pltpu.HBM(a.shape, a.dtype) for a in srcs + lands),
        in_specs=[HBM_SPACE] * (2 * n) + [SEMAPHORES, SEMAPHORES, HBM], out_specs=tuple([HBM_SPACE] * (2 * n)),
        input_output_aliases={i: i for i in range(2 * n)},
        compiler_params=pltpu.CompilerParams(has_side_effects=SPLIT_EFFECT),
    )(*srcs, *lands, send_sems, recv_sems, after)
    return list(out[:n]), list(out[n:])


def _row_tile(rows):
    return rows // 2 if (rows // 2) % SLAB_ROWS == 0 else rows


def _sum_in_device_order(me_ref, l_ref, own_ref):
    mine = own_ref[0].astype(F32)
    g = jnp.where(me_ref[0] == 0, mine, l_ref[0].astype(F32))
    for dev in range(1, N_DEV):
        g = g + jnp.where(me_ref[0] == dev, mine, l_ref[dev].astype(F32))
    return g


def _adamw(g, w, m, v):
    m_new = ADAM_B1 * m + (1.0 - ADAM_B1) * g
    v_new = ADAM_B2 * v + (1.0 - ADAM_B2) * (g * g)
    m_hat = m_new * (1.0 / (1.0 - ADAM_B1 ** ADAM_STEP))
    v_hat = v_new * (1.0 / (1.0 - ADAM_B2 ** ADAM_STEP))
    return -ADAM_LR * (m_hat / (jnp.sqrt(v_hat) + ADAM_EPS) + ADAM_WD * w), m_new, v_new


def _sum_chunks(me, landed, own, name):
    _, _, r, n = landed.shape

    def body(me_ref, l_ref, own_ref, g_out):
        g_out[...] = _sum_in_device_order(me_ref, l_ref, own_ref)[0]

    return pl.pallas_call(
        body, name="sum_" + name,
        grid_spec=pltpu.PrefetchScalarGridSpec(
            num_scalar_prefetch=1, grid=(1,),
            in_specs=[pl.BlockSpec((N_DEV, 1, r, n), lambda i, me_ref: (0, 0, 0, 0)),
                      pl.BlockSpec((1, 1, r, n), lambda i, me_ref: (me_ref[0], 0, 0, 0))],
            out_specs=pl.BlockSpec((r, n), lambda i, me_ref: (0, 0))),
        out_shape=_sds((r, n)), compiler_params=_params())(me, landed, own)


def _adamw_small(gs, ws, ms, vs):
    n = len(gs)

    def body(*refs):
        ins, outs = refs[:4 * n], refs[4 * n:]
        for i in range(n):
            outs[i][...], outs[n + i][...], outs[2 * n + i][...] = _adamw(*[ins[k * n + i][...] for k in range(4)])

    out = pl.pallas_call(body, name="adamw_small", out_shape=[_sds(w.shape) for w in ws] * 3)(*gs, *ws, *ms, *vs)
    return out[:n], out[n:2 * n], out[2 * n:]


def _sum_and_adamw(me, landed, own, wts, m, v, name, layer=None, into=None):
    layers, r, n = wts.shape
    first = 0 if layer is None else layer
    count = layers if layer is None else 1
    tr = _row_tile(r)
    blk = pl.BlockSpec((1, tr, n), lambda li, ri, me_ref: (first + li, ri, 0))
    held = [] if into is None else list(into)

    def body(me_ref, l_ref, own_ref, w_ref, m_ref, v_ref, *rest):
        g_out, d_out, m_out, v_out = rest[len(held):]
        g = _sum_in_device_order(me_ref, l_ref, own_ref)
        g_out[...] = g
        d_out[...], m_out[...], v_out[...] = _adamw(g, w_ref[...], m_ref[...], v_ref[...])

    return pl.pallas_call(
        body, name="adamw_" + name,
        grid_spec=pltpu.PrefetchScalarGridSpec(
            num_scalar_prefetch=1, grid=(count, r // tr),
            in_specs=[pl.BlockSpec((N_DEV, 1, tr, n), lambda li, ri, me_ref: (0, li, ri, 0)),
                      pl.BlockSpec((1, 1, tr, n), lambda li, ri, me_ref: (me_ref[0], li, ri, 0)), blk, blk, blk] + [HBM] * len(held),
            out_specs=[blk] * 4),
        out_shape=[_sds((layers, r, n))] * 4, input_output_aliases={6 + i: i for i in range(len(held))},
        compiler_params=_params(2))(me, landed, own, wts, m, v, *held)


EARLY = ['ab_w_in']
LATE_STAGES = {
    'out0': [('ab_w_out', None, 'ab_w_out')],
    'ffn0': [('ffn_w_gate', 0, 'Wg'), ('ffn_w_up', 0, 'Wu'), ('ffn_w_down', 0, 'Wd')],
    'mix1': [('c_w_in', None, 'c_w_in'), ('c_w_out', None, 'c_w_out')],
    'ffn1': [('ffn_w_gate', 1, 'Wg'), ('ffn_w_up', 1, 'Wu'), ('ffn_w_down', 1, 'Wd')],
}
TRANSPOSED = ('ffn_w_gate', 'ffn_w_up')


def _stored(name, a):
    return jnp.swapaxes(a, 1, 2) if name in TRANSPOSED else a


def _stored_axis(name):
    return 1 if name in TRANSPOSED else SHARD_AXIS[name]


GRAD_STAGES = {
    'late1': ([('c_w_in', None), ('c_w_out', None), ('ffn_w_gate', 1), ('ffn_w_up', 1), ('ffn_w_down', 1)],
              ['c_norm', 'c_ln_g', 'c_ln_b', 'c_w_s', 'c_b_s', 'final_norm']),
    'late0': ([('ffn_w_gate', 0), ('ffn_w_up', 0), ('ffn_w_down', 0)], ['ffn_norm', 'ffn_conv_w', 'ffn_conv_b']),
    'mid': ([('ab_w_out', None)], ['ab_conv_w', 'ab_conv_b', 'ab_w_rg_a', 'ab_b_rg_a', 'ab_w_rg_x', 'ab_b_rg_x', 'ab_lambda']),
    'last': ([('ab_w_in', None)], ['ab_norm', 'ab_q_norm', 'ab_w_q_b', 'ab_kv_norm', 'ab_w_kv_b']),
}


def _gather_early(local):
    small = [_bf(local[n]) if n in MATRICES else lax.bitcast_convert_type(local[n], BF16) for n in SMALL_SHARDED]
    gathered, zero = _all_gather([_bf(local[n]) for n in EARLY] + [_pack_slabs(small, ())])
    full = {n: local[n] for n in REPLICATED}
    for n, g in zip(EARLY, gathered):
        full[n] = _from_chunks(g, SHARD_AXIS[n])
    for n, p in zip(SMALL_SHARDED, _unpack_slabs(gathered[-1], [s.shape for s in small])):
        full[n] = _from_chunks(p if n in MATRICES else lax.bitcast_convert_type(p, F32), SHARD_AXIS[n])
    return full, zero


def kernel(x, positions, ab_norm, ab_w_in, ab_q_norm, ab_w_q_b, ab_kv_norm, ab_w_kv_b, ab_conv_w, ab_conv_b, ab_w_rg_a, ab_b_rg_a, ab_w_rg_x, ab_b_rg_x, ab_lambda, ab_w_out, c_norm, c_w_in, c_ln_g, c_ln_b, c_w_s, c_b_s, c_w_out, ffn_norm, ffn_w_gate, ffn_w_up, ffn_conv_w, ffn_conv_b, ffn_w_down, final_norm, loss_target, m_ab_norm, m_ab_w_in, m_ab_q_norm, m_ab_w_q_b, m_ab_kv_norm, m_ab_w_kv_b, m_ab_conv_w, m_ab_conv_b, m_ab_w_rg_a, m_ab_b_rg_a, m_ab_w_rg_x, m_ab_b_rg_x, m_ab_lambda, m_ab_w_out, m_c_norm, m_c_w_in, m_c_ln_g, m_c_ln_b, m_c_w_s, m_c_b_s, m_c_w_out, m_ffn_norm, m_ffn_w_gate, m_ffn_w_up, m_ffn_conv_w, m_ffn_conv_b, m_ffn_w_down, m_final_norm, v_ab_norm, v_ab_w_in, v_ab_q_norm, v_ab_w_q_b, v_ab_kv_norm, v_ab_w_kv_b, v_ab_conv_w, v_ab_conv_b, v_ab_w_rg_a, v_ab_b_rg_a, v_ab_w_rg_x, v_ab_b_rg_x, v_ab_lambda, v_ab_w_out, v_c_norm, v_c_w_in, v_c_ln_g, v_c_ln_b, v_c_w_s, v_c_b_s, v_c_w_out, v_ffn_norm, v_ffn_w_gate, v_ffn_w_up, v_ffn_conv_w, v_ffn_conv_b, v_ffn_w_down, v_final_norm):
    given = dict(locals())
    local = {n: given[n] for n in WEIGHTS}
    b, seq, d = x.shape
    t = b * seq

    me = (4 * lax.axis_index("x") + 2 * lax.axis_index("y") + lax.axis_index("c")).astype(jnp.int32)
    me1 = me.reshape(1)

    full, zero = _gather_early(local)
    gathers = {}
    for stage, members in LATE_STAGES.items():
        srcs = [_bf(_stored(n, local[n] if layer is None else local[n][layer:layer + 1]) + zero) for n, layer, _ in members]
        gathers[stage] = _start_exchange('gather_' + stage, srcs, scatter=False)
        zero = gathers[stage][4]
    w = _prepare(full)
    w['ab_norm'] = w['ab_norm'] + zero

    def late_weights(stage, after):
        _, lands = _wait_exchange('gather_' + stage, gathers[stage], after, scatter=False)
        whole = [l.reshape(1, -1, l.shape[-1]) if _stored_axis(n) == 1 else _merge_columns(l, n)
                 for (n, _, _), l in zip(LATE_STAGES[stage], lands)]
        if stage == 'out0':
            return _prepare_out(whole[0])
        return {key: a[0] for (_, _, key), a in zip(LATE_STAGES[stage], whole)}

    scatters = {}

    def start_scatter(stage, g):
        whole = _unprepare(g)
        big, small = GRAD_STAGES[stage]
        slab = [_to_chunks(whole[n], SHARD_AXIS[n]) if n in SHARD_AXIS else jnp.broadcast_to(whole[n][None], (N_DEV,) + whole[n].shape)
                for n in small]
        own = [whole[n].reshape(N_DEV, 1, whole[n].shape[1] // N_DEV, whole[n].shape[2])
               if whole[n].dtype == BF16 and _stored_axis(n) == 1 else
               _split_chunks(whole[n], _stored_axis(n), n + ('' if layer is None else str(layer))) for n, layer in big]
        own.append(_bf(_pack_slabs(slab, (N_DEV,)))[:, None])
        scatters[stage] = _start_exchange('scatter_' + stage, own, scatter=True)
        return scatters[stage][4]

    posb = jnp.broadcast_to(positions.astype(F32).reshape(t, 1), (t, LANES))
    loss, dx, grads = _local_step(x.reshape(t, d), posb, loss_target.reshape(t, d), w, seq, late_weights, start_scatter)
    start_scatter('last', grads)
    after = scatters['last'][5]

    updated, small_grads = {}, {}
    for stage, (big, small) in GRAD_STAGES.items():
        owns, landed = _wait_exchange('scatter_' + stage, scatters[stage], after, scatter=True)
        for (n, layer), own, land in zip(big, owns, landed):
            updated[n] = _sum_and_adamw(me1, land, own, _stored(n, given[n]), _stored(n, given['m_' + n]), _stored(n, given['v_' + n]),
                                        n + ('' if layer is None else str(layer)), layer, updated.get(n))
        summed = _sum_chunks(me1, landed[-1], owns[-1], stage)
        small_grads.update(zip(small, _unpack_slabs(summed, [local[n].shape for n in small])))
        after = sum([updated[n][1][:1, :1, :1] for n, _ in big], summed[:1, :1].reshape(1, 1, 1))
    names = list(small_grads)
    news = _adamw_small([small_grads[n] for n in names], *[[given[p + n] for n in names] for p in ('', 'm_', 'v_')])
    for i, n in enumerate(names):
        updated[n] = [small_grads[n], news[0][i], news[1][i], news[2][i]]
    total = lax.psum(loss[0, 0], ("x", "y", "c"))
    return (total, dx.reshape(b, seq, d), *[_stored(n, updated[n][kind]) for kind in range(4) for n in WEIGHTS])
```
